```python
import jax, jax.numpy as jnp
from jax import lax
import numpy as np

D_MODEL = 2048
BATCH = 8
SEQ = 8192
DEPTH = 1

D_MIX = D_MODEL
HEAD_DIM = 128
D_A = D_MIX // 2
D_B = D_MIX - D_A
N_HEADS_A = D_A // HEAD_DIM
N_GROUPS_B = D_B // HEAD_DIM
CHUNK = 128
CONV_W = 3
D_FF = 5632
D_IN = 2 * D_A + 3 * D_B
LN_EPS = 1e-5
ALPHA = float((2 * DEPTH) ** 0.25)
BETA = float((8 * DEPTH) ** -0.25)

kernel_name = "hybrid_sgu_shortconv_macaron_deepnorm"


def layer_norm(x, g, b):
    xf = x.astype(jnp.float32)
    mu = jnp.mean(xf, axis=-1, keepdims=True)
    d = xf - mu
    var = jnp.mean(d * d, axis=-1, keepdims=True)
    y = d * lax.rsqrt(var + LN_EPS)
    return (y * g.astype(jnp.float32) + b.astype(jnp.float32)).astype(x.dtype)


def swiglu(x, w_gate, w_up, w_down):
    return (jax.nn.silu(x @ w_gate) * (x @ w_up)) @ w_down


def chunked_sgu(z, w_s, b_s, g_v, b_v):
    bsz, seq, _ = z.shape
    n_chunks = seq // CHUNK
    u, v = z[..., :D_A], z[..., D_A:]
    shp = (bsz, n_chunks, CHUNK, N_HEADS_A, HEAD_DIM)
    v = v.reshape(shp)
    v = layer_norm(v, g_v.reshape(N_HEADS_A, HEAD_DIM), b_v.reshape(N_HEADS_A, HEAD_DIM))
    causal = jnp.tril(jnp.ones((CHUNK, CHUNK), dtype=bool))
    w = jnp.where(causal[None], w_s, jnp.zeros((), w_s.dtype))
    mixed = jnp.einsum('hts,bnshd->bnthd', w, v)
    mixed = mixed + jnp.transpose(b_s)[:, :, None]
    return (u.reshape(shp) * mixed).reshape(bsz, seq, D_A)


def gated_short_conv(gate_b, gate_c, xt, conv_w):
    h = gate_c * xt
    rhs = conv_w[:, None, :]
    y = lax.conv_general_dilated(
        h, rhs, window_strides=(1,), padding=[(CONV_W - 1, 0)],
        dimension_numbers=('NWC', 'WIO', 'NWC'), feature_group_count=D_B)
    return gate_b * y


def _fwd_setup_inputs(seed: int = 0) -> dict:
    key = jax.random.key(seed)
    ks = jax.random.split(key, 24)
    f32 = jnp.float32
    n = lambda k, shape, s: jax.random.normal(k, shape, f32) * s
    L = DEPTH
    return {
        "x": jax.random.normal(ks[0], (BATCH, SEQ, D_MODEL), f32),
        "ffa_gate": n(ks[1], (L, D_MODEL, D_FF), D_MODEL ** -0.5),
        "ffa_up": n(ks[2], (L, D_MODEL, D_FF), D_MODEL ** -0.5),
        "ffa_down": n(ks[3], (L, D_FF, D_MODEL), BETA * D_FF ** -0.5),
        "ln_a_g": 1.0 + n(ks[4], (L, D_MODEL), 0.01),
        "ln_a_b": n(ks[5], (L, D_MODEL), 0.01),
        "w_in": n(ks[6], (L, D_MODEL, D_IN), D_MODEL ** -0.5),
        "b_in": n(ks[7], (L, D_IN), 0.01),
        "w_s": n(ks[8], (L, N_HEADS_A, CHUNK, CHUNK), 0.5 * CHUNK ** -0.5),
        "b_s": 1.0 + n(ks[9], (L, N_HEADS_A, CHUNK), 0.01),
        "ln_v_g": 1.0 + n(ks[10], (L, D_A), 0.01),
        "ln_v_b": n(ks[11], (L, D_A), 0.01),
        "conv_w": n(ks[12], (L, CONV_W, D_B), CONV_W ** -0.5),
        "w_out": n(ks[13], (L, D_MIX, D_MODEL), BETA * D_MIX ** -0.5),
        "b_out": n(ks[14], (L, D_MODEL), 0.01),
        "ln_m_g": 1.0 + n(ks[15], (L, D_MODEL), 0.01),
        "ln_m_b": n(ks[16], (L, D_MODEL), 0.01),
        "ffc_gate": n(ks[17], (L, D_MODEL, D_FF), D_MODEL ** -0.5),
        "ffc_up": n(ks[18], (L, D_MODEL, D_FF), D_MODEL ** -0.5),
        "ffc_down": n(ks[19], (L, D_FF, D_MODEL), BETA * D_FF ** -0.5),
        "ln_c_g": 1.0 + n(ks[20], (L, D_MODEL), 0.01),
        "ln_c_b": n(ks[21], (L, D_MODEL), 0.01),
    }


def _fwd_reference(x, ffa_gate, ffa_up, ffa_down, ln_a_g, ln_a_b, w_in, b_in, w_s, b_s,
              ln_v_g, ln_v_b, conv_w, w_out, b_out, ln_m_g, ln_m_b,
              ffc_gate, ffc_up, ffc_down, ln_c_g, ln_c_b):
    h = x
    for l in range(DEPTH):
        h = layer_norm(ALPHA * h + 0.5 * swiglu(h, ffa_gate[l], ffa_up[l], ffa_down[l]),
                       ln_a_g[l], ln_a_b[l])
        z = h @ w_in[l] + b_in[l]
        z_a = jax.nn.gelu(z[..., :2 * D_A])
        o = 2 * D_A
        gate_b = z[..., o:o + D_B]
        gate_c = z[..., o + D_B:o + 2 * D_B]
        xt = z[..., o + 2 * D_B:]
        y_a = chunked_sgu(z_a, w_s[l], b_s[l], ln_v_g[l], ln_v_b[l])
        y_b = gated_short_conv(gate_b, gate_c, xt, conv_w[l])
        mix = jnp.concatenate([y_a, y_b], axis=-1) @ w_out[l] + b_out[l]
        h = layer_norm(ALPHA * h + mix, ln_m_g[l], ln_m_b[l])
        h = layer_norm(ALPHA * h + 0.5 * swiglu(h, ffc_gate[l], ffc_up[l], ffc_down[l]),
                       ln_c_g[l], ln_c_b[l])
    return h


import jax as _jax
import jax.numpy as _jnp

TWIN_FORMAT = 'train_step'
FWD_PARAMS = ['x', 'ffa_gate', 'ffa_up', 'ffa_down', 'ln_a_g', 'ln_a_b', 'w_in', 'b_in', 'w_s', 'b_s', 'ln_v_g', 'ln_v_b', 'conv_w', 'w_out', 'b_out', 'ln_m_g', 'ln_m_b', 'ffc_gate', 'ffc_up', 'ffc_down', 'ln_c_g', 'ln_c_b']
TWIN_WEIGHTS = ['ffa_gate', 'ffa_up', 'ffa_down', 'ln_a_g', 'ln_a_b', 'w_in', 'b_in', 'w_s', 'b_s', 'ln_v_g', 'ln_v_b', 'conv_w', 'w_out', 'b_out', 'ln_m_g', 'ln_m_b', 'ffc_gate', 'ffc_up', 'ffc_down', 'ln_c_g', 'ln_c_b']
TWIN_DIFF_INPUT = 'x'
TWIN_INPUTS = ['x', 'ffa_gate', 'ffa_up', 'ffa_down', 'ln_a_g', 'ln_a_b', 'w_in', 'b_in', 'w_s', 'b_s', 'ln_v_g', 'ln_v_b', 'conv_w', 'w_out', 'b_out', 'ln_m_g', 'ln_m_b', 'ffc_gate', 'ffc_up', 'ffc_down', 'ln_c_g', 'ln_c_b', 'loss_target', 'm_ffa_gate', 'm_ffa_up', 'm_ffa_down', 'm_ln_a_g', 'm_ln_a_b', 'm_w_in', 'm_b_in', 'm_w_s', 'm_b_s', 'm_ln_v_g', 'm_ln_v_b', 'm_conv_w', 'm_w_out', 'm_b_out', 'm_ln_m_g', 'm_ln_m_b', 'm_ffc_gate', 'm_ffc_up', 'm_ffc_down', 'm_ln_c_g', 'm_ln_c_b', 'v_ffa_gate', 'v_ffa_up', 'v_ffa_down', 'v_ln_a_g', 'v_ln_a_b', 'v_w_in', 'v_b_in', 'v_w_s', 'v_b_s', 'v_ln_v_g', 'v_ln_v_b', 'v_conv_w', 'v_w_out', 'v_b_out', 'v_ln_m_g', 'v_ln_m_b', 'v_ffc_gate', 'v_ffc_up', 'v_ffc_down', 'v_ln_c_g', 'v_ln_c_b']
TWIN_OUTPUTS = ['loss', 'grad_x', 'grad_ffa_gate', 'grad_ffa_up', 'grad_ffa_down', 'grad_ln_a_g', 'grad_ln_a_b', 'grad_w_in', 'grad_b_in', 'grad_w_s', 'grad_b_s', 'grad_ln_v_g', 'grad_ln_v_b', 'grad_conv_w', 'grad_w_out', 'grad_b_out', 'grad_ln_m_g', 'grad_ln_m_b', 'grad_ffc_gate', 'grad_ffc_up', 'grad_ffc_down', 'grad_ln_c_g', 'grad_ln_c_b', 'delta_ffa_gate', 'delta_ffa_up', 'delta_ffa_down', 'delta_ln_a_g', 'delta_ln_a_b', 'delta_w_in', 'delta_b_in', 'delta_w_s', 'delta_b_s', 'delta_ln_v_g', 'delta_ln_v_b', 'delta_conv_w', 'delta_w_out', 'delta_b_out', 'delta_ln_m_g', 'delta_ln_m_b', 'delta_ffc_gate', 'delta_ffc_up', 'delta_ffc_down', 'delta_ln_c_g', 'delta_ln_c_b', 'new_m_ffa_gate', 'new_m_ffa_up', 'new_m_ffa_down', 'new_m_ln_a_g', 'new_m_ln_a_b', 'new_m_w_in', 'new_m_b_in', 'new_m_w_s', 'new_m_b_s', 'new_m_ln_v_g', 'new_m_ln_v_b', 'new_m_conv_w', 'new_m_w_out', 'new_m_b_out', 'new_m_ln_m_g', 'new_m_ln_m_b', 'new_m_ffc_gate', 'new_m_ffc_up', 'new_m_ffc_down', 'new_m_ln_c_g', 'new_m_ln_c_b', 'new_v_ffa_gate', 'new_v_ffa_up', 'new_v_ffa_down', 'new_v_ln_a_g', 'new_v_ln_a_b', 'new_v_w_in', 'new_v_b_in', 'new_v_w_s', 'new_v_b_s', 'new_v_ln_v_g', 'new_v_ln_v_b', 'new_v_conv_w', 'new_v_w_out', 'new_v_b_out', 'new_v_ln_m_g', 'new_v_ln_m_b', 'new_v_ffc_gate', 'new_v_ffc_up', 'new_v_ffc_down', 'new_v_ln_c_g', 'new_v_ln_c_b']
TWIN_LEAF_KINDS = {'loss': 'loss', 'grad_x': 'grad_x', 'grad_ffa_gate': 'grad_w', 'grad_ffa_up': 'grad_w', 'grad_ffa_down': 'grad_w', 'grad_ln_a_g': 'grad_w', 'grad_ln_a_b': 'grad_w', 'grad_w_in': 'grad_w', 'grad_b_in': 'grad_w', 'grad_w_s': 'grad_w', 'grad_b_s': 'grad_w', 'grad_ln_v_g': 'grad_w', 'grad_ln_v_b': 'grad_w', 'grad_conv_w': 'grad_w', 'grad_w_out': 'grad_w', 'grad_b_out': 'grad_w', 'grad_ln_m_g': 'grad_w', 'grad_ln_m_b': 'grad_w', 'grad_ffc_gate': 'grad_w', 'grad_ffc_up': 'grad_w', 'grad_ffc_down': 'grad_w', 'grad_ln_c_g': 'grad_w', 'grad_ln_c_b': 'grad_w', 'delta_ffa_gate': 'delta_w', 'delta_ffa_up': 'delta_w', 'delta_ffa_down': 'delta_w', 'delta_ln_a_g': 'delta_w', 'delta_ln_a_b': 'delta_w', 'delta_w_in': 'delta_w', 'delta_b_in': 'delta_w', 'delta_w_s': 'delta_w', 'delta_b_s': 'delta_w', 'delta_ln_v_g': 'delta_w', 'delta_ln_v_b': 'delta_w', 'delta_conv_w': 'delta_w', 'delta_w_out': 'delta_w', 'delta_b_out': 'delta_w', 'delta_ln_m_g': 'delta_w', 'delta_ln_m_b': 'delta_w', 'delta_ffc_gate': 'delta_w', 'delta_ffc_up': 'delta_w', 'delta_ffc_down': 'delta_w', 'delta_ln_c_g': 'delta_w', 'delta_ln_c_b': 'delta_w', 'new_m_ffa_gate': 'new_m', 'new_m_ffa_up': 'new_m', 'new_m_ffa_down': 'new_m', 'new_m_ln_a_g': 'new_m', 'new_m_ln_a_b': 'new_m', 'new_m_w_in': 'new_m', 'new_m_b_in': 'new_m', 'new_m_w_s': 'new_m', 'new_m_b_s': 'new_m', 'new_m_ln_v_g': 'new_m', 'new_m_ln_v_b': 'new_m', 'new_m_conv_w': 'new_m', 'new_m_w_out': 'new_m', 'new_m_b_out': 'new_m', 'new_m_ln_m_g': 'new_m', 'new_m_ln_m_b': 'new_m', 'new_m_ffc_gate': 'new_m', 'new_m_ffc_up': 'new_m', 'new_m_ffc_down': 'new_m', 'new_m_ln_c_g': 'new_m', 'new_m_ln_c_b': 'new_m', 'new_v_ffa_gate': 'new_v', 'new_v_ffa_up': 'new_v', 'new_v_ffa_down': 'new_v', 'new_v_ln_a_g': 'new_v', 'new_v_ln_a_b': 'new_v', 'new_v_w_in': 'new_v', 'new_v_b_in': 'new_v', 'new_v_w_s': 'new_v', 'new_v_b_s': 'new_v', 'new_v_ln_v_g': 'new_v', 'new_v_ln_v_b': 'new_v', 'new_v_conv_w': 'new_v', 'new_v_w_out': 'new_v', 'new_v_b_out': 'new_v', 'new_v_ln_m_g': 'new_v', 'new_v_ln_m_b': 'new_v', 'new_v_ffc_gate': 'new_v', 'new_v_ffc_up': 'new_v', 'new_v_ffc_down': 'new_v', 'new_v_ln_c_g': 'new_v', 'new_v_ln_c_b': 'new_v'}


def _forward(args):
    return _fwd_reference(*[args[k] for k in FWD_PARAMS])


def _output_shape():
    def fwd():
        inp = _fwd_setup_inputs(0)
        return _fwd_reference(*[inp[k] for k in FWD_PARAMS])
    out = _jax.eval_shape(fwd)
    return out.shape, out.dtype

N_MICROBATCH = 1
ADAM_LR = 0.001
ADAM_B1 = 0.9
ADAM_B2 = 0.999
ADAM_EPS = 1e-08
ADAM_WD = 0.01
ADAM_STEP = 10
PER_EXAMPLE_BATCH_AXIS = {'x': 0, 'loss_target': 0}
SHARED_INPUTS = []
_WEIGHT_DTYPES = {'ffa_gate': _jnp.float32, 'ffa_up': _jnp.float32, 'ffa_down': _jnp.float32, 'ln_a_g': _jnp.float32, 'ln_a_b': _jnp.float32, 'w_in': _jnp.float32, 'b_in': _jnp.float32, 'w_s': _jnp.float32, 'b_s': _jnp.float32, 'ln_v_g': _jnp.float32, 'ln_v_b': _jnp.float32, 'conv_w': _jnp.float32, 'w_out': _jnp.float32, 'b_out': _jnp.float32, 'ln_m_g': _jnp.float32, 'ln_m_b': _jnp.float32, 'ffc_gate': _jnp.float32, 'ffc_up': _jnp.float32, 'ffc_down': _jnp.float32, 'ln_c_g': _jnp.float32, 'ln_c_b': _jnp.float32}
MOMENT_SCALE = {'ffa_gate': 1.280881e-02, 'ffa_up': 1.240113e-02, 'ffa_down': 3.458955e-02, 'ln_a_g': 3.873923e-01, 'ln_a_b': 2.258184e-01, 'w_in': 4.867948e-02, 'b_in': 5.160420e-02, 'w_s': 2.708000e-02, 'b_s': 3.912824e-02, 'ln_v_g': 1.359030e-02, 'ln_v_b': 1.391426e-02, 'conv_w': 5.854150e-02, 'w_out': 8.721637e-02, 'b_out': 1.817888e-01, 'ln_m_g': 4.655102e-01, 'ln_m_b': 2.341360e-01, 'ffc_gate': 1.148925e-02, 'ffc_up': 1.114477e-02, 'ffc_down': 3.106663e-02, 'ln_c_g': 3.198277e+01, 'ln_c_b': 2.809773e+00}


def _to_microbatches(a, axis):
    t = _jnp.moveaxis(a, axis, 0)
    t = t.reshape((N_MICROBATCH, t.shape[0] // N_MICROBATCH) + t.shape[1:])
    return _jnp.moveaxis(t, 1, axis + 1)


def setup_inputs(seed: int = 0) -> dict:
    inp = _fwd_setup_inputs(seed)
    key = _jax.random.fold_in(_jax.random.key(seed), 7919)
    shape, _ = _output_shape()
    out = dict(inp)
    out["loss_target"] = _jax.random.normal(_jax.random.fold_in(key, 0), shape, _jnp.float32)
    for i, name in enumerate(TWIN_WEIGHTS):
        w = inp[name].astype(_jnp.float32)
        if MOMENT_SCALE is None:
            s = _jnp.sqrt(_jnp.mean(_jnp.square(w)) + 1e-30)
        else:
            s = MOMENT_SCALE[name]
        km, kv = _jax.random.split(_jax.random.fold_in(key, i + 1))
        out[name] = w
        out["m_" + name] = s * _jax.random.normal(km, w.shape, _jnp.float32)
        out["v_" + name] = (s * s) * _jax.random.uniform(kv, w.shape, _jnp.float32, 0.5, 1.5)
    if N_MICROBATCH > 1:
        for name, axis in PER_EXAMPLE_BATCH_AXIS.items():
            out[name] = _to_microbatches(out[name], axis)
    return {'x': out['x'], 'ffa_gate': out['ffa_gate'], 'ffa_up': out['ffa_up'], 'ffa_down': out['ffa_down'], 'ln_a_g': out['ln_a_g'], 'ln_a_b': out['ln_a_b'], 'w_in': out['w_in'], 'b_in': out['b_in'], 'w_s': out['w_s'], 'b_s': out['b_s'], 'ln_v_g': out['ln_v_g'], 'ln_v_b': out['ln_v_b'], 'conv_w': out['conv_w'], 'w_out': out['w_out'], 'b_out': out['b_out'], 'ln_m_g': out['ln_m_g'], 'ln_m_b': out['ln_m_b'], 'ffc_gate': out['ffc_gate'], 'ffc_up': out['ffc_up'], 'ffc_down': out['ffc_down'], 'ln_c_g': out['ln_c_g'], 'ln_c_b': out['ln_c_b'], 'loss_target': out['loss_target'], 'm_ffa_gate': out['m_ffa_gate'], 'm_ffa_up': out['m_ffa_up'], 'm_ffa_down': out['m_ffa_down'], 'm_ln_a_g': out['m_ln_a_g'], 'm_ln_a_b': out['m_ln_a_b'], 'm_w_in': out['m_w_in'], 'm_b_in': out['m_b_in'], 'm_w_s': out['m_w_s'], 'm_b_s': out['m_b_s'], 'm_ln_v_g': out['m_ln_v_g'], 'm_ln_v_b': out['m_ln_v_b'], 'm_conv_w': out['m_conv_w'], 'm_w_out': out['m_w_out'], 'm_b_out': out['m_b_out'], 'm_ln_m_g': out['m_ln_m_g'], 'm_ln_m_b': out['m_ln_m_b'], 'm_ffc_gate': out['m_ffc_gate'], 'm_ffc_up': out['m_ffc_up'], 'm_ffc_down': out['m_ffc_down'], 'm_ln_c_g': out['m_ln_c_g'], 'm_ln_c_b': out['m_ln_c_b'], 'v_ffa_gate': out['v_ffa_gate'], 'v_ffa_up': out['v_ffa_up'], 'v_ffa_down': out['v_ffa_down'], 'v_ln_a_g': out['v_ln_a_g'], 'v_ln_a_b': out['v_ln_a_b'], 'v_w_in': out['v_w_in'], 'v_b_in': out['v_b_in'], 'v_w_s': out['v_w_s'], 'v_b_s': out['v_b_s'], 'v_ln_v_g': out['v_ln_v_g'], 'v_ln_v_b': out['v_ln_v_b'], 'v_conv_w': out['v_conv_w'], 'v_w_out': out['v_w_out'], 'v_b_out': out['v_b_out'], 'v_ln_m_g': out['v_ln_m_g'], 'v_ln_m_b': out['v_ln_m_b'], 'v_ffc_gate': out['v_ffc_gate'], 'v_ffc_up': out['v_ffc_up'], 'v_ffc_down': out['v_ffc_down'], 'v_ln_c_g': out['v_ln_c_g'], 'v_ln_c_b': out['v_ln_c_b']}


def _loss(weights, diff, rest, loss_target):
    with _jax.named_scope("forward"):
        args = {**rest, TWIN_DIFF_INPUT: diff, **{k: w.astype(_WEIGHT_DTYPES[k]) for k, w in weights.items()}}
        y = _forward(args)
    with _jax.named_scope("loss_head"):
        err = _jnp.square(y.astype(_jnp.float32) - loss_target)
        return 0.5 * _jnp.sum(_jnp.mean(err, axis=-1)) if err.ndim else 0.5 * err


def _adamw(w, g, m, v):
    m = ADAM_B1 * m + (1.0 - ADAM_B1) * g
    v = ADAM_B2 * v + (1.0 - ADAM_B2) * _jnp.square(g)
    m_hat = m / (1.0 - ADAM_B1 ** ADAM_STEP)
    v_hat = v / (1.0 - ADAM_B2 ** ADAM_STEP)
    delta = -ADAM_LR * (m_hat / (_jnp.sqrt(v_hat) + ADAM_EPS) + ADAM_WD * w)
    return delta, m, v


def reference(x, ffa_gate, ffa_up, ffa_down, ln_a_g, ln_a_b, w_in, b_in, w_s, b_s, ln_v_g, ln_v_b, conv_w, w_out, b_out, ln_m_g, ln_m_b, ffc_gate, ffc_up, ffc_down, ln_c_g, ln_c_b, loss_target, m_ffa_gate, m_ffa_up, m_ffa_down, m_ln_a_g, m_ln_a_b, m_w_in, m_b_in, m_w_s, m_b_s, m_ln_v_g, m_ln_v_b, m_conv_w, m_w_out, m_b_out, m_ln_m_g, m_ln_m_b, m_ffc_gate, m_ffc_up, m_ffc_down, m_ln_c_g, m_ln_c_b, v_ffa_gate, v_ffa_up, v_ffa_down, v_ln_a_g, v_ln_a_b, v_w_in, v_b_in, v_w_s, v_b_s, v_ln_v_g, v_ln_v_b, v_conv_w, v_w_out, v_b_out, v_ln_m_g, v_ln_m_b, v_ffc_gate, v_ffc_up, v_ffc_down, v_ln_c_g, v_ln_c_b):
    given = dict(x=x, ffa_gate=ffa_gate, ffa_up=ffa_up, ffa_down=ffa_down, ln_a_g=ln_a_g, ln_a_b=ln_a_b, w_in=w_in, b_in=b_in, w_s=w_s, b_s=b_s, ln_v_g=ln_v_g, ln_v_b=ln_v_b, conv_w=conv_w, w_out=w_out, b_out=b_out, ln_m_g=ln_m_g, ln_m_b=ln_m_b, ffc_gate=ffc_gate, ffc_up=ffc_up, ffc_down=ffc_down, ln_c_g=ln_c_g, ln_c_b=ln_c_b, loss_target=loss_target, m_ffa_gate=m_ffa_gate, m_ffa_up=m_ffa_up, m_ffa_down=m_ffa_down, m_ln_a_g=m_ln_a_g, m_ln_a_b=m_ln_a_b, m_w_in=m_w_in, m_b_in=m_b_in, m_w_s=m_w_s, m_b_s=m_b_s, m_ln_v_g=m_ln_v_g, m_ln_v_b=m_ln_v_b, m_conv_w=m_conv_w, m_w_out=m_w_out, m_b_out=m_b_out, m_ln_m_g=m_ln_m_g, m_ln_m_b=m_ln_m_b, m_ffc_gate=m_ffc_gate, m_ffc_up=m_ffc_up, m_ffc_down=m_ffc_down, m_ln_c_g=m_ln_c_g, m_ln_c_b=m_ln_c_b, v_ffa_gate=v_ffa_gate, v_ffa_up=v_ffa_up, v_ffa_down=v_ffa_down, v_ln_a_g=v_ln_a_g, v_ln_a_b=v_ln_a_b, v_w_in=v_w_in, v_b_in=v_b_in, v_w_s=v_w_s, v_b_s=v_b_s, v_ln_v_g=v_ln_v_g, v_ln_v_b=v_ln_v_b, v_conv_w=v_conv_w, v_w_out=v_w_out, v_b_out=v_b_out, v_ln_m_g=v_ln_m_g, v_ln_m_b=v_ln_m_b, v_ffc_gate=v_ffc_gate, v_ffc_up=v_ffc_up, v_ffc_down=v_ffc_down, v_ln_c_g=v_ln_c_g, v_ln_c_b=v_ln_c_b)
    weights = {n: given[n] for n in TWIN_WEIGHTS}
    shared = {n: given[n] for n in SHARED_INPUTS}
    per_example = {n: given[n] for n in ['x']}
    grad_fn = _jax.value_and_grad(_loss, argnums=(0, 1))

    def one_microbatch(ex, loss_target):
        ex = dict(ex)
        diff = ex.pop(TWIN_DIFF_INPUT)
        return grad_fn(weights, diff, {**shared, **ex}, loss_target)

    if N_MICROBATCH == 1:
        loss, (grad_w, grad_x) = one_microbatch(per_example, given["loss_target"])
    else:
        def body(carry, xs):
            loss_sum, grad_sum = carry
            l_k, (gw_k, gx_k) = one_microbatch(xs[0], xs[1])
            with _jax.named_scope("update"):
                return (loss_sum + l_k, _jax.tree.map(_jnp.add, grad_sum, gw_k)), gx_k

        init = (_jnp.zeros((), _jnp.float32), _jax.tree.map(_jnp.zeros_like, weights))
        (loss, grad_w), grad_x = _jax.lax.scan(body, init, (per_example, given["loss_target"]))
    with _jax.named_scope("update"):
        delta_w, new_m, new_v = {}, {}, {}
        for n in TWIN_WEIGHTS:
            delta_w[n], new_m[n], new_v[n] = _adamw(weights[n], grad_w[n], given["m_" + n], given["v_" + n])
    return (loss, grad_x, *[grad_w[n] for n in TWIN_WEIGHTS], *[delta_w[n] for n in TWIN_WEIGHTS],
            *[new_m[n] for n in TWIN_WEIGHTS], *[new_v[n] for n in TWIN_WEIGHTS])
```

```python
import jax
import jax.numpy as jnp
from jax import lax
from jax.experimental import pallas as pl
from jax.experimental.pallas import tpu as pltpu

BF16 = jnp.bfloat16
F32 = jnp.float32
MESH = pl.DeviceIdType.MESH

N_DEV = 8
N_CHIP = 4
HEAD = 128
CHUNK = 128
CONV_TAPS = 3
LN_EPS = 1e-5
ALPHA = float(2 ** 0.25)
GELU_C = 0.7978845608028654
GELU_A = 0.044715
ADAM_LR, ADAM_B1, ADAM_B2, ADAM_EPS, ADAM_WD, ADAM_STEP = 0.001, 0.9, 0.999, 1e-08, 0.01, 10
V7X_VMEM_LIMIT = 56 * 1024 * 1024
LANES = 128
BF16_ROWS = 16

NT_DIMS = (((1,), (1,)), ((), ()))
TN_DIMS = (((0,), (0,)), ((), ()))


def _params(*sem):
    return pltpu.CompilerParams(dimension_semantics=sem, vmem_limit_bytes=V7X_VMEM_LIMIT)


def _gelu(x):
    return 0.5 * x * (1.0 + jnp.tanh(GELU_C * (x + GELU_A * x * x * x)))


def _gelu_grad(x):
    t = jnp.tanh(GELU_C * (x + GELU_A * x * x * x))
    return 0.5 * (1.0 + t) + 0.5 * x * (1.0 - t * t) * GELU_C * (1.0 + 3.0 * GELU_A * x * x)


def _sigmoid(x):
    return 1.0 / (1.0 + jnp.exp(-x))


def _row(shape):
    return pl.BlockSpec(shape, lambda *_: (0,) * len(shape))


def _row_blocks(tm, rows=128):
    rows = min(rows, tm)
    return [slice(r, r + rows) for r in range(0, tm, rows)]


def _ln_backward(dh, xhat, rstd, gain):
    dxh = dh * gain
    m1 = jnp.mean(dxh, axis=-1, keepdims=True)
    m2 = jnp.mean(dxh * xhat, axis=-1, keepdims=True)
    return rstd * (dxh - m1 - xhat * m2)


def _ffn_gateup(xb, wg, wu, name):
    s, d = xb.shape
    nsh, _, fs = wg.shape
    tm = min(s, 1024)

    def body(x_ref, wg_ref, wu_ref, g_ref, u_ref, h_ref):
        x = x_ref[...]
        g = jnp.dot(x, wg_ref[...], preferred_element_type=F32)
        u = jnp.dot(x, wu_ref[...], preferred_element_type=F32)
        g_ref[...] = g.astype(BF16)
        u_ref[...] = u.astype(BF16)
        h_ref[...] = (g * _sigmoid(g) * u).astype(BF16)

    shp = jax.ShapeDtypeStruct((nsh, s, fs), BF16)
    w_spec = pl.BlockSpec((None, d, fs), lambda i, j: (j, 0, 0))
    o_spec = pl.BlockSpec((None, tm, fs), lambda i, j: (j, i, 0))
    return pl.pallas_call(
        body, name=name, grid=(s // tm, nsh),
        in_specs=[pl.BlockSpec((tm, d), lambda i, j: (i, 0)), w_spec, w_spec],
        out_specs=[o_spec, o_spec, o_spec], out_shape=[shp, shp, shp],
        compiler_params=_params("parallel", "arbitrary"),
    )(xb, wg, wu)


def _down_ln(a3, w3, bias, res, res_affine, ln_g, ln_b, scale, name, target=None):
    nk, s, tk = a3.shape
    d = w3.shape[2]
    tm = min(s, 512)
    final = target is not None

    def body(*refs):
        refs = list(refs)
        a_ref, w_ref = refs[:2]
        del refs[:2]
        bias_ref = refs.pop(0) if bias is not None else None
        res_ref = refs.pop(0)
        rg_ref, rb_ref = (refs.pop(0), refs.pop(0)) if res_affine is not None else (None, None)
        g_ref, b_ref = refs.pop(0), refs.pop(0)
        t_ref = refs.pop(0) if final else None
        acc_ref = refs.pop()
        i, k = pl.program_id(0), pl.program_id(1)

        @pl.when(k == 0)
        def _():
            acc_ref[...] = jnp.zeros_like(acc_ref)

        acc_ref[...] += jnp.dot(a_ref[...], w_ref[...], preferred_element_type=F32)

        @pl.when(k == nk - 1)
        def _():
            if final:
                dr_ref, drb_ref, sq_ref, dg_ref, db_ref = refs

                @pl.when(i == 0)
                def _():
                    sq_ref[...] = jnp.zeros_like(sq_ref)
                    dg_ref[...] = jnp.zeros_like(dg_ref)
                    db_ref[...] = jnp.zeros_like(db_ref)
            else:
                xh_ref, hb_ref, rstd_ref = refs

            for rows in _row_blocks(tm):
                r = res_ref[rows, :]
                if rg_ref is not None:
                    r = r * rg_ref[...] + rb_ref[...]
                y = acc_ref[rows, :]
                if bias_ref is not None:
                    y = y + bias_ref[...]
                r = ALPHA * r + scale * y
                mu = jnp.mean(r, axis=-1, keepdims=True)
                c = r - mu
                var = jnp.mean(c * c, axis=-1, keepdims=True)
                rstd = lax.rsqrt(var + LN_EPS)
                xhat = c * rstd
                h = xhat * g_ref[...] + b_ref[...]
                if not final:
                    xh_ref[rows, :] = xhat
                    hb_ref[rows, :] = h.astype(BF16)
                    rstd_ref[rows, :] = rstd
                else:
                    err = h - t_ref[rows, :]
                    sq_ref[...] += jnp.sum(err * err, axis=0, keepdims=True)
                    dh = err * (1.0 / d)
                    dg_ref[...] += jnp.sum(dh * xhat, axis=0, keepdims=True)
                    db_ref[...] += jnp.sum(dh, axis=0, keepdims=True)
                    dr = _ln_backward(dh, xhat, rstd, g_ref[...])
                    dr_ref[rows, :] = dr
                    drb_ref[rows, :] = dr.astype(BF16)

    tok = pl.BlockSpec((tm, d), lambda i, k: (i, 0))
    vec = pl.BlockSpec((1, d), lambda i, k: (0, 0))
    ins = [a3, w3]
    in_specs = [pl.BlockSpec((None, tm, tk), lambda i, k: (k, i, 0)), pl.BlockSpec((None, tk, d), lambda i, k: (k, 0, 0))]
    if bias is not None:
        ins.append(bias)
        in_specs.append(vec)
    ins.append(res)
    in_specs.append(tok)
    if res_affine is not None:
        ins += list(res_affine)
        in_specs += [vec, vec]
    ins += [ln_g, ln_b]
    in_specs += [vec, vec]
    if final:
        ins.append(target)
        in_specs.append(tok)
        out_shape = [jax.ShapeDtypeStruct((s, d), F32), jax.ShapeDtypeStruct((s, d), BF16)] + [jax.ShapeDtypeStruct((1, d), F32)] * 3
        out_specs = [tok, tok, vec, vec, vec]
        sem = ("arbitrary", "arbitrary")
    else:
        out_shape = [jax.ShapeDtypeStruct((s, d), F32), jax.ShapeDtypeStruct((s, d), BF16), jax.ShapeDtypeStruct((s, 1), F32)]
        out_specs = [tok, tok, pl.BlockSpec((tm, 1), lambda i, k: (i, 0))]
        sem = ("parallel", "arbitrary")
    return pl.pallas_call(
        body, name=name, grid=(s // tm, nk), in_specs=in_specs, out_specs=out_specs, out_shape=out_shape,
        scratch_shapes=[pltpu.VMEM((tm, d), F32)], compiler_params=_params(*sem),
    )(*ins)


def _proj_in(hb, w3, bias, name):
    s, d = hb.shape
    nsh, _, cs = w3.shape
    tm = min(s, 1024)

    def body(h_ref, w_ref, b_ref, z_ref):
        z_ref[...] = (jnp.dot(h_ref[...], w_ref[...], preferred_element_type=F32) + b_ref[...]).astype(BF16)

    return pl.pallas_call(
        body, name=name, grid=(s // tm, nsh),
        in_specs=[pl.BlockSpec((tm, d), lambda i, j: (i, 0)), pl.BlockSpec((None, d, cs), lambda i, j: (j, 0, 0)),
                  pl.BlockSpec((1, cs), lambda i, j: (0, j))],
        out_specs=pl.BlockSpec((tm, cs), lambda i, j: (i, j)), out_shape=jax.ShapeDtypeStruct((s, nsh * cs), BF16),
        compiler_params=_params("parallel", "arbitrary"),
    )(hb, w3, bias)


def _nt_hidden(ab, w3, gate_up, scale, name):
    s, kdim = ab.shape
    nj, tn, _ = w3.shape
    tm = min(s, 1024)

    def body(*refs):
        a_ref, w_ref = refs[:2]
        t = scale * lax.dot_general(a_ref[...], w_ref[...], NT_DIMS, preferred_element_type=F32)
        if gate_up is None:
            refs[2][...] = t.astype(BF16)
        else:
            g_ref, u_ref, dg_ref, du_ref = refs[2:]
            g = g_ref[...].astype(F32)
            sg = _sigmoid(g)
            du_ref[...] = (t * g * sg).astype(BF16)
            dg_ref[...] = (t * u_ref[...].astype(F32) * (sg * (1.0 + g * (1.0 - sg)))).astype(BF16)

    hid = pl.BlockSpec((None, tm, tn), lambda i, j: (j, i, 0))
    shp = jax.ShapeDtypeStruct((nj, s, tn), BF16)
    ins = [ab, w3]
    in_specs = [pl.BlockSpec((tm, kdim), lambda i, j: (i, 0)), pl.BlockSpec((None, tn, kdim), lambda i, j: (j, 0, 0))]
    if gate_up is None:
        out_specs, out_shape = hid, shp
    else:
        ins += list(gate_up)
        in_specs += [hid, hid]
        out_specs, out_shape = [hid, hid], [shp, shp]
    return pl.pallas_call(
        body, name=name, grid=(s // tm, nj), in_specs=in_specs, out_specs=out_specs, out_shape=out_shape,
        compiler_params=_params("parallel", "arbitrary"),
    )(*ins)


def _tn_dw(a, a_spec, b, b_spec, nj, m, n, s, tk, scale, name):
    def body(a_ref, b_ref, o_ref, acc_ref):
        k = pl.program_id(1)

        @pl.when(k == 0)
        def _():
            acc_ref[...] = jnp.zeros_like(acc_ref)

        acc_ref[...] += lax.dot_general(a_ref[...], b_ref[...], TN_DIMS, preferred_element_type=F32)

        @pl.when(k == s // tk - 1)
        def _():
            o_ref[...] = (scale * acc_ref[...]).astype(BF16)

    return pl.pallas_call(
        body, name=name, grid=(nj, s // tk), in_specs=[a_spec, b_spec],
        out_specs=pl.BlockSpec((None, m, n), lambda j, k: (j, 0, 0)), out_shape=jax.ShapeDtypeStruct((nj, m, n), BF16),
        scratch_shapes=[pltpu.VMEM((m, n), F32)], compiler_params=_params("parallel", "arbitrary"),
    )(a, b)


def _dw_hidden_rows(hid3, db, scale, name):
    nj, s, fs = hid3.shape
    d = db.shape[1]
    tk = min(s, 512)
    return _tn_dw(hid3, pl.BlockSpec((None, tk, fs), lambda j, k: (j, k, 0)), db, pl.BlockSpec((tk, d), lambda j, k: (k, 0)),
                  nj, fs, d, s, tk, scale, name)


def _dw_hidden_cols(ab, hid3, name):
    nj, s, fs = hid3.shape
    d = ab.shape[1]
    tk = min(s, 512)
    return _tn_dw(ab, pl.BlockSpec((tk, d), lambda j, k: (k, 0)), hid3, pl.BlockSpec((None, tk, fs), lambda j, k: (j, k, 0)),
                  nj, d, fs, s, tk, 1.0, name)


def _dw_cols(ab, dz, nj, name):
    s, d = ab.shape
    cs = dz.shape[1] // nj
    tk = min(s, 512)
    return _tn_dw(ab, pl.BlockSpec((tk, d), lambda j, k: (k, 0)), dz, pl.BlockSpec((tk, cs), lambda j, k: (k, j)),
                  nj, d, cs, s, tk, 1.0, name)


def _nt_acc(pairs, nk, dres, ln, name):
    s, d = dres.shape
    tm = min(s, 512)
    np_ = len(pairs)

    def body(*refs):
        refs = list(refs)
        pair_refs = [(refs[2 * p], refs[2 * p + 1]) for p in range(np_)]
        del refs[:2 * np_]
        dres_ref = refs.pop(0)
        if ln is not None:
            xh_ref, rstd_ref, gain_ref = refs.pop(0), refs.pop(0), refs.pop(0)
        acc_ref = refs.pop()
        i, k = pl.program_id(0), pl.program_id(1)

        @pl.when(k == 0)
        def _():
            acc_ref[...] = jnp.zeros_like(acc_ref)

        for a_ref, w_ref in pair_refs:
            acc_ref[...] += lax.dot_general(a_ref[...], w_ref[...], NT_DIMS, preferred_element_type=F32)

        @pl.when(k == nk - 1)
        def _():
            if ln is not None:
                dr_ref, drb_ref, dg_ref, db_ref, sum_ref = refs

                @pl.when(i == 0)
                def _():
                    dg_ref[...] = jnp.zeros_like(dg_ref)
                    db_ref[...] = jnp.zeros_like(db_ref)
                    sum_ref[...] = jnp.zeros_like(sum_ref)

            for rows in _row_blocks(tm):
                dh = ALPHA * dres_ref[rows, :] + acc_ref[rows, :]
                if ln is None:
                    refs[0][rows, :] = dh
                else:
                    xhat = xh_ref[rows, :]
                    dg_ref[...] += jnp.sum(dh * xhat, axis=0, keepdims=True)
                    db_ref[...] += jnp.sum(dh, axis=0, keepdims=True)
                    dr = _ln_backward(dh, xhat, rstd_ref[rows, :], gain_ref[...])
                    sum_ref[...] += jnp.sum(dr, axis=0, keepdims=True)
                    dr_ref[rows, :] = dr
                    drb_ref[rows, :] = dr.astype(BF16)

    tok = pl.BlockSpec((tm, d), lambda i, k: (i, 0))
    vec = pl.BlockSpec((1, d), lambda i, k: (0, 0))
    ins, in_specs = [], []
    for a, a_spec, w, w_spec in pairs:
        ins += [a, w]
        in_specs += [a_spec, w_spec]
    ins.append(dres)
    in_specs.append(tok)
    if ln is None:
        out_shape, out_specs, sem = jax.ShapeDtypeStruct((s, d), F32), tok, ("parallel", "arbitrary")
    else:
        ins += list(ln)
        in_specs += [tok, pl.BlockSpec((tm, 1), lambda i, k: (i, 0)), vec]
        out_shape = [jax.ShapeDtypeStruct((s, d), F32), jax.ShapeDtypeStruct((s, d), BF16)] + [jax.ShapeDtypeStruct((1, d), F32)] * 3
        out_specs, sem = [tok, tok, vec, vec, vec], ("arbitrary", "arbitrary")
    return pl.pallas_call(
        body, name=name, grid=(s // tm, nk), in_specs=in_specs, out_specs=out_specs, out_shape=out_shape,
        scratch_shapes=[pltpu.VMEM((tm, d), F32)], compiler_params=_params(*sem),
    )(*ins)


def _hidden_pairs(dg3, wg3, du3, wu3, tm):
    nj, s, fs = dg3.shape
    d = wg3.shape[1]
    a_spec = pl.BlockSpec((None, tm, fs), lambda i, k: (k, i, 0))
    w_spec = pl.BlockSpec((None, d, fs), lambda i, k: (k, 0, 0))
    return [(dg3, a_spec, wg3, w_spec), (du3, a_spec, wu3, w_spec)]


def _shift_rows_down(v, halo, k, row):
    out = pltpu.roll(v, k, 0)
    hr = halo.shape[0]
    for r in range(k):
        out = jnp.where(row == r, halo[hr - k + r:hr - k + r + 1, :], out)
    return out


def _shift_rows_up(v, halo, k, row):
    t = v.shape[0]
    out = pltpu.roll(v, t - k, 0)
    for r in range(k):
        out = jnp.where(row == t - k + r, halo[r:r + 1, :], out)
    return out


def _sgu_head_forward(z_ref, h, da, gv_ref, bv_ref):
    zu = z_ref[:, h * HEAD:(h + 1) * HEAD].astype(F32)
    zv = z_ref[:, da + h * HEAD:da + (h + 1) * HEAD].astype(F32)
    u = _gelu(zu)
    v = _gelu(zv)
    mu = jnp.mean(v, axis=-1, keepdims=True)
    c = v - mu
    rstd = lax.rsqrt(jnp.mean(c * c, axis=-1, keepdims=True) + LN_EPS)
    vhat = c * rstd
    vln = (vhat * gv_ref[h:h + 1, :] + bv_ref[h:h + 1, :]).astype(BF16)
    return zu, zv, u, vhat, rstd, vln


def _mixer_fwd(z, ws_masked, bs_wide, gv, bv, cw, name):
    s, zc = z.shape
    da = zc // 5
    nh = da // HEAD
    tm = min(s, 512)
    hb = tm // BF16_ROWS

    def body(z_ref, pc_ref, px_ref, ws_ref, bs_ref, gv_ref, bv_ref, cw_ref, y_ref):
        i = pl.program_id(0)
        for h in range(nh):
            _, _, u, _, _, vln = _sgu_head_forward(z_ref, h, da, gv_ref, bv_ref)
            for n in range(tm // CHUNK):
                rows = slice(n * CHUNK, (n + 1) * CHUNK)
                mixed = jnp.dot(ws_ref[h], vln[rows], preferred_element_type=F32) + bs_ref[h]
                y_ref[0, rows, h * HEAD:(h + 1) * HEAD] = (u[rows] * mixed).astype(BF16)
        gate_b = z_ref[:, 2 * da:3 * da].astype(F32)
        hc = z_ref[:, 3 * da:4 * da].astype(F32) * z_ref[:, 4 * da:5 * da].astype(F32)
        halo = jnp.where(i > 0, pc_ref[...].astype(F32) * px_ref[...].astype(F32), 0.0)
        row = lax.broadcasted_iota(jnp.int32, (tm, da), 0)
        y = cw_ref[0:1, :] * _shift_rows_down(hc, halo, 2, row) + cw_ref[1:2, :] * _shift_rows_down(hc, halo, 1, row) + cw_ref[2:3, :] * hc
        y_ref[1] = (gate_b * y).astype(BF16)

    prev = lambda col: pl.BlockSpec((BF16_ROWS, da), lambda i: (jnp.maximum(i * hb - 1, 0), col))
    return pl.pallas_call(
        body, name=name, grid=(s // tm,),
        in_specs=[pl.BlockSpec((tm, zc), lambda i: (i, 0)), prev(3), prev(4), _row((nh, CHUNK, CHUNK)), _row((nh, CHUNK, HEAD)),
                  _row((nh, HEAD)), _row((nh, HEAD)), _row((CONV_TAPS, da))],
        out_specs=pl.BlockSpec((2, tm, da), lambda i: (0, i, 0)), out_shape=jax.ShapeDtypeStruct((2, s, da), BF16),
        compiler_params=_params("parallel"),
    )(z, z, z, ws_masked, bs_wide, gv, bv, cw)


def _mixer_bwd(z, dy, ws_masked, bs_wide, gv, bv, cw, name):
    s, zc = z.shape
    da = zc // 5
    nh = da // HEAD
    tm = min(s, 512)
    hb = tm // BF16_ROWS
    nblk = s // tm

    def body(z_ref, pc_ref, px_ref, nb_ref, dy_ref, ndy_ref, ws_ref, bs_ref, gv_ref, bv_ref, cw_ref,
             dz_ref, dws_ref, dbs_ref, dgv_ref, dbv_ref, dcw_ref, dbin_ref):
        i = pl.program_id(0)

        @pl.when(i == 0)
        def _():
            for ref in (dws_ref, dbs_ref, dgv_ref, dbv_ref, dcw_ref, dbin_ref):
                ref[...] = jnp.zeros_like(ref)

        causal = lax.broadcasted_iota(jnp.int32, (CHUNK, CHUNK), 0) >= lax.broadcasted_iota(jnp.int32, (CHUNK, CHUNK), 1)
        for h in range(nh):
            zu, zv, u, vhat, rstd, vln = _sgu_head_forward(z_ref, h, da, gv_ref, bv_ref)
            dya = dy_ref[0, :, h * HEAD:(h + 1) * HEAD].astype(F32)
            w = ws_ref[h]
            du_parts, dvln_parts = [], []
            for n in range(tm // CHUNK):
                rows = slice(n * CHUNK, (n + 1) * CHUNK)
                mixed = jnp.dot(w, vln[rows], preferred_element_type=F32) + bs_ref[h]
                du_parts.append(dya[rows] * mixed)
                dmix = dya[rows] * u[rows]
                dmix_b = dmix.astype(BF16)
                dws_ref[h] += jnp.where(causal, lax.dot_general(dmix_b, vln[rows], NT_DIMS, preferred_element_type=F32), 0.0)
                dbs_ref[h] += dmix
                dvln_parts.append(lax.dot_general(w, dmix_b, TN_DIMS, preferred_element_type=F32))
            du = jnp.concatenate(du_parts, axis=0)
            dvln = jnp.concatenate(dvln_parts, axis=0)
            dgv_ref[h:h + 1, :] += jnp.sum(dvln * vhat, axis=0, keepdims=True)
            dbv_ref[h:h + 1, :] += jnp.sum(dvln, axis=0, keepdims=True)
            dv = _ln_backward(dvln, vhat, rstd, gv_ref[h:h + 1, :])
            dzu = du * _gelu_grad(zu)
            dzv = dv * _gelu_grad(zv)
            ucols = slice(h * HEAD, (h + 1) * HEAD)
            vcols = slice(da + h * HEAD, da + (h + 1) * HEAD)
            dz_ref[:, ucols] = dzu.astype(BF16)
            dz_ref[:, vcols] = dzv.astype(BF16)
            dbin_ref[:, ucols] += jnp.sum(dzu, axis=0, keepdims=True)
            dbin_ref[:, vcols] += jnp.sum(dzv, axis=0, keepdims=True)

        gate_b = z_ref[:, 2 * da:3 * da].astype(F32)
        gate_c = z_ref[:, 3 * da:4 * da].astype(F32)
        xt = z_ref[:, 4 * da:5 * da].astype(F32)
        hc = gate_c * xt
        halo = jnp.where(i > 0, pc_ref[...].astype(F32) * px_ref[...].astype(F32), 0.0)
        row = lax.broadcasted_iota(jnp.int32, (tm, da), 0)
        sh1 = _shift_rows_down(hc, halo, 1, row)
        sh2 = _shift_rows_down(hc, halo, 2, row)
        y = cw_ref[0:1, :] * sh2 + cw_ref[1:2, :] * sh1 + cw_ref[2:3, :] * hc
        dyb = dy_ref[1].astype(F32)
        dconv = dyb * gate_b
        nhalo = jnp.where(i < nblk - 1, ndy_ref[...].astype(F32) * nb_ref[...].astype(F32), 0.0)
        dhc = cw_ref[2:3, :] * dconv + cw_ref[1:2, :] * _shift_rows_up(dconv, nhalo, 1, row) + cw_ref[0:1, :] * _shift_rows_up(dconv, nhalo, 2, row)
        dcw_ref[0:1, :] += jnp.sum(dconv * sh2, axis=0, keepdims=True)
        dcw_ref[1:2, :] += jnp.sum(dconv * sh1, axis=0, keepdims=True)
        dcw_ref[2:3, :] += jnp.sum(dconv * hc, axis=0, keepdims=True)
        for col, val in ((2, dyb * y), (3, dhc * xt), (4, dhc * gate_c)):
            cols = slice(col * da, (col + 1) * da)
            dz_ref[:, cols] = val.astype(BF16)
            dbin_ref[:, cols] += jnp.sum(val, axis=0, keepdims=True)

        @pl.when(i == nblk - 1)
        def _():
            for h in range(nh):
                dbs_ref[h] = jnp.broadcast_to(jnp.sum(dbs_ref[h], axis=1, keepdims=True), (CHUNK, HEAD))

    prev = lambda col: pl.BlockSpec((BF16_ROWS, da), lambda i: (jnp.maximum(i * hb - 1, 0), col))
    nxt = lambda i: jnp.minimum((i + 1) * hb, s // BF16_ROWS - 1)
    return pl.pallas_call(
        body, name=name, grid=(nblk,),
        in_specs=[pl.BlockSpec((tm, zc), lambda i: (i, 0)), prev(3), prev(4), pl.BlockSpec((BF16_ROWS, da), lambda i: (nxt(i), 2)),
                  pl.BlockSpec((2, tm, da), lambda i: (0, i, 0)), pl.BlockSpec((None, BF16_ROWS, da), lambda i: (1, nxt(i), 0)),
                  _row((nh, CHUNK, CHUNK)), _row((nh, CHUNK, HEAD)), _row((nh, HEAD)), _row((nh, HEAD)), _row((CONV_TAPS, da))],
        out_specs=[pl.BlockSpec((tm, zc), lambda i: (i, 0)), _row((nh, CHUNK, CHUNK)), _row((nh, CHUNK, HEAD)), _row((nh, HEAD)),
                   _row((nh, HEAD)), _row((8, da)), _row((1, zc))],
        out_shape=[jax.ShapeDtypeStruct((s, zc), BF16), jax.ShapeDtypeStruct((nh, CHUNK, CHUNK), F32),
                   jax.ShapeDtypeStruct((nh, CHUNK, HEAD), F32), jax.ShapeDtypeStruct((nh, HEAD), F32),
                   jax.ShapeDtypeStruct((nh, HEAD), F32), jax.ShapeDtypeStruct((8, da), F32), jax.ShapeDtypeStruct((1, zc), F32)],
        compiler_params=_params("arbitrary"),
    )(z, z, z, z, dy, dy, ws_masked, bs_wide, gv, bv, cw)


def _adamw(gparts, w, m, v, name):
    n, r, c = gparts.shape
    tr = r // 4 if (r // 4) % BF16_ROWS == 0 else r

    def body(g_ref, w_ref, m_ref, v_ref, go_ref, d_ref, mo_ref, vo_ref):
        g = g_ref[0].astype(F32)
        for q in range(1, n):
            g = g + g_ref[q].astype(F32)
        m_new = ADAM_B1 * m_ref[...] + (1.0 - ADAM_B1) * g
        v_new = ADAM_B2 * v_ref[...] + (1.0 - ADAM_B2) * (g * g)
        m_hat = m_new / (1.0 - ADAM_B1 ** ADAM_STEP)
        v_hat = v_new / (1.0 - ADAM_B2 ** ADAM_STEP)
        go_ref[...] = g
        d_ref[...] = -ADAM_LR * (m_hat / (jnp.sqrt(v_hat) + ADAM_EPS) + ADAM_WD * w_ref[...])
        mo_ref[...] = m_new
        vo_ref[...] = v_new

    blk = pl.BlockSpec((tr, c), lambda i: (i, 0))
    shp = jax.ShapeDtypeStruct((r, c), F32)
    return pl.pallas_call(
        body, name=name, grid=(r // tr,), in_specs=[pl.BlockSpec((n, tr, c), lambda i: (0, i, 0)), blk, blk, blk],
        out_specs=[blk] * 4, out_shape=[shp] * 4, compiler_params=_params("parallel"),
    )(gparts, w, m, v)


def _pair_sum(g8, t4, core, name):
    _, r, c = g8.shape
    tr = r // 4 if (r // 4) % BF16_ROWS == 0 else r

    def body(core_ref, g_ref, t_ref, o_ref):
        o_ref[...] = (g_ref[...].astype(F32) + t_ref[...].astype(F32)).astype(BF16)

    return pl.pallas_call(
        body, name=name,
        grid_spec=pltpu.PrefetchScalarGridSpec(
            num_scalar_prefetch=1, grid=(N_CHIP, r // tr),
            in_specs=[pl.BlockSpec((None, tr, c), lambda q, i, core_ref: (2 * q + core_ref[0], i, 0)),
                      pl.BlockSpec((None, tr, c), lambda q, i, core_ref: (q, i, 0))],
            out_specs=pl.BlockSpec((None, tr, c), lambda q, i, core_ref: (q, i, 0))),
        out_shape=jax.ShapeDtypeStruct((N_CHIP, r, c), BF16), compiler_params=_params("parallel", "parallel"),
    )(core, g8, t4)


ANY = pl.BlockSpec(memory_space=pl.ANY)


def _place():
    x, y, c = lax.axis_index("x"), lax.axis_index("y"), lax.axis_index("c")
    return x, y, c, [(1 - x, y), (x, 1 - y), (1 - x, 1 - y)]


def _all_gather(shards, name):
    n = len(shards)

    def body(*refs):
        ins, outs = refs[:n], refs[n:2 * n]
        send_sems, recv_sems, local_sems = refs[2 * n:]
        x, y, c, chips = _place()
        me = 4 * x + 2 * y + c
        sibling = (x, y, 1 - c)

        def copy(a, k, block, to, src=None):
            dst = outs[a].at[block]
            return pltpu.make_async_remote_copy(src_ref=dst if src is None else src, dst_ref=dst, send_sem=send_sems.at[a, k],
                                                recv_sem=recv_sems.at[a, k], device_id=to, device_id_type=MESH)

        mine = [pltpu.make_async_copy(ins[a], outs[a].at[me], local_sems.at[a]) for a in range(n)]
        for cp in mine:
            cp.start()
        started = []
        for a in range(n):
            first = [copy(a, 0, me, sibling, src=ins[a])]
            first += [copy(a, 1 + j, me, (*chip, c), src=ins[a]) for j, chip in enumerate(chips)]
            for cp in first:
                cp.start()
            started += first
        for a in range(n):
            for j, (px, py) in enumerate(chips):
                block = 4 * px + 2 * py + c
                copy(a, 1 + j, block, sibling).wait_recv()
                passed = copy(a, 4 + j, block, sibling)
                passed.start()
                started.append(passed)
        for a in range(n):
            copy(a, 0, 4 * x + 2 * y + 1 - c, sibling).wait_recv()
            for j, (px, py) in enumerate(chips):
                copy(a, 4 + j, 4 * px + 2 * py + 1 - c, sibling).wait_recv()
        for cp in started:
            cp.wait_send()
        for cp in mine:
            cp.wait()

    return pl.pallas_call(
        body, name=name, in_specs=[ANY] * n, out_specs=[ANY] * n,
        out_shape=[jax.ShapeDtypeStruct((N_DEV,) + a.shape, a.dtype) for a in shards],
        scratch_shapes=[pltpu.SemaphoreType.DMA((n, 7)), pltpu.SemaphoreType.DMA((n, 7)), pltpu.SemaphoreType.DMA((n,))],
    )(*shards)


def _exchange_with_sibling(grads, name):
    n = len(grads)

    def body(*refs):
        ins, outs = refs[:n], refs[n:2 * n]
        send_sems, recv_sems = refs[2 * n:]
        x, y, c, _ = _place()
        copies = []
        for a in range(n):
            for q in range(N_CHIP):
                cp = pltpu.make_async_remote_copy(src_ref=ins[a].at[2 * q + 1 - c], dst_ref=outs[a].at[q], send_sem=send_sems.at[a, q],
                                                  recv_sem=recv_sems.at[a, q], device_id=(x, y, 1 - c), device_id_type=MESH)
                cp.start()
                copies.append(cp)
        for cp in copies:
            cp.wait()

    return pl.pallas_call(
        body, name=name, in_specs=[ANY] * n, out_specs=[ANY] * n,
        out_shape=[jax.ShapeDtypeStruct((N_CHIP,) + g.shape[1:], g.dtype) for g in grads],
        scratch_shapes=[pltpu.SemaphoreType.DMA((n, N_CHIP)), pltpu.SemaphoreType.DMA((n, N_CHIP))],
    )(*grads)


def _exchange_between_chips(sums, name):
    n = len(sums)

    def body(*refs):
        ins, outs = refs[:n], refs[n:2 * n]
        send_sems, recv_sems, local_sems = refs[2 * n:]
        x, y, c, chips = _place()
        my_chip = 2 * x + y
        mine = [pltpu.make_async_copy(ins[a].at[my_chip], outs[a].at[my_chip], local_sems.at[a]) for a in range(n)]
        for cp in mine:
            cp.start()
        copies = []
        for a in range(n):
            for j, (px, py) in enumerate(chips):
                cp = pltpu.make_async_remote_copy(src_ref=ins[a].at[2 * px + py], dst_ref=outs[a].at[my_chip], send_sem=send_sems.at[a, j],
                                                  recv_sem=recv_sems.at[a, j], device_id=(px, py, c), device_id_type=MESH)
                cp.start()
                copies.append(cp)
        for cp in copies:
            cp.wait()
        for cp in mine:
            cp.wait()

    return pl.pallas_call(
        body, name=name, in_specs=[ANY] * n, out_specs=[ANY] * n,
        out_shape=[jax.ShapeDtypeStruct(g.shape, g.dtype) for g in sums],
        scratch_shapes=[pltpu.SemaphoreType.DMA((n, 3)), pltpu.SemaphoreType.DMA((n, 3)), pltpu.SemaphoreType.DMA((n,))],
    )(*sums)


def _all_reduce_small(pack, name):
    p, lanes = pack.shape

    def body(in_ref, out_ref, gath_ref, send_sems, recv_sems):
        x, y, c, _ = _place()
        me = 4 * x + 2 * y + c
        gath_ref[me] = in_ref[...]
        flips = [(bx, by, bc) for bx in (0, 1) for by in (0, 1) for bc in (0, 1)][1:]
        peers = [(1 - x if bx else x, 1 - y if by else y, 1 - c if bc else c) for bx, by, bc in flips]
        for k, peer in enumerate(peers):
            pltpu.make_async_remote_copy(src_ref=in_ref, dst_ref=gath_ref.at[me], send_sem=send_sems.at[k], recv_sem=recv_sems.at[k],
                                         device_id=peer, device_id_type=MESH).start()
        for k, (px, py, pc) in enumerate(peers):
            pltpu.make_async_remote_copy(src_ref=in_ref, dst_ref=gath_ref.at[4 * px + 2 * py + pc], send_sem=send_sems.at[k],
                                         recv_sem=recv_sems.at[k], device_id=(px, py, pc), device_id_type=MESH).wait()
        total = gath_ref[0]
        for j in range(1, N_DEV):
            total = total + gath_ref[j]
        out_ref[...] = total

    vmem = pl.BlockSpec(memory_space=pltpu.VMEM)
    return pl.pallas_call(
        body, name=name, in_specs=[vmem], out_specs=vmem, out_shape=jax.ShapeDtypeStruct((p, lanes), F32),
        scratch_shapes=[pltpu.VMEM((N_DEV, p, lanes), F32), pltpu.SemaphoreType.DMA((7,)), pltpu.SemaphoreType.DMA((7,))],
        compiler_params=pltpu.CompilerParams(vmem_limit_bytes=V7X_VMEM_LIMIT),
    )(pack)


def _rows128(a):
    return a.reshape(-1, LANES)


def kernel(x, ffa_gate, ffa_up, ffa_down, ln_a_g, ln_a_b, w_in, b_in, w_s, b_s, ln_v_g, ln_v_b, conv_w, w_out, b_out, ln_m_g, ln_m_b, ffc_gate, ffc_up, ffc_down, ln_c_g, ln_c_b, loss_target, m_ffa_gate, m_ffa_up, m_ffa_down, m_ln_a_g, m_ln_a_b, m_w_in, m_b_in, m_w_s, m_b_s, m_ln_v_g, m_ln_v_b, m_conv_w, m_w_out, m_b_out, m_ln_m_g, m_ln_m_b, m_ffc_gate, m_ffc_up, m_ffc_down, m_ln_c_g, m_ln_c_b, v_ffa_gate, v_ffa_up, v_ffa_down, v_ln_a_g, v_ln_a_b, v_w_in, v_b_in, v_w_s, v_b_s, v_ln_v_g, v_ln_v_b, v_conv_w, v_w_out, v_b_out, v_ln_m_g, v_ln_m_b, v_ffc_gate, v_ffc_up, v_ffc_down, v_ln_c_g, v_ln_c_b):
    x2, target = x[0], loss_target[0]
    s, d = x2.shape
    da = d // 2
    nh = da // HEAD
    core = lax.axis_index("c").astype(jnp.int32).reshape(1)
    me = 4 * lax.axis_index("x") + 2 * lax.axis_index("y") + lax.axis_index("c")

    big = dict(ffa_gate=ffa_gate, ffa_up=ffa_up, ffa_down=ffa_down, w_in=w_in, w_out=w_out, ffc_gate=ffc_gate, ffc_up=ffc_up, ffc_down=ffc_down)
    big_m = dict(ffa_gate=m_ffa_gate, ffa_up=m_ffa_up, ffa_down=m_ffa_down, w_in=m_w_in, w_out=m_w_out, ffc_gate=m_ffc_gate, ffc_up=m_ffc_up, ffc_down=m_ffc_down)
    big_v = dict(ffa_gate=v_ffa_gate, ffa_up=v_ffa_up, ffa_down=v_ffa_down, w_in=v_w_in, w_out=v_w_out, ffc_gate=v_ffc_gate, ffc_up=v_ffc_up, ffc_down=v_ffc_down)
    names = list(big)
    shards = [big[k][0].astype(BF16) for k in names] + [jnp.pad(conv_w[0], ((0, 8 - CONV_TAPS), (0, 0)))]
    full = dict(zip(names + ["conv_w"], _all_gather(shards, "all_gather_weights")))
    cw = jnp.transpose(full["conv_w"][:, :CONV_TAPS, :], (1, 0, 2)).reshape(CONV_TAPS, da)
    w_out2 = full["w_out"].reshape(2, da, d)

    tril = jnp.tril(jnp.ones((CHUNK, CHUNK), dtype=bool))
    ws_masked = jnp.where(tril[None], w_s[0], 0.0).astype(BF16)
    bs_wide = jnp.broadcast_to(b_s[0][:, :, None], (nh, CHUNK, HEAD))
    gv, bv = ln_v_g.reshape(nh, HEAD), ln_v_b.reshape(nh, HEAD)

    xb = x2.astype(BF16)
    g_a, u_a, hid_a = _ffn_gateup(xb, full["ffa_gate"], full["ffa_up"], "ffa_gateup")
    xhat1, h1b, rstd1 = _down_ln(hid_a, full["ffa_down"], None, x2, None, ln_a_g, ln_a_b, 0.5, "ffa_down_ln")
    z = _proj_in(h1b, full["w_in"], b_in, "proj_in")
    ycat = _mixer_fwd(z, ws_masked, bs_wide, gv, bv, cw, "mixer_fwd")
    xhat2, h2b, rstd2 = _down_ln(ycat, w_out2, b_out, xhat1, (ln_a_g, ln_a_b), ln_m_g, ln_m_b, 1.0, "proj_out_ln")
    g_c, u_c, hid_c = _ffn_gateup(h2b, full["ffc_gate"], full["ffc_up"], "ffc_gateup")
    dr3, dr3b, sq_err, d_ln_c_g, d_ln_c_b = _down_ln(hid_c, full["ffc_down"], None, xhat2, (ln_m_g, ln_m_b), ln_c_g, ln_c_b, 0.5,
                                                     "ffc_down_ln_loss", target=target)
    loss = lax.psum((0.5 / d) * jnp.sum(sq_err), ("x", "y", "c"))

    tm = min(s, 512)
    dg_c, du_c = _nt_hidden(dr3b, full["ffc_down"], (g_c, u_c), 0.5, "ffc_bwd_hidden")
    gw = {}
    gw["ffc_down"] = _dw_hidden_rows(hid_c, dr3b, 0.5, "ffc_dw_down")
    gw["ffc_gate"] = _dw_hidden_cols(h2b, dg_c, "ffc_dw_gate")
    gw["ffc_up"] = _dw_hidden_cols(h2b, du_c, "ffc_dw_up")
    dr2, dr2b, d_ln_m_g, d_ln_m_b, d_b_out = _nt_acc(_hidden_pairs(dg_c, full["ffc_gate"], du_c, full["ffc_up"], tm), N_DEV, dr3,
                                                     (xhat2, rstd2, ln_m_g), "ffc_bwd_input_ln")
    dycat = _nt_hidden(dr2b, w_out2, None, 1.0, "proj_out_bwd")
    gw["w_out"] = _dw_hidden_rows(ycat, dr2b, 1.0, "proj_out_dw").reshape(N_DEV, da * 2 // N_DEV, d)
    dz, d_w_s, d_b_s_wide, d_gv, d_bv, d_cw, d_b_in = _mixer_bwd(z, dycat, ws_masked, bs_wide, gv, bv, cw, "mixer_bwd")
    cs = w_in.shape[2]
    dz_pairs = [(dz, pl.BlockSpec((tm, cs), lambda i, k: (i, k)), full["w_in"], pl.BlockSpec((None, d, cs), lambda i, k: (k, 0, 0)))]
    dr1, dr1b, d_ln_a_g, d_ln_a_b, _ = _nt_acc(dz_pairs, N_DEV, dr2, (xhat1, rstd1, ln_a_g), "proj_in_bwd_ln")
    gw["w_in"] = _dw_cols(h1b, dz, N_DEV, "proj_in_dw")
    dg_a, du_a = _nt_hidden(dr1b, full["ffa_down"], (g_a, u_a), 0.5, "ffa_bwd_hidden")
    gw["ffa_down"] = _dw_hidden_rows(hid_a, dr1b, 0.5, "ffa_dw_down")
    gw["ffa_gate"] = _dw_hidden_cols(xb, dg_a, "ffa_dw_gate")
    gw["ffa_up"] = _dw_hidden_cols(xb, du_a, "ffa_dw_up")
    grad_x = _nt_acc(_hidden_pairs(dg_a, full["ffa_gate"], du_a, full["ffa_up"], tm), N_DEV, dr1, None, "ffa_bwd_input")

    from_sibling = _exchange_with_sibling([gw[k] for k in names], "grads_to_sibling")
    chip_sums = [_pair_sum(gw[k], t, core, "pair_sum_" + k) for k, t in zip(names, from_sibling)]
    from_chips = _exchange_between_chips(chip_sums, "grads_between_chips")
    grads, deltas, new_m, new_v = {}, {}, {}, {}
    for k, parts in zip(names, from_chips):
        shape = big[k].shape
        out = _adamw(parts, big[k][0], big_m[k][0], big_v[k][0], "adamw_" + k)
        grads[k], deltas[k], new_m[k], new_v[k] = (o.reshape(shape) for o in out)

    small = dict(ln_a_g=ln_a_g, ln_a_b=ln_a_b, b_in=b_in, w_s=w_s, b_s=b_s, ln_v_g=ln_v_g, ln_v_b=ln_v_b, b_out=b_out,
                 ln_m_g=ln_m_g, ln_m_b=ln_m_b, ln_c_g=ln_c_g, ln_c_b=ln_c_b)
    small_m = dict(ln_a_g=m_ln_a_g, ln_a_b=m_ln_a_b, b_in=m_b_in, w_s=m_w_s, b_s=m_b_s, ln_v_g=m_ln_v_g, ln_v_b=m_ln_v_b, b_out=m_b_out,
                   ln_m_g=m_ln_m_g, ln_m_b=m_ln_m_b, ln_c_g=m_ln_c_g, ln_c_b=m_ln_c_b)
    small_v = dict(ln_a_g=v_ln_a_g, ln_a_b=v_ln_a_b, b_in=v_b_in, w_s=v_w_s, b_s=v_b_s, ln_v_g=v_ln_v_g, ln_v_b=v_ln_v_b, b_out=v_b_out,
                   ln_m_g=v_ln_m_g, ln_m_b=v_ln_m_b, ln_c_g=v_ln_c_g, ln_c_b=v_ln_c_b)
    small_g = dict(ln_a_g=d_ln_a_g, ln_a_b=d_ln_a_b, b_in=d_b_in, w_s=d_w_s, b_s=d_b_s_wide[:, :, 0], ln_v_g=d_gv, ln_v_b=d_bv, b_out=d_b_out,
                   ln_m_g=d_ln_m_g, ln_m_b=d_ln_m_b, ln_c_g=d_ln_c_g, ln_c_b=d_ln_c_b)
    snames = list(small)
    rows = [_rows128(small[k]).shape[0] for k in snames]
    conv_rows = CONV_TAPS * da // LANES
    pack = jnp.concatenate([_rows128(small_g[k]) for k in snames] + [_rows128(d_cw[:CONV_TAPS])], axis=0)
    total = _all_reduce_small(pack, "all_reduce_small")
    n_rep = sum(rows)
    conv_g = lax.dynamic_slice_in_dim(total[n_rep:n_rep + conv_rows].reshape(CONV_TAPS, da), me * conv_w.shape[2], conv_w.shape[2], axis=1)
    pad_conv = lambda a: jnp.pad(a, ((0, 8 - CONV_TAPS), (0, 0)))
    stack = lambda tree, conv: jnp.concatenate([_rows128(tree[k]) for k in snames] + [pad_conv(conv)], axis=0)
    out = _adamw(jnp.concatenate([total[:n_rep], pad_conv(conv_g)], axis=0)[None], stack(small, conv_w[0]), stack(small_m, m_conv_w[0]),
                 stack(small_v, v_conv_w[0]), "adamw_small")
    for tree, packed in zip((grads, deltas, new_m, new_v), out):
        at = 0
        for k, r in zip(snames, rows):
            tree[k] = packed[at:at + r].reshape(small[k].shape)
            at += r
        tree["conv_w"] = packed[at:at + CONV_TAPS].reshape(conv_w.shape)

    order = ["ffa_gate", "ffa_up", "ffa_down", "ln_a_g", "ln_a_b", "w_in", "b_in", "w_s", "b_s", "ln_v_g", "ln_v_b", "conv_w", "w_out", "b_out",
             "ln_m_g", "ln_m_b", "ffc_gate", "ffc_up", "ffc_down", "ln_c_g", "ln_c_b"]
    return (loss, grad_x[None], *[grads[k] for k in order], *[deltas[k] for k in order], *[new_m[k] for k in order], *[new_v[k] for k in order])
```

```python
import math

import jax
import jax.numpy as jnp
from jax import lax
from jax.experimental import pallas as pl
from jax.experimental.pallas import tpu as pltpu

BF16 = jnp.bfloat16
F32 = jnp.float32
MESH = pl.DeviceIdType.MESH

N_DEV = 8
HEAD = 128
CHUNK = 128
CONV_TAPS = 3
LN_EPS = 1e-5
ALPHA = float(2 ** 0.25)
GELU_C = 0.7978845608028654
GELU_A = 0.044715
ADAM_LR, ADAM_B1, ADAM_B2, ADAM_EPS, ADAM_WD, ADAM_STEP = 0.001, 0.9, 0.999, 1e-08, 0.01, 10
V7X_VMEM_LIMIT = 56 * 1024 * 1024
LANES = 128
BF16_ROWS = 16

NT_DIMS = (((1,), (1,)), ((), ()))
TN_DIMS = (((0,), (0,)), ((), ()))
ANY = pl.BlockSpec(memory_space=pl.ANY)


def _gelu(x):
    return 0.5 * x * (1.0 + jnp.tanh(GELU_C * (x + GELU_A * x * x * x)))


def _gelu_grad(x):
    t = jnp.tanh(GELU_C * (x + GELU_A * x * x * x))
    return 0.5 * (1.0 + t) + 0.5 * x * (1.0 - t * t) * GELU_C * (1.0 + 3.0 * GELU_A * x * x)


def _sigmoid(x):
    return 1.0 / (1.0 + jnp.exp(-x))


def _row(shape):
    return pl.BlockSpec(shape, lambda *_: (0,) * len(shape))


def _row_blocks(tm, rows=128):
    rows = min(rows, tm)
    return [slice(r, r + rows) for r in range(0, tm, rows)]


def _ln_backward(dh, xhat, rstd, gain):
    dxh = dh * gain
    m1 = jnp.mean(dxh, axis=-1, keepdims=True)
    m2 = jnp.mean(dxh * xhat, axis=-1, keepdims=True)
    return rstd * (dxh - m1 - xhat * m2)


def _place():
    x, y, c = lax.axis_index("x"), lax.axis_index("y"), lax.axis_index("c")
    return x, y, c, [(1 - x, y), (x, 1 - y), (1 - x, 1 - y)]


def _other_devices(x, y, c):
    flips = [(bx, by, bc) for bx in (0, 1) for by in (0, 1) for bc in (0, 1)][1:]
    return [(1 - x if bx else x, 1 - y if by else y, 1 - c if bc else c) for bx, by, bc in flips]


class _Gather:
    def __init__(self, shards, forward_at=0.75):
        n = len(shards)
        self.n, self.forward_at = n, forward_at
        self.inputs = list(shards)
        self.out_shapes = [jax.ShapeDtypeStruct((N_DEV,) + a.shape, a.dtype) for a in shards]
        self.scratch = [pltpu.SemaphoreType.DMA((n, 7)), pltpu.SemaphoreType.DMA((n, 7)), pltpu.SemaphoreType.DMA((n,))]

    def _copy(self, outs, sems, a, k, block, to, src=None):
        dst = outs[a].at[block]
        return pltpu.make_async_remote_copy(src_ref=dst if src is None else src, dst_ref=dst, send_sem=sems[0].at[a, k],
                                            recv_sem=sems[1].at[a, k], device_id=to, device_id_type=MESH)

    def start(self, ins, outs, sems):
        x, y, c, chips = _place()
        me = 4 * x + 2 * y + c
        for a in range(self.n):
            pltpu.make_async_copy(ins[a], outs[a].at[me], sems[2].at[a]).start()
        for a in range(self.n):
            self._copy(outs, sems, a, 0, me, (x, y, 1 - c), src=ins[a]).start()
            for j, chip in enumerate(chips):
                self._copy(outs, sems, a, 1 + j, me, (*chip, c), src=ins[a]).start()

    def forward(self, ins, outs, sems):
        x, y, c, chips = _place()
        for a in range(self.n):
            for j, (px, py) in enumerate(chips):
                block = 4 * px + 2 * py + c
                self._copy(outs, sems, a, 1 + j, block, (x, y, 1 - c)).wait_recv()
                self._copy(outs, sems, a, 4 + j, block, (x, y, 1 - c)).start()

    def finish(self, ins, outs, sems):
        x, y, c, chips = _place()
        me = 4 * x + 2 * y + c
        sibling = (x, y, 1 - c)
        for a in range(self.n):
            self._copy(outs, sems, a, 0, 4 * x + 2 * y + 1 - c, sibling).wait_recv()
            for j, (px, py) in enumerate(chips):
                self._copy(outs, sems, a, 4 + j, 4 * px + 2 * py + 1 - c, sibling).wait_recv()
        for a in range(self.n):
            for k in range(7):
                self._copy(outs, sems, a, k, me, sibling, src=ins[a]).wait_send()
            pltpu.make_async_copy(ins[a], outs[a].at[me], sems[2].at[a]).wait()

    def before(self, step, n_steps, ins, outs, sems):
        pl.when(step == 0)(lambda: self.start(ins, outs, sems))
        pl.when(step == int(self.forward_at * (n_steps - 1)))(lambda: self.forward(ins, outs, sems))

    def after(self, step, n_steps, ins, outs, sems):
        pl.when(step == n_steps - 1)(lambda: self.finish(ins, outs, sems))


class _Scatter:
    def __init__(self, partial):
        self.inputs = [partial]
        self.out_shapes = [jax.ShapeDtypeStruct(partial.shape, partial.dtype)]
        self.scratch = [pltpu.SemaphoreType.DMA((7,)), pltpu.SemaphoreType.DMA((7,)), pltpu.SemaphoreType.DMA((1,))]

    def _copies(self, ins, outs, sems):
        x, y, c, _ = _place()
        me = 4 * x + 2 * y + c
        mine = pltpu.make_async_copy(ins[0].at[me], outs[0].at[me], sems[2].at[0])
        remote = [pltpu.make_async_remote_copy(src_ref=ins[0].at[4 * px + 2 * py + pc], dst_ref=outs[0].at[me], send_sem=sems[0].at[k],
                                               recv_sem=sems[1].at[k], device_id=(px, py, pc), device_id_type=MESH)
                  for k, (px, py, pc) in enumerate(_other_devices(x, y, c))]
        return mine, remote

    def start(self, ins, outs, sems):
        mine, remote = self._copies(ins, outs, sems)
        mine.start()
        for cp in remote:
            cp.start()

    def finish(self, ins, outs, sems):
        mine, remote = self._copies(ins, outs, sems)
        for cp in remote:
            cp.wait()
        mine.wait()

    def before(self, step, n_steps, ins, outs, sems):
        pl.when(step == 0)(lambda: self.start(ins, outs, sems))

    def after(self, step, n_steps, ins, outs, sems):
        pl.when(step == n_steps - 1)(lambda: self.finish(ins, outs, sems))


def _call(body, name, grid, in_specs, out_specs, out_shape, scratch, ins, rider=None):
    single = not isinstance(out_shape, (list, tuple))
    out_shape = [out_shape] if single else list(out_shape)
    out_specs = [out_specs] if single else list(out_specs)
    params = pltpu.CompilerParams(dimension_semantics=("arbitrary",) * len(grid), vmem_limit_bytes=V7X_VMEM_LIMIT)
    if rider is None:
        outs = pl.pallas_call(body, name=name, grid=grid, in_specs=in_specs, out_specs=out_specs, out_shape=out_shape,
                              scratch_shapes=scratch, compiler_params=params)(*ins)
        return (outs[0] if single else outs), None
    n_in, n_out, n_scr = len(ins), len(out_shape), len(scratch)
    r_in, r_out = len(rider.inputs), len(rider.out_shapes)
    n_steps = math.prod(grid)

    def carried(*refs):
        refs = list(refs)
        cut = lambda n: [refs.pop(0) for _ in range(n)]
        b_in, c_in, b_out, c_out, b_scr = cut(n_in), cut(r_in), cut(n_out), cut(r_out), cut(n_scr)
        step = 0
        for axis, size in enumerate(grid):
            step = step * size + pl.program_id(axis)
        rider.before(step, n_steps, c_in, c_out, refs)
        body(*b_in, *b_out, *b_scr)
        rider.after(step, n_steps, c_in, c_out, refs)

    outs = pl.pallas_call(
        carried, name=name, grid=grid, in_specs=list(in_specs) + [ANY] * r_in, out_specs=out_specs + [ANY] * r_out,
        out_shape=out_shape + rider.out_shapes, scratch_shapes=list(scratch) + rider.scratch, compiler_params=params,
    )(*ins, *rider.inputs)
    base = outs[:n_out]
    return (base[0] if single else base), outs[n_out:]


def _all_gather(shards, name):
    g = _Gather(shards)

    def body(*refs):
        ins, outs, sems = refs[:g.n], refs[g.n:2 * g.n], refs[2 * g.n:]
        g.start(ins, outs, sems)
        g.forward(ins, outs, sems)
        g.finish(ins, outs, sems)

    return pl.pallas_call(body, name=name, in_specs=[ANY] * g.n, out_specs=[ANY] * g.n, out_shape=g.out_shapes,
                          scratch_shapes=g.scratch)(*shards)


def _all_reduce_small(pack, name):
    p, lanes = pack.shape

    def body(in_ref, out_ref, gath_ref, send_sems, recv_sems):
        x, y, c, _ = _place()
        me = 4 * x + 2 * y + c
        gath_ref[me] = in_ref[...]
        peers = _other_devices(x, y, c)
        for k, peer in enumerate(peers):
            pltpu.make_async_remote_copy(src_ref=in_ref, dst_ref=gath_ref.at[me], send_sem=send_sems.at[k], recv_sem=recv_sems.at[k],
                                         device_id=peer, device_id_type=MESH).start()
        for k, (px, py, pc) in enumerate(peers):
            pltpu.make_async_remote_copy(src_ref=in_ref, dst_ref=gath_ref.at[4 * px + 2 * py + pc], send_sem=send_sems.at[k],
                                         recv_sem=recv_sems.at[k], device_id=(px, py, pc), device_id_type=MESH).wait()
        total = gath_ref[0]
        for j in range(1, N_DEV):
            total = total + gath_ref[j]
        out_ref[...] = total

    vmem = pl.BlockSpec(memory_space=pltpu.VMEM)
    return pl.pallas_call(
        body, name=name, in_specs=[vmem], out_specs=vmem, out_shape=jax.ShapeDtypeStruct((p, lanes), F32),
        scratch_shapes=[pltpu.VMEM((N_DEV, p, lanes), F32), pltpu.SemaphoreType.DMA((7,)), pltpu.SemaphoreType.DMA((7,))],
        compiler_params=pltpu.CompilerParams(vmem_limit_bytes=V7X_VMEM_LIMIT),
    )(pack)


def _ffn_gateup(xb, wg, wu, name, rider=None):
    s, d = xb.shape
    nsh, _, fs = wg.shape
    tm = min(s, 1024)

    def body(x_ref, wg_ref, wu_ref, g_ref, u_ref, h_ref):
        x = x_ref[...]
        g = jnp.dot(x, wg_ref[...], preferred_element_type=F32)
        u = jnp.dot(x, wu_ref[...], preferred_element_type=F32)
        g_ref[...] = g.astype(BF16)
        u_ref[...] = u.astype(BF16)
        h_ref[...] = (g * _sigmoid(g) * u).astype(BF16)

    shp = jax.ShapeDtypeStruct((nsh, s, fs), BF16)
    w_spec = pl.BlockSpec((None, d, fs), lambda i, j: (j, 0, 0))
    o_spec = pl.BlockSpec((None, tm, fs), lambda i, j: (j, i, 0))
    return _call(body, name, (s // tm, nsh), [pl.BlockSpec((tm, d), lambda i, j: (i, 0)), w_spec, w_spec],
                 [o_spec, o_spec, o_spec], [shp, shp, shp], [], [xb, wg, wu], rider)


def _down_ln(a3, w3, bias, res, res_affine, ln_g, ln_b, scale, name, target=None, rider=None):
    nk, s, tk = a3.shape
    d = w3.shape[2]
    tm = min(s, 512)
    final = target is not None

    def body(*refs):
        refs = list(refs)
        a_ref, w_ref = refs[:2]
        del refs[:2]
        bias_ref = refs.pop(0) if bias is not None else None
        res_ref = refs.pop(0)
        rg_ref, rb_ref = (refs.pop(0), refs.pop(0)) if res_affine is not None else (None, None)
        g_ref, b_ref = refs.pop(0), refs.pop(0)
        t_ref = refs.pop(0) if final else None
        acc_ref = refs.pop()
        i, k = pl.program_id(0), pl.program_id(1)

        @pl.when(k == 0)
        def _():
            acc_ref[...] = jnp.zeros_like(acc_ref)

        acc_ref[...] += jnp.dot(a_ref[...], w_ref[...], preferred_element_type=F32)

        @pl.when(k == nk - 1)
        def _():
            if final:
                dr_ref, drb_ref, sq_ref, dg_ref, db_ref = refs

                @pl.when(i == 0)
                def _():
                    sq_ref[...] = jnp.zeros_like(sq_ref)
                    dg_ref[...] = jnp.zeros_like(dg_ref)
                    db_ref[...] = jnp.zeros_like(db_ref)
            else:
                xh_ref, hb_ref, rstd_ref = refs

            for rows in _row_blocks(tm):
                r = res_ref[rows, :]
                if rg_ref is not None:
                    r = r * rg_ref[...] + rb_ref[...]
                y = acc_ref[rows, :]
                if bias_ref is not None:
                    y = y + bias_ref[...]
                r = ALPHA * r + scale * y
                mu = jnp.mean(r, axis=-1, keepdims=True)
                c = r - mu
                var = jnp.mean(c * c, axis=-1, keepdims=True)
                rstd = lax.rsqrt(var + LN_EPS)
                xhat = c * rstd
                h = xhat * g_ref[...] + b_ref[...]
                if not final:
                    xh_ref[rows, :] = xhat
                    hb_ref[rows, :] = h.astype(BF16)
                    rstd_ref[rows, :] = rstd
                else:
                    err = h - t_ref[rows, :]
                    sq_ref[...] += jnp.sum(err * err, axis=0, keepdims=True)
                    dh = err * (1.0 / d)
                    dg_ref[...] += jnp.sum(dh * xhat, axis=0, keepdims=True)
                    db_ref[...] += jnp.sum(dh, axis=0, keepdims=True)
                    dr = _ln_backward(dh, xhat, rstd, g_ref[...])
                    dr_ref[rows, :] = dr
                    drb_ref[rows, :] = dr.astype(BF16)

    tok = pl.BlockSpec((tm, d), lambda i, k: (i, 0))
    vec = pl.BlockSpec((1, d), lambda i, k: (0, 0))
    ins = [a3, w3]
    in_specs = [pl.BlockSpec((None, tm, tk), lambda i, k: (k, i, 0)), pl.BlockSpec((None, tk, d), lambda i, k: (k, 0, 0))]
    if bias is not None:
        ins.append(bias)
        in_specs.append(vec)
    ins.append(res)
    in_specs.append(tok)
    if res_affine is not None:
        ins += list(res_affine)
        in_specs += [vec, vec]
    ins += [ln_g, ln_b]
    in_specs += [vec, vec]
    if final:
        ins.append(target)
        in_specs.append(tok)
        out_shape = [jax.ShapeDtypeStruct((s, d), F32), jax.ShapeDtypeStruct((s, d), BF16)] + [jax.ShapeDtypeStruct((1, d), F32)] * 3
        out_specs = [tok, tok, vec, vec, vec]
    else:
        out_shape = [jax.ShapeDtypeStruct((s, d), F32), jax.ShapeDtypeStruct((s, d), BF16), jax.ShapeDtypeStruct((s, 1), F32)]
        out_specs = [tok, tok, pl.BlockSpec((tm, 1), lambda i, k: (i, 0))]
    return _call(body, name, (s // tm, nk), in_specs, out_specs, out_shape, [pltpu.VMEM((tm, d), F32)], ins, rider)


def _proj_in(hb, w3, bias, name):
    s, d = hb.shape
    nsh, _, cs = w3.shape
    tm = min(s, 1024)

    def body(h_ref, w_ref, b_ref, z_ref):
        z_ref[...] = (jnp.dot(h_ref[...], w_ref[...], preferred_element_type=F32) + b_ref[...]).astype(BF16)

    in_specs = [pl.BlockSpec((tm, d), lambda i, j: (i, 0)), pl.BlockSpec((None, d, cs), lambda i, j: (j, 0, 0)),
                pl.BlockSpec((1, cs), lambda i, j: (0, j))]
    return _call(body, name, (s // tm, nsh), in_specs, pl.BlockSpec((tm, cs), lambda i, j: (i, j)),
                 jax.ShapeDtypeStruct((s, nsh * cs), BF16), [], [hb, w3, bias])[0]


def _nt_hidden(ab, w3, gate_up, scale, name, rider=None):
    s, kdim = ab.shape
    nj, tn, _ = w3.shape
    tm = min(s, 1024)

    def body(*refs):
        a_ref, w_ref = refs[:2]
        t = scale * lax.dot_general(a_ref[...], w_ref[...], NT_DIMS, preferred_element_type=F32)
        if gate_up is None:
            refs[2][...] = t.astype(BF16)
        else:
            g_ref, u_ref, dg_ref, du_ref = refs[2:]
            g = g_ref[...].astype(F32)
            sg = _sigmoid(g)
            du_ref[...] = (t * g * sg).astype(BF16)
            dg_ref[...] = (t * u_ref[...].astype(F32) * (sg * (1.0 + g * (1.0 - sg)))).astype(BF16)

    hid = pl.BlockSpec((None, tm, tn), lambda i, j: (j, i, 0))
    shp = jax.ShapeDtypeStruct((nj, s, tn), BF16)
    ins = [ab, w3]
    in_specs = [pl.BlockSpec((tm, kdim), lambda i, j: (i, 0)), pl.BlockSpec((None, tn, kdim), lambda i, j: (j, 0, 0))]
    if gate_up is None:
        out_specs, out_shape = hid, shp
    else:
        ins += list(gate_up)
        in_specs += [hid, hid]
        out_specs, out_shape = [hid, hid], [shp, shp]
    return _call(body, name, (s // tm, nj), in_specs, out_specs, out_shape, [], ins, rider)


def _tn_dw(a, a_spec, b, b_spec, nj, m, n, s, tk, scale, name, rider):
    def body(a_ref, b_ref, o_ref, acc_ref):
        k = pl.program_id(1)

        @pl.when(k == 0)
        def _():
            acc_ref[...] = jnp.zeros_like(acc_ref)

        acc_ref[...] += lax.dot_general(a_ref[...], b_ref[...], TN_DIMS, preferred_element_type=F32)

        @pl.when(k == s // tk - 1)
        def _():
            o_ref[...] = (scale * acc_ref[...]).astype(BF16)

    return _call(body, name, (nj, s // tk), [a_spec, b_spec], pl.BlockSpec((None, m, n), lambda j, k: (j, 0, 0)),
                 jax.ShapeDtypeStruct((nj, m, n), BF16), [pltpu.VMEM((m, n), F32)], [a, b], rider)


def _dw_hidden_rows(hid3, db, scale, name, rider=None):
    nj, s, fs = hid3.shape
    d = db.shape[1]
    tk = min(s, 512)
    return _tn_dw(hid3, pl.BlockSpec((None, tk, fs), lambda j, k: (j, k, 0)), db, pl.BlockSpec((tk, d), lambda j, k: (k, 0)),
                  nj, fs, d, s, tk, scale, name, rider)


def _dw_hidden_cols(ab, hid3, name, rider=None):
    nj, s, fs = hid3.shape
    d = ab.shape[1]
    tk = min(s, 512)
    return _tn_dw(ab, pl.BlockSpec((tk, d), lambda j, k: (k, 0)), hid3, pl.BlockSpec((None, tk, fs), lambda j, k: (j, k, 0)),
                  nj, d, fs, s, tk, 1.0, name, rider)


def _dw_cols(ab, dz, nj, name, rider=None):
    s, d = ab.shape
    cs = dz.shape[1] // nj
    tk = min(s, 512)
    return _tn_dw(ab, pl.BlockSpec((tk, d), lambda j, k: (k, 0)), dz, pl.BlockSpec((tk, cs), lambda j, k: (k, j)),
                  nj, d, cs, s, tk, 1.0, name, rider)


def _nt_acc(pairs, nk, dres, ln, name, rider=None):
    s, d = dres.shape
    tm = min(s, 512)
    np_ = len(pairs)

    def body(*refs):
        refs = list(refs)
        pair_refs = [(refs[2 * p], refs[2 * p + 1]) for p in range(np_)]
        del refs[:2 * np_]
        dres_ref = refs.pop(0)
        if ln is not None:
            xh_ref, rstd_ref, gain_ref = refs.pop(0), refs.pop(0), refs.pop(0)
        acc_ref = refs.pop()
        i, k = pl.program_id(0), pl.program_id(1)

        @pl.when(k == 0)
        def _():
            acc_ref[...] = jnp.zeros_like(acc_ref)

        for a_ref, w_ref in pair_refs:
            acc_ref[...] += lax.dot_general(a_ref[...], w_ref[...], NT_DIMS, preferred_element_type=F32)

        @pl.when(k == nk - 1)
        def _():
            if ln is not None:
                dr_ref, drb_ref, dg_ref, db_ref, sum_ref = refs

                @pl.when(i == 0)
                def _():
                    dg_ref[...] = jnp.zeros_like(dg_ref)
                    db_ref[...] = jnp.zeros_like(db_ref)
                    sum_ref[...] = jnp.zeros_like(sum_ref)

            for rows in _row_blocks(tm):
                dh = ALPHA * dres_ref[rows, :] + acc_ref[rows, :]
                if ln is None:
                    refs[0][rows, :] = dh
                else:
                    xhat = xh_ref[rows, :]
                    dg_ref[...] += jnp.sum(dh * xhat, axis=0, keepdims=True)
                    db_ref[...] += jnp.sum(dh, axis=0, keepdims=True)
                    dr = _ln_backward(dh, xhat, rstd_ref[rows, :], gain_ref[...])
                    sum_ref[...] += jnp.sum(dr, axis=0, keepdims=True)
                    dr_ref[rows, :] = dr
                    drb_ref[rows, :] = dr.astype(BF16)

    tok = pl.BlockSpec((tm, d), lambda i, k: (i, 0))
    vec = pl.BlockSpec((1, d), lambda i, k: (0, 0))
    ins, in_specs = [], []
    for a, a_spec, w, w_spec in pairs:
        ins += [a, w]
        in_specs += [a_spec, w_spec]
    ins.append(dres)
    in_specs.append(tok)
    if ln is None:
        out_shape, out_specs = jax.ShapeDtypeStruct((s, d), F32), tok
    else:
        ins += list(ln)
        in_specs += [tok, pl.BlockSpec((tm, 1), lambda i, k: (i, 0)), vec]
        out_shape = [jax.ShapeDtypeStruct((s, d), F32), jax.ShapeDtypeStruct((s, d), BF16)] + [jax.ShapeDtypeStruct((1, d), F32)] * 3
        out_specs = [tok, tok, vec, vec, vec]
    return _call(body, name, (s // tm, nk), in_specs, out_specs, out_shape, [pltpu.VMEM((tm, d), F32)], ins, rider)


def _hidden_pairs(dg3, wg3, du3, wu3, tm):
    nj, s, fs = dg3.shape
    d = wg3.shape[1]
    a_spec = pl.BlockSpec((None, tm, fs), lambda i, k: (k, i, 0))
    w_spec = pl.BlockSpec((None, d, fs), lambda i, k: (k, 0, 0))
    return [(dg3, a_spec, wg3, w_spec), (du3, a_spec, wu3, w_spec)]


def _shift_rows_down(v, halo, k, row):
    out = pltpu.roll(v, k, 0)
    hr = halo.shape[0]
    for r in range(k):
        out = jnp.where(row == r, halo[hr - k + r:hr - k + r + 1, :], out)
    return out


def _shift_rows_up(v, halo, k, row):
    t = v.shape[0]
    out = pltpu.roll(v, t - k, 0)
    for r in range(k):
        out = jnp.where(row == t - k + r, halo[r:r + 1, :], out)
    return out


def _sgu_head_forward(z_ref, h, da, gv_ref, bv_ref):
    zu = z_ref[:, h * HEAD:(h + 1) * HEAD].astype(F32)
    zv = z_ref[:, da + h * HEAD:da + (h + 1) * HEAD].astype(F32)
    u = _gelu(zu)
    v = _gelu(zv)
    mu = jnp.mean(v, axis=-1, keepdims=True)
    c = v - mu
    rstd = lax.rsqrt(jnp.mean(c * c, axis=-1, keepdims=True) + LN_EPS)
    vhat = c * rstd
    vln = (vhat * gv_ref[h:h + 1, :] + bv_ref[h:h + 1, :]).astype(BF16)
    return zu, zv, u, vhat, rstd, vln


def _mixer_fwd(z, ws_masked, bs_wide, gv, bv, cw, name):
    s, zc = z.shape
    da = zc // 5
    nh = da // HEAD
    tm = min(s, 512)
    hb = tm // BF16_ROWS

    def body(z_ref, pc_ref, px_ref, ws_ref, bs_ref, gv_ref, bv_ref, cw_ref, y_ref):
        i = pl.program_id(0)
        for h in range(nh):
            _, _, u, _, _, vln = _sgu_head_forward(z_ref, h, da, gv_ref, bv_ref)
            for n in range(tm // CHUNK):
                rows = slice(n * CHUNK, (n + 1) * CHUNK)
                mixed = jnp.dot(ws_ref[h], vln[rows], preferred_element_type=F32) + bs_ref[h]
                y_ref[0, rows, h * HEAD:(h + 1) * HEAD] = (u[rows] * mixed).astype(BF16)
        gate_b = z_ref[:, 2 * da:3 * da].astype(F32)
        hc = z_ref[:, 3 * da:4 * da].astype(F32) * z_ref[:, 4 * da:5 * da].astype(F32)
        halo = jnp.where(i > 0, pc_ref[...].astype(F32) * px_ref[...].astype(F32), 0.0)
        row = lax.broadcasted_iota(jnp.int32, (tm, da), 0)
        y = cw_ref[0:1, :] * _shift_rows_down(hc, halo, 2, row) + cw_ref[1:2, :] * _shift_rows_down(hc, halo, 1, row) + cw_ref[2:3, :] * hc
        y_ref[1] = (gate_b * y).astype(BF16)

    prev = lambda col: pl.BlockSpec((BF16_ROWS, da), lambda i: (jnp.maximum(i * hb - 1, 0), col))
    in_specs = [pl.BlockSpec((tm, zc), lambda i: (i, 0)), prev(3), prev(4), _row((nh, CHUNK, CHUNK)), _row((nh, CHUNK, HEAD)),
                _row((nh, HEAD)), _row((nh, HEAD)), _row((CONV_TAPS, da))]
    return _call(body, name, (s // tm,), in_specs, pl.BlockSpec((2, tm, da), lambda i: (0, i, 0)),
                 jax.ShapeDtypeStruct((2, s, da), BF16), [], [z, z, z, ws_masked, bs_wide, gv, bv, cw])[0]


def _mixer_bwd(z, dy, ws_masked, bs_wide, gv, bv, cw, name, rider=None):
    s, zc = z.shape
    da = zc // 5
    nh = da // HEAD
    tm = min(s, 512)
    hb = tm // BF16_ROWS
    nblk = s // tm

    def body(z_ref, pc_ref, px_ref, nb_ref, dy_ref, ndy_ref, ws_ref, bs_ref, gv_ref, bv_ref, cw_ref,
             dz_ref, dws_ref, dbs_ref, dgv_ref, dbv_ref, dcw_ref, dbin_ref):
        i = pl.program_id(0)

        @pl.when(i == 0)
        def _():
            for ref in (dws_ref, dbs_ref, dgv_ref, dbv_ref, dcw_ref, dbin_ref):
                ref[...] = jnp.zeros_like(ref)

        causal = lax.broadcasted_iota(jnp.int32, (CHUNK, CHUNK), 0) >= lax.broadcasted_iota(jnp.int32, (CHUNK, CHUNK), 1)
        for h in range(nh):
            zu, zv, u, vhat, rstd, vln = _sgu_head_forward(z_ref, h, da, gv_ref, bv_ref)
            dya = dy_ref[0, :, h * HEAD:(h + 1) * HEAD].astype(F32)
            w = ws_ref[h]
            du_parts, dvln_parts = [], []
            for n in range(tm // CHUNK):
                rows = slice(n * CHUNK, (n + 1) * CHUNK)
                mixed = jnp.dot(w, vln[rows], preferred_element_type=F32) + bs_ref[h]
                du_parts.append(dya[rows] * mixed)
                dmix = dya[rows] * u[rows]
                dmix_b = dmix.astype(BF16)
                dws_ref[h] += jnp.where(causal, lax.dot_general(dmix_b, vln[rows], NT_DIMS, preferred_element_type=F32), 0.0)
                dbs_ref[h] += dmix
                dvln_parts.append(lax.dot_general(w, dmix_b, TN_DIMS, preferred_element_type=F32))
            du = jnp.concatenate(du_parts, axis=0)
            dvln = jnp.concatenate(dvln_parts, axis=0)
            dgv_ref[h:h + 1, :] += jnp.sum(dvln * vhat, axis=0, keepdims=True)
            dbv_ref[h:h + 1, :] += jnp.sum(dvln, axis=0, keepdims=True)
            dv = _ln_backward(dvln, vhat, rstd, gv_ref[h:h + 1, :])
            dzu = du * _gelu_grad(zu)
            dzv = dv * _gelu_grad(zv)
            ucols = slice(h * HEAD, (h + 1) * HEAD)
            vcols = slice(da + h * HEAD, da + (h + 1) * HEAD)
            dz_ref[:, ucols] = dzu.astype(BF16)
            dz_ref[:, vcols] = dzv.astype(BF16)
            dbin_ref[:, ucols] += jnp.sum(dzu, axis=0, keepdims=True)
            dbin_ref[:, vcols] += jnp.sum(dzv, axis=0, keepdims=True)

        gate_b = z_ref[:, 2 * da:3 * da].astype(F32)
        gate_c = z_ref[:, 3 * da:4 * da].astype(F32)
        xt = z_ref[:, 4 * da:5 * da].astype(F32)
        hc = gate_c * xt
        halo = jnp.where(i > 0, pc_ref[...].astype(F32) * px_ref[...].astype(F32), 0.0)
        row = lax.broadcasted_iota(jnp.int32, (tm, da), 0)
        sh1 = _shift_rows_down(hc, halo, 1, row)
        sh2 = _shift_rows_down(hc, halo, 2, row)
        y = cw_ref[0:1, :] * sh2 + cw_ref[1:2, :] * sh1 + cw_ref[2:3, :] * hc
        dyb = dy_ref[1].astype(F32)
        dconv = dyb * gate_b
        nhalo = jnp.where(i < nblk - 1, ndy_ref[...].astype(F32) * nb_ref[...].astype(F32), 0.0)
        dhc = cw_ref[2:3, :] * dconv + cw_ref[1:2, :] * _shift_rows_up(dconv, nhalo, 1, row) + cw_ref[0:1, :] * _shift_rows_up(dconv, nhalo, 2, row)
        dcw_ref[0:1, :] += jnp.sum(dconv * sh2, axis=0, keepdims=True)
        dcw_ref[1:2, :] += jnp.sum(dconv * sh1, axis=0, keepdims=True)
        dcw_ref[2:3, :] += jnp.sum(dconv * hc, axis=0, keepdims=True)
        for col, val in ((2, dyb * y), (3, dhc * xt), (4, dhc * gate_c)):
            cols = slice(col * da, (col + 1) * da)
            dz_ref[:, cols] = val.astype(BF16)
            dbin_ref[:, cols] += jnp.sum(val, axis=0, keepdims=True)

        @pl.when(i == nblk - 1)
        def _():
            for h in range(nh):
                dbs_ref[h] = jnp.broadcast_to(jnp.sum(dbs_ref[h], axis=1, keepdims=True), (CHUNK, HEAD))

    prev = lambda col: pl.BlockSpec((BF16_ROWS, da), lambda i: (jnp.maximum(i * hb - 1, 0), col))
    nxt = lambda i: jnp.minimum((i + 1) * hb, s // BF16_ROWS - 1)
    in_specs = [pl.BlockSpec((tm, zc), lambda i: (i, 0)), prev(3), prev(4), pl.BlockSpec((BF16_ROWS, da), lambda i: (nxt(i), 2)),
                pl.BlockSpec((2, tm, da), lambda i: (0, i, 0)), pl.BlockSpec((None, BF16_ROWS, da), lambda i: (1, nxt(i), 0)),
                _row((nh, CHUNK, CHUNK)), _row((nh, CHUNK, HEAD)), _row((nh, HEAD)), _row((nh, HEAD)), _row((CONV_TAPS, da))]
    out_specs = [pl.BlockSpec((tm, zc), lambda i: (i, 0)), _row((nh, CHUNK, CHUNK)), _row((nh, CHUNK, HEAD)), _row((nh, HEAD)),
                 _row((nh, HEAD)), _row((8, da)), _row((1, zc))]
    out_shape = [jax.ShapeDtypeStruct((s, zc), BF16), jax.ShapeDtypeStruct((nh, CHUNK, CHUNK), F32),
                 jax.ShapeDtypeStruct((nh, CHUNK, HEAD), F32), jax.ShapeDtypeStruct((nh, HEAD), F32),
                 jax.ShapeDtypeStruct((nh, HEAD), F32), jax.ShapeDtypeStruct((8, da), F32), jax.ShapeDtypeStruct((1, zc), F32)]
    return _call(body, name, (nblk,), in_specs, out_specs, out_shape, [], [z, z, z, z, dy, dy, ws_masked, bs_wide, gv, bv, cw], rider)


def _adamw(gparts, w, m, v, name):
    n, r, c = gparts.shape
    tr = r // 4 if (r // 4) % BF16_ROWS == 0 else r

    def body(g_ref, w_ref, m_ref, v_ref, go_ref, d_ref, mo_ref, vo_ref):
        g = g_ref[0].astype(F32)
        for q in range(1, n):
            g = g + g_ref[q].astype(F32)
        m_new = ADAM_B1 * m_ref[...] + (1.0 - ADAM_B1) * g
        v_new = ADAM_B2 * v_ref[...] + (1.0 - ADAM_B2) * (g * g)
        m_hat = m_new / (1.0 - ADAM_B1 ** ADAM_STEP)
        v_hat = v_new / (1.0 - ADAM_B2 ** ADAM_STEP)
        go_ref[...] = g
        d_ref[...] = -ADAM_LR * (m_hat / (jnp.sqrt(v_hat) + ADAM_EPS) + ADAM_WD * w_ref[...])
        mo_ref[...] = m_new
        vo_ref[...] = v_new

    blk = pl.BlockSpec((tr, c), lambda i: (i, 0))
    shp = jax.ShapeDtypeStruct((r, c), F32)
    return _call(body, name, (r // tr,), [pl.BlockSpec((n, tr, c), lambda i: (0, i, 0)), blk, blk, blk], [blk] * 4, [shp] * 4, [],
                 [gparts, w, m, v])[0]


def _rows128(a):
    return a.reshape(-1, LANES)


def kernel(x, ffa_gate, ffa_up, ffa_down, ln_a_g, ln_a_b, w_in, b_in, w_s, b_s, ln_v_g, ln_v_b, conv_w, w_out, b_out, ln_m_g, ln_m_b, ffc_gate, ffc_up, ffc_down, ln_c_g, ln_c_b, loss_target, m_ffa_gate, m_ffa_up, m_ffa_down, m_ln_a_g, m_ln_a_b, m_w_in, m_b_in, m_w_s, m_b_s, m_ln_v_g, m_ln_v_b, m_conv_w, m_w_out, m_b_out, m_ln_m_g, m_ln_m_b, m_ffc_gate, m_ffc_up, m_ffc_down, m_ln_c_g, m_ln_c_b, v_ffa_gate, v_ffa_up, v_ffa_down, v_ln_a_g, v_ln_a_b, v_w_in, v_b_in, v_w_s, v_b_s, v_ln_v_g, v_ln_v_b, v_conv_w, v_w_out, v_b_out, v_ln_m_g, v_ln_m_b, v_ffc_gate, v_ffc_up, v_ffc_down, v_ln_c_g, v_ln_c_b):
    x2, target = x[0], loss_target[0]
    s, d = x2.shape
    da = d // 2
    nh = da // HEAD
    me = 4 * lax.axis_index("x") + 2 * lax.axis_index("y") + lax.axis_index("c")

    big = dict(ffa_gate=ffa_gate, ffa_up=ffa_up, ffa_down=ffa_down, w_in=w_in, w_out=w_out, ffc_gate=ffc_gate, ffc_up=ffc_up, ffc_down=ffc_down)
    big_m = dict(ffa_gate=m_ffa_gate, ffa_up=m_ffa_up, ffa_down=m_ffa_down, w_in=m_w_in, w_out=m_w_out, ffc_gate=m_ffc_gate, ffc_up=m_ffc_up, ffc_down=m_ffc_down)
    big_v = dict(ffa_gate=v_ffa_gate, ffa_up=v_ffa_up, ffa_down=v_ffa_down, w_in=v_w_in, w_out=v_w_out, ffc_gate=v_ffc_gate, ffc_up=v_ffc_up, ffc_down=v_ffc_down)
    shard = {k: w[0].astype(BF16) for k, w in big.items()}
    conv_rows = jnp.pad(conv_w[0], ((0, 8 - CONV_TAPS), (0, 0)))

    tril = jnp.tril(jnp.ones((CHUNK, CHUNK), dtype=bool))
    ws_masked = jnp.where(tril[None], w_s[0], 0.0).astype(BF16)
    bs_wide = jnp.broadcast_to(b_s[0][:, :, None], (nh, CHUNK, HEAD))
    gv, bv = ln_v_g.reshape(nh, HEAD), ln_v_b.reshape(nh, HEAD)

    full = {}
    full["ffa_gate"], full["ffa_up"], full["ffa_down"] = _all_gather([shard["ffa_gate"], shard["ffa_up"], shard["ffa_down"]], "gather_ffa")
    xb = x2.astype(BF16)
    (g_a, u_a, hid_a), (full["w_in"], full["w_out"], conv_full, full["ffc_gate"]) = _ffn_gateup(
        xb, full["ffa_gate"], full["ffa_up"], "ffa_gateup", _Gather([shard["w_in"], shard["w_out"], conv_rows, shard["ffc_gate"]]))
    (xhat1, h1b, rstd1), (full["ffc_up"], full["ffc_down"]) = _down_ln(
        hid_a, full["ffa_down"], None, x2, None, ln_a_g, ln_a_b, 0.5, "ffa_down_ln", rider=_Gather([shard["ffc_up"], shard["ffc_down"]]))
    cw = jnp.transpose(conv_full[:, :CONV_TAPS, :], (1, 0, 2)).reshape(CONV_TAPS, da)
    w_out2 = full["w_out"].reshape(2, da, d)
    z = _proj_in(h1b, full["w_in"], b_in, "proj_in")
    ycat = _mixer_fwd(z, ws_masked, bs_wide, gv, bv, cw, "mixer_fwd")
    (xhat2, h2b, rstd2), _ = _down_ln(ycat, w_out2, b_out, xhat1, (ln_a_g, ln_a_b), ln_m_g, ln_m_b, 1.0, "proj_out_ln")
    (g_c, u_c, hid_c), _ = _ffn_gateup(h2b, full["ffc_gate"], full["ffc_up"], "ffc_gateup")
    (dr3, dr3b, sq_err, d_ln_c_g, d_ln_c_b), _ = _down_ln(hid_c, full["ffc_down"], None, xhat2, (ln_m_g, ln_m_b), ln_c_g, ln_c_b, 0.5,
                                                          "ffc_down_ln_loss", target=target)
    loss = lax.psum((0.5 / d) * jnp.sum(sq_err), ("x", "y", "c"))

    tm = min(s, 512)
    landed = {}
    (dg_c, du_c), _ = _nt_hidden(dr3b, full["ffc_down"], (g_c, u_c), 0.5, "ffc_bwd_hidden")
    part, _ = _dw_hidden_rows(hid_c, dr3b, 0.5, "ffc_dw_down")
    part, (landed["ffc_down"],) = _dw_hidden_cols(h2b, dg_c, "ffc_dw_gate", _Scatter(part))
    part, (landed["ffc_gate"],) = _dw_hidden_cols(h2b, du_c, "ffc_dw_up", _Scatter(part))
    (dr2, dr2b, d_ln_m_g, d_ln_m_b, d_b_out), (landed["ffc_up"],) = _nt_acc(
        _hidden_pairs(dg_c, full["ffc_gate"], du_c, full["ffc_up"], tm), N_DEV, dr3, (xhat2, rstd2, ln_m_g), "ffc_bwd_input_ln", _Scatter(part))
    dycat, _ = _nt_hidden(dr2b, w_out2, None, 1.0, "proj_out_bwd")
    part, _ = _dw_hidden_rows(ycat, dr2b, 1.0, "proj_out_dw")
    (dz, d_w_s, d_b_s_wide, d_gv, d_bv, d_cw, d_b_in), (landed["w_out"],) = _mixer_bwd(
        z, dycat, ws_masked, bs_wide, gv, bv, cw, "mixer_bwd", _Scatter(part.reshape(N_DEV, d // N_DEV, d)))
    cs = w_in.shape[2]
    dz_pairs = [(dz, pl.BlockSpec((tm, cs), lambda i, k: (i, k)), full["w_in"], pl.BlockSpec((None, d, cs), lambda i, k: (k, 0, 0)))]
    (dr1, dr1b, d_ln_a_g, d_ln_a_b, _), _ = _nt_acc(dz_pairs, N_DEV, dr2, (xhat1, rstd1, ln_a_g), "proj_in_bwd_ln")
    part, _ = _dw_cols(h1b, dz, N_DEV, "proj_in_dw")
    (dg_a, du_a), (landed["w_in"],) = _nt_hidden(dr1b, full["ffa_down"], (g_a, u_a), 0.5, "ffa_bwd_hidden", _Scatter(part))
    part, _ = _dw_hidden_rows(hid_a, dr1b, 0.5, "ffa_dw_down")
    part, (landed["ffa_down"],) = _dw_hidden_cols(xb, dg_a, "ffa_dw_gate", _Scatter(part))
    part, (landed["ffa_gate"],) = _dw_hidden_cols(xb, du_a, "ffa_dw_up", _Scatter(part))
    grad_x, (landed["ffa_up"],) = _nt_acc(_hidden_pairs(dg_a, full["ffa_gate"], du_a, full["ffa_up"], tm), N_DEV, dr1, None, "ffa_bwd_input",
                                          _Scatter(part))

    grads, deltas, new_m, new_v = {}, {}, {}, {}
    for k in big:
        out = _adamw(landed[k], big[k][0], big_m[k][0], big_v[k][0], "adamw_" + k)
        grads[k], deltas[k], new_m[k], new_v[k] = (o.reshape(big[k].shape) for o in out)

    small = dict(ln_a_g=ln_a_g, ln_a_b=ln_a_b, b_in=b_in, w_s=w_s, b_s=b_s, ln_v_g=ln_v_g, ln_v_b=ln_v_b, b_out=b_out,
                 ln_m_g=ln_m_g, ln_m_b=ln_m_b, ln_c_g=ln_c_g, ln_c_b=ln_c_b)
    small_m = dict(ln_a_g=m_ln_a_g, ln_a_b=m_ln_a_b, b_in=m_b_in, w_s=m_w_s, b_s=m_b_s, ln_v_g=m_ln_v_g, ln_v_b=m_ln_v_b, b_out=m_b_out,
                   ln_m_g=m_ln_m_g, ln_m_b=m_ln_m_b, ln_c_g=m_ln_c_g, ln_c_b=m_ln_c_b)
    small_v = dict(ln_a_g=v_ln_a_g, ln_a_b=v_ln_a_b, b_in=v_b_in, w_s=v_w_s, b_s=v_b_s, ln_v_g=v_ln_v_g, ln_v_b=v_ln_v_b, b_out=v_b_out,
                   ln_m_g=v_ln_m_g, ln_m_b=v_ln_m_b, ln_c_g=v_ln_c_g, ln_c_b=v_ln_c_b)
    small_g = dict(ln_a_g=d_ln_a_g, ln_a_b=d_ln_a_b, b_in=d_b_in, w_s=d_w_s, b_s=d_b_s_wide[:, :, 0], ln_v_g=d_gv, ln_v_b=d_bv, b_out=d_b_out,
                   ln_m_g=d_ln_m_g, ln_m_b=d_ln_m_b, ln_c_g=d_ln_c_g, ln_c_b=d_ln_c_b)
    snames = list(small)
    rows = [_rows128(small[k]).shape[0] for k in snames]
    conv_rows_n = CONV_TAPS * da // LANES
    pack = jnp.concatenate([_rows128(small_g[k]) for k in snames] + [_rows128(d_cw[:CONV_TAPS])], axis=0)
    total = _all_reduce_small(pack, "all_reduce_small")
    n_rep = sum(rows)
    conv_g = lax.dynamic_slice_in_dim(total[n_rep:n_rep + conv_rows_n].reshape(CONV_TAPS, da), me * conv_w.shape[2], conv_w.shape[2], axis=1)
    pad_conv = lambda a: jnp.pad(a, ((0, 8 - CONV_TAPS), (0, 0)))
    stack = lambda tree, conv: jnp.concatenate([_rows128(tree[k]) for k in snames] + [pad_conv(conv)], axis=0)
    out = _adamw(jnp.concatenate([total[:n_rep], pad_conv(conv_g)], axis=0)[None], stack(small, conv_w[0]), stack(small_m, m_conv_w[0]),
                 stack(small_v, v_conv_w[0]), "adamw_small")
    for tree, packed in zip((grads, deltas, new_m, new_v), out):
        at = 0
        for k, r in zip(snames, rows):
            tree[k] = packed[at:at + r].reshape(small[k].shape)
            at += r
        tree["conv_w"] = packed[at:at + CONV_TAPS].reshape(conv_w.shape)

    order = ["ffa_gate", "ffa_up", "ffa_down", "ln_a_g", "ln_a_b", "w_in", "b_in", "w_s", "b_s", "ln_v_g", "ln_v_b", "conv_w", "w_out", "b_out",
             "ln_m_g", "ln_m_b", "ffc_gate", "ffc_up", "ffc_down", "ln_c_g", "ln_c_b"]
    return (loss, grad_x[None], *[grads[k] for k in order], *[deltas[k] for k in order], *[new_m[k] for k in order], *[new_v[k] for k in order])
```

```python
import math

import jax
import jax.numpy as jnp
from jax import lax
from jax.experimental import pallas as pl
from jax.experimental.pallas import tpu as pltpu

BF16 = jnp.bfloat16
F32 = jnp.float32
MESH = pl.DeviceIdType.MESH

N_DEV = 8
HEAD = 128
CHUNK = 128
CONV_TAPS = 3
LN_EPS = 1e-5
ALPHA = float(2 ** 0.25)
GELU_C = 0.7978845608028654
GELU_A = 0.044715
ADAM_LR, ADAM_B1, ADAM_B2, ADAM_EPS, ADAM_WD, ADAM_STEP = 0.001, 0.9, 0.999, 1e-08, 0.01, 10
V7X_VMEM_LIMIT = 56 * 1024 * 1024
LANES = 128
BF16_ROWS = 16
DW_TOKENS = 2048

NT_DIMS = (((1,), (1,)), ((), ()))
TN_DIMS = (((0,), (0,)), ((), ()))
ANY = pl.BlockSpec(memory_space=pl.ANY)


def _gelu(x):
    return 0.5 * x * (1.0 + jnp.tanh(GELU_C * (x + GELU_A * x * x * x)))


def _gelu_grad(x):
    t = jnp.tanh(GELU_C * (x + GELU_A * x * x * x))
    return 0.5 * (1.0 + t) + 0.5 * x * (1.0 - t * t) * GELU_C * (1.0 + 3.0 * GELU_A * x * x)


def _sigmoid(x):
    return 1.0 / (1.0 + jnp.exp(-x))


def _row(shape):
    return pl.BlockSpec(shape, lambda *_: (0,) * len(shape))


def _row_blocks(tm, rows=128):
    rows = min(rows, tm)
    return [slice(r, r + rows) for r in range(0, tm, rows)]


def _ln_backward(dh, xhat, rstd, gain):
    dxh = dh * gain
    m1 = jnp.mean(dxh, axis=-1, keepdims=True)
    m2 = jnp.mean(dxh * xhat, axis=-1, keepdims=True)
    return rstd * (dxh - m1 - xhat * m2)


def _place():
    x, y, c = lax.axis_index("x"), lax.axis_index("y"), lax.axis_index("c")
    return x, y, c, [(1 - x, y), (x, 1 - y), (1 - x, 1 - y)]


def _other_devices(x, y, c):
    flips = [(bx, by, bc) for bx in (0, 1) for by in (0, 1) for bc in (0, 1)][1:]
    return [(1 - x if bx else x, 1 - y if by else y, 1 - c if bc else c) for bx, by, bc in flips]


class _Gather:
    def __init__(self, shards, forward_at=0.75):
        n = len(shards)
        self.n, self.forward_at = n, forward_at
        self.inputs = list(shards)
        self.out_shapes = [jax.ShapeDtypeStruct((N_DEV,) + a.shape, a.dtype) for a in shards]
        self.scratch = [pltpu.SemaphoreType.DMA((n, 7)), pltpu.SemaphoreType.DMA((n, 7)), pltpu.SemaphoreType.DMA((n,))]

    def _copy(self, outs, sems, a, k, block, to, src=None):
        dst = outs[a].at[block]
        return pltpu.make_async_remote_copy(src_ref=dst if src is None else src, dst_ref=dst, send_sem=sems[0].at[a, k],
                                            recv_sem=sems[1].at[a, k], device_id=to, device_id_type=MESH)

    def start(self, ins, outs, sems):
        x, y, c, chips = _place()
        me = 4 * x + 2 * y + c
        for a in range(self.n):
            pltpu.make_async_copy(ins[a], outs[a].at[me], sems[2].at[a]).start()
        for a in range(self.n):
            self._copy(outs, sems, a, 0, me, (x, y, 1 - c), src=ins[a]).start()
            for j, chip in enumerate(chips):
                self._copy(outs, sems, a, 1 + j, me, (*chip, c), src=ins[a]).start()

    def forward(self, ins, outs, sems):
        x, y, c, chips = _place()
        for a in range(self.n):
            for j, (px, py) in enumerate(chips):
                block = 4 * px + 2 * py + c
                self._copy(outs, sems, a, 1 + j, block, (x, y, 1 - c)).wait_recv()
                self._copy(outs, sems, a, 4 + j, block, (x, y, 1 - c)).start()

    def finish(self, ins, outs, sems):
        x, y, c, chips = _place()
        me = 4 * x + 2 * y + c
        sibling = (x, y, 1 - c)
        for a in range(self.n):
            self._copy(outs, sems, a, 0, 4 * x + 2 * y + 1 - c, sibling).wait_recv()
            for j, (px, py) in enumerate(chips):
                self._copy(outs, sems, a, 4 + j, 4 * px + 2 * py + 1 - c, sibling).wait_recv()
        for a in range(self.n):
            for k in range(7):
                self._copy(outs, sems, a, k, me, sibling, src=ins[a]).wait_send()
            pltpu.make_async_copy(ins[a], outs[a].at[me], sems[2].at[a]).wait()

    def before(self, step, n_steps, ins, outs, sems):
        pl.when(step == 0)(lambda: self.start(ins, outs, sems))
        pl.when(step == int(self.forward_at * (n_steps - 1)))(lambda: self.forward(ins, outs, sems))

    def after(self, step, n_steps, ins, outs, sems):
        pl.when(step == n_steps - 1)(lambda: self.finish(ins, outs, sems))


class _Scatter:
    def __init__(self, partial):
        self.inputs = [partial]
        self.out_shapes = [jax.ShapeDtypeStruct(partial.shape, partial.dtype)]
        self.scratch = [pltpu.SemaphoreType.DMA((7,)), pltpu.SemaphoreType.DMA((7,)), pltpu.SemaphoreType.DMA((1,))]

    def _copies(self, ins, outs, sems):
        x, y, c, _ = _place()
        me = 4 * x + 2 * y + c
        mine = pltpu.make_async_copy(ins[0].at[me], outs[0].at[me], sems[2].at[0])
        remote = [pltpu.make_async_remote_copy(src_ref=ins[0].at[4 * px + 2 * py + pc], dst_ref=outs[0].at[me], send_sem=sems[0].at[k],
                                               recv_sem=sems[1].at[k], device_id=(px, py, pc), device_id_type=MESH)
                  for k, (px, py, pc) in enumerate(_other_devices(x, y, c))]
        return mine, remote

    def start(self, ins, outs, sems):
        mine, remote = self._copies(ins, outs, sems)
        mine.start()
        for cp in remote:
            cp.start()

    def finish(self, ins, outs, sems):
        mine, remote = self._copies(ins, outs, sems)
        for cp in remote:
            cp.wait()
        mine.wait()

    def before(self, step, n_steps, ins, outs, sems):
        pl.when(step == 0)(lambda: self.start(ins, outs, sems))

    def after(self, step, n_steps, ins, outs, sems):
        pl.when(step == n_steps - 1)(lambda: self.finish(ins, outs, sems))


def _call(body, name, grid, in_specs, out_specs, out_shape, scratch, ins, rider=None):
    single = not isinstance(out_shape, (list, tuple))
    out_shape = [out_shape] if single else list(out_shape)
    out_specs = [out_specs] if single else list(out_specs)
    params = pltpu.CompilerParams(dimension_semantics=("arbitrary",) * len(grid), vmem_limit_bytes=V7X_VMEM_LIMIT)
    if rider is None:
        outs = pl.pallas_call(body, name=name, grid=grid, in_specs=in_specs, out_specs=out_specs, out_shape=out_shape,
                              scratch_shapes=scratch, compiler_params=params)(*ins)
        return (outs[0] if single else outs), None
    n_in, n_out, n_scr = len(ins), len(out_shape), len(scratch)
    r_in, r_out = len(rider.inputs), len(rider.out_shapes)
    n_steps = math.prod(grid)

    def carried(*refs):
        refs = list(refs)
        cut = lambda n: [refs.pop(0) for _ in range(n)]
        b_in, c_in, b_out, c_out, b_scr = cut(n_in), cut(r_in), cut(n_out), cut(r_out), cut(n_scr)
        step = 0
        for axis, size in enumerate(grid):
            step = step * size + pl.program_id(axis)
        rider.before(step, n_steps, c_in, c_out, refs)
        body(*b_in, *b_out, *b_scr)
        rider.after(step, n_steps, c_in, c_out, refs)

    outs = pl.pallas_call(
        carried, name=name, grid=grid, in_specs=list(in_specs) + [ANY] * r_in, out_specs=out_specs + [ANY] * r_out,
        out_shape=out_shape + rider.out_shapes, scratch_shapes=list(scratch) + rider.scratch, compiler_params=params,
    )(*ins, *rider.inputs)
    base = outs[:n_out]
    return (base[0] if single else base), outs[n_out:]


def _all_gather(shards, name):
    g = _Gather(shards)

    def body(*refs):
        ins, outs, sems = refs[:g.n], refs[g.n:2 * g.n], refs[2 * g.n:]
        g.start(ins, outs, sems)
        g.forward(ins, outs, sems)
        g.finish(ins, outs, sems)

    return pl.pallas_call(body, name=name, in_specs=[ANY] * g.n, out_specs=[ANY] * g.n, out_shape=g.out_shapes,
                          scratch_shapes=g.scratch)(*shards)


def _all_reduce_small(pack, name):
    p, lanes = pack.shape

    def body(in_ref, out_ref, gath_ref, send_sems, recv_sems):
        x, y, c, _ = _place()
        me = 4 * x + 2 * y + c
        gath_ref[me] = in_ref[...]
        peers = _other_devices(x, y, c)
        for k, peer in enumerate(peers):
            pltpu.make_async_remote_copy(src_ref=in_ref, dst_ref=gath_ref.at[me], send_sem=send_sems.at[k], recv_sem=recv_sems.at[k],
                                         device_id=peer, device_id_type=MESH).start()
        for k, (px, py, pc) in enumerate(peers):
            pltpu.make_async_remote_copy(src_ref=in_ref, dst_ref=gath_ref.at[4 * px + 2 * py + pc], send_sem=send_sems.at[k],
                                         recv_sem=recv_sems.at[k], device_id=(px, py, pc), device_id_type=MESH).wait()
        total = gath_ref[0]
        for j in range(1, N_DEV):
            total = total + gath_ref[j]
        out_ref[...] = total

    vmem = pl.BlockSpec(memory_space=pltpu.VMEM)
    return pl.pallas_call(
        body, name=name, in_specs=[vmem], out_specs=vmem, out_shape=jax.ShapeDtypeStruct((p, lanes), F32),
        scratch_shapes=[pltpu.VMEM((N_DEV, p, lanes), F32), pltpu.SemaphoreType.DMA((7,)), pltpu.SemaphoreType.DMA((7,))],
        compiler_params=pltpu.CompilerParams(vmem_limit_bytes=V7X_VMEM_LIMIT),
    )(pack)


def _ffn_gateup(xb, wg, wu, name, rider=None):
    s, d = xb.shape
    nsh, _, fs = wg.shape
    tm = min(s, 1024)

    def body(x_ref, wg_ref, wu_ref, g_ref, u_ref, h_ref):
        x = x_ref[...]
        g = jnp.dot(x, wg_ref[...], preferred_element_type=F32)
        u = jnp.dot(x, wu_ref[...], preferred_element_type=F32)
        g_ref[...] = g.astype(BF16)
        u_ref[...] = u.astype(BF16)
        h_ref[...] = (g * _sigmoid(g) * u).astype(BF16)

    shp = jax.ShapeDtypeStruct((nsh, s, fs), BF16)
    w_spec = pl.BlockSpec((None, d, fs), lambda i, j: (j, 0, 0))
    o_spec = pl.BlockSpec((None, tm, fs), lambda i, j: (j, i, 0))
    return _call(body, name, (s // tm, nsh), [pl.BlockSpec((tm, d), lambda i, j: (i, 0)), w_spec, w_spec],
                 [o_spec, o_spec, o_spec], [shp, shp, shp], [], [xb, wg, wu], rider)


def _down_ln(a3, w3, bias, res, res_affine, ln_g, ln_b, scale, name, target=None, rider=None):
    nk, s, tk = a3.shape
    d = w3.shape[2]
    tm = min(s, 256)
    final = target is not None

    def body(*refs):
        refs = list(refs)
        a_ref, w_hbm = refs[:2]
        del refs[:2]
        bias_ref = refs.pop(0) if bias is not None else None
        res_ref = refs.pop(0)
        rg_ref, rb_ref = (refs.pop(0), refs.pop(0)) if res_affine is not None else (None, None)
        g_ref, b_ref = refs.pop(0), refs.pop(0)
        t_ref = refs.pop(0) if final else None
        w_sem = refs.pop()
        w_ref = refs.pop()
        i = pl.program_id(0)
        if final:
            dr_ref, drb_ref, sq_ref, dg_ref, db_ref = refs
        else:
            xh_ref, hb_ref, rstd_ref = refs

        @pl.when(i == 0)
        def _():
            whole = pltpu.make_async_copy(w_hbm, w_ref, w_sem.at[0])
            whole.start()
            whole.wait()
            if final:
                sq_ref[...] = jnp.zeros_like(sq_ref)
                dg_ref[...] = jnp.zeros_like(dg_ref)
                db_ref[...] = jnp.zeros_like(db_ref)

        y = jnp.dot(a_ref[0], w_ref[0], preferred_element_type=F32)
        for k in range(1, nk):
            y = y + jnp.dot(a_ref[k], w_ref[k], preferred_element_type=F32)
        if bias_ref is not None:
            y = y + bias_ref[...]
        for rows in _row_blocks(tm):
            r = res_ref[rows, :]
            if rg_ref is not None:
                r = r * rg_ref[...] + rb_ref[...]
            r = ALPHA * r + scale * y[rows]
            mu = jnp.mean(r, axis=-1, keepdims=True)
            c = r - mu
            var = jnp.mean(c * c, axis=-1, keepdims=True)
            rstd = lax.rsqrt(var + LN_EPS)
            xhat = c * rstd
            h = xhat * g_ref[...] + b_ref[...]
            if not final:
                xh_ref[rows, :] = xhat
                hb_ref[rows, :] = h.astype(BF16)
                rstd_ref[rows, :] = rstd
            else:
                err = h - t_ref[rows, :]
                sq_ref[...] += jnp.sum(err * err, axis=0, keepdims=True)
                dh = err * (1.0 / d)
                dg_ref[...] += jnp.sum(dh * xhat, axis=0, keepdims=True)
                db_ref[...] += jnp.sum(dh, axis=0, keepdims=True)
                dr = _ln_backward(dh, xhat, rstd, g_ref[...])
                dr_ref[rows, :] = dr
                drb_ref[rows, :] = dr.astype(BF16)

    tok = pl.BlockSpec((tm, d), lambda i: (i, 0))
    vec = pl.BlockSpec((1, d), lambda i: (0, 0))
    ins = [a3, w3]
    in_specs = [pl.BlockSpec((nk, tm, tk), lambda i: (0, i, 0)), ANY]
    if bias is not None:
        ins.append(bias)
        in_specs.append(vec)
    ins.append(res)
    in_specs.append(tok)
    if res_affine is not None:
        ins += list(res_affine)
        in_specs += [vec, vec]
    ins += [ln_g, ln_b]
    in_specs += [vec, vec]
    if final:
        ins.append(target)
        in_specs.append(tok)
        out_shape = [jax.ShapeDtypeStruct((s, d), F32), jax.ShapeDtypeStruct((s, d), BF16)] + [jax.ShapeDtypeStruct((1, d), F32)] * 3
        out_specs = [tok, tok, vec, vec, vec]
    else:
        out_shape = [jax.ShapeDtypeStruct((s, d), F32), jax.ShapeDtypeStruct((s, d), BF16), jax.ShapeDtypeStruct((s, 1), F32)]
        out_specs = [tok, tok, pl.BlockSpec((tm, 1), lambda i: (i, 0))]
    scratch = [pltpu.VMEM((nk, tk, d), BF16), pltpu.SemaphoreType.DMA((1,))]
    return _call(body, name, (s // tm,), in_specs, out_specs, out_shape, scratch, ins, rider)


def _proj_in(hb, w3, bias, name):
    s, d = hb.shape
    nsh, _, cs = w3.shape
    tm = min(s, 1024)

    def body(h_ref, w_ref, b_ref, z_ref):
        z_ref[...] = (jnp.dot(h_ref[...], w_ref[...], preferred_element_type=F32) + b_ref[...]).astype(BF16)

    in_specs = [pl.BlockSpec((tm, d), lambda i, j: (i, 0)), pl.BlockSpec((None, d, cs), lambda i, j: (j, 0, 0)),
                pl.BlockSpec((1, cs), lambda i, j: (0, j))]
    return _call(body, name, (s // tm, nsh), in_specs, pl.BlockSpec((tm, cs), lambda i, j: (i, j)),
                 jax.ShapeDtypeStruct((s, nsh * cs), BF16), [], [hb, w3, bias])[0]


def _nt_hidden(ab, w3, gate_up, scale, name, rider=None):
    s, kdim = ab.shape
    nj, tn, _ = w3.shape
    tm = min(s, 1024)

    def body(*refs):
        a_ref, w_ref = refs[:2]
        t = scale * lax.dot_general(a_ref[...], w_ref[...], NT_DIMS, preferred_element_type=F32)
        if gate_up is None:
            refs[2][...] = t.astype(BF16)
        else:
            g_ref, u_ref, dg_ref, du_ref = refs[2:]
            g = g_ref[...].astype(F32)
            sg = _sigmoid(g)
            du_ref[...] = (t * g * sg).astype(BF16)
            dg_ref[...] = (t * u_ref[...].astype(F32) * (sg * (1.0 + g * (1.0 - sg)))).astype(BF16)

    hid = pl.BlockSpec((None, tm, tn), lambda i, j: (j, i, 0))
    shp = jax.ShapeDtypeStruct((nj, s, tn), BF16)
    ins = [ab, w3]
    in_specs = [pl.BlockSpec((tm, kdim), lambda i, j: (i, 0)), pl.BlockSpec((None, tn, kdim), lambda i, j: (j, 0, 0))]
    if gate_up is None:
        out_specs, out_shape = hid, shp
    else:
        ins += list(gate_up)
        in_specs += [hid, hid]
        out_specs, out_shape = [hid, hid], [shp, shp]
    return _call(body, name, (s // tm, nj), in_specs, out_specs, out_shape, [], ins, rider)


def _tn_dw(a, a_spec, b, b_spec, nj, m, n, s, tk, scale, name, rider):
    def body(a_ref, b_ref, o_ref, acc_ref):
        k = pl.program_id(1)

        @pl.when(k == 0)
        def _():
            acc_ref[...] = jnp.zeros_like(acc_ref)

        acc_ref[...] += lax.dot_general(a_ref[...], b_ref[...], TN_DIMS, preferred_element_type=F32)

        @pl.when(k == s // tk - 1)
        def _():
            o_ref[...] = (scale * acc_ref[...]).astype(BF16)

    return _call(body, name, (nj, s // tk), [a_spec, b_spec], pl.BlockSpec((None, m, n), lambda j, k: (j, 0, 0)),
                 jax.ShapeDtypeStruct((nj, m, n), BF16), [pltpu.VMEM((m, n), F32)], [a, b], rider)


def _dw_hidden_rows(hid3, db, scale, name, rider=None):
    nj, s, fs = hid3.shape
    d = db.shape[1]
    tk = min(s, DW_TOKENS)
    return _tn_dw(hid3, pl.BlockSpec((None, tk, fs), lambda j, k: (j, k, 0)), db, pl.BlockSpec((tk, d), lambda j, k: (k, 0)),
                  nj, fs, d, s, tk, scale, name, rider)


def _dw_hidden_cols(ab, hid3, name, rider=None):
    nj, s, fs = hid3.shape
    d = ab.shape[1]
    tk = min(s, DW_TOKENS)
    return _tn_dw(ab, pl.BlockSpec((tk, d), lambda j, k: (k, 0)), hid3, pl.BlockSpec((None, tk, fs), lambda j, k: (j, k, 0)),
                  nj, d, fs, s, tk, 1.0, name, rider)


def _dw_cols(ab, dz, nj, name, rider=None):
    s, d = ab.shape
    cs = dz.shape[1] // nj
    tk = min(s, DW_TOKENS)
    return _tn_dw(ab, pl.BlockSpec((tk, d), lambda j, k: (k, 0)), dz, pl.BlockSpec((tk, cs), lambda j, k: (k, j)),
                  nj, d, cs, s, tk, 1.0, name, rider)


def _nt_acc(pairs, nk, dres, ln, name, rider=None):
    s, d = dres.shape
    tm = min(s, 512)
    np_ = len(pairs)

    def body(*refs):
        refs = list(refs)
        pair_refs = [(refs[2 * p], refs[2 * p + 1]) for p in range(np_)]
        del refs[:2 * np_]
        dres_ref = refs.pop(0)
        if ln is not None:
            xh_ref, rstd_ref, gain_ref = refs.pop(0), refs.pop(0), refs.pop(0)
        acc_ref = refs.pop()
        i, k = pl.program_id(0), pl.program_id(1)

        @pl.when(k == 0)
        def _():
            acc_ref[...] = jnp.zeros_like(acc_ref)

        for a_ref, w_ref in pair_refs:
            acc_ref[...] += lax.dot_general(a_ref[...], w_ref[...], NT_DIMS, preferred_element_type=F32)

        @pl.when(k == nk - 1)
        def _():
            if ln is not None:
                dr_ref, drb_ref, dg_ref, db_ref, sum_ref = refs

                @pl.when(i == 0)
                def _():
                    dg_ref[...] = jnp.zeros_like(dg_ref)
                    db_ref[...] = jnp.zeros_like(db_ref)
                    sum_ref[...] = jnp.zeros_like(sum_ref)

            for rows in _row_blocks(tm):
                dh = ALPHA * dres_ref[rows, :] + acc_ref[rows, :]
                if ln is None:
                    refs[0][rows, :] = dh
                else:
                    xhat = xh_ref[rows, :]
                    dg_ref[...] += jnp.sum(dh * xhat, axis=0, keepdims=True)
                    db_ref[...] += jnp.sum(dh, axis=0, keepdims=True)
                    dr = _ln_backward(dh, xhat, rstd_ref[rows, :], gain_ref[...])
                    sum_ref[...] += jnp.sum(dr, axis=0, keepdims=True)
                    dr_ref[rows, :] = dr
                    drb_ref[rows, :] = dr.astype(BF16)

    tok = pl.BlockSpec((tm, d), lambda i, k: (i, 0))
    vec = pl.BlockSpec((1, d), lambda i, k: (0, 0))
    ins, in_specs = [], []
    for a, a_spec, w, w_spec in pairs:
        ins += [a, w]
        in_specs += [a_spec, w_spec]
    ins.append(dres)
    in_specs.append(tok)
    if ln is None:
        out_shape, out_specs = jax.ShapeDtypeStruct((s, d), F32), tok
    else:
        ins += list(ln)
        in_specs += [tok, pl.BlockSpec((tm, 1), lambda i, k: (i, 0)), vec]
        out_shape = [jax.ShapeDtypeStruct((s, d), F32), jax.ShapeDtypeStruct((s, d), BF16)] + [jax.ShapeDtypeStruct((1, d), F32)] * 3
        out_specs = [tok, tok, vec, vec, vec]
    return _call(body, name, (s // tm, nk), in_specs, out_specs, out_shape, [pltpu.VMEM((tm, d), F32)], ins, rider)


def _proj_in_bwd_ln(dz, w3, dres, ln, name):
    s, d = dres.shape
    nj, _, cs = w3.shape
    tm = min(s, 256)

    def body(dz_ref, w_hbm, dres_ref, xh_ref, rstd_ref, gain_ref, dr_ref, drb_ref, dg_ref, db_ref, w_ref, w_sem):
        @pl.when(pl.program_id(0) == 0)
        def _():
            whole = pltpu.make_async_copy(w_hbm, w_ref, w_sem.at[0])
            whole.start()
            whole.wait()
            dg_ref[...] = jnp.zeros_like(dg_ref)
            db_ref[...] = jnp.zeros_like(db_ref)

        acc = lax.dot_general(dz_ref[:, 0:cs], w_ref[0], NT_DIMS, preferred_element_type=F32)
        for j in range(1, nj):
            acc = acc + lax.dot_general(dz_ref[:, j * cs:(j + 1) * cs], w_ref[j], NT_DIMS, preferred_element_type=F32)
        for rows in _row_blocks(tm):
            dh = ALPHA * dres_ref[rows, :] + acc[rows]
            xhat = xh_ref[rows, :]
            dg_ref[...] += jnp.sum(dh * xhat, axis=0, keepdims=True)
            db_ref[...] += jnp.sum(dh, axis=0, keepdims=True)
            dr = _ln_backward(dh, xhat, rstd_ref[rows, :], gain_ref[...])
            dr_ref[rows, :] = dr
            drb_ref[rows, :] = dr.astype(BF16)

    tok = pl.BlockSpec((tm, d), lambda i: (i, 0))
    vec = pl.BlockSpec((1, d), lambda i: (0, 0))
    in_specs = [pl.BlockSpec((tm, nj * cs), lambda i: (i, 0)), ANY, tok, tok, pl.BlockSpec((tm, 1), lambda i: (i, 0)), vec]
    out_shape = [jax.ShapeDtypeStruct((s, d), F32), jax.ShapeDtypeStruct((s, d), BF16)] + [jax.ShapeDtypeStruct((1, d), F32)] * 2
    scratch = [pltpu.VMEM(w3.shape, BF16), pltpu.SemaphoreType.DMA((1,))]
    return _call(body, name, (s // tm,), in_specs, [tok, tok, vec, vec], out_shape, scratch, [dz, w3, dres] + list(ln))[0]


def _hidden_pairs(dg3, wg3, du3, wu3, tm):
    nj, s, fs = dg3.shape
    d = wg3.shape[1]
    a_spec = pl.BlockSpec((None, tm, fs), lambda i, k: (k, i, 0))
    w_spec = pl.BlockSpec((None, d, fs), lambda i, k: (k, 0, 0))
    return [(dg3, a_spec, wg3, w_spec), (du3, a_spec, wu3, w_spec)]


def _shift_rows_down(v, halo, k, row):
    out = pltpu.roll(v, k, 0)
    hr = halo.shape[0]
    for r in range(k):
        out = jnp.where(row == r, halo[hr - k + r:hr - k + r + 1, :], out)
    return out


def _shift_rows_up(v, halo, k, row):
    t = v.shape[0]
    out = pltpu.roll(v, t - k, 0)
    for r in range(k):
        out = jnp.where(row == t - k + r, halo[r:r + 1, :], out)
    return out


def _sgu_head_forward(z_ref, h, da, gv_ref, bv_ref):
    zu = z_ref[:, h * HEAD:(h + 1) * HEAD].astype(F32)
    zv = z_ref[:, da + h * HEAD:da + (h + 1) * HEAD].astype(F32)
    u = _gelu(zu)
    v = _gelu(zv)
    mu = jnp.mean(v, axis=-1, keepdims=True)
    c = v - mu
    rstd = lax.rsqrt(jnp.mean(c * c, axis=-1, keepdims=True) + LN_EPS)
    vhat = c * rstd
    vln = (vhat * gv_ref[h:h + 1, :] + bv_ref[h:h + 1, :]).astype(BF16)
    return zu, zv, u, vhat, rstd, vln


def _mixer_fwd(z, ws_masked, bs_wide, gv, bv, cw, name):
    s, zc = z.shape
    da = zc // 5
    nh = da // HEAD
    tm = min(s, 512)
    hb = tm // BF16_ROWS

    def body(z_ref, pc_ref, px_ref, ws_ref, bs_ref, gv_ref, bv_ref, cw_ref, y_ref):
        i = pl.program_id(0)
        for h in range(nh):
            _, _, u, _, _, vln = _sgu_head_forward(z_ref, h, da, gv_ref, bv_ref)
            for n in range(tm // CHUNK):
                rows = slice(n * CHUNK, (n + 1) * CHUNK)
                mixed = jnp.dot(ws_ref[h], vln[rows], preferred_element_type=F32) + bs_ref[h]
                y_ref[0, rows, h * HEAD:(h + 1) * HEAD] = (u[rows] * mixed).astype(BF16)
        gate_b = z_ref[:, 2 * da:3 * da].astype(F32)
        hc = z_ref[:, 3 * da:4 * da].astype(F32) * z_ref[:, 4 * da:5 * da].astype(F32)
        halo = jnp.where(i > 0, pc_ref[...].astype(F32) * px_ref[...].astype(F32), 0.0)
        row = lax.broadcasted_iota(jnp.int32, (tm, da), 0)
        y = cw_ref[0:1, :] * _shift_rows_down(hc, halo, 2, row) + cw_ref[1:2, :] * _shift_rows_down(hc, halo, 1, row) + cw_ref[2:3, :] * hc
        y_ref[1] = (gate_b * y).astype(BF16)

    prev = lambda col: pl.BlockSpec((BF16_ROWS, da), lambda i: (jnp.maximum(i * hb - 1, 0), col))
    in_specs = [pl.BlockSpec((tm, zc), lambda i: (i, 0)), prev(3), prev(4), _row((nh, CHUNK, CHUNK)), _row((nh, CHUNK, HEAD)),
                _row((nh, HEAD)), _row((nh, HEAD)), _row((CONV_TAPS, da))]
    return _call(body, name, (s // tm,), in_specs, pl.BlockSpec((2, tm, da), lambda i: (0, i, 0)),
                 jax.ShapeDtypeStruct((2, s, da), BF16), [], [z, z, z, ws_masked, bs_wide, gv, bv, cw])[0]


def _mixer_bwd(z, dy, ws_masked, bs_wide, gv, bv, cw, name, rider=None):
    s, zc = z.shape
    da = zc // 5
    nh = da // HEAD
    tm = min(s, 512)
    hb = tm // BF16_ROWS
    nblk = s // tm

    def body(z_ref, pc_ref, px_ref, nb_ref, dy_ref, ndy_ref, ws_ref, bs_ref, gv_ref, bv_ref, cw_ref,
             dz_ref, dws_ref, dbs_ref, dgv_ref, dbv_ref, dcw_ref, dbin_ref):
        i = pl.program_id(0)

        @pl.when(i == 0)
        def _():
            for ref in (dws_ref, dbs_ref, dgv_ref, dbv_ref, dcw_ref, dbin_ref):
                ref[...] = jnp.zeros_like(ref)

        causal = lax.broadcasted_iota(jnp.int32, (CHUNK, CHUNK), 0) >= lax.broadcasted_iota(jnp.int32, (CHUNK, CHUNK), 1)
        for h in range(nh):
            zu, zv, u, vhat, rstd, vln = _sgu_head_forward(z_ref, h, da, gv_ref, bv_ref)
            dya = dy_ref[0, :, h * HEAD:(h + 1) * HEAD].astype(F32)
            w = ws_ref[h]
            du_parts, dvln_parts = [], []
            for n in range(tm // CHUNK):
                rows = slice(n * CHUNK, (n + 1) * CHUNK)
                mixed = jnp.dot(w, vln[rows], preferred_element_type=F32) + bs_ref[h]
                du_parts.append(dya[rows] * mixed)
                dmix = dya[rows] * u[rows]
                dmix_b = dmix.astype(BF16)
                dws_ref[h] += jnp.where(causal, lax.dot_general(dmix_b, vln[rows], NT_DIMS, preferred_element_type=F32), 0.0)
                dbs_ref[h] += dmix
                dvln_parts.append(lax.dot_general(w, dmix_b, TN_DIMS, preferred_element_type=F32))
            du = jnp.concatenate(du_parts, axis=0)
            dvln = jnp.concatenate(dvln_parts, axis=0)
            dgv_ref[h:h + 1, :] += jnp.sum(dvln * vhat, axis=0, keepdims=True)
            dbv_ref[h:h + 1, :] += jnp.sum(dvln, axis=0, keepdims=True)
            dv = _ln_backward(dvln, vhat, rstd, gv_ref[h:h + 1, :])
            dzu = du * _gelu_grad(zu)
            dzv = dv * _gelu_grad(zv)
            ucols = slice(h * HEAD, (h + 1) * HEAD)
            vcols = slice(da + h * HEAD, da + (h + 1) * HEAD)
            dz_ref[:, ucols] = dzu.astype(BF16)
            dz_ref[:, vcols] = dzv.astype(BF16)
            dbin_ref[:, ucols] += jnp.sum(dzu, axis=0, keepdims=True)
            dbin_ref[:, vcols] += jnp.sum(dzv, axis=0, keepdims=True)

        gate_b = z_ref[:, 2 * da:3 * da].astype(F32)
        gate_c = z_ref[:, 3 * da:4 * da].astype(F32)
        xt = z_ref[:, 4 * da:5 * da].astype(F32)
        hc = gate_c * xt
        halo = jnp.where(i > 0, pc_ref[...].astype(F32) * px_ref[...].astype(F32), 0.0)
        row = lax.broadcasted_iota(jnp.int32, (tm, da), 0)
        sh1 = _shift_rows_down(hc, halo, 1, row)
        sh2 = _shift_rows_down(hc, halo, 2, row)
        y = cw_ref[0:1, :] * sh2 + cw_ref[1:2, :] * sh1 + cw_ref[2:3, :] * hc
        dyb = dy_ref[1].astype(F32)
        dconv = dyb * gate_b
        nhalo = jnp.where(i < nblk - 1, ndy_ref[...].astype(F32) * nb_ref[...].astype(F32), 0.0)
        dhc = cw_ref[2:3, :] * dconv + cw_ref[1:2, :] * _shift_rows_up(dconv, nhalo, 1, row) + cw_ref[0:1, :] * _shift_rows_up(dconv, nhalo, 2, row)
        dcw_ref[0:1, :] += jnp.sum(dconv * sh2, axis=0, keepdims=True)
        dcw_ref[1:2, :] += jnp.sum(dconv * sh1, axis=0, keepdims=True)
        dcw_ref[2:3, :] += jnp.sum(dconv * hc, axis=0, keepdims=True)
        for col, val in ((2, dyb * y), (3, dhc * xt), (4, dhc * gate_c)):
            cols = slice(col * da, (col + 1) * da)
            dz_ref[:, cols] = val.astype(BF16)
            dbin_ref[:, cols] += jnp.sum(val, axis=0, keepdims=True)

        @pl.when(i == nblk - 1)
        def _():
            for h in range(nh):
                dbs_ref[h] = jnp.broadcast_to(jnp.sum(dbs_ref[h], axis=1, keepdims=True), (CHUNK, HEAD))

    prev = lambda col: pl.BlockSpec((BF16_ROWS, da), lambda i: (jnp.maximum(i * hb - 1, 0), col))
    nxt = lambda i: jnp.minimum((i + 1) * hb, s // BF16_ROWS - 1)
    in_specs = [pl.BlockSpec((tm, zc), lambda i: (i, 0)), prev(3), prev(4), pl.BlockSpec((BF16_ROWS, da), lambda i: (nxt(i), 2)),
                pl.BlockSpec((2, tm, da), lambda i: (0, i, 0)), pl.BlockSpec((None, BF16_ROWS, da), lambda i: (1, nxt(i), 0)),
                _row((nh, CHUNK, CHUNK)), _row((nh, CHUNK, HEAD)), _row((nh, HEAD)), _row((nh, HEAD)), _row((CONV_TAPS, da))]
    out_specs = [pl.BlockSpec((tm, zc), lambda i: (i, 0)), _row((nh, CHUNK, CHUNK)), _row((nh, CHUNK, HEAD)), _row((nh, HEAD)),
                 _row((nh, HEAD)), _row((8, da)), _row((1, zc))]
    out_shape = [jax.ShapeDtypeStruct((s, zc), BF16), jax.ShapeDtypeStruct((nh, CHUNK, CHUNK), F32),
                 jax.ShapeDtypeStruct((nh, CHUNK, HEAD), F32), jax.ShapeDtypeStruct((nh, HEAD), F32),
                 jax.ShapeDtypeStruct((nh, HEAD), F32), jax.ShapeDtypeStruct((8, da), F32), jax.ShapeDtypeStruct((1, zc), F32)]
    return _call(body, name, (nblk,), in_specs, out_specs, out_shape, [], [z, z, z, z, dy, dy, ws_masked, bs_wide, gv, bv, cw], rider)


def _adamw(gparts, w, m, v, name):
    n, r, c = gparts.shape
    tr = r // 4 if (r // 4) % BF16_ROWS == 0 else r

    def body(g_ref, w_ref, m_ref, v_ref, go_ref, d_ref, mo_ref, vo_ref):
        g = g_ref[0].astype(F32)
        for q in range(1, n):
            g = g + g_ref[q].astype(F32)
        m_new = ADAM_B1 * m_ref[...] + (1.0 - ADAM_B1) * g
        v_new = ADAM_B2 * v_ref[...] + (1.0 - ADAM_B2) * (g * g)
        m_hat = m_new / (1.0 - ADAM_B1 ** ADAM_STEP)
        v_hat = v_new / (1.0 - ADAM_B2 ** ADAM_STEP)
        go_ref[...] = g
        d_ref[...] = -ADAM_LR * (m_hat / (jnp.sqrt(v_hat) + ADAM_EPS) + ADAM_WD * w_ref[...])
        mo_ref[...] = m_new
        vo_ref[...] = v_new

    blk = pl.BlockSpec((tr, c), lambda i: (i, 0))
    shp = jax.ShapeDtypeStruct((r, c), F32)
    return _call(body, name, (r // tr,), [pl.BlockSpec((n, tr, c), lambda i: (0, i, 0)), blk, blk, blk], [blk] * 4, [shp] * 4, [],
                 [gparts, w, m, v])[0]


def _rows128(a):
    return a.reshape(-1, LANES)


def kernel(x, ffa_gate, ffa_up, ffa_down, ln_a_g, ln_a_b, w_in, b_in, w_s, b_s, ln_v_g, ln_v_b, conv_w, w_out, b_out, ln_m_g, ln_m_b, ffc_gate, ffc_up, ffc_down, ln_c_g, ln_c_b, loss_target, m_ffa_gate, m_ffa_up, m_ffa_down, m_ln_a_g, m_ln_a_b, m_w_in, m_b_in, m_w_s, m_b_s, m_ln_v_g, m_ln_v_b, m_conv_w, m_w_out, m_b_out, m_ln_m_g, m_ln_m_b, m_ffc_gate, m_ffc_up, m_ffc_down, m_ln_c_g, m_ln_c_b, v_ffa_gate, v_ffa_up, v_ffa_down, v_ln_a_g, v_ln_a_b, v_w_in, v_b_in, v_w_s, v_b_s, v_ln_v_g, v_ln_v_b, v_conv_w, v_w_out, v_b_out, v_ln_m_g, v_ln_m_b, v_ffc_gate, v_ffc_up, v_ffc_down, v_ln_c_g, v_ln_c_b):
    x2, target = x[0], loss_target[0]
    s, d = x2.shape
    da = d // 2
    nh = da // HEAD
    me = 4 * lax.axis_index("x") + 2 * lax.axis_index("y") + lax.axis_index("c")

    big = dict(ffa_gate=ffa_gate, ffa_up=ffa_up, ffa_down=ffa_down, w_in=w_in, w_out=w_out, ffc_gate=ffc_gate, ffc_up=ffc_up, ffc_down=ffc_down)
    big_m = dict(ffa_gate=m_ffa_gate, ffa_up=m_ffa_up, ffa_down=m_ffa_down, w_in=m_w_in, w_out=m_w_out, ffc_gate=m_ffc_gate, ffc_up=m_ffc_up, ffc_down=m_ffc_down)
    big_v = dict(ffa_gate=v_ffa_gate, ffa_up=v_ffa_up, ffa_down=v_ffa_down, w_in=v_w_in, w_out=v_w_out, ffc_gate=v_ffc_gate, ffc_up=v_ffc_up, ffc_down=v_ffc_down)
    shard = {k: w[0].astype(BF16) for k, w in big.items()}
    conv_rows = jnp.pad(conv_w[0], ((0, 8 - CONV_TAPS), (0, 0)))

    tril = jnp.tril(jnp.ones((CHUNK, CHUNK), dtype=bool))
    ws_masked = jnp.where(tril[None], w_s[0], 0.0).astype(BF16)
    bs_wide = jnp.broadcast_to(b_s[0][:, :, None], (nh, CHUNK, HEAD))
    gv, bv = ln_v_g.reshape(nh, HEAD), ln_v_b.reshape(nh, HEAD)

    full = {}
    full["ffa_gate"], full["ffa_up"], full["ffa_down"] = _all_gather([shard["ffa_gate"], shard["ffa_up"], shard["ffa_down"]], "gather_ffa")
    xb = x2.astype(BF16)
    (g_a, u_a, hid_a), (full["w_in"], full["w_out"], conv_full, full["ffc_gate"]) = _ffn_gateup(
        xb, full["ffa_gate"], full["ffa_up"], "ffa_gateup", _Gather([shard["w_in"], shard["w_out"], conv_rows, shard["ffc_gate"]]))
    (xhat1, h1b, rstd1), (full["ffc_up"], full["ffc_down"]) = _down_ln(
        hid_a, full["ffa_down"], None, x2, None, ln_a_g, ln_a_b, 0.5, "ffa_down_ln", rider=_Gather([shard["ffc_up"], shard["ffc_down"]]))
    cw = jnp.transpose(conv_full[:, :CONV_TAPS, :], (1, 0, 2)).reshape(CONV_TAPS, da)
    w_out2 = full["w_out"].reshape(2, da, d)
    z = _proj_in(h1b, full["w_in"], b_in, "proj_in")
    ycat = _mixer_fwd(z, ws_masked, bs_wide, gv, bv, cw, "mixer_fwd")
    (xhat2, h2b, rstd2), _ = _down_ln(ycat, w_out2, b_out, xhat1, (ln_a_g, ln_a_b), ln_m_g, ln_m_b, 1.0, "proj_out_ln")
    (g_c, u_c, hid_c), _ = _ffn_gateup(h2b, full["ffc_gate"], full["ffc_up"], "ffc_gateup")
    (dr3, dr3b, sq_err, d_ln_c_g, d_ln_c_b), _ = _down_ln(hid_c, full["ffc_down"], None, xhat2, (ln_m_g, ln_m_b), ln_c_g, ln_c_b, 0.5,
                                                          "ffc_down_ln_loss", target=target)
    loss = lax.psum((0.5 / d) * jnp.sum(sq_err), ("x", "y", "c"))

    tm = min(s, 512)
    landed = {}
    (dg_c, du_c), _ = _nt_hidden(dr3b, full["ffc_down"], (g_c, u_c), 0.5, "ffc_bwd_hidden")
    part, _ = _dw_hidden_rows(hid_c, dr3b, 0.5, "ffc_dw_down")
    part, (landed["ffc_down"],) = _dw_hidden_cols(h2b, dg_c, "ffc_dw_gate", _Scatter(part))
    part, (landed["ffc_gate"],) = _dw_hidden_cols(h2b, du_c, "ffc_dw_up", _Scatter(part))
    (dr2, dr2b, d_ln_m_g, d_ln_m_b, d_b_out), (landed["ffc_up"],) = _nt_acc(
        _hidden_pairs(dg_c, full["ffc_gate"], du_c, full["ffc_up"], tm), N_DEV, dr3, (xhat2, rstd2, ln_m_g), "ffc_bwd_input_ln", _Scatter(part))
    dycat, _ = _nt_hidden(dr2b, w_out2, None, 1.0, "proj_out_bwd")
    part, _ = _dw_hidden_rows(ycat, dr2b, 1.0, "proj_out_dw")
    (dz, d_w_s, d_b_s_wide, d_gv, d_bv, d_cw, d_b_in), (landed["w_out"],) = _mixer_bwd(
        z, dycat, ws_masked, bs_wide, gv, bv, cw, "mixer_bwd", _Scatter(part.reshape(N_DEV, d // N_DEV, d)))
    dr1, dr1b, d_ln_a_g, d_ln_a_b = _proj_in_bwd_ln(dz, full["w_in"], dr2, (xhat1, rstd1, ln_a_g), "proj_in_bwd_ln")
    part, _ = _dw_cols(h1b, dz, N_DEV, "proj_in_dw")
    (dg_a, du_a), (landed["w_in"],) = _nt_hidden(dr1b, full["ffa_down"], (g_a, u_a), 0.5, "ffa_bwd_hidden", _Scatter(part))
    part, _ = _dw_hidden_rows(hid_a, dr1b, 0.5, "ffa_dw_down")
    part, (landed["ffa_down"],) = _dw_hidden_cols(xb, dg_a, "ffa_dw_gate", _Scatter(part))
    part, (landed["ffa_gate"],) = _dw_hidden_cols(xb, du_a, "ffa_dw_up", _Scatter(part))
    grad_x, (landed["ffa_up"],) = _nt_acc(_hidden_pairs(dg_a, full["ffa_gate"], du_a, full["ffa_up"], tm), N_DEV, dr1, None, "ffa_bwd_input",
                                          _Scatter(part))

    grads, deltas, new_m, new_v = {}, {}, {}, {}
    for k in big:
        out = _adamw(landed[k], big[k][0], big_m[k][0], big_v[k][0], "adamw_" + k)
        grads[k], deltas[k], new_m[k], new_v[k] = (o.reshape(big[k].shape) for o in out)

    small = dict(ln_a_g=ln_a_g, ln_a_b=ln_a_b, b_in=b_in, w_s=w_s, b_s=b_s, ln_v_g=ln_v_g, ln_v_b=ln_v_b, b_out=b_out,
                 ln_m_g=ln_m_g, ln_m_b=ln_m_b, ln_c_g=ln_c_g, ln_c_b=ln_c_b)
    small_m = dict(ln_a_g=m_ln_a_g, ln_a_b=m_ln_a_b, b_in=m_b_in, w_s=m_w_s, b_s=m_b_s, ln_v_g=m_ln_v_g, ln_v_b=m_ln_v_b, b_out=m_b_out,
                   ln_m_g=m_ln_m_g, ln_m_b=m_ln_m_b, ln_c_g=m_ln_c_g, ln_c_b=m_ln_c_b)
    small_v = dict(ln_a_g=v_ln_a_g, ln_a_b=v_ln_a_b, b_in=v_b_in, w_s=v_w_s, b_s=v_b_s, ln_v_g=v_ln_v_g, ln_v_b=v_ln_v_b, b_out=v_b_out,
                   ln_m_g=v_ln_m_g, ln_m_b=v_ln_m_b, ln_c_g=v_ln_c_g, ln_c_b=v_ln_c_b)
    small_g = dict(ln_a_g=d_ln_a_g, ln_a_b=d_ln_a_b, b_in=d_b_in, w_s=d_w_s, b_s=d_b_s_wide[:, :, 0], ln_v_g=d_gv, ln_v_b=d_bv, b_out=d_b_out,
                   ln_m_g=d_ln_m_g, ln_m_b=d_ln_m_b, ln_c_g=d_ln_c_g, ln_c_b=d_ln_c_b)
    snames = list(small)
    rows = [_rows128(small[k]).shape[0] for k in snames]
    conv_rows_n = CONV_TAPS * da // LANES
    pack = jnp.concatenate([_rows128(small_g[k]) for k in snames] + [_rows128(d_cw[:CONV_TAPS])], axis=0)
    total = _all_reduce_small(pack, "all_reduce_small")
    n_rep = sum(rows)
    conv_g = lax.dynamic_slice_in_dim(total[n_rep:n_rep + conv_rows_n].reshape(CONV_TAPS, da), me * conv_w.shape[2], conv_w.shape[2], axis=1)
    pad_conv = lambda a: jnp.pad(a, ((0, 8 - CONV_TAPS), (0, 0)))
    stack = lambda tree, conv: jnp.concatenate([_rows128(tree[k]) for k in snames] + [pad_conv(conv)], axis=0)
    out = _adamw(jnp.concatenate([total[:n_rep], pad_conv(conv_g)], axis=0)[None], stack(small, conv_w[0]), stack(small_m, m_conv_w[0]),
                 stack(small_v, v_conv_w[0]), "adamw_small")
    for tree, packed in zip((grads, deltas, new_m, new_v), out):
        at = 0
        for k, r in zip(snames, rows):
            tree[k] = packed[at:at + r].reshape(small[k].shape)
            at += r
        tree["conv_w"] = packed[at:at + CONV_TAPS].reshape(conv_w.shape)

    order = ["ffa_gate", "ffa_up", "ffa_down", "ln_a_g", "ln_a_b", "w_in", "b_in", "w_s", "b_s", "ln_v_g", "ln_v_b", "conv_w", "w_out", "b_out",
             "ln_m_g", "ln_m_b", "ffc_gate", "ffc_up", "ffc_down", "ln_c_g", "ln_c_b"]
    return (loss, grad_x[None], *[grads[k] for k in order], *[deltas[k] for k in order], *[new_m[k] for k in order], *[new_v[k] for k in order])
```

```python
import math

import jax
import jax.numpy as jnp
from jax import lax
from jax.experimental import pallas as pl
from jax.experimental.pallas import tpu as pltpu

BF16 = jnp.bfloat16
F32 = jnp.float32
MESH = pl.DeviceIdType.MESH

N_DEV = 8
HEAD = 128
CHUNK = 128
CONV_TAPS = 3
LN_EPS = 1e-5
ALPHA = float(2 ** 0.25)
GELU_C = 0.7978845608028654
GELU_A = 0.044715
ADAM_LR, ADAM_B1, ADAM_B2, ADAM_EPS, ADAM_WD, ADAM_STEP = 0.001, 0.9, 0.999, 1e-08, 0.01, 10
V7X_VMEM_LIMIT = 56 * 1024 * 1024
LANES = 128
BF16_ROWS = 16
DW_TOKENS = 2048

NT_DIMS = (((1,), (1,)), ((), ()))
TN_DIMS = (((0,), (0,)), ((), ()))
ANY = pl.BlockSpec(memory_space=pl.ANY)


def _gelu(x):
    return 0.5 * x * (1.0 + jnp.tanh(GELU_C * (x + GELU_A * x * x * x)))


def _gelu_grad(x):
    t = jnp.tanh(GELU_C * (x + GELU_A * x * x * x))
    return 0.5 * (1.0 + t) + 0.5 * x * (1.0 - t * t) * GELU_C * (1.0 + 3.0 * GELU_A * x * x)


def _sigmoid(x):
    return 1.0 / (1.0 + jnp.exp(-x))


def _row(shape):
    return pl.BlockSpec(shape, lambda *_: (0,) * len(shape))


def _row_blocks(tm, rows=128):
    rows = min(rows, tm)
    return [slice(r, r + rows) for r in range(0, tm, rows)]


def _ln_backward(dh, xhat, rstd, gain):
    dxh = dh * gain
    m1 = jnp.mean(dxh, axis=-1, keepdims=True)
    m2 = jnp.mean(dxh * xhat, axis=-1, keepdims=True)
    return rstd * (dxh - m1 - xhat * m2)


def _place():
    x, y, c = lax.axis_index("x"), lax.axis_index("y"), lax.axis_index("c")
    return x, y, c, [(1 - x, y), (x, 1 - y), (1 - x, 1 - y)]


def _other_devices(x, y, c):
    flips = [(bx, by, bc) for bx in (0, 1) for by in (0, 1) for bc in (0, 1)][1:]
    return [(1 - x if bx else x, 1 - y if by else y, 1 - c if bc else c) for bx, by, bc in flips]


class _Gather:
    def __init__(self, shards, forward_at=0.75):
        n = len(shards)
        self.n, self.forward_at = n, forward_at
        self.inputs = list(shards)
        self.out_shapes = [jax.ShapeDtypeStruct((N_DEV,) + a.shape, a.dtype) for a in shards]
        self.scratch = [pltpu.SemaphoreType.DMA((n, 7)), pltpu.SemaphoreType.DMA((n, 7)), pltpu.SemaphoreType.DMA((n,))]

    def _copy(self, outs, sems, a, k, block, to, src=None):
        dst = outs[a].at[block]
        return pltpu.make_async_remote_copy(src_ref=dst if src is None else src, dst_ref=dst, send_sem=sems[0].at[a, k],
                                            recv_sem=sems[1].at[a, k], device_id=to, device_id_type=MESH)

    def start(self, ins, outs, sems):
        x, y, c, chips = _place()
        me = 4 * x + 2 * y + c
        for a in range(self.n):
            pltpu.make_async_copy(ins[a], outs[a].at[me], sems[2].at[a]).start()
        for a in range(self.n):
            self._copy(outs, sems, a, 0, me, (x, y, 1 - c), src=ins[a]).start()
            for j, chip in enumerate(chips):
                self._copy(outs, sems, a, 1 + j, me, (*chip, c), src=ins[a]).start()

    def wait_sibling(self, outs, sems, a):
        x, y, c, _ = _place()
        self._copy(outs, sems, a, 0, 4 * x + 2 * y + 1 - c, (x, y, 1 - c)).wait_recv()

    def pass_on(self, outs, sems, a, j):
        x, y, c, chips = _place()
        block = 4 * chips[j][0] + 2 * chips[j][1] + c
        self._copy(outs, sems, a, 1 + j, block, (x, y, 1 - c)).wait_recv()
        self._copy(outs, sems, a, 4 + j, block, (x, y, 1 - c)).start()

    def wait_passed(self, outs, sems, a, j):
        x, y, c, chips = _place()
        self._copy(outs, sems, a, 4 + j, 4 * chips[j][0] + 2 * chips[j][1] + 1 - c, (x, y, 1 - c)).wait_recv()

    def wait_sent(self, ins, outs, sems, a):
        x, y, c, _ = _place()
        me = 4 * x + 2 * y + c
        for k in range(7):
            self._copy(outs, sems, a, k, me, (x, y, 1 - c), src=ins[a]).wait_send()
        pltpu.make_async_copy(ins[a], outs[a].at[me], sems[2].at[a]).wait()

    def forward(self, ins, outs, sems):
        for a in range(self.n):
            for j in range(3):
                self.pass_on(outs, sems, a, j)

    def finish(self, ins, outs, sems):
        for a in range(self.n):
            self.wait_sibling(outs, sems, a)
            for j in range(3):
                self.wait_passed(outs, sems, a, j)
        for a in range(self.n):
            self.wait_sent(ins, outs, sems, a)

    def before(self, step, n_steps, ins, outs, sems):
        pl.when(step == 0)(lambda: self.start(ins, outs, sems))
        pl.when(step == int(self.forward_at * (n_steps - 1)))(lambda: self.forward(ins, outs, sems))

    def after(self, step, n_steps, ins, outs, sems):
        pl.when(step == n_steps - 1)(lambda: self.finish(ins, outs, sems))


class _Scatter:
    def __init__(self, partial):
        self.inputs = [partial]
        self.out_shapes = [jax.ShapeDtypeStruct(partial.shape, partial.dtype)]
        self.scratch = [pltpu.SemaphoreType.DMA((7,)), pltpu.SemaphoreType.DMA((7,)), pltpu.SemaphoreType.DMA((1,))]

    def _copies(self, ins, outs, sems):
        x, y, c, _ = _place()
        me = 4 * x + 2 * y + c
        mine = pltpu.make_async_copy(ins[0].at[me], outs[0].at[me], sems[2].at[0])
        remote = [pltpu.make_async_remote_copy(src_ref=ins[0].at[4 * px + 2 * py + pc], dst_ref=outs[0].at[me], send_sem=sems[0].at[k],
                                               recv_sem=sems[1].at[k], device_id=(px, py, pc), device_id_type=MESH)
                  for k, (px, py, pc) in enumerate(_other_devices(x, y, c))]
        return mine, remote

    def start(self, ins, outs, sems):
        mine, remote = self._copies(ins, outs, sems)
        mine.start()
        for cp in remote:
            cp.start()

    def finish(self, ins, outs, sems):
        mine, remote = self._copies(ins, outs, sems)
        for cp in remote:
            cp.wait()
        mine.wait()

    def before(self, step, n_steps, ins, outs, sems):
        pl.when(step == 0)(lambda: self.start(ins, outs, sems))

    def after(self, step, n_steps, ins, outs, sems):
        pl.when(step == n_steps - 1)(lambda: self.finish(ins, outs, sems))


def _call(body, name, grid, in_specs, out_specs, out_shape, scratch, ins, rider=None):
    single = not isinstance(out_shape, (list, tuple))
    out_shape = [out_shape] if single else list(out_shape)
    out_specs = [out_specs] if single else list(out_specs)
    params = pltpu.CompilerParams(dimension_semantics=("arbitrary",) * len(grid), vmem_limit_bytes=V7X_VMEM_LIMIT)
    if rider is None:
        outs = pl.pallas_call(body, name=name, grid=grid, in_specs=in_specs, out_specs=out_specs, out_shape=out_shape,
                              scratch_shapes=scratch, compiler_params=params)(*ins)
        return (outs[0] if single else outs), None
    n_in, n_out, n_scr = len(ins), len(out_shape), len(scratch)
    r_in, r_out = len(rider.inputs), len(rider.out_shapes)
    n_steps = math.prod(grid)

    def carried(*refs):
        refs = list(refs)
        cut = lambda n: [refs.pop(0) for _ in range(n)]
        b_in, c_in, b_out, c_out, b_scr = cut(n_in), cut(r_in), cut(n_out), cut(r_out), cut(n_scr)
        step = 0
        for axis, size in enumerate(grid):
            step = step * size + pl.program_id(axis)
        rider.before(step, n_steps, c_in, c_out, refs)
        body(*b_in, *b_out, *b_scr)
        rider.after(step, n_steps, c_in, c_out, refs)

    outs = pl.pallas_call(
        carried, name=name, grid=grid, in_specs=list(in_specs) + [ANY] * r_in, out_specs=out_specs + [ANY] * r_out,
        out_shape=out_shape + rider.out_shapes, scratch_shapes=list(scratch) + rider.scratch, compiler_params=params,
    )(*ins, *rider.inputs)
    base = outs[:n_out]
    return (base[0] if single else base), outs[n_out:]


def _all_reduce_small(pack, name):
    p, lanes = pack.shape

    def body(in_ref, out_ref, gath_ref, send_sems, recv_sems):
        x, y, c, _ = _place()
        me = 4 * x + 2 * y + c
        gath_ref[me] = in_ref[...]
        peers = _other_devices(x, y, c)
        for k, peer in enumerate(peers):
            pltpu.make_async_remote_copy(src_ref=in_ref, dst_ref=gath_ref.at[me], send_sem=send_sems.at[k], recv_sem=recv_sems.at[k],
                                         device_id=peer, device_id_type=MESH).start()
        for k, (px, py, pc) in enumerate(peers):
            pltpu.make_async_remote_copy(src_ref=in_ref, dst_ref=gath_ref.at[4 * px + 2 * py + pc], send_sem=send_sems.at[k],
                                         recv_sem=recv_sems.at[k], device_id=(px, py, pc), device_id_type=MESH).wait()
        total = gath_ref[0]
        for j in range(1, N_DEV):
            total = total + gath_ref[j]
        out_ref[...] = total

    vmem = pl.BlockSpec(memory_space=pltpu.VMEM)
    return pl.pallas_call(
        body, name=name, in_specs=[vmem], out_specs=vmem, out_shape=jax.ShapeDtypeStruct((p, lanes), F32),
        scratch_shapes=[pltpu.VMEM((N_DEV, p, lanes), F32), pltpu.SemaphoreType.DMA((7,)), pltpu.SemaphoreType.DMA((7,))],
        compiler_params=pltpu.CompilerParams(vmem_limit_bytes=V7X_VMEM_LIMIT),
    )(pack)


def _arrival_block(j):
    x, y, c = lax.axis_index("x"), lax.axis_index("y"), lax.axis_index("c")
    chip, other_core = j // 2, j % 2
    px = jnp.where((chip == 1) | (chip == 3), 1 - x, x)
    py = jnp.where((chip == 2) | (chip == 3), 1 - y, y)
    pc = jnp.where(other_core == 1, 1 - c, c)
    return 4 * px + 2 * py + pc


def _ffn_gateup_gathering(xb, gate_shard, up_shard, down_shard, name):
    s, d = xb.shape
    fs = gate_shard.shape[1]
    tm = min(s, 1024)
    ni = s // tm
    ask_at = ni // 2
    gather = _Gather([gate_shard, up_shard, down_shard])
    used_here, down = (0, 1), 2

    def body(x_ref, gs_ref, us_ref, ds_ref, g_ref, u_ref, h_ref, fg_ref, fu_ref, fd_ref, w_ref, w_sems, *sems):
        j, i = pl.program_id(0), pl.program_id(1)
        shards, fulls = (gs_ref, us_ref, ds_ref), (fg_ref, fu_ref, fd_ref)

        def load(slot, srcs):
            return [pltpu.make_async_copy(src, w_ref.at[slot, a], w_sems.at[slot, a]) for a, src in enumerate(srcs)]

        @pl.when((j == 0) & (i == 0))
        def _():
            gather.start(shards, fulls, sems)
            mine = load(0, (gs_ref, us_ref))
            for cp in mine:
                cp.start()
            for cp in mine:
                cp.wait()

        for nxt in range(1, N_DEV):
            @pl.when((j == nxt) & (i == 0))
            def _(nxt=nxt):
                for cp in load(nxt % 2, (fg_ref.at[0], fu_ref.at[0])):
                    cp.wait()

        for nxt in range(1, N_DEV):
            @pl.when((j == nxt - 1) & (i == ask_at))
            def _(nxt=nxt):
                for a in used_here:
                    if nxt == 1:
                        gather.wait_sibling(fulls, sems, a)
                    elif nxt % 2 == 0:
                        gather.pass_on(fulls, sems, a, nxt // 2 - 1)
                    else:
                        gather.wait_passed(fulls, sems, a, nxt // 2 - 1)
                block = _arrival_block(nxt)
                for cp in load(nxt % 2, (fg_ref.at[block], fu_ref.at[block])):
                    cp.start()

        @pl.when((j == N_DEV - 1) & (i == ask_at))
        def _():
            for other_chip in range(3):
                gather.pass_on(fulls, sems, down, other_chip)

        x = x_ref[...]
        g = jnp.dot(x, w_ref[j % 2, 0], preferred_element_type=F32)
        u = jnp.dot(x, w_ref[j % 2, 1], preferred_element_type=F32)
        g_ref[...] = g.astype(BF16)
        u_ref[...] = u.astype(BF16)
        h_ref[...] = (g * _sigmoid(g) * u).astype(BF16)

        @pl.when((j == N_DEV - 1) & (i == ni - 1))
        def _():
            gather.wait_sibling(fulls, sems, down)
            for other_chip in range(3):
                gather.wait_passed(fulls, sems, down, other_chip)
            for a in range(gather.n):
                gather.wait_sent(shards, fulls, sems, a)

    shp = jax.ShapeDtypeStruct((N_DEV, s, fs), BF16)
    o_spec = pl.BlockSpec((None, tm, fs), lambda j, i: (_arrival_block(j), i, 0))
    outs = pl.pallas_call(
        body, name=name, grid=(N_DEV, ni), in_specs=[pl.BlockSpec((tm, d), lambda j, i: (i, 0)), ANY, ANY, ANY],
        out_specs=[o_spec, o_spec, o_spec, ANY, ANY, ANY], out_shape=[shp, shp, shp] + gather.out_shapes,
        scratch_shapes=[pltpu.VMEM((2, 2, d, fs), BF16), pltpu.SemaphoreType.DMA((2, 2))] + gather.scratch,
        compiler_params=pltpu.CompilerParams(dimension_semantics=("arbitrary", "arbitrary"), vmem_limit_bytes=V7X_VMEM_LIMIT),
    )(xb, gate_shard, up_shard, down_shard)
    return outs[:3], outs[3:]


def _down_ln(a3, w3, bias, res, res_affine, ln_g, ln_b, scale, name, target=None, rider=None):
    nk, s, tk = a3.shape
    d = w3.shape[2]
    tm = min(s, 256)
    final = target is not None

    def body(*refs):
        refs = list(refs)
        a_ref, w_hbm = refs[:2]
        del refs[:2]
        bias_ref = refs.pop(0) if bias is not None else None
        res_ref = refs.pop(0)
        rg_ref, rb_ref = (refs.pop(0), refs.pop(0)) if res_affine is not None else (None, None)
        g_ref, b_ref = refs.pop(0), refs.pop(0)
        t_ref = refs.pop(0) if final else None
        w_sem = refs.pop()
        w_ref = refs.pop()
        i = pl.program_id(0)
        if final:
            dr_ref, drb_ref, sq_ref, dg_ref, db_ref = refs
        else:
            xh_ref, hb_ref, rstd_ref = refs

        @pl.when(i == 0)
        def _():
            whole = pltpu.make_async_copy(w_hbm, w_ref, w_sem.at[0])
            whole.start()
            whole.wait()
            if final:
                sq_ref[...] = jnp.zeros_like(sq_ref)
                dg_ref[...] = jnp.zeros_like(dg_ref)
                db_ref[...] = jnp.zeros_like(db_ref)

        y = jnp.dot(a_ref[0], w_ref[0], preferred_element_type=F32)
        for k in range(1, nk):
            y = y + jnp.dot(a_ref[k], w_ref[k], preferred_element_type=F32)
        if bias_ref is not None:
            y = y + bias_ref[...]
        for rows in _row_blocks(tm):
            r = res_ref[rows, :]
            if rg_ref is not None:
                r = r * rg_ref[...] + rb_ref[...]
            r = ALPHA * r + scale * y[rows]
            mu = jnp.mean(r, axis=-1, keepdims=True)
            c = r - mu
            var = jnp.mean(c * c, axis=-1, keepdims=True)
            rstd = lax.rsqrt(var + LN_EPS)
            xhat = c * rstd
            h = xhat * g_ref[...] + b_ref[...]
            if not final:
                xh_ref[rows, :] = xhat
                hb_ref[rows, :] = h.astype(BF16)
                rstd_ref[rows, :] = rstd
            else:
                err = h - t_ref[rows, :]
                sq_ref[...] += jnp.sum(err * err, axis=0, keepdims=True)
                dh = err * (1.0 / d)
                dg_ref[...] += jnp.sum(dh * xhat, axis=0, keepdims=True)
                db_ref[...] += jnp.sum(dh, axis=0, keepdims=True)
                dr = _ln_backward(dh, xhat, rstd, g_ref[...])
                dr_ref[rows, :] = dr
                drb_ref[rows, :] = dr.astype(BF16)

    tok = pl.BlockSpec((tm, d), lambda i: (i, 0))
    vec = pl.BlockSpec((1, d), lambda i: (0, 0))
    ins = [a3, w3]
    in_specs = [pl.BlockSpec((nk, tm, tk), lambda i: (0, i, 0)), ANY]
    if bias is not None:
        ins.append(bias)
        in_specs.append(vec)
    ins.append(res)
    in_specs.append(tok)
    if res_affine is not None:
        ins += list(res_affine)
        in_specs += [vec, vec]
    ins += [ln_g, ln_b]
    in_specs += [vec, vec]
    if final:
        ins.append(target)
        in_specs.append(tok)
        out_shape = [jax.ShapeDtypeStruct((s, d), F32), jax.ShapeDtypeStruct((s, d), BF16)] + [jax.ShapeDtypeStruct((1, d), F32)] * 3
        out_specs = [tok, tok, vec, vec, vec]
    else:
        out_shape = [jax.ShapeDtypeStruct((s, d), F32), jax.ShapeDtypeStruct((s, d), BF16), jax.ShapeDtypeStruct((s, 1), F32)]
        out_specs = [tok, tok, pl.BlockSpec((tm, 1), lambda i: (i, 0))]
    scratch = [pltpu.VMEM((nk, tk, d), BF16), pltpu.SemaphoreType.DMA((1,))]
    return _call(body, name, (s // tm,), in_specs, out_specs, out_shape, scratch, ins, rider)


def _proj_in(hb, w3, bias, name):
    s, d = hb.shape
    nsh, _, cs = w3.shape
    tm = min(s, 1024)

    def body(h_ref, w_ref, b_ref, z_ref):
        z_ref[...] = (jnp.dot(h_ref[...], w_ref[...], preferred_element_type=F32) + b_ref[...]).astype(BF16)

    in_specs = [pl.BlockSpec((tm, d), lambda i, j: (i, 0)), pl.BlockSpec((None, d, cs), lambda i, j: (j, 0, 0)),
                pl.BlockSpec((1, cs), lambda i, j: (0, j))]
    return _call(body, name, (s // tm, nsh), in_specs, pl.BlockSpec((tm, cs), lambda i, j: (i, j)),
                 jax.ShapeDtypeStruct((s, nsh * cs), BF16), [], [hb, w3, bias])[0]


def _nt_hidden(ab, w3, gate_up, scale, name, rider=None):
    s, kdim = ab.shape
    nj, tn, _ = w3.shape
    tm = min(s, 1024)

    def body(*refs):
        a_ref, w_ref = refs[:2]
        t = scale * lax.dot_general(a_ref[...], w_ref[...], NT_DIMS, preferred_element_type=F32)
        if gate_up is None:
            refs[2][...] = t.astype(BF16)
        else:
            g_ref, u_ref, dg_ref, du_ref = refs[2:]
            g = g_ref[...].astype(F32)
            sg = _sigmoid(g)
            du_ref[...] = (t * g * sg).astype(BF16)
            dg_ref[...] = (t * u_ref[...].astype(F32) * (sg * (1.0 + g * (1.0 - sg)))).astype(BF16)

    hid = pl.BlockSpec((None, tm, tn), lambda i, j: (j, i, 0))
    shp = jax.ShapeDtypeStruct((nj, s, tn), BF16)
    ins = [ab, w3]
    in_specs = [pl.BlockSpec((tm, kdim), lambda i, j: (i, 0)), pl.BlockSpec((None, tn, kdim), lambda i, j: (j, 0, 0))]
    if gate_up is None:
        out_specs, out_shape = hid, shp
    else:
        ins += list(gate_up)
        in_specs += [hid, hid]
        out_specs, out_shape = [hid, hid], [shp, shp]
    return _call(body, name, (s // tm, nj), in_specs, out_specs, out_shape, [], ins, rider)


def _tn_dw(a, a_spec, b, b_spec, nj, m, n, s, tk, scale, name, rider):
    def body(a_ref, b_ref, o_ref, acc_ref):
        k = pl.program_id(1)

        @pl.when(k == 0)
        def _():
            acc_ref[...] = jnp.zeros_like(acc_ref)

        acc_ref[...] += lax.dot_general(a_ref[...], b_ref[...], TN_DIMS, preferred_element_type=F32)

        @pl.when(k == s // tk - 1)
        def _():
            o_ref[...] = (scale * acc_ref[...]).astype(BF16)

    return _call(body, name, (nj, s // tk), [a_spec, b_spec], pl.BlockSpec((None, m, n), lambda j, k: (j, 0, 0)),
                 jax.ShapeDtypeStruct((nj, m, n), BF16), [pltpu.VMEM((m, n), F32)], [a, b], rider)


def _dw_hidden_rows(hid3, db, scale, name, rider=None):
    nj, s, fs = hid3.shape
    d = db.shape[1]
    tk = min(s, DW_TOKENS)
    return _tn_dw(hid3, pl.BlockSpec((None, tk, fs), lambda j, k: (j, k, 0)), db, pl.BlockSpec((tk, d), lambda j, k: (k, 0)),
                  nj, fs, d, s, tk, scale, name, rider)


def _dw_hidden_cols(ab, hid3, name, rider=None):
    nj, s, fs = hid3.shape
    d = ab.shape[1]
    tk = min(s, DW_TOKENS)
    return _tn_dw(ab, pl.BlockSpec((tk, d), lambda j, k: (k, 0)), hid3, pl.BlockSpec((None, tk, fs), lambda j, k: (j, k, 0)),
                  nj, d, fs, s, tk, 1.0, name, rider)


def _dw_cols(ab, dz, nj, name, rider=None):
    s, d = ab.shape
    cs = dz.shape[1] // nj
    tk = min(s, DW_TOKENS)
    return _tn_dw(ab, pl.BlockSpec((tk, d), lambda j, k: (k, 0)), dz, pl.BlockSpec((tk, cs), lambda j, k: (k, j)),
                  nj, d, cs, s, tk, 1.0, name, rider)


def _nt_acc(pairs, nk, dres, ln, name, rider=None):
    s, d = dres.shape
    tm = min(s, 512)
    np_ = len(pairs)

    def body(*refs):
        refs = list(refs)
        pair_refs = [(refs[2 * p], refs[2 * p + 1]) for p in range(np_)]
        del refs[:2 * np_]
        dres_ref = refs.pop(0)
        if ln is not None:
            xh_ref, rstd_ref, gain_ref = refs.pop(0), refs.pop(0), refs.pop(0)
        acc_ref = refs.pop()
        i, k = pl.program_id(0), pl.program_id(1)

        @pl.when(k == 0)
        def _():
            acc_ref[...] = jnp.zeros_like(acc_ref)

        for a_ref, w_ref in pair_refs:
            acc_ref[...] += lax.dot_general(a_ref[...], w_ref[...], NT_DIMS, preferred_element_type=F32)

        @pl.when(k == nk - 1)
        def _():
            if ln is not None:
                dr_ref, drb_ref, dg_ref, db_ref, sum_ref = refs

                @pl.when(i == 0)
                def _():
                    dg_ref[...] = jnp.zeros_like(dg_ref)
                    db_ref[...] = jnp.zeros_like(db_ref)
                    sum_ref[...] = jnp.zeros_like(sum_ref)

            for rows in _row_blocks(tm):
                dh = ALPHA * dres_ref[rows, :] + acc_ref[rows, :]
                if ln is None:
                    refs[0][rows, :] = dh
                else:
                    xhat = xh_ref[rows, :]
                    dg_ref[...] += jnp.sum(dh * xhat, axis=0, keepdims=True)
                    db_ref[...] += jnp.sum(dh, axis=0, keepdims=True)
                    dr = _ln_backward(dh, xhat, rstd_ref[rows, :], gain_ref[...])
                    sum_ref[...] += jnp.sum(dr, axis=0, keepdims=True)
                    dr_ref[rows, :] = dr
                    drb_ref[rows, :] = dr.astype(BF16)

    tok = pl.BlockSpec((tm, d), lambda i, k: (i, 0))
    vec = pl.BlockSpec((1, d), lambda i, k: (0, 0))
    ins, in_specs = [], []
    for a, a_spec, w, w_spec in pairs:
        ins += [a, w]
        in_specs += [a_spec, w_spec]
    ins.append(dres)
    in_specs.append(tok)
    if ln is None:
        out_shape, out_specs = jax.ShapeDtypeStruct((s, d), F32), tok
    else:
        ins += list(ln)
        in_specs += [tok, pl.BlockSpec((tm, 1), lambda i, k: (i, 0)), vec]
        out_shape = [jax.ShapeDtypeStruct((s, d), F32), jax.ShapeDtypeStruct((s, d), BF16)] + [jax.ShapeDtypeStruct((1, d), F32)] * 3
        out_specs = [tok, tok, vec, vec, vec]
    return _call(body, name, (s // tm, nk), in_specs, out_specs, out_shape, [pltpu.VMEM((tm, d), F32)], ins, rider)


def _proj_in_bwd_ln(dz, w3, dres, ln, name):
    s, d = dres.shape
    nj, _, cs = w3.shape
    tm = min(s, 256)

    def body(dz_ref, w_hbm, dres_ref, xh_ref, rstd_ref, gain_ref, dr_ref, drb_ref, dg_ref, db_ref, w_ref, w_sem):
        @pl.when(pl.program_id(0) == 0)
        def _():
            whole = pltpu.make_async_copy(w_hbm, w_ref, w_sem.at[0])
            whole.start()
            whole.wait()
            dg_ref[...] = jnp.zeros_like(dg_ref)
            db_ref[...] = jnp.zeros_like(db_ref)

        acc = lax.dot_general(dz_ref[:, 0:cs], w_ref[0], NT_DIMS, preferred_element_type=F32)
        for j in range(1, nj):
            acc = acc + lax.dot_general(dz_ref[:, j * cs:(j + 1) * cs], w_ref[j], NT_DIMS, preferred_element_type=F32)
        for rows in _row_blocks(tm):
            dh = ALPHA * dres_ref[rows, :] + acc[rows]
            xhat = xh_ref[rows, :]
            dg_ref[...] += jnp.sum(dh * xhat, axis=0, keepdims=True)
            db_ref[...] += jnp.sum(dh, axis=0, keepdims=True)
            dr = _ln_backward(dh, xhat, rstd_ref[rows, :], gain_ref[...])
            dr_ref[rows, :] = dr
            drb_ref[rows, :] = dr.astype(BF16)

    tok = pl.BlockSpec((tm, d), lambda i: (i, 0))
    vec = pl.BlockSpec((1, d), lambda i: (0, 0))
    in_specs = [pl.BlockSpec((tm, nj * cs), lambda i: (i, 0)), ANY, tok, tok, pl.BlockSpec((tm, 1), lambda i: (i, 0)), vec]
    out_shape = [jax.ShapeDtypeStruct((s, d), F32), jax.ShapeDtypeStruct((s, d), BF16)] + [jax.ShapeDtypeStruct((1, d), F32)] * 2
    scratch = [pltpu.VMEM(w3.shape, BF16), pltpu.SemaphoreType.DMA((1,))]
    return _call(body, name, (s // tm,), in_specs, [tok, tok, vec, vec], out_shape, scratch, [dz, w3, dres] + list(ln))[0]


def _hidden_pairs(dg3, wg3, du3, wu3, tm):
    nj, s, fs = dg3.shape
    d = wg3.shape[1]
    a_spec = pl.BlockSpec((None, tm, fs), lambda i, k: (k, i, 0))
    w_spec = pl.BlockSpec((None, d, fs), lambda i, k: (k, 0, 0))
    return [(dg3, a_spec, wg3, w_spec), (du3, a_spec, wu3, w_spec)]


def _shift_rows_down(v, halo, k, row):
    out = pltpu.roll(v, k, 0)
    hr = halo.shape[0]
    for r in range(k):
        out = jnp.where(row == r, halo[hr - k + r:hr - k + r + 1, :], out)
    return out


def _shift_rows_up(v, halo, k, row):
    t = v.shape[0]
    out = pltpu.roll(v, t - k, 0)
    for r in range(k):
        out = jnp.where(row == t - k + r, halo[r:r + 1, :], out)
    return out


def _sgu_head_forward(z_ref, h, da, gv_ref, bv_ref):
    zu = z_ref[:, h * HEAD:(h + 1) * HEAD].astype(F32)
    zv = z_ref[:, da + h * HEAD:da + (h + 1) * HEAD].astype(F32)
    u = _gelu(zu)
    v = _gelu(zv)
    mu = jnp.mean(v, axis=-1, keepdims=True)
    c = v - mu
    rstd = lax.rsqrt(jnp.mean(c * c, axis=-1, keepdims=True) + LN_EPS)
    vhat = c * rstd
    vln = (vhat * gv_ref[h:h + 1, :] + bv_ref[h:h + 1, :]).astype(BF16)
    return zu, zv, u, vhat, rstd, vln


def _mixer_fwd(z, ws_masked, bs_wide, gv, bv, cw, name):
    s, zc = z.shape
    da = zc // 5
    nh = da // HEAD
    tm = min(s, 512)
    hb = tm // BF16_ROWS

    def body(z_ref, pc_ref, px_ref, ws_ref, bs_ref, gv_ref, bv_ref, cw_ref, y_ref):
        i = pl.program_id(0)
        for h in range(nh):
            _, _, u, _, _, vln = _sgu_head_forward(z_ref, h, da, gv_ref, bv_ref)
            for n in range(tm // CHUNK):
                rows = slice(n * CHUNK, (n + 1) * CHUNK)
                mixed = jnp.dot(ws_ref[h], vln[rows], preferred_element_type=F32) + bs_ref[h]
                y_ref[0, rows, h * HEAD:(h + 1) * HEAD] = (u[rows] * mixed).astype(BF16)
        gate_b = z_ref[:, 2 * da:3 * da].astype(F32)
        hc = z_ref[:, 3 * da:4 * da].astype(F32) * z_ref[:, 4 * da:5 * da].astype(F32)
        halo = jnp.where(i > 0, pc_ref[...].astype(F32) * px_ref[...].astype(F32), 0.0)
        row = lax.broadcasted_iota(jnp.int32, (tm, da), 0)
        y = cw_ref[0:1, :] * _shift_rows_down(hc, halo, 2, row) + cw_ref[1:2, :] * _shift_rows_down(hc, halo, 1, row) + cw_ref[2:3, :] * hc
        y_ref[1] = (gate_b * y).astype(BF16)

    prev = lambda col: pl.BlockSpec((BF16_ROWS, da), lambda i: (jnp.maximum(i * hb - 1, 0), col))
    in_specs = [pl.BlockSpec((tm, zc), lambda i: (i, 0)), prev(3), prev(4), _row((nh, CHUNK, CHUNK)), _row((nh, CHUNK, HEAD)),
                _row((nh, HEAD)), _row((nh, HEAD)), _row((CONV_TAPS, da))]
    return _call(body, name, (s // tm,), in_specs, pl.BlockSpec((2, tm, da), lambda i: (0, i, 0)),
                 jax.ShapeDtypeStruct((2, s, da), BF16), [], [z, z, z, ws_masked, bs_wide, gv, bv, cw])[0]


def _mixer_bwd(z, dy, ws_masked, bs_wide, gv, bv, cw, name, rider=None):
    s, zc = z.shape
    da = zc // 5
    nh = da // HEAD
    tm = min(s, 512)
    hb = tm // BF16_ROWS
    nblk = s // tm

    def body(z_ref, pc_ref, px_ref, nb_ref, dy_ref, ndy_ref, ws_ref, bs_ref, gv_ref, bv_ref, cw_ref,
             dz_ref, dws_ref, dbs_ref, dgv_ref, dbv_ref, dcw_ref, dbin_ref):
        i = pl.program_id(0)

        @pl.when(i == 0)
        def _():
            for ref in (dws_ref, dbs_ref, dgv_ref, dbv_ref, dcw_ref, dbin_ref):
                ref[...] = jnp.zeros_like(ref)

        causal = lax.broadcasted_iota(jnp.int32, (CHUNK, CHUNK), 0) >= lax.broadcasted_iota(jnp.int32, (CHUNK, CHUNK), 1)
        for h in range(nh):
            zu, zv, u, vhat, rstd, vln = _sgu_head_forward(z_ref, h, da, gv_ref, bv_ref)
            dya = dy_ref[0, :, h * HEAD:(h + 1) * HEAD].astype(F32)
            w = ws_ref[h]
            du_parts, dvln_parts = [], []
            for n in range(tm // CHUNK):
                rows = slice(n * CHUNK, (n + 1) * CHUNK)
                mixed = jnp.dot(w, vln[rows], preferred_element_type=F32) + bs_ref[h]
                du_parts.append(dya[rows] * mixed)
                dmix = dya[rows] * u[rows]
                dmix_b = dmix.astype(BF16)
                dws_ref[h] += jnp.where(causal, lax.dot_general(dmix_b, vln[rows], NT_DIMS, preferred_element_type=F32), 0.0)
                dbs_ref[h] += dmix
                dvln_parts.append(lax.dot_general(w, dmix_b, TN_DIMS, preferred_element_type=F32))
            du = jnp.concatenate(du_parts, axis=0)
            dvln = jnp.concatenate(dvln_parts, axis=0)
            dgv_ref[h:h + 1, :] += jnp.sum(dvln * vhat, axis=0, keepdims=True)
            dbv_ref[h:h + 1, :] += jnp.sum(dvln, axis=0, keepdims=True)
            dv = _ln_backward(dvln, vhat, rstd, gv_ref[h:h + 1, :])
            dzu = du * _gelu_grad(zu)
            dzv = dv * _gelu_grad(zv)
            ucols = slice(h * HEAD, (h + 1) * HEAD)
            vcols = slice(da + h * HEAD, da + (h + 1) * HEAD)
            dz_ref[:, ucols] = dzu.astype(BF16)
            dz_ref[:, vcols] = dzv.astype(BF16)
            dbin_ref[:, ucols] += jnp.sum(dzu, axis=0, keepdims=True)
            dbin_ref[:, vcols] += jnp.sum(dzv, axis=0, keepdims=True)

        gate_b = z_ref[:, 2 * da:3 * da].astype(F32)
        gate_c = z_ref[:, 3 * da:4 * da].astype(F32)
        xt = z_ref[:, 4 * da:5 * da].astype(F32)
        hc = gate_c * xt
        halo = jnp.where(i > 0, pc_ref[...].astype(F32) * px_ref[...].astype(F32), 0.0)
        row = lax.broadcasted_iota(jnp.int32, (tm, da), 0)
        sh1 = _shift_rows_down(hc, halo, 1, row)
        sh2 = _shift_rows_down(hc, halo, 2, row)
        y = cw_ref[0:1, :] * sh2 + cw_ref[1:2, :] * sh1 + cw_ref[2:3, :] * hc
        dyb = dy_ref[1].astype(F32)
        dconv = dyb * gate_b
        nhalo = jnp.where(i < nblk - 1, ndy_ref[...].astype(F32) * nb_ref[...].astype(F32), 0.0)
        dhc = cw_ref[2:3, :] * dconv + cw_ref[1:2, :] * _shift_rows_up(dconv, nhalo, 1, row) + cw_ref[0:1, :] * _shift_rows_up(dconv, nhalo, 2, row)
        dcw_ref[0:1, :] += jnp.sum(dconv * sh2, axis=0, keepdims=True)
        dcw_ref[1:2, :] += jnp.sum(dconv * sh1, axis=0, keepdims=True)
        dcw_ref[2:3, :] += jnp.sum(dconv * hc, axis=0, keepdims=True)
        for col, val in ((2, dyb * y), (3, dhc * xt), (4, dhc * gate_c)):
            cols = slice(col * da, (col + 1) * da)
            dz_ref[:, cols] = val.astype(BF16)
            dbin_ref[:, cols] += jnp.sum(val, axis=0, keepdims=True)

        @pl.when(i == nblk - 1)
        def _():
            for h in range(nh):
                dbs_ref[h] = jnp.broadcast_to(jnp.sum(dbs_ref[h], axis=1, keepdims=True), (CHUNK, HEAD))

    prev = lambda col: pl.BlockSpec((BF16_ROWS, da), lambda i: (jnp.maximum(i * hb - 1, 0), col))
    nxt = lambda i: jnp.minimum((i + 1) * hb, s // BF16_ROWS - 1)
    in_specs = [pl.BlockSpec((tm, zc), lambda i: (i, 0)), prev(3), prev(4), pl.BlockSpec((BF16_ROWS, da), lambda i: (nxt(i), 2)),
                pl.BlockSpec((2, tm, da), lambda i: (0, i, 0)), pl.BlockSpec((None, BF16_ROWS, da), lambda i: (1, nxt(i), 0)),
                _row((nh, CHUNK, CHUNK)), _row((nh, CHUNK, HEAD)), _row((nh, HEAD)), _row((nh, HEAD)), _row((CONV_TAPS, da))]
    out_specs = [pl.BlockSpec((tm, zc), lambda i: (i, 0)), _row((nh, CHUNK, CHUNK)), _row((nh, CHUNK, HEAD)), _row((nh, HEAD)),
                 _row((nh, HEAD)), _row((8, da)), _row((1, zc))]
    out_shape = [jax.ShapeDtypeStruct((s, zc), BF16), jax.ShapeDtypeStruct((nh, CHUNK, CHUNK), F32),
                 jax.ShapeDtypeStruct((nh, CHUNK, HEAD), F32), jax.ShapeDtypeStruct((nh, HEAD), F32),
                 jax.ShapeDtypeStruct((nh, HEAD), F32), jax.ShapeDtypeStruct((8, da), F32), jax.ShapeDtypeStruct((1, zc), F32)]
    return _call(body, name, (nblk,), in_specs, out_specs, out_shape, [], [z, z, z, z, dy, dy, ws_masked, bs_wide, gv, bv, cw], rider)


def _adamw(gparts, w, m, v, name):
    n, r, c = gparts.shape
    tr = r // 4 if (r // 4) % BF16_ROWS == 0 else r

    def body(g_ref, w_ref, m_ref, v_ref, go_ref, d_ref, mo_ref, vo_ref):
        g = g_ref[0].astype(F32)
        for q in range(1, n):
            g = g + g_ref[q].astype(F32)
        m_new = ADAM_B1 * m_ref[...] + (1.0 - ADAM_B1) * g
        v_new = ADAM_B2 * v_ref[...] + (1.0 - ADAM_B2) * (g * g)
        m_hat = m_new / (1.0 - ADAM_B1 ** ADAM_STEP)
        v_hat = v_new / (1.0 - ADAM_B2 ** ADAM_STEP)
        go_ref[...] = g
        d_ref[...] = -ADAM_LR * (m_hat / (jnp.sqrt(v_hat) + ADAM_EPS) + ADAM_WD * w_ref[...])
        mo_ref[...] = m_new
        vo_ref[...] = v_new

    blk = pl.BlockSpec((tr, c), lambda i: (i, 0))
    shp = jax.ShapeDtypeStruct((r, c), F32)
    return _call(body, name, (r // tr,), [pl.BlockSpec((n, tr, c), lambda i: (0, i, 0)), blk, blk, blk], [blk] * 4, [shp] * 4, [],
                 [gparts, w, m, v])[0]


def _rows128(a):
    return a.reshape(-1, LANES)


def kernel(x, ffa_gate, ffa_up, ffa_down, ln_a_g, ln_a_b, w_in, b_in, w_s, b_s, ln_v_g, ln_v_b, conv_w, w_out, b_out, ln_m_g, ln_m_b, ffc_gate, ffc_up, ffc_down, ln_c_g, ln_c_b, loss_target, m_ffa_gate, m_ffa_up, m_ffa_down, m_ln_a_g, m_ln_a_b, m_w_in, m_b_in, m_w_s, m_b_s, m_ln_v_g, m_ln_v_b, m_conv_w, m_w_out, m_b_out, m_ln_m_g, m_ln_m_b, m_ffc_gate, m_ffc_up, m_ffc_down, m_ln_c_g, m_ln_c_b, v_ffa_gate, v_ffa_up, v_ffa_down, v_ln_a_g, v_ln_a_b, v_w_in, v_b_in, v_w_s, v_b_s, v_ln_v_g, v_ln_v_b, v_conv_w, v_w_out, v_b_out, v_ln_m_g, v_ln_m_b, v_ffc_gate, v_ffc_up, v_ffc_down, v_ln_c_g, v_ln_c_b):
    x2, target = x[0], loss_target[0]
    s, d = x2.shape
    da = d // 2
    nh = da // HEAD
    me = 4 * lax.axis_index("x") + 2 * lax.axis_index("y") + lax.axis_index("c")

    big = dict(ffa_gate=ffa_gate, ffa_up=ffa_up, ffa_down=ffa_down, w_in=w_in, w_out=w_out, ffc_gate=ffc_gate, ffc_up=ffc_up, ffc_down=ffc_down)
    big_m = dict(ffa_gate=m_ffa_gate, ffa_up=m_ffa_up, ffa_down=m_ffa_down, w_in=m_w_in, w_out=m_w_out, ffc_gate=m_ffc_gate, ffc_up=m_ffc_up, ffc_down=m_ffc_down)
    big_v = dict(ffa_gate=v_ffa_gate, ffa_up=v_ffa_up, ffa_down=v_ffa_down, w_in=v_w_in, w_out=v_w_out, ffc_gate=v_ffc_gate, ffc_up=v_ffc_up, ffc_down=v_ffc_down)
    shard = {k: w[0].astype(BF16) for k, w in big.items()}
    conv_rows = jnp.pad(conv_w[0], ((0, 8 - CONV_TAPS), (0, 0)))

    tril = jnp.tril(jnp.ones((CHUNK, CHUNK), dtype=bool))
    ws_masked = jnp.where(tril[None], w_s[0], 0.0).astype(BF16)
    bs_wide = jnp.broadcast_to(b_s[0][:, :, None], (nh, CHUNK, HEAD))
    gv, bv = ln_v_g.reshape(nh, HEAD), ln_v_b.reshape(nh, HEAD)

    full = {}
    xb = x2.astype(BF16)
    (g_a, u_a, hid_a), (full["ffa_gate"], full["ffa_up"], full["ffa_down"]) = _ffn_gateup_gathering(
        xb, shard["ffa_gate"], shard["ffa_up"], shard["ffa_down"], "ffa_gateup")
    (xhat1, h1b, rstd1), (full["w_in"], full["w_out"], conv_full) = _down_ln(
        hid_a, full["ffa_down"], None, x2, None, ln_a_g, ln_a_b, 0.5, "ffa_down_ln", rider=_Gather([shard["w_in"], shard["w_out"], conv_rows]))
    cw = jnp.transpose(conv_full[:, :CONV_TAPS, :], (1, 0, 2)).reshape(CONV_TAPS, da)
    w_out2 = full["w_out"].reshape(2, da, d)
    z = _proj_in(h1b, full["w_in"], b_in, "proj_in")
    ycat = _mixer_fwd(z, ws_masked, bs_wide, gv, bv, cw, "mixer_fwd")
    (xhat2, h2b, rstd2), _ = _down_ln(ycat, w_out2, b_out, xhat1, (ln_a_g, ln_a_b), ln_m_g, ln_m_b, 1.0, "proj_out_ln")
    (g_c, u_c, hid_c), (full["ffc_gate"], full["ffc_up"], full["ffc_down"]) = _ffn_gateup_gathering(
        h2b, shard["ffc_gate"], shard["ffc_up"], shard["ffc_down"], "ffc_gateup")
    (dr3, dr3b, sq_err, d_ln_c_g, d_ln_c_b), _ = _down_ln(hid_c, full["ffc_down"], None, xhat2, (ln_m_g, ln_m_b), ln_c_g, ln_c_b, 0.5,
                                                          "ffc_down_ln_loss", target=target)
    loss = lax.psum((0.5 / d) * jnp.sum(sq_err), ("x", "y", "c"))

    tm = min(s, 512)
    landed = {}
    (dg_c, du_c), _ = _nt_hidden(dr3b, full["ffc_down"], (g_c, u_c), 0.5, "ffc_bwd_hidden")
    part, _ = _dw_hidden_rows(hid_c, dr3b, 0.5, "ffc_dw_down")
    part, (landed["ffc_down"],) = _dw_hidden_cols(h2b, dg_c, "ffc_dw_gate", _Scatter(part))
    part, (landed["ffc_gate"],) = _dw_hidden_cols(h2b, du_c, "ffc_dw_up", _Scatter(part))
    (dr2, dr2b, d_ln_m_g, d_ln_m_b, d_b_out), (landed["ffc_up"],) = _nt_acc(
        _hidden_pairs(dg_c, full["ffc_gate"], du_c, full["ffc_up"], tm), N_DEV, dr3, (xhat2, rstd2, ln_m_g), "ffc_bwd_input_ln", _Scatter(part))
    dycat, _ = _nt_hidden(dr2b, w_out2, None, 1.0, "proj_out_bwd")
    part, _ = _dw_hidden_rows(ycat, dr2b, 1.0, "proj_out_dw")
    (dz, d_w_s, d_b_s_wide, d_gv, d_bv, d_cw, d_b_in), (landed["w_out"],) = _mixer_bwd(
        z, dycat, ws_masked, bs_wide, gv, bv, cw, "mixer_bwd", _Scatter(part.reshape(N_DEV, d // N_DEV, d)))
    dr1, dr1b, d_ln_a_g, d_ln_a_b = _proj_in_bwd_ln(dz, full["w_in"], dr2, (xhat1, rstd1, ln_a_g), "proj_in_bwd_ln")
    part, _ = _dw_cols(h1b, dz, N_DEV, "proj_in_dw")
    (dg_a, du_a), (landed["w_in"],) = _nt_hidden(dr1b, full["ffa_down"], (g_a, u_a), 0.5, "ffa_bwd_hidden", _Scatter(part))
    part, _ = _dw_hidden_rows(hid_a, dr1b, 0.5, "ffa_dw_down")
    part, (landed["ffa_down"],) = _dw_hidden_cols(xb, dg_a, "ffa_dw_gate", _Scatter(part))
    part, (landed["ffa_gate"],) = _dw_hidden_cols(xb, du_a, "ffa_dw_up", _Scatter(part))
    grad_x, (landed["ffa_up"],) = _nt_acc(_hidden_pairs(dg_a, full["ffa_gate"], du_a, full["ffa_up"], tm), N_DEV, dr1, None, "ffa_bwd_input",
                                          _Scatter(part))

    grads, deltas, new_m, new_v = {}, {}, {}, {}
    for k in big:
        out = _adamw(landed[k], big[k][0], big_m[k][0], big_v[k][0], "adamw_" + k)
        grads[k], deltas[k], new_m[k], new_v[k] = (o.reshape(big[k].shape) for o in out)

    small = dict(ln_a_g=ln_a_g, ln_a_b=ln_a_b, b_in=b_in, w_s=w_s, b_s=b_s, ln_v_g=ln_v_g, ln_v_b=ln_v_b, b_out=b_out,
                 ln_m_g=ln_m_g, ln_m_b=ln_m_b, ln_c_g=ln_c_g, ln_c_b=ln_c_b)
    small_m = dict(ln_a_g=m_ln_a_g, ln_a_b=m_ln_a_b, b_in=m_b_in, w_s=m_w_s, b_s=m_b_s, ln_v_g=m_ln_v_g, ln_v_b=m_ln_v_b, b_out=m_b_out,
                   ln_m_g=m_ln_m_g, ln_m_b=m_ln_m_b, ln_c_g=m_ln_c_g, ln_c_b=m_ln_c_b)
    small_v = dict(ln_a_g=v_ln_a_g, ln_a_b=v_ln_a_b, b_in=v_b_in, w_s=v_w_s, b_s=v_b_s, ln_v_g=v_ln_v_g, ln_v_b=v_ln_v_b, b_out=v_b_out,
                   ln_m_g=v_ln_m_g, ln_m_b=v_ln_m_b, ln_c_g=v_ln_c_g, ln_c_b=v_ln_c_b)
    small_g = dict(ln_a_g=d_ln_a_g, ln_a_b=d_ln_a_b, b_in=d_b_in, w_s=d_w_s, b_s=d_b_s_wide[:, :, 0], ln_v_g=d_gv, ln_v_b=d_bv, b_out=d_b_out,
                   ln_m_g=d_ln_m_g, ln_m_b=d_ln_m_b, ln_c_g=d_ln_c_g, ln_c_b=d_ln_c_b)
    snames = list(small)
    rows = [_rows128(small[k]).shape[0] for k in snames]
    conv_rows_n = CONV_TAPS * da // LANES
    pack = jnp.concatenate([_rows128(small_g[k]) for k in snames] + [_rows128(d_cw[:CONV_TAPS])], axis=0)
    total = _all_reduce_small(pack, "all_reduce_small")
    n_rep = sum(rows)
    conv_g = lax.dynamic_slice_in_dim(total[n_rep:n_rep + conv_rows_n].reshape(CONV_TAPS, da), me * conv_w.shape[2], conv_w.shape[2], axis=1)
    pad_conv = lambda a: jnp.pad(a, ((0, 8 - CONV_TAPS), (0, 0)))
    stack = lambda tree, conv: jnp.concatenate([_rows128(tree[k]) for k in snames] + [pad_conv(conv)], axis=0)
    out = _adamw(jnp.concatenate([total[:n_rep], pad_conv(conv_g)], axis=0)[None], stack(small, conv_w[0]), stack(small_m, m_conv_w[0]),
                 stack(small_v, v_conv_w[0]), "adamw_small")
    for tree, packed in zip((grads, deltas, new_m, new_v), out):
        at = 0
        for k, r in zip(snames, rows):
            tree[k] = packed[at:at + r].reshape(small[k].shape)
            at += r
        tree["conv_w"] = packed[at:at + CONV_TAPS].reshape(conv_w.shape)

    order = ["ffa_gate", "ffa_up", "ffa_down", "ln_a_g", "ln_a_b", "w_in", "b_in", "w_s", "b_s", "ln_v_g", "ln_v_b", "conv_w", "w_out", "b_out",
             "ln_m_g", "ln_m_b", "ffc_gate", "ffc_up", "ffc_down", "ln_c_g", "ln_c_b"]
    return (loss, grad_x[None], *[grads[k] for k in order], *[deltas[k] for k in order], *[new_m[k] for k in order], *[new_v[k] for k in order])
```

```python
import math

import jax
import jax.numpy as jnp
from jax import lax
from jax.experimental import pallas as pl
from jax.experimental.pallas import tpu as pltpu

BF16 = jnp.bfloat16
F32 = jnp.float32
MESH = pl.DeviceIdType.MESH

N_DEV = 8
HEAD = 128
CHUNK = 128
CONV_TAPS = 3
LN_EPS = 1e-5
ALPHA = float(2 ** 0.25)
GELU_C = 0.7978845608028654
GELU_A = 0.044715
ADAM_LR, ADAM_B1, ADAM_B2, ADAM_EPS, ADAM_WD, ADAM_STEP = 0.001, 0.9, 0.999, 1e-08, 0.01, 10
V7X_VMEM_LIMIT = 56 * 1024 * 1024
LANES = 128
BF16_ROWS = 16
DW_TOKENS = 2048

NT_DIMS = (((1,), (1,)), ((), ()))
TN_DIMS = (((0,), (0,)), ((), ()))
ANY = pl.BlockSpec(memory_space=pl.ANY)


def _gelu(x):
    return 0.5 * x * (1.0 + jnp.tanh(GELU_C * (x + GELU_A * x * x * x)))


def _gelu_grad(x):
    t = jnp.tanh(GELU_C * (x + GELU_A * x * x * x))
    return 0.5 * (1.0 + t) + 0.5 * x * (1.0 - t * t) * GELU_C * (1.0 + 3.0 * GELU_A * x * x)


def _sigmoid(x):
    return 1.0 / (1.0 + jnp.exp(-x))


def _row(shape):
    return pl.BlockSpec(shape, lambda *_: (0,) * len(shape))


def _row_blocks(tm, rows=128):
    rows = min(rows, tm)
    return [slice(r, r + rows) for r in range(0, tm, rows)]


def _ln_backward(dh, xhat, rstd, gain):
    dxh = dh * gain
    m1 = jnp.mean(dxh, axis=-1, keepdims=True)
    m2 = jnp.mean(dxh * xhat, axis=-1, keepdims=True)
    return rstd * (dxh - m1 - xhat * m2)


def _place():
    x, y, c = lax.axis_index("x"), lax.axis_index("y"), lax.axis_index("c")
    return x, y, c, [(1 - x, y), (x, 1 - y), (1 - x, 1 - y)]


def _other_devices(x, y, c):
    flips = [(bx, by, bc) for bx in (0, 1) for by in (0, 1) for bc in (0, 1)][1:]
    return [(1 - x if bx else x, 1 - y if by else y, 1 - c if bc else c) for bx, by, bc in flips]


class _Gather:
    def __init__(self, shards, forward_at=0.75):
        n = len(shards)
        self.n, self.forward_at = n, forward_at
        self.inputs = list(shards)
        self.out_shapes = [jax.ShapeDtypeStruct((N_DEV,) + a.shape, a.dtype) for a in shards]
        self.scratch = [pltpu.SemaphoreType.DMA((n, 7)), pltpu.SemaphoreType.DMA((n, 7)), pltpu.SemaphoreType.DMA((n,))]

    def _copy(self, outs, sems, a, k, block, to, src=None):
        dst = outs[a].at[block]
        return pltpu.make_async_remote_copy(src_ref=dst if src is None else src, dst_ref=dst, send_sem=sems[0].at[a, k],
                                            recv_sem=sems[1].at[a, k], device_id=to, device_id_type=MESH)

    def start(self, ins, outs, sems, urgent=None):
        x, y, c, chips = _place()
        me = 4 * x + 2 * y + c
        for a in range(self.n):
            pltpu.make_async_copy(ins[a], outs[a].at[me], sems[2].at[a]).start()
        urgent = list(range(self.n)) if urgent is None else list(urgent)
        for group in (urgent, [a for a in range(self.n) if a not in urgent]):
            for a in group:
                self._copy(outs, sems, a, 0, me, (x, y, 1 - c), src=ins[a]).start()
                for j in (0, 1):
                    self._copy(outs, sems, a, 1 + j, me, (*chips[j], c), src=ins[a]).start()
            for a in group:
                self._copy(outs, sems, a, 3, me, (*chips[2], c), src=ins[a]).start()

    def wait_sibling(self, outs, sems, a):
        x, y, c, _ = _place()
        self._copy(outs, sems, a, 0, 4 * x + 2 * y + 1 - c, (x, y, 1 - c)).wait_recv()

    def pass_on(self, outs, sems, a, j):
        x, y, c, chips = _place()
        block = 4 * chips[j][0] + 2 * chips[j][1] + c
        self._copy(outs, sems, a, 1 + j, block, (x, y, 1 - c)).wait_recv()
        self._copy(outs, sems, a, 4 + j, block, (x, y, 1 - c)).start()

    def wait_passed(self, outs, sems, a, j):
        x, y, c, chips = _place()
        self._copy(outs, sems, a, 4 + j, 4 * chips[j][0] + 2 * chips[j][1] + 1 - c, (x, y, 1 - c)).wait_recv()

    def wait_sent(self, ins, outs, sems, a):
        x, y, c, _ = _place()
        me = 4 * x + 2 * y + c
        for k in range(7):
            self._copy(outs, sems, a, k, me, (x, y, 1 - c), src=ins[a]).wait_send()
        pltpu.make_async_copy(ins[a], outs[a].at[me], sems[2].at[a]).wait()

    def forward(self, ins, outs, sems):
        for a in range(self.n):
            for j in range(3):
                self.pass_on(outs, sems, a, j)

    def finish(self, ins, outs, sems):
        for a in range(self.n):
            self.wait_sibling(outs, sems, a)
            for j in range(3):
                self.wait_passed(outs, sems, a, j)
        for a in range(self.n):
            self.wait_sent(ins, outs, sems, a)

    def before(self, step, n_steps, ins, outs, sems):
        pl.when(step == 0)(lambda: self.start(ins, outs, sems))
        pl.when(step == int(self.forward_at * (n_steps - 1)))(lambda: self.forward(ins, outs, sems))

    def after(self, step, n_steps, ins, outs, sems):
        pl.when(step == n_steps - 1)(lambda: self.finish(ins, outs, sems))


class _Scatter:
    def __init__(self, partial, whole=False):
        self.whole = whole
        self.inputs = [partial]
        self.out_shapes = [jax.ShapeDtypeStruct(((N_DEV,) if whole else ()) + partial.shape, partial.dtype)]
        self.scratch = [pltpu.SemaphoreType.DMA((7,)), pltpu.SemaphoreType.DMA((7,)), pltpu.SemaphoreType.DMA((1,))]

    def _copies(self, ins, outs, sems):
        x, y, c, _ = _place()
        me = 4 * x + 2 * y + c
        block = (lambda dev: ins[0]) if self.whole else (lambda dev: ins[0].at[dev])
        mine = pltpu.make_async_copy(block(me), outs[0].at[me], sems[2].at[0])
        remote = [pltpu.make_async_remote_copy(src_ref=block(4 * px + 2 * py + pc), dst_ref=outs[0].at[me], send_sem=sems[0].at[k],
                                               recv_sem=sems[1].at[k], device_id=(px, py, pc), device_id_type=MESH)
                  for k, (px, py, pc) in enumerate(_other_devices(x, y, c))]
        return mine, remote

    def start(self, ins, outs, sems):
        mine, remote = self._copies(ins, outs, sems)
        mine.start()
        for cp in remote:
            cp.start()

    def finish(self, ins, outs, sems):
        mine, remote = self._copies(ins, outs, sems)
        for cp in remote:
            cp.wait()
        mine.wait()

    def before(self, step, n_steps, ins, outs, sems):
        pl.when(step == 0)(lambda: self.start(ins, outs, sems))

    def after(self, step, n_steps, ins, outs, sems):
        pl.when(step == n_steps - 1)(lambda: self.finish(ins, outs, sems))


def _call(body, name, grid, in_specs, out_specs, out_shape, scratch, ins, rider=None):
    single = not isinstance(out_shape, (list, tuple))
    out_shape = [out_shape] if single else list(out_shape)
    out_specs = [out_specs] if single else list(out_specs)
    params = pltpu.CompilerParams(dimension_semantics=("arbitrary",) * len(grid), vmem_limit_bytes=V7X_VMEM_LIMIT)
    if rider is None:
        outs = pl.pallas_call(body, name=name, grid=grid, in_specs=in_specs, out_specs=out_specs, out_shape=out_shape,
                              scratch_shapes=scratch, compiler_params=params)(*ins)
        return (outs[0] if single else outs), None
    n_in, n_out, n_scr = len(ins), len(out_shape), len(scratch)
    r_in, r_out = len(rider.inputs), len(rider.out_shapes)
    n_steps = math.prod(grid)

    def carried(*refs):
        refs = list(refs)
        cut = lambda n: [refs.pop(0) for _ in range(n)]
        b_in, c_in, b_out, c_out, b_scr = cut(n_in), cut(r_in), cut(n_out), cut(r_out), cut(n_scr)
        step = 0
        for axis, size in enumerate(grid):
            step = step * size + pl.program_id(axis)
        rider.before(step, n_steps, c_in, c_out, refs)
        body(*b_in, *b_out, *b_scr)
        rider.after(step, n_steps, c_in, c_out, refs)

    outs = pl.pallas_call(
        carried, name=name, grid=grid, in_specs=list(in_specs) + [ANY] * r_in, out_specs=out_specs + [ANY] * r_out,
        out_shape=out_shape + rider.out_shapes, scratch_shapes=list(scratch) + rider.scratch, compiler_params=params,
    )(*ins, *rider.inputs)
    base = outs[:n_out]
    return (base[0] if single else base), outs[n_out:]


def _arrival_block(j):
    x, y, c = lax.axis_index("x"), lax.axis_index("y"), lax.axis_index("c")
    chip, other_core = j // 2, j % 2
    px = jnp.where((chip == 1) | (chip == 3), 1 - x, x)
    py = jnp.where((chip == 2) | (chip == 3), 1 - y, y)
    pc = jnp.where(other_core == 1, 1 - c, c)
    return 4 * px + 2 * py + pc


def _ffn_gateup_gathering(xb, gate, up_shard, down_shard, name):
    s, d = xb.shape
    fs = up_shard.shape[1]
    tm = min(s, 1024)
    ni = s // tm
    ask_at = max(ni - 2, 0)
    gate_here = gate.ndim == 2
    gather = _Gather(([gate] if gate_here else []) + [up_shard, down_shard])
    n_g = gather.n
    used_here, down = tuple(range(n_g - 1)), n_g - 1

    def body(x_ref, *refs):
        refs = list(refs)
        gate_full = None if gate_here else refs.pop(0)
        shards = [refs.pop(0) for _ in range(n_g)]
        g_ref, u_ref, h_ref = refs.pop(0), refs.pop(0), refs.pop(0)
        fulls = [refs.pop(0) for _ in range(n_g)]
        w_ref, w_sems = refs.pop(0), refs.pop(0)
        sems = refs
        j, i = pl.program_id(0), pl.program_id(1)
        gate_src, up_src = (fulls[0], fulls[1]) if gate_here else (gate_full, fulls[0])

        def load(slot, srcs):
            return [pltpu.make_async_copy(src, w_ref.at[slot, a], w_sems.at[slot, a]) for a, src in enumerate(srcs)]

        @pl.when((j == 0) & (i == 0))
        def _():
            gather.start(shards, fulls, sems, urgent=used_here)
            mine = load(0, (shards[0] if gate_here else gate_full.at[_arrival_block(0)], shards[n_g - 2]))
            for cp in mine:
                cp.start()
            for cp in mine:
                cp.wait()

        for nxt in range(1, N_DEV):
            @pl.when((j == nxt) & (i == 0))
            def _(nxt=nxt):
                for cp in load(nxt % 2, (gate_src.at[0], up_src.at[0])):
                    cp.wait()

        for nxt in range(1, N_DEV):
            @pl.when((j == nxt - 1) & (i == ask_at))
            def _(nxt=nxt):
                for a in used_here:
                    if nxt == 1:
                        gather.wait_sibling(fulls, sems, a)
                    elif nxt % 2 == 0:
                        gather.pass_on(fulls, sems, a, nxt // 2 - 1)
                    else:
                        gather.wait_passed(fulls, sems, a, nxt // 2 - 1)
                block = _arrival_block(nxt)
                for cp in load(nxt % 2, (gate_src.at[block], up_src.at[block])):
                    cp.start()

        @pl.when((j == N_DEV - 1) & (i == ask_at))
        def _():
            for other_chip in range(3):
                gather.pass_on(fulls, sems, down, other_chip)

        x = x_ref[...]
        g = jnp.dot(x, w_ref[j % 2, 0], preferred_element_type=F32)
        u = jnp.dot(x, w_ref[j % 2, 1], preferred_element_type=F32)
        g_ref[...] = g.astype(BF16)
        u_ref[...] = u.astype(BF16)
        h_ref[...] = (g * _sigmoid(g) * u).astype(BF16)

        @pl.when((j == N_DEV - 1) & (i == ni - 1))
        def _():
            gather.wait_sibling(fulls, sems, down)
            for other_chip in range(3):
                gather.wait_passed(fulls, sems, down, other_chip)
            for a in range(n_g):
                gather.wait_sent(shards, fulls, sems, a)

    shp = jax.ShapeDtypeStruct((N_DEV, s, fs), BF16)
    o_spec = pl.BlockSpec((None, tm, fs), lambda j, i: (_arrival_block(j), i, 0))
    ins = ([] if gate_here else [gate]) + gather.inputs
    outs = pl.pallas_call(
        body, name=name, grid=(N_DEV, ni), in_specs=[pl.BlockSpec((tm, d), lambda j, i: (i, 0))] + [ANY] * len(ins),
        out_specs=[o_spec, o_spec, o_spec] + [ANY] * n_g, out_shape=[shp, shp, shp] + gather.out_shapes,
        scratch_shapes=[pltpu.VMEM((2, 2, d, fs), BF16), pltpu.SemaphoreType.DMA((2, 2))] + gather.scratch,
        compiler_params=pltpu.CompilerParams(dimension_semantics=("arbitrary", "arbitrary"), vmem_limit_bytes=V7X_VMEM_LIMIT),
    )(xb, *ins)
    return outs[:3], outs[3:]


def _down_ln(a3, w3, bias, res, res_affine, ln_g, ln_b, scale, name, target=None, rider=None):
    nk, s, tk = a3.shape
    d = w3.shape[2]
    tm = min(s, 256)
    final = target is not None

    def body(*refs):
        refs = list(refs)
        a_ref, w_hbm = refs[:2]
        del refs[:2]
        bias_ref = refs.pop(0) if bias is not None else None
        res_ref = refs.pop(0)
        rg_ref, rb_ref = (refs.pop(0), refs.pop(0)) if res_affine is not None else (None, None)
        g_ref, b_ref = refs.pop(0), refs.pop(0)
        t_ref = refs.pop(0) if final else None
        w_sem = refs.pop()
        w_ref = refs.pop()
        i = pl.program_id(0)
        if final:
            dr_ref, drb_ref, sq_ref, dg_ref, db_ref = refs
        else:
            xh_ref, hb_ref, rstd_ref = refs

        @pl.when(i == 0)
        def _():
            whole = pltpu.make_async_copy(w_hbm, w_ref, w_sem.at[0])
            whole.start()
            whole.wait()
            if final:
                sq_ref[...] = jnp.zeros_like(sq_ref)
                dg_ref[...] = jnp.zeros_like(dg_ref)
                db_ref[...] = jnp.zeros_like(db_ref)

        y = jnp.dot(a_ref[0], w_ref[0], preferred_element_type=F32)
        for k in range(1, nk):
            y = y + jnp.dot(a_ref[k], w_ref[k], preferred_element_type=F32)
        if bias_ref is not None:
            y = y + bias_ref[...]
        for rows in _row_blocks(tm):
            r = res_ref[rows, :]
            if rg_ref is not None:
                r = r * rg_ref[...] + rb_ref[...]
            r = ALPHA * r + scale * y[rows]
            mu = jnp.mean(r, axis=-1, keepdims=True)
            c = r - mu
            var = jnp.mean(c * c, axis=-1, keepdims=True)
            rstd = lax.rsqrt(var + LN_EPS)
            xhat = c * rstd
            h = xhat * g_ref[...] + b_ref[...]
            if not final:
                xh_ref[rows, :] = xhat
                hb_ref[rows, :] = h.astype(BF16)
                rstd_ref[rows, :] = rstd
            else:
                err = h - t_ref[rows, :]
                sq_ref[...] += jnp.sum(err * err, axis=0, keepdims=True)
                dh = err * (1.0 / d)
                dg_ref[...] += jnp.sum(dh * xhat, axis=0, keepdims=True)
                db_ref[...] += jnp.sum(dh, axis=0, keepdims=True)
                dr = _ln_backward(dh, xhat, rstd, g_ref[...])
                dr_ref[rows, :] = dr
                drb_ref[rows, :] = dr.astype(BF16)

    tok = pl.BlockSpec((tm, d), lambda i: (i, 0))
    vec = pl.BlockSpec((1, d), lambda i: (0, 0))
    ins = [a3, w3]
    in_specs = [pl.BlockSpec((nk, tm, tk), lambda i: (0, i, 0)), ANY]
    if bias is not None:
        ins.append(bias)
        in_specs.append(vec)
    ins.append(res)
    in_specs.append(tok)
    if res_affine is not None:
        ins += list(res_affine)
        in_specs += [vec, vec]
    ins += [ln_g, ln_b]
    in_specs += [vec, vec]
    if final:
        ins.append(target)
        in_specs.append(tok)
        out_shape = [jax.ShapeDtypeStruct((s, d), F32), jax.ShapeDtypeStruct((s, d), BF16)] + [jax.ShapeDtypeStruct((1, d), F32)] * 3
        out_specs = [tok, tok, vec, vec, vec]
    else:
        out_shape = [jax.ShapeDtypeStruct((s, d), F32), jax.ShapeDtypeStruct((s, d), BF16), jax.ShapeDtypeStruct((s, 1), F32)]
        out_specs = [tok, tok, pl.BlockSpec((tm, 1), lambda i: (i, 0))]
    scratch = [pltpu.VMEM((nk, tk, d), BF16), pltpu.SemaphoreType.DMA((1,))]
    return _call(body, name, (s // tm,), in_specs, out_specs, out_shape, scratch, ins, rider)


def _proj_in(hb, w3, bias, name, rider=None):
    s, d = hb.shape
    nsh, _, cs = w3.shape
    tm = min(s, 1024)

    def body(h_ref, w_ref, b_ref, z_ref):
        z_ref[...] = (jnp.dot(h_ref[...], w_ref[...], preferred_element_type=F32) + b_ref[...]).astype(BF16)

    in_specs = [pl.BlockSpec((tm, d), lambda i, j: (i, 0)), pl.BlockSpec((None, d, cs), lambda i, j: (j, 0, 0)),
                pl.BlockSpec((1, cs), lambda i, j: (0, j))]
    return _call(body, name, (s // tm, nsh), in_specs, pl.BlockSpec((tm, cs), lambda i, j: (i, j)),
                 jax.ShapeDtypeStruct((s, nsh * cs), BF16), [], [hb, w3, bias], rider)


def _nt_hidden(ab, w3, gate_up, scale, name, rider=None):
    s, kdim = ab.shape
    nj, tn, _ = w3.shape
    tm = min(s, 1024)

    def body(*refs):
        a_ref, w_ref = refs[:2]
        t = scale * lax.dot_general(a_ref[...], w_ref[...], NT_DIMS, preferred_element_type=F32)
        if gate_up is None:
            refs[2][...] = t.astype(BF16)
        else:
            g_ref, u_ref, dg_ref, du_ref = refs[2:]
            g = g_ref[...].astype(F32)
            sg = _sigmoid(g)
            du_ref[...] = (t * g * sg).astype(BF16)
            dg_ref[...] = (t * u_ref[...].astype(F32) * (sg * (1.0 + g * (1.0 - sg)))).astype(BF16)

    hid = pl.BlockSpec((None, tm, tn), lambda i, j: (j, i, 0))
    shp = jax.ShapeDtypeStruct((nj, s, tn), BF16)
    ins = [ab, w3]
    in_specs = [pl.BlockSpec((tm, kdim), lambda i, j: (i, 0)), pl.BlockSpec((None, tn, kdim), lambda i, j: (j, 0, 0))]
    if gate_up is None:
        out_specs, out_shape = hid, shp
    else:
        ins += list(gate_up)
        in_specs += [hid, hid]
        out_specs, out_shape = [hid, hid], [shp, shp]
    return _call(body, name, (s // tm, nj), in_specs, out_specs, out_shape, [], ins, rider)


def _tn_dw(a, a_spec, b, b_spec, nj, m, n, s, tk, scale, name, rider):
    def body(a_ref, b_ref, o_ref, acc_ref):
        k = pl.program_id(1)

        @pl.when(k == 0)
        def _():
            acc_ref[...] = jnp.zeros_like(acc_ref)

        acc_ref[...] += lax.dot_general(a_ref[...], b_ref[...], TN_DIMS, preferred_element_type=F32)

        @pl.when(k == s // tk - 1)
        def _():
            o_ref[...] = (scale * acc_ref[...]).astype(BF16)

    return _call(body, name, (nj, s // tk), [a_spec, b_spec], pl.BlockSpec((None, m, n), lambda j, k: (j, 0, 0)),
                 jax.ShapeDtypeStruct((nj, m, n), BF16), [pltpu.VMEM((m, n), F32)], [a, b], rider)


def _dw_hidden_rows(hid3, db, scale, name, rider=None):
    nj, s, fs = hid3.shape
    d = db.shape[1]
    tk = min(s, DW_TOKENS)
    return _tn_dw(hid3, pl.BlockSpec((None, tk, fs), lambda j, k: (j, k, 0)), db, pl.BlockSpec((tk, d), lambda j, k: (k, 0)),
                  nj, fs, d, s, tk, scale, name, rider)


def _dw_hidden_cols(ab, hid3, name, rider=None):
    nj, s, fs = hid3.shape
    d = ab.shape[1]
    tk = min(s, DW_TOKENS)
    return _tn_dw(ab, pl.BlockSpec((tk, d), lambda j, k: (k, 0)), hid3, pl.BlockSpec((None, tk, fs), lambda j, k: (j, k, 0)),
                  nj, d, fs, s, tk, 1.0, name, rider)


def _dw_cols(ab, dz, nj, name, rider=None):
    s, d = ab.shape
    cs = dz.shape[1] // nj
    tk = min(s, DW_TOKENS)
    return _tn_dw(ab, pl.BlockSpec((tk, d), lambda j, k: (k, 0)), dz, pl.BlockSpec((tk, cs), lambda j, k: (k, j)),
                  nj, d, cs, s, tk, 1.0, name, rider)


def _nt_acc(pairs, nk, dres, ln, name, rider=None):
    s, d = dres.shape
    tm = min(s, 512)
    np_ = len(pairs)

    def body(*refs):
        refs = list(refs)
        pair_refs = [(refs[2 * p], refs[2 * p + 1]) for p in range(np_)]
        del refs[:2 * np_]
        dres_ref = refs.pop(0)
        if ln is not None:
            xh_ref, rstd_ref, gain_ref = refs.pop(0), refs.pop(0), refs.pop(0)
        acc_ref = refs.pop()
        i, k = pl.program_id(0), pl.program_id(1)

        @pl.when(k == 0)
        def _():
            acc_ref[...] = jnp.zeros_like(acc_ref)

        for a_ref, w_ref in pair_refs:
            acc_ref[...] += lax.dot_general(a_ref[...], w_ref[...], NT_DIMS, preferred_element_type=F32)

        @pl.when(k == nk - 1)
        def _():
            if ln is not None:
                dr_ref, drb_ref, dg_ref, db_ref, sum_ref = refs

                @pl.when(i == 0)
                def _():
                    dg_ref[...] = jnp.zeros_like(dg_ref)
                    db_ref[...] = jnp.zeros_like(db_ref)
                    sum_ref[...] = jnp.zeros_like(sum_ref)

            for rows in _row_blocks(tm):
                dh = ALPHA * dres_ref[rows, :] + acc_ref[rows, :]
                if ln is None:
                    refs[0][rows, :] = dh
                else:
                    xhat = xh_ref[rows, :]
                    dg_ref[...] += jnp.sum(dh * xhat, axis=0, keepdims=True)
                    db_ref[...] += jnp.sum(dh, axis=0, keepdims=True)
                    dr = _ln_backward(dh, xhat, rstd_ref[rows, :], gain_ref[...])
                    sum_ref[...] += jnp.sum(dr, axis=0, keepdims=True)
                    dr_ref[rows, :] = dr
                    drb_ref[rows, :] = dr.astype(BF16)

    tok = pl.BlockSpec((tm, d), lambda i, k: (i, 0))
    vec = pl.BlockSpec((1, d), lambda i, k: (0, 0))
    ins, in_specs = [], []
    for a, a_spec, w, w_spec in pairs:
        ins += [a, w]
        in_specs += [a_spec, w_spec]
    ins.append(dres)
    in_specs.append(tok)
    if ln is None:
        out_shape, out_specs = jax.ShapeDtypeStruct((s, d), F32), tok
    else:
        ins += list(ln)
        in_specs += [tok, pl.BlockSpec((tm, 1), lambda i, k: (i, 0)), vec]
        out_shape = [jax.ShapeDtypeStruct((s, d), F32), jax.ShapeDtypeStruct((s, d), BF16)] + [jax.ShapeDtypeStruct((1, d), F32)] * 3
        out_specs = [tok, tok, vec, vec, vec]
    return _call(body, name, (s // tm, nk), in_specs, out_specs, out_shape, [pltpu.VMEM((tm, d), F32)], ins, rider)


def _proj_in_bwd_ln(dz, w3, dres, ln, name):
    s, d = dres.shape
    nj, _, cs = w3.shape
    tm = min(s, 256)

    def body(dz_ref, w_hbm, dres_ref, xh_ref, rstd_ref, gain_ref, dr_ref, drb_ref, dg_ref, db_ref, w_ref, w_sem):
        @pl.when(pl.program_id(0) == 0)
        def _():
            whole = pltpu.make_async_copy(w_hbm, w_ref, w_sem.at[0])
            whole.start()
            whole.wait()
            dg_ref[...] = jnp.zeros_like(dg_ref)
            db_ref[...] = jnp.zeros_like(db_ref)

        acc = lax.dot_general(dz_ref[:, 0:cs], w_ref[0], NT_DIMS, preferred_element_type=F32)
        for j in range(1, nj):
            acc = acc + lax.dot_general(dz_ref[:, j * cs:(j + 1) * cs], w_ref[j], NT_DIMS, preferred_element_type=F32)
        for rows in _row_blocks(tm):
            dh = ALPHA * dres_ref[rows, :] + acc[rows]
            xhat = xh_ref[rows, :]
            dg_ref[...] += jnp.sum(dh * xhat, axis=0, keepdims=True)
            db_ref[...] += jnp.sum(dh, axis=0, keepdims=True)
            dr = _ln_backward(dh, xhat, rstd_ref[rows, :], gain_ref[...])
            dr_ref[rows, :] = dr
            drb_ref[rows, :] = dr.astype(BF16)

    tok = pl.BlockSpec((tm, d), lambda i: (i, 0))
    vec = pl.BlockSpec((1, d), lambda i: (0, 0))
    in_specs = [pl.BlockSpec((tm, nj * cs), lambda i: (i, 0)), ANY, tok, tok, pl.BlockSpec((tm, 1), lambda i: (i, 0)), vec]
    out_shape = [jax.ShapeDtypeStruct((s, d), F32), jax.ShapeDtypeStruct((s, d), BF16)] + [jax.ShapeDtypeStruct((1, d), F32)] * 2
    scratch = [pltpu.VMEM(w3.shape, BF16), pltpu.SemaphoreType.DMA((1,))]
    return _call(body, name, (s // tm,), in_specs, [tok, tok, vec, vec], out_shape, scratch, [dz, w3, dres] + list(ln))[0]


def _hidden_pairs(dg3, wg3, du3, wu3, tm):
    nj, s, fs = dg3.shape
    d = wg3.shape[1]
    a_spec = pl.BlockSpec((None, tm, fs), lambda i, k: (k, i, 0))
    w_spec = pl.BlockSpec((None, d, fs), lambda i, k: (k, 0, 0))
    return [(dg3, a_spec, wg3, w_spec), (du3, a_spec, wu3, w_spec)]


def _shift_rows_down(v, halo, k, row):
    out = pltpu.roll(v, k, 0)
    hr = halo.shape[0]
    for r in range(k):
        out = jnp.where(row == r, halo[hr - k + r:hr - k + r + 1, :], out)
    return out


def _shift_rows_up(v, halo, k, row):
    t = v.shape[0]
    out = pltpu.roll(v, t - k, 0)
    for r in range(k):
        out = jnp.where(row == t - k + r, halo[r:r + 1, :], out)
    return out


def _sgu_head_forward(z_ref, h, da, gv_ref, bv_ref):
    zu = z_ref[:, h * HEAD:(h + 1) * HEAD].astype(F32)
    zv = z_ref[:, da + h * HEAD:da + (h + 1) * HEAD].astype(F32)
    u = _gelu(zu)
    v = _gelu(zv)
    mu = jnp.mean(v, axis=-1, keepdims=True)
    c = v - mu
    rstd = lax.rsqrt(jnp.mean(c * c, axis=-1, keepdims=True) + LN_EPS)
    vhat = c * rstd
    vln = (vhat * gv_ref[h:h + 1, :] + bv_ref[h:h + 1, :]).astype(BF16)
    return zu, zv, u, vhat, rstd, vln


def _mixer_fwd(z, ws_masked, bs_wide, gv, bv, cw, name):
    s, zc = z.shape
    da = zc // 5
    nh = da // HEAD
    tm = min(s, 512)
    hb = tm // BF16_ROWS

    def body(z_ref, pc_ref, px_ref, ws_ref, bs_ref, gv_ref, bv_ref, cw_ref, y_ref):
        i = pl.program_id(0)
        for h in range(nh):
            _, _, u, _, _, vln = _sgu_head_forward(z_ref, h, da, gv_ref, bv_ref)
            for n in range(tm // CHUNK):
                rows = slice(n * CHUNK, (n + 1) * CHUNK)
                mixed = jnp.dot(ws_ref[h], vln[rows], preferred_element_type=F32) + bs_ref[h]
                y_ref[0, rows, h * HEAD:(h + 1) * HEAD] = (u[rows] * mixed).astype(BF16)
        gate_b = z_ref[:, 2 * da:3 * da].astype(F32)
        hc = z_ref[:, 3 * da:4 * da].astype(F32) * z_ref[:, 4 * da:5 * da].astype(F32)
        halo = jnp.where(i > 0, pc_ref[...].astype(F32) * px_ref[...].astype(F32), 0.0)
        row = lax.broadcasted_iota(jnp.int32, (tm, da), 0)
        y = cw_ref[0:1, :] * _shift_rows_down(hc, halo, 2, row) + cw_ref[1:2, :] * _shift_rows_down(hc, halo, 1, row) + cw_ref[2:3, :] * hc
        y_ref[1] = (gate_b * y).astype(BF16)

    prev = lambda col: pl.BlockSpec((BF16_ROWS, da), lambda i: (jnp.maximum(i * hb - 1, 0), col))
    in_specs = [pl.BlockSpec((tm, zc), lambda i: (i, 0)), prev(3), prev(4), _row((nh, CHUNK, CHUNK)), _row((nh, CHUNK, HEAD)),
                _row((nh, HEAD)), _row((nh, HEAD)), _row((CONV_TAPS, da))]
    return _call(body, name, (s // tm,), in_specs, pl.BlockSpec((2, tm, da), lambda i: (0, i, 0)),
                 jax.ShapeDtypeStruct((2, s, da), BF16), [], [z, z, z, ws_masked, bs_wide, gv, bv, cw])[0]


def _mixer_bwd(z, dy, ws_masked, bs_wide, gv, bv, cw, name, rider=None):
    s, zc = z.shape
    da = zc // 5
    nh = da // HEAD
    tm = min(s, 512)
    hb = tm // BF16_ROWS
    nblk = s // tm

    def body(z_ref, pc_ref, px_ref, nb_ref, dy_ref, ndy_ref, ws_ref, bs_ref, gv_ref, bv_ref, cw_ref,
             dz_ref, dws_ref, dbs_ref, dgv_ref, dbv_ref, dcw_ref, dbin_ref):
        i = pl.program_id(0)

        @pl.when(i == 0)
        def _():
            for ref in (dws_ref, dbs_ref, dgv_ref, dbv_ref, dcw_ref, dbin_ref):
                ref[...] = jnp.zeros_like(ref)

        causal = lax.broadcasted_iota(jnp.int32, (CHUNK, CHUNK), 0) >= lax.broadcasted_iota(jnp.int32, (CHUNK, CHUNK), 1)
        for h in range(nh):
            zu, zv, u, vhat, rstd, vln = _sgu_head_forward(z_ref, h, da, gv_ref, bv_ref)
            dya = dy_ref[0, :, h * HEAD:(h + 1) * HEAD].astype(F32)
            w = ws_ref[h]
            du_parts, dvln_parts = [], []
            for n in range(tm // CHUNK):
                rows = slice(n * CHUNK, (n + 1) * CHUNK)
                mixed = jnp.dot(w, vln[rows], preferred_element_type=F32) + bs_ref[h]
                du_parts.append(dya[rows] * mixed)
                dmix = dya[rows] * u[rows]
                dmix_b = dmix.astype(BF16)
                dws_ref[h] += jnp.where(causal, lax.dot_general(dmix_b, vln[rows], NT_DIMS, preferred_element_type=F32), 0.0)
                dbs_ref[h] += dmix
                dvln_parts.append(lax.dot_general(w, dmix_b, TN_DIMS, preferred_element_type=F32))
            du = jnp.concatenate(du_parts, axis=0)
            dvln = jnp.concatenate(dvln_parts, axis=0)
            dgv_ref[h:h + 1, :] += jnp.sum(dvln * vhat, axis=0, keepdims=True)
            dbv_ref[h:h + 1, :] += jnp.sum(dvln, axis=0, keepdims=True)
            dv = _ln_backward(dvln, vhat, rstd, gv_ref[h:h + 1, :])
            dzu = du * _gelu_grad(zu)
            dzv = dv * _gelu_grad(zv)
            ucols = slice(h * HEAD, (h + 1) * HEAD)
            vcols = slice(da + h * HEAD, da + (h + 1) * HEAD)
            dz_ref[:, ucols] = dzu.astype(BF16)
            dz_ref[:, vcols] = dzv.astype(BF16)
            dbin_ref[:, ucols] += jnp.sum(dzu, axis=0, keepdims=True)
            dbin_ref[:, vcols] += jnp.sum(dzv, axis=0, keepdims=True)

        gate_b = z_ref[:, 2 * da:3 * da].astype(F32)
        gate_c = z_ref[:, 3 * da:4 * da].astype(F32)
        xt = z_ref[:, 4 * da:5 * da].astype(F32)
        hc = gate_c * xt
        halo = jnp.where(i > 0, pc_ref[...].astype(F32) * px_ref[...].astype(F32), 0.0)
        row = lax.broadcasted_iota(jnp.int32, (tm, da), 0)
        sh1 = _shift_rows_down(hc, halo, 1, row)
        sh2 = _shift_rows_down(hc, halo, 2, row)
        y = cw_ref[0:1, :] * sh2 + cw_ref[1:2, :] * sh1 + cw_ref[2:3, :] * hc
        dyb = dy_ref[1].astype(F32)
        dconv = dyb * gate_b
        nhalo = jnp.where(i < nblk - 1, ndy_ref[...].astype(F32) * nb_ref[...].astype(F32), 0.0)
        dhc = cw_ref[2:3, :] * dconv + cw_ref[1:2, :] * _shift_rows_up(dconv, nhalo, 1, row) + cw_ref[0:1, :] * _shift_rows_up(dconv, nhalo, 2, row)
        dcw_ref[0:1, :] += jnp.sum(dconv * sh2, axis=0, keepdims=True)
        dcw_ref[1:2, :] += jnp.sum(dconv * sh1, axis=0, keepdims=True)
        dcw_ref[2:3, :] += jnp.sum(dconv * hc, axis=0, keepdims=True)
        for col, val in ((2, dyb * y), (3, dhc * xt), (4, dhc * gate_c)):
            cols = slice(col * da, (col + 1) * da)
            dz_ref[:, cols] = val.astype(BF16)
            dbin_ref[:, cols] += jnp.sum(val, axis=0, keepdims=True)

        @pl.when(i == nblk - 1)
        def _():
            for h in range(nh):
                dbs_ref[h] = jnp.broadcast_to(jnp.sum(dbs_ref[h], axis=1, keepdims=True), (CHUNK, HEAD))

    prev = lambda col: pl.BlockSpec((BF16_ROWS, da), lambda i: (jnp.maximum(i * hb - 1, 0), col))
    nxt = lambda i: jnp.minimum((i + 1) * hb, s // BF16_ROWS - 1)
    in_specs = [pl.BlockSpec((tm, zc), lambda i: (i, 0)), prev(3), prev(4), pl.BlockSpec((BF16_ROWS, da), lambda i: (nxt(i), 2)),
                pl.BlockSpec((2, tm, da), lambda i: (0, i, 0)), pl.BlockSpec((None, BF16_ROWS, da), lambda i: (1, nxt(i), 0)),
                _row((nh, CHUNK, CHUNK)), _row((nh, CHUNK, HEAD)), _row((nh, HEAD)), _row((nh, HEAD)), _row((CONV_TAPS, da))]
    out_specs = [pl.BlockSpec((tm, zc), lambda i: (i, 0)), _row((nh, CHUNK, CHUNK)), _row((nh, CHUNK, HEAD)), _row((nh, HEAD)),
                 _row((nh, HEAD)), _row((8, da)), _row((1, zc))]
    out_shape = [jax.ShapeDtypeStruct((s, zc), BF16), jax.ShapeDtypeStruct((nh, CHUNK, CHUNK), F32),
                 jax.ShapeDtypeStruct((nh, CHUNK, HEAD), F32), jax.ShapeDtypeStruct((nh, HEAD), F32),
                 jax.ShapeDtypeStruct((nh, HEAD), F32), jax.ShapeDtypeStruct((8, da), F32), jax.ShapeDtypeStruct((1, zc), F32)]
    return _call(body, name, (nblk,), in_specs, out_specs, out_shape, [], [z, z, z, z, dy, dy, ws_masked, bs_wide, gv, bv, cw], rider)


def _adamw(gparts, w, m, v, name):
    n, r, c = gparts.shape
    tr = r // 4 if (r // 4) % BF16_ROWS == 0 else r

    def body(g_ref, w_ref, m_ref, v_ref, go_ref, d_ref, mo_ref, vo_ref):
        g = g_ref[0].astype(F32)
        for q in range(1, n):
            g = g + g_ref[q].astype(F32)
        m_new = ADAM_B1 * m_ref[...] + (1.0 - ADAM_B1) * g
        v_new = ADAM_B2 * v_ref[...] + (1.0 - ADAM_B2) * (g * g)
        m_hat = m_new / (1.0 - ADAM_B1 ** ADAM_STEP)
        v_hat = v_new / (1.0 - ADAM_B2 ** ADAM_STEP)
        go_ref[...] = g
        d_ref[...] = -ADAM_LR * (m_hat / (jnp.sqrt(v_hat) + ADAM_EPS) + ADAM_WD * w_ref[...])
        mo_ref[...] = m_new
        vo_ref[...] = v_new

    blk = pl.BlockSpec((tr, c), lambda i: (i, 0))
    shp = jax.ShapeDtypeStruct((r, c), F32)
    return _call(body, name, (r // tr,), [pl.BlockSpec((n, tr, c), lambda i: (0, i, 0)), blk, blk, blk], [blk] * 4, [shp] * 4, [],
                 [gparts, w, m, v])[0]


def _rows128(a):
    return a.reshape(-1, LANES)


def kernel(x, ffa_gate, ffa_up, ffa_down, ln_a_g, ln_a_b, w_in, b_in, w_s, b_s, ln_v_g, ln_v_b, conv_w, w_out, b_out, ln_m_g, ln_m_b, ffc_gate, ffc_up, ffc_down, ln_c_g, ln_c_b, loss_target, m_ffa_gate, m_ffa_up, m_ffa_down, m_ln_a_g, m_ln_a_b, m_w_in, m_b_in, m_w_s, m_b_s, m_ln_v_g, m_ln_v_b, m_conv_w, m_w_out, m_b_out, m_ln_m_g, m_ln_m_b, m_ffc_gate, m_ffc_up, m_ffc_down, m_ln_c_g, m_ln_c_b, v_ffa_gate, v_ffa_up, v_ffa_down, v_ln_a_g, v_ln_a_b, v_w_in, v_b_in, v_w_s, v_b_s, v_ln_v_g, v_ln_v_b, v_conv_w, v_w_out, v_b_out, v_ln_m_g, v_ln_m_b, v_ffc_gate, v_ffc_up, v_ffc_down, v_ln_c_g, v_ln_c_b):
    x2, target = x[0], loss_target[0]
    s, d = x2.shape
    da = d // 2
    nh = da // HEAD
    me = 4 * lax.axis_index("x") + 2 * lax.axis_index("y") + lax.axis_index("c")

    big = dict(ffa_gate=ffa_gate, ffa_up=ffa_up, ffa_down=ffa_down, w_in=w_in, w_out=w_out, ffc_gate=ffc_gate, ffc_up=ffc_up, ffc_down=ffc_down)
    big_m = dict(ffa_gate=m_ffa_gate, ffa_up=m_ffa_up, ffa_down=m_ffa_down, w_in=m_w_in, w_out=m_w_out, ffc_gate=m_ffc_gate, ffc_up=m_ffc_up, ffc_down=m_ffc_down)
    big_v = dict(ffa_gate=v_ffa_gate, ffa_up=v_ffa_up, ffa_down=v_ffa_down, w_in=v_w_in, w_out=v_w_out, ffc_gate=v_ffc_gate, ffc_up=v_ffc_up, ffc_down=v_ffc_down)
    shard = {k: w[0].astype(BF16) for k, w in big.items()}
    conv_rows = jnp.pad(conv_w[0], ((0, 8 - CONV_TAPS), (0, 0)))

    tril = jnp.tril(jnp.ones((CHUNK, CHUNK), dtype=bool))
    ws_masked = jnp.where(tril[None], w_s[0], 0.0).astype(BF16)
    bs_wide = jnp.broadcast_to(b_s[0][:, :, None], (nh, CHUNK, HEAD))
    gv, bv = ln_v_g.reshape(nh, HEAD), ln_v_b.reshape(nh, HEAD)

    full = {}
    xb = x2.astype(BF16)
    (g_a, u_a, hid_a), (full["ffa_gate"], full["ffa_up"], full["ffa_down"]) = _ffn_gateup_gathering(
        xb, shard["ffa_gate"], shard["ffa_up"], shard["ffa_down"], "ffa_gateup")
    (xhat1, h1b, rstd1), (full["w_in"], full["w_out"], conv_full) = _down_ln(
        hid_a, full["ffa_down"], None, x2, None, ln_a_g, ln_a_b, 0.5, "ffa_down_ln", rider=_Gather([shard["w_in"], shard["w_out"], conv_rows]))
    cw = jnp.transpose(conv_full[:, :CONV_TAPS, :], (1, 0, 2)).reshape(CONV_TAPS, da)
    w_out2 = full["w_out"].reshape(2, da, d)
    z, (full["ffc_gate"],) = _proj_in(h1b, full["w_in"], b_in, "proj_in", _Gather([shard["ffc_gate"]]))
    ycat = _mixer_fwd(z, ws_masked, bs_wide, gv, bv, cw, "mixer_fwd")
    (xhat2, h2b, rstd2), _ = _down_ln(ycat, w_out2, b_out, xhat1, (ln_a_g, ln_a_b), ln_m_g, ln_m_b, 1.0, "proj_out_ln")
    (g_c, u_c, hid_c), (full["ffc_up"], full["ffc_down"]) = _ffn_gateup_gathering(
        h2b, full["ffc_gate"], shard["ffc_up"], shard["ffc_down"], "ffc_gateup")
    (dr3, dr3b, sq_err, d_ln_c_g, d_ln_c_b), _ = _down_ln(hid_c, full["ffc_down"], None, xhat2, (ln_m_g, ln_m_b), ln_c_g, ln_c_b, 0.5,
                                                          "ffc_down_ln_loss", target=target)
    loss = lax.psum((0.5 / d) * jnp.sum(sq_err), ("x", "y", "c"))

    tm = min(s, 512)
    landed = {}
    (dg_c, du_c), _ = _nt_hidden(dr3b, full["ffc_down"], (g_c, u_c), 0.5, "ffc_bwd_hidden")
    part, _ = _dw_hidden_rows(hid_c, dr3b, 0.5, "ffc_dw_down")
    part, (landed["ffc_down"],) = _dw_hidden_cols(h2b, dg_c, "ffc_dw_gate", _Scatter(part))
    part, (landed["ffc_gate"],) = _dw_hidden_cols(h2b, du_c, "ffc_dw_up", _Scatter(part))
    (dr2, dr2b, d_ln_m_g, d_ln_m_b, d_b_out), (landed["ffc_up"],) = _nt_acc(
        _hidden_pairs(dg_c, full["ffc_gate"], du_c, full["ffc_up"], tm), N_DEV, dr3, (xhat2, rstd2, ln_m_g), "ffc_bwd_input_ln", _Scatter(part))
    dycat, _ = _nt_hidden(dr2b, w_out2, None, 1.0, "proj_out_bwd")
    part, _ = _dw_hidden_rows(ycat, dr2b, 1.0, "proj_out_dw")
    (dz, d_w_s, d_b_s_wide, d_gv, d_bv, d_cw, d_b_in), (landed["w_out"],) = _mixer_bwd(
        z, dycat, ws_masked, bs_wide, gv, bv, cw, "mixer_bwd", _Scatter(part.reshape(N_DEV, d // N_DEV, d)))
    dr1, dr1b, d_ln_a_g, d_ln_a_b = _proj_in_bwd_ln(dz, full["w_in"], dr2, (xhat1, rstd1, ln_a_g), "proj_in_bwd_ln")
    small_g = dict(ln_a_g=d_ln_a_g, ln_a_b=d_ln_a_b, b_in=d_b_in, w_s=d_w_s, b_s=d_b_s_wide[:, :, 0], ln_v_g=d_gv, ln_v_b=d_bv, b_out=d_b_out,
                   ln_m_g=d_ln_m_g, ln_m_b=d_ln_m_b, ln_c_g=d_ln_c_g, ln_c_b=d_ln_c_b)
    pack = jnp.concatenate([_rows128(g) for g in small_g.values()] + [_rows128(d_cw[:CONV_TAPS])], axis=0)
    part, (packs,) = _dw_cols(h1b, dz, N_DEV, "proj_in_dw", _Scatter(pack, whole=True))
    (dg_a, du_a), (landed["w_in"],) = _nt_hidden(dr1b, full["ffa_down"], (g_a, u_a), 0.5, "ffa_bwd_hidden", _Scatter(part))
    part, _ = _dw_hidden_rows(hid_a, dr1b, 0.5, "ffa_dw_down")
    part, (landed["ffa_down"],) = _dw_hidden_cols(xb, dg_a, "ffa_dw_gate", _Scatter(part))
    part, (landed["ffa_gate"],) = _dw_hidden_cols(xb, du_a, "ffa_dw_up", _Scatter(part))
    grad_x, (landed["ffa_up"],) = _nt_acc(_hidden_pairs(dg_a, full["ffa_gate"], du_a, full["ffa_up"], tm), N_DEV, dr1, None, "ffa_bwd_input",
                                          _Scatter(part))

    grads, deltas, new_m, new_v = {}, {}, {}, {}
    for k in big:
        out = _adamw(landed[k], big[k][0], big_m[k][0], big_v[k][0], "adamw_" + k)
        grads[k], deltas[k], new_m[k], new_v[k] = (o.reshape(big[k].shape) for o in out)

    small = dict(ln_a_g=ln_a_g, ln_a_b=ln_a_b, b_in=b_in, w_s=w_s, b_s=b_s, ln_v_g=ln_v_g, ln_v_b=ln_v_b, b_out=b_out,
                 ln_m_g=ln_m_g, ln_m_b=ln_m_b, ln_c_g=ln_c_g, ln_c_b=ln_c_b)
    small_m = dict(ln_a_g=m_ln_a_g, ln_a_b=m_ln_a_b, b_in=m_b_in, w_s=m_w_s, b_s=m_b_s, ln_v_g=m_ln_v_g, ln_v_b=m_ln_v_b, b_out=m_b_out,
                   ln_m_g=m_ln_m_g, ln_m_b=m_ln_m_b, ln_c_g=m_ln_c_g, ln_c_b=m_ln_c_b)
    small_v = dict(ln_a_g=v_ln_a_g, ln_a_b=v_ln_a_b, b_in=v_b_in, w_s=v_w_s, b_s=v_b_s, ln_v_g=v_ln_v_g, ln_v_b=v_ln_v_b, b_out=v_b_out,
                   ln_m_g=v_ln_m_g, ln_m_b=v_ln_m_b, ln_c_g=v_ln_c_g, ln_c_b=v_ln_c_b)
    snames = list(small)
    assert snames == list(small_g)
    rows = [_rows128(small[k]).shape[0] for k in snames]
    n_rep = sum(rows)
    cw_cols = conv_w.shape[2]
    conv_parts = lax.dynamic_slice_in_dim(packs[:, n_rep:].reshape(N_DEV, CONV_TAPS, da), me * cw_cols, cw_cols, axis=2)
    pad_conv = lambda a: jnp.pad(a, ((0, 8 - CONV_TAPS), (0, 0)))
    stack = lambda tree, conv: jnp.concatenate([_rows128(tree[k]) for k in snames] + [pad_conv(conv)], axis=0)
    small_parts = jnp.concatenate([packs[:, :n_rep], jnp.pad(conv_parts, ((0, 0), (0, 8 - CONV_TAPS), (0, 0)))], axis=1)
    out = _adamw(small_parts, stack(small, conv_w[0]), stack(small_m, m_conv_w[0]), stack(small_v, v_conv_w[0]), "adamw_small")
    for tree, packed in zip((grads, deltas, new_m, new_v), out):
        at = 0
        for k, r in zip(snames, rows):
            tree[k] = packed[at:at + r].reshape(small[k].shape)
            at += r
        tree["conv_w"] = packed[at:at + CONV_TAPS].reshape(conv_w.shape)

    order = ["ffa_gate", "ffa_up", "ffa_down", "ln_a_g", "ln_a_b", "w_in", "b_in", "w_s", "b_s", "ln_v_g", "ln_v_b", "conv_w", "w_out", "b_out",
             "ln_m_g", "ln_m_b", "ffc_gate", "ffc_up", "ffc_down", "ln_c_g", "ln_c_b"]
    return (loss, grad_x[None], *[grads[k] for k in order], *[deltas[k] for k in order], *[new_m[k] for k in order], *[new_v[k] for k in order])
```

```python
import math

import jax
import jax.numpy as jnp
from jax import lax
from jax.experimental import pallas as pl
from jax.experimental.pallas import tpu as pltpu

BF16 = jnp.bfloat16
F32 = jnp.float32
MESH = pl.DeviceIdType.MESH

N_DEV = 8
HEAD = 128
CHUNK = 128
CONV_TAPS = 3
LN_EPS = 1e-5
ALPHA = float(2 ** 0.25)
GELU_C = 0.7978845608028654
GELU_A = 0.044715
ADAM_LR, ADAM_B1, ADAM_B2, ADAM_EPS, ADAM_WD, ADAM_STEP = 0.001, 0.9, 0.999, 1e-08, 0.01, 10
V7X_VMEM_LIMIT = 56 * 1024 * 1024
LANES = 128
BF16_ROWS = 16
TRANSPOSED = ("ffa_gate", "ffa_up", "ffc_gate", "ffc_up")
DW_TOKENS = 2048

NT_DIMS = (((1,), (1,)), ((), ()))
TN_DIMS = (((0,), (0,)), ((), ()))
ANY = pl.BlockSpec(memory_space=pl.ANY)


def _gelu(x):
    return 0.5 * x * (1.0 + jnp.tanh(GELU_C * (x + GELU_A * x * x * x)))


def _gelu_grad(x):
    t = jnp.tanh(GELU_C * (x + GELU_A * x * x * x))
    return 0.5 * (1.0 + t) + 0.5 * x * (1.0 - t * t) * GELU_C * (1.0 + 3.0 * GELU_A * x * x)


def _sigmoid(x):
    return 1.0 / (1.0 + jnp.exp(-x))


def _row(shape):
    return pl.BlockSpec(shape, lambda *_: (0,) * len(shape))


def _row_blocks(tm, rows=128):
    rows = min(rows, tm)
    return [slice(r, r + rows) for r in range(0, tm, rows)]


def _ln_backward(dh, xhat, rstd, gain):
    dxh = dh * gain
    m1 = jnp.mean(dxh, axis=-1, keepdims=True)
    m2 = jnp.mean(dxh * xhat, axis=-1, keepdims=True)
    return rstd * (dxh - m1 - xhat * m2)


def _place():
    x, y, c = lax.axis_index("x"), lax.axis_index("y"), lax.axis_index("c")
    return x, y, c, [(1 - x, y), (x, 1 - y), (1 - x, 1 - y)]


def _other_devices(x, y, c):
    flips = [(bx, by, bc) for bx in (0, 1) for by in (0, 1) for bc in (0, 1)][1:]
    return [(1 - x if bx else x, 1 - y if by else y, 1 - c if bc else c) for bx, by, bc in flips]


class _Gather:
    def __init__(self, shards, forward_at=0.75):
        n = len(shards)
        self.n, self.forward_at = n, forward_at
        self.inputs = list(shards)
        self.out_shapes = [jax.ShapeDtypeStruct((N_DEV,) + a.shape, a.dtype) for a in shards]
        self.scratch = [pltpu.SemaphoreType.DMA((n, 7)), pltpu.SemaphoreType.DMA((n, 7)), pltpu.SemaphoreType.DMA((n,))]

    def _copy(self, outs, sems, a, k, block, to, src=None):
        dst = outs[a].at[block]
        return pltpu.make_async_remote_copy(src_ref=dst if src is None else src, dst_ref=dst, send_sem=sems[0].at[a, k],
                                            recv_sem=sems[1].at[a, k], device_id=to, device_id_type=MESH)

    def start(self, ins, outs, sems, urgent=None):
        x, y, c, chips = _place()
        me = 4 * x + 2 * y + c
        for a in range(self.n):
            pltpu.make_async_copy(ins[a], outs[a].at[me], sems[2].at[a]).start()
        urgent = list(range(self.n)) if urgent is None else list(urgent)
        for group in (urgent, [a for a in range(self.n) if a not in urgent]):
            for a in group:
                self._copy(outs, sems, a, 0, me, (x, y, 1 - c), src=ins[a]).start()
                for j in (0, 1):
                    self._copy(outs, sems, a, 1 + j, me, (*chips[j], c), src=ins[a]).start()
            for a in group:
                self._copy(outs, sems, a, 3, me, (*chips[2], c), src=ins[a]).start()

    def wait_sibling(self, outs, sems, a):
        x, y, c, _ = _place()
        self._copy(outs, sems, a, 0, 4 * x + 2 * y + 1 - c, (x, y, 1 - c)).wait_recv()

    def pass_on(self, outs, sems, a, j):
        x, y, c, chips = _place()
        block = 4 * chips[j][0] + 2 * chips[j][1] + c
        self._copy(outs, sems, a, 1 + j, block, (x, y, 1 - c)).wait_recv()
        self._copy(outs, sems, a, 4 + j, block, (x, y, 1 - c)).start()

    def wait_passed(self, outs, sems, a, j):
        x, y, c, chips = _place()
        self._copy(outs, sems, a, 4 + j, 4 * chips[j][0] + 2 * chips[j][1] + 1 - c, (x, y, 1 - c)).wait_recv()

    def wait_sent(self, ins, outs, sems, a):
        x, y, c, _ = _place()
        me = 4 * x + 2 * y + c
        for k in range(7):
            self._copy(outs, sems, a, k, me, (x, y, 1 - c), src=ins[a]).wait_send()
        pltpu.make_async_copy(ins[a], outs[a].at[me], sems[2].at[a]).wait()

    def forward(self, ins, outs, sems):
        for a in range(self.n):
            for j in range(3):
                self.pass_on(outs, sems, a, j)

    def finish(self, ins, outs, sems):
        for a in range(self.n):
            self.wait_sibling(outs, sems, a)
            for j in range(3):
                self.wait_passed(outs, sems, a, j)
        for a in range(self.n):
            self.wait_sent(ins, outs, sems, a)

    def before(self, step, n_steps, ins, outs, sems):
        pl.when(step == 0)(lambda: self.start(ins, outs, sems))
        pl.when(step == int(self.forward_at * (n_steps - 1)))(lambda: self.forward(ins, outs, sems))

    def after(self, step, n_steps, ins, outs, sems):
        pl.when(step == n_steps - 1)(lambda: self.finish(ins, outs, sems))


class _Scatter:
    def __init__(self, partial, whole=False):
        self.whole = whole
        self.inputs = [partial]
        self.out_shapes = [jax.ShapeDtypeStruct(((N_DEV,) if whole else ()) + partial.shape, partial.dtype)]
        self.scratch = [pltpu.SemaphoreType.DMA((7,)), pltpu.SemaphoreType.DMA((7,)), pltpu.SemaphoreType.DMA((1,))]

    def _copies(self, ins, outs, sems):
        x, y, c, _ = _place()
        me = 4 * x + 2 * y + c
        block = (lambda dev: ins[0]) if self.whole else (lambda dev: ins[0].at[dev])
        mine = pltpu.make_async_copy(block(me), outs[0].at[me], sems[2].at[0])
        remote = [pltpu.make_async_remote_copy(src_ref=block(4 * px + 2 * py + pc), dst_ref=outs[0].at[me], send_sem=sems[0].at[k],
                                               recv_sem=sems[1].at[k], device_id=(px, py, pc), device_id_type=MESH)
                  for k, (px, py, pc) in enumerate(_other_devices(x, y, c))]
        return mine, remote

    def start(self, ins, outs, sems):
        mine, remote = self._copies(ins, outs, sems)
        mine.start()
        for cp in remote:
            cp.start()

    def finish(self, ins, outs, sems):
        mine, remote = self._copies(ins, outs, sems)
        for cp in remote:
            cp.wait()
        mine.wait()

    def before(self, step, n_steps, ins, outs, sems):
        pl.when(step == 0)(lambda: self.start(ins, outs, sems))

    def after(self, step, n_steps, ins, outs, sems):
        pl.when(step == n_steps - 1)(lambda: self.finish(ins, outs, sems))


def _call(body, name, grid, in_specs, out_specs, out_shape, scratch, ins, rider=None):
    single = not isinstance(out_shape, (list, tuple))
    out_shape = [out_shape] if single else list(out_shape)
    out_specs = [out_specs] if single else list(out_specs)
    params = pltpu.CompilerParams(dimension_semantics=("arbitrary",) * len(grid), vmem_limit_bytes=V7X_VMEM_LIMIT)
    if rider is None:
        outs = pl.pallas_call(body, name=name, grid=grid, in_specs=in_specs, out_specs=out_specs, out_shape=out_shape,
                              scratch_shapes=scratch, compiler_params=params)(*ins)
        return (outs[0] if single else outs), None
    n_in, n_out, n_scr = len(ins), len(out_shape), len(scratch)
    r_in, r_out = len(rider.inputs), len(rider.out_shapes)
    n_steps = math.prod(grid)

    def carried(*refs):
        refs = list(refs)
        cut = lambda n: [refs.pop(0) for _ in range(n)]
        b_in, c_in, b_out, c_out, b_scr = cut(n_in), cut(r_in), cut(n_out), cut(r_out), cut(n_scr)
        step = 0
        for axis, size in enumerate(grid):
            step = step * size + pl.program_id(axis)
        rider.before(step, n_steps, c_in, c_out, refs)
        body(*b_in, *b_out, *b_scr)
        rider.after(step, n_steps, c_in, c_out, refs)

    outs = pl.pallas_call(
        carried, name=name, grid=grid, in_specs=list(in_specs) + [ANY] * r_in, out_specs=out_specs + [ANY] * r_out,
        out_shape=out_shape + rider.out_shapes, scratch_shapes=list(scratch) + rider.scratch, compiler_params=params,
    )(*ins, *rider.inputs)
    base = outs[:n_out]
    return (base[0] if single else base), outs[n_out:]


def _arrival_block(j):
    x, y, c = lax.axis_index("x"), lax.axis_index("y"), lax.axis_index("c")
    chip, other_core = j // 2, j % 2
    px = jnp.where((chip == 1) | (chip == 3), 1 - x, x)
    py = jnp.where((chip == 2) | (chip == 3), 1 - y, y)
    pc = jnp.where(other_core == 1, 1 - c, c)
    return 4 * px + 2 * py + pc


def _ffn_gateup_gathering(xb, gate, up_shard, down_shard, name):
    s, d = xb.shape
    fs = up_shard.shape[0]
    tm = min(s, 1024)
    ni = s // tm
    ask_at = max(ni - 2, 0)
    gate_here = gate.ndim == 2
    gather = _Gather(([gate] if gate_here else []) + [up_shard, down_shard])
    n_g = gather.n
    used_here, down = tuple(range(n_g - 1)), n_g - 1

    def body(x_ref, *refs):
        refs = list(refs)
        gate_full = None if gate_here else refs.pop(0)
        shards = [refs.pop(0) for _ in range(n_g)]
        g_ref, u_ref, h_ref = refs.pop(0), refs.pop(0), refs.pop(0)
        fulls = [refs.pop(0) for _ in range(n_g)]
        w_ref, w_sems = refs.pop(0), refs.pop(0)
        sems = refs
        j, i = pl.program_id(0), pl.program_id(1)
        gate_src, up_src = (fulls[0], fulls[1]) if gate_here else (gate_full, fulls[0])

        def load(slot, srcs):
            return [pltpu.make_async_copy(src, w_ref.at[slot, a], w_sems.at[slot, a]) for a, src in enumerate(srcs)]

        @pl.when((j == 0) & (i == 0))
        def _():
            gather.start(shards, fulls, sems, urgent=used_here)
            mine = load(0, (shards[0] if gate_here else gate_full.at[_arrival_block(0)], shards[n_g - 2]))
            for cp in mine:
                cp.start()
            for cp in mine:
                cp.wait()

        for nxt in range(1, N_DEV):
            @pl.when((j == nxt) & (i == 0))
            def _(nxt=nxt):
                for cp in load(nxt % 2, (gate_src.at[0], up_src.at[0])):
                    cp.wait()

        for nxt in range(1, N_DEV):
            @pl.when((j == nxt - 1) & (i == ask_at))
            def _(nxt=nxt):
                for a in used_here:
                    if nxt == 1:
                        gather.wait_sibling(fulls, sems, a)
                    elif nxt % 2 == 0:
                        gather.pass_on(fulls, sems, a, nxt // 2 - 1)
                    else:
                        gather.wait_passed(fulls, sems, a, nxt // 2 - 1)
                block = _arrival_block(nxt)
                for cp in load(nxt % 2, (gate_src.at[block], up_src.at[block])):
                    cp.start()

        @pl.when((j == N_DEV - 1) & (i == ask_at))
        def _():
            for other_chip in range(3):
                gather.pass_on(fulls, sems, down, other_chip)

        x = x_ref[...]
        g = lax.dot_general(x, w_ref[j % 2, 0], NT_DIMS, preferred_element_type=F32)
        u = lax.dot_general(x, w_ref[j % 2, 1], NT_DIMS, preferred_element_type=F32)
        g_ref[...] = g.astype(BF16)
        u_ref[...] = u.astype(BF16)
        h_ref[...] = (g * _sigmoid(g) * u).astype(BF16)

        @pl.when((j == N_DEV - 1) & (i == ni - 1))
        def _():
            gather.wait_sibling(fulls, sems, down)
            for other_chip in range(3):
                gather.wait_passed(fulls, sems, down, other_chip)
            for a in range(n_g):
                gather.wait_sent(shards, fulls, sems, a)

    shp = jax.ShapeDtypeStruct((N_DEV, s, fs), BF16)
    o_spec = pl.BlockSpec((None, tm, fs), lambda j, i: (_arrival_block(j), i, 0))
    ins = ([] if gate_here else [gate]) + gather.inputs
    outs = pl.pallas_call(
        body, name=name, grid=(N_DEV, ni), in_specs=[pl.BlockSpec((tm, d), lambda j, i: (i, 0))] + [ANY] * len(ins),
        out_specs=[o_spec, o_spec, o_spec] + [ANY] * n_g, out_shape=[shp, shp, shp] + gather.out_shapes,
        scratch_shapes=[pltpu.VMEM((2, 2, fs, d), BF16), pltpu.SemaphoreType.DMA((2, 2))] + gather.scratch,
        compiler_params=pltpu.CompilerParams(dimension_semantics=("arbitrary", "arbitrary"), vmem_limit_bytes=V7X_VMEM_LIMIT),
    )(xb, *ins)
    return outs[:3], outs[3:]


def _down_ln(a3, w3, bias, res, res_affine, ln_g, ln_b, scale, name, target=None, rider=None):
    nk, s, tk = a3.shape
    d = w3.shape[2]
    tm = min(s, 256)
    final = target is not None

    def body(*refs):
        refs = list(refs)
        a_ref, w_hbm = refs[:2]
        del refs[:2]
        bias_ref = refs.pop(0) if bias is not None else None
        res_ref = refs.pop(0)
        rg_ref, rb_ref = (refs.pop(0), refs.pop(0)) if res_affine is not None else (None, None)
        g_ref, b_ref = refs.pop(0), refs.pop(0)
        t_ref = refs.pop(0) if final else None
        w_sem = refs.pop()
        w_ref = refs.pop()
        i = pl.program_id(0)
        if final:
            dr_ref, drb_ref, sq_ref, dg_ref, db_ref = refs
        else:
            xh_ref, hb_ref, rstd_ref = refs

        @pl.when(i == 0)
        def _():
            whole = pltpu.make_async_copy(w_hbm, w_ref, w_sem.at[0])
            whole.start()
            whole.wait()
            if final:
                sq_ref[...] = jnp.zeros_like(sq_ref)
                dg_ref[...] = jnp.zeros_like(dg_ref)
                db_ref[...] = jnp.zeros_like(db_ref)

        y = jnp.dot(a_ref[0], w_ref[0], preferred_element_type=F32)
        for k in range(1, nk):
            y = y + jnp.dot(a_ref[k], w_ref[k], preferred_element_type=F32)
        if bias_ref is not None:
            y = y + bias_ref[...]
        for rows in _row_blocks(tm):
            r = res_ref[rows, :]
            if rg_ref is not None:
                r = r * rg_ref[...] + rb_ref[...]
            r = ALPHA * r + scale * y[rows]
            mu = jnp.mean(r, axis=-1, keepdims=True)
            c = r - mu
            var = jnp.mean(c * c, axis=-1, keepdims=True)
            rstd = lax.rsqrt(var + LN_EPS)
            xhat = c * rstd
            h = xhat * g_ref[...] + b_ref[...]
            if not final:
                xh_ref[rows, :] = xhat
                hb_ref[rows, :] = h.astype(BF16)
                rstd_ref[rows, :] = rstd
            else:
                err = h - t_ref[rows, :]
                sq_ref[...] += jnp.sum(err * err, axis=0, keepdims=True)
                dh = err * (1.0 / d)
                dg_ref[...] += jnp.sum(dh * xhat, axis=0, keepdims=True)
                db_ref[...] += jnp.sum(dh, axis=0, keepdims=True)
                dr = _ln_backward(dh, xhat, rstd, g_ref[...])
                dr_ref[rows, :] = dr
                drb_ref[rows, :] = dr.astype(BF16)

    tok = pl.BlockSpec((tm, d), lambda i: (i, 0))
    vec = pl.BlockSpec((1, d), lambda i: (0, 0))
    ins = [a3, w3]
    in_specs = [pl.BlockSpec((nk, tm, tk), lambda i: (0, i, 0)), ANY]
    if bias is not None:
        ins.append(bias)
        in_specs.append(vec)
    ins.append(res)
    in_specs.append(tok)
    if res_affine is not None:
        ins += list(res_affine)
        in_specs += [vec, vec]
    ins += [ln_g, ln_b]
    in_specs += [vec, vec]
    if final:
        ins.append(target)
        in_specs.append(tok)
        out_shape = [jax.ShapeDtypeStruct((s, d), F32), jax.ShapeDtypeStruct((s, d), BF16)] + [jax.ShapeDtypeStruct((1, d), F32)] * 3
        out_specs = [tok, tok, vec, vec, vec]
    else:
        out_shape = [jax.ShapeDtypeStruct((s, d), F32), jax.ShapeDtypeStruct((s, d), BF16), jax.ShapeDtypeStruct((s, 1), F32)]
        out_specs = [tok, tok, pl.BlockSpec((tm, 1), lambda i: (i, 0))]
    scratch = [pltpu.VMEM((nk, tk, d), BF16), pltpu.SemaphoreType.DMA((1,))]
    return _call(body, name, (s // tm,), in_specs, out_specs, out_shape, scratch, ins, rider)


def _proj_in(hb, w3, bias, name, rider=None):
    s, d = hb.shape
    nsh, _, cs = w3.shape
    tm = min(s, 1024)

    def body(h_ref, w_ref, b_ref, z_ref):
        z_ref[...] = (jnp.dot(h_ref[...], w_ref[...], preferred_element_type=F32) + b_ref[...]).astype(BF16)

    in_specs = [pl.BlockSpec((tm, d), lambda i, j: (i, 0)), pl.BlockSpec((None, d, cs), lambda i, j: (j, 0, 0)),
                pl.BlockSpec((1, cs), lambda i, j: (0, j))]
    return _call(body, name, (s // tm, nsh), in_specs, pl.BlockSpec((tm, cs), lambda i, j: (i, j)),
                 jax.ShapeDtypeStruct((s, nsh * cs), BF16), [], [hb, w3, bias], rider)


def _nt_hidden(ab, w3, gate_up, scale, name, rider=None):
    s, kdim = ab.shape
    nj, tn, _ = w3.shape
    tm = min(s, 1024)

    def body(*refs):
        a_ref, w_ref = refs[:2]
        t = scale * lax.dot_general(a_ref[...], w_ref[...], NT_DIMS, preferred_element_type=F32)
        if gate_up is None:
            refs[2][...] = t.astype(BF16)
        else:
            g_ref, u_ref, dg_ref, du_ref = refs[2:]
            g = g_ref[...].astype(F32)
            sg = _sigmoid(g)
            du_ref[...] = (t * g * sg).astype(BF16)
            dg_ref[...] = (t * u_ref[...].astype(F32) * (sg * (1.0 + g * (1.0 - sg)))).astype(BF16)

    hid = pl.BlockSpec((None, tm, tn), lambda i, j: (j, i, 0))
    shp = jax.ShapeDtypeStruct((nj, s, tn), BF16)
    ins = [ab, w3]
    in_specs = [pl.BlockSpec((tm, kdim), lambda i, j: (i, 0)), pl.BlockSpec((None, tn, kdim), lambda i, j: (j, 0, 0))]
    if gate_up is None:
        out_specs, out_shape = hid, shp
    else:
        ins += list(gate_up)
        in_specs += [hid, hid]
        out_specs, out_shape = [hid, hid], [shp, shp]
    return _call(body, name, (s // tm, nj), in_specs, out_specs, out_shape, [], ins, rider)


def _tn_dw(a, a_spec, b, b_spec, nj, m, n, s, tk, scale, name, rider):
    def body(a_ref, b_ref, o_ref, acc_ref):
        k = pl.program_id(1)

        @pl.when(k == 0)
        def _():
            acc_ref[...] = jnp.zeros_like(acc_ref)

        acc_ref[...] += lax.dot_general(a_ref[...], b_ref[...], TN_DIMS, preferred_element_type=F32)

        @pl.when(k == s // tk - 1)
        def _():
            o_ref[...] = (scale * acc_ref[...]).astype(BF16)

    return _call(body, name, (nj, s // tk), [a_spec, b_spec], pl.BlockSpec((None, m, n), lambda j, k: (j, 0, 0)),
                 jax.ShapeDtypeStruct((nj, m, n), BF16), [pltpu.VMEM((m, n), F32)], [a, b], rider)


def _dw_hidden_rows(hid3, db, scale, name, rider=None):
    nj, s, fs = hid3.shape
    d = db.shape[1]
    tk = min(s, DW_TOKENS)
    return _tn_dw(hid3, pl.BlockSpec((None, tk, fs), lambda j, k: (j, k, 0)), db, pl.BlockSpec((tk, d), lambda j, k: (k, 0)),
                  nj, fs, d, s, tk, scale, name, rider)


def _dw_cols(ab, dz, nj, name, rider=None):
    s, d = ab.shape
    cs = dz.shape[1] // nj
    tk = min(s, DW_TOKENS)
    return _tn_dw(ab, pl.BlockSpec((tk, d), lambda j, k: (k, 0)), dz, pl.BlockSpec((tk, cs), lambda j, k: (k, j)),
                  nj, d, cs, s, tk, 1.0, name, rider)


def _ffn_bwd_input(dg3, wg3, du3, wu3, dres, ln, name, rider=None):
    s, d = dres.shape
    nk, _, fs = dg3.shape
    tm = min(s, 512)

    def body(*refs):
        refs = list(refs)
        dg_in, wg_ref, du_in, wu_ref, dres_ref = refs[:5]
        del refs[:5]
        if ln is not None:
            xh_ref, rstd_ref, gain_ref = refs.pop(0), refs.pop(0), refs.pop(0)
        acc_ref = refs.pop()
        i, k = pl.program_id(0), pl.program_id(1)

        @pl.when(k == 0)
        def _():
            acc_ref[...] = jnp.zeros_like(acc_ref)

        acc_ref[...] += (jnp.dot(dg_in[...], wg_ref[...], preferred_element_type=F32)
                         + jnp.dot(du_in[...], wu_ref[...], preferred_element_type=F32))

        @pl.when(k == nk - 1)
        def _():
            if ln is not None:
                dr_ref, drb_ref, dg_ref, db_ref, sum_ref = refs

                @pl.when(i == 0)
                def _():
                    dg_ref[...] = jnp.zeros_like(dg_ref)
                    db_ref[...] = jnp.zeros_like(db_ref)
                    sum_ref[...] = jnp.zeros_like(sum_ref)

            for rows in _row_blocks(tm):
                dh = ALPHA * dres_ref[rows, :] + acc_ref[rows, :]
                if ln is None:
                    refs[0][rows, :] = dh
                else:
                    xhat = xh_ref[rows, :]
                    dg_ref[...] += jnp.sum(dh * xhat, axis=0, keepdims=True)
                    db_ref[...] += jnp.sum(dh, axis=0, keepdims=True)
                    dr = _ln_backward(dh, xhat, rstd_ref[rows, :], gain_ref[...])
                    sum_ref[...] += jnp.sum(dr, axis=0, keepdims=True)
                    dr_ref[rows, :] = dr
                    drb_ref[rows, :] = dr.astype(BF16)

    tok = pl.BlockSpec((tm, d), lambda i, k: (i, 0))
    vec = pl.BlockSpec((1, d), lambda i, k: (0, 0))
    a_spec = pl.BlockSpec((None, tm, fs), lambda i, k: (k, i, 0))
    w_spec = pl.BlockSpec((None, fs, d), lambda i, k: (k, 0, 0))
    ins, in_specs = [dg3, wg3, du3, wu3, dres], [a_spec, w_spec, a_spec, w_spec, tok]
    if ln is None:
        out_shape, out_specs = jax.ShapeDtypeStruct((s, d), F32), tok
    else:
        ins += list(ln)
        in_specs += [tok, pl.BlockSpec((tm, 1), lambda i, k: (i, 0)), vec]
        out_shape = [jax.ShapeDtypeStruct((s, d), F32), jax.ShapeDtypeStruct((s, d), BF16)] + [jax.ShapeDtypeStruct((1, d), F32)] * 3
        out_specs = [tok, tok, vec, vec, vec]
    return _call(body, name, (s // tm, nk), in_specs, out_specs, out_shape, [pltpu.VMEM((tm, d), F32)], ins, rider)


def _proj_in_bwd_ln(dz, w3, dres, ln, name):
    s, d = dres.shape
    nj, _, cs = w3.shape
    tm = min(s, 256)

    def body(dz_ref, w_hbm, dres_ref, xh_ref, rstd_ref, gain_ref, dr_ref, drb_ref, dg_ref, db_ref, w_ref, w_sem):
        @pl.when(pl.program_id(0) == 0)
        def _():
            whole = pltpu.make_async_copy(w_hbm, w_ref, w_sem.at[0])
            whole.start()
            whole.wait()
            dg_ref[...] = jnp.zeros_like(dg_ref)
            db_ref[...] = jnp.zeros_like(db_ref)

        acc = lax.dot_general(dz_ref[:, 0:cs], w_ref[0], NT_DIMS, preferred_element_type=F32)
        for j in range(1, nj):
            acc = acc + lax.dot_general(dz_ref[:, j * cs:(j + 1) * cs], w_ref[j], NT_DIMS, preferred_element_type=F32)
        for rows in _row_blocks(tm):
            dh = ALPHA * dres_ref[rows, :] + acc[rows]
            xhat = xh_ref[rows, :]
            dg_ref[...] += jnp.sum(dh * xhat, axis=0, keepdims=True)
            db_ref[...] += jnp.sum(dh, axis=0, keepdims=True)
            dr = _ln_backward(dh, xhat, rstd_ref[rows, :], gain_ref[...])
            dr_ref[rows, :] = dr
            drb_ref[rows, :] = dr.astype(BF16)

    tok = pl.BlockSpec((tm, d), lambda i: (i, 0))
    vec = pl.BlockSpec((1, d), lambda i: (0, 0))
    in_specs = [pl.BlockSpec((tm, nj * cs), lambda i: (i, 0)), ANY, tok, tok, pl.BlockSpec((tm, 1), lambda i: (i, 0)), vec]
    out_shape = [jax.ShapeDtypeStruct((s, d), F32), jax.ShapeDtypeStruct((s, d), BF16)] + [jax.ShapeDtypeStruct((1, d), F32)] * 2
    scratch = [pltpu.VMEM(w3.shape, BF16), pltpu.SemaphoreType.DMA((1,))]
    return _call(body, name, (s // tm,), in_specs, [tok, tok, vec, vec], out_shape, scratch, [dz, w3, dres] + list(ln))[0]


def _shift_rows_down(v, halo, k, row):
    out = pltpu.roll(v, k, 0)
    hr = halo.shape[0]
    for r in range(k):
        out = jnp.where(row == r, halo[hr - k + r:hr - k + r + 1, :], out)
    return out


def _shift_rows_up(v, halo, k, row):
    t = v.shape[0]
    out = pltpu.roll(v, t - k, 0)
    for r in range(k):
        out = jnp.where(row == t - k + r, halo[r:r + 1, :], out)
    return out


def _sgu_head_forward(z_ref, h, da, gv_ref, bv_ref):
    zu = z_ref[:, h * HEAD:(h + 1) * HEAD].astype(F32)
    zv = z_ref[:, da + h * HEAD:da + (h + 1) * HEAD].astype(F32)
    u = _gelu(zu)
    v = _gelu(zv)
    mu = jnp.mean(v, axis=-1, keepdims=True)
    c = v - mu
    rstd = lax.rsqrt(jnp.mean(c * c, axis=-1, keepdims=True) + LN_EPS)
    vhat = c * rstd
    vln = (vhat * gv_ref[h:h + 1, :] + bv_ref[h:h + 1, :]).astype(BF16)
    return zu, zv, u, vhat, rstd, vln


def _mixer_fwd(z, ws_masked, bs_wide, gv, bv, cw, name):
    s, zc = z.shape
    da = zc // 5
    nh = da // HEAD
    tm = min(s, 512)
    hb = tm // BF16_ROWS

    def body(z_ref, pc_ref, px_ref, ws_ref, bs_ref, gv_ref, bv_ref, cw_ref, y_ref):
        i = pl.program_id(0)
        for h in range(nh):
            _, _, u, _, _, vln = _sgu_head_forward(z_ref, h, da, gv_ref, bv_ref)
            for n in range(tm // CHUNK):
                rows = slice(n * CHUNK, (n + 1) * CHUNK)
                mixed = jnp.dot(ws_ref[h], vln[rows], preferred_element_type=F32) + bs_ref[h]
                y_ref[0, rows, h * HEAD:(h + 1) * HEAD] = (u[rows] * mixed).astype(BF16)
        gate_b = z_ref[:, 2 * da:3 * da].astype(F32)
        hc = z_ref[:, 3 * da:4 * da].astype(F32) * z_ref[:, 4 * da:5 * da].astype(F32)
        halo = jnp.where(i > 0, pc_ref[...].astype(F32) * px_ref[...].astype(F32), 0.0)
        row = lax.broadcasted_iota(jnp.int32, (tm, da), 0)
        y = cw_ref[0:1, :] * _shift_rows_down(hc, halo, 2, row) + cw_ref[1:2, :] * _shift_rows_down(hc, halo, 1, row) + cw_ref[2:3, :] * hc
        y_ref[1] = (gate_b * y).astype(BF16)

    prev = lambda col: pl.BlockSpec((BF16_ROWS, da), lambda i: (jnp.maximum(i * hb - 1, 0), col))
    in_specs = [pl.BlockSpec((tm, zc), lambda i: (i, 0)), prev(3), prev(4), _row((nh, CHUNK, CHUNK)), _row((nh, CHUNK, HEAD)),
                _row((nh, HEAD)), _row((nh, HEAD)), _row((CONV_TAPS, da))]
    return _call(body, name, (s // tm,), in_specs, pl.BlockSpec((2, tm, da), lambda i: (0, i, 0)),
                 jax.ShapeDtypeStruct((2, s, da), BF16), [], [z, z, z, ws_masked, bs_wide, gv, bv, cw])[0]


def _mixer_bwd(z, dy, ws_masked, bs_wide, gv, bv, cw, name, rider=None):
    s, zc = z.shape
    da = zc // 5
    nh = da // HEAD
    tm = min(s, 512)
    hb = tm // BF16_ROWS
    nblk = s // tm

    def body(z_ref, pc_ref, px_ref, nb_ref, dy_ref, ndy_ref, ws_ref, bs_ref, gv_ref, bv_ref, cw_ref,
             dz_ref, dws_ref, dbs_ref, dgv_ref, dbv_ref, dcw_ref, dbin_ref):
        i = pl.program_id(0)

        @pl.when(i == 0)
        def _():
            for ref in (dws_ref, dbs_ref, dgv_ref, dbv_ref, dcw_ref, dbin_ref):
                ref[...] = jnp.zeros_like(ref)

        causal = lax.broadcasted_iota(jnp.int32, (CHUNK, CHUNK), 0) >= lax.broadcasted_iota(jnp.int32, (CHUNK, CHUNK), 1)
        for h in range(nh):
            zu, zv, u, vhat, rstd, vln = _sgu_head_forward(z_ref, h, da, gv_ref, bv_ref)
            dya = dy_ref[0, :, h * HEAD:(h + 1) * HEAD].astype(F32)
            w = ws_ref[h]
            du_parts, dvln_parts = [], []
            for n in range(tm // CHUNK):
                rows = slice(n * CHUNK, (n + 1) * CHUNK)
                mixed = jnp.dot(w, vln[rows], preferred_element_type=F32) + bs_ref[h]
                du_parts.append(dya[rows] * mixed)
                dmix = dya[rows] * u[rows]
                dmix_b = dmix.astype(BF16)
                dws_ref[h] += jnp.where(causal, lax.dot_general(dmix_b, vln[rows], NT_DIMS, preferred_element_type=F32), 0.0)
                dbs_ref[h] += dmix
                dvln_parts.append(lax.dot_general(w, dmix_b, TN_DIMS, preferred_element_type=F32))
            du = jnp.concatenate(du_parts, axis=0)
            dvln = jnp.concatenate(dvln_parts, axis=0)
            dgv_ref[h:h + 1, :] += jnp.sum(dvln * vhat, axis=0, keepdims=True)
            dbv_ref[h:h + 1, :] += jnp.sum(dvln, axis=0, keepdims=True)
            dv = _ln_backward(dvln, vhat, rstd, gv_ref[h:h + 1, :])
            dzu = du * _gelu_grad(zu)
            dzv = dv * _gelu_grad(zv)
            ucols = slice(h * HEAD, (h + 1) * HEAD)
            vcols = slice(da + h * HEAD, da + (h + 1) * HEAD)
            dz_ref[:, ucols] = dzu.astype(BF16)
            dz_ref[:, vcols] = dzv.astype(BF16)
            dbin_ref[:, ucols] += jnp.sum(dzu, axis=0, keepdims=True)
            dbin_ref[:, vcols] += jnp.sum(dzv, axis=0, keepdims=True)

        gate_b = z_ref[:, 2 * da:3 * da].astype(F32)
        gate_c = z_ref[:, 3 * da:4 * da].astype(F32)
        xt = z_ref[:, 4 * da:5 * da].astype(F32)
        hc = gate_c * xt
        halo = jnp.where(i > 0, pc_ref[...].astype(F32) * px_ref[...].astype(F32), 0.0)
        row = lax.broadcasted_iota(jnp.int32, (tm, da), 0)
        sh1 = _shift_rows_down(hc, halo, 1, row)
        sh2 = _shift_rows_down(hc, halo, 2, row)
        y = cw_ref[0:1, :] * sh2 + cw_ref[1:2, :] * sh1 + cw_ref[2:3, :] * hc
        dyb = dy_ref[1].astype(F32)
        dconv = dyb * gate_b
        nhalo = jnp.where(i < nblk - 1, ndy_ref[...].astype(F32) * nb_ref[...].astype(F32), 0.0)
        dhc = cw_ref[2:3, :] * dconv + cw_ref[1:2, :] * _shift_rows_up(dconv, nhalo, 1, row) + cw_ref[0:1, :] * _shift_rows_up(dconv, nhalo, 2, row)
        dcw_ref[0:1, :] += jnp.sum(dconv * sh2, axis=0, keepdims=True)
        dcw_ref[1:2, :] += jnp.sum(dconv * sh1, axis=0, keepdims=True)
        dcw_ref[2:3, :] += jnp.sum(dconv * hc, axis=0, keepdims=True)
        for col, val in ((2, dyb * y), (3, dhc * xt), (4, dhc * gate_c)):
            cols = slice(col * da, (col + 1) * da)
            dz_ref[:, cols] = val.astype(BF16)
            dbin_ref[:, cols] += jnp.sum(val, axis=0, keepdims=True)

        @pl.when(i == nblk - 1)
        def _():
            for h in range(nh):
                dbs_ref[h] = jnp.broadcast_to(jnp.sum(dbs_ref[h], axis=1, keepdims=True), (CHUNK, HEAD))

    prev = lambda col: pl.BlockSpec((BF16_ROWS, da), lambda i: (jnp.maximum(i * hb - 1, 0), col))
    nxt = lambda i: jnp.minimum((i + 1) * hb, s // BF16_ROWS - 1)
    in_specs = [pl.BlockSpec((tm, zc), lambda i: (i, 0)), prev(3), prev(4), pl.BlockSpec((BF16_ROWS, da), lambda i: (nxt(i), 2)),
                pl.BlockSpec((2, tm, da), lambda i: (0, i, 0)), pl.BlockSpec((None, BF16_ROWS, da), lambda i: (1, nxt(i), 0)),
                _row((nh, CHUNK, CHUNK)), _row((nh, CHUNK, HEAD)), _row((nh, HEAD)), _row((nh, HEAD)), _row((CONV_TAPS, da))]
    out_specs = [pl.BlockSpec((tm, zc), lambda i: (i, 0)), _row((nh, CHUNK, CHUNK)), _row((nh, CHUNK, HEAD)), _row((nh, HEAD)),
                 _row((nh, HEAD)), _row((8, da)), _row((1, zc))]
    out_shape = [jax.ShapeDtypeStruct((s, zc), BF16), jax.ShapeDtypeStruct((nh, CHUNK, CHUNK), F32),
                 jax.ShapeDtypeStruct((nh, CHUNK, HEAD), F32), jax.ShapeDtypeStruct((nh, HEAD), F32),
                 jax.ShapeDtypeStruct((nh, HEAD), F32), jax.ShapeDtypeStruct((8, da), F32), jax.ShapeDtypeStruct((1, zc), F32)]
    return _call(body, name, (nblk,), in_specs, out_specs, out_shape, [], [z, z, z, z, dy, dy, ws_masked, bs_wide, gv, bv, cw], rider)


def _adamw(gparts, w, m, v, name):
    n, r, c = gparts.shape
    tr = r // 4 if (r // 4) % BF16_ROWS == 0 else r

    def body(g_ref, w_ref, m_ref, v_ref, go_ref, d_ref, mo_ref, vo_ref):
        g = g_ref[0].astype(F32)
        for q in range(1, n):
            g = g + g_ref[q].astype(F32)
        m_new = ADAM_B1 * m_ref[...] + (1.0 - ADAM_B1) * g
        v_new = ADAM_B2 * v_ref[...] + (1.0 - ADAM_B2) * (g * g)
        m_hat = m_new / (1.0 - ADAM_B1 ** ADAM_STEP)
        v_hat = v_new / (1.0 - ADAM_B2 ** ADAM_STEP)
        go_ref[...] = g
        d_ref[...] = -ADAM_LR * (m_hat / (jnp.sqrt(v_hat) + ADAM_EPS) + ADAM_WD * w_ref[...])
        mo_ref[...] = m_new
        vo_ref[...] = v_new

    blk = pl.BlockSpec((tr, c), lambda i: (i, 0))
    shp = jax.ShapeDtypeStruct((r, c), F32)
    return _call(body, name, (r // tr,), [pl.BlockSpec((n, tr, c), lambda i: (0, i, 0)), blk, blk, blk], [blk] * 4, [shp] * 4, [],
                 [gparts, w, m, v])[0]


def _rows128(a):
    return a.reshape(-1, LANES)


def kernel(x, ffa_gate, ffa_up, ffa_down, ln_a_g, ln_a_b, w_in, b_in, w_s, b_s, ln_v_g, ln_v_b, conv_w, w_out, b_out, ln_m_g, ln_m_b, ffc_gate, ffc_up, ffc_down, ln_c_g, ln_c_b, loss_target, m_ffa_gate, m_ffa_up, m_ffa_down, m_ln_a_g, m_ln_a_b, m_w_in, m_b_in, m_w_s, m_b_s, m_ln_v_g, m_ln_v_b, m_conv_w, m_w_out, m_b_out, m_ln_m_g, m_ln_m_b, m_ffc_gate, m_ffc_up, m_ffc_down, m_ln_c_g, m_ln_c_b, v_ffa_gate, v_ffa_up, v_ffa_down, v_ln_a_g, v_ln_a_b, v_w_in, v_b_in, v_w_s, v_b_s, v_ln_v_g, v_ln_v_b, v_conv_w, v_w_out, v_b_out, v_ln_m_g, v_ln_m_b, v_ffc_gate, v_ffc_up, v_ffc_down, v_ln_c_g, v_ln_c_b):
    x2, target = x[0], loss_target[0]
    s, d = x2.shape
    da = d // 2
    nh = da // HEAD
    me = 4 * lax.axis_index("x") + 2 * lax.axis_index("y") + lax.axis_index("c")

    big = dict(ffa_gate=ffa_gate, ffa_up=ffa_up, ffa_down=ffa_down, w_in=w_in, w_out=w_out, ffc_gate=ffc_gate, ffc_up=ffc_up, ffc_down=ffc_down)
    big_m = dict(ffa_gate=m_ffa_gate, ffa_up=m_ffa_up, ffa_down=m_ffa_down, w_in=m_w_in, w_out=m_w_out, ffc_gate=m_ffc_gate, ffc_up=m_ffc_up, ffc_down=m_ffc_down)
    big_v = dict(ffa_gate=v_ffa_gate, ffa_up=v_ffa_up, ffa_down=v_ffa_down, w_in=v_w_in, w_out=v_w_out, ffc_gate=v_ffc_gate, ffc_up=v_ffc_up, ffc_down=v_ffc_down)
    local = lambda k, a: jnp.transpose(a[0]) if k in TRANSPOSED else a[0]
    shard = {k: local(k, w).astype(BF16) for k, w in big.items()}
    conv_rows = jnp.pad(conv_w[0], ((0, 8 - CONV_TAPS), (0, 0)))

    tril = jnp.tril(jnp.ones((CHUNK, CHUNK), dtype=bool))
    ws_masked = jnp.where(tril[None], w_s[0], 0.0).astype(BF16)
    bs_wide = jnp.broadcast_to(b_s[0][:, :, None], (nh, CHUNK, HEAD))
    gv, bv = ln_v_g.reshape(nh, HEAD), ln_v_b.reshape(nh, HEAD)

    full = {}
    xb = x2.astype(BF16)
    (g_a, u_a, hid_a), (full["ffa_gate"], full["ffa_up"], full["ffa_down"]) = _ffn_gateup_gathering(
        xb, shard["ffa_gate"], shard["ffa_up"], shard["ffa_down"], "ffa_gateup")
    (xhat1, h1b, rstd1), (full["w_in"], full["w_out"], conv_full) = _down_ln(
        hid_a, full["ffa_down"], None, x2, None, ln_a_g, ln_a_b, 0.5, "ffa_down_ln", rider=_Gather([shard["w_in"], shard["w_out"], conv_rows]))
    cw = jnp.transpose(conv_full[:, :CONV_TAPS, :], (1, 0, 2)).reshape(CONV_TAPS, da)
    w_out2 = full["w_out"].reshape(2, da, d)
    z, (full["ffc_gate"],) = _proj_in(h1b, full["w_in"], b_in, "proj_in", _Gather([shard["ffc_gate"]]))
    ycat = _mixer_fwd(z, ws_masked, bs_wide, gv, bv, cw, "mixer_fwd")
    (xhat2, h2b, rstd2), _ = _down_ln(ycat, w_out2, b_out, xhat1, (ln_a_g, ln_a_b), ln_m_g, ln_m_b, 1.0, "proj_out_ln")
    (g_c, u_c, hid_c), (full["ffc_up"], full["ffc_down"]) = _ffn_gateup_gathering(
        h2b, full["ffc_gate"], shard["ffc_up"], shard["ffc_down"], "ffc_gateup")
    (dr3, dr3b, sq_err, d_ln_c_g, d_ln_c_b), _ = _down_ln(hid_c, full["ffc_down"], None, xhat2, (ln_m_g, ln_m_b), ln_c_g, ln_c_b, 0.5,
                                                          "ffc_down_ln_loss", target=target)
    loss = lax.psum((0.5 / d) * jnp.sum(sq_err), ("x", "y", "c"))

    landed = {}
    (dg_c, du_c), _ = _nt_hidden(dr3b, full["ffc_down"], (g_c, u_c), 0.5, "ffc_bwd_hidden")
    part, _ = _dw_hidden_rows(hid_c, dr3b, 0.5, "ffc_dw_down")
    part, (landed["ffc_down"],) = _dw_hidden_rows(dg_c, h2b, 1.0, "ffc_dw_gate", _Scatter(part))
    part, (landed["ffc_gate"],) = _dw_hidden_rows(du_c, h2b, 1.0, "ffc_dw_up", _Scatter(part))
    (dr2, dr2b, d_ln_m_g, d_ln_m_b, d_b_out), (landed["ffc_up"],) = _ffn_bwd_input(
        dg_c, full["ffc_gate"], du_c, full["ffc_up"], dr3, (xhat2, rstd2, ln_m_g), "ffc_bwd_input_ln", _Scatter(part))
    dycat, _ = _nt_hidden(dr2b, w_out2, None, 1.0, "proj_out_bwd")
    part, _ = _dw_hidden_rows(ycat, dr2b, 1.0, "proj_out_dw")
    (dz, d_w_s, d_b_s_wide, d_gv, d_bv, d_cw, d_b_in), (landed["w_out"],) = _mixer_bwd(
        z, dycat, ws_masked, bs_wide, gv, bv, cw, "mixer_bwd", _Scatter(part.reshape(N_DEV, d // N_DEV, d)))
    dr1, dr1b, d_ln_a_g, d_ln_a_b = _proj_in_bwd_ln(dz, full["w_in"], dr2, (xhat1, rstd1, ln_a_g), "proj_in_bwd_ln")
    small_g = dict(ln_a_g=d_ln_a_g, ln_a_b=d_ln_a_b, b_in=d_b_in, w_s=d_w_s, b_s=d_b_s_wide[:, :, 0], ln_v_g=d_gv, ln_v_b=d_bv, b_out=d_b_out,
                   ln_m_g=d_ln_m_g, ln_m_b=d_ln_m_b, ln_c_g=d_ln_c_g, ln_c_b=d_ln_c_b)
    pack = jnp.concatenate([_rows128(g) for g in small_g.values()] + [_rows128(d_cw[:CONV_TAPS])], axis=0)
    part, (packs,) = _dw_cols(h1b, dz, N_DEV, "proj_in_dw", _Scatter(pack, whole=True))
    (dg_a, du_a), (landed["w_in"],) = _nt_hidden(dr1b, full["ffa_down"], (g_a, u_a), 0.5, "ffa_bwd_hidden", _Scatter(part))
    part, _ = _dw_hidden_rows(hid_a, dr1b, 0.5, "ffa_dw_down")
    part, (landed["ffa_down"],) = _dw_hidden_rows(dg_a, xb, 1.0, "ffa_dw_gate", _Scatter(part))
    part, (landed["ffa_gate"],) = _dw_hidden_rows(du_a, xb, 1.0, "ffa_dw_up", _Scatter(part))
    grad_x, (landed["ffa_up"],) = _ffn_bwd_input(dg_a, full["ffa_gate"], du_a, full["ffa_up"], dr1, None, "ffa_bwd_input", _Scatter(part))

    grads, deltas, new_m, new_v = {}, {}, {}, {}
    for k in big:
        out = _adamw(landed[k], local(k, big[k]), local(k, big_m[k]), local(k, big_v[k]), "adamw_" + k)
        grads[k], deltas[k], new_m[k], new_v[k] = ((jnp.transpose(o) if k in TRANSPOSED else o).reshape(big[k].shape) for o in out)

    small = dict(ln_a_g=ln_a_g, ln_a_b=ln_a_b, b_in=b_in, w_s=w_s, b_s=b_s, ln_v_g=ln_v_g, ln_v_b=ln_v_b, b_out=b_out,
                 ln_m_g=ln_m_g, ln_m_b=ln_m_b, ln_c_g=ln_c_g, ln_c_b=ln_c_b)
    small_m = dict(ln_a_g=m_ln_a_g, ln_a_b=m_ln_a_b, b_in=m_b_in, w_s=m_w_s, b_s=m_b_s, ln_v_g=m_ln_v_g, ln_v_b=m_ln_v_b, b_out=m_b_out,
                   ln_m_g=m_ln_m_g, ln_m_b=m_ln_m_b, ln_c_g=m_ln_c_g, ln_c_b=m_ln_c_b)
    small_v = dict(ln_a_g=v_ln_a_g, ln_a_b=v_ln_a_b, b_in=v_b_in, w_s=v_w_s, b_s=v_b_s, ln_v_g=v_ln_v_g, ln_v_b=v_ln_v_b, b_out=v_b_out,
                   ln_m_g=v_ln_m_g, ln_m_b=v_ln_m_b, ln_c_g=v_ln_c_g, ln_c_b=v_ln_c_b)
    snames = list(small)
    assert snames == list(small_g)
    rows = [_rows128(small[k]).shape[0] for k in snames]
    n_rep = sum(rows)
    cw_cols = conv_w.shape[2]
    conv_parts = lax.dynamic_slice_in_dim(packs[:, n_rep:].reshape(N_DEV, CONV_TAPS, da), me * cw_cols, cw_cols, axis=2)
    pad_conv = lambda a: jnp.pad(a, ((0, 8 - CONV_TAPS), (0, 0)))
    stack = lambda tree, conv: jnp.concatenate([_rows128(tree[k]) for k in snames] + [pad_conv(conv)], axis=0)
    small_parts = jnp.concatenate([packs[:, :n_rep], jnp.pad(conv_parts, ((0, 0), (0, 8 - CONV_TAPS), (0, 0)))], axis=1)
    out = _adamw(small_parts, stack(small, conv_w[0]), stack(small_m, m_conv_w[0]), stack(small_v, v_conv_w[0]), "adamw_small")
    for tree, packed in zip((grads, deltas, new_m, new_v), out):
        at = 0
        for k, r in zip(snames, rows):
            tree[k] = packed[at:at + r].reshape(small[k].shape)
            at += r
        tree["conv_w"] = packed[at:at + CONV_TAPS].reshape(conv_w.shape)

    order = ["ffa_gate", "ffa_up", "ffa_down", "ln_a_g", "ln_a_b", "w_in", "b_in", "w_s", "b_s", "ln_v_g", "ln_v_b", "conv_w", "w_out", "b_out",
             "ln_m_g", "ln_m_b", "ffc_gate", "ffc_up", "ffc_down", "ln_c_g", "ln_c_b"]
    return (loss, grad_x[None], *[grads[k] for k in order], *[deltas[k] for k in order], *[new_m[k] for k in order], *[new_v[k] for k in order])
```

```python
import math

import jax
import jax.numpy as jnp
from jax import lax
from jax.experimental import pallas as pl
from jax.experimental.pallas import tpu as pltpu

BF16 = jnp.bfloat16
F32 = jnp.float32
MESH = pl.DeviceIdType.MESH

N_DEV = 8
HEAD = 128
CHUNK = 128
CONV_TAPS = 3
LN_EPS = 1e-5
ALPHA = float(2 ** 0.25)
GELU_C = 0.7978845608028654
GELU_A = 0.044715
ADAM_LR, ADAM_B1, ADAM_B2, ADAM_EPS, ADAM_WD, ADAM_STEP = 0.001, 0.9, 0.999, 1e-08, 0.01, 10
V7X_VMEM_LIMIT = 56 * 1024 * 1024
LANES = 128
BF16_ROWS = 16
MXU_COLS = 256
TRANSPOSED = ("ffa_gate", "ffa_up", "ffc_gate", "ffc_up")
DW_TOKENS = 2048

NT_DIMS = (((1,), (1,)), ((), ()))
TN_DIMS = (((0,), (0,)), ((), ()))
ANY = pl.BlockSpec(memory_space=pl.ANY)


def _gelu(x):
    return 0.5 * x * (1.0 + jnp.tanh(GELU_C * (x + GELU_A * x * x * x)))


def _gelu_grad(x):
    t = jnp.tanh(GELU_C * (x + GELU_A * x * x * x))
    return 0.5 * (1.0 + t) + 0.5 * x * (1.0 - t * t) * GELU_C * (1.0 + 3.0 * GELU_A * x * x)


def _sigmoid(x):
    return 1.0 / (1.0 + jnp.exp(-x))


def _row(shape):
    return pl.BlockSpec(shape, lambda *_: (0,) * len(shape))


def _row_blocks(tm, rows=128):
    rows = min(rows, tm)
    return [slice(r, r + rows) for r in range(0, tm, rows)]


def _ln_backward(dh, xhat, rstd, gain):
    dxh = dh * gain
    m1 = jnp.mean(dxh, axis=-1, keepdims=True)
    m2 = jnp.mean(dxh * xhat, axis=-1, keepdims=True)
    return rstd * (dxh - m1 - xhat * m2)


def _place():
    x, y, c = lax.axis_index("x"), lax.axis_index("y"), lax.axis_index("c")
    return x, y, c, [(1 - x, y), (x, 1 - y), (1 - x, 1 - y)]


def _other_devices(x, y, c):
    flips = [(bx, by, bc) for bx in (0, 1) for by in (0, 1) for bc in (0, 1)][1:]
    return [(1 - x if bx else x, 1 - y if by else y, 1 - c if bc else c) for bx, by, bc in flips]


class _Gather:
    def __init__(self, shards, forward_at=0.75):
        n = len(shards)
        self.n, self.forward_at = n, forward_at
        self.inputs = list(shards)
        self.out_shapes = [jax.ShapeDtypeStruct((N_DEV,) + a.shape, a.dtype) for a in shards]
        self.scratch = [pltpu.SemaphoreType.DMA((n, 7)), pltpu.SemaphoreType.DMA((n, 7)), pltpu.SemaphoreType.DMA((n,))]

    def _copy(self, outs, sems, a, k, block, to, src=None):
        dst = outs[a].at[block]
        return pltpu.make_async_remote_copy(src_ref=dst if src is None else src, dst_ref=dst, send_sem=sems[0].at[a, k],
                                            recv_sem=sems[1].at[a, k], device_id=to, device_id_type=MESH)

    def start(self, ins, outs, sems, urgent=None):
        x, y, c, chips = _place()
        me = 4 * x + 2 * y + c
        for a in range(self.n):
            pltpu.make_async_copy(ins[a], outs[a].at[me], sems[2].at[a]).start()
        urgent = list(range(self.n)) if urgent is None else list(urgent)
        for group in (urgent, [a for a in range(self.n) if a not in urgent]):
            for a in group:
                self._copy(outs, sems, a, 0, me, (x, y, 1 - c), src=ins[a]).start()
                for j in (0, 1):
                    self._copy(outs, sems, a, 1 + j, me, (*chips[j], c), src=ins[a]).start()
            for a in group:
                self._copy(outs, sems, a, 3, me, (*chips[2], c), src=ins[a]).start()

    def wait_sibling(self, outs, sems, a):
        x, y, c, _ = _place()
        self._copy(outs, sems, a, 0, 4 * x + 2 * y + 1 - c, (x, y, 1 - c)).wait_recv()

    def pass_on(self, outs, sems, a, j):
        x, y, c, chips = _place()
        block = 4 * chips[j][0] + 2 * chips[j][1] + c
        self._copy(outs, sems, a, 1 + j, block, (x, y, 1 - c)).wait_recv()
        self._copy(outs, sems, a, 4 + j, block, (x, y, 1 - c)).start()

    def wait_passed(self, outs, sems, a, j):
        x, y, c, chips = _place()
        self._copy(outs, sems, a, 4 + j, 4 * chips[j][0] + 2 * chips[j][1] + 1 - c, (x, y, 1 - c)).wait_recv()

    def wait_sent(self, ins, outs, sems, a):
        x, y, c, _ = _place()
        me = 4 * x + 2 * y + c
        for k in range(7):
            self._copy(outs, sems, a, k, me, (x, y, 1 - c), src=ins[a]).wait_send()
        pltpu.make_async_copy(ins[a], outs[a].at[me], sems[2].at[a]).wait()

    def forward(self, ins, outs, sems):
        for a in range(self.n):
            for j in range(3):
                self.pass_on(outs, sems, a, j)

    def finish(self, ins, outs, sems):
        for a in range(self.n):
            self.wait_sibling(outs, sems, a)
            for j in range(3):
                self.wait_passed(outs, sems, a, j)
        for a in range(self.n):
            self.wait_sent(ins, outs, sems, a)

    def before(self, step, n_steps, ins, outs, sems):
        pl.when(step == 0)(lambda: self.start(ins, outs, sems))
        pl.when(step == int(self.forward_at * (n_steps - 1)))(lambda: self.forward(ins, outs, sems))

    def after(self, step, n_steps, ins, outs, sems):
        pl.when(step == n_steps - 1)(lambda: self.finish(ins, outs, sems))


class _Scatter:
    def __init__(self, partial, whole=False):
        self.whole = whole
        self.inputs = [partial]
        self.out_shapes = [jax.ShapeDtypeStruct(((N_DEV,) if whole else ()) + partial.shape, partial.dtype)]
        self.scratch = [pltpu.SemaphoreType.DMA((7,)), pltpu.SemaphoreType.DMA((7,)), pltpu.SemaphoreType.DMA((1,))]

    def _copies(self, ins, outs, sems):
        x, y, c, _ = _place()
        me = 4 * x + 2 * y + c
        block = (lambda dev: ins[0]) if self.whole else (lambda dev: ins[0].at[dev])
        mine = pltpu.make_async_copy(block(me), outs[0].at[me], sems[2].at[0])
        remote = [pltpu.make_async_remote_copy(src_ref=block(4 * px + 2 * py + pc), dst_ref=outs[0].at[me], send_sem=sems[0].at[k],
                                               recv_sem=sems[1].at[k], device_id=(px, py, pc), device_id_type=MESH)
                  for k, (px, py, pc) in enumerate(_other_devices(x, y, c))]
        return mine, remote

    def start(self, ins, outs, sems):
        mine, remote = self._copies(ins, outs, sems)
        mine.start()
        for cp in remote:
            cp.start()

    def finish(self, ins, outs, sems):
        mine, remote = self._copies(ins, outs, sems)
        for cp in remote:
            cp.wait()
        mine.wait()

    def before(self, step, n_steps, ins, outs, sems):
        pl.when(step == 0)(lambda: self.start(ins, outs, sems))

    def after(self, step, n_steps, ins, outs, sems):
        pl.when(step == n_steps - 1)(lambda: self.finish(ins, outs, sems))


def _call(body, name, grid, in_specs, out_specs, out_shape, scratch, ins, rider=None):
    single = not isinstance(out_shape, (list, tuple))
    out_shape = [out_shape] if single else list(out_shape)
    out_specs = [out_specs] if single else list(out_specs)
    params = pltpu.CompilerParams(dimension_semantics=("arbitrary",) * len(grid), vmem_limit_bytes=V7X_VMEM_LIMIT)
    if rider is None:
        outs = pl.pallas_call(body, name=name, grid=grid, in_specs=in_specs, out_specs=out_specs, out_shape=out_shape,
                              scratch_shapes=scratch, compiler_params=params)(*ins)
        return (outs[0] if single else outs), None
    n_in, n_out, n_scr = len(ins), len(out_shape), len(scratch)
    r_in, r_out = len(rider.inputs), len(rider.out_shapes)
    n_steps = math.prod(grid)

    def carried(*refs):
        refs = list(refs)
        cut = lambda n: [refs.pop(0) for _ in range(n)]
        b_in, c_in, b_out, c_out, b_scr = cut(n_in), cut(r_in), cut(n_out), cut(r_out), cut(n_scr)
        step = 0
        for axis, size in enumerate(grid):
            step = step * size + pl.program_id(axis)
        rider.before(step, n_steps, c_in, c_out, refs)
        body(*b_in, *b_out, *b_scr)
        rider.after(step, n_steps, c_in, c_out, refs)

    outs = pl.pallas_call(
        carried, name=name, grid=grid, in_specs=list(in_specs) + [ANY] * r_in, out_specs=out_specs + [ANY] * r_out,
        out_shape=out_shape + rider.out_shapes, scratch_shapes=list(scratch) + rider.scratch, compiler_params=params,
    )(*ins, *rider.inputs)
    base = outs[:n_out]
    return (base[0] if single else base), outs[n_out:]


def _arrival_block(j):
    x, y, c = lax.axis_index("x"), lax.axis_index("y"), lax.axis_index("c")
    chip, other_core = j // 2, j % 2
    px = jnp.where((chip == 1) | (chip == 3), 1 - x, x)
    py = jnp.where((chip == 2) | (chip == 3), 1 - y, y)
    pc = jnp.where(other_core == 1, 1 - c, c)
    return 4 * px + 2 * py + pc


def _ffn_gateup_gathering(xb, gate, up_shard, down_shard, name):
    s, d = xb.shape
    fs = up_shard.shape[0]
    tm = min(s, 1024)
    ni = s // tm
    ask_at = max(ni - 2, 0)
    gate_here = gate.ndim == 2
    gather = _Gather(([gate] if gate_here else []) + [up_shard, down_shard])
    n_g = gather.n
    used_here, down = tuple(range(n_g - 1)), n_g - 1

    def body(x_ref, *refs):
        refs = list(refs)
        gate_full = None if gate_here else refs.pop(0)
        shards = [refs.pop(0) for _ in range(n_g)]
        g_ref, u_ref, h_ref = refs.pop(0), refs.pop(0), refs.pop(0)
        fulls = [refs.pop(0) for _ in range(n_g)]
        w_ref, w_sems = refs.pop(0), refs.pop(0)
        sems = refs
        j, i = pl.program_id(0), pl.program_id(1)
        gate_src, up_src = (fulls[0], fulls[1]) if gate_here else (gate_full, fulls[0])

        def load(slot, srcs):
            return [pltpu.make_async_copy(src, w_ref.at[slot, a], w_sems.at[slot, a]) for a, src in enumerate(srcs)]

        @pl.when((j == 0) & (i == 0))
        def _():
            gather.start(shards, fulls, sems, urgent=used_here)
            mine = load(0, (shards[0] if gate_here else gate_full.at[_arrival_block(0)], shards[n_g - 2]))
            for cp in mine:
                cp.start()
            for cp in mine:
                cp.wait()

        for nxt in range(1, N_DEV):
            @pl.when((j == nxt) & (i == 0))
            def _(nxt=nxt):
                for cp in load(nxt % 2, (gate_src.at[0], up_src.at[0])):
                    cp.wait()

        for nxt in range(1, N_DEV):
            @pl.when((j == nxt - 1) & (i == ask_at))
            def _(nxt=nxt):
                for a in used_here:
                    if nxt == 1:
                        gather.wait_sibling(fulls, sems, a)
                    elif nxt % 2 == 0:
                        gather.pass_on(fulls, sems, a, nxt // 2 - 1)
                    else:
                        gather.wait_passed(fulls, sems, a, nxt // 2 - 1)
                block = _arrival_block(nxt)
                for cp in load(nxt % 2, (gate_src.at[block], up_src.at[block])):
                    cp.start()

        @pl.when((j == N_DEV - 1) & (i == ask_at))
        def _():
            for other_chip in range(3):
                gather.pass_on(fulls, sems, down, other_chip)

        x = x_ref[...]
        g = lax.dot_general(x, w_ref[j % 2, 0], NT_DIMS, preferred_element_type=F32)
        u = lax.dot_general(x, w_ref[j % 2, 1], NT_DIMS, preferred_element_type=F32)
        g_ref[...] = g.astype(BF16)
        u_ref[...] = u.astype(BF16)
        h_ref[...] = (g * _sigmoid(g) * u).astype(BF16)

        @pl.when((j == N_DEV - 1) & (i == ni - 1))
        def _():
            gather.wait_sibling(fulls, sems, down)
            for other_chip in range(3):
                gather.wait_passed(fulls, sems, down, other_chip)
            for a in range(n_g):
                gather.wait_sent(shards, fulls, sems, a)

    shp = jax.ShapeDtypeStruct((N_DEV, s, fs), BF16)
    o_spec = pl.BlockSpec((None, tm, fs), lambda j, i: (_arrival_block(j), i, 0))
    ins = ([] if gate_here else [gate]) + gather.inputs
    outs = pl.pallas_call(
        body, name=name, grid=(N_DEV, ni), in_specs=[pl.BlockSpec((tm, d), lambda j, i: (i, 0))] + [ANY] * len(ins),
        out_specs=[o_spec, o_spec, o_spec] + [ANY] * n_g, out_shape=[shp, shp, shp] + gather.out_shapes,
        scratch_shapes=[pltpu.VMEM((2, 2, fs, d), BF16), pltpu.SemaphoreType.DMA((2, 2))] + gather.scratch,
        compiler_params=pltpu.CompilerParams(dimension_semantics=("arbitrary", "arbitrary"), vmem_limit_bytes=V7X_VMEM_LIMIT),
    )(xb, *ins)
    return outs[:3], outs[3:]


def _down_ln(a3, w3, bias, res, res_affine, ln_g, ln_b, scale, name, target=None, rider=None):
    nk, s, tk = a3.shape
    d = w3.shape[2]
    tm = min(s, 256)
    final = target is not None

    def body(*refs):
        refs = list(refs)
        a_ref, w_hbm = refs[:2]
        del refs[:2]
        bias_ref = refs.pop(0) if bias is not None else None
        res_ref = refs.pop(0)
        rg_ref, rb_ref = (refs.pop(0), refs.pop(0)) if res_affine is not None else (None, None)
        g_ref, b_ref = refs.pop(0), refs.pop(0)
        t_ref = refs.pop(0) if final else None
        w_sem = refs.pop()
        w_ref = refs.pop()
        i = pl.program_id(0)
        if final:
            dr_ref, drb_ref, sq_ref, dg_ref, db_ref = refs
        else:
            xh_ref, hb_ref, rstd_ref = refs

        @pl.when(i == 0)
        def _():
            whole = pltpu.make_async_copy(w_hbm, w_ref, w_sem.at[0])
            whole.start()
            whole.wait()
            if final:
                sq_ref[...] = jnp.zeros_like(sq_ref)
                dg_ref[...] = jnp.zeros_like(dg_ref)
                db_ref[...] = jnp.zeros_like(db_ref)

        y = jnp.dot(a_ref[0], w_ref[0], preferred_element_type=F32)
        for k in range(1, nk):
            y = y + jnp.dot(a_ref[k], w_ref[k], preferred_element_type=F32)
        if bias_ref is not None:
            y = y + bias_ref[...]
        for rows in _row_blocks(tm):
            r = res_ref[rows, :]
            if rg_ref is not None:
                r = r * rg_ref[...] + rb_ref[...]
            r = ALPHA * r + scale * y[rows]
            mu = jnp.mean(r, axis=-1, keepdims=True)
            c = r - mu
            var = jnp.mean(c * c, axis=-1, keepdims=True)
            rstd = lax.rsqrt(var + LN_EPS)
            xhat = c * rstd
            h = xhat * g_ref[...] + b_ref[...]
            if not final:
                xh_ref[rows, :] = xhat
                hb_ref[rows, :] = h.astype(BF16)
                rstd_ref[rows, :] = rstd
            else:
                err = h - t_ref[rows, :]
                sq_ref[...] += jnp.sum(err * err, axis=0, keepdims=True)
                dh = err * (1.0 / d)
                dg_ref[...] += jnp.sum(dh * xhat, axis=0, keepdims=True)
                db_ref[...] += jnp.sum(dh, axis=0, keepdims=True)
                dr = _ln_backward(dh, xhat, rstd, g_ref[...])
                dr_ref[rows, :] = dr
                drb_ref[rows, :] = (scale * dr).astype(BF16)

    tok = pl.BlockSpec((tm, d), lambda i: (i, 0))
    vec = pl.BlockSpec((1, d), lambda i: (0, 0))
    ins = [a3, w3]
    in_specs = [pl.BlockSpec((nk, tm, tk), lambda i: (0, i, 0)), ANY]
    if bias is not None:
        ins.append(bias)
        in_specs.append(vec)
    ins.append(res)
    in_specs.append(tok)
    if res_affine is not None:
        ins += list(res_affine)
        in_specs += [vec, vec]
    ins += [ln_g, ln_b]
    in_specs += [vec, vec]
    if final:
        ins.append(target)
        in_specs.append(tok)
        out_shape = [jax.ShapeDtypeStruct((s, d), F32), jax.ShapeDtypeStruct((s, d), BF16)] + [jax.ShapeDtypeStruct((1, d), F32)] * 3
        out_specs = [tok, tok, vec, vec, vec]
    else:
        out_shape = [jax.ShapeDtypeStruct((s, d), F32), jax.ShapeDtypeStruct((s, d), BF16), jax.ShapeDtypeStruct((s, 1), F32)]
        out_specs = [tok, tok, pl.BlockSpec((tm, 1), lambda i: (i, 0))]
    scratch = [pltpu.VMEM((nk, tk, d), BF16), pltpu.SemaphoreType.DMA((1,))]
    return _call(body, name, (s // tm,), in_specs, out_specs, out_shape, scratch, ins, rider)


def _proj_in(hb, w3, bias, name, rider=None):
    s, d = hb.shape
    nsh, _, cs = w3.shape
    tm = min(s, 1024)

    def body(h_ref, w_ref, b_ref, z_ref):
        z_ref[...] = (jnp.dot(h_ref[...], w_ref[...], preferred_element_type=F32) + b_ref[...]).astype(BF16)

    in_specs = [pl.BlockSpec((tm, d), lambda i, j: (i, 0)), pl.BlockSpec((None, d, cs), lambda i, j: (j, 0, 0)),
                pl.BlockSpec((1, cs), lambda i, j: (0, j))]
    return _call(body, name, (s // tm, nsh), in_specs, pl.BlockSpec((tm, cs), lambda i, j: (i, j)),
                 jax.ShapeDtypeStruct((s, nsh * cs), BF16), [], [hb, w3, bias], rider)


def _nt_hidden(ab, w3, name):
    s, kdim = ab.shape
    nj, tn, _ = w3.shape
    tm = min(s, 1024)

    def body(a_ref, w_ref, o_ref):
        o_ref[...] = lax.dot_general(a_ref[...], w_ref[...], NT_DIMS, preferred_element_type=F32).astype(BF16)

    in_specs = [pl.BlockSpec((tm, kdim), lambda i, j: (i, 0)), pl.BlockSpec((None, tn, kdim), lambda i, j: (j, 0, 0))]
    return _call(body, name, (s // tm, nj), in_specs, pl.BlockSpec((None, tm, tn), lambda i, j: (j, i, 0)),
                 jax.ShapeDtypeStruct((nj, s, tn), BF16), [], [ab, w3])[0]


def _ffn_bwd_hidden(ab, w3, gate3, up3, name, rider=None):
    s, kdim = ab.shape
    nj, tn, _ = w3.shape
    tm = min(s, 1024)

    def body(a_ref, w_ref, g_ref, u_ref, dg_ref, du_ref):
        a = a_ref[...]
        for c0 in range(0, tn, MXU_COLS):
            cols = slice(c0, min(c0 + MXU_COLS, tn))
            t = lax.dot_general(a, w_ref[cols, :], NT_DIMS, preferred_element_type=F32)
            g = g_ref[:, cols].astype(F32)
            sg = 0.5 * jnp.tanh(0.5 * g) + 0.5
            silu = g * sg
            du_ref[:, cols] = (t * silu).astype(BF16)
            dg_ref[:, cols] = (t * u_ref[:, cols].astype(F32) * (sg + silu * (1.0 - sg))).astype(BF16)

    hid = pl.BlockSpec((None, tm, tn), lambda i, j: (j, i, 0))
    shp = jax.ShapeDtypeStruct((nj, s, tn), BF16)
    in_specs = [pl.BlockSpec((tm, kdim), lambda i, j: (i, 0)), pl.BlockSpec((None, tn, kdim), lambda i, j: (j, 0, 0)), hid, hid]
    return _call(body, name, (s // tm, nj), in_specs, [hid, hid], [shp, shp], [], [ab, w3, gate3, up3], rider)


def _tn_dw(a, a_spec, b, b_spec, nj, m, n, s, tk, name, rider):
    def body(a_ref, b_ref, o_ref, acc_ref):
        k = pl.program_id(1)

        @pl.when(k == 0)
        def _():
            acc_ref[...] = jnp.zeros_like(acc_ref)

        acc_ref[...] += lax.dot_general(a_ref[...], b_ref[...], TN_DIMS, preferred_element_type=F32)

        @pl.when(k == s // tk - 1)
        def _():
            o_ref[...] = acc_ref[...].astype(BF16)

    return _call(body, name, (nj, s // tk), [a_spec, b_spec], pl.BlockSpec((None, m, n), lambda j, k: (j, 0, 0)),
                 jax.ShapeDtypeStruct((nj, m, n), BF16), [pltpu.VMEM((m, n), F32)], [a, b], rider)


def _dw_hidden_rows(hid3, db, name, rider=None):
    nj, s, fs = hid3.shape
    d = db.shape[1]
    tk = min(s, DW_TOKENS)
    return _tn_dw(hid3, pl.BlockSpec((None, tk, fs), lambda j, k: (j, k, 0)), db, pl.BlockSpec((tk, d), lambda j, k: (k, 0)),
                  nj, fs, d, s, tk, name, rider)


def _dw_cols(ab, dz, nj, name, rider=None):
    s, d = ab.shape
    cs = dz.shape[1] // nj
    tk = min(s, DW_TOKENS)
    return _tn_dw(ab, pl.BlockSpec((tk, d), lambda j, k: (k, 0)), dz, pl.BlockSpec((tk, cs), lambda j, k: (k, j)),
                  nj, d, cs, s, tk, name, rider)


def _ffn_bwd_input(dg3, wg3, du3, wu3, dres, ln, name, rider=None):
    s, d = dres.shape
    nk, _, fs = dg3.shape
    tm = min(s, 512)

    def body(*refs):
        refs = list(refs)
        dg_in, wg_ref, du_in, wu_ref, dres_ref = refs[:5]
        del refs[:5]
        if ln is not None:
            xh_ref, rstd_ref, gain_ref = refs.pop(0), refs.pop(0), refs.pop(0)
        acc_ref = refs.pop()
        i, k = pl.program_id(0), pl.program_id(1)

        @pl.when(k == 0)
        def _():
            acc_ref[...] = jnp.zeros_like(acc_ref)

        acc_ref[...] += (jnp.dot(dg_in[...], wg_ref[...], preferred_element_type=F32)
                         + jnp.dot(du_in[...], wu_ref[...], preferred_element_type=F32))

        @pl.when(k == nk - 1)
        def _():
            if ln is not None:
                dr_ref, drb_ref, dg_ref, db_ref, sum_ref = refs

                @pl.when(i == 0)
                def _():
                    dg_ref[...] = jnp.zeros_like(dg_ref)
                    db_ref[...] = jnp.zeros_like(db_ref)
                    sum_ref[...] = jnp.zeros_like(sum_ref)

            for rows in _row_blocks(tm):
                dh = ALPHA * dres_ref[rows, :] + acc_ref[rows, :]
                if ln is None:
                    refs[0][rows, :] = dh
                else:
                    xhat = xh_ref[rows, :]
                    dg_ref[...] += jnp.sum(dh * xhat, axis=0, keepdims=True)
                    db_ref[...] += jnp.sum(dh, axis=0, keepdims=True)
                    dr = _ln_backward(dh, xhat, rstd_ref[rows, :], gain_ref[...])
                    sum_ref[...] += jnp.sum(dr, axis=0, keepdims=True)
                    dr_ref[rows, :] = dr
                    drb_ref[rows, :] = dr.astype(BF16)

    tok = pl.BlockSpec((tm, d), lambda i, k: (i, 0))
    vec = pl.BlockSpec((1, d), lambda i, k: (0, 0))
    a_spec = pl.BlockSpec((None, tm, fs), lambda i, k: (k, i, 0))
    w_spec = pl.BlockSpec((None, fs, d), lambda i, k: (k, 0, 0))
    ins, in_specs = [dg3, wg3, du3, wu3, dres], [a_spec, w_spec, a_spec, w_spec, tok]
    if ln is None:
        out_shape, out_specs = jax.ShapeDtypeStruct((s, d), F32), tok
    else:
        ins += list(ln)
        in_specs += [tok, pl.BlockSpec((tm, 1), lambda i, k: (i, 0)), vec]
        out_shape = [jax.ShapeDtypeStruct((s, d), F32), jax.ShapeDtypeStruct((s, d), BF16)] + [jax.ShapeDtypeStruct((1, d), F32)] * 3
        out_specs = [tok, tok, vec, vec, vec]
    return _call(body, name, (s // tm, nk), in_specs, out_specs, out_shape, [pltpu.VMEM((tm, d), F32)], ins, rider)


def _proj_in_bwd_ln(dz, w3, dres, ln, branch_scale, name):
    s, d = dres.shape
    nj, _, cs = w3.shape
    tm = min(s, 256)

    def body(dz_ref, w_hbm, dres_ref, xh_ref, rstd_ref, gain_ref, dr_ref, drb_ref, dg_ref, db_ref, w_ref, w_sem):
        @pl.when(pl.program_id(0) == 0)
        def _():
            whole = pltpu.make_async_copy(w_hbm, w_ref, w_sem.at[0])
            whole.start()
            whole.wait()
            dg_ref[...] = jnp.zeros_like(dg_ref)
            db_ref[...] = jnp.zeros_like(db_ref)

        acc = lax.dot_general(dz_ref[:, 0:cs], w_ref[0], NT_DIMS, preferred_element_type=F32)
        for j in range(1, nj):
            acc = acc + lax.dot_general(dz_ref[:, j * cs:(j + 1) * cs], w_ref[j], NT_DIMS, preferred_element_type=F32)
        for rows in _row_blocks(tm):
            dh = ALPHA * dres_ref[rows, :] + acc[rows]
            xhat = xh_ref[rows, :]
            dg_ref[...] += jnp.sum(dh * xhat, axis=0, keepdims=True)
            db_ref[...] += jnp.sum(dh, axis=0, keepdims=True)
            dr = _ln_backward(dh, xhat, rstd_ref[rows, :], gain_ref[...])
            dr_ref[rows, :] = dr
            drb_ref[rows, :] = (branch_scale * dr).astype(BF16)

    tok = pl.BlockSpec((tm, d), lambda i: (i, 0))
    vec = pl.BlockSpec((1, d), lambda i: (0, 0))
    in_specs = [pl.BlockSpec((tm, nj * cs), lambda i: (i, 0)), ANY, tok, tok, pl.BlockSpec((tm, 1), lambda i: (i, 0)), vec]
    out_shape = [jax.ShapeDtypeStruct((s, d), F32), jax.ShapeDtypeStruct((s, d), BF16)] + [jax.ShapeDtypeStruct((1, d), F32)] * 2
    scratch = [pltpu.VMEM(w3.shape, BF16), pltpu.SemaphoreType.DMA((1,))]
    return _call(body, name, (s // tm,), in_specs, [tok, tok, vec, vec], out_shape, scratch, [dz, w3, dres] + list(ln))[0]


def _shift_rows_down(v, halo, k, row):
    out = pltpu.roll(v, k, 0)
    hr = halo.shape[0]
    for r in range(k):
        out = jnp.where(row == r, halo[hr - k + r:hr - k + r + 1, :], out)
    return out


def _shift_rows_up(v, halo, k, row):
    t = v.shape[0]
    out = pltpu.roll(v, t - k, 0)
    for r in range(k):
        out = jnp.where(row == t - k + r, halo[r:r + 1, :], out)
    return out


def _sgu_head_forward(z_ref, h, da, gv_ref, bv_ref):
    zu = z_ref[:, h * HEAD:(h + 1) * HEAD].astype(F32)
    zv = z_ref[:, da + h * HEAD:da + (h + 1) * HEAD].astype(F32)
    u = _gelu(zu)
    v = _gelu(zv)
    mu = jnp.mean(v, axis=-1, keepdims=True)
    c = v - mu
    rstd = lax.rsqrt(jnp.mean(c * c, axis=-1, keepdims=True) + LN_EPS)
    vhat = c * rstd
    vln = (vhat * gv_ref[h:h + 1, :] + bv_ref[h:h + 1, :]).astype(BF16)
    return zu, zv, u, vhat, rstd, vln


def _mixer_fwd(z, ws_masked, bs_wide, gv, bv, cw, name):
    s, zc = z.shape
    da = zc // 5
    nh = da // HEAD
    tm = min(s, 512)
    hb = tm // BF16_ROWS

    def body(z_ref, pc_ref, px_ref, ws_ref, bs_ref, gv_ref, bv_ref, cw_ref, y_ref):
        i = pl.program_id(0)
        for h in range(nh):
            _, _, u, _, _, vln = _sgu_head_forward(z_ref, h, da, gv_ref, bv_ref)
            for n in range(tm // CHUNK):
                rows = slice(n * CHUNK, (n + 1) * CHUNK)
                mixed = jnp.dot(ws_ref[h], vln[rows], preferred_element_type=F32) + bs_ref[h]
                y_ref[0, rows, h * HEAD:(h + 1) * HEAD] = (u[rows] * mixed).astype(BF16)
        gate_b = z_ref[:, 2 * da:3 * da].astype(F32)
        hc = z_ref[:, 3 * da:4 * da].astype(F32) * z_ref[:, 4 * da:5 * da].astype(F32)
        halo = jnp.where(i > 0, pc_ref[...].astype(F32) * px_ref[...].astype(F32), 0.0)
        row = lax.broadcasted_iota(jnp.int32, (tm, da), 0)
        y = cw_ref[0:1, :] * _shift_rows_down(hc, halo, 2, row) + cw_ref[1:2, :] * _shift_rows_down(hc, halo, 1, row) + cw_ref[2:3, :] * hc
        y_ref[1] = (gate_b * y).astype(BF16)

    prev = lambda col: pl.BlockSpec((BF16_ROWS, da), lambda i: (jnp.maximum(i * hb - 1, 0), col))
    in_specs = [pl.BlockSpec((tm, zc), lambda i: (i, 0)), prev(3), prev(4), _row((nh, CHUNK, CHUNK)), _row((nh, CHUNK, HEAD)),
                _row((nh, HEAD)), _row((nh, HEAD)), _row((CONV_TAPS, da))]
    return _call(body, name, (s // tm,), in_specs, pl.BlockSpec((2, tm, da), lambda i: (0, i, 0)),
                 jax.ShapeDtypeStruct((2, s, da), BF16), [], [z, z, z, ws_masked, bs_wide, gv, bv, cw])[0]


def _mixer_bwd(z, dy, ws_masked, bs_wide, gv, bv, cw, name, rider=None):
    s, zc = z.shape
    da = zc // 5
    nh = da // HEAD
    tm = min(s, 512)
    hb = tm // BF16_ROWS
    nblk = s // tm

    def body(z_ref, pc_ref, px_ref, nb_ref, dy_ref, ndy_ref, ws_ref, bs_ref, gv_ref, bv_ref, cw_ref,
             dz_ref, dws_ref, dbs_ref, dgv_ref, dbv_ref, dcw_ref, dbin_ref):
        i = pl.program_id(0)

        @pl.when(i == 0)
        def _():
            for ref in (dws_ref, dbs_ref, dgv_ref, dbv_ref, dcw_ref, dbin_ref):
                ref[...] = jnp.zeros_like(ref)

        causal = lax.broadcasted_iota(jnp.int32, (CHUNK, CHUNK), 0) >= lax.broadcasted_iota(jnp.int32, (CHUNK, CHUNK), 1)
        for h in range(nh):
            zu, zv, u, vhat, rstd, vln = _sgu_head_forward(z_ref, h, da, gv_ref, bv_ref)
            dya = dy_ref[0, :, h * HEAD:(h + 1) * HEAD].astype(F32)
            w = ws_ref[h]
            du_parts, dvln_parts = [], []
            for n in range(tm // CHUNK):
                rows = slice(n * CHUNK, (n + 1) * CHUNK)
                mixed = jnp.dot(w, vln[rows], preferred_element_type=F32) + bs_ref[h]
                du_parts.append(dya[rows] * mixed)
                dmix = dya[rows] * u[rows]
                dmix_b = dmix.astype(BF16)
                dws_ref[h] += jnp.where(causal, lax.dot_general(dmix_b, vln[rows], NT_DIMS, preferred_element_type=F32), 0.0)
                dbs_ref[h] += dmix
                dvln_parts.append(lax.dot_general(w, dmix_b, TN_DIMS, preferred_element_type=F32))
            du = jnp.concatenate(du_parts, axis=0)
            dvln = jnp.concatenate(dvln_parts, axis=0)
            dgv_ref[h:h + 1, :] += jnp.sum(dvln * vhat, axis=0, keepdims=True)
            dbv_ref[h:h + 1, :] += jnp.sum(dvln, axis=0, keepdims=True)
            dv = _ln_backward(dvln, vhat, rstd, gv_ref[h:h + 1, :])
            dzu = du * _gelu_grad(zu)
            dzv = dv * _gelu_grad(zv)
            ucols = slice(h * HEAD, (h + 1) * HEAD)
            vcols = slice(da + h * HEAD, da + (h + 1) * HEAD)
            dz_ref[:, ucols] = dzu.astype(BF16)
            dz_ref[:, vcols] = dzv.astype(BF16)
            dbin_ref[:, ucols] += jnp.sum(dzu, axis=0, keepdims=True)
            dbin_ref[:, vcols] += jnp.sum(dzv, axis=0, keepdims=True)

        gate_b = z_ref[:, 2 * da:3 * da].astype(F32)
        gate_c = z_ref[:, 3 * da:4 * da].astype(F32)
        xt = z_ref[:, 4 * da:5 * da].astype(F32)
        hc = gate_c * xt
        halo = jnp.where(i > 0, pc_ref[...].astype(F32) * px_ref[...].astype(F32), 0.0)
        row = lax.broadcasted_iota(jnp.int32, (tm, da), 0)
        sh1 = _shift_rows_down(hc, halo, 1, row)
        sh2 = _shift_rows_down(hc, halo, 2, row)
        y = cw_ref[0:1, :] * sh2 + cw_ref[1:2, :] * sh1 + cw_ref[2:3, :] * hc
        dyb = dy_ref[1].astype(F32)
        dconv = dyb * gate_b
        nhalo = jnp.where(i < nblk - 1, ndy_ref[...].astype(F32) * nb_ref[...].astype(F32), 0.0)
        dhc = cw_ref[2:3, :] * dconv + cw_ref[1:2, :] * _shift_rows_up(dconv, nhalo, 1, row) + cw_ref[0:1, :] * _shift_rows_up(dconv, nhalo, 2, row)
        dcw_ref[0:1, :] += jnp.sum(dconv * sh2, axis=0, keepdims=True)
        dcw_ref[1:2, :] += jnp.sum(dconv * sh1, axis=0, keepdims=True)
        dcw_ref[2:3, :] += jnp.sum(dconv * hc, axis=0, keepdims=True)
        for col, val in ((2, dyb * y), (3, dhc * xt), (4, dhc * gate_c)):
            cols = slice(col * da, (col + 1) * da)
            dz_ref[:, cols] = val.astype(BF16)
            dbin_ref[:, cols] += jnp.sum(val, axis=0, keepdims=True)

        @pl.when(i == nblk - 1)
        def _():
            for h in range(nh):
                dbs_ref[h] = jnp.broadcast_to(jnp.sum(dbs_ref[h], axis=1, keepdims=True), (CHUNK, HEAD))

    prev = lambda col: pl.BlockSpec((BF16_ROWS, da), lambda i: (jnp.maximum(i * hb - 1, 0), col))
    nxt = lambda i: jnp.minimum((i + 1) * hb, s // BF16_ROWS - 1)
    in_specs = [pl.BlockSpec((tm, zc), lambda i: (i, 0)), prev(3), prev(4), pl.BlockSpec((BF16_ROWS, da), lambda i: (nxt(i), 2)),
                pl.BlockSpec((2, tm, da), lambda i: (0, i, 0)), pl.BlockSpec((None, BF16_ROWS, da), lambda i: (1, nxt(i), 0)),
                _row((nh, CHUNK, CHUNK)), _row((nh, CHUNK, HEAD)), _row((nh, HEAD)), _row((nh, HEAD)), _row((CONV_TAPS, da))]
    out_specs = [pl.BlockSpec((tm, zc), lambda i: (i, 0)), _row((nh, CHUNK, CHUNK)), _row((nh, CHUNK, HEAD)), _row((nh, HEAD)),
                 _row((nh, HEAD)), _row((8, da)), _row((1, zc))]
    out_shape = [jax.ShapeDtypeStruct((s, zc), BF16), jax.ShapeDtypeStruct((nh, CHUNK, CHUNK), F32),
                 jax.ShapeDtypeStruct((nh, CHUNK, HEAD), F32), jax.ShapeDtypeStruct((nh, HEAD), F32),
                 jax.ShapeDtypeStruct((nh, HEAD), F32), jax.ShapeDtypeStruct((8, da), F32), jax.ShapeDtypeStruct((1, zc), F32)]
    return _call(body, name, (nblk,), in_specs, out_specs, out_shape, [], [z, z, z, z, dy, dy, ws_masked, bs_wide, gv, bv, cw], rider)


def _adam_update(g, w, m, v):
    m_new = ADAM_B1 * m + (1.0 - ADAM_B1) * g
    v_new = ADAM_B2 * v + (1.0 - ADAM_B2) * (g * g)
    m_hat = m_new / (1.0 - ADAM_B1 ** ADAM_STEP)
    v_hat = v_new / (1.0 - ADAM_B2 ** ADAM_STEP)
    return -ADAM_LR * (m_hat / (jnp.sqrt(v_hat) + ADAM_EPS) + ADAM_WD * w), m_new, v_new


def _adamw(gparts, w, m, v, name):
    n, r, c = gparts.shape
    tr = r // 4 if (r // 4) % BF16_ROWS == 0 else r

    def body(g_ref, w_ref, m_ref, v_ref, go_ref, d_ref, mo_ref, vo_ref):
        g = g_ref[0].astype(F32)
        for q in range(1, n):
            g = g + g_ref[q].astype(F32)
        go_ref[...] = g
        d_ref[...], mo_ref[...], vo_ref[...] = _adam_update(g, w_ref[...], m_ref[...], v_ref[...])

    blk = pl.BlockSpec((tr, c), lambda i: (i, 0))
    shp = jax.ShapeDtypeStruct((r, c), F32)
    return _call(body, name, (r // tr,), [pl.BlockSpec((n, tr, c), lambda i: (0, i, 0)), blk, blk, blk], [blk] * 4, [shp] * 4, [],
                 [gparts, w, m, v])[0]


def _adamw_small(packs, rows, w, m, v, conv, name):
    n_par, n_dev = len(rows), packs.shape[0]
    taps = conv[0].shape[0]

    def body(*refs):
        refs = list(refs)
        cut = lambda n: [refs.pop(0) for _ in range(n)]
        p_ref, w_refs, m_refs, v_refs, (cw_ref, cm_ref, cv_ref) = refs.pop(0), cut(n_par), cut(n_par), cut(n_par), cut(3)
        outs = [cut(4) for _ in range(n_par + 1)]
        at = 0
        for k in range(n_par):
            g = p_ref[0, at:at + rows[k], :]
            for dev in range(1, n_dev):
                g = g + p_ref[dev, at:at + rows[k], :]
            go_ref, d_ref, mo_ref, vo_ref = outs[k]
            go_ref[...] = g
            d_ref[...], mo_ref[...], vo_ref[...] = _adam_update(g, w_refs[k][...], m_refs[k][...], v_refs[k][...])
            at += rows[k]
        me = 4 * lax.axis_index("x") + 2 * lax.axis_index("y") + lax.axis_index("c")
        go_ref, d_ref, mo_ref, vo_ref = outs[n_par]
        for tap in range(taps):
            row = pl.ds(at + tap * n_dev + me, 1)
            g = p_ref[0, row, :]
            for dev in range(1, n_dev):
                g = g + p_ref[dev, row, :]
            one = slice(tap, tap + 1)
            go_ref[one, :] = g
            d_ref[one, :], mo_ref[one, :], vo_ref[one, :] = _adam_update(g, cw_ref[one, :], cm_ref[one, :], cv_ref[one, :])

    vmem = pl.BlockSpec(memory_space=pltpu.VMEM)
    ins = [packs] + list(w) + list(m) + list(v) + list(conv)
    out_shape = [jax.ShapeDtypeStruct(a.shape, F32) for a in list(w) + [conv[0]] for _ in range(4)]
    outs = pl.pallas_call(body, name=name, in_specs=[vmem] * len(ins), out_specs=[vmem] * len(out_shape), out_shape=out_shape,
                          compiler_params=pltpu.CompilerParams(vmem_limit_bytes=V7X_VMEM_LIMIT))(*ins)
    return [outs[4 * k:4 * k + 4] for k in range(n_par + 1)]


def _rows128(a):
    return a.reshape(-1, LANES)


def kernel(x, ffa_gate, ffa_up, ffa_down, ln_a_g, ln_a_b, w_in, b_in, w_s, b_s, ln_v_g, ln_v_b, conv_w, w_out, b_out, ln_m_g, ln_m_b, ffc_gate, ffc_up, ffc_down, ln_c_g, ln_c_b, loss_target, m_ffa_gate, m_ffa_up, m_ffa_down, m_ln_a_g, m_ln_a_b, m_w_in, m_b_in, m_w_s, m_b_s, m_ln_v_g, m_ln_v_b, m_conv_w, m_w_out, m_b_out, m_ln_m_g, m_ln_m_b, m_ffc_gate, m_ffc_up, m_ffc_down, m_ln_c_g, m_ln_c_b, v_ffa_gate, v_ffa_up, v_ffa_down, v_ln_a_g, v_ln_a_b, v_w_in, v_b_in, v_w_s, v_b_s, v_ln_v_g, v_ln_v_b, v_conv_w, v_w_out, v_b_out, v_ln_m_g, v_ln_m_b, v_ffc_gate, v_ffc_up, v_ffc_down, v_ln_c_g, v_ln_c_b):
    x2, target = x[0], loss_target[0]
    s, d = x2.shape
    da = d // 2
    nh = da // HEAD

    big = dict(ffa_gate=ffa_gate, ffa_up=ffa_up, ffa_down=ffa_down, w_in=w_in, w_out=w_out, ffc_gate=ffc_gate, ffc_up=ffc_up, ffc_down=ffc_down)
    big_m = dict(ffa_gate=m_ffa_gate, ffa_up=m_ffa_up, ffa_down=m_ffa_down, w_in=m_w_in, w_out=m_w_out, ffc_gate=m_ffc_gate, ffc_up=m_ffc_up, ffc_down=m_ffc_down)
    big_v = dict(ffa_gate=v_ffa_gate, ffa_up=v_ffa_up, ffa_down=v_ffa_down, w_in=v_w_in, w_out=v_w_out, ffc_gate=v_ffc_gate, ffc_up=v_ffc_up, ffc_down=v_ffc_down)
    local = lambda k, a: jnp.transpose(a[0]) if k in TRANSPOSED else a[0]
    shard = {k: local(k, w).astype(BF16) for k, w in big.items()}
    conv_rows = jnp.pad(conv_w[0], ((0, 8 - CONV_TAPS), (0, 0)))

    tril = jnp.tril(jnp.ones((CHUNK, CHUNK), dtype=bool))
    ws_masked = jnp.where(tril[None], w_s[0], 0.0).astype(BF16)
    bs_wide = jnp.broadcast_to(b_s[0][:, :, None], (nh, CHUNK, HEAD))
    gv, bv = ln_v_g.reshape(nh, HEAD), ln_v_b.reshape(nh, HEAD)

    full = {}
    xb = x2.astype(BF16)
    (g_a, u_a, hid_a), (full["ffa_gate"], full["ffa_up"], full["ffa_down"]) = _ffn_gateup_gathering(
        xb, shard["ffa_gate"], shard["ffa_up"], shard["ffa_down"], "ffa_gateup")
    (xhat1, h1b, rstd1), (full["w_in"], full["w_out"], conv_full) = _down_ln(
        hid_a, full["ffa_down"], None, x2, None, ln_a_g, ln_a_b, 0.5, "ffa_down_ln", rider=_Gather([shard["w_in"], shard["w_out"], conv_rows]))
    cw = jnp.transpose(conv_full[:, :CONV_TAPS, :], (1, 0, 2)).reshape(CONV_TAPS, da)
    w_out2 = full["w_out"].reshape(2, da, d)
    z, (full["ffc_gate"],) = _proj_in(h1b, full["w_in"], b_in, "proj_in", _Gather([shard["ffc_gate"]]))
    ycat = _mixer_fwd(z, ws_masked, bs_wide, gv, bv, cw, "mixer_fwd")
    (xhat2, h2b, rstd2), _ = _down_ln(ycat, w_out2, b_out, xhat1, (ln_a_g, ln_a_b), ln_m_g, ln_m_b, 1.0, "proj_out_ln")
    (g_c, u_c, hid_c), (full["ffc_up"], full["ffc_down"]) = _ffn_gateup_gathering(
        h2b, full["ffc_gate"], shard["ffc_up"], shard["ffc_down"], "ffc_gateup")
    (dr3, dr3b, sq_err, d_ln_c_g, d_ln_c_b), _ = _down_ln(hid_c, full["ffc_down"], None, xhat2, (ln_m_g, ln_m_b), ln_c_g, ln_c_b, 0.5,
                                                          "ffc_down_ln_loss", target=target)
    loss = lax.psum((0.5 / d) * jnp.sum(sq_err), ("x", "y", "c"))

    landed = {}
    (dg_c, du_c), _ = _ffn_bwd_hidden(dr3b, full["ffc_down"], g_c, u_c, "ffc_bwd_hidden")
    part, _ = _dw_hidden_rows(hid_c, dr3b, "ffc_dw_down")
    part, (landed["ffc_down"],) = _dw_hidden_rows(dg_c, h2b, "ffc_dw_gate", _Scatter(part))
    part, (landed["ffc_gate"],) = _dw_hidden_rows(du_c, h2b, "ffc_dw_up", _Scatter(part))
    (dr2, dr2b, d_ln_m_g, d_ln_m_b, d_b_out), (landed["ffc_up"],) = _ffn_bwd_input(
        dg_c, full["ffc_gate"], du_c, full["ffc_up"], dr3, (xhat2, rstd2, ln_m_g), "ffc_bwd_input_ln", _Scatter(part))
    dycat = _nt_hidden(dr2b, w_out2, "proj_out_bwd")
    part, _ = _dw_hidden_rows(ycat, dr2b, "proj_out_dw")
    (dz, d_w_s, d_b_s_wide, d_gv, d_bv, d_cw, d_b_in), (landed["w_out"],) = _mixer_bwd(
        z, dycat, ws_masked, bs_wide, gv, bv, cw, "mixer_bwd", _Scatter(part.reshape(N_DEV, d // N_DEV, d)))
    dr1, dr1b, d_ln_a_g, d_ln_a_b = _proj_in_bwd_ln(dz, full["w_in"], dr2, (xhat1, rstd1, ln_a_g), 0.5, "proj_in_bwd_ln")
    small_g = dict(ln_a_g=d_ln_a_g, ln_a_b=d_ln_a_b, b_in=d_b_in, w_s=d_w_s, b_s=d_b_s_wide[:, :, 0], ln_v_g=d_gv, ln_v_b=d_bv, b_out=d_b_out,
                   ln_m_g=d_ln_m_g, ln_m_b=d_ln_m_b, ln_c_g=d_ln_c_g, ln_c_b=d_ln_c_b)
    pack = jnp.concatenate([_rows128(g) for g in small_g.values()] + [_rows128(d_cw[:CONV_TAPS])], axis=0)
    part, (packs,) = _dw_cols(h1b, dz, N_DEV, "proj_in_dw", _Scatter(pack, whole=True))
    (dg_a, du_a), (landed["w_in"],) = _ffn_bwd_hidden(dr1b, full["ffa_down"], g_a, u_a, "ffa_bwd_hidden", _Scatter(part))
    part, _ = _dw_hidden_rows(hid_a, dr1b, "ffa_dw_down")
    part, (landed["ffa_down"],) = _dw_hidden_rows(dg_a, xb, "ffa_dw_gate", _Scatter(part))
    part, (landed["ffa_gate"],) = _dw_hidden_rows(du_a, xb, "ffa_dw_up", _Scatter(part))
    grad_x, (landed["ffa_up"],) = _ffn_bwd_input(dg_a, full["ffa_gate"], du_a, full["ffa_up"], dr1, None, "ffa_bwd_input", _Scatter(part))

    grads, deltas, new_m, new_v = {}, {}, {}, {}
    for k in big:
        out = _adamw(landed[k], local(k, big[k]), local(k, big_m[k]), local(k, big_v[k]), "adamw_" + k)
        grads[k], deltas[k], new_m[k], new_v[k] = ((jnp.transpose(o) if k in TRANSPOSED else o).reshape(big[k].shape) for o in out)

    small = dict(ln_a_g=ln_a_g, ln_a_b=ln_a_b, b_in=b_in, w_s=w_s, b_s=b_s, ln_v_g=ln_v_g, ln_v_b=ln_v_b, b_out=b_out,
                 ln_m_g=ln_m_g, ln_m_b=ln_m_b, ln_c_g=ln_c_g, ln_c_b=ln_c_b)
    small_m = dict(ln_a_g=m_ln_a_g, ln_a_b=m_ln_a_b, b_in=m_b_in, w_s=m_w_s, b_s=m_b_s, ln_v_g=m_ln_v_g, ln_v_b=m_ln_v_b, b_out=m_b_out,
                   ln_m_g=m_ln_m_g, ln_m_b=m_ln_m_b, ln_c_g=m_ln_c_g, ln_c_b=m_ln_c_b)
    small_v = dict(ln_a_g=v_ln_a_g, ln_a_b=v_ln_a_b, b_in=v_b_in, w_s=v_w_s, b_s=v_b_s, ln_v_g=v_ln_v_g, ln_v_b=v_ln_v_b, b_out=v_b_out,
                   ln_m_g=v_ln_m_g, ln_m_b=v_ln_m_b, ln_c_g=v_ln_c_g, ln_c_b=v_ln_c_b)
    snames = list(small)
    assert snames == list(small_g) and conv_w.shape[2] == LANES and da == N_DEV * LANES
    views = lambda tree: [_rows128(tree[k]) for k in snames]
    out = _adamw_small(packs, [a.shape[0] for a in views(small)], views(small), views(small_m), views(small_v),
                       (conv_w[0], m_conv_w[0], v_conv_w[0]), "adamw_small")
    for k, per_param in zip(snames + ["conv_w"], out):
        shape = conv_w.shape if k == "conv_w" else small[k].shape
        grads[k], deltas[k], new_m[k], new_v[k] = (o.reshape(shape) for o in per_param)

    order = ["ffa_gate", "ffa_up", "ffa_down", "ln_a_g", "ln_a_b", "w_in", "b_in", "w_s", "b_s", "ln_v_g", "ln_v_b", "conv_w", "w_out", "b_out",
             "ln_m_g", "ln_m_b", "ffc_gate", "ffc_up", "ffc_down", "ln_c_g", "ln_c_b"]
    return (loss, grad_x[None], *[grads[k] for k in order], *[deltas[k] for k in order], *[new_m[k] for k in order], *[new_v[k] for k in order])
```

```python
import math

import jax
import jax.numpy as jnp
from jax import lax
from jax.experimental import pallas as pl
from jax.experimental.pallas import tpu as pltpu

BF16 = jnp.bfloat16
F32 = jnp.float32
MESH = pl.DeviceIdType.MESH

N_DEV = 8
HEAD = 128
CHUNK = 128
CONV_TAPS = 3
LN_EPS = 1e-5
ALPHA = float(2 ** 0.25)
GELU_C = 0.7978845608028654
GELU_A = 0.044715
ADAM_LR, ADAM_B1, ADAM_B2, ADAM_EPS, ADAM_WD, ADAM_STEP = 0.001, 0.9, 0.999, 1e-08, 0.01, 10
V7X_VMEM_LIMIT = 56 * 1024 * 1024
LANES = 128
BF16_ROWS = 16
MXU_COLS = 256
TRANSPOSED = ("ffa_gate", "ffa_up", "ffc_gate", "ffc_up")
DW_TOKENS = 2048

NT_DIMS = (((1,), (1,)), ((), ()))
TN_DIMS = (((0,), (0,)), ((), ()))
ANY = pl.BlockSpec(memory_space=pl.ANY)


def _gelu(x):
    return 0.5 * x * (1.0 + jnp.tanh(GELU_C * (x + GELU_A * x * x * x)))


def _gelu_grad(x):
    t = jnp.tanh(GELU_C * (x + GELU_A * x * x * x))
    return 0.5 * (1.0 + t) + 0.5 * x * (1.0 - t * t) * GELU_C * (1.0 + 3.0 * GELU_A * x * x)


def _sigmoid(x):
    return 1.0 / (1.0 + jnp.exp(-x))


def _row(shape):
    return pl.BlockSpec(shape, lambda *_: (0,) * len(shape))


def _row_blocks(tm, rows=128):
    rows = min(rows, tm)
    return [slice(r, r + rows) for r in range(0, tm, rows)]


def _ln_backward(dh, xhat, rstd, gain):
    dxh = dh * gain
    m1 = jnp.mean(dxh, axis=-1, keepdims=True)
    m2 = jnp.mean(dxh * xhat, axis=-1, keepdims=True)
    return rstd * (dxh - m1 - xhat * m2)


def _place():
    x, y, c = lax.axis_index("x"), lax.axis_index("y"), lax.axis_index("c")
    return x, y, c, [(1 - x, y), (x, 1 - y), (1 - x, 1 - y)]


def _other_devices(x, y, c):
    flips = [(bx, by, bc) for bx in (0, 1) for by in (0, 1) for bc in (0, 1)][1:]
    return [(1 - x if bx else x, 1 - y if by else y, 1 - c if bc else c) for bx, by, bc in flips]


class _Gather:
    def __init__(self, shards, forward_at=0.75):
        n = len(shards)
        self.n, self.forward_at = n, forward_at
        self.inputs = list(shards)
        self.out_shapes = [jax.ShapeDtypeStruct((N_DEV,) + a.shape, a.dtype) for a in shards]
        self.scratch = [pltpu.SemaphoreType.DMA((n, 7)), pltpu.SemaphoreType.DMA((n, 7)), pltpu.SemaphoreType.DMA((n,))]

    def _copy(self, outs, sems, a, k, block, to, src=None):
        dst = outs[a].at[block]
        return pltpu.make_async_remote_copy(src_ref=dst if src is None else src, dst_ref=dst, send_sem=sems[0].at[a, k],
                                            recv_sem=sems[1].at[a, k], device_id=to, device_id_type=MESH)

    def start(self, ins, outs, sems, urgent=None):
        x, y, c, chips = _place()
        me = 4 * x + 2 * y + c
        for a in range(self.n):
            pltpu.make_async_copy(ins[a], outs[a].at[me], sems[2].at[a]).start()
        urgent = list(range(self.n)) if urgent is None else list(urgent)
        for group in (urgent, [a for a in range(self.n) if a not in urgent]):
            for a in group:
                self._copy(outs, sems, a, 0, me, (x, y, 1 - c), src=ins[a]).start()
                for j in (0, 1):
                    self._copy(outs, sems, a, 1 + j, me, (*chips[j], c), src=ins[a]).start()
            for a in group:
                self._copy(outs, sems, a, 3, me, (*chips[2], c), src=ins[a]).start()

    def wait_sibling(self, outs, sems, a):
        x, y, c, _ = _place()
        self._copy(outs, sems, a, 0, 4 * x + 2 * y + 1 - c, (x, y, 1 - c)).wait_recv()

    def pass_on(self, outs, sems, a, j):
        x, y, c, chips = _place()
        block = 4 * chips[j][0] + 2 * chips[j][1] + c
        self._copy(outs, sems, a, 1 + j, block, (x, y, 1 - c)).wait_recv()
        self._copy(outs, sems, a, 4 + j, block, (x, y, 1 - c)).start()

    def wait_passed(self, outs, sems, a, j):
        x, y, c, chips = _place()
        self._copy(outs, sems, a, 4 + j, 4 * chips[j][0] + 2 * chips[j][1] + 1 - c, (x, y, 1 - c)).wait_recv()

    def wait_sent(self, ins, outs, sems, a):
        x, y, c, _ = _place()
        me = 4 * x + 2 * y + c
        for k in range(7):
            self._copy(outs, sems, a, k, me, (x, y, 1 - c), src=ins[a]).wait_send()
        pltpu.make_async_copy(ins[a], outs[a].at[me], sems[2].at[a]).wait()

    def forward(self, ins, outs, sems):
        for a in range(self.n):
            for j in range(3):
                self.pass_on(outs, sems, a, j)

    def finish(self, ins, outs, sems):
        for a in range(self.n):
            self.wait_sibling(outs, sems, a)
            for j in range(3):
                self.wait_passed(outs, sems, a, j)
        for a in range(self.n):
            self.wait_sent(ins, outs, sems, a)

    def before(self, step, n_steps, ins, outs, sems):
        pl.when(step == 0)(lambda: self.start(ins, outs, sems))
        pl.when(step == int(self.forward_at * (n_steps - 1)))(lambda: self.forward(ins, outs, sems))

    def after(self, step, n_steps, ins, outs, sems):
        pl.when(step == n_steps - 1)(lambda: self.finish(ins, outs, sems))


class _Scatter:
    def __init__(self, partial, whole=False):
        self.whole = whole
        self.inputs = [partial]
        self.out_shapes = [jax.ShapeDtypeStruct(((N_DEV,) if whole else ()) + partial.shape, partial.dtype)]
        self.scratch = [pltpu.SemaphoreType.DMA((7,)), pltpu.SemaphoreType.DMA((7,)), pltpu.SemaphoreType.DMA((1,))]

    def _copies(self, ins, outs, sems):
        x, y, c, _ = _place()
        me = 4 * x + 2 * y + c
        block = (lambda dev: ins[0]) if self.whole else (lambda dev: ins[0].at[dev])
        mine = pltpu.make_async_copy(block(me), outs[0].at[me], sems[2].at[0])
        remote = [pltpu.make_async_remote_copy(src_ref=block(4 * px + 2 * py + pc), dst_ref=outs[0].at[me], send_sem=sems[0].at[k],
                                               recv_sem=sems[1].at[k], device_id=(px, py, pc), device_id_type=MESH)
                  for k, (px, py, pc) in enumerate(_other_devices(x, y, c))]
        return mine, remote

    def start(self, ins, outs, sems):
        mine, remote = self._copies(ins, outs, sems)
        mine.start()
        for cp in remote:
            cp.start()

    def finish(self, ins, outs, sems):
        mine, remote = self._copies(ins, outs, sems)
        for cp in remote:
            cp.wait()
        mine.wait()

    def before(self, step, n_steps, ins, outs, sems):
        pl.when(step == 0)(lambda: self.start(ins, outs, sems))

    def after(self, step, n_steps, ins, outs, sems):
        pl.when(step == n_steps - 1)(lambda: self.finish(ins, outs, sems))


class _ChipScatter(_Scatter):
    def __init__(self, sums):
        super().__init__(sums)
        self.scratch = [pltpu.SemaphoreType.DMA((3,)), pltpu.SemaphoreType.DMA((3,)), pltpu.SemaphoreType.DMA((1,))]

    def _copies(self, ins, outs, sems):
        x, y, c, chips = _place()
        my_chip = 2 * x + y
        mine = pltpu.make_async_copy(ins[0].at[my_chip], outs[0].at[my_chip], sems[2].at[0])
        remote = [pltpu.make_async_remote_copy(src_ref=ins[0].at[2 * px + py], dst_ref=outs[0].at[my_chip], send_sem=sems[0].at[k],
                                               recv_sem=sems[1].at[k], device_id=(px, py, c), device_id_type=MESH)
                  for k, (px, py) in enumerate(chips)]
        return mine, remote


def _call(body, name, grid, in_specs, out_specs, out_shape, scratch, ins, rider=None):
    single = not isinstance(out_shape, (list, tuple))
    out_shape = [out_shape] if single else list(out_shape)
    out_specs = [out_specs] if single else list(out_specs)
    params = pltpu.CompilerParams(dimension_semantics=("arbitrary",) * len(grid), vmem_limit_bytes=V7X_VMEM_LIMIT)
    if rider is None:
        outs = pl.pallas_call(body, name=name, grid=grid, in_specs=in_specs, out_specs=out_specs, out_shape=out_shape,
                              scratch_shapes=scratch, compiler_params=params)(*ins)
        return (outs[0] if single else outs), None
    n_in, n_out, n_scr = len(ins), len(out_shape), len(scratch)
    r_in, r_out = len(rider.inputs), len(rider.out_shapes)
    n_steps = math.prod(grid)

    def carried(*refs):
        refs = list(refs)
        cut = lambda n: [refs.pop(0) for _ in range(n)]
        b_in, c_in, b_out, c_out, b_scr = cut(n_in), cut(r_in), cut(n_out), cut(r_out), cut(n_scr)
        step = 0
        for axis, size in enumerate(grid):
            step = step * size + pl.program_id(axis)
        rider.before(step, n_steps, c_in, c_out, refs)
        body(*b_in, *b_out, *b_scr)
        rider.after(step, n_steps, c_in, c_out, refs)

    outs = pl.pallas_call(
        carried, name=name, grid=grid, in_specs=list(in_specs) + [ANY] * r_in, out_specs=out_specs + [ANY] * r_out,
        out_shape=out_shape + rider.out_shapes, scratch_shapes=list(scratch) + rider.scratch, compiler_params=params,
    )(*ins, *rider.inputs)
    base = outs[:n_out]
    return (base[0] if single else base), outs[n_out:]


def _arrival_block(j):
    x, y, c = lax.axis_index("x"), lax.axis_index("y"), lax.axis_index("c")
    chip, other_core = j // 2, j % 2
    px = jnp.where((chip == 1) | (chip == 3), 1 - x, x)
    py = jnp.where((chip == 2) | (chip == 3), 1 - y, y)
    pc = jnp.where(other_core == 1, 1 - c, c)
    return 4 * px + 2 * py + pc


def _ffn_gateup_gathering(xb, gate, up_shard, down_shard, name):
    s, d = xb.shape
    fs = up_shard.shape[0]
    tm = min(s, 1024)
    ni = s // tm
    ask_at = max(ni - 2, 0)
    gate_here = gate.ndim == 2
    gather = _Gather(([gate] if gate_here else []) + [up_shard, down_shard])
    n_g = gather.n
    used_here, down = tuple(range(n_g - 1)), n_g - 1

    def body(x_ref, *refs):
        refs = list(refs)
        gate_full = None if gate_here else refs.pop(0)
        shards = [refs.pop(0) for _ in range(n_g)]
        g_ref, u_ref, h_ref = refs.pop(0), refs.pop(0), refs.pop(0)
        fulls = [refs.pop(0) for _ in range(n_g)]
        w_ref, w_sems = refs.pop(0), refs.pop(0)
        sems = refs
        j, i = pl.program_id(0), pl.program_id(1)
        gate_src, up_src = (fulls[0], fulls[1]) if gate_here else (gate_full, fulls[0])

        def load(slot, srcs):
            return [pltpu.make_async_copy(src, w_ref.at[slot, a], w_sems.at[slot, a]) for a, src in enumerate(srcs)]

        @pl.when((j == 0) & (i == 0))
        def _():
            gather.start(shards, fulls, sems, urgent=used_here)
            mine = load(0, (shards[0] if gate_here else gate_full.at[_arrival_block(0)], shards[n_g - 2]))
            for cp in mine:
                cp.start()
            for cp in mine:
                cp.wait()

        for nxt in range(1, N_DEV):
            @pl.when((j == nxt) & (i == 0))
            def _(nxt=nxt):
                for cp in load(nxt % 2, (gate_src.at[0], up_src.at[0])):
                    cp.wait()

        for nxt in range(1, N_DEV):
            @pl.when((j == nxt - 1) & (i == ask_at))
            def _(nxt=nxt):
                for a in used_here:
                    if nxt == 1:
                        gather.wait_sibling(fulls, sems, a)
                    elif nxt % 2 == 0:
                        gather.pass_on(fulls, sems, a, nxt // 2 - 1)
                    else:
                        gather.wait_passed(fulls, sems, a, nxt // 2 - 1)
                block = _arrival_block(nxt)
                for cp in load(nxt % 2, (gate_src.at[block], up_src.at[block])):
                    cp.start()

        @pl.when((j == N_DEV - 1) & (i == ask_at))
        def _():
            for other_chip in range(3):
                gather.pass_on(fulls, sems, down, other_chip)

        x = x_ref[...]
        g = lax.dot_general(x, w_ref[j % 2, 0], NT_DIMS, preferred_element_type=F32)
        u = lax.dot_general(x, w_ref[j % 2, 1], NT_DIMS, preferred_element_type=F32)
        g_ref[...] = g.astype(BF16)
        u_ref[...] = u.astype(BF16)
        h_ref[...] = (g * _sigmoid(g) * u).astype(BF16)

        @pl.when((j == N_DEV - 1) & (i == ni - 1))
        def _():
            gather.wait_sibling(fulls, sems, down)
            for other_chip in range(3):
                gather.wait_passed(fulls, sems, down, other_chip)
            for a in range(n_g):
                gather.wait_sent(shards, fulls, sems, a)

    shp = jax.ShapeDtypeStruct((N_DEV, s, fs), BF16)
    o_spec = pl.BlockSpec((None, tm, fs), lambda j, i: (_arrival_block(j), i, 0))
    ins = ([] if gate_here else [gate]) + gather.inputs
    outs = pl.pallas_call(
        body, name=name, grid=(N_DEV, ni), in_specs=[pl.BlockSpec((tm, d), lambda j, i: (i, 0))] + [ANY] * len(ins),
        out_specs=[o_spec, o_spec, o_spec] + [ANY] * n_g, out_shape=[shp, shp, shp] + gather.out_shapes,
        scratch_shapes=[pltpu.VMEM((2, 2, fs, d), BF16), pltpu.SemaphoreType.DMA((2, 2))] + gather.scratch,
        compiler_params=pltpu.CompilerParams(dimension_semantics=("arbitrary", "arbitrary"), vmem_limit_bytes=V7X_VMEM_LIMIT),
    )(xb, *ins)
    return outs[:3], outs[3:]


def _down_ln(a3, w3, bias, res, res_affine, ln_g, ln_b, scale, name, target=None, rider=None):
    nk, s, tk = a3.shape
    d = w3.shape[2]
    tm = min(s, 256)
    final = target is not None

    def body(*refs):
        refs = list(refs)
        a_ref, w_hbm = refs[:2]
        del refs[:2]
        bias_ref = refs.pop(0) if bias is not None else None
        res_ref = refs.pop(0)
        rg_ref, rb_ref = (refs.pop(0), refs.pop(0)) if res_affine is not None else (None, None)
        g_ref, b_ref = refs.pop(0), refs.pop(0)
        t_ref = refs.pop(0) if final else None
        w_sem = refs.pop()
        w_ref = refs.pop()
        i = pl.program_id(0)
        if final:
            dr_ref, drb_ref, sq_ref, dg_ref, db_ref = refs
        else:
            xh_ref, hb_ref, rstd_ref = refs

        @pl.when(i == 0)
        def _():
            whole = pltpu.make_async_copy(w_hbm, w_ref, w_sem.at[0])
            whole.start()
            whole.wait()
            if final:
                sq_ref[...] = jnp.zeros_like(sq_ref)
                dg_ref[...] = jnp.zeros_like(dg_ref)
                db_ref[...] = jnp.zeros_like(db_ref)

        y = jnp.dot(a_ref[0], w_ref[0], preferred_element_type=F32)
        for k in range(1, nk):
            y = y + jnp.dot(a_ref[k], w_ref[k], preferred_element_type=F32)
        if bias_ref is not None:
            y = y + bias_ref[...]
        for rows in _row_blocks(tm):
            r = res_ref[rows, :]
            if rg_ref is not None:
                r = r * rg_ref[...] + rb_ref[...]
            r = ALPHA * r + scale * y[rows]
            mu = jnp.mean(r, axis=-1, keepdims=True)
            c = r - mu
            var = jnp.mean(c * c, axis=-1, keepdims=True)
            rstd = lax.rsqrt(var + LN_EPS)
            xhat = c * rstd
            h = xhat * g_ref[...] + b_ref[...]
            if not final:
                xh_ref[rows, :] = xhat
                hb_ref[rows, :] = h.astype(BF16)
                rstd_ref[rows, :] = rstd
            else:
                err = h - t_ref[rows, :]
                sq_ref[...] += jnp.sum(err * err, axis=0, keepdims=True)
                dh = err * (1.0 / d)
                dg_ref[...] += jnp.sum(dh * xhat, axis=0, keepdims=True)
                db_ref[...] += jnp.sum(dh, axis=0, keepdims=True)
                dr = _ln_backward(dh, xhat, rstd, g_ref[...])
                dr_ref[rows, :] = dr
                drb_ref[rows, :] = (scale * dr).astype(BF16)

    tok = pl.BlockSpec((tm, d), lambda i: (i, 0))
    vec = pl.BlockSpec((1, d), lambda i: (0, 0))
    ins = [a3, w3]
    in_specs = [pl.BlockSpec((nk, tm, tk), lambda i: (0, i, 0)), ANY]
    if bias is not None:
        ins.append(bias)
        in_specs.append(vec)
    ins.append(res)
    in_specs.append(tok)
    if res_affine is not None:
        ins += list(res_affine)
        in_specs += [vec, vec]
    ins += [ln_g, ln_b]
    in_specs += [vec, vec]
    if final:
        ins.append(target)
        in_specs.append(tok)
        out_shape = [jax.ShapeDtypeStruct((s, d), F32), jax.ShapeDtypeStruct((s, d), BF16)] + [jax.ShapeDtypeStruct((1, d), F32)] * 3
        out_specs = [tok, tok, vec, vec, vec]
    else:
        out_shape = [jax.ShapeDtypeStruct((s, d), F32), jax.ShapeDtypeStruct((s, d), BF16), jax.ShapeDtypeStruct((s, 1), F32)]
        out_specs = [tok, tok, pl.BlockSpec((tm, 1), lambda i: (i, 0))]
    scratch = [pltpu.VMEM((nk, tk, d), BF16), pltpu.SemaphoreType.DMA((1,))]
    return _call(body, name, (s // tm,), in_specs, out_specs, out_shape, scratch, ins, rider)


def _proj_in(hb, w3, bias, name, rider=None):
    s, d = hb.shape
    nsh, _, cs = w3.shape
    tm = min(s, 1024)

    def body(h_ref, w_ref, b_ref, z_ref):
        z_ref[...] = (jnp.dot(h_ref[...], w_ref[...], preferred_element_type=F32) + b_ref[...]).astype(BF16)

    in_specs = [pl.BlockSpec((tm, d), lambda i, j: (i, 0)), pl.BlockSpec((None, d, cs), lambda i, j: (j, 0, 0)),
                pl.BlockSpec((1, cs), lambda i, j: (0, j))]
    return _call(body, name, (s // tm, nsh), in_specs, pl.BlockSpec((tm, cs), lambda i, j: (i, j)),
                 jax.ShapeDtypeStruct((s, nsh * cs), BF16), [], [hb, w3, bias], rider)


def _nt_hidden(ab, w3, name):
    s, kdim = ab.shape
    nj, tn, _ = w3.shape
    tm = min(s, 1024)

    def body(a_ref, w_ref, o_ref):
        o_ref[...] = lax.dot_general(a_ref[...], w_ref[...], NT_DIMS, preferred_element_type=F32).astype(BF16)

    in_specs = [pl.BlockSpec((tm, kdim), lambda i, j: (i, 0)), pl.BlockSpec((None, tn, kdim), lambda i, j: (j, 0, 0))]
    return _call(body, name, (s // tm, nj), in_specs, pl.BlockSpec((None, tm, tn), lambda i, j: (j, i, 0)),
                 jax.ShapeDtypeStruct((nj, s, tn), BF16), [], [ab, w3])[0]


def _ffn_bwd_hidden(ab, w3, gate3, up3, name, rider=None):
    s, kdim = ab.shape
    nj, tn, _ = w3.shape
    tm = min(s, 1024)

    def body(a_ref, w_ref, g_ref, u_ref, dg_ref, du_ref):
        a = a_ref[...]
        for c0 in range(0, tn, MXU_COLS):
            cols = slice(c0, min(c0 + MXU_COLS, tn))
            t = lax.dot_general(a, w_ref[cols, :], NT_DIMS, preferred_element_type=F32)
            g = g_ref[:, cols].astype(F32)
            sg = 0.5 * jnp.tanh(0.5 * g) + 0.5
            silu = g * sg
            du_ref[:, cols] = (t * silu).astype(BF16)
            dg_ref[:, cols] = (t * u_ref[:, cols].astype(F32) * (sg + silu * (1.0 - sg))).astype(BF16)

    hid = pl.BlockSpec((None, tm, tn), lambda i, j: (j, i, 0))
    shp = jax.ShapeDtypeStruct((nj, s, tn), BF16)
    in_specs = [pl.BlockSpec((tm, kdim), lambda i, j: (i, 0)), pl.BlockSpec((None, tn, kdim), lambda i, j: (j, 0, 0)), hid, hid]
    return _call(body, name, (s // tm, nj), in_specs, [hid, hid], [shp, shp], [], [ab, w3, gate3, up3], rider)


def _tn_dw(a, a_spec, b, b_spec, nj, m, n, s, tk, name, rider):
    def body(a_ref, b_ref, o_ref, acc_ref):
        k = pl.program_id(1)

        @pl.when(k == 0)
        def _():
            acc_ref[...] = jnp.zeros_like(acc_ref)

        acc_ref[...] += lax.dot_general(a_ref[...], b_ref[...], TN_DIMS, preferred_element_type=F32)

        @pl.when(k == s // tk - 1)
        def _():
            o_ref[...] = acc_ref[...].astype(BF16)

    return _call(body, name, (nj, s // tk), [a_spec, b_spec], pl.BlockSpec((None, m, n), lambda j, k: (j, 0, 0)),
                 jax.ShapeDtypeStruct((nj, m, n), BF16), [pltpu.VMEM((m, n), F32)], [a, b], rider)


def _dw_hidden_rows(hid3, db, name, rider=None):
    nj, s, fs = hid3.shape
    d = db.shape[1]
    tk = min(s, DW_TOKENS)
    return _tn_dw(hid3, pl.BlockSpec((None, tk, fs), lambda j, k: (j, k, 0)), db, pl.BlockSpec((tk, d), lambda j, k: (k, 0)),
                  nj, fs, d, s, tk, name, rider)


def _dw_hidden_rows_paired(hid3, db, name, rider=None):
    nj, s, fs = hid3.shape
    d = db.shape[1]
    tk = min(s, DW_TOKENS)
    nk = s // tk
    half = nj // 2

    def device_of(j):
        c = lax.axis_index("c")
        return 2 * (j % half) + jnp.where(j < half, 1 - c, c)

    def body(a_ref, b_ref, o_ref, theirs_ref, acc_ref, stage_ref, got_ref, send_sems, recv_sems, load_sem):
        j, k = pl.program_id(0), pl.program_id(1)
        x, y, c = lax.axis_index("x"), lax.axis_index("y"), lax.axis_index("c")

        def to_sibling(q):
            return pltpu.make_async_remote_copy(src_ref=stage_ref, dst_ref=theirs_ref.at[q], send_sem=send_sems.at[q],
                                                recv_sem=recv_sems.at[q], device_id=(x, y, 1 - c), device_id_type=MESH)

        @pl.when(k == 0)
        def _():
            acc_ref[...] = jnp.zeros_like(acc_ref)

        acc_ref[...] += lax.dot_general(a_ref[...], b_ref[...], TN_DIMS, preferred_element_type=F32)

        for q in range(half):
            @pl.when((j == q) & (k == nk - 1))
            def _(q=q):
                if q > 0:
                    to_sibling(q - 1).wait_send()
                stage_ref[...] = acc_ref[...].astype(BF16)
                to_sibling(q).start()

            @pl.when((j == half + q) & (k == nk - 1))
            def _(q=q):
                if q == 0:
                    to_sibling(half - 1).wait_send()
                to_sibling(q).wait_recv()
                load = pltpu.make_async_copy(theirs_ref.at[q], got_ref, load_sem.at[0])
                load.start()
                load.wait()
                o_ref[...] = (acc_ref[...] + got_ref[...].astype(F32)).astype(BF16)

    in_specs = [pl.BlockSpec((None, tk, fs), lambda j, k: (device_of(j), k, 0)), pl.BlockSpec((tk, d), lambda j, k: (k, 0))]
    out_specs = [pl.BlockSpec((None, fs, d), lambda j, k: (jnp.maximum(j - half, 0), 0, 0)), ANY]
    shp = jax.ShapeDtypeStruct((half, fs, d), BF16)
    scratch = [pltpu.VMEM((fs, d), F32), pltpu.VMEM((fs, d), BF16), pltpu.VMEM((fs, d), BF16),
               pltpu.SemaphoreType.DMA((half,)), pltpu.SemaphoreType.DMA((half,)), pltpu.SemaphoreType.DMA((1,))]
    (sums, _), riders_out = _call(body, name, (nj, nk), in_specs, out_specs, [shp, shp], scratch, [hid3, db], rider)
    return sums, riders_out


def _dw_cols(ab, dz, nj, name, rider=None):
    s, d = ab.shape
    cs = dz.shape[1] // nj
    tk = min(s, DW_TOKENS)
    return _tn_dw(ab, pl.BlockSpec((tk, d), lambda j, k: (k, 0)), dz, pl.BlockSpec((tk, cs), lambda j, k: (k, j)),
                  nj, d, cs, s, tk, name, rider)


def _ffn_bwd_input(dg3, wg3, du3, wu3, dres, ln, name, rider=None):
    s, d = dres.shape
    nk, _, fs = dg3.shape
    tm = min(s, 512)

    def body(*refs):
        refs = list(refs)
        dg_in, wg_ref, du_in, wu_ref, dres_ref = refs[:5]
        del refs[:5]
        if ln is not None:
            xh_ref, rstd_ref, gain_ref = refs.pop(0), refs.pop(0), refs.pop(0)
        acc_ref = refs.pop()
        i, k = pl.program_id(0), pl.program_id(1)

        @pl.when(k == 0)
        def _():
            acc_ref[...] = jnp.zeros_like(acc_ref)

        acc_ref[...] += (jnp.dot(dg_in[...], wg_ref[...], preferred_element_type=F32)
                         + jnp.dot(du_in[...], wu_ref[...], preferred_element_type=F32))

        @pl.when(k == nk - 1)
        def _():
            if ln is not None:
                dr_ref, drb_ref, dg_ref, db_ref, sum_ref = refs

                @pl.when(i == 0)
                def _():
                    dg_ref[...] = jnp.zeros_like(dg_ref)
                    db_ref[...] = jnp.zeros_like(db_ref)
                    sum_ref[...] = jnp.zeros_like(sum_ref)

            for rows in _row_blocks(tm):
                dh = ALPHA * dres_ref[rows, :] + acc_ref[rows, :]
                if ln is None:
                    refs[0][rows, :] = dh
                else:
                    xhat = xh_ref[rows, :]
                    dg_ref[...] += jnp.sum(dh * xhat, axis=0, keepdims=True)
                    db_ref[...] += jnp.sum(dh, axis=0, keepdims=True)
                    dr = _ln_backward(dh, xhat, rstd_ref[rows, :], gain_ref[...])
                    sum_ref[...] += jnp.sum(dr, axis=0, keepdims=True)
                    dr_ref[rows, :] = dr
                    drb_ref[rows, :] = dr.astype(BF16)

    tok = pl.BlockSpec((tm, d), lambda i, k: (i, 0))
    vec = pl.BlockSpec((1, d), lambda i, k: (0, 0))
    a_spec = pl.BlockSpec((None, tm, fs), lambda i, k: (k, i, 0))
    w_spec = pl.BlockSpec((None, fs, d), lambda i, k: (k, 0, 0))
    ins, in_specs = [dg3, wg3, du3, wu3, dres], [a_spec, w_spec, a_spec, w_spec, tok]
    if ln is None:
        out_shape, out_specs = jax.ShapeDtypeStruct((s, d), F32), tok
    else:
        ins += list(ln)
        in_specs += [tok, pl.BlockSpec((tm, 1), lambda i, k: (i, 0)), vec]
        out_shape = [jax.ShapeDtypeStruct((s, d), F32), jax.ShapeDtypeStruct((s, d), BF16)] + [jax.ShapeDtypeStruct((1, d), F32)] * 3
        out_specs = [tok, tok, vec, vec, vec]
    return _call(body, name, (s // tm, nk), in_specs, out_specs, out_shape, [pltpu.VMEM((tm, d), F32)], ins, rider)


def _proj_in_bwd_ln(dz, w3, dres, ln, branch_scale, name):
    s, d = dres.shape
    nj, _, cs = w3.shape
    tm = min(s, 256)

    def body(dz_ref, w_hbm, dres_ref, xh_ref, rstd_ref, gain_ref, dr_ref, drb_ref, dg_ref, db_ref, w_ref, w_sem):
        @pl.when(pl.program_id(0) == 0)
        def _():
            whole = pltpu.make_async_copy(w_hbm, w_ref, w_sem.at[0])
            whole.start()
            whole.wait()
            dg_ref[...] = jnp.zeros_like(dg_ref)
            db_ref[...] = jnp.zeros_like(db_ref)

        acc = lax.dot_general(dz_ref[:, 0:cs], w_ref[0], NT_DIMS, preferred_element_type=F32)
        for j in range(1, nj):
            acc = acc + lax.dot_general(dz_ref[:, j * cs:(j + 1) * cs], w_ref[j], NT_DIMS, preferred_element_type=F32)
        for rows in _row_blocks(tm):
            dh = ALPHA * dres_ref[rows, :] + acc[rows]
            xhat = xh_ref[rows, :]
            dg_ref[...] += jnp.sum(dh * xhat, axis=0, keepdims=True)
            db_ref[...] += jnp.sum(dh, axis=0, keepdims=True)
            dr = _ln_backward(dh, xhat, rstd_ref[rows, :], gain_ref[...])
            dr_ref[rows, :] = dr
            drb_ref[rows, :] = (branch_scale * dr).astype(BF16)

    tok = pl.BlockSpec((tm, d), lambda i: (i, 0))
    vec = pl.BlockSpec((1, d), lambda i: (0, 0))
    in_specs = [pl.BlockSpec((tm, nj * cs), lambda i: (i, 0)), ANY, tok, tok, pl.BlockSpec((tm, 1), lambda i: (i, 0)), vec]
    out_shape = [jax.ShapeDtypeStruct((s, d), F32), jax.ShapeDtypeStruct((s, d), BF16)] + [jax.ShapeDtypeStruct((1, d), F32)] * 2
    scratch = [pltpu.VMEM(w3.shape, BF16), pltpu.SemaphoreType.DMA((1,))]
    return _call(body, name, (s // tm,), in_specs, [tok, tok, vec, vec], out_shape, scratch, [dz, w3, dres] + list(ln))[0]


def _shift_rows_down(v, halo, k, row):
    out = pltpu.roll(v, k, 0)
    hr = halo.shape[0]
    for r in range(k):
        out = jnp.where(row == r, halo[hr - k + r:hr - k + r + 1, :], out)
    return out


def _shift_rows_up(v, halo, k, row):
    t = v.shape[0]
    out = pltpu.roll(v, t - k, 0)
    for r in range(k):
        out = jnp.where(row == t - k + r, halo[r:r + 1, :], out)
    return out


def _sgu_head_forward(z_ref, h, da, gv_ref, bv_ref):
    zu = z_ref[:, h * HEAD:(h + 1) * HEAD].astype(F32)
    zv = z_ref[:, da + h * HEAD:da + (h + 1) * HEAD].astype(F32)
    u = _gelu(zu)
    v = _gelu(zv)
    mu = jnp.mean(v, axis=-1, keepdims=True)
    c = v - mu
    rstd = lax.rsqrt(jnp.mean(c * c, axis=-1, keepdims=True) + LN_EPS)
    vhat = c * rstd
    vln = (vhat * gv_ref[h:h + 1, :] + bv_ref[h:h + 1, :]).astype(BF16)
    return zu, zv, u, vhat, rstd, vln


def _mixer_fwd(z, ws_masked, bs_wide, gv, bv, cw, name):
    s, zc = z.shape
    da = zc // 5
    nh = da // HEAD
    tm = min(s, 512)
    hb = tm // BF16_ROWS

    def body(z_ref, pc_ref, px_ref, ws_ref, bs_ref, gv_ref, bv_ref, cw_ref, y_ref):
        i = pl.program_id(0)
        for h in range(nh):
            _, _, u, _, _, vln = _sgu_head_forward(z_ref, h, da, gv_ref, bv_ref)
            for n in range(tm // CHUNK):
                rows = slice(n * CHUNK, (n + 1) * CHUNK)
                mixed = jnp.dot(ws_ref[h], vln[rows], preferred_element_type=F32) + bs_ref[h]
                y_ref[0, rows, h * HEAD:(h + 1) * HEAD] = (u[rows] * mixed).astype(BF16)
        gate_b = z_ref[:, 2 * da:3 * da].astype(F32)
        hc = z_ref[:, 3 * da:4 * da].astype(F32) * z_ref[:, 4 * da:5 * da].astype(F32)
        halo = jnp.where(i > 0, pc_ref[...].astype(F32) * px_ref[...].astype(F32), 0.0)
        row = lax.broadcasted_iota(jnp.int32, (tm, da), 0)
        y = cw_ref[0:1, :] * _shift_rows_down(hc, halo, 2, row) + cw_ref[1:2, :] * _shift_rows_down(hc, halo, 1, row) + cw_ref[2:3, :] * hc
        y_ref[1] = (gate_b * y).astype(BF16)

    prev = lambda col: pl.BlockSpec((BF16_ROWS, da), lambda i: (jnp.maximum(i * hb - 1, 0), col))
    in_specs = [pl.BlockSpec((tm, zc), lambda i: (i, 0)), prev(3), prev(4), _row((nh, CHUNK, CHUNK)), _row((nh, CHUNK, HEAD)),
                _row((nh, HEAD)), _row((nh, HEAD)), _row((CONV_TAPS, da))]
    return _call(body, name, (s // tm,), in_specs, pl.BlockSpec((2, tm, da), lambda i: (0, i, 0)),
                 jax.ShapeDtypeStruct((2, s, da), BF16), [], [z, z, z, ws_masked, bs_wide, gv, bv, cw])[0]


def _mixer_bwd(z, dy, ws_masked, bs_wide, gv, bv, cw, name, rider=None):
    s, zc = z.shape
    da = zc // 5
    nh = da // HEAD
    tm = min(s, 512)
    hb = tm // BF16_ROWS
    nblk = s // tm

    def body(z_ref, pc_ref, px_ref, nb_ref, dy_ref, ndy_ref, ws_ref, bs_ref, gv_ref, bv_ref, cw_ref,
             dz_ref, dws_ref, dbs_ref, dgv_ref, dbv_ref, dcw_ref, dbin_ref):
        i = pl.program_id(0)

        @pl.when(i == 0)
        def _():
            for ref in (dws_ref, dbs_ref, dgv_ref, dbv_ref, dcw_ref, dbin_ref):
                ref[...] = jnp.zeros_like(ref)

        causal = lax.broadcasted_iota(jnp.int32, (CHUNK, CHUNK), 0) >= lax.broadcasted_iota(jnp.int32, (CHUNK, CHUNK), 1)
        for h in range(nh):
            zu, zv, u, vhat, rstd, vln = _sgu_head_forward(z_ref, h, da, gv_ref, bv_ref)
            dya = dy_ref[0, :, h * HEAD:(h + 1) * HEAD].astype(F32)
            w = ws_ref[h]
            du_parts, dvln_parts = [], []
            for n in range(tm // CHUNK):
                rows = slice(n * CHUNK, (n + 1) * CHUNK)
                mixed = jnp.dot(w, vln[rows], preferred_element_type=F32) + bs_ref[h]
                du_parts.append(dya[rows] * mixed)
                dmix = dya[rows] * u[rows]
                dmix_b = dmix.astype(BF16)
                dws_ref[h] += jnp.where(causal, lax.dot_general(dmix_b, vln[rows], NT_DIMS, preferred_element_type=F32), 0.0)
                dbs_ref[h] += dmix
                dvln_parts.append(lax.dot_general(w, dmix_b, TN_DIMS, preferred_element_type=F32))
            du = jnp.concatenate(du_parts, axis=0)
            dvln = jnp.concatenate(dvln_parts, axis=0)
            dgv_ref[h:h + 1, :] += jnp.sum(dvln * vhat, axis=0, keepdims=True)
            dbv_ref[h:h + 1, :] += jnp.sum(dvln, axis=0, keepdims=True)
            dv = _ln_backward(dvln, vhat, rstd, gv_ref[h:h + 1, :])
            dzu = du * _gelu_grad(zu)
            dzv = dv * _gelu_grad(zv)
            ucols = slice(h * HEAD, (h + 1) * HEAD)
            vcols = slice(da + h * HEAD, da + (h + 1) * HEAD)
            dz_ref[:, ucols] = dzu.astype(BF16)
            dz_ref[:, vcols] = dzv.astype(BF16)
            dbin_ref[:, ucols] += jnp.sum(dzu, axis=0, keepdims=True)
            dbin_ref[:, vcols] += jnp.sum(dzv, axis=0, keepdims=True)

        gate_b = z_ref[:, 2 * da:3 * da].astype(F32)
        gate_c = z_ref[:, 3 * da:4 * da].astype(F32)
        xt = z_ref[:, 4 * da:5 * da].astype(F32)
        hc = gate_c * xt
        halo = jnp.where(i > 0, pc_ref[...].astype(F32) * px_ref[...].astype(F32), 0.0)
        row = lax.broadcasted_iota(jnp.int32, (tm, da), 0)
        sh1 = _shift_rows_down(hc, halo, 1, row)
        sh2 = _shift_rows_down(hc, halo, 2, row)
        y = cw_ref[0:1, :] * sh2 + cw_ref[1:2, :] * sh1 + cw_ref[2:3, :] * hc
        dyb = dy_ref[1].astype(F32)
        dconv = dyb * gate_b
        nhalo = jnp.where(i < nblk - 1, ndy_ref[...].astype(F32) * nb_ref[...].astype(F32), 0.0)
        dhc = cw_ref[2:3, :] * dconv + cw_ref[1:2, :] * _shift_rows_up(dconv, nhalo, 1, row) + cw_ref[0:1, :] * _shift_rows_up(dconv, nhalo, 2, row)
        dcw_ref[0:1, :] += jnp.sum(dconv * sh2, axis=0, keepdims=True)
        dcw_ref[1:2, :] += jnp.sum(dconv * sh1, axis=0, keepdims=True)
        dcw_ref[2:3, :] += jnp.sum(dconv * hc, axis=0, keepdims=True)
        for col, val in ((2, dyb * y), (3, dhc * xt), (4, dhc * gate_c)):
            cols = slice(col * da, (col + 1) * da)
            dz_ref[:, cols] = val.astype(BF16)
            dbin_ref[:, cols] += jnp.sum(val, axis=0, keepdims=True)

        @pl.when(i == nblk - 1)
        def _():
            for h in range(nh):
                dbs_ref[h] = jnp.broadcast_to(jnp.sum(dbs_ref[h], axis=1, keepdims=True), (CHUNK, HEAD))

    prev = lambda col: pl.BlockSpec((BF16_ROWS, da), lambda i: (jnp.maximum(i * hb - 1, 0), col))
    nxt = lambda i: jnp.minimum((i + 1) * hb, s // BF16_ROWS - 1)
    in_specs = [pl.BlockSpec((tm, zc), lambda i: (i, 0)), prev(3), prev(4), pl.BlockSpec((BF16_ROWS, da), lambda i: (nxt(i), 2)),
                pl.BlockSpec((2, tm, da), lambda i: (0, i, 0)), pl.BlockSpec((None, BF16_ROWS, da), lambda i: (1, nxt(i), 0)),
                _row((nh, CHUNK, CHUNK)), _row((nh, CHUNK, HEAD)), _row((nh, HEAD)), _row((nh, HEAD)), _row((CONV_TAPS, da))]
    out_specs = [pl.BlockSpec((tm, zc), lambda i: (i, 0)), _row((nh, CHUNK, CHUNK)), _row((nh, CHUNK, HEAD)), _row((nh, HEAD)),
                 _row((nh, HEAD)), _row((8, da)), _row((1, zc))]
    out_shape = [jax.ShapeDtypeStruct((s, zc), BF16), jax.ShapeDtypeStruct((nh, CHUNK, CHUNK), F32),
                 jax.ShapeDtypeStruct((nh, CHUNK, HEAD), F32), jax.ShapeDtypeStruct((nh, HEAD), F32),
                 jax.ShapeDtypeStruct((nh, HEAD), F32), jax.ShapeDtypeStruct((8, da), F32), jax.ShapeDtypeStruct((1, zc), F32)]
    return _call(body, name, (nblk,), in_specs, out_specs, out_shape, [], [z, z, z, z, dy, dy, ws_masked, bs_wide, gv, bv, cw], rider)


def _adam_update(g, w, m, v):
    m_new = ADAM_B1 * m + (1.0 - ADAM_B1) * g
    v_new = ADAM_B2 * v + (1.0 - ADAM_B2) * (g * g)
    m_hat = m_new / (1.0 - ADAM_B1 ** ADAM_STEP)
    v_hat = v_new / (1.0 - ADAM_B2 ** ADAM_STEP)
    return -ADAM_LR * (m_hat / (jnp.sqrt(v_hat) + ADAM_EPS) + ADAM_WD * w), m_new, v_new


def _adamw(gparts, w, m, v, name):
    n, r, c = gparts.shape
    tr = r // 4 if (r // 4) % BF16_ROWS == 0 else r

    def body(g_ref, w_ref, m_ref, v_ref, go_ref, d_ref, mo_ref, vo_ref):
        g = g_ref[0].astype(F32)
        for q in range(1, n):
            g = g + g_ref[q].astype(F32)
        go_ref[...] = g
        d_ref[...], mo_ref[...], vo_ref[...] = _adam_update(g, w_ref[...], m_ref[...], v_ref[...])

    blk = pl.BlockSpec((tr, c), lambda i: (i, 0))
    shp = jax.ShapeDtypeStruct((r, c), F32)
    return _call(body, name, (r // tr,), [pl.BlockSpec((n, tr, c), lambda i: (0, i, 0)), blk, blk, blk], [blk] * 4, [shp] * 4, [],
                 [gparts, w, m, v])[0]


def _adamw_small(packs, rows, w, m, v, conv, name):
    n_par, n_dev = len(rows), packs.shape[0]
    taps = conv[0].shape[0]

    def body(*refs):
        refs = list(refs)
        cut = lambda n: [refs.pop(0) for _ in range(n)]
        p_ref, w_refs, m_refs, v_refs, (cw_ref, cm_ref, cv_ref) = refs.pop(0), cut(n_par), cut(n_par), cut(n_par), cut(3)
        outs = [cut(4) for _ in range(n_par + 1)]
        at = 0
        for k in range(n_par):
            g = p_ref[0, at:at + rows[k], :]
            for dev in range(1, n_dev):
                g = g + p_ref[dev, at:at + rows[k], :]
            go_ref, d_ref, mo_ref, vo_ref = outs[k]
            go_ref[...] = g
            d_ref[...], mo_ref[...], vo_ref[...] = _adam_update(g, w_refs[k][...], m_refs[k][...], v_refs[k][...])
            at += rows[k]
        me = 4 * lax.axis_index("x") + 2 * lax.axis_index("y") + lax.axis_index("c")
        go_ref, d_ref, mo_ref, vo_ref = outs[n_par]
        for tap in range(taps):
            row = pl.ds(at + tap * n_dev + me, 1)
            g = p_ref[0, row, :]
            for dev in range(1, n_dev):
                g = g + p_ref[dev, row, :]
            one = slice(tap, tap + 1)
            go_ref[one, :] = g
            d_ref[one, :], mo_ref[one, :], vo_ref[one, :] = _adam_update(g, cw_ref[one, :], cm_ref[one, :], cv_ref[one, :])

    vmem = pl.BlockSpec(memory_space=pltpu.VMEM)
    ins = [packs] + list(w) + list(m) + list(v) + list(conv)
    out_shape = [jax.ShapeDtypeStruct(a.shape, F32) for a in list(w) + [conv[0]] for _ in range(4)]
    outs = pl.pallas_call(body, name=name, in_specs=[vmem] * len(ins), out_specs=[vmem] * len(out_shape), out_shape=out_shape,
                          compiler_params=pltpu.CompilerParams(vmem_limit_bytes=V7X_VMEM_LIMIT))(*ins)
    return [outs[4 * k:4 * k + 4] for k in range(n_par + 1)]


def _rows128(a):
    return a.reshape(-1, LANES)


def kernel(x, ffa_gate, ffa_up, ffa_down, ln_a_g, ln_a_b, w_in, b_in, w_s, b_s, ln_v_g, ln_v_b, conv_w, w_out, b_out, ln_m_g, ln_m_b, ffc_gate, ffc_up, ffc_down, ln_c_g, ln_c_b, loss_target, m_ffa_gate, m_ffa_up, m_ffa_down, m_ln_a_g, m_ln_a_b, m_w_in, m_b_in, m_w_s, m_b_s, m_ln_v_g, m_ln_v_b, m_conv_w, m_w_out, m_b_out, m_ln_m_g, m_ln_m_b, m_ffc_gate, m_ffc_up, m_ffc_down, m_ln_c_g, m_ln_c_b, v_ffa_gate, v_ffa_up, v_ffa_down, v_ln_a_g, v_ln_a_b, v_w_in, v_b_in, v_w_s, v_b_s, v_ln_v_g, v_ln_v_b, v_conv_w, v_w_out, v_b_out, v_ln_m_g, v_ln_m_b, v_ffc_gate, v_ffc_up, v_ffc_down, v_ln_c_g, v_ln_c_b):
    x2, target = x[0], loss_target[0]
    s, d = x2.shape
    da = d // 2
    nh = da // HEAD

    big = dict(ffa_gate=ffa_gate, ffa_up=ffa_up, ffa_down=ffa_down, w_in=w_in, w_out=w_out, ffc_gate=ffc_gate, ffc_up=ffc_up, ffc_down=ffc_down)
    big_m = dict(ffa_gate=m_ffa_gate, ffa_up=m_ffa_up, ffa_down=m_ffa_down, w_in=m_w_in, w_out=m_w_out, ffc_gate=m_ffc_gate, ffc_up=m_ffc_up, ffc_down=m_ffc_down)
    big_v = dict(ffa_gate=v_ffa_gate, ffa_up=v_ffa_up, ffa_down=v_ffa_down, w_in=v_w_in, w_out=v_w_out, ffc_gate=v_ffc_gate, ffc_up=v_ffc_up, ffc_down=v_ffc_down)
    local = lambda k, a: jnp.transpose(a[0]) if k in TRANSPOSED else a[0]
    shard = {k: local(k, w).astype(BF16) for k, w in big.items()}
    conv_rows = jnp.pad(conv_w[0], ((0, 8 - CONV_TAPS), (0, 0)))

    tril = jnp.tril(jnp.ones((CHUNK, CHUNK), dtype=bool))
    ws_masked = jnp.where(tril[None], w_s[0], 0.0).astype(BF16)
    bs_wide = jnp.broadcast_to(b_s[0][:, :, None], (nh, CHUNK, HEAD))
    gv, bv = ln_v_g.reshape(nh, HEAD), ln_v_b.reshape(nh, HEAD)

    full = {}
    xb = x2.astype(BF16)
    (g_a, u_a, hid_a), (full["ffa_gate"], full["ffa_up"], full["ffa_down"]) = _ffn_gateup_gathering(
        xb, shard["ffa_gate"], shard["ffa_up"], shard["ffa_down"], "ffa_gateup")
    (xhat1, h1b, rstd1), (full["w_in"], full["w_out"], conv_full) = _down_ln(
        hid_a, full["ffa_down"], None, x2, None, ln_a_g, ln_a_b, 0.5, "ffa_down_ln", rider=_Gather([shard["w_in"], shard["w_out"], conv_rows]))
    cw = jnp.transpose(conv_full[:, :CONV_TAPS, :], (1, 0, 2)).reshape(CONV_TAPS, da)
    w_out2 = full["w_out"].reshape(2, da, d)
    z, (full["ffc_gate"],) = _proj_in(h1b, full["w_in"], b_in, "proj_in", _Gather([shard["ffc_gate"]]))
    ycat = _mixer_fwd(z, ws_masked, bs_wide, gv, bv, cw, "mixer_fwd")
    (xhat2, h2b, rstd2), _ = _down_ln(ycat, w_out2, b_out, xhat1, (ln_a_g, ln_a_b), ln_m_g, ln_m_b, 1.0, "proj_out_ln")
    (g_c, u_c, hid_c), (full["ffc_up"], full["ffc_down"]) = _ffn_gateup_gathering(
        h2b, full["ffc_gate"], shard["ffc_up"], shard["ffc_down"], "ffc_gateup")
    (dr3, dr3b, sq_err, d_ln_c_g, d_ln_c_b), _ = _down_ln(hid_c, full["ffc_down"], None, xhat2, (ln_m_g, ln_m_b), ln_c_g, ln_c_b, 0.5,
                                                          "ffc_down_ln_loss", target=target)
    loss = lax.psum((0.5 / d) * jnp.sum(sq_err), ("x", "y", "c"))

    landed = {}
    (dg_c, du_c), _ = _ffn_bwd_hidden(dr3b, full["ffc_down"], g_c, u_c, "ffc_bwd_hidden")
    part, _ = _dw_hidden_rows_paired(hid_c, dr3b, "ffc_dw_down")
    part, (landed["ffc_down"],) = _dw_hidden_rows_paired(dg_c, h2b, "ffc_dw_gate", _ChipScatter(part))
    part, (landed["ffc_gate"],) = _dw_hidden_rows_paired(du_c, h2b, "ffc_dw_up", _ChipScatter(part))
    (dr2, dr2b, d_ln_m_g, d_ln_m_b, d_b_out), (landed["ffc_up"],) = _ffn_bwd_input(
        dg_c, full["ffc_gate"], du_c, full["ffc_up"], dr3, (xhat2, rstd2, ln_m_g), "ffc_bwd_input_ln", _ChipScatter(part))
    dycat = _nt_hidden(dr2b, w_out2, "proj_out_bwd")
    part, _ = _dw_hidden_rows(ycat, dr2b, "proj_out_dw")
    (dz, d_w_s, d_b_s_wide, d_gv, d_bv, d_cw, d_b_in), (landed["w_out"],) = _mixer_bwd(
        z, dycat, ws_masked, bs_wide, gv, bv, cw, "mixer_bwd", _Scatter(part.reshape(N_DEV, d // N_DEV, d)))
    dr1, dr1b, d_ln_a_g, d_ln_a_b = _proj_in_bwd_ln(dz, full["w_in"], dr2, (xhat1, rstd1, ln_a_g), 0.5, "proj_in_bwd_ln")
    small_g = dict(ln_a_g=d_ln_a_g, ln_a_b=d_ln_a_b, b_in=d_b_in, w_s=d_w_s, b_s=d_b_s_wide[:, :, 0], ln_v_g=d_gv, ln_v_b=d_bv, b_out=d_b_out,
                   ln_m_g=d_ln_m_g, ln_m_b=d_ln_m_b, ln_c_g=d_ln_c_g, ln_c_b=d_ln_c_b)
    pack = jnp.concatenate([_rows128(g) for g in small_g.values()] + [_rows128(d_cw[:CONV_TAPS])], axis=0)
    part, (packs,) = _dw_cols(h1b, dz, N_DEV, "proj_in_dw", _Scatter(pack, whole=True))
    (dg_a, du_a), (landed["w_in"],) = _ffn_bwd_hidden(dr1b, full["ffa_down"], g_a, u_a, "ffa_bwd_hidden", _Scatter(part))
    part, _ = _dw_hidden_rows_paired(hid_a, dr1b, "ffa_dw_down")
    part, (landed["ffa_down"],) = _dw_hidden_rows_paired(dg_a, xb, "ffa_dw_gate", _ChipScatter(part))
    part, (landed["ffa_gate"],) = _dw_hidden_rows_paired(du_a, xb, "ffa_dw_up", _ChipScatter(part))
    grad_x, (landed["ffa_up"],) = _ffn_bwd_input(dg_a, full["ffa_gate"], du_a, full["ffa_up"], dr1, None, "ffa_bwd_input", _ChipScatter(part))

    grads, deltas, new_m, new_v = {}, {}, {}, {}
    for k in big:
        out = _adamw(landed[k], local(k, big[k]), local(k, big_m[k]), local(k, big_v[k]), "adamw_" + k)
        grads[k], deltas[k], new_m[k], new_v[k] = ((jnp.transpose(o) if k in TRANSPOSED else o).reshape(big[k].shape) for o in out)

    small = dict(ln_a_g=ln_a_g, ln_a_b=ln_a_b, b_in=b_in, w_s=w_s, b_s=b_s, ln_v_g=ln_v_g, ln_v_b=ln_v_b, b_out=b_out,
                 ln_m_g=ln_m_g, ln_m_b=ln_m_b, ln_c_g=ln_c_g, ln_c_b=ln_c_b)
    small_m = dict(ln_a_g=m_ln_a_g, ln_a_b=m_ln_a_b, b_in=m_b_in, w_s=m_w_s, b_s=m_b_s, ln_v_g=m_ln_v_g, ln_v_b=m_ln_v_b, b_out=m_b_out,
                   ln_m_g=m_ln_m_g, ln_m_b=m_ln_m_b, ln_c_g=m_ln_c_g, ln_c_b=m_ln_c_b)
    small_v = dict(ln_a_g=v_ln_a_g, ln_a_b=v_ln_a_b, b_in=v_b_in, w_s=v_w_s, b_s=v_b_s, ln_v_g=v_ln_v_g, ln_v_b=v_ln_v_b, b_out=v_b_out,
                   ln_m_g=v_ln_m_g, ln_m_b=v_ln_m_b, ln_c_g=v_ln_c_g, ln_c_b=v_ln_c_b)
    snames = list(small)
    assert snames == list(small_g) and conv_w.shape[2] == LANES and da == N_DEV * LANES
    views = lambda tree: [_rows128(tree[k]) for k in snames]
    out = _adamw_small(packs, [a.shape[0] for a in views(small)], views(small), views(small_m), views(small_v),
                       (conv_w[0], m_conv_w[0], v_conv_w[0]), "adamw_small")
    for k, per_param in zip(snames + ["conv_w"], out):
        shape = conv_w.shape if k == "conv_w" else small[k].shape
        grads[k], deltas[k], new_m[k], new_v[k] = (o.reshape(shape) for o in per_param)

    order = ["ffa_gate", "ffa_up", "ffa_down", "ln_a_g", "ln_a_b", "w_in", "b_in", "w_s", "b_s", "ln_v_g", "ln_v_b", "conv_w", "w_out", "b_out",
             "ln_m_g", "ln_m_b", "ffc_gate", "ffc_up", "ffc_down", "ln_c_g", "ln_c_b"]
    return (loss, grad_x[None], *[grads[k] for k in order], *[deltas[k] for k in order], *[new_m[k] for k in order], *[new_v[k] for k in order])
```

```python
import math

import jax
import jax.numpy as jnp
from jax import lax
from jax.experimental import pallas as pl
from jax.experimental.pallas import tpu as pltpu

BF16 = jnp.bfloat16
F32 = jnp.float32
MESH = pl.DeviceIdType.MESH

N_DEV = 8
HEAD = 128
CHUNK = 128
CONV_TAPS = 3
LN_EPS = 1e-5
ALPHA = float(2 ** 0.25)
GELU_C = 0.7978845608028654
GELU_A = 0.044715
ADAM_LR, ADAM_B1, ADAM_B2, ADAM_EPS, ADAM_WD, ADAM_STEP = 0.001, 0.9, 0.999, 1e-08, 0.01, 10
V7X_VMEM_LIMIT = 56 * 1024 * 1024
LANES = 128
BF16_ROWS = 16
MXU_COLS = 256
TRANSPOSED = ("ffa_gate", "ffa_up", "ffc_gate", "ffc_up")
PROJ_IN_COLS = 1280
DW_TOKENS = 2048

NT_DIMS = (((1,), (1,)), ((), ()))
TN_DIMS = (((0,), (0,)), ((), ()))
ANY = pl.BlockSpec(memory_space=pl.ANY)


def _gelu_tanh(x):
    return jnp.tanh(GELU_C * (x + GELU_A * x * x * x))


def _gelu(x, t):
    return 0.5 * x * (1.0 + t)


def _gelu_grad(x, t):
    return 0.5 * (1.0 + t) + 0.5 * x * (1.0 - t * t) * GELU_C * (1.0 + 3.0 * GELU_A * x * x)


def _sigmoid(x):
    return 1.0 / (1.0 + jnp.exp(-x))


def _row(shape):
    return pl.BlockSpec(shape, lambda *_: (0,) * len(shape))


def _row_blocks(tm, rows=128):
    rows = min(rows, tm)
    return [slice(r, r + rows) for r in range(0, tm, rows)]


def _ln_backward(dh, xhat, rstd, gain):
    dxh = dh * gain
    m1 = jnp.mean(dxh, axis=-1, keepdims=True)
    m2 = jnp.mean(dxh * xhat, axis=-1, keepdims=True)
    return rstd * (dxh - m1 - xhat * m2)


def _place():
    x, y, c = lax.axis_index("x"), lax.axis_index("y"), lax.axis_index("c")
    return x, y, c, [(1 - x, y), (x, 1 - y), (1 - x, 1 - y)]


def _other_devices(x, y, c):
    flips = [(bx, by, bc) for bx in (0, 1) for by in (0, 1) for bc in (0, 1)][1:]
    return [(1 - x if bx else x, 1 - y if by else y, 1 - c if bc else c) for bx, by, bc in flips]


class _Gather:
    def __init__(self, shards, forward_at=0.75, by_columns=()):
        n = len(shards)
        self.n, self.forward_at, self.by_columns = n, forward_at, tuple(by_columns)
        self.inputs = list(shards)
        self.out_shapes = [jax.ShapeDtypeStruct((a.shape[0], N_DEV * a.shape[1]) if i in self.by_columns else (N_DEV,) + a.shape, a.dtype)
                           for i, a in enumerate(shards)]
        self.scratch = [pltpu.SemaphoreType.DMA((n, 7)), pltpu.SemaphoreType.DMA((n, 7)), pltpu.SemaphoreType.DMA((n,))]

    def _block(self, outs, a, dev):
        if a in self.by_columns:
            cols = outs[a].shape[1] // N_DEV
            return outs[a].at[:, pl.ds(dev * cols, cols)]
        return outs[a].at[dev]

    def _copy(self, outs, sems, a, k, block, to, src=None):
        dst = self._block(outs, a, block)
        return pltpu.make_async_remote_copy(src_ref=dst if src is None else src, dst_ref=dst, send_sem=sems[0].at[a, k],
                                            recv_sem=sems[1].at[a, k], device_id=to, device_id_type=MESH)

    def start(self, ins, outs, sems, urgent=None):
        x, y, c, chips = _place()
        me = 4 * x + 2 * y + c
        for a in range(self.n):
            pltpu.make_async_copy(ins[a], self._block(outs, a, me), sems[2].at[a]).start()
        urgent = list(range(self.n)) if urgent is None else list(urgent)
        for group in (urgent, [a for a in range(self.n) if a not in urgent]):
            for a in group:
                self._copy(outs, sems, a, 0, me, (x, y, 1 - c), src=ins[a]).start()
                for j in (0, 1):
                    self._copy(outs, sems, a, 1 + j, me, (*chips[j], c), src=ins[a]).start()
            for a in group:
                self._copy(outs, sems, a, 3, me, (*chips[2], c), src=ins[a]).start()

    def wait_sibling(self, outs, sems, a):
        x, y, c, _ = _place()
        self._copy(outs, sems, a, 0, 4 * x + 2 * y + 1 - c, (x, y, 1 - c)).wait_recv()

    def pass_on(self, outs, sems, a, j):
        x, y, c, chips = _place()
        block = 4 * chips[j][0] + 2 * chips[j][1] + c
        self._copy(outs, sems, a, 1 + j, block, (x, y, 1 - c)).wait_recv()
        self._copy(outs, sems, a, 4 + j, block, (x, y, 1 - c)).start()

    def wait_passed(self, outs, sems, a, j):
        x, y, c, chips = _place()
        self._copy(outs, sems, a, 4 + j, 4 * chips[j][0] + 2 * chips[j][1] + 1 - c, (x, y, 1 - c)).wait_recv()

    def wait_sent(self, ins, outs, sems, a):
        x, y, c, _ = _place()
        me = 4 * x + 2 * y + c
        for k in range(7):
            self._copy(outs, sems, a, k, me, (x, y, 1 - c), src=ins[a]).wait_send()
        pltpu.make_async_copy(ins[a], self._block(outs, a, me), sems[2].at[a]).wait()

    def forward(self, ins, outs, sems):
        for a in range(self.n):
            for j in range(3):
                self.pass_on(outs, sems, a, j)

    def finish(self, ins, outs, sems):
        for a in range(self.n):
            self.wait_sibling(outs, sems, a)
            for j in range(3):
                self.wait_passed(outs, sems, a, j)
        for a in range(self.n):
            self.wait_sent(ins, outs, sems, a)

    def before(self, step, n_steps, ins, outs, sems):
        pl.when(step == 0)(lambda: self.start(ins, outs, sems))
        pl.when(step == int(self.forward_at * (n_steps - 1)))(lambda: self.forward(ins, outs, sems))

    def after(self, step, n_steps, ins, outs, sems):
        pl.when(step == n_steps - 1)(lambda: self.finish(ins, outs, sems))


class _Scatter:
    def __init__(self, partial, whole=False, by_columns=False):
        self.whole, self.by_columns = whole, by_columns
        self.inputs = [partial]
        if whole:
            shape = (N_DEV,) + partial.shape
        elif by_columns:
            shape = (N_DEV, partial.shape[0], partial.shape[1] // N_DEV)
        else:
            shape = partial.shape
        self.out_shapes = [jax.ShapeDtypeStruct(shape, partial.dtype)]
        self.scratch = [pltpu.SemaphoreType.DMA((7,)), pltpu.SemaphoreType.DMA((7,)), pltpu.SemaphoreType.DMA((1,))]

    def _copies(self, ins, outs, sems):
        x, y, c, _ = _place()
        me = 4 * x + 2 * y + c
        if self.whole:
            block = lambda dev: ins[0]
        elif self.by_columns:
            cols = ins[0].shape[1] // N_DEV
            block = lambda dev: ins[0].at[:, pl.ds(dev * cols, cols)]
        else:
            block = lambda dev: ins[0].at[dev]
        mine = pltpu.make_async_copy(block(me), outs[0].at[me], sems[2].at[0])
        remote = [pltpu.make_async_remote_copy(src_ref=block(4 * px + 2 * py + pc), dst_ref=outs[0].at[me], send_sem=sems[0].at[k],
                                               recv_sem=sems[1].at[k], device_id=(px, py, pc), device_id_type=MESH)
                  for k, (px, py, pc) in enumerate(_other_devices(x, y, c))]
        return mine, remote

    def start(self, ins, outs, sems):
        mine, remote = self._copies(ins, outs, sems)
        mine.start()
        for cp in remote:
            cp.start()

    def finish(self, ins, outs, sems):
        mine, remote = self._copies(ins, outs, sems)
        for cp in remote:
            cp.wait()
        mine.wait()

    def before(self, step, n_steps, ins, outs, sems):
        pl.when(step == 0)(lambda: self.start(ins, outs, sems))

    def after(self, step, n_steps, ins, outs, sems):
        pl.when(step == n_steps - 1)(lambda: self.finish(ins, outs, sems))


class _ChipScatter(_Scatter):
    def __init__(self, sums):
        super().__init__(sums)
        self.scratch = [pltpu.SemaphoreType.DMA((3,)), pltpu.SemaphoreType.DMA((3,)), pltpu.SemaphoreType.DMA((1,))]

    def _copies(self, ins, outs, sems):
        x, y, c, chips = _place()
        my_chip = 2 * x + y
        mine = pltpu.make_async_copy(ins[0].at[my_chip], outs[0].at[my_chip], sems[2].at[0])
        remote = [pltpu.make_async_remote_copy(src_ref=ins[0].at[2 * px + py], dst_ref=outs[0].at[my_chip], send_sem=sems[0].at[k],
                                               recv_sem=sems[1].at[k], device_id=(px, py, c), device_id_type=MESH)
                  for k, (px, py) in enumerate(chips)]
        return mine, remote


def _call(body, name, grid, in_specs, out_specs, out_shape, scratch, ins, rider=None):
    single = not isinstance(out_shape, (list, tuple))
    out_shape = [out_shape] if single else list(out_shape)
    out_specs = [out_specs] if single else list(out_specs)
    params = pltpu.CompilerParams(dimension_semantics=("arbitrary",) * len(grid), vmem_limit_bytes=V7X_VMEM_LIMIT)
    if rider is None:
        outs = pl.pallas_call(body, name=name, grid=grid, in_specs=in_specs, out_specs=out_specs, out_shape=out_shape,
                              scratch_shapes=scratch, compiler_params=params)(*ins)
        return (outs[0] if single else outs), None
    n_in, n_out, n_scr = len(ins), len(out_shape), len(scratch)
    r_in, r_out = len(rider.inputs), len(rider.out_shapes)
    n_steps = math.prod(grid)

    def carried(*refs):
        refs = list(refs)
        cut = lambda n: [refs.pop(0) for _ in range(n)]
        b_in, c_in, b_out, c_out, b_scr = cut(n_in), cut(r_in), cut(n_out), cut(r_out), cut(n_scr)
        step = 0
        for axis, size in enumerate(grid):
            step = step * size + pl.program_id(axis)
        rider.before(step, n_steps, c_in, c_out, refs)
        body(*b_in, *b_out, *b_scr)
        rider.after(step, n_steps, c_in, c_out, refs)

    outs = pl.pallas_call(
        carried, name=name, grid=grid, in_specs=list(in_specs) + [ANY] * r_in, out_specs=out_specs + [ANY] * r_out,
        out_shape=out_shape + rider.out_shapes, scratch_shapes=list(scratch) + rider.scratch, compiler_params=params,
    )(*ins, *rider.inputs)
    base = outs[:n_out]
    return (base[0] if single else base), outs[n_out:]


def _arrival_block(j):
    x, y, c = lax.axis_index("x"), lax.axis_index("y"), lax.axis_index("c")
    chip, other_core = j // 2, j % 2
    px = jnp.where((chip == 1) | (chip == 3), 1 - x, x)
    py = jnp.where((chip == 2) | (chip == 3), 1 - y, y)
    pc = jnp.where(other_core == 1, 1 - c, c)
    return 4 * px + 2 * py + pc


def _ffn_gateup_gathering(xb, gate, up_shard, down_shard, name):
    s, d = xb.shape
    fs = up_shard.shape[0]
    tm = min(s, 1024)
    ni = s // tm
    ask_at = max(ni - 2, 0)
    gate_here = gate.ndim == 2
    gather = _Gather(([gate] if gate_here else []) + [up_shard, down_shard])
    n_g = gather.n
    used_here, down = tuple(range(n_g - 1)), n_g - 1

    def body(x_ref, *refs):
        refs = list(refs)
        gate_full = None if gate_here else refs.pop(0)
        shards = [refs.pop(0) for _ in range(n_g)]
        g_ref, u_ref, h_ref = refs.pop(0), refs.pop(0), refs.pop(0)
        fulls = [refs.pop(0) for _ in range(n_g)]
        w_ref, w_sems = refs.pop(0), refs.pop(0)
        sems = refs
        j, i = pl.program_id(0), pl.program_id(1)
        gate_src, up_src = (fulls[0], fulls[1]) if gate_here else (gate_full, fulls[0])

        def load(slot, srcs):
            return [pltpu.make_async_copy(src, w_ref.at[slot, a], w_sems.at[slot, a]) for a, src in enumerate(srcs)]

        @pl.when((j == 0) & (i == 0))
        def _():
            gather.start(shards, fulls, sems, urgent=used_here)
            mine = load(0, (shards[0] if gate_here else gate_full.at[_arrival_block(0)], shards[n_g - 2]))
            for cp in mine:
                cp.start()
            for cp in mine:
                cp.wait()

        for nxt in range(1, N_DEV):
            @pl.when((j == nxt) & (i == 0))
            def _(nxt=nxt):
                for cp in load(nxt % 2, (gate_src.at[0], up_src.at[0])):
                    cp.wait()

        for nxt in range(1, N_DEV):
            @pl.when((j == nxt - 1) & (i == ask_at))
            def _(nxt=nxt):
                for a in used_here:
                    if nxt == 1:
                        gather.wait_sibling(fulls, sems, a)
                    elif nxt % 2 == 0:
                        gather.pass_on(fulls, sems, a, nxt // 2 - 1)
                    else:
                        gather.wait_passed(fulls, sems, a, nxt // 2 - 1)
                block = _arrival_block(nxt)
                for cp in load(nxt % 2, (gate_src.at[block], up_src.at[block])):
                    cp.start()

        @pl.when((j == N_DEV - 1) & (i == ask_at))
        def _():
            for other_chip in range(3):
                gather.pass_on(fulls, sems, down, other_chip)

        x = x_ref[...]
        g = lax.dot_general(x, w_ref[j % 2, 0], NT_DIMS, preferred_element_type=F32)
        u = lax.dot_general(x, w_ref[j % 2, 1], NT_DIMS, preferred_element_type=F32)
        g_ref[...] = g.astype(BF16)
        u_ref[...] = u.astype(BF16)
        h_ref[...] = (g * _sigmoid(g) * u).astype(BF16)

        @pl.when((j == N_DEV - 1) & (i == ni - 1))
        def _():
            gather.wait_sibling(fulls, sems, down)
            for other_chip in range(3):
                gather.wait_passed(fulls, sems, down, other_chip)
            for a in range(n_g):
                gather.wait_sent(shards, fulls, sems, a)

    shp = jax.ShapeDtypeStruct((N_DEV, s, fs), BF16)
    o_spec = pl.BlockSpec((None, tm, fs), lambda j, i: (_arrival_block(j), i, 0))
    ins = ([] if gate_here else [gate]) + gather.inputs
    outs = pl.pallas_call(
        body, name=name, grid=(N_DEV, ni), in_specs=[pl.BlockSpec((tm, d), lambda j, i: (i, 0))] + [ANY] * len(ins),
        out_specs=[o_spec, o_spec, o_spec] + [ANY] * n_g, out_shape=[shp, shp, shp] + gather.out_shapes,
        scratch_shapes=[pltpu.VMEM((2, 2, fs, d), BF16), pltpu.SemaphoreType.DMA((2, 2))] + gather.scratch,
        compiler_params=pltpu.CompilerParams(dimension_semantics=("arbitrary", "arbitrary"), vmem_limit_bytes=V7X_VMEM_LIMIT),
    )(xb, *ins)
    return outs[:3], outs[3:]


def _down_ln(a3, w3, bias, res, res_affine, ln_g, ln_b, scale, name, target=None, rider=None):
    nk, s, tk = a3.shape
    d = w3.shape[2]
    tm = min(s, 256)
    final = target is not None

    def body(*refs):
        refs = list(refs)
        a_ref, w_hbm = refs[:2]
        del refs[:2]
        bias_ref = refs.pop(0) if bias is not None else None
        res_ref = refs.pop(0)
        rg_ref, rb_ref = (refs.pop(0), refs.pop(0)) if res_affine is not None else (None, None)
        g_ref, b_ref = refs.pop(0), refs.pop(0)
        t_ref = refs.pop(0) if final else None
        w_sem = refs.pop()
        w_ref = refs.pop()
        i = pl.program_id(0)
        if final:
            dr_ref, drb_ref, sq_ref, dg_ref, db_ref = refs
        else:
            xh_ref, hb_ref, rstd_ref = refs

        @pl.when(i == 0)
        def _():
            whole = pltpu.make_async_copy(w_hbm, w_ref, w_sem.at[0])
            whole.start()
            whole.wait()
            if final:
                sq_ref[...] = jnp.zeros_like(sq_ref)
                dg_ref[...] = jnp.zeros_like(dg_ref)
                db_ref[...] = jnp.zeros_like(db_ref)

        y = jnp.dot(a_ref[0], w_ref[0], preferred_element_type=F32)
        for k in range(1, nk):
            y = y + jnp.dot(a_ref[k], w_ref[k], preferred_element_type=F32)
        if bias_ref is not None:
            y = y + bias_ref[...]
        for rows in _row_blocks(tm):
            r = res_ref[rows, :]
            if rg_ref is not None:
                r = r * rg_ref[...] + rb_ref[...]
            r = ALPHA * r + scale * y[rows]
            mu = jnp.mean(r, axis=-1, keepdims=True)
            c = r - mu
            var = jnp.mean(c * c, axis=-1, keepdims=True)
            rstd = lax.rsqrt(var + LN_EPS)
            xhat = c * rstd
            h = xhat * g_ref[...] + b_ref[...]
            if not final:
                xh_ref[rows, :] = xhat
                hb_ref[rows, :] = h.astype(BF16)
                rstd_ref[rows, :] = rstd
            else:
                err = h - t_ref[rows, :]
                sq_ref[...] += jnp.sum(err * err, axis=0, keepdims=True)
                dh = err * (1.0 / d)
                dg_ref[...] += jnp.sum(dh * xhat, axis=0, keepdims=True)
                db_ref[...] += jnp.sum(dh, axis=0, keepdims=True)
                dr = _ln_backward(dh, xhat, rstd, g_ref[...])
                dr_ref[rows, :] = dr
                drb_ref[rows, :] = (scale * dr).astype(BF16)

    tok = pl.BlockSpec((tm, d), lambda i: (i, 0))
    vec = pl.BlockSpec((1, d), lambda i: (0, 0))
    ins = [a3, w3]
    in_specs = [pl.BlockSpec((nk, tm, tk), lambda i: (0, i, 0)), ANY]
    if bias is not None:
        ins.append(bias)
        in_specs.append(vec)
    ins.append(res)
    in_specs.append(tok)
    if res_affine is not None:
        ins += list(res_affine)
        in_specs += [vec, vec]
    ins += [ln_g, ln_b]
    in_specs += [vec, vec]
    if final:
        ins.append(target)
        in_specs.append(tok)
        out_shape = [jax.ShapeDtypeStruct((s, d), F32), jax.ShapeDtypeStruct((s, d), BF16)] + [jax.ShapeDtypeStruct((1, d), F32)] * 3
        out_specs = [tok, tok, vec, vec, vec]
    else:
        out_shape = [jax.ShapeDtypeStruct((s, d), F32), jax.ShapeDtypeStruct((s, d), BF16), jax.ShapeDtypeStruct((s, 1), F32)]
        out_specs = [tok, tok, pl.BlockSpec((tm, 1), lambda i: (i, 0))]
    scratch = [pltpu.VMEM((nk, tk, d), BF16), pltpu.SemaphoreType.DMA((1,))]
    return _call(body, name, (s // tm,), in_specs, out_specs, out_shape, scratch, ins, rider)


def _proj_in(hb, w, bias, name, rider=None):
    s, d = hb.shape
    n = w.shape[1]
    tm = min(s, 1024)
    tn = PROJ_IN_COLS if n % PROJ_IN_COLS == 0 else n

    def body(h_ref, w_ref, b_ref, z_ref):
        z_ref[...] = (jnp.dot(h_ref[...], w_ref[...], preferred_element_type=F32) + b_ref[...]).astype(BF16)

    in_specs = [pl.BlockSpec((tm, d), lambda i, j: (i, 0)), pl.BlockSpec((d, tn), lambda i, j: (0, j)),
                pl.BlockSpec((1, tn), lambda i, j: (0, j))]
    return _call(body, name, (s // tm, n // tn), in_specs, pl.BlockSpec((tm, tn), lambda i, j: (i, j)),
                 jax.ShapeDtypeStruct((s, n), BF16), [], [hb, w, bias], rider)


def _nt_hidden(ab, w3, name):
    s, kdim = ab.shape
    nj, tn, _ = w3.shape
    tm = min(s, 1024)

    def body(a_ref, w_ref, o_ref):
        o_ref[...] = lax.dot_general(a_ref[...], w_ref[...], NT_DIMS, preferred_element_type=F32).astype(BF16)

    in_specs = [pl.BlockSpec((tm, kdim), lambda i, j: (i, 0)), pl.BlockSpec((None, tn, kdim), lambda i, j: (j, 0, 0))]
    return _call(body, name, (s // tm, nj), in_specs, pl.BlockSpec((None, tm, tn), lambda i, j: (j, i, 0)),
                 jax.ShapeDtypeStruct((nj, s, tn), BF16), [], [ab, w3])[0]


def _ffn_bwd_hidden(ab, w3, gate3, up3, name, rider=None):
    s, kdim = ab.shape
    nj, tn, _ = w3.shape
    tm = min(s, 1024)

    def body(a_ref, w_ref, g_ref, u_ref, dg_ref, du_ref):
        a = a_ref[...]
        for c0 in range(0, tn, MXU_COLS):
            cols = slice(c0, min(c0 + MXU_COLS, tn))
            t = lax.dot_general(a, w_ref[cols, :], NT_DIMS, preferred_element_type=F32)
            g = g_ref[:, cols].astype(F32)
            sg = 0.5 * jnp.tanh(0.5 * g) + 0.5
            silu = g * sg
            du_ref[:, cols] = (t * silu).astype(BF16)
            dg_ref[:, cols] = (t * u_ref[:, cols].astype(F32) * (sg + silu * (1.0 - sg))).astype(BF16)

    hid = pl.BlockSpec((None, tm, tn), lambda i, j: (j, i, 0))
    shp = jax.ShapeDtypeStruct((nj, s, tn), BF16)
    in_specs = [pl.BlockSpec((tm, kdim), lambda i, j: (i, 0)), pl.BlockSpec((None, tn, kdim), lambda i, j: (j, 0, 0)), hid, hid]
    return _call(body, name, (s // tm, nj), in_specs, [hid, hid], [shp, shp], [], [ab, w3, gate3, up3], rider)


def _tn_dw(a, a_spec, b, b_spec, nj, m, n, s, tk, name, rider):
    def body(a_ref, b_ref, o_ref, acc_ref):
        k = pl.program_id(1)

        @pl.when(k == 0)
        def _():
            acc_ref[...] = jnp.zeros_like(acc_ref)

        acc_ref[...] += lax.dot_general(a_ref[...], b_ref[...], TN_DIMS, preferred_element_type=F32)

        @pl.when(k == s // tk - 1)
        def _():
            o_ref[...] = acc_ref[...].astype(BF16)

    return _call(body, name, (nj, s // tk), [a_spec, b_spec], pl.BlockSpec((None, m, n), lambda j, k: (j, 0, 0)),
                 jax.ShapeDtypeStruct((nj, m, n), BF16), [pltpu.VMEM((m, n), F32)], [a, b], rider)


def _dw_hidden_rows(hid3, db, name, rider=None):
    nj, s, fs = hid3.shape
    d = db.shape[1]
    tk = min(s, DW_TOKENS)
    return _tn_dw(hid3, pl.BlockSpec((None, tk, fs), lambda j, k: (j, k, 0)), db, pl.BlockSpec((tk, d), lambda j, k: (k, 0)),
                  nj, fs, d, s, tk, name, rider)


def _dw_hidden_rows_paired(hid3, db, name, rider=None):
    nj, s, fs = hid3.shape
    d = db.shape[1]
    tk = min(s, DW_TOKENS)
    nk = s // tk
    half = nj // 2

    def device_of(j):
        c = lax.axis_index("c")
        return 2 * (j % half) + jnp.where(j < half, 1 - c, c)

    def body(a_ref, b_ref, o_ref, theirs_ref, acc_ref, stage_ref, got_ref, send_sems, recv_sems, load_sem):
        j, k = pl.program_id(0), pl.program_id(1)
        x, y, c = lax.axis_index("x"), lax.axis_index("y"), lax.axis_index("c")

        def to_sibling(q):
            return pltpu.make_async_remote_copy(src_ref=stage_ref, dst_ref=theirs_ref.at[q], send_sem=send_sems.at[q],
                                                recv_sem=recv_sems.at[q], device_id=(x, y, 1 - c), device_id_type=MESH)

        @pl.when(k == 0)
        def _():
            acc_ref[...] = jnp.zeros_like(acc_ref)

        acc_ref[...] += lax.dot_general(a_ref[...], b_ref[...], TN_DIMS, preferred_element_type=F32)

        for q in range(half):
            @pl.when((j == q) & (k == nk - 1))
            def _(q=q):
                if q > 0:
                    to_sibling(q - 1).wait_send()
                stage_ref[...] = acc_ref[...].astype(BF16)
                to_sibling(q).start()

            @pl.when((j == half + q) & (k == nk - 1))
            def _(q=q):
                if q == 0:
                    to_sibling(half - 1).wait_send()
                to_sibling(q).wait_recv()
                load = pltpu.make_async_copy(theirs_ref.at[q], got_ref, load_sem.at[0])
                load.start()
                load.wait()
                o_ref[...] = (acc_ref[...] + got_ref[...].astype(F32)).astype(BF16)

    in_specs = [pl.BlockSpec((None, tk, fs), lambda j, k: (device_of(j), k, 0)), pl.BlockSpec((tk, d), lambda j, k: (k, 0))]
    out_specs = [pl.BlockSpec((None, fs, d), lambda j, k: (jnp.maximum(j - half, 0), 0, 0)), ANY]
    shp = jax.ShapeDtypeStruct((half, fs, d), BF16)
    scratch = [pltpu.VMEM((fs, d), F32), pltpu.VMEM((fs, d), BF16), pltpu.VMEM((fs, d), BF16),
               pltpu.SemaphoreType.DMA((half,)), pltpu.SemaphoreType.DMA((half,)), pltpu.SemaphoreType.DMA((1,))]
    (sums, _), riders_out = _call(body, name, (nj, nk), in_specs, out_specs, [shp, shp], scratch, [hid3, db], rider)
    return sums, riders_out


def _dw_cols(ab, dz, name, rider=None):
    s, d = ab.shape
    n = dz.shape[1]
    tn = PROJ_IN_COLS if n % PROJ_IN_COLS == 0 else n
    tk = min(s, DW_TOKENS)
    tr = d // 2

    def body(a_ref, b_ref, o_ref, acc_ref):
        k = pl.program_id(2)

        @pl.when(k == 0)
        def _():
            acc_ref[...] = jnp.zeros_like(acc_ref)

        acc_ref[...] += lax.dot_general(a_ref[...], b_ref[...], TN_DIMS, preferred_element_type=F32)

        @pl.when(k == s // tk - 1)
        def _():
            o_ref[...] = acc_ref[...].astype(BF16)

    in_specs = [pl.BlockSpec((tk, tr), lambda j, r, k: (k, r)), pl.BlockSpec((tk, tn), lambda j, r, k: (k, j))]
    return _call(body, name, (n // tn, d // tr, s // tk), in_specs, pl.BlockSpec((tr, tn), lambda j, r, k: (r, j)),
                 jax.ShapeDtypeStruct((d, n), BF16), [pltpu.VMEM((tr, tn), F32)], [ab, dz], rider)


def _ffn_bwd_input(dg3, wg3, du3, wu3, dres, ln, name, rider=None):
    s, d = dres.shape
    nk, _, fs = dg3.shape
    tm = min(s, 512)

    def body(*refs):
        refs = list(refs)
        dg_in, wg_ref, du_in, wu_ref, dres_ref = refs[:5]
        del refs[:5]
        if ln is not None:
            xh_ref, rstd_ref, gain_ref = refs.pop(0), refs.pop(0), refs.pop(0)
        acc_ref = refs.pop()
        i, k = pl.program_id(0), pl.program_id(1)

        @pl.when(k == 0)
        def _():
            acc_ref[...] = jnp.zeros_like(acc_ref)

        acc_ref[...] += (jnp.dot(dg_in[...], wg_ref[...], preferred_element_type=F32)
                         + jnp.dot(du_in[...], wu_ref[...], preferred_element_type=F32))

        @pl.when(k == nk - 1)
        def _():
            if ln is not None:
                dr_ref, drb_ref, dg_ref, db_ref, sum_ref = refs

                @pl.when(i == 0)
                def _():
                    dg_ref[...] = jnp.zeros_like(dg_ref)
                    db_ref[...] = jnp.zeros_like(db_ref)
                    sum_ref[...] = jnp.zeros_like(sum_ref)

            for rows in _row_blocks(tm):
                dh = ALPHA * dres_ref[rows, :] + acc_ref[rows, :]
                if ln is None:
                    refs[0][rows, :] = dh
                else:
                    xhat = xh_ref[rows, :]
                    dg_ref[...] += jnp.sum(dh * xhat, axis=0, keepdims=True)
                    db_ref[...] += jnp.sum(dh, axis=0, keepdims=True)
                    dr = _ln_backward(dh, xhat, rstd_ref[rows, :], gain_ref[...])
                    sum_ref[...] += jnp.sum(dr, axis=0, keepdims=True)
                    dr_ref[rows, :] = dr
                    drb_ref[rows, :] = dr.astype(BF16)

    tok = pl.BlockSpec((tm, d), lambda i, k: (i, 0))
    vec = pl.BlockSpec((1, d), lambda i, k: (0, 0))
    a_spec = pl.BlockSpec((None, tm, fs), lambda i, k: (k, i, 0))
    w_spec = pl.BlockSpec((None, fs, d), lambda i, k: (k, 0, 0))
    ins, in_specs = [dg3, wg3, du3, wu3, dres], [a_spec, w_spec, a_spec, w_spec, tok]
    if ln is None:
        out_shape, out_specs = jax.ShapeDtypeStruct((s, d), F32), tok
    else:
        ins += list(ln)
        in_specs += [tok, pl.BlockSpec((tm, 1), lambda i, k: (i, 0)), vec]
        out_shape = [jax.ShapeDtypeStruct((s, d), F32), jax.ShapeDtypeStruct((s, d), BF16)] + [jax.ShapeDtypeStruct((1, d), F32)] * 3
        out_specs = [tok, tok, vec, vec, vec]
    return _call(body, name, (s // tm, nk), in_specs, out_specs, out_shape, [pltpu.VMEM((tm, d), F32)], ins, rider)


def _proj_in_bwd_ln(dz, w, dres, ln, branch_scale, name):
    s, d = dres.shape
    n = w.shape[1]
    tm = min(s, 256)

    def body(dz_ref, w_hbm, dres_ref, xh_ref, rstd_ref, gain_ref, dr_ref, drb_ref, dg_ref, db_ref, w_ref, w_sem):
        @pl.when(pl.program_id(0) == 0)
        def _():
            whole = pltpu.make_async_copy(w_hbm, w_ref, w_sem.at[0])
            whole.start()
            whole.wait()
            dg_ref[...] = jnp.zeros_like(dg_ref)
            db_ref[...] = jnp.zeros_like(db_ref)

        acc = lax.dot_general(dz_ref[...], w_ref[...], NT_DIMS, preferred_element_type=F32)
        for rows in _row_blocks(tm):
            dh = ALPHA * dres_ref[rows, :] + acc[rows]
            xhat = xh_ref[rows, :]
            dg_ref[...] += jnp.sum(dh * xhat, axis=0, keepdims=True)
            db_ref[...] += jnp.sum(dh, axis=0, keepdims=True)
            dr = _ln_backward(dh, xhat, rstd_ref[rows, :], gain_ref[...])
            dr_ref[rows, :] = dr
            drb_ref[rows, :] = (branch_scale * dr).astype(BF16)

    tok = pl.BlockSpec((tm, d), lambda i: (i, 0))
    vec = pl.BlockSpec((1, d), lambda i: (0, 0))
    in_specs = [pl.BlockSpec((tm, n), lambda i: (i, 0)), ANY, tok, tok, pl.BlockSpec((tm, 1), lambda i: (i, 0)), vec]
    out_shape = [jax.ShapeDtypeStruct((s, d), F32), jax.ShapeDtypeStruct((s, d), BF16)] + [jax.ShapeDtypeStruct((1, d), F32)] * 2
    scratch = [pltpu.VMEM(w.shape, BF16), pltpu.SemaphoreType.DMA((1,))]
    return _call(body, name, (s // tm,), in_specs, [tok, tok, vec, vec], out_shape, scratch, [dz, w, dres] + list(ln))[0]


def _shift_rows_down(v, halo, k, row):
    out = pltpu.roll(v, k, 0)
    hr = halo.shape[0]
    for r in range(k):
        out = jnp.where(row == r, halo[hr - k + r:hr - k + r + 1, :], out)
    return out


def _shift_rows_up(v, halo, k, row):
    t = v.shape[0]
    out = pltpu.roll(v, t - k, 0)
    for r in range(k):
        out = jnp.where(row == t - k + r, halo[r:r + 1, :], out)
    return out


def _sgu_head_forward(z_ref, h, da, gv_ref, bv_ref):
    zu = z_ref[:, h * HEAD:(h + 1) * HEAD].astype(F32)
    zv = z_ref[:, da + h * HEAD:da + (h + 1) * HEAD].astype(F32)
    tu, tv = _gelu_tanh(zu), _gelu_tanh(zv)
    u = _gelu(zu, tu)
    v = _gelu(zv, tv)
    mu = jnp.mean(v, axis=-1, keepdims=True)
    c = v - mu
    rstd = lax.rsqrt(jnp.mean(c * c, axis=-1, keepdims=True) + LN_EPS)
    vhat = c * rstd
    vln = (vhat * gv_ref[h:h + 1, :] + bv_ref[h:h + 1, :]).astype(BF16)
    return (zu, tu), (zv, tv), u, vhat, rstd, vln


def _mixer_fwd(z, ws_masked, bs_wide, gv, bv, cw, name):
    s, zc = z.shape
    da = zc // 5
    nh = da // HEAD
    tm = min(s, 512)
    hb = tm // BF16_ROWS

    def body(z_ref, pc_ref, px_ref, ws_ref, bs_ref, gv_ref, bv_ref, cw_ref, y_ref):
        i = pl.program_id(0)
        for h in range(nh):
            _, _, u, _, _, vln = _sgu_head_forward(z_ref, h, da, gv_ref, bv_ref)
            for n in range(tm // CHUNK):
                rows = slice(n * CHUNK, (n + 1) * CHUNK)
                mixed = jnp.dot(ws_ref[h], vln[rows], preferred_element_type=F32) + bs_ref[h]
                y_ref[0, rows, h * HEAD:(h + 1) * HEAD] = (u[rows] * mixed).astype(BF16)
        gate_b = z_ref[:, 2 * da:3 * da].astype(F32)
        hc = z_ref[:, 3 * da:4 * da].astype(F32) * z_ref[:, 4 * da:5 * da].astype(F32)
        halo = jnp.where(i > 0, pc_ref[...].astype(F32) * px_ref[...].astype(F32), 0.0)
        row = lax.broadcasted_iota(jnp.int32, (tm, da), 0)
        y = cw_ref[0:1, :] * _shift_rows_down(hc, halo, 2, row) + cw_ref[1:2, :] * _shift_rows_down(hc, halo, 1, row) + cw_ref[2:3, :] * hc
        y_ref[1] = (gate_b * y).astype(BF16)

    prev = lambda col: pl.BlockSpec((BF16_ROWS, da), lambda i: (jnp.maximum(i * hb - 1, 0), col))
    in_specs = [pl.BlockSpec((tm, zc), lambda i: (i, 0)), prev(3), prev(4), _row((nh, CHUNK, CHUNK)), _row((nh, CHUNK, HEAD)),
                _row((nh, HEAD)), _row((nh, HEAD)), _row((CONV_TAPS, da))]
    return _call(body, name, (s // tm,), in_specs, pl.BlockSpec((2, tm, da), lambda i: (0, i, 0)),
                 jax.ShapeDtypeStruct((2, s, da), BF16), [], [z, z, z, ws_masked, bs_wide, gv, bv, cw])[0]


def _mixer_bwd(z, dy, ws_masked, bs_wide, gv, bv, cw, name, rider=None):
    s, zc = z.shape
    da = zc // 5
    nh = da // HEAD
    tm = min(s, 512)
    hb = tm // BF16_ROWS
    nblk = s // tm

    def body(z_ref, pc_ref, px_ref, nb_ref, dy_ref, ndy_ref, ws_ref, bs_ref, gv_ref, bv_ref, cw_ref,
             dz_ref, dws_ref, dbs_ref, dgv_ref, dbv_ref, dcw_ref, dbin_ref):
        i = pl.program_id(0)

        @pl.when(i == 0)
        def _():
            for ref in (dws_ref, dbs_ref, dgv_ref, dbv_ref, dcw_ref, dbin_ref):
                ref[...] = jnp.zeros_like(ref)

        causal = lax.broadcasted_iota(jnp.int32, (CHUNK, CHUNK), 0) >= lax.broadcasted_iota(jnp.int32, (CHUNK, CHUNK), 1)
        for h in range(nh):
            zu, zv, u, vhat, rstd, vln = _sgu_head_forward(z_ref, h, da, gv_ref, bv_ref)
            dya = dy_ref[0, :, h * HEAD:(h + 1) * HEAD].astype(F32)
            w = ws_ref[h]
            du_parts, dvln_parts = [], []
            for n in range(tm // CHUNK):
                rows = slice(n * CHUNK, (n + 1) * CHUNK)
                mixed = jnp.dot(w, vln[rows], preferred_element_type=F32) + bs_ref[h]
                du_parts.append(dya[rows] * mixed)
                dmix = dya[rows] * u[rows]
                dmix_b = dmix.astype(BF16)
                dws_ref[h] += jnp.where(causal, lax.dot_general(dmix_b, vln[rows], NT_DIMS, preferred_element_type=F32), 0.0)
                dbs_ref[h] += dmix
                dvln_parts.append(lax.dot_general(w, dmix_b, TN_DIMS, preferred_element_type=F32))
            du = jnp.concatenate(du_parts, axis=0)
            dvln = jnp.concatenate(dvln_parts, axis=0)
            dgv_ref[h:h + 1, :] += jnp.sum(dvln * vhat, axis=0, keepdims=True)
            dbv_ref[h:h + 1, :] += jnp.sum(dvln, axis=0, keepdims=True)
            dv = _ln_backward(dvln, vhat, rstd, gv_ref[h:h + 1, :])
            dzu = du * _gelu_grad(*zu)
            dzv = dv * _gelu_grad(*zv)
            ucols = slice(h * HEAD, (h + 1) * HEAD)
            vcols = slice(da + h * HEAD, da + (h + 1) * HEAD)
            dz_ref[:, ucols] = dzu.astype(BF16)
            dz_ref[:, vcols] = dzv.astype(BF16)
            dbin_ref[:, ucols] += jnp.sum(dzu, axis=0, keepdims=True)
            dbin_ref[:, vcols] += jnp.sum(dzv, axis=0, keepdims=True)

        gate_b = z_ref[:, 2 * da:3 * da].astype(F32)
        gate_c = z_ref[:, 3 * da:4 * da].astype(F32)
        xt = z_ref[:, 4 * da:5 * da].astype(F32)
        hc = gate_c * xt
        halo = jnp.where(i > 0, pc_ref[...].astype(F32) * px_ref[...].astype(F32), 0.0)
        row = lax.broadcasted_iota(jnp.int32, (tm, da), 0)
        sh1 = _shift_rows_down(hc, halo, 1, row)
        sh2 = _shift_rows_down(hc, halo, 2, row)
        y = cw_ref[0:1, :] * sh2 + cw_ref[1:2, :] * sh1 + cw_ref[2:3, :] * hc
        dyb = dy_ref[1].astype(F32)
        dconv = dyb * gate_b
        nhalo = jnp.where(i < nblk - 1, ndy_ref[...].astype(F32) * nb_ref[...].astype(F32), 0.0)
        dhc = cw_ref[2:3, :] * dconv + cw_ref[1:2, :] * _shift_rows_up(dconv, nhalo, 1, row) + cw_ref[0:1, :] * _shift_rows_up(dconv, nhalo, 2, row)
        dcw_ref[0:1, :] += jnp.sum(dconv * sh2, axis=0, keepdims=True)
        dcw_ref[1:2, :] += jnp.sum(dconv * sh1, axis=0, keepdims=True)
        dcw_ref[2:3, :] += jnp.sum(dconv * hc, axis=0, keepdims=True)
        for col, val in ((2, dyb * y), (3, dhc * xt), (4, dhc * gate_c)):
            cols = slice(col * da, (col + 1) * da)
            dz_ref[:, cols] = val.astype(BF16)
            dbin_ref[:, cols] += jnp.sum(val, axis=0, keepdims=True)

        @pl.when(i == nblk - 1)
        def _():
            for h in range(nh):
                dbs_ref[h] = jnp.broadcast_to(jnp.sum(dbs_ref[h], axis=1, keepdims=True), (CHUNK, HEAD))

    prev = lambda col: pl.BlockSpec((BF16_ROWS, da), lambda i: (jnp.maximum(i * hb - 1, 0), col))
    nxt = lambda i: jnp.minimum((i + 1) * hb, s // BF16_ROWS - 1)
    in_specs = [pl.BlockSpec((tm, zc), lambda i: (i, 0)), prev(3), prev(4), pl.BlockSpec((BF16_ROWS, da), lambda i: (nxt(i), 2)),
                pl.BlockSpec((2, tm, da), lambda i: (0, i, 0)), pl.BlockSpec((None, BF16_ROWS, da), lambda i: (1, nxt(i), 0)),
                _row((nh, CHUNK, CHUNK)), _row((nh, CHUNK, HEAD)), _row((nh, HEAD)), _row((nh, HEAD)), _row((CONV_TAPS, da))]
    out_specs = [pl.BlockSpec((tm, zc), lambda i: (i, 0)), _row((nh, CHUNK, CHUNK)), _row((nh, CHUNK, HEAD)), _row((nh, HEAD)),
                 _row((nh, HEAD)), _row((8, da)), _row((1, zc))]
    out_shape = [jax.ShapeDtypeStruct((s, zc), BF16), jax.ShapeDtypeStruct((nh, CHUNK, CHUNK), F32),
                 jax.ShapeDtypeStruct((nh, CHUNK, HEAD), F32), jax.ShapeDtypeStruct((nh, HEAD), F32),
                 jax.ShapeDtypeStruct((nh, HEAD), F32), jax.ShapeDtypeStruct((8, da), F32), jax.ShapeDtypeStruct((1, zc), F32)]
    return _call(body, name, (nblk,), in_specs, out_specs, out_shape, [], [z, z, z, z, dy, dy, ws_masked, bs_wide, gv, bv, cw], rider)


def _adam_update(g, w, m, v):
    m_new = ADAM_B1 * m + (1.0 - ADAM_B1) * g
    v_new = ADAM_B2 * v + (1.0 - ADAM_B2) * (g * g)
    m_hat = m_new / (1.0 - ADAM_B1 ** ADAM_STEP)
    v_hat = v_new / (1.0 - ADAM_B2 ** ADAM_STEP)
    return -ADAM_LR * (m_hat / (jnp.sqrt(v_hat) + ADAM_EPS) + ADAM_WD * w), m_new, v_new


def _adamw(gparts, w, m, v, name):
    n, r, c = gparts.shape
    tr = r // 4 if (r // 4) % BF16_ROWS == 0 else r

    def body(g_ref, w_ref, m_ref, v_ref, go_ref, d_ref, mo_ref, vo_ref):
        g = g_ref[0].astype(F32)
        for q in range(1, n):
            g = g + g_ref[q].astype(F32)
        go_ref[...] = g
        d_ref[...], mo_ref[...], vo_ref[...] = _adam_update(g, w_ref[...], m_ref[...], v_ref[...])

    blk = pl.BlockSpec((tr, c), lambda i: (i, 0))
    shp = jax.ShapeDtypeStruct((r, c), F32)
    return _call(body, name, (r // tr,), [pl.BlockSpec((n, tr, c), lambda i: (0, i, 0)), blk, blk, blk], [blk] * 4, [shp] * 4, [],
                 [gparts, w, m, v])[0]


def _adamw_small(packs, rows, w, m, v, conv, name):
    n_par, n_dev = len(rows), packs.shape[0]
    taps = conv[0].shape[0]

    def body(*refs):
        refs = list(refs)
        cut = lambda n: [refs.pop(0) for _ in range(n)]
        p_ref, w_refs, m_refs, v_refs, (cw_ref, cm_ref, cv_ref) = refs.pop(0), cut(n_par), cut(n_par), cut(n_par), cut(3)
        outs = [cut(4) for _ in range(n_par + 1)]
        at = 0
        for k in range(n_par):
            g = p_ref[0, at:at + rows[k], :]
            for dev in range(1, n_dev):
                g = g + p_ref[dev, at:at + rows[k], :]
            go_ref, d_ref, mo_ref, vo_ref = outs[k]
            go_ref[...] = g
            d_ref[...], mo_ref[...], vo_ref[...] = _adam_update(g, w_refs[k][...], m_refs[k][...], v_refs[k][...])
            at += rows[k]
        me = 4 * lax.axis_index("x") + 2 * lax.axis_index("y") + lax.axis_index("c")
        go_ref, d_ref, mo_ref, vo_ref = outs[n_par]
        for tap in range(taps):
            row = pl.ds(at + tap * n_dev + me, 1)
            g = p_ref[0, row, :]
            for dev in range(1, n_dev):
                g = g + p_ref[dev, row, :]
            one = slice(tap, tap + 1)
            go_ref[one, :] = g
            d_ref[one, :], mo_ref[one, :], vo_ref[one, :] = _adam_update(g, cw_ref[one, :], cm_ref[one, :], cv_ref[one, :])

    vmem = pl.BlockSpec(memory_space=pltpu.VMEM)
    ins = [packs] + list(w) + list(m) + list(v) + list(conv)
    out_shape = [jax.ShapeDtypeStruct(a.shape, F32) for a in list(w) + [conv[0]] for _ in range(4)]
    outs = pl.pallas_call(body, name=name, in_specs=[vmem] * len(ins), out_specs=[vmem] * len(out_shape), out_shape=out_shape,
                          compiler_params=pltpu.CompilerParams(vmem_limit_bytes=V7X_VMEM_LIMIT))(*ins)
    return [outs[4 * k:4 * k + 4] for k in range(n_par + 1)]


def _rows128(a):
    return a.reshape(-1, LANES)


def kernel(x, ffa_gate, ffa_up, ffa_down, ln_a_g, ln_a_b, w_in, b_in, w_s, b_s, ln_v_g, ln_v_b, conv_w, w_out, b_out, ln_m_g, ln_m_b, ffc_gate, ffc_up, ffc_down, ln_c_g, ln_c_b, loss_target, m_ffa_gate, m_ffa_up, m_ffa_down, m_ln_a_g, m_ln_a_b, m_w_in, m_b_in, m_w_s, m_b_s, m_ln_v_g, m_ln_v_b, m_conv_w, m_w_out, m_b_out, m_ln_m_g, m_ln_m_b, m_ffc_gate, m_ffc_up, m_ffc_down, m_ln_c_g, m_ln_c_b, v_ffa_gate, v_ffa_up, v_ffa_down, v_ln_a_g, v_ln_a_b, v_w_in, v_b_in, v_w_s, v_b_s, v_ln_v_g, v_ln_v_b, v_conv_w, v_w_out, v_b_out, v_ln_m_g, v_ln_m_b, v_ffc_gate, v_ffc_up, v_ffc_down, v_ln_c_g, v_ln_c_b):
    x2, target = x[0], loss_target[0]
    s, d = x2.shape
    da = d // 2
    nh = da // HEAD

    big = dict(ffa_gate=ffa_gate, ffa_up=ffa_up, ffa_down=ffa_down, w_in=w_in, w_out=w_out, ffc_gate=ffc_gate, ffc_up=ffc_up, ffc_down=ffc_down)
    big_m = dict(ffa_gate=m_ffa_gate, ffa_up=m_ffa_up, ffa_down=m_ffa_down, w_in=m_w_in, w_out=m_w_out, ffc_gate=m_ffc_gate, ffc_up=m_ffc_up, ffc_down=m_ffc_down)
    big_v = dict(ffa_gate=v_ffa_gate, ffa_up=v_ffa_up, ffa_down=v_ffa_down, w_in=v_w_in, w_out=v_w_out, ffc_gate=v_ffc_gate, ffc_up=v_ffc_up, ffc_down=v_ffc_down)
    local = lambda k, a: jnp.transpose(a[0]) if k in TRANSPOSED else a[0]
    shard = {k: local(k, w).astype(BF16) for k, w in big.items()}
    conv_rows = jnp.pad(conv_w[0], ((0, 8 - CONV_TAPS), (0, 0)))

    tril = jnp.tril(jnp.ones((CHUNK, CHUNK), dtype=bool))
    ws_masked = jnp.where(tril[None], w_s[0], 0.0).astype(BF16)
    bs_wide = jnp.broadcast_to(b_s[0][:, :, None], (nh, CHUNK, HEAD))
    gv, bv = ln_v_g.reshape(nh, HEAD), ln_v_b.reshape(nh, HEAD)

    full = {}
    xb = x2.astype(BF16)
    (g_a, u_a, hid_a), (full["ffa_gate"], full["ffa_up"], full["ffa_down"]) = _ffn_gateup_gathering(
        xb, shard["ffa_gate"], shard["ffa_up"], shard["ffa_down"], "ffa_gateup")
    (xhat1, h1b, rstd1), (full["w_in"], full["w_out"], conv_full) = _down_ln(
        hid_a, full["ffa_down"], None, x2, None, ln_a_g, ln_a_b, 0.5, "ffa_down_ln", rider=_Gather([shard["w_in"], shard["w_out"], conv_rows], by_columns=(0,)))
    cw = jnp.transpose(conv_full[:, :CONV_TAPS, :], (1, 0, 2)).reshape(CONV_TAPS, da)
    w_out2 = full["w_out"].reshape(2, da, d)
    z, (full["ffc_gate"],) = _proj_in(h1b, full["w_in"], b_in, "proj_in", _Gather([shard["ffc_gate"]]))
    ycat = _mixer_fwd(z, ws_masked, bs_wide, gv, bv, cw, "mixer_fwd")
    (xhat2, h2b, rstd2), _ = _down_ln(ycat, w_out2, b_out, xhat1, (ln_a_g, ln_a_b), ln_m_g, ln_m_b, 1.0, "proj_out_ln")
    (g_c, u_c, hid_c), (full["ffc_up"], full["ffc_down"]) = _ffn_gateup_gathering(
        h2b, full["ffc_gate"], shard["ffc_up"], shard["ffc_down"], "ffc_gateup")
    (dr3, dr3b, sq_err, d_ln_c_g, d_ln_c_b), _ = _down_ln(hid_c, full["ffc_down"], None, xhat2, (ln_m_g, ln_m_b), ln_c_g, ln_c_b, 0.5,
                                                          "ffc_down_ln_loss", target=target)
    loss = lax.psum((0.5 / d) * jnp.sum(sq_err), ("x", "y", "c"))

    landed = {}
    (dg_c, du_c), _ = _ffn_bwd_hidden(dr3b, full["ffc_down"], g_c, u_c, "ffc_bwd_hidden")
    part, _ = _dw_hidden_rows_paired(hid_c, dr3b, "ffc_dw_down")
    part, (landed["ffc_down"],) = _dw_hidden_rows_paired(dg_c, h2b, "ffc_dw_gate", _ChipScatter(part))
    part, (landed["ffc_gate"],) = _dw_hidden_rows_paired(du_c, h2b, "ffc_dw_up", _ChipScatter(part))
    (dr2, dr2b, d_ln_m_g, d_ln_m_b, d_b_out), (landed["ffc_up"],) = _ffn_bwd_input(
        dg_c, full["ffc_gate"], du_c, full["ffc_up"], dr3, (xhat2, rstd2, ln_m_g), "ffc_bwd_input_ln", _ChipScatter(part))
    dycat = _nt_hidden(dr2b, w_out2, "proj_out_bwd")
    part, _ = _dw_hidden_rows(ycat, dr2b, "proj_out_dw")
    (dz, d_w_s, d_b_s_wide, d_gv, d_bv, d_cw, d_b_in), (landed["w_out"],) = _mixer_bwd(
        z, dycat, ws_masked, bs_wide, gv, bv, cw, "mixer_bwd", _Scatter(part.reshape(N_DEV, d // N_DEV, d)))
    dr1, dr1b, d_ln_a_g, d_ln_a_b = _proj_in_bwd_ln(dz, full["w_in"], dr2, (xhat1, rstd1, ln_a_g), 0.5, "proj_in_bwd_ln")
    small_g = dict(ln_a_g=d_ln_a_g, ln_a_b=d_ln_a_b, b_in=d_b_in, w_s=d_w_s, b_s=d_b_s_wide[:, :, 0], ln_v_g=d_gv, ln_v_b=d_bv, b_out=d_b_out,
                   ln_m_g=d_ln_m_g, ln_m_b=d_ln_m_b, ln_c_g=d_ln_c_g, ln_c_b=d_ln_c_b)
    pack = jnp.concatenate([_rows128(g) for g in small_g.values()] + [_rows128(d_cw[:CONV_TAPS])], axis=0)
    part, (packs,) = _dw_cols(h1b, dz, "proj_in_dw", _Scatter(pack, whole=True))
    (dg_a, du_a), (landed["w_in"],) = _ffn_bwd_hidden(dr1b, full["ffa_down"], g_a, u_a, "ffa_bwd_hidden", _Scatter(part, by_columns=True))
    part, _ = _dw_hidden_rows_paired(hid_a, dr1b, "ffa_dw_down")
    part, (landed["ffa_down"],) = _dw_hidden_rows_paired(dg_a, xb, "ffa_dw_gate", _ChipScatter(part))
    part, (landed["ffa_gate"],) = _dw_hidden_rows_paired(du_a, xb, "ffa_dw_up", _ChipScatter(part))
    grad_x, (landed["ffa_up"],) = _ffn_bwd_input(dg_a, full["ffa_gate"], du_a, full["ffa_up"], dr1, None, "ffa_bwd_input", _ChipScatter(part))

    grads, deltas, new_m, new_v = {}, {}, {}, {}
    for k in big:
        out = _adamw(landed[k], local(k, big[k]), local(k, big_m[k]), local(k, big_v[k]), "adamw_" + k)
        grads[k], deltas[k], new_m[k], new_v[k] = ((jnp.transpose(o) if k in TRANSPOSED else o).reshape(big[k].shape) for o in out)

    small = dict(ln_a_g=ln_a_g, ln_a_b=ln_a_b, b_in=b_in, w_s=w_s, b_s=b_s, ln_v_g=ln_v_g, ln_v_b=ln_v_b, b_out=b_out,
                 ln_m_g=ln_m_g, ln_m_b=ln_m_b, ln_c_g=ln_c_g, ln_c_b=ln_c_b)
    small_m = dict(ln_a_g=m_ln_a_g, ln_a_b=m_ln_a_b, b_in=m_b_in, w_s=m_w_s, b_s=m_b_s, ln_v_g=m_ln_v_g, ln_v_b=m_ln_v_b, b_out=m_b_out,
                   ln_m_g=m_ln_m_g, ln_m_b=m_ln_m_b, ln_c_g=m_ln_c_g, ln_c_b=m_ln_c_b)
    small_v = dict(ln_a_g=v_ln_a_g, ln_a_b=v_ln_a_b, b_in=v_b_in, w_s=v_w_s, b_s=v_b_s, ln_v_g=v_ln_v_g, ln_v_b=v_ln_v_b, b_out=v_b_out,
                   ln_m_g=v_ln_m_g, ln_m_b=v_ln_m_b, ln_c_g=v_ln_c_g, ln_c_b=v_ln_c_b)
    snames = list(small)
    assert snames == list(small_g) and conv_w.shape[2] == LANES and da == N_DEV * LANES
    views = lambda tree: [_rows128(tree[k]) for k in snames]
    out = _adamw_small(packs, [a.shape[0] for a in views(small)], views(small), views(small_m), views(small_v),
                       (conv_w[0], m_conv_w[0], v_conv_w[0]), "adamw_small")
    for k, per_param in zip(snames + ["conv_w"], out):
        shape = conv_w.shape if k == "conv_w" else small[k].shape
        grads[k], deltas[k], new_m[k], new_v[k] = (o.reshape(shape) for o in per_param)

    order = ["ffa_gate", "ffa_up", "ffa_down", "ln_a_g", "ln_a_b", "w_in", "b_in", "w_s", "b_s", "ln_v_g", "ln_v_b", "conv_w", "w_out", "b_out",
             "ln_m_g", "ln_m_b", "ffc_gate", "ffc_up", "ffc_down", "ln_c_g", "ln_c_b"]
    return (loss, grad_x[None], *[grads[k] for k in order], *[deltas[k] for k in order], *[new_m[k] for k in order], *[new_v[k] for k in order])
```

```python
import math

import jax
import jax.numpy as jnp
from jax import lax
from jax.experimental import pallas as pl
from jax.experimental.pallas import tpu as pltpu

BF16 = jnp.bfloat16
F32 = jnp.float32
MESH = pl.DeviceIdType.MESH

N_DEV = 8
HEAD = 128
CHUNK = 128
CONV_TAPS = 3
LN_EPS = 1e-5
ALPHA = float(2 ** 0.25)
GELU_C = 0.7978845608028654
GELU_A = 0.044715
ADAM_LR, ADAM_B1, ADAM_B2, ADAM_EPS, ADAM_WD, ADAM_STEP = 0.001, 0.9, 0.999, 1e-08, 0.01, 10
V7X_VMEM_LIMIT = 56 * 1024 * 1024
LANES = 128
BF16_ROWS = 16
MXU_COLS = 256
TRANSPOSED = ("ffa_gate", "ffa_up", "ffc_gate", "ffc_up")
PROJ_IN_COLS = 1280
DW_TOKENS = 2048

NT_DIMS = (((1,), (1,)), ((), ()))
TN_DIMS = (((0,), (0,)), ((), ()))
ANY = pl.BlockSpec(memory_space=pl.ANY)


def _gelu_tanh(x):
    return jnp.tanh(GELU_C * (x + GELU_A * x * x * x))


def _gelu(x, t):
    return 0.5 * x * (1.0 + t)


def _gelu_grad(x, t):
    return 0.5 * (1.0 + t) + 0.5 * x * (1.0 - t * t) * GELU_C * (1.0 + 3.0 * GELU_A * x * x)


def _sigmoid(x):
    return 0.5 * jnp.tanh(0.5 * x) + 0.5


def _row(shape):
    return pl.BlockSpec(shape, lambda *_: (0,) * len(shape))


def _row_blocks(tm, rows=128):
    rows = min(rows, tm)
    return [slice(r, r + rows) for r in range(0, tm, rows)]


def _ln_backward(dh, xhat, rstd, gain):
    dxh = dh * gain
    m1 = jnp.mean(dxh, axis=-1, keepdims=True)
    m2 = jnp.mean(dxh * xhat, axis=-1, keepdims=True)
    return rstd * (dxh - m1 - xhat * m2)


def _place():
    x, y, c = lax.axis_index("x"), lax.axis_index("y"), lax.axis_index("c")
    return x, y, c, [(1 - x, y), (x, 1 - y), (1 - x, 1 - y)]


def _other_devices(x, y, c):
    flips = [(bx, by, bc) for bx in (0, 1) for by in (0, 1) for bc in (0, 1)][1:]
    return [(1 - x if bx else x, 1 - y if by else y, 1 - c if bc else c) for bx, by, bc in flips]


class _Gather:
    def __init__(self, shards, forward_at=0.75, by_columns=()):
        n = len(shards)
        self.n, self.forward_at, self.by_columns = n, forward_at, tuple(by_columns)
        self.inputs = list(shards)
        self.out_shapes = [jax.ShapeDtypeStruct((a.shape[0], N_DEV * a.shape[1]) if i in self.by_columns else (N_DEV,) + a.shape, a.dtype)
                           for i, a in enumerate(shards)]
        self.scratch = [pltpu.SemaphoreType.DMA((n, 7)), pltpu.SemaphoreType.DMA((n, 7)), pltpu.SemaphoreType.DMA((n,))]

    def _block(self, outs, a, dev):
        if a in self.by_columns:
            cols = outs[a].shape[1] // N_DEV
            return outs[a].at[:, pl.ds(dev * cols, cols)]
        return outs[a].at[dev]

    def _copy(self, outs, sems, a, k, block, to, src=None):
        dst = self._block(outs, a, block)
        return pltpu.make_async_remote_copy(src_ref=dst if src is None else src, dst_ref=dst, send_sem=sems[0].at[a, k],
                                            recv_sem=sems[1].at[a, k], device_id=to, device_id_type=MESH)

    def start(self, ins, outs, sems, urgent=None):
        x, y, c, chips = _place()
        me = 4 * x + 2 * y + c
        for a in range(self.n):
            pltpu.make_async_copy(ins[a], self._block(outs, a, me), sems[2].at[a]).start()
        urgent = list(range(self.n)) if urgent is None else list(urgent)
        for group in (urgent, [a for a in range(self.n) if a not in urgent]):
            for a in group:
                self._copy(outs, sems, a, 0, me, (x, y, 1 - c), src=ins[a]).start()
                for j in (0, 1):
                    self._copy(outs, sems, a, 1 + j, me, (*chips[j], c), src=ins[a]).start()
            for a in group:
                self._copy(outs, sems, a, 3, me, (*chips[2], c), src=ins[a]).start()

    def wait_sibling(self, outs, sems, a):
        x, y, c, _ = _place()
        self._copy(outs, sems, a, 0, 4 * x + 2 * y + 1 - c, (x, y, 1 - c)).wait_recv()

    def pass_on(self, outs, sems, a, j):
        x, y, c, chips = _place()
        block = 4 * chips[j][0] + 2 * chips[j][1] + c
        self._copy(outs, sems, a, 1 + j, block, (x, y, 1 - c)).wait_recv()
        self._copy(outs, sems, a, 4 + j, block, (x, y, 1 - c)).start()

    def wait_passed(self, outs, sems, a, j):
        x, y, c, chips = _place()
        self._copy(outs, sems, a, 4 + j, 4 * chips[j][0] + 2 * chips[j][1] + 1 - c, (x, y, 1 - c)).wait_recv()

    def wait_sent(self, ins, outs, sems, a):
        x, y, c, _ = _place()
        me = 4 * x + 2 * y + c
        for k in range(7):
            self._copy(outs, sems, a, k, me, (x, y, 1 - c), src=ins[a]).wait_send()
        pltpu.make_async_copy(ins[a], self._block(outs, a, me), sems[2].at[a]).wait()

    def forward(self, ins, outs, sems):
        for a in range(self.n):
            for j in range(3):
                self.pass_on(outs, sems, a, j)

    def finish(self, ins, outs, sems):
        for a in range(self.n):
            self.wait_sibling(outs, sems, a)
            for j in range(3):
                self.wait_passed(outs, sems, a, j)
        for a in range(self.n):
            self.wait_sent(ins, outs, sems, a)

    def before(self, step, n_steps, ins, outs, sems):
        pl.when(step == 0)(lambda: self.start(ins, outs, sems))
        pl.when(step == int(self.forward_at * (n_steps - 1)))(lambda: self.forward(ins, outs, sems))

    def after(self, step, n_steps, ins, outs, sems):
        pl.when(step == n_steps - 1)(lambda: self.finish(ins, outs, sems))


class _Scatter:
    def __init__(self, partial, whole=False, by_columns=False):
        self.whole, self.by_columns = whole, by_columns
        self.inputs = [partial]
        if whole:
            shape = (N_DEV,) + partial.shape
        elif by_columns:
            shape = (N_DEV, partial.shape[0], partial.shape[1] // N_DEV)
        else:
            shape = partial.shape
        self.out_shapes = [jax.ShapeDtypeStruct(shape, partial.dtype)]
        self.scratch = [pltpu.SemaphoreType.DMA((7,)), pltpu.SemaphoreType.DMA((7,)), pltpu.SemaphoreType.DMA((1,))]

    def _copies(self, ins, outs, sems):
        x, y, c, _ = _place()
        me = 4 * x + 2 * y + c
        if self.whole:
            block = lambda dev: ins[0]
        elif self.by_columns:
            cols = ins[0].shape[1] // N_DEV
            block = lambda dev: ins[0].at[:, pl.ds(dev * cols, cols)]
        else:
            block = lambda dev: ins[0].at[dev]
        mine = pltpu.make_async_copy(block(me), outs[0].at[me], sems[2].at[0])
        remote = [pltpu.make_async_remote_copy(src_ref=block(4 * px + 2 * py + pc), dst_ref=outs[0].at[me], send_sem=sems[0].at[k],
                                               recv_sem=sems[1].at[k], device_id=(px, py, pc), device_id_type=MESH)
                  for k, (px, py, pc) in enumerate(_other_devices(x, y, c))]
        return mine, remote

    def start(self, ins, outs, sems):
        mine, remote = self._copies(ins, outs, sems)
        mine.start()
        for cp in remote:
            cp.start()

    def finish(self, ins, outs, sems):
        mine, remote = self._copies(ins, outs, sems)
        for cp in remote:
            cp.wait()
        mine.wait()

    def before(self, step, n_steps, ins, outs, sems):
        pl.when(step == 0)(lambda: self.start(ins, outs, sems))

    def after(self, step, n_steps, ins, outs, sems):
        pl.when(step == n_steps - 1)(lambda: self.finish(ins, outs, sems))


class _ChipScatter(_Scatter):
    def __init__(self, sums):
        super().__init__(sums)
        self.scratch = [pltpu.SemaphoreType.DMA((3,)), pltpu.SemaphoreType.DMA((3,)), pltpu.SemaphoreType.DMA((1,))]

    def _copies(self, ins, outs, sems):
        x, y, c, chips = _place()
        my_chip = 2 * x + y
        mine = pltpu.make_async_copy(ins[0].at[my_chip], outs[0].at[my_chip], sems[2].at[0])
        remote = [pltpu.make_async_remote_copy(src_ref=ins[0].at[2 * px + py], dst_ref=outs[0].at[my_chip], send_sem=sems[0].at[k],
                                               recv_sem=sems[1].at[k], device_id=(px, py, c), device_id_type=MESH)
                  for k, (px, py) in enumerate(chips)]
        return mine, remote


def _call(body, name, grid, in_specs, out_specs, out_shape, scratch, ins, rider=None):
    single = not isinstance(out_shape, (list, tuple))
    out_shape = [out_shape] if single else list(out_shape)
    out_specs = [out_specs] if single else list(out_specs)
    params = pltpu.CompilerParams(dimension_semantics=("arbitrary",) * len(grid), vmem_limit_bytes=V7X_VMEM_LIMIT)
    if rider is None:
        outs = pl.pallas_call(body, name=name, grid=grid, in_specs=in_specs, out_specs=out_specs, out_shape=out_shape,
                              scratch_shapes=scratch, compiler_params=params)(*ins)
        return (outs[0] if single else outs), None
    n_in, n_out, n_scr = len(ins), len(out_shape), len(scratch)
    r_in, r_out = len(rider.inputs), len(rider.out_shapes)
    n_steps = math.prod(grid)

    def carried(*refs):
        refs = list(refs)
        cut = lambda n: [refs.pop(0) for _ in range(n)]
        b_in, c_in, b_out, c_out, b_scr = cut(n_in), cut(r_in), cut(n_out), cut(r_out), cut(n_scr)
        step = 0
        for axis, size in enumerate(grid):
            step = step * size + pl.program_id(axis)
        rider.before(step, n_steps, c_in, c_out, refs)
        body(*b_in, *b_out, *b_scr)
        rider.after(step, n_steps, c_in, c_out, refs)

    outs = pl.pallas_call(
        carried, name=name, grid=grid, in_specs=list(in_specs) + [ANY] * r_in, out_specs=out_specs + [ANY] * r_out,
        out_shape=out_shape + rider.out_shapes, scratch_shapes=list(scratch) + rider.scratch, compiler_params=params,
    )(*ins, *rider.inputs)
    base = outs[:n_out]
    return (base[0] if single else base), outs[n_out:]


def _arrival_block(j):
    x, y, c = lax.axis_index("x"), lax.axis_index("y"), lax.axis_index("c")
    chip, other_core = j // 2, j % 2
    px = jnp.where((chip == 1) | (chip == 3), 1 - x, x)
    py = jnp.where((chip == 2) | (chip == 3), 1 - y, y)
    pc = jnp.where(other_core == 1, 1 - c, c)
    return 4 * px + 2 * py + pc


def _ffn_gateup_gathering(xb, gate, up_shard, down_shard, name):
    s, d = xb.shape
    fs = up_shard.shape[0]
    tm = min(s, 1024)
    ni = s // tm
    ask_at = max(ni - 2, 0)
    gate_here = gate.ndim == 2
    gather = _Gather(([gate] if gate_here else []) + [up_shard, down_shard])
    n_g = gather.n
    used_here, down = tuple(range(n_g - 1)), n_g - 1

    def body(x_ref, *refs):
        refs = list(refs)
        gate_full = None if gate_here else refs.pop(0)
        shards = [refs.pop(0) for _ in range(n_g)]
        silu_ref, udsilu_ref, h_ref = refs.pop(0), refs.pop(0), refs.pop(0)
        fulls = [refs.pop(0) for _ in range(n_g)]
        w_ref, w_sems = refs.pop(0), refs.pop(0)
        sems = refs
        j, i = pl.program_id(0), pl.program_id(1)
        gate_src, up_src = (fulls[0], fulls[1]) if gate_here else (gate_full, fulls[0])

        def load(slot, srcs):
            return [pltpu.make_async_copy(src, w_ref.at[slot, a], w_sems.at[slot, a]) for a, src in enumerate(srcs)]

        @pl.when((j == 0) & (i == 0))
        def _():
            gather.start(shards, fulls, sems, urgent=used_here)
            mine = load(0, (shards[0] if gate_here else gate_full.at[_arrival_block(0)], shards[n_g - 2]))
            for cp in mine:
                cp.start()
            for cp in mine:
                cp.wait()

        for nxt in range(1, N_DEV):
            @pl.when((j == nxt) & (i == 0))
            def _(nxt=nxt):
                for cp in load(nxt % 2, (gate_src.at[0], up_src.at[0])):
                    cp.wait()

        for nxt in range(1, N_DEV):
            @pl.when((j == nxt - 1) & (i == ask_at))
            def _(nxt=nxt):
                for a in used_here:
                    if nxt == 1:
                        gather.wait_sibling(fulls, sems, a)
                    elif nxt % 2 == 0:
                        gather.pass_on(fulls, sems, a, nxt // 2 - 1)
                    else:
                        gather.wait_passed(fulls, sems, a, nxt // 2 - 1)
                block = _arrival_block(nxt)
                for cp in load(nxt % 2, (gate_src.at[block], up_src.at[block])):
                    cp.start()

        @pl.when((j == N_DEV - 1) & (i == ask_at))
        def _():
            for other_chip in range(3):
                gather.pass_on(fulls, sems, down, other_chip)

        x = x_ref[...]
        g = lax.dot_general(x, w_ref[j % 2, 0], NT_DIMS, preferred_element_type=F32)
        u = lax.dot_general(x, w_ref[j % 2, 1], NT_DIMS, preferred_element_type=F32)
        sg = _sigmoid(g)
        silu = g * sg
        silu_ref[...] = silu.astype(BF16)
        udsilu_ref[...] = (u * (sg + silu * (1.0 - sg))).astype(BF16)
        h_ref[...] = (silu * u).astype(BF16)

        @pl.when((j == N_DEV - 1) & (i == ni - 1))
        def _():
            gather.wait_sibling(fulls, sems, down)
            for other_chip in range(3):
                gather.wait_passed(fulls, sems, down, other_chip)
            for a in range(n_g):
                gather.wait_sent(shards, fulls, sems, a)

    shp = jax.ShapeDtypeStruct((N_DEV, s, fs), BF16)
    o_spec = pl.BlockSpec((None, tm, fs), lambda j, i: (_arrival_block(j), i, 0))
    ins = ([] if gate_here else [gate]) + gather.inputs
    outs = pl.pallas_call(
        body, name=name, grid=(N_DEV, ni), in_specs=[pl.BlockSpec((tm, d), lambda j, i: (i, 0))] + [ANY] * len(ins),
        out_specs=[o_spec, o_spec, o_spec] + [ANY] * n_g, out_shape=[shp, shp, shp] + gather.out_shapes,
        scratch_shapes=[pltpu.VMEM((2, 2, fs, d), BF16), pltpu.SemaphoreType.DMA((2, 2))] + gather.scratch,
        compiler_params=pltpu.CompilerParams(dimension_semantics=("arbitrary", "arbitrary"), vmem_limit_bytes=V7X_VMEM_LIMIT),
    )(xb, *ins)
    return outs[:3], outs[3:]


def _down_ln(a3, w3, bias, res, res_affine, ln_g, ln_b, scale, name, target=None, rider=None):
    nk, s, tk = a3.shape
    d = w3.shape[2]
    tm = min(s, 256)
    final = target is not None

    def body(*refs):
        refs = list(refs)
        a_ref, w_hbm = refs[:2]
        del refs[:2]
        bias_ref = refs.pop(0) if bias is not None else None
        res_ref = refs.pop(0)
        rg_ref, rb_ref = (refs.pop(0), refs.pop(0)) if res_affine is not None else (None, None)
        g_ref, b_ref = refs.pop(0), refs.pop(0)
        t_ref = refs.pop(0) if final else None
        w_sem = refs.pop()
        w_ref = refs.pop()
        i = pl.program_id(0)
        if final:
            dr_ref, drb_ref, sq_ref, dg_ref, db_ref = refs
        else:
            xh_ref, hb_ref, rstd_ref = refs

        @pl.when(i == 0)
        def _():
            whole = pltpu.make_async_copy(w_hbm, w_ref, w_sem.at[0])
            whole.start()
            whole.wait()
            if final:
                sq_ref[...] = jnp.zeros_like(sq_ref)
                dg_ref[...] = jnp.zeros_like(dg_ref)
                db_ref[...] = jnp.zeros_like(db_ref)

        y = jnp.dot(a_ref[0], w_ref[0], preferred_element_type=F32)
        for k in range(1, nk):
            y = y + jnp.dot(a_ref[k], w_ref[k], preferred_element_type=F32)
        if bias_ref is not None:
            y = y + bias_ref[...]
        for rows in _row_blocks(tm):
            r = res_ref[rows, :]
            if rg_ref is not None:
                r = r * rg_ref[...] + rb_ref[...]
            r = ALPHA * r + scale * y[rows]
            mu = jnp.mean(r, axis=-1, keepdims=True)
            c = r - mu
            var = jnp.mean(c * c, axis=-1, keepdims=True)
            rstd = lax.rsqrt(var + LN_EPS)
            xhat = c * rstd
            h = xhat * g_ref[...] + b_ref[...]
            if not final:
                xh_ref[rows, :] = xhat
                hb_ref[rows, :] = h.astype(BF16)
                rstd_ref[rows, :] = rstd
            else:
                err = h - t_ref[rows, :]
                sq_ref[...] += jnp.sum(err * err, axis=0, keepdims=True)
                dh = err * (1.0 / d)
                dg_ref[...] += jnp.sum(dh * xhat, axis=0, keepdims=True)
                db_ref[...] += jnp.sum(dh, axis=0, keepdims=True)
                dr = _ln_backward(dh, xhat, rstd, g_ref[...])
                dr_ref[rows, :] = dr
                drb_ref[rows, :] = (scale * dr).astype(BF16)

    tok = pl.BlockSpec((tm, d), lambda i: (i, 0))
    vec = pl.BlockSpec((1, d), lambda i: (0, 0))
    ins = [a3, w3]
    in_specs = [pl.BlockSpec((nk, tm, tk), lambda i: (0, i, 0)), ANY]
    if bias is not None:
        ins.append(bias)
        in_specs.append(vec)
    ins.append(res)
    in_specs.append(tok)
    if res_affine is not None:
        ins += list(res_affine)
        in_specs += [vec, vec]
    ins += [ln_g, ln_b]
    in_specs += [vec, vec]
    if final:
        ins.append(target)
        in_specs.append(tok)
        out_shape = [jax.ShapeDtypeStruct((s, d), F32), jax.ShapeDtypeStruct((s, d), BF16)] + [jax.ShapeDtypeStruct((1, d), F32)] * 3
        out_specs = [tok, tok, vec, vec, vec]
    else:
        out_shape = [jax.ShapeDtypeStruct((s, d), F32), jax.ShapeDtypeStruct((s, d), BF16), jax.ShapeDtypeStruct((s, 1), F32)]
        out_specs = [tok, tok, pl.BlockSpec((tm, 1), lambda i: (i, 0))]
    scratch = [pltpu.VMEM((nk, tk, d), BF16), pltpu.SemaphoreType.DMA((1,))]
    return _call(body, name, (s // tm,), in_specs, out_specs, out_shape, scratch, ins, rider)


def _proj_in(hb, w, bias, name, rider=None):
    s, d = hb.shape
    n = w.shape[1]
    tm = min(s, 1024)
    tn = PROJ_IN_COLS if n % PROJ_IN_COLS == 0 else n

    def body(h_ref, w_ref, b_ref, z_ref):
        z_ref[...] = (jnp.dot(h_ref[...], w_ref[...], preferred_element_type=F32) + b_ref[...]).astype(BF16)

    in_specs = [pl.BlockSpec((tm, d), lambda i, j: (i, 0)), pl.BlockSpec((d, tn), lambda i, j: (0, j)),
                pl.BlockSpec((1, tn), lambda i, j: (0, j))]
    return _call(body, name, (s // tm, n // tn), in_specs, pl.BlockSpec((tm, tn), lambda i, j: (i, j)),
                 jax.ShapeDtypeStruct((s, n), BF16), [], [hb, w, bias], rider)


def _nt_hidden(ab, w3, name):
    s, kdim = ab.shape
    nj, tn, _ = w3.shape
    tm = min(s, 1024)

    def body(a_ref, w_ref, o_ref):
        o_ref[...] = lax.dot_general(a_ref[...], w_ref[...], NT_DIMS, preferred_element_type=F32).astype(BF16)

    in_specs = [pl.BlockSpec((tm, kdim), lambda i, j: (i, 0)), pl.BlockSpec((None, tn, kdim), lambda i, j: (j, 0, 0))]
    return _call(body, name, (s // tm, nj), in_specs, pl.BlockSpec((None, tm, tn), lambda i, j: (j, i, 0)),
                 jax.ShapeDtypeStruct((nj, s, tn), BF16), [], [ab, w3])[0]


def _ffn_bwd_hidden(ab, w3, silu3, udsilu3, name, rider=None):
    s, kdim = ab.shape
    nj, tn, _ = w3.shape
    tm = min(s, 1024)

    def body(a_ref, w_ref, silu_ref, udsilu_ref, dg_ref, du_ref):
        a = a_ref[...]
        for c0 in range(0, tn, MXU_COLS):
            cols = slice(c0, min(c0 + MXU_COLS, tn))
            t = lax.dot_general(a, w_ref[cols, :], NT_DIMS, preferred_element_type=F32)
            du_ref[:, cols] = (t * silu_ref[:, cols].astype(F32)).astype(BF16)
            dg_ref[:, cols] = (t * udsilu_ref[:, cols].astype(F32)).astype(BF16)

    hid = pl.BlockSpec((None, tm, tn), lambda i, j: (j, i, 0))
    shp = jax.ShapeDtypeStruct((nj, s, tn), BF16)
    in_specs = [pl.BlockSpec((tm, kdim), lambda i, j: (i, 0)), pl.BlockSpec((None, tn, kdim), lambda i, j: (j, 0, 0)), hid, hid]
    return _call(body, name, (s // tm, nj), in_specs, [hid, hid], [shp, shp], [], [ab, w3, silu3, udsilu3], rider)


def _tn_dw(a, a_spec, b, b_spec, nj, m, n, s, tk, name, rider):
    def body(a_ref, b_ref, o_ref, acc_ref):
        k = pl.program_id(1)

        @pl.when(k == 0)
        def _():
            acc_ref[...] = jnp.zeros_like(acc_ref)

        acc_ref[...] += lax.dot_general(a_ref[...], b_ref[...], TN_DIMS, preferred_element_type=F32)

        @pl.when(k == s // tk - 1)
        def _():
            o_ref[...] = acc_ref[...].astype(BF16)

    return _call(body, name, (nj, s // tk), [a_spec, b_spec], pl.BlockSpec((None, m, n), lambda j, k: (j, 0, 0)),
                 jax.ShapeDtypeStruct((nj, m, n), BF16), [pltpu.VMEM((m, n), F32)], [a, b], rider)


def _dw_hidden_rows(hid3, db, name, rider=None):
    nj, s, fs = hid3.shape
    d = db.shape[1]
    tk = min(s, DW_TOKENS)
    return _tn_dw(hid3, pl.BlockSpec((None, tk, fs), lambda j, k: (j, k, 0)), db, pl.BlockSpec((tk, d), lambda j, k: (k, 0)),
                  nj, fs, d, s, tk, name, rider)


def _dw_hidden_rows_paired(hid3, db, name, rider=None):
    nj, s, fs = hid3.shape
    d = db.shape[1]
    tk = min(s, DW_TOKENS)
    nk = s // tk
    half = nj // 2

    def device_of(j):
        c = lax.axis_index("c")
        return 2 * (j % half) + jnp.where(j < half, 1 - c, c)

    def body(a_ref, b_ref, o_ref, theirs_ref, acc_ref, stage_ref, got_ref, send_sems, recv_sems, load_sem):
        j, k = pl.program_id(0), pl.program_id(1)
        x, y, c = lax.axis_index("x"), lax.axis_index("y"), lax.axis_index("c")

        def to_sibling(q):
            return pltpu.make_async_remote_copy(src_ref=stage_ref, dst_ref=theirs_ref.at[q], send_sem=send_sems.at[q],
                                                recv_sem=recv_sems.at[q], device_id=(x, y, 1 - c), device_id_type=MESH)

        @pl.when(k == 0)
        def _():
            acc_ref[...] = jnp.zeros_like(acc_ref)

        acc_ref[...] += lax.dot_general(a_ref[...], b_ref[...], TN_DIMS, preferred_element_type=F32)

        for q in range(half):
            @pl.when((j == q) & (k == nk - 1))
            def _(q=q):
                if q > 0:
                    to_sibling(q - 1).wait_send()
                stage_ref[...] = acc_ref[...].astype(BF16)
                to_sibling(q).start()

            @pl.when((j == half + q) & (k == nk - 1))
            def _(q=q):
                if q == 0:
                    to_sibling(half - 1).wait_send()
                to_sibling(q).wait_recv()
                load = pltpu.make_async_copy(theirs_ref.at[q], got_ref, load_sem.at[0])
                load.start()
                load.wait()
                o_ref[...] = (acc_ref[...] + got_ref[...].astype(F32)).astype(BF16)

    in_specs = [pl.BlockSpec((None, tk, fs), lambda j, k: (device_of(j), k, 0)), pl.BlockSpec((tk, d), lambda j, k: (k, 0))]
    out_specs = [pl.BlockSpec((None, fs, d), lambda j, k: (jnp.maximum(j - half, 0), 0, 0)), ANY]
    shp = jax.ShapeDtypeStruct((half, fs, d), BF16)
    scratch = [pltpu.VMEM((fs, d), F32), pltpu.VMEM((fs, d), BF16), pltpu.VMEM((fs, d), BF16),
               pltpu.SemaphoreType.DMA((half,)), pltpu.SemaphoreType.DMA((half,)), pltpu.SemaphoreType.DMA((1,))]
    (sums, _), riders_out = _call(body, name, (nj, nk), in_specs, out_specs, [shp, shp], scratch, [hid3, db], rider)
    return sums, riders_out


def _dw_cols(ab, dz, name, rider=None):
    s, d = ab.shape
    n = dz.shape[1]
    tn = PROJ_IN_COLS if n % PROJ_IN_COLS == 0 else n
    tk = min(s, DW_TOKENS)
    tr = d // 2

    def body(a_ref, b_ref, o_ref, acc_ref):
        k = pl.program_id(2)

        @pl.when(k == 0)
        def _():
            acc_ref[...] = jnp.zeros_like(acc_ref)

        acc_ref[...] += lax.dot_general(a_ref[...], b_ref[...], TN_DIMS, preferred_element_type=F32)

        @pl.when(k == s // tk - 1)
        def _():
            o_ref[...] = acc_ref[...].astype(BF16)

    in_specs = [pl.BlockSpec((tk, tr), lambda j, r, k: (k, r)), pl.BlockSpec((tk, tn), lambda j, r, k: (k, j))]
    return _call(body, name, (n // tn, d // tr, s // tk), in_specs, pl.BlockSpec((tr, tn), lambda j, r, k: (r, j)),
                 jax.ShapeDtypeStruct((d, n), BF16), [pltpu.VMEM((tr, tn), F32)], [ab, dz], rider)


def _ffn_bwd_input(dg3, wg3, du3, wu3, dres, ln, name, rider=None):
    s, d = dres.shape
    nk, _, fs = dg3.shape
    tm = min(s, 512)

    def body(*refs):
        refs = list(refs)
        dg_in, wg_ref, du_in, wu_ref, dres_ref = refs[:5]
        del refs[:5]
        if ln is not None:
            xh_ref, rstd_ref, gain_ref = refs.pop(0), refs.pop(0), refs.pop(0)
        acc_ref = refs.pop()
        i, k = pl.program_id(0), pl.program_id(1)

        @pl.when(k == 0)
        def _():
            acc_ref[...] = jnp.zeros_like(acc_ref)

        acc_ref[...] += (jnp.dot(dg_in[...], wg_ref[...], preferred_element_type=F32)
                         + jnp.dot(du_in[...], wu_ref[...], preferred_element_type=F32))

        @pl.when(k == nk - 1)
        def _():
            if ln is not None:
                dr_ref, drb_ref, dg_ref, db_ref, sum_ref = refs

                @pl.when(i == 0)
                def _():
                    dg_ref[...] = jnp.zeros_like(dg_ref)
                    db_ref[...] = jnp.zeros_like(db_ref)
                    sum_ref[...] = jnp.zeros_like(sum_ref)

            for rows in _row_blocks(tm):
                dh = ALPHA * dres_ref[rows, :] + acc_ref[rows, :]
                if ln is None:
                    refs[0][rows, :] = dh
                else:
                    xhat = xh_ref[rows, :]
                    dg_ref[...] += jnp.sum(dh * xhat, axis=0, keepdims=True)
                    db_ref[...] += jnp.sum(dh, axis=0, keepdims=True)
                    dr = _ln_backward(dh, xhat, rstd_ref[rows, :], gain_ref[...])
                    sum_ref[...] += jnp.sum(dr, axis=0, keepdims=True)
                    dr_ref[rows, :] = dr
                    drb_ref[rows, :] = dr.astype(BF16)

    tok = pl.BlockSpec((tm, d), lambda i, k: (i, 0))
    vec = pl.BlockSpec((1, d), lambda i, k: (0, 0))
    a_spec = pl.BlockSpec((None, tm, fs), lambda i, k: (k, i, 0))
    w_spec = pl.BlockSpec((None, fs, d), lambda i, k: (k, 0, 0))
    ins, in_specs = [dg3, wg3, du3, wu3, dres], [a_spec, w_spec, a_spec, w_spec, tok]
    if ln is None:
        out_shape, out_specs = jax.ShapeDtypeStruct((s, d), F32), tok
    else:
        ins += list(ln)
        in_specs += [tok, pl.BlockSpec((tm, 1), lambda i, k: (i, 0)), vec]
        out_shape = [jax.ShapeDtypeStruct((s, d), F32), jax.ShapeDtypeStruct((s, d), BF16)] + [jax.ShapeDtypeStruct((1, d), F32)] * 3
        out_specs = [tok, tok, vec, vec, vec]
    return _call(body, name, (s // tm, nk), in_specs, out_specs, out_shape, [pltpu.VMEM((tm, d), F32)], ins, rider)


def _proj_in_bwd_ln(dz, w, dres, ln, branch_scale, name):
    s, d = dres.shape
    n = w.shape[1]
    tm = min(s, 256)

    def body(dz_ref, w_hbm, dres_ref, xh_ref, rstd_ref, gain_ref, dr_ref, drb_ref, dg_ref, db_ref, w_ref, w_sem):
        @pl.when(pl.program_id(0) == 0)
        def _():
            whole = pltpu.make_async_copy(w_hbm, w_ref, w_sem.at[0])
            whole.start()
            whole.wait()
            dg_ref[...] = jnp.zeros_like(dg_ref)
            db_ref[...] = jnp.zeros_like(db_ref)

        acc = lax.dot_general(dz_ref[...], w_ref[...], NT_DIMS, preferred_element_type=F32)
        for rows in _row_blocks(tm):
            dh = ALPHA * dres_ref[rows, :] + acc[rows]
            xhat = xh_ref[rows, :]
            dg_ref[...] += jnp.sum(dh * xhat, axis=0, keepdims=True)
            db_ref[...] += jnp.sum(dh, axis=0, keepdims=True)
            dr = _ln_backward(dh, xhat, rstd_ref[rows, :], gain_ref[...])
            dr_ref[rows, :] = dr
            drb_ref[rows, :] = (branch_scale * dr).astype(BF16)

    tok = pl.BlockSpec((tm, d), lambda i: (i, 0))
    vec = pl.BlockSpec((1, d), lambda i: (0, 0))
    in_specs = [pl.BlockSpec((tm, n), lambda i: (i, 0)), ANY, tok, tok, pl.BlockSpec((tm, 1), lambda i: (i, 0)), vec]
    out_shape = [jax.ShapeDtypeStruct((s, d), F32), jax.ShapeDtypeStruct((s, d), BF16)] + [jax.ShapeDtypeStruct((1, d), F32)] * 2
    scratch = [pltpu.VMEM(w.shape, BF16), pltpu.SemaphoreType.DMA((1,))]
    return _call(body, name, (s // tm,), in_specs, [tok, tok, vec, vec], out_shape, scratch, [dz, w, dres] + list(ln))[0]


def _shift_rows_down(v, halo, k, row):
    out = pltpu.roll(v, k, 0)
    hr = halo.shape[0]
    for r in range(k):
        out = jnp.where(row == r, halo[hr - k + r:hr - k + r + 1, :], out)
    return out


def _shift_rows_up(v, halo, k, row):
    t = v.shape[0]
    out = pltpu.roll(v, t - k, 0)
    for r in range(k):
        out = jnp.where(row == t - k + r, halo[r:r + 1, :], out)
    return out


def _sgu_head_forward(z_ref, h, da, gv_ref, bv_ref):
    zu = z_ref[:, h * HEAD:(h + 1) * HEAD].astype(F32)
    zv = z_ref[:, da + h * HEAD:da + (h + 1) * HEAD].astype(F32)
    tu, tv = _gelu_tanh(zu), _gelu_tanh(zv)
    u = _gelu(zu, tu)
    v = _gelu(zv, tv)
    mu = jnp.mean(v, axis=-1, keepdims=True)
    c = v - mu
    rstd = lax.rsqrt(jnp.mean(c * c, axis=-1, keepdims=True) + LN_EPS)
    vhat = c * rstd
    vln = (vhat * gv_ref[h:h + 1, :] + bv_ref[h:h + 1, :]).astype(BF16)
    return (zu, tu), (zv, tv), u, vhat, rstd, vln


def _mixer_fwd(z, ws_masked, bs_wide, gv, bv, cw, name):
    s, zc = z.shape
    da = zc // 5
    nh = da // HEAD
    tm = min(s, 512)
    hb = tm // BF16_ROWS

    def body(z_ref, pc_ref, px_ref, ws_ref, bs_ref, gv_ref, bv_ref, cw_ref, y_ref):
        i = pl.program_id(0)
        for h in range(nh):
            _, _, u, _, _, vln = _sgu_head_forward(z_ref, h, da, gv_ref, bv_ref)
            for n in range(tm // CHUNK):
                rows = slice(n * CHUNK, (n + 1) * CHUNK)
                mixed = jnp.dot(ws_ref[h], vln[rows], preferred_element_type=F32) + bs_ref[h]
                y_ref[0, rows, h * HEAD:(h + 1) * HEAD] = (u[rows] * mixed).astype(BF16)
        gate_b = z_ref[:, 2 * da:3 * da].astype(F32)
        hc = z_ref[:, 3 * da:4 * da].astype(F32) * z_ref[:, 4 * da:5 * da].astype(F32)
        halo = jnp.where(i > 0, pc_ref[...].astype(F32) * px_ref[...].astype(F32), 0.0)
        row = lax.broadcasted_iota(jnp.int32, (tm, da), 0)
        y = cw_ref[0:1, :] * _shift_rows_down(hc, halo, 2, row) + cw_ref[1:2, :] * _shift_rows_down(hc, halo, 1, row) + cw_ref[2:3, :] * hc
        y_ref[1] = (gate_b * y).astype(BF16)

    prev = lambda col: pl.BlockSpec((BF16_ROWS, da), lambda i: (jnp.maximum(i * hb - 1, 0), col))
    in_specs = [pl.BlockSpec((tm, zc), lambda i: (i, 0)), prev(3), prev(4), _row((nh, CHUNK, CHUNK)), _row((nh, CHUNK, HEAD)),
                _row((nh, HEAD)), _row((nh, HEAD)), _row((CONV_TAPS, da))]
    return _call(body, name, (s // tm,), in_specs, pl.BlockSpec((2, tm, da), lambda i: (0, i, 0)),
                 jax.ShapeDtypeStruct((2, s, da), BF16), [], [z, z, z, ws_masked, bs_wide, gv, bv, cw])[0]


def _mixer_bwd(z, dy, ws_masked, bs_wide, gv, bv, cw, name, rider=None):
    s, zc = z.shape
    da = zc // 5
    nh = da // HEAD
    tm = min(s, 512)
    hb = tm // BF16_ROWS
    nblk = s // tm

    def body(z_ref, pc_ref, px_ref, nb_ref, dy_ref, ndy_ref, ws_ref, bs_ref, gv_ref, bv_ref, cw_ref,
             dz_ref, dws_ref, dbs_ref, dgv_ref, dbv_ref, dcw_ref, dbin_ref):
        i = pl.program_id(0)

        @pl.when(i == 0)
        def _():
            for ref in (dws_ref, dbs_ref, dgv_ref, dbv_ref, dcw_ref, dbin_ref):
                ref[...] = jnp.zeros_like(ref)

        causal = lax.broadcasted_iota(jnp.int32, (CHUNK, CHUNK), 0) >= lax.broadcasted_iota(jnp.int32, (CHUNK, CHUNK), 1)
        for h in range(nh):
            zu, zv, u, vhat, rstd, vln = _sgu_head_forward(z_ref, h, da, gv_ref, bv_ref)
            dya = dy_ref[0, :, h * HEAD:(h + 1) * HEAD].astype(F32)
            w = ws_ref[h]
            du_parts, dvln_parts = [], []
            for n in range(tm // CHUNK):
                rows = slice(n * CHUNK, (n + 1) * CHUNK)
                mixed = jnp.dot(w, vln[rows], preferred_element_type=F32) + bs_ref[h]
                du_parts.append(dya[rows] * mixed)
                dmix = dya[rows] * u[rows]
                dmix_b = dmix.astype(BF16)
                dws_ref[h] += jnp.where(causal, lax.dot_general(dmix_b, vln[rows], NT_DIMS, preferred_element_type=F32), 0.0)
                dbs_ref[h] += dmix
                dvln_parts.append(lax.dot_general(w, dmix_b, TN_DIMS, preferred_element_type=F32))
            du = jnp.concatenate(du_parts, axis=0)
            dvln = jnp.concatenate(dvln_parts, axis=0)
            dgv_ref[h:h + 1, :] += jnp.sum(dvln * vhat, axis=0, keepdims=True)
            dbv_ref[h:h + 1, :] += jnp.sum(dvln, axis=0, keepdims=True)
            dv = _ln_backward(dvln, vhat, rstd, gv_ref[h:h + 1, :])
            dzu = du * _gelu_grad(*zu)
            dzv = dv * _gelu_grad(*zv)
            ucols = slice(h * HEAD, (h + 1) * HEAD)
            vcols = slice(da + h * HEAD, da + (h + 1) * HEAD)
            dz_ref[:, ucols] = dzu.astype(BF16)
            dz_ref[:, vcols] = dzv.astype(BF16)
            dbin_ref[:, ucols] += jnp.sum(dzu, axis=0, keepdims=True)
            dbin_ref[:, vcols] += jnp.sum(dzv, axis=0, keepdims=True)

        gate_b = z_ref[:, 2 * da:3 * da].astype(F32)
        gate_c = z_ref[:, 3 * da:4 * da].astype(F32)
        xt = z_ref[:, 4 * da:5 * da].astype(F32)
        hc = gate_c * xt
        halo = jnp.where(i > 0, pc_ref[...].astype(F32) * px_ref[...].astype(F32), 0.0)
        row = lax.broadcasted_iota(jnp.int32, (tm, da), 0)
        sh1 = _shift_rows_down(hc, halo, 1, row)
        sh2 = _shift_rows_down(hc, halo, 2, row)
        y = cw_ref[0:1, :] * sh2 + cw_ref[1:2, :] * sh1 + cw_ref[2:3, :] * hc
        dyb = dy_ref[1].astype(F32)
        dconv = dyb * gate_b
        nhalo = jnp.where(i < nblk - 1, ndy_ref[...].astype(F32) * nb_ref[...].astype(F32), 0.0)
        dhc = cw_ref[2:3, :] * dconv + cw_ref[1:2, :] * _shift_rows_up(dconv, nhalo, 1, row) + cw_ref[0:1, :] * _shift_rows_up(dconv, nhalo, 2, row)
        dcw_ref[0:1, :] += jnp.sum(dconv * sh2, axis=0, keepdims=True)
        dcw_ref[1:2, :] += jnp.sum(dconv * sh1, axis=0, keepdims=True)
        dcw_ref[2:3, :] += jnp.sum(dconv * hc, axis=0, keepdims=True)
        for col, val in ((2, dyb * y), (3, dhc * xt), (4, dhc * gate_c)):
            cols = slice(col * da, (col + 1) * da)
            dz_ref[:, cols] = val.astype(BF16)
            dbin_ref[:, cols] += jnp.sum(val, axis=0, keepdims=True)

        @pl.when(i == nblk - 1)
        def _():
            for h in range(nh):
                dbs_ref[h] = jnp.broadcast_to(jnp.sum(dbs_ref[h], axis=1, keepdims=True), (CHUNK, HEAD))

    prev = lambda col: pl.BlockSpec((BF16_ROWS, da), lambda i: (jnp.maximum(i * hb - 1, 0), col))
    nxt = lambda i: jnp.minimum((i + 1) * hb, s // BF16_ROWS - 1)
    in_specs = [pl.BlockSpec((tm, zc), lambda i: (i, 0)), prev(3), prev(4), pl.BlockSpec((BF16_ROWS, da), lambda i: (nxt(i), 2)),
                pl.BlockSpec((2, tm, da), lambda i: (0, i, 0)), pl.BlockSpec((None, BF16_ROWS, da), lambda i: (1, nxt(i), 0)),
                _row((nh, CHUNK, CHUNK)), _row((nh, CHUNK, HEAD)), _row((nh, HEAD)), _row((nh, HEAD)), _row((CONV_TAPS, da))]
    out_specs = [pl.BlockSpec((tm, zc), lambda i: (i, 0)), _row((nh, CHUNK, CHUNK)), _row((nh, CHUNK, HEAD)), _row((nh, HEAD)),
                 _row((nh, HEAD)), _row((8, da)), _row((1, zc))]
    out_shape = [jax.ShapeDtypeStruct((s, zc), BF16), jax.ShapeDtypeStruct((nh, CHUNK, CHUNK), F32),
                 jax.ShapeDtypeStruct((nh, CHUNK, HEAD), F32), jax.ShapeDtypeStruct((nh, HEAD), F32),
                 jax.ShapeDtypeStruct((nh, HEAD), F32), jax.ShapeDtypeStruct((8, da), F32), jax.ShapeDtypeStruct((1, zc), F32)]
    return _call(body, name, (nblk,), in_specs, out_specs, out_shape, [], [z, z, z, z, dy, dy, ws_masked, bs_wide, gv, bv, cw], rider)


def _adam_update(g, w, m, v):
    m_new = ADAM_B1 * m + (1.0 - ADAM_B1) * g
    v_new = ADAM_B2 * v + (1.0 - ADAM_B2) * (g * g)
    m_hat = m_new / (1.0 - ADAM_B1 ** ADAM_STEP)
    v_hat = v_new / (1.0 - ADAM_B2 ** ADAM_STEP)
    return -ADAM_LR * (m_hat / (jnp.sqrt(v_hat) + ADAM_EPS) + ADAM_WD * w), m_new, v_new


def _adamw(gparts, w, m, v, name):
    n, r, c = gparts.shape
    tr = r // 4 if (r // 4) % BF16_ROWS == 0 else r

    def body(g_ref, w_ref, m_ref, v_ref, go_ref, d_ref, mo_ref, vo_ref):
        g = g_ref[0].astype(F32)
        for q in range(1, n):
            g = g + g_ref[q].astype(F32)
        go_ref[...] = g
        d_ref[...], mo_ref[...], vo_ref[...] = _adam_update(g, w_ref[...], m_ref[...], v_ref[...])

    blk = pl.BlockSpec((tr, c), lambda i: (i, 0))
    shp = jax.ShapeDtypeStruct((r, c), F32)
    return _call(body, name, (r // tr,), [pl.BlockSpec((n, tr, c), lambda i: (0, i, 0)), blk, blk, blk], [blk] * 4, [shp] * 4, [],
                 [gparts, w, m, v])[0]


def _adamw_small(packs, rows, w, m, v, conv, name):
    n_par, n_dev = len(rows), packs.shape[0]
    taps = conv[0].shape[0]

    def body(*refs):
        refs = list(refs)
        cut = lambda n: [refs.pop(0) for _ in range(n)]
        p_ref, w_refs, m_refs, v_refs, (cw_ref, cm_ref, cv_ref) = refs.pop(0), cut(n_par), cut(n_par), cut(n_par), cut(3)
        outs = [cut(4) for _ in range(n_par + 1)]
        at = 0
        for k in range(n_par):
            g = p_ref[0, at:at + rows[k], :]
            for dev in range(1, n_dev):
                g = g + p_ref[dev, at:at + rows[k], :]
            go_ref, d_ref, mo_ref, vo_ref = outs[k]
            go_ref[...] = g
            d_ref[...], mo_ref[...], vo_ref[...] = _adam_update(g, w_refs[k][...], m_refs[k][...], v_refs[k][...])
            at += rows[k]
        me = 4 * lax.axis_index("x") + 2 * lax.axis_index("y") + lax.axis_index("c")
        go_ref, d_ref, mo_ref, vo_ref = outs[n_par]
        for tap in range(taps):
            row = pl.ds(at + tap * n_dev + me, 1)
            g = p_ref[0, row, :]
            for dev in range(1, n_dev):
                g = g + p_ref[dev, row, :]
            one = slice(tap, tap + 1)
            go_ref[one, :] = g
            d_ref[one, :], mo_ref[one, :], vo_ref[one, :] = _adam_update(g, cw_ref[one, :], cm_ref[one, :], cv_ref[one, :])

    vmem = pl.BlockSpec(memory_space=pltpu.VMEM)
    ins = [packs] + list(w) + list(m) + list(v) + list(conv)
    out_shape = [jax.ShapeDtypeStruct(a.shape, F32) for a in list(w) + [conv[0]] for _ in range(4)]
    outs = pl.pallas_call(body, name=name, in_specs=[vmem] * len(ins), out_specs=[vmem] * len(out_shape), out_shape=out_shape,
                          compiler_params=pltpu.CompilerParams(vmem_limit_bytes=V7X_VMEM_LIMIT))(*ins)
    return [outs[4 * k:4 * k + 4] for k in range(n_par + 1)]


def _rows128(a):
    return a.reshape(-1, LANES)


def kernel(x, ffa_gate, ffa_up, ffa_down, ln_a_g, ln_a_b, w_in, b_in, w_s, b_s, ln_v_g, ln_v_b, conv_w, w_out, b_out, ln_m_g, ln_m_b, ffc_gate, ffc_up, ffc_down, ln_c_g, ln_c_b, loss_target, m_ffa_gate, m_ffa_up, m_ffa_down, m_ln_a_g, m_ln_a_b, m_w_in, m_b_in, m_w_s, m_b_s, m_ln_v_g, m_ln_v_b, m_conv_w, m_w_out, m_b_out, m_ln_m_g, m_ln_m_b, m_ffc_gate, m_ffc_up, m_ffc_down, m_ln_c_g, m_ln_c_b, v_ffa_gate, v_ffa_up, v_ffa_down, v_ln_a_g, v_ln_a_b, v_w_in, v_b_in, v_w_s, v_b_s, v_ln_v_g, v_ln_v_b, v_conv_w, v_w_out, v_b_out, v_ln_m_g, v_ln_m_b, v_ffc_gate, v_ffc_up, v_ffc_down, v_ln_c_g, v_ln_c_b):
    x2, target = x[0], loss_target[0]
    s, d = x2.shape
    da = d // 2
    nh = da // HEAD

    big = dict(ffa_gate=ffa_gate, ffa_up=ffa_up, ffa_down=ffa_down, w_in=w_in, w_out=w_out, ffc_gate=ffc_gate, ffc_up=ffc_up, ffc_down=ffc_down)
    big_m = dict(ffa_gate=m_ffa_gate, ffa_up=m_ffa_up, ffa_down=m_ffa_down, w_in=m_w_in, w_out=m_w_out, ffc_gate=m_ffc_gate, ffc_up=m_ffc_up, ffc_down=m_ffc_down)
    big_v = dict(ffa_gate=v_ffa_gate, ffa_up=v_ffa_up, ffa_down=v_ffa_down, w_in=v_w_in, w_out=v_w_out, ffc_gate=v_ffc_gate, ffc_up=v_ffc_up, ffc_down=v_ffc_down)
    local = lambda k, a: jnp.transpose(a[0]) if k in TRANSPOSED else a[0]
    shard = {k: local(k, w).astype(BF16) for k, w in big.items()}
    conv_rows = jnp.pad(conv_w[0], ((0, 8 - CONV_TAPS), (0, 0)))

    tril = jnp.tril(jnp.ones((CHUNK, CHUNK), dtype=bool))
    ws_masked = jnp.where(tril[None], w_s[0], 0.0).astype(BF16)
    bs_wide = jnp.broadcast_to(b_s[0][:, :, None], (nh, CHUNK, HEAD))
    gv, bv = ln_v_g.reshape(nh, HEAD), ln_v_b.reshape(nh, HEAD)

    full = {}
    xb = x2.astype(BF16)
    (silu_a, udsilu_a, hid_a), (full["ffa_gate"], full["ffa_up"], full["ffa_down"]) = _ffn_gateup_gathering(
        xb, shard["ffa_gate"], shard["ffa_up"], shard["ffa_down"], "ffa_gateup")
    (xhat1, h1b, rstd1), (full["w_in"], full["w_out"], conv_full) = _down_ln(
        hid_a, full["ffa_down"], None, x2, None, ln_a_g, ln_a_b, 0.5, "ffa_down_ln", rider=_Gather([shard["w_in"], shard["w_out"], conv_rows], by_columns=(0,)))
    cw = jnp.transpose(conv_full[:, :CONV_TAPS, :], (1, 0, 2)).reshape(CONV_TAPS, da)
    w_out2 = full["w_out"].reshape(2, da, d)
    z, (full["ffc_gate"],) = _proj_in(h1b, full["w_in"], b_in, "proj_in", _Gather([shard["ffc_gate"]]))
    ycat = _mixer_fwd(z, ws_masked, bs_wide, gv, bv, cw, "mixer_fwd")
    (xhat2, h2b, rstd2), _ = _down_ln(ycat, w_out2, b_out, xhat1, (ln_a_g, ln_a_b), ln_m_g, ln_m_b, 1.0, "proj_out_ln")
    (silu_c, udsilu_c, hid_c), (full["ffc_up"], full["ffc_down"]) = _ffn_gateup_gathering(
        h2b, full["ffc_gate"], shard["ffc_up"], shard["ffc_down"], "ffc_gateup")
    (dr3, dr3b, sq_err, d_ln_c_g, d_ln_c_b), _ = _down_ln(hid_c, full["ffc_down"], None, xhat2, (ln_m_g, ln_m_b), ln_c_g, ln_c_b, 0.5,
                                                          "ffc_down_ln_loss", target=target)
    loss = lax.psum((0.5 / d) * jnp.sum(sq_err), ("x", "y", "c"))

    landed = {}
    (dg_c, du_c), _ = _ffn_bwd_hidden(dr3b, full["ffc_down"], silu_c, udsilu_c, "ffc_bwd_hidden")
    part, _ = _dw_hidden_rows_paired(hid_c, dr3b, "ffc_dw_down")
    part, (landed["ffc_down"],) = _dw_hidden_rows_paired(dg_c, h2b, "ffc_dw_gate", _ChipScatter(part))
    part, (landed["ffc_gate"],) = _dw_hidden_rows_paired(du_c, h2b, "ffc_dw_up", _ChipScatter(part))
    (dr2, dr2b, d_ln_m_g, d_ln_m_b, d_b_out), (landed["ffc_up"],) = _ffn_bwd_input(
        dg_c, full["ffc_gate"], du_c, full["ffc_up"], dr3, (xhat2, rstd2, ln_m_g), "ffc_bwd_input_ln", _ChipScatter(part))
    dycat = _nt_hidden(dr2b, w_out2, "proj_out_bwd")
    part, _ = _dw_hidden_rows(ycat, dr2b, "proj_out_dw")
    (dz, d_w_s, d_b_s_wide, d_gv, d_bv, d_cw, d_b_in), (landed["w_out"],) = _mixer_bwd(
        z, dycat, ws_masked, bs_wide, gv, bv, cw, "mixer_bwd", _Scatter(part.reshape(N_DEV, d // N_DEV, d)))
    dr1, dr1b, d_ln_a_g, d_ln_a_b = _proj_in_bwd_ln(dz, full["w_in"], dr2, (xhat1, rstd1, ln_a_g), 0.5, "proj_in_bwd_ln")
    small_g = dict(ln_a_g=d_ln_a_g, ln_a_b=d_ln_a_b, b_in=d_b_in, w_s=d_w_s, b_s=d_b_s_wide[:, :, 0], ln_v_g=d_gv, ln_v_b=d_bv, b_out=d_b_out,
                   ln_m_g=d_ln_m_g, ln_m_b=d_ln_m_b, ln_c_g=d_ln_c_g, ln_c_b=d_ln_c_b)
    pack = jnp.concatenate([_rows128(g) for g in small_g.values()] + [_rows128(d_cw[:CONV_TAPS])], axis=0)
    part, (packs,) = _dw_cols(h1b, dz, "proj_in_dw", _Scatter(pack, whole=True))
    (dg_a, du_a), (landed["w_in"],) = _ffn_bwd_hidden(dr1b, full["ffa_down"], silu_a, udsilu_a, "ffa_bwd_hidden", _Scatter(part, by_columns=True))
    part, _ = _dw_hidden_rows_paired(hid_a, dr1b, "ffa_dw_down")
    part, (landed["ffa_down"],) = _dw_hidden_rows_paired(dg_a, xb, "ffa_dw_gate", _ChipScatter(part))
    part, (landed["ffa_gate"],) = _dw_hidden_rows_paired(du_a, xb, "ffa_dw_up", _ChipScatter(part))
    grad_x, (landed["ffa_up"],) = _ffn_bwd_input(dg_a, full["ffa_gate"], du_a, full["ffa_up"], dr1, None, "ffa_bwd_input", _ChipScatter(part))

    grads, deltas, new_m, new_v = {}, {}, {}, {}
    for k in big:
        out = _adamw(landed[k], local(k, big[k]), local(k, big_m[k]), local(k, big_v[k]), "adamw_" + k)
        grads[k], deltas[k], new_m[k], new_v[k] = ((jnp.transpose(o) if k in TRANSPOSED else o).reshape(big[k].shape) for o in out)

    small = dict(ln_a_g=ln_a_g, ln_a_b=ln_a_b, b_in=b_in, w_s=w_s, b_s=b_s, ln_v_g=ln_v_g, ln_v_b=ln_v_b, b_out=b_out,
                 ln_m_g=ln_m_g, ln_m_b=ln_m_b, ln_c_g=ln_c_g, ln_c_b=ln_c_b)
    small_m = dict(ln_a_g=m_ln_a_g, ln_a_b=m_ln_a_b, b_in=m_b_in, w_s=m_w_s, b_s=m_b_s, ln_v_g=m_ln_v_g, ln_v_b=m_ln_v_b, b_out=m_b_out,
                   ln_m_g=m_ln_m_g, ln_m_b=m_ln_m_b, ln_c_g=m_ln_c_g, ln_c_b=m_ln_c_b)
    small_v = dict(ln_a_g=v_ln_a_g, ln_a_b=v_ln_a_b, b_in=v_b_in, w_s=v_w_s, b_s=v_b_s, ln_v_g=v_ln_v_g, ln_v_b=v_ln_v_b, b_out=v_b_out,
                   ln_m_g=v_ln_m_g, ln_m_b=v_ln_m_b, ln_c_g=v_ln_c_g, ln_c_b=v_ln_c_b)
    snames = list(small)
    assert snames == list(small_g) and conv_w.shape[2] == LANES and da == N_DEV * LANES
    views = lambda tree: [_rows128(tree[k]) for k in snames]
    out = _adamw_small(packs, [a.shape[0] for a in views(small)], views(small), views(small_m), views(small_v),
                       (conv_w[0], m_conv_w[0], v_conv_w[0]), "adamw_small")
    for k, per_param in zip(snames + ["conv_w"], out):
        shape = conv_w.shape if k == "conv_w" else small[k].shape
        grads[k], deltas[k], new_m[k], new_v[k] = (o.reshape(shape) for o in per_param)

    order = ["ffa_gate", "ffa_up", "ffa_down", "ln_a_g", "ln_a_b", "w_in", "b_in", "w_s", "b_s", "ln_v_g", "ln_v_b", "conv_w", "w_out", "b_out",
             "ln_m_g", "ln_m_b", "ffc_gate", "ffc_up", "ffc_down", "ln_c_g", "ln_c_b"]
    return (loss, grad_x[None], *[grads[k] for k in order], *[deltas[k] for k in order], *[new_m[k] for k in order], *[new_v[k] for k in order])
```

```python
import math

import jax
import jax.numpy as jnp
from jax import lax
from jax.experimental import pallas as pl
from jax.experimental.pallas import tpu as pltpu

BF16 = jnp.bfloat16
F32 = jnp.float32
MESH = pl.DeviceIdType.MESH

N_DEV = 8
HEAD = 128
CHUNK = 128
CONV_TAPS = 3
LN_EPS = 1e-5
ALPHA = float(2 ** 0.25)
GELU_C = 0.7978845608028654
GELU_A = 0.044715
ADAM_LR, ADAM_B1, ADAM_B2, ADAM_EPS, ADAM_WD, ADAM_STEP = 0.001, 0.9, 0.999, 1e-08, 0.01, 10
V7X_VMEM_LIMIT = 56 * 1024 * 1024
LANES = 128
BF16_ROWS = 16
MXU_COLS = 256
TRANSPOSED = ("ffa_gate", "ffa_up", "ffc_gate", "ffc_up")
PROJ_IN_COLS = 1280
MIXER_TOKENS = 512
DW_TOKENS = 2048

NT_DIMS = (((1,), (1,)), ((), ()))
TN_DIMS = (((0,), (0,)), ((), ()))
ANY = pl.BlockSpec(memory_space=pl.ANY)


def _gelu_tanh(x):
    return jnp.tanh(GELU_C * (x + GELU_A * x * x * x))


def _gelu(x, t):
    return 0.5 * x * (1.0 + t)


def _gelu_grad(x, t):
    return 0.5 * (1.0 + t) + 0.5 * x * (1.0 - t * t) * GELU_C * (1.0 + 3.0 * GELU_A * x * x)


def _sigmoid(x):
    return 0.5 * jnp.tanh(0.5 * x) + 0.5


def _row(shape):
    return pl.BlockSpec(shape, lambda *_: (0,) * len(shape))


def _row_blocks(tm, rows=128):
    rows = min(rows, tm)
    return [slice(r, r + rows) for r in range(0, tm, rows)]


def _ln_backward(dh, xhat, rstd, gain):
    dxh = dh * gain
    m1 = jnp.mean(dxh, axis=-1, keepdims=True)
    m2 = jnp.mean(dxh * xhat, axis=-1, keepdims=True)
    return rstd * (dxh - m1 - xhat * m2)


def _place():
    x, y, c = lax.axis_index("x"), lax.axis_index("y"), lax.axis_index("c")
    return x, y, c, [(1 - x, y), (x, 1 - y), (1 - x, 1 - y)]


def _other_devices(x, y, c):
    flips = [(bx, by, bc) for bx in (0, 1) for by in (0, 1) for bc in (0, 1)][1:]
    return [(1 - x if bx else x, 1 - y if by else y, 1 - c if bc else c) for bx, by, bc in flips]


class _Gather:
    def __init__(self, shards, forward_at=0.75, by_columns=()):
        n = len(shards)
        self.n, self.forward_at, self.by_columns = n, forward_at, tuple(by_columns)
        self.inputs = list(shards)
        self.out_shapes = [jax.ShapeDtypeStruct((a.shape[0], N_DEV * a.shape[1]) if i in self.by_columns else (N_DEV,) + a.shape, a.dtype)
                           for i, a in enumerate(shards)]
        self.scratch = [pltpu.SemaphoreType.DMA((n, 7)), pltpu.SemaphoreType.DMA((n, 7)), pltpu.SemaphoreType.DMA((n,))]

    def _block(self, outs, a, dev):
        if a in self.by_columns:
            cols = outs[a].shape[1] // N_DEV
            return outs[a].at[:, pl.ds(dev * cols, cols)]
        return outs[a].at[dev]

    def _copy(self, outs, sems, a, k, block, to, src=None):
        dst = self._block(outs, a, block)
        return pltpu.make_async_remote_copy(src_ref=dst if src is None else src, dst_ref=dst, send_sem=sems[0].at[a, k],
                                            recv_sem=sems[1].at[a, k], device_id=to, device_id_type=MESH)

    def start(self, ins, outs, sems, urgent=None):
        x, y, c, chips = _place()
        me = 4 * x + 2 * y + c
        for a in range(self.n):
            pltpu.make_async_copy(ins[a], self._block(outs, a, me), sems[2].at[a]).start()
        urgent = list(range(self.n)) if urgent is None else list(urgent)
        for group in (urgent, [a for a in range(self.n) if a not in urgent]):
            for a in group:
                self._copy(outs, sems, a, 0, me, (x, y, 1 - c), src=ins[a]).start()
                for j in (0, 1):
                    self._copy(outs, sems, a, 1 + j, me, (*chips[j], c), src=ins[a]).start()
            for a in group:
                self._copy(outs, sems, a, 3, me, (*chips[2], c), src=ins[a]).start()

    def wait_sibling(self, outs, sems, a):
        x, y, c, _ = _place()
        self._copy(outs, sems, a, 0, 4 * x + 2 * y + 1 - c, (x, y, 1 - c)).wait_recv()

    def pass_on(self, outs, sems, a, j):
        x, y, c, chips = _place()
        block = 4 * chips[j][0] + 2 * chips[j][1] + c
        self._copy(outs, sems, a, 1 + j, block, (x, y, 1 - c)).wait_recv()
        self._copy(outs, sems, a, 4 + j, block, (x, y, 1 - c)).start()

    def wait_passed(self, outs, sems, a, j):
        x, y, c, chips = _place()
        self._copy(outs, sems, a, 4 + j, 4 * chips[j][0] + 2 * chips[j][1] + 1 - c, (x, y, 1 - c)).wait_recv()

    def wait_sent(self, ins, outs, sems, a):
        x, y, c, _ = _place()
        me = 4 * x + 2 * y + c
        for k in range(7):
            self._copy(outs, sems, a, k, me, (x, y, 1 - c), src=ins[a]).wait_send()
        pltpu.make_async_copy(ins[a], self._block(outs, a, me), sems[2].at[a]).wait()

    def forward(self, ins, outs, sems):
        for a in range(self.n):
            for j in range(3):
                self.pass_on(outs, sems, a, j)

    def finish(self, ins, outs, sems):
        for a in range(self.n):
            self.wait_sibling(outs, sems, a)
            for j in range(3):
                self.wait_passed(outs, sems, a, j)
        for a in range(self.n):
            self.wait_sent(ins, outs, sems, a)

    def before(self, step, n_steps, ins, outs, sems):
        pl.when(step == 0)(lambda: self.start(ins, outs, sems))
        pl.when(step == int(self.forward_at * (n_steps - 1)))(lambda: self.forward(ins, outs, sems))

    def after(self, step, n_steps, ins, outs, sems):
        pl.when(step == n_steps - 1)(lambda: self.finish(ins, outs, sems))


class _Scatter:
    def __init__(self, partial, whole=False, by_columns=False):
        self.whole, self.by_columns = whole, by_columns
        self.inputs = [partial]
        if whole:
            shape = (N_DEV,) + partial.shape
        elif by_columns:
            shape = (N_DEV, partial.shape[0], partial.shape[1] // N_DEV)
        else:
            shape = partial.shape
        self.out_shapes = [jax.ShapeDtypeStruct(shape, partial.dtype)]
        self.scratch = [pltpu.SemaphoreType.DMA((7,)), pltpu.SemaphoreType.DMA((7,)), pltpu.SemaphoreType.DMA((1,))]

    def _copies(self, ins, outs, sems):
        x, y, c, _ = _place()
        me = 4 * x + 2 * y + c
        if self.whole:
            block = lambda dev: ins[0]
        elif self.by_columns:
            cols = ins[0].shape[1] // N_DEV
            block = lambda dev: ins[0].at[:, pl.ds(dev * cols, cols)]
        else:
            block = lambda dev: ins[0].at[dev]
        mine = pltpu.make_async_copy(block(me), outs[0].at[me], sems[2].at[0])
        remote = [pltpu.make_async_remote_copy(src_ref=block(4 * px + 2 * py + pc), dst_ref=outs[0].at[me], send_sem=sems[0].at[k],
                                               recv_sem=sems[1].at[k], device_id=(px, py, pc), device_id_type=MESH)
                  for k, (px, py, pc) in enumerate(_other_devices(x, y, c))]
        return mine, remote

    def start(self, ins, outs, sems):
        mine, remote = self._copies(ins, outs, sems)
        mine.start()
        for cp in remote:
            cp.start()

    def finish(self, ins, outs, sems):
        mine, remote = self._copies(ins, outs, sems)
        for cp in remote:
            cp.wait()
        mine.wait()

    def before(self, step, n_steps, ins, outs, sems):
        pl.when(step == 0)(lambda: self.start(ins, outs, sems))

    def after(self, step, n_steps, ins, outs, sems):
        pl.when(step == n_steps - 1)(lambda: self.finish(ins, outs, sems))


class _ChipScatter(_Scatter):
    def __init__(self, sums):
        super().__init__(sums)
        self.scratch = [pltpu.SemaphoreType.DMA((3,)), pltpu.SemaphoreType.DMA((3,)), pltpu.SemaphoreType.DMA((1,))]

    def _copies(self, ins, outs, sems):
        x, y, c, chips = _place()
        my_chip = 2 * x + y
        mine = pltpu.make_async_copy(ins[0].at[my_chip], outs[0].at[my_chip], sems[2].at[0])
        remote = [pltpu.make_async_remote_copy(src_ref=ins[0].at[2 * px + py], dst_ref=outs[0].at[my_chip], send_sem=sems[0].at[k],
                                               recv_sem=sems[1].at[k], device_id=(px, py, c), device_id_type=MESH)
                  for k, (px, py) in enumerate(chips)]
        return mine, remote


def _call(body, name, grid, in_specs, out_specs, out_shape, scratch, ins, rider=None):
    single = not isinstance(out_shape, (list, tuple))
    out_shape = [out_shape] if single else list(out_shape)
    out_specs = [out_specs] if single else list(out_specs)
    params = pltpu.CompilerParams(dimension_semantics=("arbitrary",) * len(grid), vmem_limit_bytes=V7X_VMEM_LIMIT)
    if rider is None:
        outs = pl.pallas_call(body, name=name, grid=grid, in_specs=in_specs, out_specs=out_specs, out_shape=out_shape,
                              scratch_shapes=scratch, compiler_params=params)(*ins)
        return (outs[0] if single else outs), None
    n_in, n_out, n_scr = len(ins), len(out_shape), len(scratch)
    r_in, r_out = len(rider.inputs), len(rider.out_shapes)
    n_steps = math.prod(grid)

    def carried(*refs):
        refs = list(refs)
        cut = lambda n: [refs.pop(0) for _ in range(n)]
        b_in, c_in, b_out, c_out, b_scr = cut(n_in), cut(r_in), cut(n_out), cut(r_out), cut(n_scr)
        step = 0
        for axis, size in enumerate(grid):
            step = step * size + pl.program_id(axis)
        rider.before(step, n_steps, c_in, c_out, refs)
        body(*b_in, *b_out, *b_scr)
        rider.after(step, n_steps, c_in, c_out, refs)

    outs = pl.pallas_call(
        carried, name=name, grid=grid, in_specs=list(in_specs) + [ANY] * r_in, out_specs=out_specs + [ANY] * r_out,
        out_shape=out_shape + rider.out_shapes, scratch_shapes=list(scratch) + rider.scratch, compiler_params=params,
    )(*ins, *rider.inputs)
    base = outs[:n_out]
    return (base[0] if single else base), outs[n_out:]


def _arrival_block(j):
    x, y, c = lax.axis_index("x"), lax.axis_index("y"), lax.axis_index("c")
    chip, other_core = j // 2, j % 2
    px = jnp.where((chip == 1) | (chip == 3), 1 - x, x)
    py = jnp.where((chip == 2) | (chip == 3), 1 - y, y)
    pc = jnp.where(other_core == 1, 1 - c, c)
    return 4 * px + 2 * py + pc


def _ffn_gateup_gathering(xb, gate, up_shard, down_shard, name):
    s, d = xb.shape
    fs = up_shard.shape[0]
    tm = min(s, 1024)
    ni = s // tm
    ask_at = max(ni - 2, 0)
    gate_here = gate.ndim == 2
    gather = _Gather(([gate] if gate_here else []) + [up_shard, down_shard])
    n_g = gather.n
    used_here, down = tuple(range(n_g - 1)), n_g - 1

    def body(x_ref, *refs):
        refs = list(refs)
        gate_full = None if gate_here else refs.pop(0)
        shards = [refs.pop(0) for _ in range(n_g)]
        silu_ref, udsilu_ref, h_ref = refs.pop(0), refs.pop(0), refs.pop(0)
        fulls = [refs.pop(0) for _ in range(n_g)]
        w_ref, w_sems = refs.pop(0), refs.pop(0)
        sems = refs
        j, i = pl.program_id(0), pl.program_id(1)
        gate_src, up_src = (fulls[0], fulls[1]) if gate_here else (gate_full, fulls[0])

        def load(slot, srcs):
            return [pltpu.make_async_copy(src, w_ref.at[slot, a], w_sems.at[slot, a]) for a, src in enumerate(srcs)]

        @pl.when((j == 0) & (i == 0))
        def _():
            gather.start(shards, fulls, sems, urgent=used_here)
            mine = load(0, (shards[0] if gate_here else gate_full.at[_arrival_block(0)], shards[n_g - 2]))
            for cp in mine:
                cp.start()
            for cp in mine:
                cp.wait()

        for nxt in range(1, N_DEV):
            @pl.when((j == nxt) & (i == 0))
            def _(nxt=nxt):
                for cp in load(nxt % 2, (gate_src.at[0], up_src.at[0])):
                    cp.wait()

        for nxt in range(1, N_DEV):
            @pl.when((j == nxt - 1) & (i == ask_at))
            def _(nxt=nxt):
                for a in used_here:
                    if nxt == 1:
                        gather.wait_sibling(fulls, sems, a)
                    elif nxt % 2 == 0:
                        gather.pass_on(fulls, sems, a, nxt // 2 - 1)
                    else:
                        gather.wait_passed(fulls, sems, a, nxt // 2 - 1)
                block = _arrival_block(nxt)
                for cp in load(nxt % 2, (gate_src.at[block], up_src.at[block])):
                    cp.start()

        @pl.when((j == N_DEV - 1) & (i == ask_at))
        def _():
            for other_chip in range(3):
                gather.pass_on(fulls, sems, down, other_chip)

        x = x_ref[...]
        g = lax.dot_general(x, w_ref[j % 2, 0], NT_DIMS, preferred_element_type=F32)
        u = lax.dot_general(x, w_ref[j % 2, 1], NT_DIMS, preferred_element_type=F32)
        sg = _sigmoid(g)
        silu = g * sg
        silu_ref[...] = silu.astype(BF16)
        udsilu_ref[...] = (u * (sg + silu * (1.0 - sg))).astype(BF16)
        h_ref[...] = (silu * u).astype(BF16)

        @pl.when((j == N_DEV - 1) & (i == ni - 1))
        def _():
            gather.wait_sibling(fulls, sems, down)
            for other_chip in range(3):
                gather.wait_passed(fulls, sems, down, other_chip)
            for a in range(n_g):
                gather.wait_sent(shards, fulls, sems, a)

    shp = jax.ShapeDtypeStruct((N_DEV, s, fs), BF16)
    o_spec = pl.BlockSpec((None, tm, fs), lambda j, i: (_arrival_block(j), i, 0))
    ins = ([] if gate_here else [gate]) + gather.inputs
    outs = pl.pallas_call(
        body, name=name, grid=(N_DEV, ni), in_specs=[pl.BlockSpec((tm, d), lambda j, i: (i, 0))] + [ANY] * len(ins),
        out_specs=[o_spec, o_spec, o_spec] + [ANY] * n_g, out_shape=[shp, shp, shp] + gather.out_shapes,
        scratch_shapes=[pltpu.VMEM((2, 2, fs, d), BF16), pltpu.SemaphoreType.DMA((2, 2))] + gather.scratch,
        compiler_params=pltpu.CompilerParams(dimension_semantics=("arbitrary", "arbitrary"), vmem_limit_bytes=V7X_VMEM_LIMIT),
    )(xb, *ins)
    return outs[:3], outs[3:]


def _down_ln(a3, w3, bias, res, res_affine, ln_g, ln_b, scale, name, target=None, rider=None):
    nk, s, tk = a3.shape
    d = w3.shape[2]
    tm = min(s, 256)
    final = target is not None

    def body(*refs):
        refs = list(refs)
        a_ref, w_hbm = refs[:2]
        del refs[:2]
        bias_ref = refs.pop(0) if bias is not None else None
        res_ref = refs.pop(0)
        rg_ref, rb_ref = (refs.pop(0), refs.pop(0)) if res_affine is not None else (None, None)
        g_ref, b_ref = refs.pop(0), refs.pop(0)
        t_ref = refs.pop(0) if final else None
        w_sem = refs.pop()
        w_ref = refs.pop()
        i = pl.program_id(0)
        if final:
            dr_ref, drb_ref, sq_ref, dg_ref, db_ref = refs
        else:
            xh_ref, hb_ref, rstd_ref = refs

        @pl.when(i == 0)
        def _():
            whole = pltpu.make_async_copy(w_hbm, w_ref, w_sem.at[0])
            whole.start()
            whole.wait()
            if final:
                sq_ref[...] = jnp.zeros_like(sq_ref)
                dg_ref[...] = jnp.zeros_like(dg_ref)
                db_ref[...] = jnp.zeros_like(db_ref)

        y = jnp.dot(a_ref[0], w_ref[0], preferred_element_type=F32)
        for k in range(1, nk):
            y = y + jnp.dot(a_ref[k], w_ref[k], preferred_element_type=F32)
        if bias_ref is not None:
            y = y + bias_ref[...]
        for rows in _row_blocks(tm):
            r = res_ref[rows, :]
            if rg_ref is not None:
                r = r * rg_ref[...] + rb_ref[...]
            r = ALPHA * r + scale * y[rows]
            mu = jnp.mean(r, axis=-1, keepdims=True)
            c = r - mu
            var = jnp.mean(c * c, axis=-1, keepdims=True)
            rstd = lax.rsqrt(var + LN_EPS)
            xhat = c * rstd
            h = xhat * g_ref[...] + b_ref[...]
            if not final:
                xh_ref[rows, :] = xhat
                hb_ref[rows, :] = h.astype(BF16)
                rstd_ref[rows, :] = rstd
            else:
                err = h - t_ref[rows, :]
                sq_ref[...] += jnp.sum(err * err, axis=0, keepdims=True)
                dh = err * (1.0 / d)
                dg_ref[...] += jnp.sum(dh * xhat, axis=0, keepdims=True)
                db_ref[...] += jnp.sum(dh, axis=0, keepdims=True)
                dr = _ln_backward(dh, xhat, rstd, g_ref[...])
                dr_ref[rows, :] = dr
                drb_ref[rows, :] = (scale * dr).astype(BF16)

    tok = pl.BlockSpec((tm, d), lambda i: (i, 0))
    vec = pl.BlockSpec((1, d), lambda i: (0, 0))
    ins = [a3, w3]
    in_specs = [pl.BlockSpec((nk, tm, tk), lambda i: (0, i, 0)), ANY]
    if bias is not None:
        ins.append(bias)
        in_specs.append(vec)
    ins.append(res)
    in_specs.append(tok)
    if res_affine is not None:
        ins += list(res_affine)
        in_specs += [vec, vec]
    ins += [ln_g, ln_b]
    in_specs += [vec, vec]
    if final:
        ins.append(target)
        in_specs.append(tok)
        out_shape = [jax.ShapeDtypeStruct((s, d), F32), jax.ShapeDtypeStruct((s, d), BF16)] + [jax.ShapeDtypeStruct((1, d), F32)] * 3
        out_specs = [tok, tok, vec, vec, vec]
    else:
        out_shape = [jax.ShapeDtypeStruct((s, d), F32), jax.ShapeDtypeStruct((s, d), BF16), jax.ShapeDtypeStruct((s, 1), F32)]
        out_specs = [tok, tok, pl.BlockSpec((tm, 1), lambda i: (i, 0))]
    scratch = [pltpu.VMEM((nk, tk, d), BF16), pltpu.SemaphoreType.DMA((1,))]
    return _call(body, name, (s // tm,), in_specs, out_specs, out_shape, scratch, ins, rider)


def _proj_in(hb, w, bias, name, rider=None):
    s, d = hb.shape
    n = w.shape[1]
    tm = min(s, 1024)
    tn = PROJ_IN_COLS if n % PROJ_IN_COLS == 0 else n

    def body(h_ref, w_ref, b_ref, z_ref):
        z_ref[...] = (jnp.dot(h_ref[...], w_ref[...], preferred_element_type=F32) + b_ref[...]).astype(BF16)

    in_specs = [pl.BlockSpec((tm, d), lambda i, j: (i, 0)), pl.BlockSpec((d, tn), lambda i, j: (0, j)),
                pl.BlockSpec((1, tn), lambda i, j: (0, j))]
    return _call(body, name, (s // tm, n // tn), in_specs, pl.BlockSpec((tm, tn), lambda i, j: (i, j)),
                 jax.ShapeDtypeStruct((s, n), BF16), [], [hb, w, bias], rider)


def _nt_hidden(ab, w3, name):
    s, kdim = ab.shape
    nj, tn, _ = w3.shape
    tm = min(s, 1024)

    def body(a_ref, w_ref, o_ref):
        o_ref[...] = lax.dot_general(a_ref[...], w_ref[...], NT_DIMS, preferred_element_type=F32).astype(BF16)

    in_specs = [pl.BlockSpec((tm, kdim), lambda i, j: (i, 0)), pl.BlockSpec((None, tn, kdim), lambda i, j: (j, 0, 0))]
    return _call(body, name, (s // tm, nj), in_specs, pl.BlockSpec((None, tm, tn), lambda i, j: (j, i, 0)),
                 jax.ShapeDtypeStruct((nj, s, tn), BF16), [], [ab, w3])[0]


def _ffn_bwd_hidden(ab, w3, silu3, udsilu3, name, rider=None):
    s, kdim = ab.shape
    nj, tn, _ = w3.shape
    tm = min(s, 1024)

    def body(a_ref, w_ref, silu_ref, udsilu_ref, dg_ref, du_ref):
        a = a_ref[...]
        for c0 in range(0, tn, MXU_COLS):
            cols = slice(c0, min(c0 + MXU_COLS, tn))
            t = lax.dot_general(a, w_ref[cols, :], NT_DIMS, preferred_element_type=F32)
            du_ref[:, cols] = (t * silu_ref[:, cols].astype(F32)).astype(BF16)
            dg_ref[:, cols] = (t * udsilu_ref[:, cols].astype(F32)).astype(BF16)

    hid = pl.BlockSpec((None, tm, tn), lambda i, j: (j, i, 0))
    shp = jax.ShapeDtypeStruct((nj, s, tn), BF16)
    in_specs = [pl.BlockSpec((tm, kdim), lambda i, j: (i, 0)), pl.BlockSpec((None, tn, kdim), lambda i, j: (j, 0, 0)), hid, hid]
    return _call(body, name, (s // tm, nj), in_specs, [hid, hid], [shp, shp], [], [ab, w3, silu3, udsilu3], rider)


def _tn_dw(a, a_spec, b, b_spec, nj, m, n, s, tk, name, rider):
    def body(a_ref, b_ref, o_ref, acc_ref):
        k = pl.program_id(1)

        @pl.when(k == 0)
        def _():
            acc_ref[...] = jnp.zeros_like(acc_ref)

        acc_ref[...] += lax.dot_general(a_ref[...], b_ref[...], TN_DIMS, preferred_element_type=F32)

        @pl.when(k == s // tk - 1)
        def _():
            o_ref[...] = acc_ref[...].astype(BF16)

    return _call(body, name, (nj, s // tk), [a_spec, b_spec], pl.BlockSpec((None, m, n), lambda j, k: (j, 0, 0)),
                 jax.ShapeDtypeStruct((nj, m, n), BF16), [pltpu.VMEM((m, n), F32)], [a, b], rider)


def _dw_hidden_rows(hid3, db, name, rider=None):
    nj, s, fs = hid3.shape
    d = db.shape[1]
    tk = min(s, DW_TOKENS)
    return _tn_dw(hid3, pl.BlockSpec((None, tk, fs), lambda j, k: (j, k, 0)), db, pl.BlockSpec((tk, d), lambda j, k: (k, 0)),
                  nj, fs, d, s, tk, name, rider)


def _dw_hidden_rows_paired(hid3, db, name, rider=None):
    nj, s, fs = hid3.shape
    d = db.shape[1]
    tk = min(s, DW_TOKENS)
    nk = s // tk
    half = nj // 2

    def device_of(j):
        c = lax.axis_index("c")
        return 2 * (j % half) + jnp.where(j < half, 1 - c, c)

    def body(a_ref, b_ref, o_ref, theirs_ref, acc_ref, stage_ref, got_ref, send_sems, recv_sems, load_sem):
        j, k = pl.program_id(0), pl.program_id(1)
        x, y, c = lax.axis_index("x"), lax.axis_index("y"), lax.axis_index("c")

        def to_sibling(q):
            return pltpu.make_async_remote_copy(src_ref=stage_ref, dst_ref=theirs_ref.at[q], send_sem=send_sems.at[q],
                                                recv_sem=recv_sems.at[q], device_id=(x, y, 1 - c), device_id_type=MESH)

        def fetch(q):
            return pltpu.make_async_copy(theirs_ref.at[q], got_ref, load_sem.at[0])

        @pl.when(k == 0)
        def _():
            acc_ref[...] = jnp.zeros_like(acc_ref)

        acc_ref[...] += lax.dot_general(a_ref[...], b_ref[...], TN_DIMS, preferred_element_type=F32)

        for q in range(half):
            @pl.when((j == q) & (k == nk - 1))
            def _(q=q):
                if q > 0:
                    to_sibling(q - 1).wait_send()
                stage_ref[...] = acc_ref[...].astype(BF16)
                to_sibling(q).start()

            @pl.when((j == half + q) & (k == 0))
            def _(q=q):
                to_sibling(q).wait_recv()
                fetch(q).start()

            @pl.when((j == half + q) & (k == nk - 1))
            def _(q=q):
                if q == 0:
                    to_sibling(half - 1).wait_send()
                fetch(q).wait()
                o_ref[...] = (acc_ref[...] + got_ref[...].astype(F32)).astype(BF16)

    in_specs = [pl.BlockSpec((None, tk, fs), lambda j, k: (device_of(j), k, 0)), pl.BlockSpec((tk, d), lambda j, k: (k, 0))]
    out_specs = [pl.BlockSpec((None, fs, d), lambda j, k: (jnp.maximum(j - half, 0), 0, 0)), ANY]
    shp = jax.ShapeDtypeStruct((half, fs, d), BF16)
    scratch = [pltpu.VMEM((fs, d), F32), pltpu.VMEM((fs, d), BF16), pltpu.VMEM((fs, d), BF16),
               pltpu.SemaphoreType.DMA((half,)), pltpu.SemaphoreType.DMA((half,)), pltpu.SemaphoreType.DMA((1,))]
    (sums, _), riders_out = _call(body, name, (nj, nk), in_specs, out_specs, [shp, shp], scratch, [hid3, db], rider)
    return sums, riders_out


def _dw_cols(ab, dz, name, rider=None):
    s, d = ab.shape
    n = dz.shape[1]
    tn = PROJ_IN_COLS if n % PROJ_IN_COLS == 0 else n
    tk = min(s, DW_TOKENS)
    tr = d // 2

    def body(a_ref, b_ref, o_ref, acc_ref):
        k = pl.program_id(2)

        @pl.when(k == 0)
        def _():
            acc_ref[...] = jnp.zeros_like(acc_ref)

        acc_ref[...] += lax.dot_general(a_ref[...], b_ref[...], TN_DIMS, preferred_element_type=F32)

        @pl.when(k == s // tk - 1)
        def _():
            o_ref[...] = acc_ref[...].astype(BF16)

    in_specs = [pl.BlockSpec((tk, tr), lambda j, r, k: (k, r)), pl.BlockSpec((tk, tn), lambda j, r, k: (k, j))]
    return _call(body, name, (n // tn, d // tr, s // tk), in_specs, pl.BlockSpec((tr, tn), lambda j, r, k: (r, j)),
                 jax.ShapeDtypeStruct((d, n), BF16), [pltpu.VMEM((tr, tn), F32)], [ab, dz], rider)


def _ffn_bwd_input(dg3, wg3, du3, wu3, dres, ln, name, rider=None):
    s, d = dres.shape
    nk, _, fs = dg3.shape
    tm = min(s, 512)

    def body(*refs):
        refs = list(refs)
        dg_in, wg_ref, du_in, wu_ref, dres_ref = refs[:5]
        del refs[:5]
        if ln is not None:
            xh_ref, rstd_ref, gain_ref = refs.pop(0), refs.pop(0), refs.pop(0)
        acc_ref = refs.pop()
        i, k = pl.program_id(0), pl.program_id(1)

        @pl.when(k == 0)
        def _():
            acc_ref[...] = jnp.zeros_like(acc_ref)

        acc_ref[...] += (jnp.dot(dg_in[...], wg_ref[...], preferred_element_type=F32)
                         + jnp.dot(du_in[...], wu_ref[...], preferred_element_type=F32))

        @pl.when(k == nk - 1)
        def _():
            if ln is not None:
                dr_ref, drb_ref, dg_ref, db_ref, sum_ref = refs

                @pl.when(i == 0)
                def _():
                    dg_ref[...] = jnp.zeros_like(dg_ref)
                    db_ref[...] = jnp.zeros_like(db_ref)
                    sum_ref[...] = jnp.zeros_like(sum_ref)

            for rows in _row_blocks(tm):
                dh = ALPHA * dres_ref[rows, :] + acc_ref[rows, :]
                if ln is None:
                    refs[0][rows, :] = dh
                else:
                    xhat = xh_ref[rows, :]
                    dg_ref[...] += jnp.sum(dh * xhat, axis=0, keepdims=True)
                    db_ref[...] += jnp.sum(dh, axis=0, keepdims=True)
                    dr = _ln_backward(dh, xhat, rstd_ref[rows, :], gain_ref[...])
                    sum_ref[...] += jnp.sum(dr, axis=0, keepdims=True)
                    dr_ref[rows, :] = dr
                    drb_ref[rows, :] = dr.astype(BF16)

    tok = pl.BlockSpec((tm, d), lambda i, k: (i, 0))
    vec = pl.BlockSpec((1, d), lambda i, k: (0, 0))
    a_spec = pl.BlockSpec((None, tm, fs), lambda i, k: (k, i, 0))
    w_spec = pl.BlockSpec((None, fs, d), lambda i, k: (k, 0, 0))
    ins, in_specs = [dg3, wg3, du3, wu3, dres], [a_spec, w_spec, a_spec, w_spec, tok]
    if ln is None:
        out_shape, out_specs = jax.ShapeDtypeStruct((s, d), F32), tok
    else:
        ins += list(ln)
        in_specs += [tok, pl.BlockSpec((tm, 1), lambda i, k: (i, 0)), vec]
        out_shape = [jax.ShapeDtypeStruct((s, d), F32), jax.ShapeDtypeStruct((s, d), BF16)] + [jax.ShapeDtypeStruct((1, d), F32)] * 3
        out_specs = [tok, tok, vec, vec, vec]
    return _call(body, name, (s // tm, nk), in_specs, out_specs, out_shape, [pltpu.VMEM((tm, d), F32)], ins, rider)


def _proj_in_bwd_ln(dz, w, dres, ln, branch_scale, name):
    s, d = dres.shape
    n = w.shape[1]
    tm = min(s, 256)

    def body(dz_ref, w_hbm, dres_ref, xh_ref, rstd_ref, gain_ref, dr_ref, drb_ref, dg_ref, db_ref, w_ref, w_sem):
        @pl.when(pl.program_id(0) == 0)
        def _():
            whole = pltpu.make_async_copy(w_hbm, w_ref, w_sem.at[0])
            whole.start()
            whole.wait()
            dg_ref[...] = jnp.zeros_like(dg_ref)
            db_ref[...] = jnp.zeros_like(db_ref)

        acc = lax.dot_general(dz_ref[...], w_ref[...], NT_DIMS, preferred_element_type=F32)
        for rows in _row_blocks(tm):
            dh = ALPHA * dres_ref[rows, :] + acc[rows]
            xhat = xh_ref[rows, :]
            dg_ref[...] += jnp.sum(dh * xhat, axis=0, keepdims=True)
            db_ref[...] += jnp.sum(dh, axis=0, keepdims=True)
            dr = _ln_backward(dh, xhat, rstd_ref[rows, :], gain_ref[...])
            dr_ref[rows, :] = dr
            drb_ref[rows, :] = (branch_scale * dr).astype(BF16)

    tok = pl.BlockSpec((tm, d), lambda i: (i, 0))
    vec = pl.BlockSpec((1, d), lambda i: (0, 0))
    in_specs = [pl.BlockSpec((tm, n), lambda i: (i, 0)), ANY, tok, tok, pl.BlockSpec((tm, 1), lambda i: (i, 0)), vec]
    out_shape = [jax.ShapeDtypeStruct((s, d), F32), jax.ShapeDtypeStruct((s, d), BF16)] + [jax.ShapeDtypeStruct((1, d), F32)] * 2
    scratch = [pltpu.VMEM(w.shape, BF16), pltpu.SemaphoreType.DMA((1,))]
    return _call(body, name, (s // tm,), in_specs, [tok, tok, vec, vec], out_shape, scratch, [dz, w, dres] + list(ln))[0]


def _shift_rows_down(v, halo, k, row):
    out = pltpu.roll(v, k, 0)
    hr = halo.shape[0]
    for r in range(k):
        out = jnp.where(row == r, halo[hr - k + r:hr - k + r + 1, :], out)
    return out


def _shift_rows_up(v, halo, k, row):
    t = v.shape[0]
    out = pltpu.roll(v, t - k, 0)
    for r in range(k):
        out = jnp.where(row == t - k + r, halo[r:r + 1, :], out)
    return out


def _sgu_head_forward(z_ref, h, da, gv_ref, bv_ref):
    zu = z_ref[:, h * HEAD:(h + 1) * HEAD].astype(F32)
    zv = z_ref[:, da + h * HEAD:da + (h + 1) * HEAD].astype(F32)
    tu, tv = _gelu_tanh(zu), _gelu_tanh(zv)
    u = _gelu(zu, tu)
    v = _gelu(zv, tv)
    mu = jnp.mean(v, axis=-1, keepdims=True)
    c = v - mu
    rstd = lax.rsqrt(jnp.mean(c * c, axis=-1, keepdims=True) + LN_EPS)
    vhat = c * rstd
    vln = (vhat * gv_ref[h:h + 1, :] + bv_ref[h:h + 1, :]).astype(BF16)
    return (zu, tu), (zv, tv), u, vhat, rstd, vln


def _mixer_fwd(z, ws_masked, bs_wide, gv, bv, cw, name):
    s, zc = z.shape
    da = zc // 5
    nh = da // HEAD
    tm = min(s, MIXER_TOKENS)
    hb = tm // BF16_ROWS

    def body(z_ref, pc_ref, px_ref, ws_ref, bs_ref, gv_ref, bv_ref, cw_ref, y_ref):
        i = pl.program_id(0)
        for h in range(nh):
            _, _, u, _, _, vln = _sgu_head_forward(z_ref, h, da, gv_ref, bv_ref)
            for n in range(tm // CHUNK):
                rows = slice(n * CHUNK, (n + 1) * CHUNK)
                mixed = jnp.dot(ws_ref[h], vln[rows], preferred_element_type=F32) + bs_ref[h]
                y_ref[0, rows, h * HEAD:(h + 1) * HEAD] = (u[rows] * mixed).astype(BF16)
        gate_b = z_ref[:, 2 * da:3 * da].astype(F32)
        hc = z_ref[:, 3 * da:4 * da].astype(F32) * z_ref[:, 4 * da:5 * da].astype(F32)
        halo = jnp.where(i > 0, pc_ref[...].astype(F32) * px_ref[...].astype(F32), 0.0)
        row = lax.broadcasted_iota(jnp.int32, (tm, da), 0)
        y = cw_ref[0:1, :] * _shift_rows_down(hc, halo, 2, row) + cw_ref[1:2, :] * _shift_rows_down(hc, halo, 1, row) + cw_ref[2:3, :] * hc
        y_ref[1] = (gate_b * y).astype(BF16)

    prev = lambda col: pl.BlockSpec((BF16_ROWS, da), lambda i: (jnp.maximum(i * hb - 1, 0), col))
    in_specs = [pl.BlockSpec((tm, zc), lambda i: (i, 0)), prev(3), prev(4), _row((nh, CHUNK, CHUNK)), _row((nh, CHUNK, HEAD)),
                _row((nh, HEAD)), _row((nh, HEAD)), _row((CONV_TAPS, da))]
    return _call(body, name, (s // tm,), in_specs, pl.BlockSpec((2, tm, da), lambda i: (0, i, 0)),
                 jax.ShapeDtypeStruct((2, s, da), BF16), [], [z, z, z, ws_masked, bs_wide, gv, bv, cw])[0]


def _mixer_bwd(z, dy, ws_masked, bs_wide, gv, bv, cw, name, rider=None):
    s, zc = z.shape
    da = zc // 5
    nh = da // HEAD
    tm = min(s, MIXER_TOKENS)
    hb = tm // BF16_ROWS
    nblk = s // tm

    def body(z_ref, pc_ref, px_ref, nb_ref, dy_ref, ndy_ref, ws_ref, bs_ref, gv_ref, bv_ref, cw_ref,
             dz_ref, dws_ref, dbs_ref, dgv_ref, dbv_ref, dcw_ref, dbin_ref):
        i = pl.program_id(0)

        @pl.when(i == 0)
        def _():
            for ref in (dws_ref, dbs_ref, dgv_ref, dbv_ref, dcw_ref, dbin_ref):
                ref[...] = jnp.zeros_like(ref)

        causal = lax.broadcasted_iota(jnp.int32, (CHUNK, CHUNK), 0) >= lax.broadcasted_iota(jnp.int32, (CHUNK, CHUNK), 1)
        for h in range(nh):
            zu, zv, u, vhat, rstd, vln = _sgu_head_forward(z_ref, h, da, gv_ref, bv_ref)
            dya = dy_ref[0, :, h * HEAD:(h + 1) * HEAD].astype(F32)
            w = ws_ref[h]
            du_parts, dvln_parts = [], []
            for n in range(tm // CHUNK):
                rows = slice(n * CHUNK, (n + 1) * CHUNK)
                mixed = jnp.dot(w, vln[rows], preferred_element_type=F32) + bs_ref[h]
                du_parts.append(dya[rows] * mixed)
                dmix = dya[rows] * u[rows]
                dmix_b = dmix.astype(BF16)
                dws_ref[h] += jnp.where(causal, lax.dot_general(dmix_b, vln[rows], NT_DIMS, preferred_element_type=F32), 0.0)
                dbs_ref[h] += dmix
                dvln_parts.append(lax.dot_general(w, dmix_b, TN_DIMS, preferred_element_type=F32))
            du = jnp.concatenate(du_parts, axis=0)
            dvln = jnp.concatenate(dvln_parts, axis=0)
            dgv_ref[h:h + 1, :] += jnp.sum(dvln * vhat, axis=0, keepdims=True)
            dbv_ref[h:h + 1, :] += jnp.sum(dvln, axis=0, keepdims=True)
            dv = _ln_backward(dvln, vhat, rstd, gv_ref[h:h + 1, :])
            dzu = du * _gelu_grad(*zu)
            dzv = dv * _gelu_grad(*zv)
            ucols = slice(h * HEAD, (h + 1) * HEAD)
            vcols = slice(da + h * HEAD, da + (h + 1) * HEAD)
            dz_ref[:, ucols] = dzu.astype(BF16)
            dz_ref[:, vcols] = dzv.astype(BF16)
            dbin_ref[:, ucols] += jnp.sum(dzu, axis=0, keepdims=True)
            dbin_ref[:, vcols] += jnp.sum(dzv, axis=0, keepdims=True)

        gate_b = z_ref[:, 2 * da:3 * da].astype(F32)
        gate_c = z_ref[:, 3 * da:4 * da].astype(F32)
        xt = z_ref[:, 4 * da:5 * da].astype(F32)
        hc = gate_c * xt
        halo = jnp.where(i > 0, pc_ref[...].astype(F32) * px_ref[...].astype(F32), 0.0)
        row = lax.broadcasted_iota(jnp.int32, (tm, da), 0)
        sh1 = _shift_rows_down(hc, halo, 1, row)
        sh2 = _shift_rows_down(hc, halo, 2, row)
        y = cw_ref[0:1, :] * sh2 + cw_ref[1:2, :] * sh1 + cw_ref[2:3, :] * hc
        dyb = dy_ref[1].astype(F32)
        dconv = dyb * gate_b
        nhalo = jnp.where(i < nblk - 1, ndy_ref[...].astype(F32) * nb_ref[...].astype(F32), 0.0)
        dhc = cw_ref[2:3, :] * dconv + cw_ref[1:2, :] * _shift_rows_up(dconv, nhalo, 1, row) + cw_ref[0:1, :] * _shift_rows_up(dconv, nhalo, 2, row)
        dcw_ref[0:1, :] += jnp.sum(dconv * sh2, axis=0, keepdims=True)
        dcw_ref[1:2, :] += jnp.sum(dconv * sh1, axis=0, keepdims=True)
        dcw_ref[2:3, :] += jnp.sum(dconv * hc, axis=0, keepdims=True)
        for col, val in ((2, dyb * y), (3, dhc * xt), (4, dhc * gate_c)):
            cols = slice(col * da, (col + 1) * da)
            dz_ref[:, cols] = val.astype(BF16)
            dbin_ref[:, cols] += jnp.sum(val, axis=0, keepdims=True)

        @pl.when(i == nblk - 1)
        def _():
            for h in range(nh):
                dbs_ref[h] = jnp.broadcast_to(jnp.sum(dbs_ref[h], axis=1, keepdims=True), (CHUNK, HEAD))

    prev = lambda col: pl.BlockSpec((BF16_ROWS, da), lambda i: (jnp.maximum(i * hb - 1, 0), col))
    nxt = lambda i: jnp.minimum((i + 1) * hb, s // BF16_ROWS - 1)
    in_specs = [pl.BlockSpec((tm, zc), lambda i: (i, 0)), prev(3), prev(4), pl.BlockSpec((BF16_ROWS, da), lambda i: (nxt(i), 2)),
                pl.BlockSpec((2, tm, da), lambda i: (0, i, 0)), pl.BlockSpec((None, BF16_ROWS, da), lambda i: (1, nxt(i), 0)),
                _row((nh, CHUNK, CHUNK)), _row((nh, CHUNK, HEAD)), _row((nh, HEAD)), _row((nh, HEAD)), _row((CONV_TAPS, da))]
    out_specs = [pl.BlockSpec((tm, zc), lambda i: (i, 0)), _row((nh, CHUNK, CHUNK)), _row((nh, CHUNK, HEAD)), _row((nh, HEAD)),
                 _row((nh, HEAD)), _row((8, da)), _row((1, zc))]
    out_shape = [jax.ShapeDtypeStruct((s, zc), BF16), jax.ShapeDtypeStruct((nh, CHUNK, CHUNK), F32),
                 jax.ShapeDtypeStruct((nh, CHUNK, HEAD), F32), jax.ShapeDtypeStruct((nh, HEAD), F32),
                 jax.ShapeDtypeStruct((nh, HEAD), F32), jax.ShapeDtypeStruct((8, da), F32), jax.ShapeDtypeStruct((1, zc), F32)]
    return _call(body, name, (nblk,), in_specs, out_specs, out_shape, [], [z, z, z, z, dy, dy, ws_masked, bs_wide, gv, bv, cw], rider)


def _adam_update(g, w, m, v):
    m_new = ADAM_B1 * m + (1.0 - ADAM_B1) * g
    v_new = ADAM_B2 * v + (1.0 - ADAM_B2) * (g * g)
    m_hat = m_new / (1.0 - ADAM_B1 ** ADAM_STEP)
    v_hat = v_new / (1.0 - ADAM_B2 ** ADAM_STEP)
    return -ADAM_LR * (m_hat / (jnp.sqrt(v_hat) + ADAM_EPS) + ADAM_WD * w), m_new, v_new


def _adamw(gparts, w, m, v, name):
    n, r, c = gparts.shape
    tr = r // 4 if (r // 4) % BF16_ROWS == 0 else r

    def body(g_ref, w_ref, m_ref, v_ref, go_ref, d_ref, mo_ref, vo_ref):
        g = g_ref[0].astype(F32)
        for q in range(1, n):
            g = g + g_ref[q].astype(F32)
        go_ref[...] = g
        d_ref[...], mo_ref[...], vo_ref[...] = _adam_update(g, w_ref[...], m_ref[...], v_ref[...])

    blk = pl.BlockSpec((tr, c), lambda i: (i, 0))
    shp = jax.ShapeDtypeStruct((r, c), F32)
    return _call(body, name, (r // tr,), [pl.BlockSpec((n, tr, c), lambda i: (0, i, 0)), blk, blk, blk], [blk] * 4, [shp] * 4, [],
                 [gparts, w, m, v])[0]


def _adamw_small(packs, rows, w, m, v, conv, name):
    n_par, n_dev = len(rows), packs.shape[0]
    taps = conv[0].shape[0]

    def body(*refs):
        refs = list(refs)
        cut = lambda n: [refs.pop(0) for _ in range(n)]
        p_ref, w_refs, m_refs, v_refs, (cw_ref, cm_ref, cv_ref) = refs.pop(0), cut(n_par), cut(n_par), cut(n_par), cut(3)
        outs = [cut(4) for _ in range(n_par + 1)]
        at = 0
        for k in range(n_par):
            g = p_ref[0, at:at + rows[k], :]
            for dev in range(1, n_dev):
                g = g + p_ref[dev, at:at + rows[k], :]
            go_ref, d_ref, mo_ref, vo_ref = outs[k]
            go_ref[...] = g
            d_ref[...], mo_ref[...], vo_ref[...] = _adam_update(g, w_refs[k][...], m_refs[k][...], v_refs[k][...])
            at += rows[k]
        me = 4 * lax.axis_index("x") + 2 * lax.axis_index("y") + lax.axis_index("c")
        go_ref, d_ref, mo_ref, vo_ref = outs[n_par]
        for tap in range(taps):
            row = pl.ds(at + tap * n_dev + me, 1)
            g = p_ref[0, row, :]
            for dev in range(1, n_dev):
                g = g + p_ref[dev, row, :]
            one = slice(tap, tap + 1)
            go_ref[one, :] = g
            d_ref[one, :], mo_ref[one, :], vo_ref[one, :] = _adam_update(g, cw_ref[one, :], cm_ref[one, :], cv_ref[one, :])

    vmem = pl.BlockSpec(memory_space=pltpu.VMEM)
    ins = [packs] + list(w) + list(m) + list(v) + list(conv)
    out_shape = [jax.ShapeDtypeStruct(a.shape, F32) for a in list(w) + [conv[0]] for _ in range(4)]
    outs = pl.pallas_call(body, name=name, in_specs=[vmem] * len(ins), out_specs=[vmem] * len(out_shape), out_shape=out_shape,
                          compiler_params=pltpu.CompilerParams(vmem_limit_bytes=V7X_VMEM_LIMIT))(*ins)
    return [outs[4 * k:4 * k + 4] for k in range(n_par + 1)]


def _rows128(a):
    return a.reshape(-1, LANES)


def kernel(x, ffa_gate, ffa_up, ffa_down, ln_a_g, ln_a_b, w_in, b_in, w_s, b_s, ln_v_g, ln_v_b, conv_w, w_out, b_out, ln_m_g, ln_m_b, ffc_gate, ffc_up, ffc_down, ln_c_g, ln_c_b, loss_target, m_ffa_gate, m_ffa_up, m_ffa_down, m_ln_a_g, m_ln_a_b, m_w_in, m_b_in, m_w_s, m_b_s, m_ln_v_g, m_ln_v_b, m_conv_w, m_w_out, m_b_out, m_ln_m_g, m_ln_m_b, m_ffc_gate, m_ffc_up, m_ffc_down, m_ln_c_g, m_ln_c_b, v_ffa_gate, v_ffa_up, v_ffa_down, v_ln_a_g, v_ln_a_b, v_w_in, v_b_in, v_w_s, v_b_s, v_ln_v_g, v_ln_v_b, v_conv_w, v_w_out, v_b_out, v_ln_m_g, v_ln_m_b, v_ffc_gate, v_ffc_up, v_ffc_down, v_ln_c_g, v_ln_c_b):
    x2, target = x[0], loss_target[0]
    s, d = x2.shape
    da = d // 2
    nh = da // HEAD

    big = dict(ffa_gate=ffa_gate, ffa_up=ffa_up, ffa_down=ffa_down, w_in=w_in, w_out=w_out, ffc_gate=ffc_gate, ffc_up=ffc_up, ffc_down=ffc_down)
    big_m = dict(ffa_gate=m_ffa_gate, ffa_up=m_ffa_up, ffa_down=m_ffa_down, w_in=m_w_in, w_out=m_w_out, ffc_gate=m_ffc_gate, ffc_up=m_ffc_up, ffc_down=m_ffc_down)
    big_v = dict(ffa_gate=v_ffa_gate, ffa_up=v_ffa_up, ffa_down=v_ffa_down, w_in=v_w_in, w_out=v_w_out, ffc_gate=v_ffc_gate, ffc_up=v_ffc_up, ffc_down=v_ffc_down)
    local = lambda k, a: jnp.transpose(a[0]) if k in TRANSPOSED else a[0]
    shard = {k: local(k, w).astype(BF16) for k, w in big.items()}
    conv_rows = jnp.pad(conv_w[0], ((0, 8 - CONV_TAPS), (0, 0)))

    tril = jnp.tril(jnp.ones((CHUNK, CHUNK), dtype=bool))
    ws_masked = jnp.where(tril[None], w_s[0], 0.0).astype(BF16)
    bs_wide = jnp.broadcast_to(b_s[0][:, :, None], (nh, CHUNK, HEAD))
    gv, bv = ln_v_g.reshape(nh, HEAD), ln_v_b.reshape(nh, HEAD)

    full = {}
    xb = x2.astype(BF16)
    (silu_a, udsilu_a, hid_a), (full["ffa_gate"], full["ffa_up"], full["ffa_down"]) = _ffn_gateup_gathering(
        xb, shard["ffa_gate"], shard["ffa_up"], shard["ffa_down"], "ffa_gateup")
    (xhat1, h1b, rstd1), (full["w_in"], full["w_out"], conv_full) = _down_ln(
        hid_a, full["ffa_down"], None, x2, None, ln_a_g, ln_a_b, 0.5, "ffa_down_ln", rider=_Gather([shard["w_in"], shard["w_out"], conv_rows], by_columns=(0,)))
    cw = jnp.transpose(conv_full[:, :CONV_TAPS, :], (1, 0, 2)).reshape(CONV_TAPS, da)
    w_out2 = full["w_out"].reshape(2, da, d)
    z, (full["ffc_gate"],) = _proj_in(h1b, full["w_in"], b_in, "proj_in", _Gather([shard["ffc_gate"]]))
    ycat = _mixer_fwd(z, ws_masked, bs_wide, gv, bv, cw, "mixer_fwd")
    (xhat2, h2b, rstd2), _ = _down_ln(ycat, w_out2, b_out, xhat1, (ln_a_g, ln_a_b), ln_m_g, ln_m_b, 1.0, "proj_out_ln")
    (silu_c, udsilu_c, hid_c), (full["ffc_up"], full["ffc_down"]) = _ffn_gateup_gathering(
        h2b, full["ffc_gate"], shard["ffc_up"], shard["ffc_down"], "ffc_gateup")
    (dr3, dr3b, sq_err, d_ln_c_g, d_ln_c_b), _ = _down_ln(hid_c, full["ffc_down"], None, xhat2, (ln_m_g, ln_m_b), ln_c_g, ln_c_b, 0.5,
                                                          "ffc_down_ln_loss", target=target)
    loss = lax.psum((0.5 / d) * jnp.sum(sq_err), ("x", "y", "c"))

    landed = {}
    (dg_c, du_c), _ = _ffn_bwd_hidden(dr3b, full["ffc_down"], silu_c, udsilu_c, "ffc_bwd_hidden")
    part, _ = _dw_hidden_rows_paired(hid_c, dr3b, "ffc_dw_down")
    part, (landed["ffc_down"],) = _dw_hidden_rows_paired(dg_c, h2b, "ffc_dw_gate", _ChipScatter(part))
    part, (landed["ffc_gate"],) = _dw_hidden_rows_paired(du_c, h2b, "ffc_dw_up", _ChipScatter(part))
    (dr2, dr2b, d_ln_m_g, d_ln_m_b, d_b_out), (landed["ffc_up"],) = _ffn_bwd_input(
        dg_c, full["ffc_gate"], du_c, full["ffc_up"], dr3, (xhat2, rstd2, ln_m_g), "ffc_bwd_input_ln", _ChipScatter(part))
    dycat = _nt_hidden(dr2b, w_out2, "proj_out_bwd")
    part, _ = _dw_hidden_rows(ycat, dr2b, "proj_out_dw")
    (dz, d_w_s, d_b_s_wide, d_gv, d_bv, d_cw, d_b_in), (landed["w_out"],) = _mixer_bwd(
        z, dycat, ws_masked, bs_wide, gv, bv, cw, "mixer_bwd", _Scatter(part.reshape(N_DEV, d // N_DEV, d)))
    dr1, dr1b, d_ln_a_g, d_ln_a_b = _proj_in_bwd_ln(dz, full["w_in"], dr2, (xhat1, rstd1, ln_a_g), 0.5, "proj_in_bwd_ln")
    small_g = dict(ln_a_g=d_ln_a_g, ln_a_b=d_ln_a_b, b_in=d_b_in, w_s=d_w_s, b_s=d_b_s_wide[:, :, 0], ln_v_g=d_gv, ln_v_b=d_bv, b_out=d_b_out,
                   ln_m_g=d_ln_m_g, ln_m_b=d_ln_m_b, ln_c_g=d_ln_c_g, ln_c_b=d_ln_c_b)
    pack = jnp.concatenate([_rows128(g) for g in small_g.values()] + [_rows128(d_cw[:CONV_TAPS])], axis=0)
    part, (packs,) = _dw_cols(h1b, dz, "proj_in_dw", _Scatter(pack, whole=True))
    (dg_a, du_a), (landed["w_in"],) = _ffn_bwd_hidden(dr1b, full["ffa_down"], silu_a, udsilu_a, "ffa_bwd_hidden", _Scatter(part, by_columns=True))
    part, _ = _dw_hidden_rows_paired(hid_a, dr1b, "ffa_dw_down")
    part, (landed["ffa_down"],) = _dw_hidden_rows_paired(dg_a, xb, "ffa_dw_gate", _ChipScatter(part))
    part, (landed["ffa_gate"],) = _dw_hidden_rows_paired(du_a, xb, "ffa_dw_up", _ChipScatter(part))
    grad_x, (landed["ffa_up"],) = _ffn_bwd_input(dg_a, full["ffa_gate"], du_a, full["ffa_up"], dr1, None, "ffa_bwd_input", _ChipScatter(part))

    grads, deltas, new_m, new_v = {}, {}, {}, {}
    for k in big:
        out = _adamw(landed[k], local(k, big[k]), local(k, big_m[k]), local(k, big_v[k]), "adamw_" + k)
        grads[k], deltas[k], new_m[k], new_v[k] = ((jnp.transpose(o) if k in TRANSPOSED else o).reshape(big[k].shape) for o in out)

    small = dict(ln_a_g=ln_a_g, ln_a_b=ln_a_b, b_in=b_in, w_s=w_s, b_s=b_s, ln_v_g=ln_v_g, ln_v_b=ln_v_b, b_out=b_out,
                 ln_m_g=ln_m_g, ln_m_b=ln_m_b, ln_c_g=ln_c_g, ln_c_b=ln_c_b)
    small_m = dict(ln_a_g=m_ln_a_g, ln_a_b=m_ln_a_b, b_in=m_b_in, w_s=m_w_s, b_s=m_b_s, ln_v_g=m_ln_v_g, ln_v_b=m_ln_v_b, b_out=m_b_out,
                   ln_m_g=m_ln_m_g, ln_m_b=m_ln_m_b, ln_c_g=m_ln_c_g, ln_c_b=m_ln_c_b)
    small_v = dict(ln_a_g=v_ln_a_g, ln_a_b=v_ln_a_b, b_in=v_b_in, w_s=v_w_s, b_s=v_b_s, ln_v_g=v_ln_v_g, ln_v_b=v_ln_v_b, b_out=v_b_out,
                   ln_m_g=v_ln_m_g, ln_m_b=v_ln_m_b, ln_c_g=v_ln_c_g, ln_c_b=v_ln_c_b)
    snames = list(small)
    assert snames == list(small_g) and conv_w.shape[2] == LANES and da == N_DEV * LANES
    views = lambda tree: [_rows128(tree[k]) for k in snames]
    out = _adamw_small(packs, [a.shape[0] for a in views(small)], views(small), views(small_m), views(small_v),
                       (conv_w[0], m_conv_w[0], v_conv_w[0]), "adamw_small")
    for k, per_param in zip(snames + ["conv_w"], out):
        shape = conv_w.shape if k == "conv_w" else small[k].shape
        grads[k], deltas[k], new_m[k], new_v[k] = (o.reshape(shape) for o in per_param)

    order = ["ffa_gate", "ffa_up", "ffa_down", "ln_a_g", "ln_a_b", "w_in", "b_in", "w_s", "b_s", "ln_v_g", "ln_v_b", "conv_w", "w_out", "b_out",
             "ln_m_g", "ln_m_b", "ffc_gate", "ffc_up", "ffc_down", "ln_c_g", "ln_c_b"]
    return (loss, grad_x[None], *[grads[k] for k in order], *[deltas[k] for k in order], *[new_m[k] for k in order], *[new_v[k] for k in order])
```

```python
import math

import jax
import jax.numpy as jnp
from jax import lax
from jax.experimental import pallas as pl
from jax.experimental.pallas import tpu as pltpu

BF16 = jnp.bfloat16
F32 = jnp.float32
MESH = pl.DeviceIdType.MESH

N_DEV = 8
HEAD = 128
CHUNK = 128
CONV_TAPS = 3
LN_EPS = 1e-5
ALPHA = float(2 ** 0.25)
GELU_C = 0.7978845608028654
GELU_A = 0.044715
ADAM_LR, ADAM_B1, ADAM_B2, ADAM_EPS, ADAM_WD, ADAM_STEP = 0.001, 0.9, 0.999, 1e-08, 0.01, 10
V7X_VMEM_LIMIT = 56 * 1024 * 1024
LANES = 128
BF16_ROWS = 16
MXU_COLS = 256
TRANSPOSED = ("ffa_gate", "ffa_up", "ffc_gate", "ffc_up")
PROJ_IN_COLS = 1280
MIXER_TOKENS = 512
ADAM_SIDE_ROWS = 64
DW_TOKENS = 2048

NT_DIMS = (((1,), (1,)), ((), ()))
TN_DIMS = (((0,), (0,)), ((), ()))
ANY = pl.BlockSpec(memory_space=pl.ANY)


def _gelu_tanh(x):
    return jnp.tanh(GELU_C * (x + GELU_A * x * x * x))


def _gelu(x, t):
    return 0.5 * x * (1.0 + t)


def _gelu_grad(x, t):
    return 0.5 * (1.0 + t) + 0.5 * x * (1.0 - t * t) * GELU_C * (1.0 + 3.0 * GELU_A * x * x)


def _sigmoid(x):
    return 0.5 * jnp.tanh(0.5 * x) + 0.5


def _row(shape):
    return pl.BlockSpec(shape, lambda *_: (0,) * len(shape))


def _row_blocks(tm, rows=128):
    rows = min(rows, tm)
    return [slice(r, r + rows) for r in range(0, tm, rows)]


def _ln_backward(dh, xhat, rstd, gain):
    dxh = dh * gain
    m1 = jnp.mean(dxh, axis=-1, keepdims=True)
    m2 = jnp.mean(dxh * xhat, axis=-1, keepdims=True)
    return rstd * (dxh - m1 - xhat * m2)


def _place():
    x, y, c = lax.axis_index("x"), lax.axis_index("y"), lax.axis_index("c")
    return x, y, c, [(1 - x, y), (x, 1 - y), (1 - x, 1 - y)]


def _other_devices(x, y, c):
    flips = [(bx, by, bc) for bx in (0, 1) for by in (0, 1) for bc in (0, 1)][1:]
    return [(1 - x if bx else x, 1 - y if by else y, 1 - c if bc else c) for bx, by, bc in flips]


class _Gather:
    def __init__(self, shards, forward_at=0.75, by_columns=()):
        n = len(shards)
        self.n, self.forward_at, self.by_columns = n, forward_at, tuple(by_columns)
        self.inputs = list(shards)
        self.out_shapes = [jax.ShapeDtypeStruct((a.shape[0], N_DEV * a.shape[1]) if i in self.by_columns else (N_DEV,) + a.shape, a.dtype)
                           for i, a in enumerate(shards)]
        self.scratch = [pltpu.SemaphoreType.DMA((n, 7)), pltpu.SemaphoreType.DMA((n, 7)), pltpu.SemaphoreType.DMA((n,))]

    def _block(self, outs, a, dev):
        if a in self.by_columns:
            cols = outs[a].shape[1] // N_DEV
            return outs[a].at[:, pl.ds(dev * cols, cols)]
        return outs[a].at[dev]

    def _copy(self, outs, sems, a, k, block, to, src=None):
        dst = self._block(outs, a, block)
        return pltpu.make_async_remote_copy(src_ref=dst if src is None else src, dst_ref=dst, send_sem=sems[0].at[a, k],
                                            recv_sem=sems[1].at[a, k], device_id=to, device_id_type=MESH)

    def start(self, ins, outs, sems, urgent=None):
        x, y, c, chips = _place()
        me = 4 * x + 2 * y + c
        for a in range(self.n):
            pltpu.make_async_copy(ins[a], self._block(outs, a, me), sems[2].at[a]).start()
        urgent = list(range(self.n)) if urgent is None else list(urgent)
        for group in (urgent, [a for a in range(self.n) if a not in urgent]):
            for a in group:
                self._copy(outs, sems, a, 0, me, (x, y, 1 - c), src=ins[a]).start()
                for j in (0, 1):
                    self._copy(outs, sems, a, 1 + j, me, (*chips[j], c), src=ins[a]).start()
            for a in group:
                self._copy(outs, sems, a, 3, me, (*chips[2], c), src=ins[a]).start()

    def wait_sibling(self, outs, sems, a):
        x, y, c, _ = _place()
        self._copy(outs, sems, a, 0, 4 * x + 2 * y + 1 - c, (x, y, 1 - c)).wait_recv()

    def pass_on(self, outs, sems, a, j):
        x, y, c, chips = _place()
        block = 4 * chips[j][0] + 2 * chips[j][1] + c
        self._copy(outs, sems, a, 1 + j, block, (x, y, 1 - c)).wait_recv()
        self._copy(outs, sems, a, 4 + j, block, (x, y, 1 - c)).start()

    def wait_passed(self, outs, sems, a, j):
        x, y, c, chips = _place()
        self._copy(outs, sems, a, 4 + j, 4 * chips[j][0] + 2 * chips[j][1] + 1 - c, (x, y, 1 - c)).wait_recv()

    def wait_sent(self, ins, outs, sems, a):
        x, y, c, _ = _place()
        me = 4 * x + 2 * y + c
        for k in range(7):
            self._copy(outs, sems, a, k, me, (x, y, 1 - c), src=ins[a]).wait_send()
        pltpu.make_async_copy(ins[a], self._block(outs, a, me), sems[2].at[a]).wait()

    def forward(self, ins, outs, sems):
        for a in range(self.n):
            for j in range(3):
                self.pass_on(outs, sems, a, j)

    def finish(self, ins, outs, sems):
        for a in range(self.n):
            self.wait_sibling(outs, sems, a)
            for j in range(3):
                self.wait_passed(outs, sems, a, j)
        for a in range(self.n):
            self.wait_sent(ins, outs, sems, a)

    def before(self, step, n_steps, ins, outs, sems):
        pl.when(step == 0)(lambda: self.start(ins, outs, sems))
        pl.when(step == int(self.forward_at * (n_steps - 1)))(lambda: self.forward(ins, outs, sems))

    def after(self, step, n_steps, ins, outs, sems):
        pl.when(step == n_steps - 1)(lambda: self.finish(ins, outs, sems))


class _Scatter:
    def __init__(self, partial, whole=False, by_columns=False):
        self.whole, self.by_columns = whole, by_columns
        self.inputs = [partial]
        if whole:
            shape = (N_DEV,) + partial.shape
        elif by_columns:
            shape = (N_DEV, partial.shape[0], partial.shape[1] // N_DEV)
        else:
            shape = partial.shape
        self.out_shapes = [jax.ShapeDtypeStruct(shape, partial.dtype)]
        self.scratch = [pltpu.SemaphoreType.DMA((7,)), pltpu.SemaphoreType.DMA((7,)), pltpu.SemaphoreType.DMA((1,))]

    def _copies(self, ins, outs, sems):
        x, y, c, _ = _place()
        me = 4 * x + 2 * y + c
        if self.whole:
            block = lambda dev: ins[0]
        elif self.by_columns:
            cols = ins[0].shape[1] // N_DEV
            block = lambda dev: ins[0].at[:, pl.ds(dev * cols, cols)]
        else:
            block = lambda dev: ins[0].at[dev]
        mine = pltpu.make_async_copy(block(me), outs[0].at[me], sems[2].at[0])
        remote = [pltpu.make_async_remote_copy(src_ref=block(4 * px + 2 * py + pc), dst_ref=outs[0].at[me], send_sem=sems[0].at[k],
                                               recv_sem=sems[1].at[k], device_id=(px, py, pc), device_id_type=MESH)
                  for k, (px, py, pc) in enumerate(_other_devices(x, y, c))]
        return mine, remote

    def start(self, ins, outs, sems):
        mine, remote = self._copies(ins, outs, sems)
        mine.start()
        for cp in remote:
            cp.start()

    def finish(self, ins, outs, sems):
        mine, remote = self._copies(ins, outs, sems)
        for cp in remote:
            cp.wait()
        mine.wait()

    def before(self, step, n_steps, ins, outs, sems):
        pl.when(step == 0)(lambda: self.start(ins, outs, sems))

    def after(self, step, n_steps, ins, outs, sems):
        pl.when(step == n_steps - 1)(lambda: self.finish(ins, outs, sems))


class _ChipScatter(_Scatter):
    def __init__(self, sums):
        super().__init__(sums)
        self.scratch = [pltpu.SemaphoreType.DMA((3,)), pltpu.SemaphoreType.DMA((3,)), pltpu.SemaphoreType.DMA((1,))]

    def _copies(self, ins, outs, sems):
        x, y, c, chips = _place()
        my_chip = 2 * x + y
        mine = pltpu.make_async_copy(ins[0].at[my_chip], outs[0].at[my_chip], sems[2].at[0])
        remote = [pltpu.make_async_remote_copy(src_ref=ins[0].at[2 * px + py], dst_ref=outs[0].at[my_chip], send_sem=sems[0].at[k],
                                               recv_sem=sems[1].at[k], device_id=(px, py, c), device_id_type=MESH)
                  for k, (px, py) in enumerate(chips)]
        return mine, remote


class _AdamSide:
    def __init__(self, gparts, w, m, v):
        self.n, self.r, self.c = gparts.shape
        self.inputs = [gparts, w, m, v]
        self.out_shapes = [jax.ShapeDtypeStruct((self.r, self.c), F32)] * 4
        self.scratch = []

    def _plan(self, n_steps):
        rows = ADAM_SIDE_ROWS
        while self.r // rows > n_steps:
            rows *= 2
        assert self.r % rows == 0
        return rows, self.r // rows

    def _rows(self, grid):
        rows, n_blocks = self._plan(math.prod(grid))

        def block(*ids):
            step = 0
            for size, pid in zip(grid, ids):
                step = step * size + pid
            return jnp.minimum(step, n_blocks - 1)
        return pl.BlockSpec((rows, self.c), lambda *ids: (block(*ids), 0)), block, rows

    def in_specs(self, grid):
        spec, block, rows = self._rows(grid)
        return [pl.BlockSpec((self.n, rows, self.c), lambda *ids: (0, block(*ids), 0)), spec, spec, spec]

    def out_specs(self, grid):
        return [self._rows(grid)[0]] * 4

    def before(self, step, n_steps, ins, outs, sems):
        @pl.when(step < self._plan(n_steps)[1])
        def _():
            g_ref, w_ref, m_ref, v_ref = ins
            g = g_ref[0].astype(F32)
            for q in range(1, self.n):
                g = g + g_ref[q].astype(F32)
            outs[0][...] = g
            outs[1][...], outs[2][...], outs[3][...] = _adam_update(g, w_ref[...], m_ref[...], v_ref[...])

    def after(self, step, n_steps, ins, outs, sems):
        pass


def _call(body, name, grid, in_specs, out_specs, out_shape, scratch, ins, rider=None):
    single = not isinstance(out_shape, (list, tuple))
    out_shape = [out_shape] if single else list(out_shape)
    out_specs = [out_specs] if single else list(out_specs)
    params = pltpu.CompilerParams(dimension_semantics=("arbitrary",) * len(grid), vmem_limit_bytes=V7X_VMEM_LIMIT)
    if rider is None:
        outs = pl.pallas_call(body, name=name, grid=grid, in_specs=in_specs, out_specs=out_specs, out_shape=out_shape,
                              scratch_shapes=scratch, compiler_params=params)(*ins)
        return (outs[0] if single else outs), None
    riders = list(rider) if isinstance(rider, (list, tuple)) else [rider]
    n_in, n_out, n_scr = len(ins), len(out_shape), len(scratch)
    n_steps = math.prod(grid)
    specs = lambda r, kind, count: getattr(r, kind)(grid) if hasattr(r, kind) else [ANY] * count

    def carried(*refs):
        refs = list(refs)
        cut = lambda n: [refs.pop(0) for _ in range(n)]
        b_in, r_in = cut(n_in), [cut(len(r.inputs)) for r in riders]
        b_out, r_out = cut(n_out), [cut(len(r.out_shapes)) for r in riders]
        b_scr, r_scr = cut(n_scr), [cut(len(r.scratch)) for r in riders]
        step = 0
        for axis, size in enumerate(grid):
            step = step * size + pl.program_id(axis)
        for r, r_ins, r_outs, r_sems in zip(riders, r_in, r_out, r_scr):
            r.before(step, n_steps, r_ins, r_outs, r_sems)
        body(*b_in, *b_out, *b_scr)
        for r, r_ins, r_outs, r_sems in zip(riders, r_in, r_out, r_scr):
            r.after(step, n_steps, r_ins, r_outs, r_sems)

    outs = pl.pallas_call(
        carried, name=name, grid=grid,
        in_specs=list(in_specs) + [sp for r in riders for sp in specs(r, "in_specs", len(r.inputs))],
        out_specs=out_specs + [sp for r in riders for sp in specs(r, "out_specs", len(r.out_shapes))],
        out_shape=out_shape + [sh for r in riders for sh in r.out_shapes],
        scratch_shapes=list(scratch) + [sc for r in riders for sc in r.scratch], compiler_params=params,
    )(*ins, *[a for r in riders for a in r.inputs])
    base, rest = outs[:n_out], list(outs[n_out:])
    per_rider = [[rest.pop(0) for _ in r.out_shapes] for r in riders]
    return (base[0] if single else base), (per_rider if isinstance(rider, (list, tuple)) else per_rider[0])


def _arrival_block(j):
    x, y, c = lax.axis_index("x"), lax.axis_index("y"), lax.axis_index("c")
    chip, other_core = j // 2, j % 2
    px = jnp.where((chip == 1) | (chip == 3), 1 - x, x)
    py = jnp.where((chip == 2) | (chip == 3), 1 - y, y)
    pc = jnp.where(other_core == 1, 1 - c, c)
    return 4 * px + 2 * py + pc


def _ffn_gateup_gathering(xb, gate, up_shard, down_shard, name):
    s, d = xb.shape
    fs = up_shard.shape[0]
    tm = min(s, 1024)
    ni = s // tm
    ask_at = max(ni - 2, 0)
    gate_here = gate.ndim == 2
    gather = _Gather(([gate] if gate_here else []) + [up_shard, down_shard])
    n_g = gather.n
    used_here, down = tuple(range(n_g - 1)), n_g - 1

    def body(x_ref, *refs):
        refs = list(refs)
        gate_full = None if gate_here else refs.pop(0)
        shards = [refs.pop(0) for _ in range(n_g)]
        silu_ref, udsilu_ref, h_ref = refs.pop(0), refs.pop(0), refs.pop(0)
        fulls = [refs.pop(0) for _ in range(n_g)]
        w_ref, w_sems = refs.pop(0), refs.pop(0)
        sems = refs
        j, i = pl.program_id(0), pl.program_id(1)
        gate_src, up_src = (fulls[0], fulls[1]) if gate_here else (gate_full, fulls[0])

        def load(slot, srcs):
            return [pltpu.make_async_copy(src, w_ref.at[slot, a], w_sems.at[slot, a]) for a, src in enumerate(srcs)]

        @pl.when((j == 0) & (i == 0))
        def _():
            gather.start(shards, fulls, sems, urgent=used_here)
            mine = load(0, (shards[0] if gate_here else gate_full.at[_arrival_block(0)], shards[n_g - 2]))
            for cp in mine:
                cp.start()
            for cp in mine:
                cp.wait()

        for nxt in range(1, N_DEV):
            @pl.when((j == nxt) & (i == 0))
            def _(nxt=nxt):
                for cp in load(nxt % 2, (gate_src.at[0], up_src.at[0])):
                    cp.wait()

        for nxt in range(1, N_DEV):
            @pl.when((j == nxt - 1) & (i == ask_at))
            def _(nxt=nxt):
                for a in used_here:
                    if nxt == 1:
                        gather.wait_sibling(fulls, sems, a)
                    elif nxt % 2 == 0:
                        gather.pass_on(fulls, sems, a, nxt // 2 - 1)
                    else:
                        gather.wait_passed(fulls, sems, a, nxt // 2 - 1)
                block = _arrival_block(nxt)
                for cp in load(nxt % 2, (gate_src.at[block], up_src.at[block])):
                    cp.start()

        @pl.when((j == N_DEV - 1) & (i == ask_at))
        def _():
            for other_chip in range(3):
                gather.pass_on(fulls, sems, down, other_chip)

        x = x_ref[...]
        g = lax.dot_general(x, w_ref[j % 2, 0], NT_DIMS, preferred_element_type=F32)
        u = lax.dot_general(x, w_ref[j % 2, 1], NT_DIMS, preferred_element_type=F32)
        sg = _sigmoid(g)
        silu = g * sg
        silu_ref[...] = silu.astype(BF16)
        udsilu_ref[...] = (u * (sg + silu * (1.0 - sg))).astype(BF16)
        h_ref[...] = (silu * u).astype(BF16)

        @pl.when((j == N_DEV - 1) & (i == ni - 1))
        def _():
            gather.wait_sibling(fulls, sems, down)
            for other_chip in range(3):
                gather.wait_passed(fulls, sems, down, other_chip)
            for a in range(n_g):
                gather.wait_sent(shards, fulls, sems, a)

    shp = jax.ShapeDtypeStruct((N_DEV, s, fs), BF16)
    o_spec = pl.BlockSpec((None, tm, fs), lambda j, i: (_arrival_block(j), i, 0))
    ins = ([] if gate_here else [gate]) + gather.inputs
    outs = pl.pallas_call(
        body, name=name, grid=(N_DEV, ni), in_specs=[pl.BlockSpec((tm, d), lambda j, i: (i, 0))] + [ANY] * len(ins),
        out_specs=[o_spec, o_spec, o_spec] + [ANY] * n_g, out_shape=[shp, shp, shp] + gather.out_shapes,
        scratch_shapes=[pltpu.VMEM((2, 2, fs, d), BF16), pltpu.SemaphoreType.DMA((2, 2))] + gather.scratch,
        compiler_params=pltpu.CompilerParams(dimension_semantics=("arbitrary", "arbitrary"), vmem_limit_bytes=V7X_VMEM_LIMIT),
    )(xb, *ins)
    return outs[:3], outs[3:]


def _down_ln(a3, w3, bias, res, res_affine, ln_g, ln_b, scale, name, target=None, rider=None):
    nk, s, tk = a3.shape
    d = w3.shape[2]
    tm = min(s, 256)
    final = target is not None

    def body(*refs):
        refs = list(refs)
        a_ref, w_hbm = refs[:2]
        del refs[:2]
        bias_ref = refs.pop(0) if bias is not None else None
        res_ref = refs.pop(0)
        rg_ref, rb_ref = (refs.pop(0), refs.pop(0)) if res_affine is not None else (None, None)
        g_ref, b_ref = refs.pop(0), refs.pop(0)
        t_ref = refs.pop(0) if final else None
        w_sem = refs.pop()
        w_ref = refs.pop()
        i = pl.program_id(0)
        if final:
            dr_ref, drb_ref, sq_ref, dg_ref, db_ref = refs
        else:
            xh_ref, hb_ref, rstd_ref = refs

        @pl.when(i == 0)
        def _():
            whole = pltpu.make_async_copy(w_hbm, w_ref, w_sem.at[0])
            whole.start()
            whole.wait()
            if final:
                sq_ref[...] = jnp.zeros_like(sq_ref)
                dg_ref[...] = jnp.zeros_like(dg_ref)
                db_ref[...] = jnp.zeros_like(db_ref)

        y = jnp.dot(a_ref[0], w_ref[0], preferred_element_type=F32)
        for k in range(1, nk):
            y = y + jnp.dot(a_ref[k], w_ref[k], preferred_element_type=F32)
        if bias_ref is not None:
            y = y + bias_ref[...]
        for rows in _row_blocks(tm):
            r = res_ref[rows, :]
            if rg_ref is not None:
                r = r * rg_ref[...] + rb_ref[...]
            r = ALPHA * r + scale * y[rows]
            mu = jnp.mean(r, axis=-1, keepdims=True)
            c = r - mu
            var = jnp.mean(c * c, axis=-1, keepdims=True)
            rstd = lax.rsqrt(var + LN_EPS)
            xhat = c * rstd
            h = xhat * g_ref[...] + b_ref[...]
            if not final:
                xh_ref[rows, :] = xhat
                hb_ref[rows, :] = h.astype(BF16)
                rstd_ref[rows, :] = rstd
            else:
                err = h - t_ref[rows, :]
                sq_ref[...] += jnp.sum(err * err, axis=0, keepdims=True)
                dh = err * (1.0 / d)
                dg_ref[...] += jnp.sum(dh * xhat, axis=0, keepdims=True)
                db_ref[...] += jnp.sum(dh, axis=0, keepdims=True)
                dr = _ln_backward(dh, xhat, rstd, g_ref[...])
                dr_ref[rows, :] = dr
                drb_ref[rows, :] = (scale * dr).astype(BF16)

    tok = pl.BlockSpec((tm, d), lambda i: (i, 0))
    vec = pl.BlockSpec((1, d), lambda i: (0, 0))
    ins = [a3, w3]
    in_specs = [pl.BlockSpec((nk, tm, tk), lambda i: (0, i, 0)), ANY]
    if bias is not None:
        ins.append(bias)
        in_specs.append(vec)
    ins.append(res)
    in_specs.append(tok)
    if res_affine is not None:
        ins += list(res_affine)
        in_specs += [vec, vec]
    ins += [ln_g, ln_b]
    in_specs += [vec, vec]
    if final:
        ins.append(target)
        in_specs.append(tok)
        out_shape = [jax.ShapeDtypeStruct((s, d), F32), jax.ShapeDtypeStruct((s, d), BF16)] + [jax.ShapeDtypeStruct((1, d), F32)] * 3
        out_specs = [tok, tok, vec, vec, vec]
    else:
        out_shape = [jax.ShapeDtypeStruct((s, d), F32), jax.ShapeDtypeStruct((s, d), BF16), jax.ShapeDtypeStruct((s, 1), F32)]
        out_specs = [tok, tok, pl.BlockSpec((tm, 1), lambda i: (i, 0))]
    scratch = [pltpu.VMEM((nk, tk, d), BF16), pltpu.SemaphoreType.DMA((1,))]
    return _call(body, name, (s // tm,), in_specs, out_specs, out_shape, scratch, ins, rider)


def _proj_in(hb, w, bias, name, rider=None):
    s, d = hb.shape
    n = w.shape[1]
    tm = min(s, 1024)
    tn = PROJ_IN_COLS if n % PROJ_IN_COLS == 0 else n

    def body(h_ref, w_ref, b_ref, z_ref):
        z_ref[...] = (jnp.dot(h_ref[...], w_ref[...], preferred_element_type=F32) + b_ref[...]).astype(BF16)

    in_specs = [pl.BlockSpec((tm, d), lambda i, j: (i, 0)), pl.BlockSpec((d, tn), lambda i, j: (0, j)),
                pl.BlockSpec((1, tn), lambda i, j: (0, j))]
    return _call(body, name, (s // tm, n // tn), in_specs, pl.BlockSpec((tm, tn), lambda i, j: (i, j)),
                 jax.ShapeDtypeStruct((s, n), BF16), [], [hb, w, bias], rider)


def _nt_hidden(ab, w3, name):
    s, kdim = ab.shape
    nj, tn, _ = w3.shape
    tm = min(s, 1024)

    def body(a_ref, w_ref, o_ref):
        o_ref[...] = lax.dot_general(a_ref[...], w_ref[...], NT_DIMS, preferred_element_type=F32).astype(BF16)

    in_specs = [pl.BlockSpec((tm, kdim), lambda i, j: (i, 0)), pl.BlockSpec((None, tn, kdim), lambda i, j: (j, 0, 0))]
    return _call(body, name, (s // tm, nj), in_specs, pl.BlockSpec((None, tm, tn), lambda i, j: (j, i, 0)),
                 jax.ShapeDtypeStruct((nj, s, tn), BF16), [], [ab, w3])[0]


def _ffn_bwd_hidden(ab, w3, silu3, udsilu3, name, rider=None):
    s, kdim = ab.shape
    nj, tn, _ = w3.shape
    tm = min(s, 1024)

    def body(a_ref, w_ref, silu_ref, udsilu_ref, dg_ref, du_ref):
        a = a_ref[...]
        for c0 in range(0, tn, MXU_COLS):
            cols = slice(c0, min(c0 + MXU_COLS, tn))
            t = lax.dot_general(a, w_ref[cols, :], NT_DIMS, preferred_element_type=F32)
            du_ref[:, cols] = (t * silu_ref[:, cols].astype(F32)).astype(BF16)
            dg_ref[:, cols] = (t * udsilu_ref[:, cols].astype(F32)).astype(BF16)

    hid = pl.BlockSpec((None, tm, tn), lambda i, j: (j, i, 0))
    shp = jax.ShapeDtypeStruct((nj, s, tn), BF16)
    in_specs = [pl.BlockSpec((tm, kdim), lambda i, j: (i, 0)), pl.BlockSpec((None, tn, kdim), lambda i, j: (j, 0, 0)), hid, hid]
    return _call(body, name, (s // tm, nj), in_specs, [hid, hid], [shp, shp], [], [ab, w3, silu3, udsilu3], rider)


def _tn_dw(a, a_spec, b, b_spec, nj, m, n, s, tk, name, rider):
    def body(a_ref, b_ref, o_ref, acc_ref):
        k = pl.program_id(1)

        @pl.when(k == 0)
        def _():
            acc_ref[...] = jnp.zeros_like(acc_ref)

        acc_ref[...] += lax.dot_general(a_ref[...], b_ref[...], TN_DIMS, preferred_element_type=F32)

        @pl.when(k == s // tk - 1)
        def _():
            o_ref[...] = acc_ref[...].astype(BF16)

    return _call(body, name, (nj, s // tk), [a_spec, b_spec], pl.BlockSpec((None, m, n), lambda j, k: (j, 0, 0)),
                 jax.ShapeDtypeStruct((nj, m, n), BF16), [pltpu.VMEM((m, n), F32)], [a, b], rider)


def _dw_hidden_rows(hid3, db, name, rider=None):
    nj, s, fs = hid3.shape
    d = db.shape[1]
    tk = min(s, DW_TOKENS)
    return _tn_dw(hid3, pl.BlockSpec((None, tk, fs), lambda j, k: (j, k, 0)), db, pl.BlockSpec((tk, d), lambda j, k: (k, 0)),
                  nj, fs, d, s, tk, name, rider)


def _dw_hidden_rows_paired(hid3, db, name, rider=None):
    nj, s, fs = hid3.shape
    d = db.shape[1]
    tk = min(s, DW_TOKENS)
    nk = s // tk
    half = nj // 2

    def device_of(j):
        c = lax.axis_index("c")
        return 2 * (j % half) + jnp.where(j < half, 1 - c, c)

    def body(a_ref, b_ref, o_ref, theirs_ref, acc_ref, stage_ref, got_ref, send_sems, recv_sems, load_sem):
        j, k = pl.program_id(0), pl.program_id(1)
        x, y, c = lax.axis_index("x"), lax.axis_index("y"), lax.axis_index("c")

        def to_sibling(q):
            return pltpu.make_async_remote_copy(src_ref=stage_ref, dst_ref=theirs_ref.at[q], send_sem=send_sems.at[q],
                                                recv_sem=recv_sems.at[q], device_id=(x, y, 1 - c), device_id_type=MESH)

        def fetch(q):
            return pltpu.make_async_copy(theirs_ref.at[q], got_ref, load_sem.at[0])

        @pl.when(k == 0)
        def _():
            acc_ref[...] = jnp.zeros_like(acc_ref)

        acc_ref[...] += lax.dot_general(a_ref[...], b_ref[...], TN_DIMS, preferred_element_type=F32)

        for q in range(half):
            @pl.when((j == q) & (k == nk - 1))
            def _(q=q):
                if q > 0:
                    to_sibling(q - 1).wait_send()
                stage_ref[...] = acc_ref[...].astype(BF16)
                to_sibling(q).start()

            @pl.when((j == half + q) & (k == 0))
            def _(q=q):
                to_sibling(q).wait_recv()
                fetch(q).start()

            @pl.when((j == half + q) & (k == nk - 1))
            def _(q=q):
                if q == 0:
                    to_sibling(half - 1).wait_send()
                fetch(q).wait()
                o_ref[...] = (acc_ref[...] + got_ref[...].astype(F32)).astype(BF16)

    in_specs = [pl.BlockSpec((None, tk, fs), lambda j, k: (device_of(j), k, 0)), pl.BlockSpec((tk, d), lambda j, k: (k, 0))]
    out_specs = [pl.BlockSpec((None, fs, d), lambda j, k: (jnp.maximum(j - half, 0), 0, 0)), ANY]
    shp = jax.ShapeDtypeStruct((half, fs, d), BF16)
    scratch = [pltpu.VMEM((fs, d), F32), pltpu.VMEM((fs, d), BF16), pltpu.VMEM((fs, d), BF16),
               pltpu.SemaphoreType.DMA((half,)), pltpu.SemaphoreType.DMA((half,)), pltpu.SemaphoreType.DMA((1,))]
    (sums, _), riders_out = _call(body, name, (nj, nk), in_specs, out_specs, [shp, shp], scratch, [hid3, db], rider)
    return sums, riders_out


def _dw_cols(ab, dz, name, rider=None):
    s, d = ab.shape
    n = dz.shape[1]
    tn = PROJ_IN_COLS if n % PROJ_IN_COLS == 0 else n
    tk = min(s, DW_TOKENS)
    tr = d // 2

    def body(a_ref, b_ref, o_ref, acc_ref):
        k = pl.program_id(2)

        @pl.when(k == 0)
        def _():
            acc_ref[...] = jnp.zeros_like(acc_ref)

        acc_ref[...] += lax.dot_general(a_ref[...], b_ref[...], TN_DIMS, preferred_element_type=F32)

        @pl.when(k == s // tk - 1)
        def _():
            o_ref[...] = acc_ref[...].astype(BF16)

    in_specs = [pl.BlockSpec((tk, tr), lambda j, r, k: (k, r)), pl.BlockSpec((tk, tn), lambda j, r, k: (k, j))]
    return _call(body, name, (n // tn, d // tr, s // tk), in_specs, pl.BlockSpec((tr, tn), lambda j, r, k: (r, j)),
                 jax.ShapeDtypeStruct((d, n), BF16), [pltpu.VMEM((tr, tn), F32)], [ab, dz], rider)


def _ffn_bwd_input(dg3, wg3, du3, wu3, dres, ln, name, rider=None):
    s, d = dres.shape
    nk, _, fs = dg3.shape
    tm = min(s, 512)

    def body(*refs):
        refs = list(refs)
        dg_in, wg_ref, du_in, wu_ref, dres_ref = refs[:5]
        del refs[:5]
        if ln is not None:
            xh_ref, rstd_ref, gain_ref = refs.pop(0), refs.pop(0), refs.pop(0)
        acc_ref = refs.pop()
        i, k = pl.program_id(0), pl.program_id(1)

        @pl.when(k == 0)
        def _():
            acc_ref[...] = jnp.zeros_like(acc_ref)

        acc_ref[...] += (jnp.dot(dg_in[...], wg_ref[...], preferred_element_type=F32)
                         + jnp.dot(du_in[...], wu_ref[...], preferred_element_type=F32))

        @pl.when(k == nk - 1)
        def _():
            if ln is not None:
                dr_ref, drb_ref, dg_ref, db_ref, sum_ref = refs

                @pl.when(i == 0)
                def _():
                    dg_ref[...] = jnp.zeros_like(dg_ref)
                    db_ref[...] = jnp.zeros_like(db_ref)
                    sum_ref[...] = jnp.zeros_like(sum_ref)

            for rows in _row_blocks(tm):
                dh = ALPHA * dres_ref[rows, :] + acc_ref[rows, :]
                if ln is None:
                    refs[0][rows, :] = dh
                else:
                    xhat = xh_ref[rows, :]
                    dg_ref[...] += jnp.sum(dh * xhat, axis=0, keepdims=True)
                    db_ref[...] += jnp.sum(dh, axis=0, keepdims=True)
                    dr = _ln_backward(dh, xhat, rstd_ref[rows, :], gain_ref[...])
                    sum_ref[...] += jnp.sum(dr, axis=0, keepdims=True)
                    dr_ref[rows, :] = dr
                    drb_ref[rows, :] = dr.astype(BF16)

    tok = pl.BlockSpec((tm, d), lambda i, k: (i, 0))
    vec = pl.BlockSpec((1, d), lambda i, k: (0, 0))
    a_spec = pl.BlockSpec((None, tm, fs), lambda i, k: (k, i, 0))
    w_spec = pl.BlockSpec((None, fs, d), lambda i, k: (k, 0, 0))
    ins, in_specs = [dg3, wg3, du3, wu3, dres], [a_spec, w_spec, a_spec, w_spec, tok]
    if ln is None:
        out_shape, out_specs = jax.ShapeDtypeStruct((s, d), F32), tok
    else:
        ins += list(ln)
        in_specs += [tok, pl.BlockSpec((tm, 1), lambda i, k: (i, 0)), vec]
        out_shape = [jax.ShapeDtypeStruct((s, d), F32), jax.ShapeDtypeStruct((s, d), BF16)] + [jax.ShapeDtypeStruct((1, d), F32)] * 3
        out_specs = [tok, tok, vec, vec, vec]
    return _call(body, name, (s // tm, nk), in_specs, out_specs, out_shape, [pltpu.VMEM((tm, d), F32)], ins, rider)


def _proj_in_bwd_ln(dz, w, dres, ln, branch_scale, name):
    s, d = dres.shape
    n = w.shape[1]
    tm = min(s, 256)

    def body(dz_ref, w_hbm, dres_ref, xh_ref, rstd_ref, gain_ref, dr_ref, drb_ref, dg_ref, db_ref, w_ref, w_sem):
        @pl.when(pl.program_id(0) == 0)
        def _():
            whole = pltpu.make_async_copy(w_hbm, w_ref, w_sem.at[0])
            whole.start()
            whole.wait()
            dg_ref[...] = jnp.zeros_like(dg_ref)
            db_ref[...] = jnp.zeros_like(db_ref)

        acc = lax.dot_general(dz_ref[...], w_ref[...], NT_DIMS, preferred_element_type=F32)
        for rows in _row_blocks(tm):
            dh = ALPHA * dres_ref[rows, :] + acc[rows]
            xhat = xh_ref[rows, :]
            dg_ref[...] += jnp.sum(dh * xhat, axis=0, keepdims=True)
            db_ref[...] += jnp.sum(dh, axis=0, keepdims=True)
            dr = _ln_backward(dh, xhat, rstd_ref[rows, :], gain_ref[...])
            dr_ref[rows, :] = dr
            drb_ref[rows, :] = (branch_scale * dr).astype(BF16)

    tok = pl.BlockSpec((tm, d), lambda i: (i, 0))
    vec = pl.BlockSpec((1, d), lambda i: (0, 0))
    in_specs = [pl.BlockSpec((tm, n), lambda i: (i, 0)), ANY, tok, tok, pl.BlockSpec((tm, 1), lambda i: (i, 0)), vec]
    out_shape = [jax.ShapeDtypeStruct((s, d), F32), jax.ShapeDtypeStruct((s, d), BF16)] + [jax.ShapeDtypeStruct((1, d), F32)] * 2
    scratch = [pltpu.VMEM(w.shape, BF16), pltpu.SemaphoreType.DMA((1,))]
    return _call(body, name, (s // tm,), in_specs, [tok, tok, vec, vec], out_shape, scratch, [dz, w, dres] + list(ln))[0]


def _shift_rows_down(v, halo, k, row):
    out = pltpu.roll(v, k, 0)
    hr = halo.shape[0]
    for r in range(k):
        out = jnp.where(row == r, halo[hr - k + r:hr - k + r + 1, :], out)
    return out


def _shift_rows_up(v, halo, k, row):
    t = v.shape[0]
    out = pltpu.roll(v, t - k, 0)
    for r in range(k):
        out = jnp.where(row == t - k + r, halo[r:r + 1, :], out)
    return out


def _sgu_head_forward(z_ref, h, da, gv_ref, bv_ref):
    zu = z_ref[:, h * HEAD:(h + 1) * HEAD].astype(F32)
    zv = z_ref[:, da + h * HEAD:da + (h + 1) * HEAD].astype(F32)
    tu, tv = _gelu_tanh(zu), _gelu_tanh(zv)
    u = _gelu(zu, tu)
    v = _gelu(zv, tv)
    mu = jnp.mean(v, axis=-1, keepdims=True)
    c = v - mu
    rstd = lax.rsqrt(jnp.mean(c * c, axis=-1, keepdims=True) + LN_EPS)
    vhat = c * rstd
    vln = (vhat * gv_ref[h:h + 1, :] + bv_ref[h:h + 1, :]).astype(BF16)
    return (zu, tu), (zv, tv), u, vhat, rstd, vln


def _mixer_fwd(z, ws_masked, bs_wide, gv, bv, cw, name):
    s, zc = z.shape
    da = zc // 5
    nh = da // HEAD
    tm = min(s, MIXER_TOKENS)
    hb = tm // BF16_ROWS

    def body(z_ref, pc_ref, px_ref, ws_ref, bs_ref, gv_ref, bv_ref, cw_ref, y_ref):
        i = pl.program_id(0)
        for h in range(nh):
            _, _, u, _, _, vln = _sgu_head_forward(z_ref, h, da, gv_ref, bv_ref)
            for n in range(tm // CHUNK):
                rows = slice(n * CHUNK, (n + 1) * CHUNK)
                mixed = jnp.dot(ws_ref[h], vln[rows], preferred_element_type=F32) + bs_ref[h]
                y_ref[0, rows, h * HEAD:(h + 1) * HEAD] = (u[rows] * mixed).astype(BF16)
        gate_b = z_ref[:, 2 * da:3 * da].astype(F32)
        hc = z_ref[:, 3 * da:4 * da].astype(F32) * z_ref[:, 4 * da:5 * da].astype(F32)
        halo = jnp.where(i > 0, pc_ref[...].astype(F32) * px_ref[...].astype(F32), 0.0)
        row = lax.broadcasted_iota(jnp.int32, (tm, da), 0)
        y = cw_ref[0:1, :] * _shift_rows_down(hc, halo, 2, row) + cw_ref[1:2, :] * _shift_rows_down(hc, halo, 1, row) + cw_ref[2:3, :] * hc
        y_ref[1] = (gate_b * y).astype(BF16)

    prev = lambda col: pl.BlockSpec((BF16_ROWS, da), lambda i: (jnp.maximum(i * hb - 1, 0), col))
    in_specs = [pl.BlockSpec((tm, zc), lambda i: (i, 0)), prev(3), prev(4), _row((nh, CHUNK, CHUNK)), _row((nh, CHUNK, HEAD)),
                _row((nh, HEAD)), _row((nh, HEAD)), _row((CONV_TAPS, da))]
    return _call(body, name, (s // tm,), in_specs, pl.BlockSpec((2, tm, da), lambda i: (0, i, 0)),
                 jax.ShapeDtypeStruct((2, s, da), BF16), [], [z, z, z, ws_masked, bs_wide, gv, bv, cw])[0]


def _mixer_bwd(z, dy, ws_masked, bs_wide, gv, bv, cw, name, rider=None):
    s, zc = z.shape
    da = zc // 5
    nh = da // HEAD
    tm = min(s, MIXER_TOKENS)
    hb = tm // BF16_ROWS
    nblk = s // tm

    def body(z_ref, pc_ref, px_ref, nb_ref, dy_ref, ndy_ref, ws_ref, bs_ref, gv_ref, bv_ref, cw_ref,
             dz_ref, dws_ref, dbs_ref, dgv_ref, dbv_ref, dcw_ref, dbin_ref):
        i = pl.program_id(0)

        @pl.when(i == 0)
        def _():
            for ref in (dws_ref, dbs_ref, dgv_ref, dbv_ref, dcw_ref, dbin_ref):
                ref[...] = jnp.zeros_like(ref)

        causal = lax.broadcasted_iota(jnp.int32, (CHUNK, CHUNK), 0) >= lax.broadcasted_iota(jnp.int32, (CHUNK, CHUNK), 1)
        for h in range(nh):
            zu, zv, u, vhat, rstd, vln = _sgu_head_forward(z_ref, h, da, gv_ref, bv_ref)
            dya = dy_ref[0, :, h * HEAD:(h + 1) * HEAD].astype(F32)
            w = ws_ref[h]
            du_parts, dvln_parts = [], []
            for n in range(tm // CHUNK):
                rows = slice(n * CHUNK, (n + 1) * CHUNK)
                mixed = jnp.dot(w, vln[rows], preferred_element_type=F32) + bs_ref[h]
                du_parts.append(dya[rows] * mixed)
                dmix = dya[rows] * u[rows]
                dmix_b = dmix.astype(BF16)
                dws_ref[h] += jnp.where(causal, lax.dot_general(dmix_b, vln[rows], NT_DIMS, preferred_element_type=F32), 0.0)
                dbs_ref[h] += dmix
                dvln_parts.append(lax.dot_general(w, dmix_b, TN_DIMS, preferred_element_type=F32))
            du = jnp.concatenate(du_parts, axis=0)
            dvln = jnp.concatenate(dvln_parts, axis=0)
            dgv_ref[h:h + 1, :] += jnp.sum(dvln * vhat, axis=0, keepdims=True)
            dbv_ref[h:h + 1, :] += jnp.sum(dvln, axis=0, keepdims=True)
            dv = _ln_backward(dvln, vhat, rstd, gv_ref[h:h + 1, :])
            dzu = du * _gelu_grad(*zu)
            dzv = dv * _gelu_grad(*zv)
            ucols = slice(h * HEAD, (h + 1) * HEAD)
            vcols = slice(da + h * HEAD, da + (h + 1) * HEAD)
            dz_ref[:, ucols] = dzu.astype(BF16)
            dz_ref[:, vcols] = dzv.astype(BF16)
            dbin_ref[:, ucols] += jnp.sum(dzu, axis=0, keepdims=True)
            dbin_ref[:, vcols] += jnp.sum(dzv, axis=0, keepdims=True)

        gate_b = z_ref[:, 2 * da:3 * da].astype(F32)
        gate_c = z_ref[:, 3 * da:4 * da].astype(F32)
        xt = z_ref[:, 4 * da:5 * da].astype(F32)
        hc = gate_c * xt
        halo = jnp.where(i > 0, pc_ref[...].astype(F32) * px_ref[...].astype(F32), 0.0)
        row = lax.broadcasted_iota(jnp.int32, (tm, da), 0)
        sh1 = _shift_rows_down(hc, halo, 1, row)
        sh2 = _shift_rows_down(hc, halo, 2, row)
        y = cw_ref[0:1, :] * sh2 + cw_ref[1:2, :] * sh1 + cw_ref[2:3, :] * hc
        dyb = dy_ref[1].astype(F32)
        dconv = dyb * gate_b
        nhalo = jnp.where(i < nblk - 1, ndy_ref[...].astype(F32) * nb_ref[...].astype(F32), 0.0)
        dhc = cw_ref[2:3, :] * dconv + cw_ref[1:2, :] * _shift_rows_up(dconv, nhalo, 1, row) + cw_ref[0:1, :] * _shift_rows_up(dconv, nhalo, 2, row)
        dcw_ref[0:1, :] += jnp.sum(dconv * sh2, axis=0, keepdims=True)
        dcw_ref[1:2, :] += jnp.sum(dconv * sh1, axis=0, keepdims=True)
        dcw_ref[2:3, :] += jnp.sum(dconv * hc, axis=0, keepdims=True)
        for col, val in ((2, dyb * y), (3, dhc * xt), (4, dhc * gate_c)):
            cols = slice(col * da, (col + 1) * da)
            dz_ref[:, cols] = val.astype(BF16)
            dbin_ref[:, cols] += jnp.sum(val, axis=0, keepdims=True)

        @pl.when(i == nblk - 1)
        def _():
            for h in range(nh):
                dbs_ref[h] = jnp.broadcast_to(jnp.sum(dbs_ref[h], axis=1, keepdims=True), (CHUNK, HEAD))

    prev = lambda col: pl.BlockSpec((BF16_ROWS, da), lambda i: (jnp.maximum(i * hb - 1, 0), col))
    nxt = lambda i: jnp.minimum((i + 1) * hb, s // BF16_ROWS - 1)
    in_specs = [pl.BlockSpec((tm, zc), lambda i: (i, 0)), prev(3), prev(4), pl.BlockSpec((BF16_ROWS, da), lambda i: (nxt(i), 2)),
                pl.BlockSpec((2, tm, da), lambda i: (0, i, 0)), pl.BlockSpec((None, BF16_ROWS, da), lambda i: (1, nxt(i), 0)),
                _row((nh, CHUNK, CHUNK)), _row((nh, CHUNK, HEAD)), _row((nh, HEAD)), _row((nh, HEAD)), _row((CONV_TAPS, da))]
    out_specs = [pl.BlockSpec((tm, zc), lambda i: (i, 0)), _row((nh, CHUNK, CHUNK)), _row((nh, CHUNK, HEAD)), _row((nh, HEAD)),
                 _row((nh, HEAD)), _row((8, da)), _row((1, zc))]
    out_shape = [jax.ShapeDtypeStruct((s, zc), BF16), jax.ShapeDtypeStruct((nh, CHUNK, CHUNK), F32),
                 jax.ShapeDtypeStruct((nh, CHUNK, HEAD), F32), jax.ShapeDtypeStruct((nh, HEAD), F32),
                 jax.ShapeDtypeStruct((nh, HEAD), F32), jax.ShapeDtypeStruct((8, da), F32), jax.ShapeDtypeStruct((1, zc), F32)]
    return _call(body, name, (nblk,), in_specs, out_specs, out_shape, [], [z, z, z, z, dy, dy, ws_masked, bs_wide, gv, bv, cw], rider)


def _adam_update(g, w, m, v):
    m_new = ADAM_B1 * m + (1.0 - ADAM_B1) * g
    v_new = ADAM_B2 * v + (1.0 - ADAM_B2) * (g * g)
    m_hat = m_new / (1.0 - ADAM_B1 ** ADAM_STEP)
    v_hat = v_new / (1.0 - ADAM_B2 ** ADAM_STEP)
    return -ADAM_LR * (m_hat / (jnp.sqrt(v_hat) + ADAM_EPS) + ADAM_WD * w), m_new, v_new


def _adamw(gparts, w, m, v, name):
    n, r, c = gparts.shape
    tr = r // 4 if (r // 4) % BF16_ROWS == 0 else r

    def body(g_ref, w_ref, m_ref, v_ref, go_ref, d_ref, mo_ref, vo_ref):
        g = g_ref[0].astype(F32)
        for q in range(1, n):
            g = g + g_ref[q].astype(F32)
        go_ref[...] = g
        d_ref[...], mo_ref[...], vo_ref[...] = _adam_update(g, w_ref[...], m_ref[...], v_ref[...])

    blk = pl.BlockSpec((tr, c), lambda i: (i, 0))
    shp = jax.ShapeDtypeStruct((r, c), F32)
    return _call(body, name, (r // tr,), [pl.BlockSpec((n, tr, c), lambda i: (0, i, 0)), blk, blk, blk], [blk] * 4, [shp] * 4, [],
                 [gparts, w, m, v])[0]


def _adamw_small(packs, rows, w, m, v, conv, name):
    n_par, n_dev = len(rows), packs.shape[0]
    taps = conv[0].shape[0]

    def body(*refs):
        refs = list(refs)
        cut = lambda n: [refs.pop(0) for _ in range(n)]
        p_ref, w_refs, m_refs, v_refs, (cw_ref, cm_ref, cv_ref) = refs.pop(0), cut(n_par), cut(n_par), cut(n_par), cut(3)
        outs = [cut(4) for _ in range(n_par + 1)]
        at = 0
        for k in range(n_par):
            g = p_ref[0, at:at + rows[k], :]
            for dev in range(1, n_dev):
                g = g + p_ref[dev, at:at + rows[k], :]
            go_ref, d_ref, mo_ref, vo_ref = outs[k]
            go_ref[...] = g
            d_ref[...], mo_ref[...], vo_ref[...] = _adam_update(g, w_refs[k][...], m_refs[k][...], v_refs[k][...])
            at += rows[k]
        me = 4 * lax.axis_index("x") + 2 * lax.axis_index("y") + lax.axis_index("c")
        go_ref, d_ref, mo_ref, vo_ref = outs[n_par]
        for tap in range(taps):
            row = pl.ds(at + tap * n_dev + me, 1)
            g = p_ref[0, row, :]
            for dev in range(1, n_dev):
                g = g + p_ref[dev, row, :]
            one = slice(tap, tap + 1)
            go_ref[one, :] = g
            d_ref[one, :], mo_ref[one, :], vo_ref[one, :] = _adam_update(g, cw_ref[one, :], cm_ref[one, :], cv_ref[one, :])

    vmem = pl.BlockSpec(memory_space=pltpu.VMEM)
    ins = [packs] + list(w) + list(m) + list(v) + list(conv)
    out_shape = [jax.ShapeDtypeStruct(a.shape, F32) for a in list(w) + [conv[0]] for _ in range(4)]
    outs = pl.pallas_call(body, name=name, in_specs=[vmem] * len(ins), out_specs=[vmem] * len(out_shape), out_shape=out_shape,
                          compiler_params=pltpu.CompilerParams(vmem_limit_bytes=V7X_VMEM_LIMIT))(*ins)
    return [outs[4 * k:4 * k + 4] for k in range(n_par + 1)]


def _rows128(a):
    return a.reshape(-1, LANES)


def kernel(x, ffa_gate, ffa_up, ffa_down, ln_a_g, ln_a_b, w_in, b_in, w_s, b_s, ln_v_g, ln_v_b, conv_w, w_out, b_out, ln_m_g, ln_m_b, ffc_gate, ffc_up, ffc_down, ln_c_g, ln_c_b, loss_target, m_ffa_gate, m_ffa_up, m_ffa_down, m_ln_a_g, m_ln_a_b, m_w_in, m_b_in, m_w_s, m_b_s, m_ln_v_g, m_ln_v_b, m_conv_w, m_w_out, m_b_out, m_ln_m_g, m_ln_m_b, m_ffc_gate, m_ffc_up, m_ffc_down, m_ln_c_g, m_ln_c_b, v_ffa_gate, v_ffa_up, v_ffa_down, v_ln_a_g, v_ln_a_b, v_w_in, v_b_in, v_w_s, v_b_s, v_ln_v_g, v_ln_v_b, v_conv_w, v_w_out, v_b_out, v_ln_m_g, v_ln_m_b, v_ffc_gate, v_ffc_up, v_ffc_down, v_ln_c_g, v_ln_c_b):
    x2, target = x[0], loss_target[0]
    s, d = x2.shape
    da = d // 2
    nh = da // HEAD

    big = dict(ffa_gate=ffa_gate, ffa_up=ffa_up, ffa_down=ffa_down, w_in=w_in, w_out=w_out, ffc_gate=ffc_gate, ffc_up=ffc_up, ffc_down=ffc_down)
    big_m = dict(ffa_gate=m_ffa_gate, ffa_up=m_ffa_up, ffa_down=m_ffa_down, w_in=m_w_in, w_out=m_w_out, ffc_gate=m_ffc_gate, ffc_up=m_ffc_up, ffc_down=m_ffc_down)
    big_v = dict(ffa_gate=v_ffa_gate, ffa_up=v_ffa_up, ffa_down=v_ffa_down, w_in=v_w_in, w_out=v_w_out, ffc_gate=v_ffc_gate, ffc_up=v_ffc_up, ffc_down=v_ffc_down)
    local = lambda k, a: jnp.transpose(a[0]) if k in TRANSPOSED else a[0]
    shard = {k: local(k, w).astype(BF16) for k, w in big.items()}
    conv_rows = jnp.pad(conv_w[0], ((0, 8 - CONV_TAPS), (0, 0)))

    tril = jnp.tril(jnp.ones((CHUNK, CHUNK), dtype=bool))
    ws_masked = jnp.where(tril[None], w_s[0], 0.0).astype(BF16)
    bs_wide = jnp.broadcast_to(b_s[0][:, :, None], (nh, CHUNK, HEAD))
    gv, bv = ln_v_g.reshape(nh, HEAD), ln_v_b.reshape(nh, HEAD)

    full = {}
    xb = x2.astype(BF16)
    (silu_a, udsilu_a, hid_a), (full["ffa_gate"], full["ffa_up"], full["ffa_down"]) = _ffn_gateup_gathering(
        xb, shard["ffa_gate"], shard["ffa_up"], shard["ffa_down"], "ffa_gateup")
    (xhat1, h1b, rstd1), (full["w_in"], full["w_out"], conv_full) = _down_ln(
        hid_a, full["ffa_down"], None, x2, None, ln_a_g, ln_a_b, 0.5, "ffa_down_ln", rider=_Gather([shard["w_in"], shard["w_out"], conv_rows], by_columns=(0,)))
    cw = jnp.transpose(conv_full[:, :CONV_TAPS, :], (1, 0, 2)).reshape(CONV_TAPS, da)
    w_out2 = full["w_out"].reshape(2, da, d)
    z, (full["ffc_gate"],) = _proj_in(h1b, full["w_in"], b_in, "proj_in", _Gather([shard["ffc_gate"]]))
    ycat = _mixer_fwd(z, ws_masked, bs_wide, gv, bv, cw, "mixer_fwd")
    (xhat2, h2b, rstd2), _ = _down_ln(ycat, w_out2, b_out, xhat1, (ln_a_g, ln_a_b), ln_m_g, ln_m_b, 1.0, "proj_out_ln")
    (silu_c, udsilu_c, hid_c), (full["ffc_up"], full["ffc_down"]) = _ffn_gateup_gathering(
        h2b, full["ffc_gate"], shard["ffc_up"], shard["ffc_down"], "ffc_gateup")
    (dr3, dr3b, sq_err, d_ln_c_g, d_ln_c_b), _ = _down_ln(hid_c, full["ffc_down"], None, xhat2, (ln_m_g, ln_m_b), ln_c_g, ln_c_b, 0.5,
                                                          "ffc_down_ln_loss", target=target)
    loss = lax.psum((0.5 / d) * jnp.sum(sq_err), ("x", "y", "c"))

    landed, updated = {}, {}
    adam_side = lambda k: _AdamSide(landed[k], local(k, big[k]), local(k, big_m[k]), local(k, big_v[k]))
    (dg_c, du_c), _ = _ffn_bwd_hidden(dr3b, full["ffc_down"], silu_c, udsilu_c, "ffc_bwd_hidden")
    part, _ = _dw_hidden_rows_paired(hid_c, dr3b, "ffc_dw_down")
    part, (landed["ffc_down"],) = _dw_hidden_rows_paired(dg_c, h2b, "ffc_dw_gate", _ChipScatter(part))
    part, ((landed["ffc_gate"],), updated["ffc_down"]) = _dw_hidden_rows_paired(
        du_c, h2b, "ffc_dw_up", [_ChipScatter(part), adam_side("ffc_down")])
    (dr2, dr2b, d_ln_m_g, d_ln_m_b, d_b_out), (landed["ffc_up"],) = _ffn_bwd_input(
        dg_c, full["ffc_gate"], du_c, full["ffc_up"], dr3, (xhat2, rstd2, ln_m_g), "ffc_bwd_input_ln", _ChipScatter(part))
    dycat = _nt_hidden(dr2b, w_out2, "proj_out_bwd")
    part, _ = _dw_hidden_rows(ycat, dr2b, "proj_out_dw")
    (dz, d_w_s, d_b_s_wide, d_gv, d_bv, d_cw, d_b_in), (landed["w_out"],) = _mixer_bwd(
        z, dycat, ws_masked, bs_wide, gv, bv, cw, "mixer_bwd", _Scatter(part.reshape(N_DEV, d // N_DEV, d)))
    dr1, dr1b, d_ln_a_g, d_ln_a_b = _proj_in_bwd_ln(dz, full["w_in"], dr2, (xhat1, rstd1, ln_a_g), 0.5, "proj_in_bwd_ln")
    small_g = dict(ln_a_g=d_ln_a_g, ln_a_b=d_ln_a_b, b_in=d_b_in, w_s=d_w_s, b_s=d_b_s_wide[:, :, 0], ln_v_g=d_gv, ln_v_b=d_bv, b_out=d_b_out,
                   ln_m_g=d_ln_m_g, ln_m_b=d_ln_m_b, ln_c_g=d_ln_c_g, ln_c_b=d_ln_c_b)
    pack = jnp.concatenate([_rows128(g) for g in small_g.values()] + [_rows128(d_cw[:CONV_TAPS])], axis=0)
    part, ((packs,), updated["ffc_gate"]) = _dw_cols(h1b, dz, "proj_in_dw", [_Scatter(pack, whole=True), adam_side("ffc_gate")])
    (dg_a, du_a), (landed["w_in"],) = _ffn_bwd_hidden(dr1b, full["ffa_down"], silu_a, udsilu_a, "ffa_bwd_hidden", _Scatter(part, by_columns=True))
    part, (updated["ffc_up"],) = _dw_hidden_rows_paired(hid_a, dr1b, "ffa_dw_down", [adam_side("ffc_up")])
    part, ((landed["ffa_down"],), updated["w_in"]) = _dw_hidden_rows_paired(dg_a, xb, "ffa_dw_gate", [_ChipScatter(part), adam_side("w_in")])
    part, ((landed["ffa_gate"],), updated["ffa_down"]) = _dw_hidden_rows_paired(
        du_a, xb, "ffa_dw_up", [_ChipScatter(part), adam_side("ffa_down")])
    grad_x, ((landed["ffa_up"],), updated["ffa_gate"]) = _ffn_bwd_input(
        dg_a, full["ffa_gate"], du_a, full["ffa_up"], dr1, None, "ffa_bwd_input", [_ChipScatter(part), adam_side("ffa_gate")])

    grads, deltas, new_m, new_v = {}, {}, {}, {}
    for k in big:
        out = updated[k] if k in updated else _adamw(landed[k], local(k, big[k]), local(k, big_m[k]), local(k, big_v[k]), "adamw_" + k)
        grads[k], deltas[k], new_m[k], new_v[k] = ((jnp.transpose(o) if k in TRANSPOSED else o).reshape(big[k].shape) for o in out)

    small = dict(ln_a_g=ln_a_g, ln_a_b=ln_a_b, b_in=b_in, w_s=w_s, b_s=b_s, ln_v_g=ln_v_g, ln_v_b=ln_v_b, b_out=b_out,
                 ln_m_g=ln_m_g, ln_m_b=ln_m_b, ln_c_g=ln_c_g, ln_c_b=ln_c_b)
    small_m = dict(ln_a_g=m_ln_a_g, ln_a_b=m_ln_a_b, b_in=m_b_in, w_s=m_w_s, b_s=m_b_s, ln_v_g=m_ln_v_g, ln_v_b=m_ln_v_b, b_out=m_b_out,
                   ln_m_g=m_ln_m_g, ln_m_b=m_ln_m_b, ln_c_g=m_ln_c_g, ln_c_b=m_ln_c_b)
    small_v = dict(ln_a_g=v_ln_a_g, ln_a_b=v_ln_a_b, b_in=v_b_in, w_s=v_w_s, b_s=v_b_s, ln_v_g=v_ln_v_g, ln_v_b=v_ln_v_b, b_out=v_b_out,
                   ln_m_g=v_ln_m_g, ln_m_b=v_ln_m_b, ln_c_g=v_ln_c_g, ln_c_b=v_ln_c_b)
    snames = list(small)
    assert snames == list(small_g) and conv_w.shape[2] == LANES and da == N_DEV * LANES
    views = lambda tree: [_rows128(tree[k]) for k in snames]
    out = _adamw_small(packs, [a.shape[0] for a in views(small)], views(small), views(small_m), views(small_v),
                       (conv_w[0], m_conv_w[0], v_conv_w[0]), "adamw_small")
    for k, per_param in zip(snames + ["conv_w"], out):
        shape = conv_w.shape if k == "conv_w" else small[k].shape
        grads[k], deltas[k], new_m[k], new_v[k] = (o.reshape(shape) for o in per_param)

    order = ["ffa_gate", "ffa_up", "ffa_down", "ln_a_g", "ln_a_b", "w_in", "b_in", "w_s", "b_s", "ln_v_g", "ln_v_b", "conv_w", "w_out", "b_out",
             "ln_m_g", "ln_m_b", "ffc_gate", "ffc_up", "ffc_down", "ln_c_g", "ln_c_b"]
    return (loss, grad_x[None], *[grads[k] for k in order], *[deltas[k] for k in order], *[new_m[k] for k in order], *[new_v[k] for k in order])
```

```python
import math

import jax
import jax.numpy as jnp
from jax import lax
from jax.experimental import pallas as pl
from jax.experimental.pallas import tpu as pltpu

BF16 = jnp.bfloat16
F32 = jnp.float32
MESH = pl.DeviceIdType.MESH

N_DEV = 8
HEAD = 128
CHUNK = 128
CONV_TAPS = 3
LN_EPS = 1e-5
ALPHA = float(2 ** 0.25)
GELU_C = 0.7978845608028654
GELU_A = 0.044715
ADAM_LR, ADAM_B1, ADAM_B2, ADAM_EPS, ADAM_WD, ADAM_STEP = 0.001, 0.9, 0.999, 1e-08, 0.01, 10
V7X_VMEM_LIMIT = 56 * 1024 * 1024
LANES = 128
BF16_ROWS = 16
MXU_COLS = 256
TRANSPOSED = ("ffa_gate", "ffa_up", "ffc_gate", "ffc_up")
PROJ_IN_COLS = 1280
MIXER_TOKENS = 512
DW_TOKENS = 2048

NT_DIMS = (((1,), (1,)), ((), ()))
TN_DIMS = (((0,), (0,)), ((), ()))
ANY = pl.BlockSpec(memory_space=pl.ANY)


def _gelu_tanh(x):
    return jnp.tanh(GELU_C * (x + GELU_A * x * x * x))


def _gelu(x, t):
    return 0.5 * x * (1.0 + t)


def _gelu_grad(x, t):
    return 0.5 * (1.0 + t) + 0.5 * x * (1.0 - t * t) * GELU_C * (1.0 + 3.0 * GELU_A * x * x)


def _sigmoid(x):
    return 0.5 * jnp.tanh(0.5 * x) + 0.5


def _row(shape):
    return pl.BlockSpec(shape, lambda *_: (0,) * len(shape))


def _row_blocks(tm, rows=128):
    rows = min(rows, tm)
    return [slice(r, r + rows) for r in range(0, tm, rows)]


def _ln_backward(dh, xhat, rstd, gain):
    dxh = dh * gain
    m1 = jnp.mean(dxh, axis=-1, keepdims=True)
    m2 = jnp.mean(dxh * xhat, axis=-1, keepdims=True)
    return rstd * (dxh - m1 - xhat * m2)


def _place():
    x, y, c = lax.axis_index("x"), lax.axis_index("y"), lax.axis_index("c")
    return x, y, c, [(1 - x, y), (x, 1 - y), (1 - x, 1 - y)]


def _other_devices(x, y, c):
    flips = [(bx, by, bc) for bx in (0, 1) for by in (0, 1) for bc in (0, 1)][1:]
    return [(1 - x if bx else x, 1 - y if by else y, 1 - c if bc else c) for bx, by, bc in flips]


class _Gather:
    def __init__(self, shards, forward_at=0.75, by_columns=()):
        n = len(shards)
        self.n, self.forward_at, self.by_columns = n, forward_at, tuple(by_columns)
        self.inputs = list(shards)
        self.out_shapes = [jax.ShapeDtypeStruct((a.shape[0], N_DEV * a.shape[1]) if i in self.by_columns else (N_DEV,) + a.shape, a.dtype)
                           for i, a in enumerate(shards)]
        self.scratch = [pltpu.SemaphoreType.DMA((n, 7)), pltpu.SemaphoreType.DMA((n, 7)), pltpu.SemaphoreType.DMA((n,))]

    def _block(self, outs, a, dev):
        if a in self.by_columns:
            cols = outs[a].shape[1] // N_DEV
            return outs[a].at[:, pl.ds(dev * cols, cols)]
        return outs[a].at[dev]

    def _copy(self, outs, sems, a, k, block, to, src=None):
        dst = self._block(outs, a, block)
        return pltpu.make_async_remote_copy(src_ref=dst if src is None else src, dst_ref=dst, send_sem=sems[0].at[a, k],
                                            recv_sem=sems[1].at[a, k], device_id=to, device_id_type=MESH)

    def start(self, ins, outs, sems, urgent=None):
        x, y, c, chips = _place()
        me = 4 * x + 2 * y + c
        for a in range(self.n):
            pltpu.make_async_copy(ins[a], self._block(outs, a, me), sems[2].at[a]).start()
        urgent = list(range(self.n)) if urgent is None else list(urgent)
        for group in (urgent, [a for a in range(self.n) if a not in urgent]):
            for a in group:
                self._copy(outs, sems, a, 0, me, (x, y, 1 - c), src=ins[a]).start()
                for j in (0, 1):
                    self._copy(outs, sems, a, 1 + j, me, (*chips[j], c), src=ins[a]).start()
            for a in group:
                self._copy(outs, sems, a, 3, me, (*chips[2], c), src=ins[a]).start()

    def wait_sibling(self, outs, sems, a):
        x, y, c, _ = _place()
        self._copy(outs, sems, a, 0, 4 * x + 2 * y + 1 - c, (x, y, 1 - c)).wait_recv()

    def pass_on(self, outs, sems, a, j):
        x, y, c, chips = _place()
        block = 4 * chips[j][0] + 2 * chips[j][1] + c
        self._copy(outs, sems, a, 1 + j, block, (x, y, 1 - c)).wait_recv()
        self._copy(outs, sems, a, 4 + j, block, (x, y, 1 - c)).start()

    def wait_passed(self, outs, sems, a, j):
        x, y, c, chips = _place()
        self._copy(outs, sems, a, 4 + j, 4 * chips[j][0] + 2 * chips[j][1] + 1 - c, (x, y, 1 - c)).wait_recv()

    def wait_sent(self, ins, outs, sems, a):
        x, y, c, _ = _place()
        me = 4 * x + 2 * y + c
        for k in range(7):
            self._copy(outs, sems, a, k, me, (x, y, 1 - c), src=ins[a]).wait_send()
        pltpu.make_async_copy(ins[a], self._block(outs, a, me), sems[2].at[a]).wait()

    def forward(self, ins, outs, sems):
        for a in range(self.n):
            for j in range(3):
                self.pass_on(outs, sems, a, j)

    def finish(self, ins, outs, sems):
        for a in range(self.n):
            self.wait_sibling(outs, sems, a)
            for j in range(3):
                self.wait_passed(outs, sems, a, j)
        for a in range(self.n):
            self.wait_sent(ins, outs, sems, a)

    def before(self, step, n_steps, ins, outs, sems):
        pl.when(step == 0)(lambda: self.start(ins, outs, sems))
        pl.when(step == int(self.forward_at * (n_steps - 1)))(lambda: self.forward(ins, outs, sems))

    def after(self, step, n_steps, ins, outs, sems):
        pl.when(step == n_steps - 1)(lambda: self.finish(ins, outs, sems))


class _Scatter:
    def __init__(self, partial, whole=False, by_columns=False):
        self.whole, self.by_columns = whole, by_columns
        self.inputs = [partial]
        if whole:
            shape = (N_DEV,) + partial.shape
        elif by_columns:
            shape = (N_DEV, partial.shape[0], partial.shape[1] // N_DEV)
        else:
            shape = partial.shape
        self.out_shapes = [jax.ShapeDtypeStruct(shape, partial.dtype)]
        self.scratch = [pltpu.SemaphoreType.DMA((7,)), pltpu.SemaphoreType.DMA((7,)), pltpu.SemaphoreType.DMA((1,))]

    def _copies(self, ins, outs, sems):
        x, y, c, _ = _place()
        me = 4 * x + 2 * y + c
        if self.whole:
            block = lambda dev: ins[0]
        elif self.by_columns:
            cols = ins[0].shape[1] // N_DEV
            block = lambda dev: ins[0].at[:, pl.ds(dev * cols, cols)]
        else:
            block = lambda dev: ins[0].at[dev]
        mine = pltpu.make_async_copy(block(me), outs[0].at[me], sems[2].at[0])
        remote = [pltpu.make_async_remote_copy(src_ref=block(4 * px + 2 * py + pc), dst_ref=outs[0].at[me], send_sem=sems[0].at[k],
                                               recv_sem=sems[1].at[k], device_id=(px, py, pc), device_id_type=MESH)
                  for k, (px, py, pc) in enumerate(_other_devices(x, y, c))]
        return mine, remote

    def start(self, ins, outs, sems):
        mine, remote = self._copies(ins, outs, sems)
        mine.start()
        for cp in remote:
            cp.start()

    def finish(self, ins, outs, sems):
        mine, remote = self._copies(ins, outs, sems)
        for cp in remote:
            cp.wait()
        mine.wait()

    def before(self, step, n_steps, ins, outs, sems):
        pl.when(step == 0)(lambda: self.start(ins, outs, sems))

    def after(self, step, n_steps, ins, outs, sems):
        pl.when(step == n_steps - 1)(lambda: self.finish(ins, outs, sems))


class _ChipScatter(_Scatter):
    def __init__(self, sums):
        super().__init__(sums)
        self.scratch = [pltpu.SemaphoreType.DMA((3,)), pltpu.SemaphoreType.DMA((3,)), pltpu.SemaphoreType.DMA((1,))]

    def _copies(self, ins, outs, sems):
        x, y, c, chips = _place()
        my_chip = 2 * x + y
        mine = pltpu.make_async_copy(ins[0].at[my_chip], outs[0].at[my_chip], sems[2].at[0])
        remote = [pltpu.make_async_remote_copy(src_ref=ins[0].at[2 * px + py], dst_ref=outs[0].at[my_chip], send_sem=sems[0].at[k],
                                               recv_sem=sems[1].at[k], device_id=(px, py, c), device_id_type=MESH)
                  for k, (px, py) in enumerate(chips)]
        return mine, remote


def _call(body, name, grid, in_specs, out_specs, out_shape, scratch, ins, rider=None):
    single = not isinstance(out_shape, (list, tuple))
    out_shape = [out_shape] if single else list(out_shape)
    out_specs = [out_specs] if single else list(out_specs)
    params = pltpu.CompilerParams(dimension_semantics=("arbitrary",) * len(grid), vmem_limit_bytes=V7X_VMEM_LIMIT)
    if rider is None:
        outs = pl.pallas_call(body, name=name, grid=grid, in_specs=in_specs, out_specs=out_specs, out_shape=out_shape,
                              scratch_shapes=scratch, compiler_params=params)(*ins)
        return (outs[0] if single else outs), None
    n_in, n_out, n_scr = len(ins), len(out_shape), len(scratch)
    r_in, r_out = len(rider.inputs), len(rider.out_shapes)
    n_steps = math.prod(grid)

    def carried(*refs):
        refs = list(refs)
        cut = lambda n: [refs.pop(0) for _ in range(n)]
        b_in, c_in, b_out, c_out, b_scr = cut(n_in), cut(r_in), cut(n_out), cut(r_out), cut(n_scr)
        step = 0
        for axis, size in enumerate(grid):
            step = step * size + pl.program_id(axis)
        rider.before(step, n_steps, c_in, c_out, refs)
        body(*b_in, *b_out, *b_scr)
        rider.after(step, n_steps, c_in, c_out, refs)

    outs = pl.pallas_call(
        carried, name=name, grid=grid, in_specs=list(in_specs) + [ANY] * r_in, out_specs=out_specs + [ANY] * r_out,
        out_shape=out_shape + rider.out_shapes, scratch_shapes=list(scratch) + rider.scratch, compiler_params=params,
    )(*ins, *rider.inputs)
    base = outs[:n_out]
    return (base[0] if single else base), outs[n_out:]


def _arrival_block(j):
    x, y, c = lax.axis_index("x"), lax.axis_index("y"), lax.axis_index("c")
    chip, other_core = j // 2, j % 2
    px = jnp.where((chip == 1) | (chip == 3), 1 - x, x)
    py = jnp.where((chip == 2) | (chip == 3), 1 - y, y)
    pc = jnp.where(other_core == 1, 1 - c, c)
    return 4 * px + 2 * py + pc


def _ffn_gateup_gathering(xb, gate, up_shard, down_shard, name):
    s, d = xb.shape
    fs = up_shard.shape[0]
    tm = min(s, 1024)
    ni = s // tm
    ask_at = max(ni - 1, 0)
    gate_here = gate.ndim == 2
    gather = _Gather(([gate] if gate_here else []) + [up_shard, down_shard])
    n_g = gather.n
    used_here, down = tuple(range(n_g - 1)), n_g - 1

    def body(x_ref, *refs):
        refs = list(refs)
        gate_full = None if gate_here else refs.pop(0)
        shards = [refs.pop(0) for _ in range(n_g)]
        silu_ref, udsilu_ref, h_ref = refs.pop(0), refs.pop(0), refs.pop(0)
        fulls = [refs.pop(0) for _ in range(n_g)]
        w_ref, w_sems = refs.pop(0), refs.pop(0)
        sems = refs
        j, i = pl.program_id(0), pl.program_id(1)
        gate_src, up_src = (fulls[0], fulls[1]) if gate_here else (gate_full, fulls[0])

        def load(slot, srcs):
            return [pltpu.make_async_copy(src, w_ref.at[slot, a], w_sems.at[slot, a]) for a, src in enumerate(srcs)]

        @pl.when((j == 0) & (i == 0))
        def _():
            gather.start(shards, fulls, sems, urgent=used_here)
            mine = load(0, (shards[0] if gate_here else gate_full.at[_arrival_block(0)], shards[n_g - 2]))
            for cp in mine:
                cp.start()
            for cp in mine:
                cp.wait()

        for nxt in range(1, N_DEV):
            @pl.when((j == nxt) & (i == 0))
            def _(nxt=nxt):
                for cp in load(nxt % 2, (gate_src.at[0], up_src.at[0])):
                    cp.wait()

        for nxt in range(1, N_DEV):
            @pl.when((j == nxt - 1) & (i == ask_at))
            def _(nxt=nxt):
                for a in used_here:
                    if nxt == 1:
                        gather.wait_sibling(fulls, sems, a)
                    elif nxt % 2 == 0:
                        gather.pass_on(fulls, sems, a, nxt // 2 - 1)
                    else:
                        gather.wait_passed(fulls, sems, a, nxt // 2 - 1)
                block = _arrival_block(nxt)
                for cp in load(nxt % 2, (gate_src.at[block], up_src.at[block])):
                    cp.start()

        @pl.when((j == N_DEV - 1) & (i == ask_at))
        def _():
            for other_chip in range(3):
                gather.pass_on(fulls, sems, down, other_chip)

        x = x_ref[...]
        g = lax.dot_general(x, w_ref[j % 2, 0], NT_DIMS, preferred_element_type=F32)
        u = lax.dot_general(x, w_ref[j % 2, 1], NT_DIMS, preferred_element_type=F32)
        sg = _sigmoid(g)
        silu = g * sg
        silu_ref[...] = silu.astype(BF16)
        udsilu_ref[...] = (u * (sg + silu * (1.0 - sg))).astype(BF16)
        h_ref[...] = (silu * u).astype(BF16)

        @pl.when((j == N_DEV - 1) & (i == ni - 1))
        def _():
            gather.wait_sibling(fulls, sems, down)
            for other_chip in range(3):
                gather.wait_passed(fulls, sems, down, other_chip)
            for a in range(n_g):
                gather.wait_sent(shards, fulls, sems, a)

    shp = jax.ShapeDtypeStruct((N_DEV, s, fs), BF16)
    o_spec = pl.BlockSpec((None, tm, fs), lambda j, i: (_arrival_block(j), i, 0))
    ins = ([] if gate_here else [gate]) + gather.inputs
    outs = pl.pallas_call(
        body, name=name, grid=(N_DEV, ni), in_specs=[pl.BlockSpec((tm, d), lambda j, i: (i, 0))] + [ANY] * len(ins),
        out_specs=[o_spec, o_spec, o_spec] + [ANY] * n_g, out_shape=[shp, shp, shp] + gather.out_shapes,
        scratch_shapes=[pltpu.VMEM((2, 2, fs, d), BF16), pltpu.SemaphoreType.DMA((2, 2))] + gather.scratch,
        compiler_params=pltpu.CompilerParams(dimension_semantics=("arbitrary", "arbitrary"), vmem_limit_bytes=V7X_VMEM_LIMIT),
    )(xb, *ins)
    return outs[:3], outs[3:]


def _down_ln(a3, w3, bias, res, res_affine, ln_g, ln_b, scale, name, target=None, rider=None):
    nk, s, tk = a3.shape
    d = w3.shape[2]
    tm = min(s, 256)
    final = target is not None

    def body(*refs):
        refs = list(refs)
        a_ref, w_hbm = refs[:2]
        del refs[:2]
        bias_ref = refs.pop(0) if bias is not None else None
        res_ref = refs.pop(0)
        rg_ref, rb_ref = (refs.pop(0), refs.pop(0)) if res_affine is not None else (None, None)
        g_ref, b_ref = refs.pop(0), refs.pop(0)
        t_ref = refs.pop(0) if final else None
        w_sem = refs.pop()
        w_ref = refs.pop()
        i = pl.program_id(0)
        if final:
            dr_ref, drb_ref, sq_ref, dg_ref, db_ref = refs
        else:
            xh_ref, hb_ref, rstd_ref = refs

        @pl.when(i == 0)
        def _():
            whole = pltpu.make_async_copy(w_hbm, w_ref, w_sem.at[0])
            whole.start()
            whole.wait()
            if final:
                sq_ref[...] = jnp.zeros_like(sq_ref)
                dg_ref[...] = jnp.zeros_like(dg_ref)
                db_ref[...] = jnp.zeros_like(db_ref)

        y = jnp.dot(a_ref[0], w_ref[0], preferred_element_type=F32)
        for k in range(1, nk):
            y = y + jnp.dot(a_ref[k], w_ref[k], preferred_element_type=F32)
        if bias_ref is not None:
            y = y + bias_ref[...]
        for rows in _row_blocks(tm):
            r = res_ref[rows, :]
            if rg_ref is not None:
                r = r * rg_ref[...] + rb_ref[...]
            r = ALPHA * r + scale * y[rows]
            mu = jnp.mean(r, axis=-1, keepdims=True)
            c = r - mu
            var = jnp.mean(c * c, axis=-1, keepdims=True)
            rstd = lax.rsqrt(var + LN_EPS)
            xhat = c * rstd
            h = xhat * g_ref[...] + b_ref[...]
            if not final:
                xh_ref[rows, :] = xhat
                hb_ref[rows, :] = h.astype(BF16)
                rstd_ref[rows, :] = rstd
            else:
                err = h - t_ref[rows, :]
                sq_ref[...] += jnp.sum(err * err, axis=0, keepdims=True)
                dh = err * (1.0 / d)
                dg_ref[...] += jnp.sum(dh * xhat, axis=0, keepdims=True)
                db_ref[...] += jnp.sum(dh, axis=0, keepdims=True)
                dr = _ln_backward(dh, xhat, rstd, g_ref[...])
                dr_ref[rows, :] = dr
                drb_ref[rows, :] = (scale * dr).astype(BF16)

    tok = pl.BlockSpec((tm, d), lambda i: (i, 0))
    vec = pl.BlockSpec((1, d), lambda i: (0, 0))
    ins = [a3, w3]
    in_specs = [pl.BlockSpec((nk, tm, tk), lambda i: (0, i, 0)), ANY]
    if bias is not None:
        ins.append(bias)
        in_specs.append(vec)
    ins.append(res)
    in_specs.append(tok)
    if res_affine is not None:
        ins += list(res_affine)
        in_specs += [vec, vec]
    ins += [ln_g, ln_b]
    in_specs += [vec, vec]
    if final:
        ins.append(target)
        in_specs.append(tok)
        out_shape = [jax.ShapeDtypeStruct((s, d), F32), jax.ShapeDtypeStruct((s, d), BF16)] + [jax.ShapeDtypeStruct((1, d), F32)] * 3
        out_specs = [tok, tok, vec, vec, vec]
    else:
        out_shape = [jax.ShapeDtypeStruct((s, d), F32), jax.ShapeDtypeStruct((s, d), BF16), jax.ShapeDtypeStruct((s, 1), F32)]
        out_specs = [tok, tok, pl.BlockSpec((tm, 1), lambda i: (i, 0))]
    scratch = [pltpu.VMEM((nk, tk, d), BF16), pltpu.SemaphoreType.DMA((1,))]
    return _call(body, name, (s // tm,), in_specs, out_specs, out_shape, scratch, ins, rider)


def _proj_in(hb, w, bias, name, rider=None):
    s, d = hb.shape
    n = w.shape[1]
    tm = min(s, 1024)
    tn = PROJ_IN_COLS if n % PROJ_IN_COLS == 0 else n

    def body(h_ref, w_ref, b_ref, z_ref):
        z_ref[...] = (jnp.dot(h_ref[...], w_ref[...], preferred_element_type=F32) + b_ref[...]).astype(BF16)

    in_specs = [pl.BlockSpec((tm, d), lambda i, j: (i, 0)), pl.BlockSpec((d, tn), lambda i, j: (0, j)),
                pl.BlockSpec((1, tn), lambda i, j: (0, j))]
    return _call(body, name, (s // tm, n // tn), in_specs, pl.BlockSpec((tm, tn), lambda i, j: (i, j)),
                 jax.ShapeDtypeStruct((s, n), BF16), [], [hb, w, bias], rider)


def _nt_hidden(ab, w3, name):
    s, kdim = ab.shape
    nj, tn, _ = w3.shape
    tm = min(s, 1024)

    def body(a_ref, w_ref, o_ref):
        o_ref[...] = lax.dot_general(a_ref[...], w_ref[...], NT_DIMS, preferred_element_type=F32).astype(BF16)

    in_specs = [pl.BlockSpec((tm, kdim), lambda i, j: (i, 0)), pl.BlockSpec((None, tn, kdim), lambda i, j: (j, 0, 0))]
    return _call(body, name, (s // tm, nj), in_specs, pl.BlockSpec((None, tm, tn), lambda i, j: (j, i, 0)),
                 jax.ShapeDtypeStruct((nj, s, tn), BF16), [], [ab, w3])[0]


def _ffn_bwd_hidden(ab, w3, silu3, udsilu3, name, rider=None):
    s, kdim = ab.shape
    nj, tn, _ = w3.shape
    tm = min(s, 1024)

    def body(a_ref, w_ref, silu_ref, udsilu_ref, dg_ref, du_ref):
        a = a_ref[...]
        for c0 in range(0, tn, MXU_COLS):
            cols = slice(c0, min(c0 + MXU_COLS, tn))
            t = lax.dot_general(a, w_ref[cols, :], NT_DIMS, preferred_element_type=F32)
            du_ref[:, cols] = (t * silu_ref[:, cols].astype(F32)).astype(BF16)
            dg_ref[:, cols] = (t * udsilu_ref[:, cols].astype(F32)).astype(BF16)

    hid = pl.BlockSpec((None, tm, tn), lambda i, j: (j, i, 0))
    shp = jax.ShapeDtypeStruct((nj, s, tn), BF16)
    in_specs = [pl.BlockSpec((tm, kdim), lambda i, j: (i, 0)), pl.BlockSpec((None, tn, kdim), lambda i, j: (j, 0, 0)), hid, hid]
    return _call(body, name, (s // tm, nj), in_specs, [hid, hid], [shp, shp], [], [ab, w3, silu3, udsilu3], rider)


def _tn_dw(a, a_spec, b, b_spec, nj, m, n, s, tk, name, rider):
    def body(a_ref, b_ref, o_ref, acc_ref):
        k = pl.program_id(1)

        @pl.when(k == 0)
        def _():
            acc_ref[...] = jnp.zeros_like(acc_ref)

        acc_ref[...] += lax.dot_general(a_ref[...], b_ref[...], TN_DIMS, preferred_element_type=F32)

        @pl.when(k == s // tk - 1)
        def _():
            o_ref[...] = acc_ref[...].astype(BF16)

    return _call(body, name, (nj, s // tk), [a_spec, b_spec], pl.BlockSpec((None, m, n), lambda j, k: (j, 0, 0)),
                 jax.ShapeDtypeStruct((nj, m, n), BF16), [pltpu.VMEM((m, n), F32)], [a, b], rider)


def _dw_hidden_rows(hid3, db, name, rider=None):
    nj, s, fs = hid3.shape
    d = db.shape[1]
    tk = min(s, DW_TOKENS)
    return _tn_dw(hid3, pl.BlockSpec((None, tk, fs), lambda j, k: (j, k, 0)), db, pl.BlockSpec((tk, d), lambda j, k: (k, 0)),
                  nj, fs, d, s, tk, name, rider)


def _dw_hidden_rows_paired(hid3, db, name, rider=None):
    nj, s, fs = hid3.shape
    d = db.shape[1]
    tk = min(s, DW_TOKENS)
    nk = s // tk
    half = nj // 2

    def device_of(j):
        c = lax.axis_index("c")
        return 2 * (j % half) + jnp.where(j < half, 1 - c, c)

    def body(a_ref, b_ref, o_ref, theirs_ref, acc_ref, stage_ref, got_ref, send_sems, recv_sems, load_sem):
        j, k = pl.program_id(0), pl.program_id(1)
        x, y, c = lax.axis_index("x"), lax.axis_index("y"), lax.axis_index("c")

        def to_sibling(q):
            return pltpu.make_async_remote_copy(src_ref=stage_ref, dst_ref=theirs_ref.at[q], send_sem=send_sems.at[q],
                                                recv_sem=recv_sems.at[q], device_id=(x, y, 1 - c), device_id_type=MESH)

        def fetch(q):
            return pltpu.make_async_copy(theirs_ref.at[q], got_ref, load_sem.at[0])

        @pl.when(k == 0)
        def _():
            acc_ref[...] = jnp.zeros_like(acc_ref)

        acc_ref[...] += lax.dot_general(a_ref[...], b_ref[...], TN_DIMS, preferred_element_type=F32)

        for q in range(half):
            @pl.when((j == q) & (k == nk - 1))
            def _(q=q):
                if q > 0:
                    to_sibling(q - 1).wait_send()
                stage_ref[...] = acc_ref[...].astype(BF16)
                to_sibling(q).start()

            @pl.when((j == half + q) & (k == 0))
            def _(q=q):
                to_sibling(q).wait_recv()
                fetch(q).start()

            @pl.when((j == half + q) & (k == nk - 1))
            def _(q=q):
                if q == 0:
                    to_sibling(half - 1).wait_send()
                fetch(q).wait()
                o_ref[...] = (acc_ref[...] + got_ref[...].astype(F32)).astype(BF16)

    in_specs = [pl.BlockSpec((None, tk, fs), lambda j, k: (device_of(j), k, 0)), pl.BlockSpec((tk, d), lambda j, k: (k, 0))]
    out_specs = [pl.BlockSpec((None, fs, d), lambda j, k: (jnp.maximum(j - half, 0), 0, 0)), ANY]
    shp = jax.ShapeDtypeStruct((half, fs, d), BF16)
    scratch = [pltpu.VMEM((fs, d), F32), pltpu.VMEM((fs, d), BF16), pltpu.VMEM((fs, d), BF16),
               pltpu.SemaphoreType.DMA((half,)), pltpu.SemaphoreType.DMA((half,)), pltpu.SemaphoreType.DMA((1,))]
    (sums, _), riders_out = _call(body, name, (nj, nk), in_specs, out_specs, [shp, shp], scratch, [hid3, db], rider)
    return sums, riders_out


def _dw_cols(ab, dz, name, rider=None):
    s, d = ab.shape
    n = dz.shape[1]
    tn = PROJ_IN_COLS if n % PROJ_IN_COLS == 0 else n
    tk = min(s, DW_TOKENS)
    tr = d // 2

    def body(a_ref, b_ref, o_ref, acc_ref):
        k = pl.program_id(2)

        @pl.when(k == 0)
        def _():
            acc_ref[...] = jnp.zeros_like(acc_ref)

        acc_ref[...] += lax.dot_general(a_ref[...], b_ref[...], TN_DIMS, preferred_element_type=F32)

        @pl.when(k == s // tk - 1)
        def _():
            o_ref[...] = acc_ref[...].astype(BF16)

    in_specs = [pl.BlockSpec((tk, tr), lambda j, r, k: (k, r)), pl.BlockSpec((tk, tn), lambda j, r, k: (k, j))]
    return _call(body, name, (n // tn, d // tr, s // tk), in_specs, pl.BlockSpec((tr, tn), lambda j, r, k: (r, j)),
                 jax.ShapeDtypeStruct((d, n), BF16), [pltpu.VMEM((tr, tn), F32)], [ab, dz], rider)


def _ffn_bwd_input(dg3, wg3, du3, wu3, dres, ln, name, rider=None):
    s, d = dres.shape
    nk, _, fs = dg3.shape
    tm = min(s, 512)

    def body(*refs):
        refs = list(refs)
        dg_in, wg_ref, du_in, wu_ref, dres_ref = refs[:5]
        del refs[:5]
        if ln is not None:
            xh_ref, rstd_ref, gain_ref = refs.pop(0), refs.pop(0), refs.pop(0)
        acc_ref = refs.pop()
        i, k = pl.program_id(0), pl.program_id(1)

        @pl.when(k == 0)
        def _():
            acc_ref[...] = jnp.zeros_like(acc_ref)

        acc_ref[...] += (jnp.dot(dg_in[...], wg_ref[...], preferred_element_type=F32)
                         + jnp.dot(du_in[...], wu_ref[...], preferred_element_type=F32))

        @pl.when(k == nk - 1)
        def _():
            if ln is not None:
                dr_ref, drb_ref, dg_ref, db_ref, sum_ref = refs

                @pl.when(i == 0)
                def _():
                    dg_ref[...] = jnp.zeros_like(dg_ref)
                    db_ref[...] = jnp.zeros_like(db_ref)
                    sum_ref[...] = jnp.zeros_like(sum_ref)

            for rows in _row_blocks(tm):
                dh = ALPHA * dres_ref[rows, :] + acc_ref[rows, :]
                if ln is None:
                    refs[0][rows, :] = dh
                else:
                    xhat = xh_ref[rows, :]
                    dg_ref[...] += jnp.sum(dh * xhat, axis=0, keepdims=True)
                    db_ref[...] += jnp.sum(dh, axis=0, keepdims=True)
                    dr = _ln_backward(dh, xhat, rstd_ref[rows, :], gain_ref[...])
                    sum_ref[...] += jnp.sum(dr, axis=0, keepdims=True)
                    dr_ref[rows, :] = dr
                    drb_ref[rows, :] = dr.astype(BF16)

    tok = pl.BlockSpec((tm, d), lambda i, k: (i, 0))
    vec = pl.BlockSpec((1, d), lambda i, k: (0, 0))
    a_spec = pl.BlockSpec((None, tm, fs), lambda i, k: (k, i, 0))
    w_spec = pl.BlockSpec((None, fs, d), lambda i, k: (k, 0, 0))
    ins, in_specs = [dg3, wg3, du3, wu3, dres], [a_spec, w_spec, a_spec, w_spec, tok]
    if ln is None:
        out_shape, out_specs = jax.ShapeDtypeStruct((s, d), F32), tok
    else:
        ins += list(ln)
        in_specs += [tok, pl.BlockSpec((tm, 1), lambda i, k: (i, 0)), vec]
        out_shape = [jax.ShapeDtypeStruct((s, d), F32), jax.ShapeDtypeStruct((s, d), BF16)] + [jax.ShapeDtypeStruct((1, d), F32)] * 3
        out_specs = [tok, tok, vec, vec, vec]
    return _call(body, name, (s // tm, nk), in_specs, out_specs, out_shape, [pltpu.VMEM((tm, d), F32)], ins, rider)


def _proj_in_bwd_ln(dz, w, dres, ln, branch_scale, name):
    s, d = dres.shape
    n = w.shape[1]
    tm = min(s, 256)

    def body(dz_ref, w_hbm, dres_ref, xh_ref, rstd_ref, gain_ref, dr_ref, drb_ref, dg_ref, db_ref, w_ref, w_sem):
        @pl.when(pl.program_id(0) == 0)
        def _():
            whole = pltpu.make_async_copy(w_hbm, w_ref, w_sem.at[0])
            whole.start()
            whole.wait()
            dg_ref[...] = jnp.zeros_like(dg_ref)
            db_ref[...] = jnp.zeros_like(db_ref)

        acc = lax.dot_general(dz_ref[...], w_ref[...], NT_DIMS, preferred_element_type=F32)
        for rows in _row_blocks(tm):
            dh = ALPHA * dres_ref[rows, :] + acc[rows]
            xhat = xh_ref[rows, :]
            dg_ref[...] += jnp.sum(dh * xhat, axis=0, keepdims=True)
            db_ref[...] += jnp.sum(dh, axis=0, keepdims=True)
            dr = _ln_backward(dh, xhat, rstd_ref[rows, :], gain_ref[...])
            dr_ref[rows, :] = dr
            drb_ref[rows, :] = (branch_scale * dr).astype(BF16)

    tok = pl.BlockSpec((tm, d), lambda i: (i, 0))
    vec = pl.BlockSpec((1, d), lambda i: (0, 0))
    in_specs = [pl.BlockSpec((tm, n), lambda i: (i, 0)), ANY, tok, tok, pl.BlockSpec((tm, 1), lambda i: (i, 0)), vec]
    out_shape = [jax.ShapeDtypeStruct((s, d), F32), jax.ShapeDtypeStruct((s, d), BF16)] + [jax.ShapeDtypeStruct((1, d), F32)] * 2
    scratch = [pltpu.VMEM(w.shape, BF16), pltpu.SemaphoreType.DMA((1,))]
    return _call(body, name, (s // tm,), in_specs, [tok, tok, vec, vec], out_shape, scratch, [dz, w, dres] + list(ln))[0]


def _shift_rows_down(v, halo, k, row):
    out = pltpu.roll(v, k, 0)
    hr = halo.shape[0]
    for r in range(k):
        out = jnp.where(row == r, halo[hr - k + r:hr - k + r + 1, :], out)
    return out


def _shift_rows_up(v, halo, k, row):
    t = v.shape[0]
    out = pltpu.roll(v, t - k, 0)
    for r in range(k):
        out = jnp.where(row == t - k + r, halo[r:r + 1, :], out)
    return out


def _sgu_head_forward(z_ref, h, da, gv_ref, bv_ref):
    zu = z_ref[:, h * HEAD:(h + 1) * HEAD].astype(F32)
    zv = z_ref[:, da + h * HEAD:da + (h + 1) * HEAD].astype(F32)
    tu, tv = _gelu_tanh(zu), _gelu_tanh(zv)
    u = _gelu(zu, tu)
    v = _gelu(zv, tv)
    mu = jnp.mean(v, axis=-1, keepdims=True)
    c = v - mu
    rstd = lax.rsqrt(jnp.mean(c * c, axis=-1, keepdims=True) + LN_EPS)
    vhat = c * rstd
    vln = (vhat * gv_ref[h:h + 1, :] + bv_ref[h:h + 1, :]).astype(BF16)
    return (zu, tu), (zv, tv), u, vhat, rstd, vln


def _mixer_fwd(z, ws_masked, bs_wide, gv, bv, cw, name):
    s, zc = z.shape
    da = zc // 5
    nh = da // HEAD
    tm = min(s, MIXER_TOKENS)
    hb = tm // BF16_ROWS

    def body(z_ref, pc_ref, px_ref, ws_ref, bs_ref, gv_ref, bv_ref, cw_ref, y_ref):
        i = pl.program_id(0)
        for h in range(nh):
            _, _, u, _, _, vln = _sgu_head_forward(z_ref, h, da, gv_ref, bv_ref)
            for n in range(tm // CHUNK):
                rows = slice(n * CHUNK, (n + 1) * CHUNK)
                mixed = jnp.dot(ws_ref[h], vln[rows], preferred_element_type=F32) + bs_ref[h]
                y_ref[0, rows, h * HEAD:(h + 1) * HEAD] = (u[rows] * mixed).astype(BF16)
        gate_b = z_ref[:, 2 * da:3 * da].astype(F32)
        hc = z_ref[:, 3 * da:4 * da].astype(F32) * z_ref[:, 4 * da:5 * da].astype(F32)
        halo = jnp.where(i > 0, pc_ref[...].astype(F32) * px_ref[...].astype(F32), 0.0)
        row = lax.broadcasted_iota(jnp.int32, (tm, da), 0)
        y = cw_ref[0:1, :] * _shift_rows_down(hc, halo, 2, row) + cw_ref[1:2, :] * _shift_rows_down(hc, halo, 1, row) + cw_ref[2:3, :] * hc
        y_ref[1] = (gate_b * y).astype(BF16)

    prev = lambda col: pl.BlockSpec((BF16_ROWS, da), lambda i: (jnp.maximum(i * hb - 1, 0), col))
    in_specs = [pl.BlockSpec((tm, zc), lambda i: (i, 0)), prev(3), prev(4), _row((nh, CHUNK, CHUNK)), _row((nh, CHUNK, HEAD)),
                _row((nh, HEAD)), _row((nh, HEAD)), _row((CONV_TAPS, da))]
    return _call(body, name, (s // tm,), in_specs, pl.BlockSpec((2, tm, da), lambda i: (0, i, 0)),
                 jax.ShapeDtypeStruct((2, s, da), BF16), [], [z, z, z, ws_masked, bs_wide, gv, bv, cw])[0]


def _mixer_bwd(z, dy, ws_masked, bs_wide, gv, bv, cw, name, rider=None):
    s, zc = z.shape
    da = zc // 5
    nh = da // HEAD
    tm = min(s, MIXER_TOKENS)
    hb = tm // BF16_ROWS
    nblk = s // tm

    def body(z_ref, pc_ref, px_ref, nb_ref, dy_ref, ndy_ref, ws_ref, bs_ref, gv_ref, bv_ref, cw_ref,
             dz_ref, dws_ref, dbs_ref, dgv_ref, dbv_ref, dcw_ref, dbin_ref):
        i = pl.program_id(0)

        @pl.when(i == 0)
        def _():
            for ref in (dws_ref, dbs_ref, dgv_ref, dbv_ref, dcw_ref, dbin_ref):
                ref[...] = jnp.zeros_like(ref)

        causal = lax.broadcasted_iota(jnp.int32, (CHUNK, CHUNK), 0) >= lax.broadcasted_iota(jnp.int32, (CHUNK, CHUNK), 1)
        for h in range(nh):
            zu, zv, u, vhat, rstd, vln = _sgu_head_forward(z_ref, h, da, gv_ref, bv_ref)
            dya = dy_ref[0, :, h * HEAD:(h + 1) * HEAD].astype(F32)
            w = ws_ref[h]
            du_parts, dvln_parts = [], []
            for n in range(tm // CHUNK):
                rows = slice(n * CHUNK, (n + 1) * CHUNK)
                mixed = jnp.dot(w, vln[rows], preferred_element_type=F32) + bs_ref[h]
                du_parts.append(dya[rows] * mixed)
                dmix = dya[rows] * u[rows]
                dmix_b = dmix.astype(BF16)
                dws_ref[h] += jnp.where(causal, lax.dot_general(dmix_b, vln[rows], NT_DIMS, preferred_element_type=F32), 0.0)
                dbs_ref[h] += dmix
                dvln_parts.append(lax.dot_general(w, dmix_b, TN_DIMS, preferred_element_type=F32))
            du = jnp.concatenate(du_parts, axis=0)
            dvln = jnp.concatenate(dvln_parts, axis=0)
            dgv_ref[h:h + 1, :] += jnp.sum(dvln * vhat, axis=0, keepdims=True)
            dbv_ref[h:h + 1, :] += jnp.sum(dvln, axis=0, keepdims=True)
            dv = _ln_backward(dvln, vhat, rstd, gv_ref[h:h + 1, :])
            dzu = du * _gelu_grad(*zu)
            dzv = dv * _gelu_grad(*zv)
            ucols = slice(h * HEAD, (h + 1) * HEAD)
            vcols = slice(da + h * HEAD, da + (h + 1) * HEAD)
            dz_ref[:, ucols] = dzu.astype(BF16)
            dz_ref[:, vcols] = dzv.astype(BF16)
            dbin_ref[:, ucols] += jnp.sum(dzu, axis=0, keepdims=True)
            dbin_ref[:, vcols] += jnp.sum(dzv, axis=0, keepdims=True)

        gate_b = z_ref[:, 2 * da:3 * da].astype(F32)
        gate_c = z_ref[:, 3 * da:4 * da].astype(F32)
        xt = z_ref[:, 4 * da:5 * da].astype(F32)
        hc = gate_c * xt
        halo = jnp.where(i > 0, pc_ref[...].astype(F32) * px_ref[...].astype(F32), 0.0)
        row = lax.broadcasted_iota(jnp.int32, (tm, da), 0)
        sh1 = _shift_rows_down(hc, halo, 1, row)
        sh2 = _shift_rows_down(hc, halo, 2, row)
        y = cw_ref[0:1, :] * sh2 + cw_ref[1:2, :] * sh1 + cw_ref[2:3, :] * hc
        dyb = dy_ref[1].astype(F32)
        dconv = dyb * gate_b
        nhalo = jnp.where(i < nblk - 1, ndy_ref[...].astype(F32) * nb_ref[...].astype(F32), 0.0)
        dhc = cw_ref[2:3, :] * dconv + cw_ref[1:2, :] * _shift_rows_up(dconv, nhalo, 1, row) + cw_ref[0:1, :] * _shift_rows_up(dconv, nhalo, 2, row)
        dcw_ref[0:1, :] += jnp.sum(dconv * sh2, axis=0, keepdims=True)
        dcw_ref[1:2, :] += jnp.sum(dconv * sh1, axis=0, keepdims=True)
        dcw_ref[2:3, :] += jnp.sum(dconv * hc, axis=0, keepdims=True)
        for col, val in ((2, dyb * y), (3, dhc * xt), (4, dhc * gate_c)):
            cols = slice(col * da, (col + 1) * da)
            dz_ref[:, cols] = val.astype(BF16)
            dbin_ref[:, cols] += jnp.sum(val, axis=0, keepdims=True)

        @pl.when(i == nblk - 1)
        def _():
            for h in range(nh):
                dbs_ref[h] = jnp.broadcast_to(jnp.sum(dbs_ref[h], axis=1, keepdims=True), (CHUNK, HEAD))

    prev = lambda col: pl.BlockSpec((BF16_ROWS, da), lambda i: (jnp.maximum(i * hb - 1, 0), col))
    nxt = lambda i: jnp.minimum((i + 1) * hb, s // BF16_ROWS - 1)
    in_specs = [pl.BlockSpec((tm, zc), lambda i: (i, 0)), prev(3), prev(4), pl.BlockSpec((BF16_ROWS, da), lambda i: (nxt(i), 2)),
                pl.BlockSpec((2, tm, da), lambda i: (0, i, 0)), pl.BlockSpec((None, BF16_ROWS, da), lambda i: (1, nxt(i), 0)),
                _row((nh, CHUNK, CHUNK)), _row((nh, CHUNK, HEAD)), _row((nh, HEAD)), _row((nh, HEAD)), _row((CONV_TAPS, da))]
    out_specs = [pl.BlockSpec((tm, zc), lambda i: (i, 0)), _row((nh, CHUNK, CHUNK)), _row((nh, CHUNK, HEAD)), _row((nh, HEAD)),
                 _row((nh, HEAD)), _row((8, da)), _row((1, zc))]
    out_shape = [jax.ShapeDtypeStruct((s, zc), BF16), jax.ShapeDtypeStruct((nh, CHUNK, CHUNK), F32),
                 jax.ShapeDtypeStruct((nh, CHUNK, HEAD), F32), jax.ShapeDtypeStruct((nh, HEAD), F32),
                 jax.ShapeDtypeStruct((nh, HEAD), F32), jax.ShapeDtypeStruct((8, da), F32), jax.ShapeDtypeStruct((1, zc), F32)]
    return _call(body, name, (nblk,), in_specs, out_specs, out_shape, [], [z, z, z, z, dy, dy, ws_masked, bs_wide, gv, bv, cw], rider)


def _adam_update(g, w, m, v):
    m_new = ADAM_B1 * m + (1.0 - ADAM_B1) * g
    v_new = ADAM_B2 * v + (1.0 - ADAM_B2) * (g * g)
    m_hat = m_new / (1.0 - ADAM_B1 ** ADAM_STEP)
    v_hat = v_new / (1.0 - ADAM_B2 ** ADAM_STEP)
    return -ADAM_LR * (m_hat / (jnp.sqrt(v_hat) + ADAM_EPS) + ADAM_WD * w), m_new, v_new


def _adamw(gparts, w, m, v, name):
    n, r, c = gparts.shape
    tr = r // 4 if (r // 4) % BF16_ROWS == 0 else r

    def body(g_ref, w_ref, m_ref, v_ref, go_ref, d_ref, mo_ref, vo_ref):
        g = g_ref[0].astype(F32)
        for q in range(1, n):
            g = g + g_ref[q].astype(F32)
        go_ref[...] = g
        d_ref[...], mo_ref[...], vo_ref[...] = _adam_update(g, w_ref[...], m_ref[...], v_ref[...])

    blk = pl.BlockSpec((tr, c), lambda i: (i, 0))
    shp = jax.ShapeDtypeStruct((r, c), F32)
    return _call(body, name, (r // tr,), [pl.BlockSpec((n, tr, c), lambda i: (0, i, 0)), blk, blk, blk], [blk] * 4, [shp] * 4, [],
                 [gparts, w, m, v])[0]


def _adamw_small(packs, rows, w, m, v, conv, name):
    n_par, n_dev = len(rows), packs.shape[0]
    taps = conv[0].shape[0]

    def body(*refs):
        refs = list(refs)
        cut = lambda n: [refs.pop(0) for _ in range(n)]
        p_ref, w_refs, m_refs, v_refs, (cw_ref, cm_ref, cv_ref) = refs.pop(0), cut(n_par), cut(n_par), cut(n_par), cut(3)
        outs = [cut(4) for _ in range(n_par + 1)]
        at = 0
        for k in range(n_par):
            g = p_ref[0, at:at + rows[k], :]
            for dev in range(1, n_dev):
                g = g + p_ref[dev, at:at + rows[k], :]
            go_ref, d_ref, mo_ref, vo_ref = outs[k]
            go_ref[...] = g
            d_ref[...], mo_ref[...], vo_ref[...] = _adam_update(g, w_refs[k][...], m_refs[k][...], v_refs[k][...])
            at += rows[k]
        me = 4 * lax.axis_index("x") + 2 * lax.axis_index("y") + lax.axis_index("c")
        go_ref, d_ref, mo_ref, vo_ref = outs[n_par]
        for tap in range(taps):
            row = pl.ds(at + tap * n_dev + me, 1)
            g = p_ref[0, row, :]
            for dev in range(1, n_dev):
                g = g + p_ref[dev, row, :]
            one = slice(tap, tap + 1)
            go_ref[one, :] = g
            d_ref[one, :], mo_ref[one, :], vo_ref[one, :] = _adam_update(g, cw_ref[one, :], cm_ref[one, :], cv_ref[one, :])

    vmem = pl.BlockSpec(memory_space=pltpu.VMEM)
    ins = [packs] + list(w) + list(m) + list(v) + list(conv)
    out_shape = [jax.ShapeDtypeStruct(a.shape, F32) for a in list(w) + [conv[0]] for _ in range(4)]
    outs = pl.pallas_call(body, name=name, in_specs=[vmem] * len(ins), out_specs=[vmem] * len(out_shape), out_shape=out_shape,
                          compiler_params=pltpu.CompilerParams(vmem_limit_bytes=V7X_VMEM_LIMIT))(*ins)
    return [outs[4 * k:4 * k + 4] for k in range(n_par + 1)]


def _rows128(a):
    return a.reshape(-1, LANES)


def kernel(x, ffa_gate, ffa_up, ffa_down, ln_a_g, ln_a_b, w_in, b_in, w_s, b_s, ln_v_g, ln_v_b, conv_w, w_out, b_out, ln_m_g, ln_m_b, ffc_gate, ffc_up, ffc_down, ln_c_g, ln_c_b, loss_target, m_ffa_gate, m_ffa_up, m_ffa_down, m_ln_a_g, m_ln_a_b, m_w_in, m_b_in, m_w_s, m_b_s, m_ln_v_g, m_ln_v_b, m_conv_w, m_w_out, m_b_out, m_ln_m_g, m_ln_m_b, m_ffc_gate, m_ffc_up, m_ffc_down, m_ln_c_g, m_ln_c_b, v_ffa_gate, v_ffa_up, v_ffa_down, v_ln_a_g, v_ln_a_b, v_w_in, v_b_in, v_w_s, v_b_s, v_ln_v_g, v_ln_v_b, v_conv_w, v_w_out, v_b_out, v_ln_m_g, v_ln_m_b, v_ffc_gate, v_ffc_up, v_ffc_down, v_ln_c_g, v_ln_c_b):
    x2, target = x[0], loss_target[0]
    s, d = x2.shape
    da = d // 2
    nh = da // HEAD

    big = dict(ffa_gate=ffa_gate, ffa_up=ffa_up, ffa_down=ffa_down, w_in=w_in, w_out=w_out, ffc_gate=ffc_gate, ffc_up=ffc_up, ffc_down=ffc_down)
    big_m = dict(ffa_gate=m_ffa_gate, ffa_up=m_ffa_up, ffa_down=m_ffa_down, w_in=m_w_in, w_out=m_w_out, ffc_gate=m_ffc_gate, ffc_up=m_ffc_up, ffc_down=m_ffc_down)
    big_v = dict(ffa_gate=v_ffa_gate, ffa_up=v_ffa_up, ffa_down=v_ffa_down, w_in=v_w_in, w_out=v_w_out, ffc_gate=v_ffc_gate, ffc_up=v_ffc_up, ffc_down=v_ffc_down)
    local = lambda k, a: jnp.transpose(a[0]) if k in TRANSPOSED else a[0]
    shard = {k: local(k, w).astype(BF16) for k, w in big.items()}
    conv_rows = jnp.pad(conv_w[0], ((0, 8 - CONV_TAPS), (0, 0)))

    tril = jnp.tril(jnp.ones((CHUNK, CHUNK), dtype=bool))
    ws_masked = jnp.where(tril[None], w_s[0], 0.0).astype(BF16)
    bs_wide = jnp.broadcast_to(b_s[0][:, :, None], (nh, CHUNK, HEAD))
    gv, bv = ln_v_g.reshape(nh, HEAD), ln_v_b.reshape(nh, HEAD)

    full = {}
    xb = x2.astype(BF16)
    (silu_a, udsilu_a, hid_a), (full["ffa_gate"], full["ffa_up"], full["ffa_down"]) = _ffn_gateup_gathering(
        xb, shard["ffa_gate"], shard["ffa_up"], shard["ffa_down"], "ffa_gateup")
    (xhat1, h1b, rstd1), (full["w_in"], full["w_out"], conv_full) = _down_ln(
        hid_a, full["ffa_down"], None, x2, None, ln_a_g, ln_a_b, 0.5, "ffa_down_ln", rider=_Gather([shard["w_in"], shard["w_out"], conv_rows], by_columns=(0,)))
    cw = jnp.transpose(conv_full[:, :CONV_TAPS, :], (1, 0, 2)).reshape(CONV_TAPS, da)
    w_out2 = full["w_out"].reshape(2, da, d)
    z, (full["ffc_gate"],) = _proj_in(h1b, full["w_in"], b_in, "proj_in", _Gather([shard["ffc_gate"]]))
    ycat = _mixer_fwd(z, ws_masked, bs_wide, gv, bv, cw, "mixer_fwd")
    (xhat2, h2b, rstd2), _ = _down_ln(ycat, w_out2, b_out, xhat1, (ln_a_g, ln_a_b), ln_m_g, ln_m_b, 1.0, "proj_out_ln")
    (silu_c, udsilu_c, hid_c), (full["ffc_up"], full["ffc_down"]) = _ffn_gateup_gathering(
        h2b, full["ffc_gate"], shard["ffc_up"], shard["ffc_down"], "ffc_gateup")
    (dr3, dr3b, sq_err, d_ln_c_g, d_ln_c_b), _ = _down_ln(hid_c, full["ffc_down"], None, xhat2, (ln_m_g, ln_m_b), ln_c_g, ln_c_b, 0.5,
                                                          "ffc_down_ln_loss", target=target)
    loss = lax.psum((0.5 / d) * jnp.sum(sq_err), ("x", "y", "c"))

    landed = {}
    (dg_c, du_c), _ = _ffn_bwd_hidden(dr3b, full["ffc_down"], silu_c, udsilu_c, "ffc_bwd_hidden")
    part, _ = _dw_hidden_rows_paired(hid_c, dr3b, "ffc_dw_down")
    part, (landed["ffc_down"],) = _dw_hidden_rows_paired(dg_c, h2b, "ffc_dw_gate", _ChipScatter(part))
    part, (landed["ffc_gate"],) = _dw_hidden_rows_paired(du_c, h2b, "ffc_dw_up", _ChipScatter(part))
    (dr2, dr2b, d_ln_m_g, d_ln_m_b, d_b_out), (landed["ffc_up"],) = _ffn_bwd_input(
        dg_c, full["ffc_gate"], du_c, full["ffc_up"], dr3, (xhat2, rstd2, ln_m_g), "ffc_bwd_input_ln", _ChipScatter(part))
    dycat = _nt_hidden(dr2b, w_out2, "proj_out_bwd")
    part, _ = _dw_hidden_rows(ycat, dr2b, "proj_out_dw")
    (dz, d_w_s, d_b_s_wide, d_gv, d_bv, d_cw, d_b_in), (landed["w_out"],) = _mixer_bwd(
        z, dycat, ws_masked, bs_wide, gv, bv, cw, "mixer_bwd", _Scatter(part.reshape(N_DEV, d // N_DEV, d)))
    dr1, dr1b, d_ln_a_g, d_ln_a_b = _proj_in_bwd_ln(dz, full["w_in"], dr2, (xhat1, rstd1, ln_a_g), 0.5, "proj_in_bwd_ln")
    small_g = dict(ln_a_g=d_ln_a_g, ln_a_b=d_ln_a_b, b_in=d_b_in, w_s=d_w_s, b_s=d_b_s_wide[:, :, 0], ln_v_g=d_gv, ln_v_b=d_bv, b_out=d_b_out,
                   ln_m_g=d_ln_m_g, ln_m_b=d_ln_m_b, ln_c_g=d_ln_c_g, ln_c_b=d_ln_c_b)
    pack = jnp.concatenate([_rows128(g) for g in small_g.values()] + [_rows128(d_cw[:CONV_TAPS])], axis=0)
    part, (packs,) = _dw_cols(h1b, dz, "proj_in_dw", _Scatter(pack, whole=True))
    (dg_a, du_a), (landed["w_in"],) = _ffn_bwd_hidden(dr1b, full["ffa_down"], silu_a, udsilu_a, "ffa_bwd_hidden", _Scatter(part, by_columns=True))
    part, _ = _dw_hidden_rows_paired(hid_a, dr1b, "ffa_dw_down")
    part, (landed["ffa_down"],) = _dw_hidden_rows_paired(dg_a, xb, "ffa_dw_gate", _ChipScatter(part))
    part, (landed["ffa_gate"],) = _dw_hidden_rows_paired(du_a, xb, "ffa_dw_up", _ChipScatter(part))
    grad_x, (landed["ffa_up"],) = _ffn_bwd_input(dg_a, full["ffa_gate"], du_a, full["ffa_up"], dr1, None, "ffa_bwd_input", _ChipScatter(part))

    grads, deltas, new_m, new_v = {}, {}, {}, {}
    for k in big:
        out = _adamw(landed[k], local(k, big[k]), local(k, big_m[k]), local(k, big_v[k]), "adamw_" + k)
        grads[k], deltas[k], new_m[k], new_v[k] = ((jnp.transpose(o) if k in TRANSPOSED else o).reshape(big[k].shape) for o in out)

    small = dict(ln_a_g=ln_a_g, ln_a_b=ln_a_b, b_in=b_in, w_s=w_s, b_s=b_s, ln_v_g=ln_v_g, ln_v_b=ln_v_b, b_out=b_out,
                 ln_m_g=ln_m_g, ln_m_b=ln_m_b, ln_c_g=ln_c_g, ln_c_b=ln_c_b)
    small_m = dict(ln_a_g=m_ln_a_g, ln_a_b=m_ln_a_b, b_in=m_b_in, w_s=m_w_s, b_s=m_b_s, ln_v_g=m_ln_v_g, ln_v_b=m_ln_v_b, b_out=m_b_out,
                   ln_m_g=m_ln_m_g, ln_m_b=m_ln_m_b, ln_c_g=m_ln_c_g, ln_c_b=m_ln_c_b)
    small_v = dict(ln_a_g=v_ln_a_g, ln_a_b=v_ln_a_b, b_in=v_b_in, w_s=v_w_s, b_s=v_b_s, ln_v_g=v_ln_v_g, ln_v_b=v_ln_v_b, b_out=v_b_out,
                   ln_m_g=v_ln_m_g, ln_m_b=v_ln_m_b, ln_c_g=v_ln_c_g, ln_c_b=v_ln_c_b)
    snames = list(small)
    assert snames == list(small_g) and conv_w.shape[2] == LANES and da == N_DEV * LANES
    views = lambda tree: [_rows128(tree[k]) for k in snames]
    out = _adamw_small(packs, [a.shape[0] for a in views(small)], views(small), views(small_m), views(small_v),
                       (conv_w[0], m_conv_w[0], v_conv_w[0]), "adamw_small")
    for k, per_param in zip(snames + ["conv_w"], out):
        shape = conv_w.shape if k == "conv_w" else small[k].shape
        grads[k], deltas[k], new_m[k], new_v[k] = (o.reshape(shape) for o in per_param)

    order = ["ffa_gate", "ffa_up", "ffa_down", "ln_a_g", "ln_a_b", "w_in", "b_in", "w_s", "b_s", "ln_v_g", "ln_v_b", "conv_w", "w_out", "b_out",
             "ln_m_g", "ln_m_b", "ffc_gate", "ffc_up", "ffc_down", "ln_c_g", "ln_c_b"]
    return (loss, grad_x[None], *[grads[k] for k in order], *[deltas[k] for k in order], *[new_m[k] for k in order], *[new_v[k] for k in order])
```

```python
import math

import jax
import jax.numpy as jnp
from jax import lax
from jax.experimental import pallas as pl
from jax.experimental.pallas import tpu as pltpu

BF16 = jnp.bfloat16
F32 = jnp.float32
MESH = pl.DeviceIdType.MESH

N_DEV = 8
HEAD = 128
CHUNK = 128
CONV_TAPS = 3
LN_EPS = 1e-5
ALPHA = float(2 ** 0.25)
GELU_C = 0.7978845608028654
GELU_A = 0.044715
ADAM_LR, ADAM_B1, ADAM_B2, ADAM_EPS, ADAM_WD, ADAM_STEP = 0.001, 0.9, 0.999, 1e-08, 0.01, 10
V7X_VMEM_LIMIT = 56 * 1024 * 1024
LANES = 128
BF16_ROWS = 16
MXU_COLS = 256
TRANSPOSED = ("ffa_gate", "ffa_up", "ffc_gate", "ffc_up")
PROJ_IN_COLS = 1280
MIXER_TOKENS = 512
DW_TOKENS = 2048

NT_DIMS = (((1,), (1,)), ((), ()))
TN_DIMS = (((0,), (0,)), ((), ()))
ANY = pl.BlockSpec(memory_space=pl.ANY)


def _gelu_tanh(x):
    return jnp.tanh(GELU_C * (x + GELU_A * x * x * x))


def _gelu(x, t):
    return 0.5 * x * (1.0 + t)


def _gelu_grad(x, t):
    return 0.5 * (1.0 + t) + 0.5 * x * (1.0 - t * t) * GELU_C * (1.0 + 3.0 * GELU_A * x * x)


def _sigmoid(x):
    return 0.5 * jnp.tanh(0.5 * x) + 0.5


def _row(shape):
    return pl.BlockSpec(shape, lambda *_: (0,) * len(shape))


def _row_blocks(tm, rows=128):
    rows = min(rows, tm)
    return [slice(r, r + rows) for r in range(0, tm, rows)]


def _ln_backward(dh, xhat, rstd, gain):
    dxh = dh * gain
    m1 = jnp.mean(dxh, axis=-1, keepdims=True)
    m2 = jnp.mean(dxh * xhat, axis=-1, keepdims=True)
    return rstd * (dxh - m1 - xhat * m2)


def _place():
    x, y, c = lax.axis_index("x"), lax.axis_index("y"), lax.axis_index("c")
    return x, y, c, [(1 - x, y), (x, 1 - y), (1 - x, 1 - y)]


def _other_devices(x, y, c):
    flips = [(bx, by, bc) for bx in (0, 1) for by in (0, 1) for bc in (0, 1)][1:]
    return [(1 - x if bx else x, 1 - y if by else y, 1 - c if bc else c) for bx, by, bc in flips]


class _Gather:
    def __init__(self, shards, forward_at=0.75, by_columns=()):
        n = len(shards)
        self.n, self.forward_at, self.by_columns = n, forward_at, tuple(by_columns)
        self.inputs = list(shards)
        self.out_shapes = [jax.ShapeDtypeStruct((a.shape[0], N_DEV * a.shape[1]) if i in self.by_columns else (N_DEV,) + a.shape, a.dtype)
                           for i, a in enumerate(shards)]
        self.scratch = [pltpu.SemaphoreType.DMA((n, 7)), pltpu.SemaphoreType.DMA((n, 7)), pltpu.SemaphoreType.DMA((n,))]

    def _block(self, outs, a, dev):
        if a in self.by_columns:
            cols = outs[a].shape[1] // N_DEV
            return outs[a].at[:, pl.ds(dev * cols, cols)]
        return outs[a].at[dev]

    def _copy(self, outs, sems, a, k, block, to, src=None):
        dst = self._block(outs, a, block)
        return pltpu.make_async_remote_copy(src_ref=dst if src is None else src, dst_ref=dst, send_sem=sems[0].at[a, k],
                                            recv_sem=sems[1].at[a, k], device_id=to, device_id_type=MESH)

    def start(self, ins, outs, sems, urgent=None):
        x, y, c, chips = _place()
        me = 4 * x + 2 * y + c
        for a in range(self.n):
            pltpu.make_async_copy(ins[a], self._block(outs, a, me), sems[2].at[a]).start()
        urgent = list(range(self.n)) if urgent is None else list(urgent)
        for group in (urgent, [a for a in range(self.n) if a not in urgent]):
            for a in group:
                self._copy(outs, sems, a, 0, me, (x, y, 1 - c), src=ins[a]).start()
                for j in (0, 1):
                    self._copy(outs, sems, a, 1 + j, me, (*chips[j], c), src=ins[a]).start()
            for a in group:
                self._copy(outs, sems, a, 3, me, (*chips[2], c), src=ins[a]).start()

    def wait_sibling(self, outs, sems, a):
        x, y, c, _ = _place()
        self._copy(outs, sems, a, 0, 4 * x + 2 * y + 1 - c, (x, y, 1 - c)).wait_recv()

    def pass_on(self, outs, sems, a, j):
        x, y, c, chips = _place()
        block = 4 * chips[j][0] + 2 * chips[j][1] + c
        self._copy(outs, sems, a, 1 + j, block, (x, y, 1 - c)).wait_recv()
        self._copy(outs, sems, a, 4 + j, block, (x, y, 1 - c)).start()

    def wait_passed(self, outs, sems, a, j):
        x, y, c, chips = _place()
        self._copy(outs, sems, a, 4 + j, 4 * chips[j][0] + 2 * chips[j][1] + 1 - c, (x, y, 1 - c)).wait_recv()

    def wait_sent(self, ins, outs, sems, a):
        x, y, c, _ = _place()
        me = 4 * x + 2 * y + c
        for k in range(7):
            self._copy(outs, sems, a, k, me, (x, y, 1 - c), src=ins[a]).wait_send()
        pltpu.make_async_copy(ins[a], self._block(outs, a, me), sems[2].at[a]).wait()

    def forward(self, ins, outs, sems):
        for a in range(self.n):
            for j in range(3):
                self.pass_on(outs, sems, a, j)

    def finish(self, ins, outs, sems):
        for a in range(self.n):
            self.wait_sibling(outs, sems, a)
            for j in range(3):
                self.wait_passed(outs, sems, a, j)
        for a in range(self.n):
            self.wait_sent(ins, outs, sems, a)

    def before(self, step, n_steps, ins, outs, sems):
        pl.when(step == 0)(lambda: self.start(ins, outs, sems))
        pl.when(step == int(self.forward_at * (n_steps - 1)))(lambda: self.forward(ins, outs, sems))

    def after(self, step, n_steps, ins, outs, sems):
        pl.when(step == n_steps - 1)(lambda: self.finish(ins, outs, sems))


class _Scatter:
    def __init__(self, partial, whole=False, by_columns=False):
        self.whole, self.by_columns = whole, by_columns
        self.inputs = [partial]
        if whole:
            shape = (N_DEV,) + partial.shape
        elif by_columns:
            shape = (N_DEV, partial.shape[0], partial.shape[1] // N_DEV)
        else:
            shape = partial.shape
        self.out_shapes = [jax.ShapeDtypeStruct(shape, partial.dtype)]
        self.scratch = [pltpu.SemaphoreType.DMA((7,)), pltpu.SemaphoreType.DMA((7,)), pltpu.SemaphoreType.DMA((1,))]

    def _copies(self, ins, outs, sems):
        x, y, c, _ = _place()
        me = 4 * x + 2 * y + c
        if self.whole:
            block = lambda dev: ins[0]
        elif self.by_columns:
            cols = ins[0].shape[1] // N_DEV
            block = lambda dev: ins[0].at[:, pl.ds(dev * cols, cols)]
        else:
            block = lambda dev: ins[0].at[dev]
        mine = pltpu.make_async_copy(block(me), outs[0].at[me], sems[2].at[0])
        remote = [pltpu.make_async_remote_copy(src_ref=block(4 * px + 2 * py + pc), dst_ref=outs[0].at[me], send_sem=sems[0].at[k],
                                               recv_sem=sems[1].at[k], device_id=(px, py, pc), device_id_type=MESH)
                  for k, (px, py, pc) in enumerate(_other_devices(x, y, c))]
        return mine, remote

    def start(self, ins, outs, sems):
        mine, remote = self._copies(ins, outs, sems)
        mine.start()
        for cp in remote:
            cp.start()

    def finish(self, ins, outs, sems):
        mine, remote = self._copies(ins, outs, sems)
        for cp in remote:
            cp.wait()
        mine.wait()

    def before(self, step, n_steps, ins, outs, sems):
        pl.when(step == 0)(lambda: self.start(ins, outs, sems))

    def after(self, step, n_steps, ins, outs, sems):
        pl.when(step == n_steps - 1)(lambda: self.finish(ins, outs, sems))


class _ChipScatter(_Scatter):
    def __init__(self, sums):
        super().__init__(sums)
        self.scratch = [pltpu.SemaphoreType.DMA((3,)), pltpu.SemaphoreType.DMA((3,)), pltpu.SemaphoreType.DMA((1,))]

    def _copies(self, ins, outs, sems):
        x, y, c, chips = _place()
        my_chip = 2 * x + y
        mine = pltpu.make_async_copy(ins[0].at[my_chip], outs[0].at[my_chip], sems[2].at[0])
        remote = [pltpu.make_async_remote_copy(src_ref=ins[0].at[2 * px + py], dst_ref=outs[0].at[my_chip], send_sem=sems[0].at[k],
                                               recv_sem=sems[1].at[k], device_id=(px, py, c), device_id_type=MESH)
                  for k, (px, py) in enumerate(chips)]
        return mine, remote


def _call(body, name, grid, in_specs, out_specs, out_shape, scratch, ins, rider=None):
    single = not isinstance(out_shape, (list, tuple))
    out_shape = [out_shape] if single else list(out_shape)
    out_specs = [out_specs] if single else list(out_specs)
    params = pltpu.CompilerParams(dimension_semantics=("arbitrary",) * len(grid), vmem_limit_bytes=V7X_VMEM_LIMIT)
    if rider is None:
        outs = pl.pallas_call(body, name=name, grid=grid, in_specs=in_specs, out_specs=out_specs, out_shape=out_shape,
                              scratch_shapes=scratch, compiler_params=params)(*ins)
        return (outs[0] if single else outs), None
    n_in, n_out, n_scr = len(ins), len(out_shape), len(scratch)
    r_in, r_out = len(rider.inputs), len(rider.out_shapes)
    n_steps = math.prod(grid)

    def carried(*refs):
        refs = list(refs)
        cut = lambda n: [refs.pop(0) for _ in range(n)]
        b_in, c_in, b_out, c_out, b_scr = cut(n_in), cut(r_in), cut(n_out), cut(r_out), cut(n_scr)
        step = 0
        for axis, size in enumerate(grid):
            step = step * size + pl.program_id(axis)
        rider.before(step, n_steps, c_in, c_out, refs)
        body(*b_in, *b_out, *b_scr)
        rider.after(step, n_steps, c_in, c_out, refs)

    outs = pl.pallas_call(
        carried, name=name, grid=grid, in_specs=list(in_specs) + [ANY] * r_in, out_specs=out_specs + [ANY] * r_out,
        out_shape=out_shape + rider.out_shapes, scratch_shapes=list(scratch) + rider.scratch, compiler_params=params,
    )(*ins, *rider.inputs)
    base = outs[:n_out]
    return (base[0] if single else base), outs[n_out:]


def _arrival_block(j):
    x, y, c = lax.axis_index("x"), lax.axis_index("y"), lax.axis_index("c")
    chip, other_core = j // 2, j % 2
    px = jnp.where((chip == 1) | (chip == 3), 1 - x, x)
    py = jnp.where((chip == 2) | (chip == 3), 1 - y, y)
    pc = jnp.where(other_core == 1, 1 - c, c)
    return 4 * px + 2 * py + pc


def _ffn_gateup_gathering(xb, gate, up_shard, down_shard, name):
    s, d = xb.shape
    fs = up_shard.shape[0]
    tm = min(s, 1024)
    ni = s // tm
    ask_at = max(ni - 2, 0)
    gate_here = gate.ndim == 2
    gather = _Gather(([gate] if gate_here else []) + [up_shard, down_shard])
    n_g = gather.n
    used_here, down = tuple(range(n_g - 1)), n_g - 1

    def body(x_ref, *refs):
        refs = list(refs)
        gate_full = None if gate_here else refs.pop(0)
        shards = [refs.pop(0) for _ in range(n_g)]
        silu_ref, udsilu_ref, h_ref = refs.pop(0), refs.pop(0), refs.pop(0)
        fulls = [refs.pop(0) for _ in range(n_g)]
        w_ref, w_sems = refs.pop(0), refs.pop(0)
        sems = refs
        j, i = pl.program_id(0), pl.program_id(1)
        gate_src, up_src = (fulls[0], fulls[1]) if gate_here else (gate_full, fulls[0])

        def load(slot, srcs):
            return [pltpu.make_async_copy(src, w_ref.at[slot, a], w_sems.at[slot, a]) for a, src in enumerate(srcs)]

        @pl.when((j == 0) & (i == 0))
        def _():
            gather.start(shards, fulls, sems, urgent=used_here)
            mine = load(0, (shards[0] if gate_here else gate_full.at[_arrival_block(0)], shards[n_g - 2]))
            for cp in mine:
                cp.start()
            for cp in mine:
                cp.wait()

        for nxt in range(1, N_DEV):
            @pl.when((j == nxt) & (i == 0))
            def _(nxt=nxt):
                for cp in load(nxt % 2, (gate_src.at[0], up_src.at[0])):
                    cp.wait()

        for nxt in range(1, N_DEV):
            @pl.when((j == nxt - 1) & (i == ask_at))
            def _(nxt=nxt):
                for a in used_here:
                    if nxt == 1:
                        gather.wait_sibling(fulls, sems, a)
                    elif nxt % 2 == 0:
                        gather.pass_on(fulls, sems, a, nxt // 2 - 1)
                    else:
                        gather.wait_passed(fulls, sems, a, nxt // 2 - 1)
                block = _arrival_block(nxt)
                for cp in load(nxt % 2, (gate_src.at[block], up_src.at[block])):
                    cp.start()

        @pl.when((j == N_DEV - 1) & (i == ask_at))
        def _():
            for other_chip in range(3):
                gather.pass_on(fulls, sems, down, other_chip)

        x = x_ref[...]
        g = lax.dot_general(x, w_ref[j % 2, 0], NT_DIMS, preferred_element_type=F32)
        u = lax.dot_general(x, w_ref[j % 2, 1], NT_DIMS, preferred_element_type=F32)
        sg = _sigmoid(g)
        silu = g * sg
        silu_ref[...] = silu.astype(BF16)
        udsilu_ref[...] = (u * (sg + silu * (1.0 - sg))).astype(BF16)
        h_ref[...] = (silu * u).astype(BF16)

        @pl.when((j == N_DEV - 1) & (i == ni - 1))
        def _():
            gather.wait_sibling(fulls, sems, down)
            for other_chip in range(3):
                gather.wait_passed(fulls, sems, down, other_chip)
            for a in range(n_g):
                gather.wait_sent(shards, fulls, sems, a)

    shp = jax.ShapeDtypeStruct((N_DEV, s, fs), BF16)
    o_spec = pl.BlockSpec((None, tm, fs), lambda j, i: (_arrival_block(j), i, 0))
    ins = ([] if gate_here else [gate]) + gather.inputs
    outs = pl.pallas_call(
        body, name=name, grid=(N_DEV, ni), in_specs=[pl.BlockSpec((tm, d), lambda j, i: (i, 0))] + [ANY] * len(ins),
        out_specs=[o_spec, o_spec, o_spec] + [ANY] * n_g, out_shape=[shp, shp, shp] + gather.out_shapes,
        scratch_shapes=[pltpu.VMEM((2, 2, fs, d), BF16), pltpu.SemaphoreType.DMA((2, 2))] + gather.scratch,
        compiler_params=pltpu.CompilerParams(dimension_semantics=("arbitrary", "arbitrary"), vmem_limit_bytes=V7X_VMEM_LIMIT),
    )(xb, *ins)
    return outs[:3], outs[3:]


def _down_ln(a3, w3, bias, res, res_affine, ln_g, ln_b, scale, name, target=None, rider=None):
    nk, s, tk = a3.shape
    d = w3.shape[2]
    tm = min(s, 256)
    final = target is not None

    def body(*refs):
        refs = list(refs)
        a_ref, w_hbm = refs[:2]
        del refs[:2]
        bias_ref = refs.pop(0) if bias is not None else None
        res_ref = refs.pop(0)
        rg_ref, rb_ref = (refs.pop(0), refs.pop(0)) if res_affine is not None else (None, None)
        g_ref, b_ref = refs.pop(0), refs.pop(0)
        t_ref = refs.pop(0) if final else None
        w_sem = refs.pop()
        w_ref = refs.pop()
        i = pl.program_id(0)
        if final:
            dr_ref, drb_ref, sq_ref, dg_ref, db_ref = refs
        else:
            xh_ref, hb_ref, rstd_ref = refs

        @pl.when(i == 0)
        def _():
            whole = pltpu.make_async_copy(w_hbm, w_ref, w_sem.at[0])
            whole.start()
            whole.wait()
            if final:
                sq_ref[...] = jnp.zeros_like(sq_ref)
                dg_ref[...] = jnp.zeros_like(dg_ref)
                db_ref[...] = jnp.zeros_like(db_ref)

        y = jnp.dot(a_ref[0], w_ref[0], preferred_element_type=F32)
        for k in range(1, nk):
            y = y + jnp.dot(a_ref[k], w_ref[k], preferred_element_type=F32)
        if bias_ref is not None:
            y = y + bias_ref[...]
        for rows in _row_blocks(tm):
            r = res_ref[rows, :]
            if rg_ref is not None:
                r = r * rg_ref[...] + rb_ref[...]
            r = ALPHA * r + scale * y[rows]
            mu = jnp.mean(r, axis=-1, keepdims=True)
            c = r - mu
            var = jnp.mean(c * c, axis=-1, keepdims=True)
            rstd = lax.rsqrt(var + LN_EPS)
            xhat = c * rstd
            h = xhat * g_ref[...] + b_ref[...]
            if not final:
                xh_ref[rows, :] = xhat
                hb_ref[rows, :] = h.astype(BF16)
                rstd_ref[rows, :] = rstd
            else:
                err = h - t_ref[rows, :]
                sq_ref[...] += jnp.sum(err * err, axis=0, keepdims=True)
                dh = err * (1.0 / d)
                dg_ref[...] += jnp.sum(dh * xhat, axis=0, keepdims=True)
                db_ref[...] += jnp.sum(dh, axis=0, keepdims=True)
                dr = _ln_backward(dh, xhat, rstd, g_ref[...])
                dr_ref[rows, :] = dr
                drb_ref[rows, :] = (scale * dr).astype(BF16)

    tok = pl.BlockSpec((tm, d), lambda i: (i, 0))
    vec = pl.BlockSpec((1, d), lambda i: (0, 0))
    ins = [a3, w3]
    in_specs = [pl.BlockSpec((nk, tm, tk), lambda i: (0, i, 0)), ANY]
    if bias is not None:
        ins.append(bias)
        in_specs.append(vec)
    ins.append(res)
    in_specs.append(tok)
    if res_affine is not None:
        ins += list(res_affine)
        in_specs += [vec, vec]
    ins += [ln_g, ln_b]
    in_specs += [vec, vec]
    if final:
        ins.append(target)
        in_specs.append(tok)
        out_shape = [jax.ShapeDtypeStruct((s, d), F32), jax.ShapeDtypeStruct((s, d), BF16)] + [jax.ShapeDtypeStruct((1, d), F32)] * 3
        out_specs = [tok, tok, vec, vec, vec]
    else:
        out_shape = [jax.ShapeDtypeStruct((s, d), F32), jax.ShapeDtypeStruct((s, d), BF16), jax.ShapeDtypeStruct((s, 1), F32)]
        out_specs = [tok, tok, pl.BlockSpec((tm, 1), lambda i: (i, 0))]
    scratch = [pltpu.VMEM((nk, tk, d), BF16), pltpu.SemaphoreType.DMA((1,))]
    return _call(body, name, (s // tm,), in_specs, out_specs, out_shape, scratch, ins, rider)


def _proj_in(hb, w, bias, name, rider=None):
    s, d = hb.shape
    n = w.shape[1]
    tm = min(s, 1024)
    tn = PROJ_IN_COLS if n % PROJ_IN_COLS == 0 else n

    def body(h_ref, w_ref, b_ref, z_ref):
        z_ref[...] = (jnp.dot(h_ref[...], w_ref[...], preferred_element_type=F32) + b_ref[...]).astype(BF16)

    in_specs = [pl.BlockSpec((tm, d), lambda i, j: (i, 0)), pl.BlockSpec((d, tn), lambda i, j: (0, j)),
                pl.BlockSpec((1, tn), lambda i, j: (0, j))]
    return _call(body, name, (s // tm, n // tn), in_specs, pl.BlockSpec((tm, tn), lambda i, j: (i, j)),
                 jax.ShapeDtypeStruct((s, n), BF16), [], [hb, w, bias], rider)


def _nt_hidden(ab, w3, name):
    s, kdim = ab.shape
    nj, tn, _ = w3.shape
    tm = min(s, 1024)

    def body(a_ref, w_ref, o_ref):
        o_ref[...] = lax.dot_general(a_ref[...], w_ref[...], NT_DIMS, preferred_element_type=F32).astype(BF16)

    in_specs = [pl.BlockSpec((tm, kdim), lambda i, j: (i, 0)), pl.BlockSpec((None, tn, kdim), lambda i, j: (j, 0, 0))]
    return _call(body, name, (s // tm, nj), in_specs, pl.BlockSpec((None, tm, tn), lambda i, j: (j, i, 0)),
                 jax.ShapeDtypeStruct((nj, s, tn), BF16), [], [ab, w3])[0]


def _ffn_bwd_hidden(ab, w3, silu3, udsilu3, name, rider=None):
    s, kdim = ab.shape
    nj, tn, _ = w3.shape
    tm = min(s, 1024)

    def body(a_ref, w_ref, silu_ref, udsilu_ref, dg_ref, du_ref):
        a = a_ref[...]
        for c0 in range(0, tn, MXU_COLS):
            cols = slice(c0, min(c0 + MXU_COLS, tn))
            t = lax.dot_general(a, w_ref[cols, :], NT_DIMS, preferred_element_type=F32)
            du_ref[:, cols] = (t * silu_ref[:, cols].astype(F32)).astype(BF16)
            dg_ref[:, cols] = (t * udsilu_ref[:, cols].astype(F32)).astype(BF16)

    hid = pl.BlockSpec((None, tm, tn), lambda i, j: (j, i, 0))
    shp = jax.ShapeDtypeStruct((nj, s, tn), BF16)
    in_specs = [pl.BlockSpec((tm, kdim), lambda i, j: (i, 0)), pl.BlockSpec((None, tn, kdim), lambda i, j: (j, 0, 0)), hid, hid]
    return _call(body, name, (s // tm, nj), in_specs, [hid, hid], [shp, shp], [], [ab, w3, silu3, udsilu3], rider)


def _tn_dw(a, a_spec, b, b_spec, nj, m, n, s, tk, name, rider):
    def body(a_ref, b_ref, o_ref, acc_ref):
        k = pl.program_id(1)

        @pl.when(k == 0)
        def _():
            acc_ref[...] = jnp.zeros_like(acc_ref)

        acc_ref[...] += lax.dot_general(a_ref[...], b_ref[...], TN_DIMS, preferred_element_type=F32)

        @pl.when(k == s // tk - 1)
        def _():
            o_ref[...] = acc_ref[...].astype(BF16)

    return _call(body, name, (nj, s // tk), [a_spec, b_spec], pl.BlockSpec((None, m, n), lambda j, k: (j, 0, 0)),
                 jax.ShapeDtypeStruct((nj, m, n), BF16), [pltpu.VMEM((m, n), F32)], [a, b], rider)


def _dw_hidden_rows(hid3, db, name, rider=None):
    nj, s, fs = hid3.shape
    d = db.shape[1]
    tk = min(s, DW_TOKENS)
    return _tn_dw(hid3, pl.BlockSpec((None, tk, fs), lambda j, k: (j, k, 0)), db, pl.BlockSpec((tk, d), lambda j, k: (k, 0)),
                  nj, fs, d, s, tk, name, rider)


def _dw_hidden_rows_paired(hid3, db, name, rider=None):
    nj, s, fs = hid3.shape
    d = db.shape[1]
    tk = min(s, DW_TOKENS)
    nk = s // tk
    half = nj // 2

    def device_of(j):
        c = lax.axis_index("c")
        return 2 * (j % half) + jnp.where(j < half, 1 - c, c)

    def body(a_ref, b_ref, o_ref, theirs_ref, acc_ref, stage_ref, got_ref, send_sems, recv_sems, load_sem):
        j, k = pl.program_id(0), pl.program_id(1)
        x, y, c = lax.axis_index("x"), lax.axis_index("y"), lax.axis_index("c")

        def to_sibling(q):
            return pltpu.make_async_remote_copy(src_ref=stage_ref, dst_ref=theirs_ref.at[q], send_sem=send_sems.at[q],
                                                recv_sem=recv_sems.at[q], device_id=(x, y, 1 - c), device_id_type=MESH)

        def fetch(q):
            return pltpu.make_async_copy(theirs_ref.at[q], got_ref, load_sem.at[0])

        @pl.when(k == 0)
        def _():
            acc_ref[...] = jnp.zeros_like(acc_ref)

        acc_ref[...] += lax.dot_general(a_ref[...], b_ref[...], TN_DIMS, preferred_element_type=F32)

        for q in range(half):
            @pl.when((j == q) & (k == nk - 1))
            def _(q=q):
                if q > 0:
                    to_sibling(q - 1).wait_send()
                stage_ref[...] = acc_ref[...].astype(BF16)
                to_sibling(q).start()

            @pl.when((j == half + q) & (k == 0))
            def _(q=q):
                to_sibling(q).wait_recv()
                fetch(q).start()

            @pl.when((j == half + q) & (k == nk - 1))
            def _(q=q):
                if q == 0:
                    to_sibling(half - 1).wait_send()
                fetch(q).wait()
                o_ref[...] = (acc_ref[...] + got_ref[...].astype(F32)).astype(BF16)

    in_specs = [pl.BlockSpec((None, tk, fs), lambda j, k: (device_of(j), k, 0)), pl.BlockSpec((tk, d), lambda j, k: (k, 0))]
    out_specs = [pl.BlockSpec((None, fs, d), lambda j, k: (jnp.maximum(j - half, 0), 0, 0)), ANY]
    shp = jax.ShapeDtypeStruct((half, fs, d), BF16)
    scratch = [pltpu.VMEM((fs, d), F32), pltpu.VMEM((fs, d), BF16), pltpu.VMEM((fs, d), BF16),
               pltpu.SemaphoreType.DMA((half,)), pltpu.SemaphoreType.DMA((half,)), pltpu.SemaphoreType.DMA((1,))]
    (sums, _), riders_out = _call(body, name, (nj, nk), in_specs, out_specs, [shp, shp], scratch, [hid3, db], rider)
    return sums, riders_out


def _dw_cols(ab, dz, name, rider=None):
    s, d = ab.shape
    n = dz.shape[1]
    tn = PROJ_IN_COLS if n % PROJ_IN_COLS == 0 else n
    tk = min(s, DW_TOKENS)
    tr = d // 2

    def body(a_ref, b_ref, o_ref, acc_ref):
        k = pl.program_id(2)

        @pl.when(k == 0)
        def _():
            acc_ref[...] = jnp.zeros_like(acc_ref)

        acc_ref[...] += lax.dot_general(a_ref[...], b_ref[...], TN_DIMS, preferred_element_type=F32)

        @pl.when(k == s // tk - 1)
        def _():
            o_ref[...] = acc_ref[...].astype(BF16)

    in_specs = [pl.BlockSpec((tk, tr), lambda j, r, k: (k, r)), pl.BlockSpec((tk, tn), lambda j, r, k: (k, j))]
    return _call(body, name, (n // tn, d // tr, s // tk), in_specs, pl.BlockSpec((tr, tn), lambda j, r, k: (r, j)),
                 jax.ShapeDtypeStruct((d, n), BF16), [pltpu.VMEM((tr, tn), F32)], [ab, dz], rider)


def _ffn_bwd_input(dg3, wg3, du3, wu3, dres, ln, name, rider=None):
    s, d = dres.shape
    nk, _, fs = dg3.shape
    tm = min(s, 512)

    def body(*refs):
        refs = list(refs)
        dg_in, wg_ref, du_in, wu_ref, dres_ref = refs[:5]
        del refs[:5]
        if ln is not None:
            xh_ref, rstd_ref, gain_ref = refs.pop(0), refs.pop(0), refs.pop(0)
        acc_ref = refs.pop()
        i, k = pl.program_id(0), pl.program_id(1)

        @pl.when(k == 0)
        def _():
            acc_ref[...] = jnp.zeros_like(acc_ref)

        acc_ref[...] += (jnp.dot(dg_in[...], wg_ref[...], preferred_element_type=F32)
                         + jnp.dot(du_in[...], wu_ref[...], preferred_element_type=F32))

        @pl.when(k == nk - 1)
        def _():
            if ln is not None:
                dr_ref, drb_ref, dg_ref, db_ref, sum_ref = refs

                @pl.when(i == 0)
                def _():
                    dg_ref[...] = jnp.zeros_like(dg_ref)
                    db_ref[...] = jnp.zeros_like(db_ref)
                    sum_ref[...] = jnp.zeros_like(sum_ref)

            for rows in _row_blocks(tm):
                dh = ALPHA * dres_ref[rows, :] + acc_ref[rows, :]
                if ln is None:
                    refs[0][rows, :] = dh
                else:
                    xhat = xh_ref[rows, :]
                    dg_ref[...] += jnp.sum(dh * xhat, axis=0, keepdims=True)
                    db_ref[...] += jnp.sum(dh, axis=0, keepdims=True)
                    dr = _ln_backward(dh, xhat, rstd_ref[rows, :], gain_ref[...])
                    sum_ref[...] += jnp.sum(dr, axis=0, keepdims=True)
                    dr_ref[rows, :] = dr
                    drb_ref[rows, :] = dr.astype(BF16)

    tok = pl.BlockSpec((tm, d), lambda i, k: (i, 0))
    vec = pl.BlockSpec((1, d), lambda i, k: (0, 0))
    a_spec = pl.BlockSpec((None, tm, fs), lambda i, k: (k, i, 0))
    w_spec = pl.BlockSpec((None, fs, d), lambda i, k: (k, 0, 0))
    ins, in_specs = [dg3, wg3, du3, wu3, dres], [a_spec, w_spec, a_spec, w_spec, tok]
    if ln is None:
        out_shape, out_specs = jax.ShapeDtypeStruct((s, d), F32), tok
    else:
        ins += list(ln)
        in_specs += [tok, pl.BlockSpec((tm, 1), lambda i, k: (i, 0)), vec]
        out_shape = [jax.ShapeDtypeStruct((s, d), F32), jax.ShapeDtypeStruct((s, d), BF16)] + [jax.ShapeDtypeStruct((1, d), F32)] * 3
        out_specs = [tok, tok, vec, vec, vec]
    return _call(body, name, (s // tm, nk), in_specs, out_specs, out_shape, [pltpu.VMEM((tm, d), F32)], ins, rider)


def _proj_in_bwd_ln(dz, w, dres, ln, branch_scale, name):
    s, d = dres.shape
    n = w.shape[1]
    tm = min(s, 256)

    def body(dz_ref, w_hbm, dres_ref, xh_ref, rstd_ref, gain_ref, dr_ref, drb_ref, dg_ref, db_ref, w_ref, w_sem):
        @pl.when(pl.program_id(0) == 0)
        def _():
            whole = pltpu.make_async_copy(w_hbm, w_ref, w_sem.at[0])
            whole.start()
            whole.wait()
            dg_ref[...] = jnp.zeros_like(dg_ref)
            db_ref[...] = jnp.zeros_like(db_ref)

        acc = lax.dot_general(dz_ref[...], w_ref[...], NT_DIMS, preferred_element_type=F32)
        for rows in _row_blocks(tm):
            dh = ALPHA * dres_ref[rows, :] + acc[rows]
            xhat = xh_ref[rows, :]
            dg_ref[...] += jnp.sum(dh * xhat, axis=0, keepdims=True)
            db_ref[...] += jnp.sum(dh, axis=0, keepdims=True)
            dr = _ln_backward(dh, xhat, rstd_ref[rows, :], gain_ref[...])
            dr_ref[rows, :] = dr
            drb_ref[rows, :] = (branch_scale * dr).astype(BF16)

    tok = pl.BlockSpec((tm, d), lambda i: (i, 0))
    vec = pl.BlockSpec((1, d), lambda i: (0, 0))
    in_specs = [pl.BlockSpec((tm, n), lambda i: (i, 0)), ANY, tok, tok, pl.BlockSpec((tm, 1), lambda i: (i, 0)), vec]
    out_shape = [jax.ShapeDtypeStruct((s, d), F32), jax.ShapeDtypeStruct((s, d), BF16)] + [jax.ShapeDtypeStruct((1, d), F32)] * 2
    scratch = [pltpu.VMEM(w.shape, BF16), pltpu.SemaphoreType.DMA((1,))]
    return _call(body, name, (s // tm,), in_specs, [tok, tok, vec, vec], out_shape, scratch, [dz, w, dres] + list(ln))[0]


def _shift_rows_down(v, halo, k, row):
    out = pltpu.roll(v, k, 0)
    hr = halo.shape[0]
    for r in range(k):
        out = jnp.where(row == r, halo[hr - k + r:hr - k + r + 1, :], out)
    return out


def _shift_rows_up(v, halo, k, row):
    t = v.shape[0]
    out = pltpu.roll(v, t - k, 0)
    for r in range(k):
        out = jnp.where(row == t - k + r, halo[r:r + 1, :], out)
    return out


def _sgu_head_forward(z_ref, h, da, gv_ref, bv_ref):
    zu = z_ref[:, h * HEAD:(h + 1) * HEAD].astype(F32)
    zv = z_ref[:, da + h * HEAD:da + (h + 1) * HEAD].astype(F32)
    tu, tv = _gelu_tanh(zu), _gelu_tanh(zv)
    u = _gelu(zu, tu)
    v = _gelu(zv, tv)
    mu = jnp.mean(v, axis=-1, keepdims=True)
    c = v - mu
    rstd = lax.rsqrt(jnp.mean(c * c, axis=-1, keepdims=True) + LN_EPS)
    vhat = c * rstd
    vln = (vhat * gv_ref[h:h + 1, :] + bv_ref[h:h + 1, :]).astype(BF16)
    return (zu, tu), (zv, tv), u, vhat, rstd, vln


def _mixer_fwd(z, ws_masked, bs_wide, gv, bv, cw, name):
    s, zc = z.shape
    da = zc // 5
    nh = da // HEAD
    tm = min(s, MIXER_TOKENS)
    hb = tm // BF16_ROWS

    def body(z_ref, pc_ref, px_ref, ws_ref, bs_ref, gv_ref, bv_ref, cw_ref, y_ref):
        i = pl.program_id(0)
        for h in range(nh):
            _, _, u, _, _, vln = _sgu_head_forward(z_ref, h, da, gv_ref, bv_ref)
            for n in range(tm // CHUNK):
                rows = slice(n * CHUNK, (n + 1) * CHUNK)
                mixed = jnp.dot(ws_ref[h], vln[rows], preferred_element_type=F32) + bs_ref[h]
                y_ref[0, rows, h * HEAD:(h + 1) * HEAD] = (u[rows] * mixed).astype(BF16)
        gate_b = z_ref[:, 2 * da:3 * da].astype(F32)
        hc = z_ref[:, 3 * da:4 * da].astype(F32) * z_ref[:, 4 * da:5 * da].astype(F32)
        halo = jnp.where(i > 0, pc_ref[...].astype(F32) * px_ref[...].astype(F32), 0.0)
        row = lax.broadcasted_iota(jnp.int32, (tm, da), 0)
        y = cw_ref[0:1, :] * _shift_rows_down(hc, halo, 2, row) + cw_ref[1:2, :] * _shift_rows_down(hc, halo, 1, row) + cw_ref[2:3, :] * hc
        y_ref[1] = (gate_b * y).astype(BF16)

    prev = lambda col: pl.BlockSpec((BF16_ROWS, da), lambda i: (jnp.maximum(i * hb - 1, 0), col))
    in_specs = [pl.BlockSpec((tm, zc), lambda i: (i, 0)), prev(3), prev(4), _row((nh, CHUNK, CHUNK)), _row((nh, CHUNK, HEAD)),
                _row((nh, HEAD)), _row((nh, HEAD)), _row((CONV_TAPS, da))]
    return _call(body, name, (s // tm,), in_specs, pl.BlockSpec((2, tm, da), lambda i: (0, i, 0)),
                 jax.ShapeDtypeStruct((2, s, da), BF16), [], [z, z, z, ws_masked, bs_wide, gv, bv, cw])[0]


def _mixer_bwd(z, dy, ws_masked, bs_wide, gv, bv, cw, name, rider=None):
    s, zc = z.shape
    da = zc // 5
    nh = da // HEAD
    tm = min(s, MIXER_TOKENS)
    hb = tm // BF16_ROWS
    nblk = s // tm

    def body(z_ref, pc_ref, px_ref, nb_ref, dy_ref, ndy_ref, ws_ref, bs_ref, gv_ref, bv_ref, cw_ref,
             dz_ref, dws_ref, dbs_ref, dgv_ref, dbv_ref, dcw_ref, dbin_ref):
        i = pl.program_id(0)

        @pl.when(i == 0)
        def _():
            for ref in (dws_ref, dbs_ref, dgv_ref, dbv_ref, dcw_ref, dbin_ref):
                ref[...] = jnp.zeros_like(ref)

        causal = lax.broadcasted_iota(jnp.int32, (CHUNK, CHUNK), 0) >= lax.broadcasted_iota(jnp.int32, (CHUNK, CHUNK), 1)
        for h in range(nh):
            zu, zv, u, vhat, rstd, vln = _sgu_head_forward(z_ref, h, da, gv_ref, bv_ref)
            dya = dy_ref[0, :, h * HEAD:(h + 1) * HEAD].astype(F32)
            w = ws_ref[h]
            du_parts, dvln_parts = [], []
            for n in range(tm // CHUNK):
                rows = slice(n * CHUNK, (n + 1) * CHUNK)
                mixed = jnp.dot(w, vln[rows], preferred_element_type=F32) + bs_ref[h]
                du_parts.append(dya[rows] * mixed)
                dmix = dya[rows] * u[rows]
                dmix_b = dmix.astype(BF16)
                dws_ref[h] += jnp.where(causal, lax.dot_general(dmix_b, vln[rows], NT_DIMS, preferred_element_type=F32), 0.0)
                dbs_ref[h] += dmix
                dvln_parts.append(lax.dot_general(w, dmix_b, TN_DIMS, preferred_element_type=F32))
            du = jnp.concatenate(du_parts, axis=0)
            dvln = jnp.concatenate(dvln_parts, axis=0)
            dgv_ref[h:h + 1, :] += jnp.sum(dvln * vhat, axis=0, keepdims=True)
            dbv_ref[h:h + 1, :] += jnp.sum(dvln, axis=0, keepdims=True)
            dv = _ln_backward(dvln, vhat, rstd, gv_ref[h:h + 1, :])
            dzu = du * _gelu_grad(*zu)
            dzv = dv * _gelu_grad(*zv)
            ucols = slice(h * HEAD, (h + 1) * HEAD)
            vcols = slice(da + h * HEAD, da + (h + 1) * HEAD)
            dz_ref[:, ucols] = dzu.astype(BF16)
            dz_ref[:, vcols] = dzv.astype(BF16)
            dbin_ref[:, ucols] += jnp.sum(dzu, axis=0, keepdims=True)
            dbin_ref[:, vcols] += jnp.sum(dzv, axis=0, keepdims=True)

        gate_b = z_ref[:, 2 * da:3 * da].astype(F32)
        gate_c = z_ref[:, 3 * da:4 * da].astype(F32)
        xt = z_ref[:, 4 * da:5 * da].astype(F32)
        hc = gate_c * xt
        halo = jnp.where(i > 0, pc_ref[...].astype(F32) * px_ref[...].astype(F32), 0.0)
        row = lax.broadcasted_iota(jnp.int32, (tm, da), 0)
        sh1 = _shift_rows_down(hc, halo, 1, row)
        sh2 = _shift_rows_down(hc, halo, 2, row)
        y = cw_ref[0:1, :] * sh2 + cw_ref[1:2, :] * sh1 + cw_ref[2:3, :] * hc
        dyb = dy_ref[1].astype(F32)
        dconv = dyb * gate_b
        nhalo = jnp.where(i < nblk - 1, ndy_ref[...].astype(F32) * nb_ref[...].astype(F32), 0.0)
        dhc = cw_ref[2:3, :] * dconv + cw_ref[1:2, :] * _shift_rows_up(dconv, nhalo, 1, row) + cw_ref[0:1, :] * _shift_rows_up(dconv, nhalo, 2, row)
        dcw_ref[0:1, :] += jnp.sum(dconv * sh2, axis=0, keepdims=True)
        dcw_ref[1:2, :] += jnp.sum(dconv * sh1, axis=0, keepdims=True)
        dcw_ref[2:3, :] += jnp.sum(dconv * hc, axis=0, keepdims=True)
        for col, val in ((2, dyb * y), (3, dhc * xt), (4, dhc * gate_c)):
            cols = slice(col * da, (col + 1) * da)
            dz_ref[:, cols] = val.astype(BF16)
            dbin_ref[:, cols] += jnp.sum(val, axis=0, keepdims=True)

        @pl.when(i == nblk - 1)
        def _():
            for h in range(nh):
                dbs_ref[h] = jnp.broadcast_to(jnp.sum(dbs_ref[h], axis=1, keepdims=True), (CHUNK, HEAD))

    prev = lambda col: pl.BlockSpec((BF16_ROWS, da), lambda i: (jnp.maximum(i * hb - 1, 0), col))
    nxt = lambda i: jnp.minimum((i + 1) * hb, s // BF16_ROWS - 1)
    in_specs = [pl.BlockSpec((tm, zc), lambda i: (i, 0)), prev(3), prev(4), pl.BlockSpec((BF16_ROWS, da), lambda i: (nxt(i), 2)),
                pl.BlockSpec((2, tm, da), lambda i: (0, i, 0)), pl.BlockSpec((None, BF16_ROWS, da), lambda i: (1, nxt(i), 0)),
                _row((nh, CHUNK, CHUNK)), _row((nh, CHUNK, HEAD)), _row((nh, HEAD)), _row((nh, HEAD)), _row((CONV_TAPS, da))]
    out_specs = [pl.BlockSpec((tm, zc), lambda i: (i, 0)), _row((nh, CHUNK, CHUNK)), _row((nh, CHUNK, HEAD)), _row((nh, HEAD)),
                 _row((nh, HEAD)), _row((8, da)), _row((1, zc))]
    out_shape = [jax.ShapeDtypeStruct((s, zc), BF16), jax.ShapeDtypeStruct((nh, CHUNK, CHUNK), F32),
                 jax.ShapeDtypeStruct((nh, CHUNK, HEAD), F32), jax.ShapeDtypeStruct((nh, HEAD), F32),
                 jax.ShapeDtypeStruct((nh, HEAD), F32), jax.ShapeDtypeStruct((8, da), F32), jax.ShapeDtypeStruct((1, zc), F32)]
    return _call(body, name, (nblk,), in_specs, out_specs, out_shape, [], [z, z, z, z, dy, dy, ws_masked, bs_wide, gv, bv, cw], rider)


def _adam_update(g, w, m, v):
    m_new = ADAM_B1 * m + (1.0 - ADAM_B1) * g
    v_new = ADAM_B2 * v + (1.0 - ADAM_B2) * (g * g)
    m_hat = m_new / (1.0 - ADAM_B1 ** ADAM_STEP)
    v_hat = v_new / (1.0 - ADAM_B2 ** ADAM_STEP)
    return -ADAM_LR * (m_hat / (jnp.sqrt(v_hat) + ADAM_EPS) + ADAM_WD * w), m_new, v_new


def _adamw(gparts, w, m, v, name):
    n, r, c = gparts.shape
    tr = r // 4 if (r // 4) % BF16_ROWS == 0 else r

    def body(g_ref, w_ref, m_ref, v_ref, go_ref, d_ref, mo_ref, vo_ref):
        g = g_ref[0].astype(F32)
        for q in range(1, n):
            g = g + g_ref[q].astype(F32)
        go_ref[...] = g
        d_ref[...], mo_ref[...], vo_ref[...] = _adam_update(g, w_ref[...], m_ref[...], v_ref[...])

    blk = pl.BlockSpec((tr, c), lambda i: (i, 0))
    shp = jax.ShapeDtypeStruct((r, c), F32)
    return _call(body, name, (r // tr,), [pl.BlockSpec((n, tr, c), lambda i: (0, i, 0)), blk, blk, blk], [blk] * 4, [shp] * 4, [],
                 [gparts, w, m, v])[0]


def _adamw_small(packs, rows, w, m, v, conv, loss_scale, name):
    n_par, n_dev = len(rows), packs.shape[0]
    taps = conv[0].shape[0]

    def body(*refs):
        refs = list(refs)
        cut = lambda n: [refs.pop(0) for _ in range(n)]
        p_ref, w_refs, m_refs, v_refs, (cw_ref, cm_ref, cv_ref) = refs.pop(0), cut(n_par), cut(n_par), cut(n_par), cut(3)
        outs = [cut(4) for _ in range(n_par + 1)]
        loss_ref = refs.pop(0)
        at = 0
        for k in range(n_par):
            g = p_ref[0, at:at + rows[k], :]
            for dev in range(1, n_dev):
                g = g + p_ref[dev, at:at + rows[k], :]
            go_ref, d_ref, mo_ref, vo_ref = outs[k]
            go_ref[...] = g
            d_ref[...], mo_ref[...], vo_ref[...] = _adam_update(g, w_refs[k][...], m_refs[k][...], v_refs[k][...])
            at += rows[k]
        me = 4 * lax.axis_index("x") + 2 * lax.axis_index("y") + lax.axis_index("c")
        go_ref, d_ref, mo_ref, vo_ref = outs[n_par]
        for tap in range(taps):
            row = pl.ds(at + tap * n_dev + me, 1)
            g = p_ref[0, row, :]
            for dev in range(1, n_dev):
                g = g + p_ref[dev, row, :]
            one = slice(tap, tap + 1)
            go_ref[one, :] = g
            d_ref[one, :], mo_ref[one, :], vo_ref[one, :] = _adam_update(g, cw_ref[one, :], cm_ref[one, :], cv_ref[one, :])
        at += taps * n_dev
        sq = p_ref[0, at:, :]
        for dev in range(1, n_dev):
            sq = sq + p_ref[dev, at:, :]
        total = jnp.sum(jnp.sum(sq, axis=0, keepdims=True), axis=1, keepdims=True)
        loss_ref[...] = jnp.broadcast_to(loss_scale * total, loss_ref.shape)

    vmem = pl.BlockSpec(memory_space=pltpu.VMEM)
    ins = [packs] + list(w) + list(m) + list(v) + list(conv)
    out_shape = [jax.ShapeDtypeStruct(a.shape, F32) for a in list(w) + [conv[0]] for _ in range(4)] + [jax.ShapeDtypeStruct((8, LANES), F32)]
    outs = pl.pallas_call(body, name=name, in_specs=[vmem] * len(ins), out_specs=[vmem] * len(out_shape), out_shape=out_shape,
                          compiler_params=pltpu.CompilerParams(vmem_limit_bytes=V7X_VMEM_LIMIT))(*ins)
    return [outs[4 * k:4 * k + 4] for k in range(n_par + 1)], outs[-1][0, 0]


def _rows128(a):
    return a.reshape(-1, LANES)


def kernel(x, ffa_gate, ffa_up, ffa_down, ln_a_g, ln_a_b, w_in, b_in, w_s, b_s, ln_v_g, ln_v_b, conv_w, w_out, b_out, ln_m_g, ln_m_b, ffc_gate, ffc_up, ffc_down, ln_c_g, ln_c_b, loss_target, m_ffa_gate, m_ffa_up, m_ffa_down, m_ln_a_g, m_ln_a_b, m_w_in, m_b_in, m_w_s, m_b_s, m_ln_v_g, m_ln_v_b, m_conv_w, m_w_out, m_b_out, m_ln_m_g, m_ln_m_b, m_ffc_gate, m_ffc_up, m_ffc_down, m_ln_c_g, m_ln_c_b, v_ffa_gate, v_ffa_up, v_ffa_down, v_ln_a_g, v_ln_a_b, v_w_in, v_b_in, v_w_s, v_b_s, v_ln_v_g, v_ln_v_b, v_conv_w, v_w_out, v_b_out, v_ln_m_g, v_ln_m_b, v_ffc_gate, v_ffc_up, v_ffc_down, v_ln_c_g, v_ln_c_b):
    x2, target = x[0], loss_target[0]
    s, d = x2.shape
    da = d // 2
    nh = da // HEAD

    big = dict(ffa_gate=ffa_gate, ffa_up=ffa_up, ffa_down=ffa_down, w_in=w_in, w_out=w_out, ffc_gate=ffc_gate, ffc_up=ffc_up, ffc_down=ffc_down)
    big_m = dict(ffa_gate=m_ffa_gate, ffa_up=m_ffa_up, ffa_down=m_ffa_down, w_in=m_w_in, w_out=m_w_out, ffc_gate=m_ffc_gate, ffc_up=m_ffc_up, ffc_down=m_ffc_down)
    big_v = dict(ffa_gate=v_ffa_gate, ffa_up=v_ffa_up, ffa_down=v_ffa_down, w_in=v_w_in, w_out=v_w_out, ffc_gate=v_ffc_gate, ffc_up=v_ffc_up, ffc_down=v_ffc_down)
    local = lambda k, a: jnp.transpose(a[0]) if k in TRANSPOSED else a[0]
    shard = {k: local(k, w).astype(BF16) for k, w in big.items()}
    conv_rows = jnp.pad(conv_w[0], ((0, 8 - CONV_TAPS), (0, 0)))

    tril = jnp.tril(jnp.ones((CHUNK, CHUNK), dtype=bool))
    ws_masked = jnp.where(tril[None], w_s[0], 0.0).astype(BF16)
    bs_wide = jnp.broadcast_to(b_s[0][:, :, None], (nh, CHUNK, HEAD))
    gv, bv = ln_v_g.reshape(nh, HEAD), ln_v_b.reshape(nh, HEAD)

    full = {}
    xb = x2.astype(BF16)
    (silu_a, udsilu_a, hid_a), (full["ffa_gate"], full["ffa_up"], full["ffa_down"]) = _ffn_gateup_gathering(
        xb, shard["ffa_gate"], shard["ffa_up"], shard["ffa_down"], "ffa_gateup")
    (xhat1, h1b, rstd1), (full["w_in"], full["w_out"], conv_full) = _down_ln(
        hid_a, full["ffa_down"], None, x2, None, ln_a_g, ln_a_b, 0.5, "ffa_down_ln", rider=_Gather([shard["w_in"], shard["w_out"], conv_rows], by_columns=(0,)))
    cw = jnp.transpose(conv_full[:, :CONV_TAPS, :], (1, 0, 2)).reshape(CONV_TAPS, da)
    w_out2 = full["w_out"].reshape(2, da, d)
    z, (full["ffc_gate"],) = _proj_in(h1b, full["w_in"], b_in, "proj_in", _Gather([shard["ffc_gate"]]))
    ycat = _mixer_fwd(z, ws_masked, bs_wide, gv, bv, cw, "mixer_fwd")
    (xhat2, h2b, rstd2), _ = _down_ln(ycat, w_out2, b_out, xhat1, (ln_a_g, ln_a_b), ln_m_g, ln_m_b, 1.0, "proj_out_ln")
    (silu_c, udsilu_c, hid_c), (full["ffc_up"], full["ffc_down"]) = _ffn_gateup_gathering(
        h2b, full["ffc_gate"], shard["ffc_up"], shard["ffc_down"], "ffc_gateup")
    (dr3, dr3b, sq_err, d_ln_c_g, d_ln_c_b), _ = _down_ln(hid_c, full["ffc_down"], None, xhat2, (ln_m_g, ln_m_b), ln_c_g, ln_c_b, 0.5,
                                                          "ffc_down_ln_loss", target=target)

    landed = {}
    (dg_c, du_c), _ = _ffn_bwd_hidden(dr3b, full["ffc_down"], silu_c, udsilu_c, "ffc_bwd_hidden")
    part, _ = _dw_hidden_rows_paired(hid_c, dr3b, "ffc_dw_down")
    part, (landed["ffc_down"],) = _dw_hidden_rows_paired(dg_c, h2b, "ffc_dw_gate", _ChipScatter(part))
    part, (landed["ffc_gate"],) = _dw_hidden_rows_paired(du_c, h2b, "ffc_dw_up", _ChipScatter(part))
    (dr2, dr2b, d_ln_m_g, d_ln_m_b, d_b_out), (landed["ffc_up"],) = _ffn_bwd_input(
        dg_c, full["ffc_gate"], du_c, full["ffc_up"], dr3, (xhat2, rstd2, ln_m_g), "ffc_bwd_input_ln", _ChipScatter(part))
    dycat = _nt_hidden(dr2b, w_out2, "proj_out_bwd")
    part, _ = _dw_hidden_rows(ycat, dr2b, "proj_out_dw")
    (dz, d_w_s, d_b_s_wide, d_gv, d_bv, d_cw, d_b_in), (landed["w_out"],) = _mixer_bwd(
        z, dycat, ws_masked, bs_wide, gv, bv, cw, "mixer_bwd", _Scatter(part.reshape(N_DEV, d // N_DEV, d)))
    dr1, dr1b, d_ln_a_g, d_ln_a_b = _proj_in_bwd_ln(dz, full["w_in"], dr2, (xhat1, rstd1, ln_a_g), 0.5, "proj_in_bwd_ln")
    small_g = dict(ln_a_g=d_ln_a_g, ln_a_b=d_ln_a_b, b_in=d_b_in, w_s=d_w_s, b_s=d_b_s_wide[:, :, 0], ln_v_g=d_gv, ln_v_b=d_bv, b_out=d_b_out,
                   ln_m_g=d_ln_m_g, ln_m_b=d_ln_m_b, ln_c_g=d_ln_c_g, ln_c_b=d_ln_c_b)
    pack = jnp.concatenate([_rows128(g) for g in small_g.values()] + [_rows128(d_cw[:CONV_TAPS]), _rows128(sq_err)], axis=0)
    part, (packs,) = _dw_cols(h1b, dz, "proj_in_dw", _Scatter(pack, whole=True))
    (dg_a, du_a), (landed["w_in"],) = _ffn_bwd_hidden(dr1b, full["ffa_down"], silu_a, udsilu_a, "ffa_bwd_hidden", _Scatter(part, by_columns=True))
    part, _ = _dw_hidden_rows_paired(hid_a, dr1b, "ffa_dw_down")
    part, (landed["ffa_down"],) = _dw_hidden_rows_paired(dg_a, xb, "ffa_dw_gate", _ChipScatter(part))
    part, (landed["ffa_gate"],) = _dw_hidden_rows_paired(du_a, xb, "ffa_dw_up", _ChipScatter(part))
    grad_x, (landed["ffa_up"],) = _ffn_bwd_input(dg_a, full["ffa_gate"], du_a, full["ffa_up"], dr1, None, "ffa_bwd_input", _ChipScatter(part))

    grads, deltas, new_m, new_v = {}, {}, {}, {}
    for k in big:
        out = _adamw(landed[k], local(k, big[k]), local(k, big_m[k]), local(k, big_v[k]), "adamw_" + k)
        grads[k], deltas[k], new_m[k], new_v[k] = ((jnp.transpose(o) if k in TRANSPOSED else o).reshape(big[k].shape) for o in out)

    small = dict(ln_a_g=ln_a_g, ln_a_b=ln_a_b, b_in=b_in, w_s=w_s, b_s=b_s, ln_v_g=ln_v_g, ln_v_b=ln_v_b, b_out=b_out,
                 ln_m_g=ln_m_g, ln_m_b=ln_m_b, ln_c_g=ln_c_g, ln_c_b=ln_c_b)
    small_m = dict(ln_a_g=m_ln_a_g, ln_a_b=m_ln_a_b, b_in=m_b_in, w_s=m_w_s, b_s=m_b_s, ln_v_g=m_ln_v_g, ln_v_b=m_ln_v_b, b_out=m_b_out,
                   ln_m_g=m_ln_m_g, ln_m_b=m_ln_m_b, ln_c_g=m_ln_c_g, ln_c_b=m_ln_c_b)
    small_v = dict(ln_a_g=v_ln_a_g, ln_a_b=v_ln_a_b, b_in=v_b_in, w_s=v_w_s, b_s=v_b_s, ln_v_g=v_ln_v_g, ln_v_b=v_ln_v_b, b_out=v_b_out,
                   ln_m_g=v_ln_m_g, ln_m_b=v_ln_m_b, ln_c_g=v_ln_c_g, ln_c_b=v_ln_c_b)
    snames = list(small)
    assert snames == list(small_g) and conv_w.shape[2] == LANES and da == N_DEV * LANES
    views = lambda tree: [_rows128(tree[k]) for k in snames]
    out, loss = _adamw_small(packs, [a.shape[0] for a in views(small)], views(small), views(small_m), views(small_v),
                             (conv_w[0], m_conv_w[0], v_conv_w[0]), 0.5 / d, "adamw_small")
    for k, per_param in zip(snames + ["conv_w"], out):
        shape = conv_w.shape if k == "conv_w" else small[k].shape
        grads[k], deltas[k], new_m[k], new_v[k] = (o.reshape(shape) for o in per_param)

    order = ["ffa_gate", "ffa_up", "ffa_down", "ln_a_g", "ln_a_b", "w_in", "b_in", "w_s", "b_s", "ln_v_g", "ln_v_b", "conv_w", "w_out", "b_out",
             "ln_m_g", "ln_m_b", "ffc_gate", "ffc_up", "ffc_down", "ln_c_g", "ln_c_b"]
    return (loss, grad_x[None], *[grads[k] for k in order], *[deltas[k] for k in order], *[new_m[k] for k in order], *[new_v[k] for k in order])
```

```python
import math

import jax
import jax.numpy as jnp
from jax import lax
from jax.experimental import pallas as pl
from jax.experimental.pallas import tpu as pltpu

BF16 = jnp.bfloat16
F32 = jnp.float32
MESH = pl.DeviceIdType.MESH

N_DEV = 8
HEAD = 128
CHUNK = 128
CONV_TAPS = 3
LN_EPS = 1e-5
ALPHA = float(2 ** 0.25)
GELU_C = 0.7978845608028654
GELU_A = 0.044715
ADAM_LR, ADAM_B1, ADAM_B2, ADAM_EPS, ADAM_WD, ADAM_STEP = 0.001, 0.9, 0.999, 1e-08, 0.01, 10
V7X_VMEM_LIMIT = 56 * 1024 * 1024
LANES = 128
BF16_ROWS = 16
MXU_COLS = 256
TRANSPOSED = ("ffa_gate", "ffa_up", "ffc_gate", "ffc_up")
PROJ_IN_COLS = 1280
MIXER_TOKENS = 512
DW_TOKENS = 2048

NT_DIMS = (((1,), (1,)), ((), ()))
TN_DIMS = (((0,), (0,)), ((), ()))
ANY = pl.BlockSpec(memory_space=pl.ANY)


def _gelu_tanh(x):
    return jnp.tanh(GELU_C * (x + GELU_A * x * x * x))


def _gelu(x, t):
    return 0.5 * x * (1.0 + t)


def _gelu_grad(x, t):
    return 0.5 * (1.0 + t) + 0.5 * x * (1.0 - t * t) * GELU_C * (1.0 + 3.0 * GELU_A * x * x)


def _sigmoid(x):
    return 0.5 * jnp.tanh(0.5 * x) + 0.5


def _row(shape):
    return pl.BlockSpec(shape, lambda *_: (0,) * len(shape))


def _row_blocks(tm, rows=128):
    rows = min(rows, tm)
    return [slice(r, r + rows) for r in range(0, tm, rows)]


def _ln_backward(dh, xhat, rstd, gain):
    dxh = dh * gain
    m1 = jnp.mean(dxh, axis=-1, keepdims=True)
    m2 = jnp.mean(dxh * xhat, axis=-1, keepdims=True)
    return rstd * (dxh - m1 - xhat * m2)


def _place():
    x, y, c = lax.axis_index("x"), lax.axis_index("y"), lax.axis_index("c")
    return x, y, c, [(1 - x, y), (x, 1 - y), (1 - x, 1 - y)]


def _other_devices(x, y, c):
    flips = [(bx, by, bc) for bx in (0, 1) for by in (0, 1) for bc in (0, 1)][1:]
    return [(1 - x if bx else x, 1 - y if by else y, 1 - c if bc else c) for bx, by, bc in flips]


class _Gather:
    def __init__(self, shards, forward_at=0.75, by_columns=()):
        n = len(shards)
        self.n, self.forward_at, self.by_columns = n, forward_at, tuple(by_columns)
        self.inputs = list(shards)
        self.out_shapes = [jax.ShapeDtypeStruct((a.shape[0], N_DEV * a.shape[1]) if i in self.by_columns else (N_DEV,) + a.shape, a.dtype)
                           for i, a in enumerate(shards)]
        self.scratch = [pltpu.SemaphoreType.DMA((n, 7)), pltpu.SemaphoreType.DMA((n, 7)), pltpu.SemaphoreType.DMA((n,))]

    def _block(self, outs, a, dev):
        if a in self.by_columns:
            cols = outs[a].shape[1] // N_DEV
            return outs[a].at[:, pl.ds(dev * cols, cols)]
        return outs[a].at[dev]

    def _copy(self, outs, sems, a, k, block, to, src=None):
        dst = self._block(outs, a, block)
        return pltpu.make_async_remote_copy(src_ref=dst if src is None else src, dst_ref=dst, send_sem=sems[0].at[a, k],
                                            recv_sem=sems[1].at[a, k], device_id=to, device_id_type=MESH)

    def start(self, ins, outs, sems, urgent=None):
        x, y, c, chips = _place()
        me = 4 * x + 2 * y + c
        for a in range(self.n):
            pltpu.make_async_copy(ins[a], self._block(outs, a, me), sems[2].at[a]).start()
        urgent = list(range(self.n)) if urgent is None else list(urgent)
        for group in (urgent, [a for a in range(self.n) if a not in urgent]):
            for a in group:
                self._copy(outs, sems, a, 0, me, (x, y, 1 - c), src=ins[a]).start()
                for j in (0, 1):
                    self._copy(outs, sems, a, 1 + j, me, (*chips[j], c), src=ins[a]).start()
            for a in group:
                self._copy(outs, sems, a, 3, me, (*chips[2], c), src=ins[a]).start()

    def wait_sibling(self, outs, sems, a):
        x, y, c, _ = _place()
        self._copy(outs, sems, a, 0, 4 * x + 2 * y + 1 - c, (x, y, 1 - c)).wait_recv()

    def pass_on(self, outs, sems, a, j):
        x, y, c, chips = _place()
        block = 4 * chips[j][0] + 2 * chips[j][1] + c
        self._copy(outs, sems, a, 1 + j, block, (x, y, 1 - c)).wait_recv()
        self._copy(outs, sems, a, 4 + j, block, (x, y, 1 - c)).start()

    def wait_passed(self, outs, sems, a, j):
        x, y, c, chips = _place()
        self._copy(outs, sems, a, 4 + j, 4 * chips[j][0] + 2 * chips[j][1] + 1 - c, (x, y, 1 - c)).wait_recv()

    def wait_sent(self, ins, outs, sems, a):
        x, y, c, _ = _place()
        me = 4 * x + 2 * y + c
        for k in range(7):
            self._copy(outs, sems, a, k, me, (x, y, 1 - c), src=ins[a]).wait_send()
        pltpu.make_async_copy(ins[a], self._block(outs, a, me), sems[2].at[a]).wait()

    def forward(self, ins, outs, sems):
        for a in range(self.n):
            for j in range(3):
                self.pass_on(outs, sems, a, j)

    def finish(self, ins, outs, sems):
        for a in range(self.n):
            self.wait_sibling(outs, sems, a)
            for j in range(3):
                self.wait_passed(outs, sems, a, j)
        for a in range(self.n):
            self.wait_sent(ins, outs, sems, a)

    def before(self, step, n_steps, ins, outs, sems):
        pl.when(step == 0)(lambda: self.start(ins, outs, sems))
        pl.when(step == int(self.forward_at * (n_steps - 1)))(lambda: self.forward(ins, outs, sems))

    def after(self, step, n_steps, ins, outs, sems):
        pl.when(step == n_steps - 1)(lambda: self.finish(ins, outs, sems))


class _Scatter:
    def __init__(self, partial, whole=False, by_columns=False):
        self.whole, self.by_columns = whole, by_columns
        self.inputs = [partial]
        if whole:
            shape = (N_DEV,) + partial.shape
        elif by_columns:
            shape = (N_DEV, partial.shape[0], partial.shape[1] // N_DEV)
        else:
            shape = partial.shape
        self.out_shapes = [jax.ShapeDtypeStruct(shape, partial.dtype)]
        self.scratch = [pltpu.SemaphoreType.DMA((7,)), pltpu.SemaphoreType.DMA((7,)), pltpu.SemaphoreType.DMA((1,))]

    def _copies(self, ins, outs, sems):
        x, y, c, _ = _place()
        me = 4 * x + 2 * y + c
        if self.whole:
            block = lambda dev: ins[0]
        elif self.by_columns:
            cols = ins[0].shape[1] // N_DEV
            block = lambda dev: ins[0].at[:, pl.ds(dev * cols, cols)]
        else:
            block = lambda dev: ins[0].at[dev]
        mine = pltpu.make_async_copy(block(me), outs[0].at[me], sems[2].at[0])
        remote = [pltpu.make_async_remote_copy(src_ref=block(4 * px + 2 * py + pc), dst_ref=outs[0].at[me], send_sem=sems[0].at[k],
                                               recv_sem=sems[1].at[k], device_id=(px, py, pc), device_id_type=MESH)
                  for k, (px, py, pc) in enumerate(_other_devices(x, y, c))]
        return mine, remote

    def start(self, ins, outs, sems):
        mine, remote = self._copies(ins, outs, sems)
        mine.start()
        for cp in remote:
            cp.start()

    def finish(self, ins, outs, sems):
        mine, remote = self._copies(ins, outs, sems)
        for cp in remote:
            cp.wait()
        mine.wait()

    def before(self, step, n_steps, ins, outs, sems):
        pl.when(step == 0)(lambda: self.start(ins, outs, sems))

    def after(self, step, n_steps, ins, outs, sems):
        pl.when(step == n_steps - 1)(lambda: self.finish(ins, outs, sems))


class _ChipScatter(_Scatter):
    def __init__(self, sums):
        super().__init__(sums)
        self.scratch = [pltpu.SemaphoreType.DMA((3,)), pltpu.SemaphoreType.DMA((3,)), pltpu.SemaphoreType.DMA((1,))]

    def _copies(self, ins, outs, sems):
        x, y, c, chips = _place()
        my_chip = 2 * x + y
        mine = pltpu.make_async_copy(ins[0].at[my_chip], outs[0].at[my_chip], sems[2].at[0])
        remote = [pltpu.make_async_remote_copy(src_ref=ins[0].at[2 * px + py], dst_ref=outs[0].at[my_chip], send_sem=sems[0].at[k],
                                               recv_sem=sems[1].at[k], device_id=(px, py, c), device_id_type=MESH)
                  for k, (px, py) in enumerate(chips)]
        return mine, remote


def _call(body, name, grid, in_specs, out_specs, out_shape, scratch, ins, rider=None):
    single = not isinstance(out_shape, (list, tuple))
    out_shape = [out_shape] if single else list(out_shape)
    out_specs = [out_specs] if single else list(out_specs)
    params = pltpu.CompilerParams(dimension_semantics=("arbitrary",) * len(grid), vmem_limit_bytes=V7X_VMEM_LIMIT)
    if rider is None:
        outs = pl.pallas_call(body, name=name, grid=grid, in_specs=in_specs, out_specs=out_specs, out_shape=out_shape,
                              scratch_shapes=scratch, compiler_params=params)(*ins)
        return (outs[0] if single else outs), None
    n_in, n_out, n_scr = len(ins), len(out_shape), len(scratch)
    r_in, r_out = len(rider.inputs), len(rider.out_shapes)
    n_steps = math.prod(grid)

    def carried(*refs):
        refs = list(refs)
        cut = lambda n: [refs.pop(0) for _ in range(n)]
        b_in, c_in, b_out, c_out, b_scr = cut(n_in), cut(r_in), cut(n_out), cut(r_out), cut(n_scr)
        step = 0
        for axis, size in enumerate(grid):
            step = step * size + pl.program_id(axis)
        rider.before(step, n_steps, c_in, c_out, refs)
        body(*b_in, *b_out, *b_scr)
        rider.after(step, n_steps, c_in, c_out, refs)

    outs = pl.pallas_call(
        carried, name=name, grid=grid, in_specs=list(in_specs) + [ANY] * r_in, out_specs=out_specs + [ANY] * r_out,
        out_shape=out_shape + rider.out_shapes, scratch_shapes=list(scratch) + rider.scratch, compiler_params=params,
    )(*ins, *rider.inputs)
    base = outs[:n_out]
    return (base[0] if single else base), outs[n_out:]


def _arrival_block(j):
    x, y, c = lax.axis_index("x"), lax.axis_index("y"), lax.axis_index("c")
    chip, other_core = j // 2, j % 2
    px = jnp.where((chip == 1) | (chip == 3), 1 - x, x)
    py = jnp.where((chip == 2) | (chip == 3), 1 - y, y)
    pc = jnp.where(other_core == 1, 1 - c, c)
    return 4 * px + 2 * py + pc


def _ffn_gateup_gathering(xb, gate, up_shard, down_shard, name):
    s, d = xb.shape
    fs = up_shard.shape[0]
    tm = min(s, 1024)
    ni = s // tm
    ask_at = max(ni - 2, 0)
    gate_here = gate.ndim == 2
    gather = _Gather(([gate] if gate_here else []) + [up_shard, down_shard])
    n_g = gather.n
    used_here, down = tuple(range(n_g - 1)), n_g - 1

    def body(x_ref, *refs):
        refs = list(refs)
        gate_full = None if gate_here else refs.pop(0)
        shards = [refs.pop(0) for _ in range(n_g)]
        silu_ref, udsilu_ref, h_ref = refs.pop(0), refs.pop(0), refs.pop(0)
        fulls = [refs.pop(0) for _ in range(n_g)]
        w_ref, w_sems = refs.pop(0), refs.pop(0)
        sems = refs
        j, i = pl.program_id(0), pl.program_id(1)
        gate_src, up_src = (fulls[0], fulls[1]) if gate_here else (gate_full, fulls[0])

        def load(slot, srcs):
            return [pltpu.make_async_copy(src, w_ref.at[slot, a], w_sems.at[slot, a]) for a, src in enumerate(srcs)]

        @pl.when((j == 0) & (i == 0))
        def _():
            gather.start(shards, fulls, sems, urgent=used_here)
            mine = load(0, (shards[0] if gate_here else gate_full.at[_arrival_block(0)], shards[n_g - 2]))
            for cp in mine:
                cp.start()
            for cp in mine:
                cp.wait()

        for nxt in range(1, N_DEV):
            @pl.when((j == nxt) & (i == 0))
            def _(nxt=nxt):
                for cp in load(nxt % 2, (gate_src.at[0], up_src.at[0])):
                    cp.wait()

        for nxt in range(1, N_DEV):
            @pl.when((j == nxt - 1) & (i == ask_at))
            def _(nxt=nxt):
                for a in used_here:
                    if nxt == 1:
                        gather.wait_sibling(fulls, sems, a)
                    elif nxt % 2 == 0:
                        gather.pass_on(fulls, sems, a, nxt // 2 - 1)
                    else:
                        gather.wait_passed(fulls, sems, a, nxt // 2 - 1)
                block = _arrival_block(nxt)
                for cp in load(nxt % 2, (gate_src.at[block], up_src.at[block])):
                    cp.start()

        @pl.when((j == N_DEV - 1) & (i == ask_at))
        def _():
            for other_chip in range(3):
                gather.pass_on(fulls, sems, down, other_chip)

        x = x_ref[...]
        g = lax.dot_general(x, w_ref[j % 2, 0], NT_DIMS, preferred_element_type=F32)
        u = lax.dot_general(x, w_ref[j % 2, 1], NT_DIMS, preferred_element_type=F32)
        sg = _sigmoid(g)
        silu = g * sg
        silu_ref[...] = silu.astype(BF16)
        udsilu_ref[...] = (u * (sg + silu * (1.0 - sg))).astype(BF16)
        h_ref[...] = (silu * u).astype(BF16)

        @pl.when((j == N_DEV - 1) & (i == ni - 1))
        def _():
            gather.wait_sibling(fulls, sems, down)
            for other_chip in range(3):
                gather.wait_passed(fulls, sems, down, other_chip)
            for a in range(n_g):
                gather.wait_sent(shards, fulls, sems, a)

    shp = jax.ShapeDtypeStruct((N_DEV, s, fs), BF16)
    o_spec = pl.BlockSpec((None, tm, fs), lambda j, i: (_arrival_block(j), i, 0))
    ins = ([] if gate_here else [gate]) + gather.inputs
    outs = pl.pallas_call(
        body, name=name, grid=(N_DEV, ni), in_specs=[pl.BlockSpec((tm, d), lambda j, i: (i, 0))] + [ANY] * len(ins),
        out_specs=[o_spec, o_spec, o_spec] + [ANY] * n_g, out_shape=[shp, shp, shp] + gather.out_shapes,
        scratch_shapes=[pltpu.VMEM((2, 2, fs, d), BF16), pltpu.SemaphoreType.DMA((2, 2))] + gather.scratch,
        compiler_params=pltpu.CompilerParams(dimension_semantics=("arbitrary", "arbitrary"), vmem_limit_bytes=V7X_VMEM_LIMIT),
    )(xb, *ins)
    return outs[:3], outs[3:]


def _down_ln(a3, w3, bias, res, res_affine, ln_g, ln_b, scale, name, target=None, rider=None):
    nk, s, tk = a3.shape
    d = w3.shape[2]
    tm = min(s, 256)
    final = target is not None

    def body(*refs):
        refs = list(refs)
        a_ref, w_hbm = refs[:2]
        del refs[:2]
        bias_ref = refs.pop(0) if bias is not None else None
        res_ref = refs.pop(0)
        rg_ref, rb_ref = (refs.pop(0), refs.pop(0)) if res_affine is not None else (None, None)
        g_ref, b_ref = refs.pop(0), refs.pop(0)
        t_ref = refs.pop(0) if final else None
        w_sem = refs.pop()
        w_ref = refs.pop()
        i = pl.program_id(0)
        if final:
            dr_ref, drb_ref, sq_ref, dg_ref, db_ref = refs
        else:
            xh_ref, hb_ref, rstd_ref = refs

        @pl.when(i == 0)
        def _():
            whole = pltpu.make_async_copy(w_hbm, w_ref, w_sem.at[0])
            whole.start()
            whole.wait()
            if final:
                sq_ref[...] = jnp.zeros_like(sq_ref)
                dg_ref[...] = jnp.zeros_like(dg_ref)
                db_ref[...] = jnp.zeros_like(db_ref)

        y = jnp.dot(a_ref[0], w_ref[0], preferred_element_type=F32)
        for k in range(1, nk):
            y = y + jnp.dot(a_ref[k], w_ref[k], preferred_element_type=F32)
        if bias_ref is not None:
            y = y + bias_ref[...]
        for rows in _row_blocks(tm):
            r = res_ref[rows, :]
            if rg_ref is not None:
                r = r * rg_ref[...] + rb_ref[...]
            r = ALPHA * r + scale * y[rows]
            mu = jnp.mean(r, axis=-1, keepdims=True)
            c = r - mu
            var = jnp.mean(c * c, axis=-1, keepdims=True)
            rstd = lax.rsqrt(var + LN_EPS)
            xhat = c * rstd
            h = xhat * g_ref[...] + b_ref[...]
            if not final:
                xh_ref[rows, :] = xhat
                hb_ref[rows, :] = h.astype(BF16)
                rstd_ref[rows, :] = rstd
            else:
                err = h - t_ref[rows, :]
                sq_ref[...] += jnp.sum(err * err, axis=0, keepdims=True)
                dh = err * (1.0 / d)
                dg_ref[...] += jnp.sum(dh * xhat, axis=0, keepdims=True)
                db_ref[...] += jnp.sum(dh, axis=0, keepdims=True)
                dr = _ln_backward(dh, xhat, rstd, g_ref[...])
                dr_ref[rows, :] = dr
                drb_ref[rows, :] = (scale * dr).astype(BF16)

    tok = pl.BlockSpec((tm, d), lambda i: (i, 0))
    vec = pl.BlockSpec((1, d), lambda i: (0, 0))
    ins = [a3, w3]
    in_specs = [pl.BlockSpec((nk, tm, tk), lambda i: (0, i, 0)), ANY]
    if bias is not None:
        ins.append(bias)
        in_specs.append(vec)
    ins.append(res)
    in_specs.append(tok)
    if res_affine is not None:
        ins += list(res_affine)
        in_specs += [vec, vec]
    ins += [ln_g, ln_b]
    in_specs += [vec, vec]
    if final:
        ins.append(target)
        in_specs.append(tok)
        out_shape = [jax.ShapeDtypeStruct((s, d), F32), jax.ShapeDtypeStruct((s, d), BF16)] + [jax.ShapeDtypeStruct((1, d), F32)] * 3
        out_specs = [tok, tok, vec, vec, vec]
    else:
        out_shape = [jax.ShapeDtypeStruct((s, d), F32), jax.ShapeDtypeStruct((s, d), BF16), jax.ShapeDtypeStruct((s, 1), F32)]
        out_specs = [tok, tok, pl.BlockSpec((tm, 1), lambda i: (i, 0))]
    scratch = [pltpu.VMEM((nk, tk, d), BF16), pltpu.SemaphoreType.DMA((1,))]
    return _call(body, name, (s // tm,), in_specs, out_specs, out_shape, scratch, ins, rider)


def _proj_in(hb, w, bias, name, rider=None):
    s, d = hb.shape
    n = w.shape[1]
    tm = min(s, 1024)
    tn = PROJ_IN_COLS if n % PROJ_IN_COLS == 0 else n

    def body(h_ref, w_ref, b_ref, z_ref):
        z_ref[...] = (jnp.dot(h_ref[...], w_ref[...], preferred_element_type=F32) + b_ref[...]).astype(BF16)

    in_specs = [pl.BlockSpec((tm, d), lambda i, j: (i, 0)), pl.BlockSpec((d, tn), lambda i, j: (0, j)),
                pl.BlockSpec((1, tn), lambda i, j: (0, j))]
    return _call(body, name, (s // tm, n // tn), in_specs, pl.BlockSpec((tm, tn), lambda i, j: (i, j)),
                 jax.ShapeDtypeStruct((s, n), BF16), [], [hb, w, bias], rider)


def _nt_hidden(ab, w3, name):
    s, kdim = ab.shape
    nj, tn, _ = w3.shape
    tm = min(s, 1024)

    def body(a_ref, w_ref, o_ref):
        o_ref[...] = lax.dot_general(a_ref[...], w_ref[...], NT_DIMS, preferred_element_type=F32).astype(BF16)

    in_specs = [pl.BlockSpec((tm, kdim), lambda i, j: (i, 0)), pl.BlockSpec((None, tn, kdim), lambda i, j: (j, 0, 0))]
    return _call(body, name, (s // tm, nj), in_specs, pl.BlockSpec((None, tm, tn), lambda i, j: (j, i, 0)),
                 jax.ShapeDtypeStruct((nj, s, tn), BF16), [], [ab, w3])[0]


def _ffn_bwd_hidden(ab, w3, silu3, udsilu3, name, rider=None):
    s, kdim = ab.shape
    nj, tn, _ = w3.shape
    tm = min(s, 1024)

    def body(a_ref, w_ref, silu_ref, udsilu_ref, dg_ref, du_ref):
        a = a_ref[...]
        for c0 in range(0, tn, MXU_COLS):
            cols = slice(c0, min(c0 + MXU_COLS, tn))
            t = lax.dot_general(a, w_ref[cols, :], NT_DIMS, preferred_element_type=F32)
            du_ref[:, cols] = (t * silu_ref[:, cols].astype(F32)).astype(BF16)
            dg_ref[:, cols] = (t * udsilu_ref[:, cols].astype(F32)).astype(BF16)

    hid = pl.BlockSpec((None, tm, tn), lambda i, j: (j, i, 0))
    shp = jax.ShapeDtypeStruct((nj, s, tn), BF16)
    in_specs = [pl.BlockSpec((tm, kdim), lambda i, j: (i, 0)), pl.BlockSpec((None, tn, kdim), lambda i, j: (j, 0, 0)), hid, hid]
    return _call(body, name, (s // tm, nj), in_specs, [hid, hid], [shp, shp], [], [ab, w3, silu3, udsilu3], rider)


def _tn_dw(a, a_spec, b, b_spec, nj, m, n, s, tk, name, rider):
    def body(a_ref, b_ref, o_ref, acc_ref):
        k = pl.program_id(1)

        @pl.when(k == 0)
        def _():
            acc_ref[...] = jnp.zeros_like(acc_ref)

        acc_ref[...] += lax.dot_general(a_ref[...], b_ref[...], TN_DIMS, preferred_element_type=F32)

        @pl.when(k == s // tk - 1)
        def _():
            o_ref[...] = acc_ref[...].astype(BF16)

    return _call(body, name, (nj, s // tk), [a_spec, b_spec], pl.BlockSpec((None, m, n), lambda j, k: (j, 0, 0)),
                 jax.ShapeDtypeStruct((nj, m, n), BF16), [pltpu.VMEM((m, n), F32)], [a, b], rider)


def _dw_hidden_rows(hid3, db, name, rider=None):
    nj, s, fs = hid3.shape
    d = db.shape[1]
    tk = min(s, DW_TOKENS)
    return _tn_dw(hid3, pl.BlockSpec((None, tk, fs), lambda j, k: (j, k, 0)), db, pl.BlockSpec((tk, d), lambda j, k: (k, 0)),
                  nj, fs, d, s, tk, name, rider)


def _dw_hidden_rows_paired(hid3, db, name, rider=None):
    nj, s, fs = hid3.shape
    d = db.shape[1]
    halves = 2
    dh = d // halves
    tk = min(s, halves * DW_TOKENS)
    nk = s // tk
    half = nj // 2
    n_pieces = half * halves

    def device_of(j):
        c = lax.axis_index("c")
        return 2 * (j % half) + jnp.where(j < half, 1 - c, c)

    def body(a_ref, b_ref, o_ref, theirs_ref, acc_ref, stage_ref, got_ref, send_sems, recv_sems, load_sem):
        j, r, k = pl.program_id(0), pl.program_id(1), pl.program_id(2)
        x, y, c = lax.axis_index("x"), lax.axis_index("y"), lax.axis_index("c")

        def landing(piece):
            q, col = divmod(piece, halves)
            return theirs_ref.at[q, :, pl.ds(col * dh, dh)]

        def to_sibling(piece):
            return pltpu.make_async_remote_copy(src_ref=stage_ref, dst_ref=landing(piece), send_sem=send_sems.at[piece],
                                                recv_sem=recv_sems.at[piece], device_id=(x, y, 1 - c), device_id_type=MESH)

        def fetch(piece):
            return pltpu.make_async_copy(landing(piece), got_ref, load_sem.at[0])

        @pl.when(k == 0)
        def _():
            acc_ref[...] = jnp.zeros_like(acc_ref)

        acc_ref[...] += lax.dot_general(a_ref[...], b_ref[...], TN_DIMS, preferred_element_type=F32)

        for piece in range(n_pieces):
            q, col = divmod(piece, halves)

            @pl.when((j == q) & (r == col) & (k == nk - 1))
            def _(piece=piece):
                if piece > 0:
                    to_sibling(piece - 1).wait_send()
                stage_ref[...] = acc_ref[...].astype(BF16)
                to_sibling(piece).start()

            @pl.when((j == half + q) & (r == col) & (k == 0))
            def _(piece=piece):
                to_sibling(piece).wait_recv()
                fetch(piece).start()

            @pl.when((j == half + q) & (r == col) & (k == nk - 1))
            def _(piece=piece):
                if piece == 0:
                    to_sibling(n_pieces - 1).wait_send()
                fetch(piece).wait()
                o_ref[...] = (acc_ref[...] + got_ref[...].astype(F32)).astype(BF16)

    mine = lambda j, r: jnp.where(j < half, 0, r)
    in_specs = [pl.BlockSpec((None, tk, fs), lambda j, r, k: (device_of(j), k, 0)), pl.BlockSpec((tk, dh), lambda j, r, k: (k, r))]
    out_specs = [pl.BlockSpec((None, fs, dh), lambda j, r, k: (jnp.maximum(j - half, 0), 0, mine(j, r))), ANY]
    shp = jax.ShapeDtypeStruct((half, fs, d), BF16)
    scratch = [pltpu.VMEM((fs, dh), F32), pltpu.VMEM((fs, dh), BF16), pltpu.VMEM((fs, dh), BF16),
               pltpu.SemaphoreType.DMA((n_pieces,)), pltpu.SemaphoreType.DMA((n_pieces,)), pltpu.SemaphoreType.DMA((1,))]
    (sums, _), riders_out = _call(body, name, (nj, halves, nk), in_specs, out_specs, [shp, shp], scratch, [hid3, db], rider)
    return sums, riders_out


def _dw_cols(ab, dz, name, rider=None):
    s, d = ab.shape
    n = dz.shape[1]
    tn = PROJ_IN_COLS if n % PROJ_IN_COLS == 0 else n
    tk = min(s, DW_TOKENS)
    tr = d // 2

    def body(a_ref, b_ref, o_ref, acc_ref):
        k = pl.program_id(2)

        @pl.when(k == 0)
        def _():
            acc_ref[...] = jnp.zeros_like(acc_ref)

        acc_ref[...] += lax.dot_general(a_ref[...], b_ref[...], TN_DIMS, preferred_element_type=F32)

        @pl.when(k == s // tk - 1)
        def _():
            o_ref[...] = acc_ref[...].astype(BF16)

    in_specs = [pl.BlockSpec((tk, tr), lambda j, r, k: (k, r)), pl.BlockSpec((tk, tn), lambda j, r, k: (k, j))]
    return _call(body, name, (n // tn, d // tr, s // tk), in_specs, pl.BlockSpec((tr, tn), lambda j, r, k: (r, j)),
                 jax.ShapeDtypeStruct((d, n), BF16), [pltpu.VMEM((tr, tn), F32)], [ab, dz], rider)


def _ffn_bwd_input(dg3, wg3, du3, wu3, dres, ln, name, rider=None):
    s, d = dres.shape
    nk, _, fs = dg3.shape
    tm = min(s, 512)

    def body(*refs):
        refs = list(refs)
        dg_in, wg_ref, du_in, wu_ref, dres_ref = refs[:5]
        del refs[:5]
        if ln is not None:
            xh_ref, rstd_ref, gain_ref = refs.pop(0), refs.pop(0), refs.pop(0)
        acc_ref = refs.pop()
        i, k = pl.program_id(0), pl.program_id(1)

        @pl.when(k == 0)
        def _():
            acc_ref[...] = jnp.zeros_like(acc_ref)

        acc_ref[...] += (jnp.dot(dg_in[...], wg_ref[...], preferred_element_type=F32)
                         + jnp.dot(du_in[...], wu_ref[...], preferred_element_type=F32))

        @pl.when(k == nk - 1)
        def _():
            if ln is not None:
                dr_ref, drb_ref, dg_ref, db_ref, sum_ref = refs

                @pl.when(i == 0)
                def _():
                    dg_ref[...] = jnp.zeros_like(dg_ref)
                    db_ref[...] = jnp.zeros_like(db_ref)
                    sum_ref[...] = jnp.zeros_like(sum_ref)

            for rows in _row_blocks(tm):
                dh = ALPHA * dres_ref[rows, :] + acc_ref[rows, :]
                if ln is None:
                    refs[0][rows, :] = dh
                else:
                    xhat = xh_ref[rows, :]
                    dg_ref[...] += jnp.sum(dh * xhat, axis=0, keepdims=True)
                    db_ref[...] += jnp.sum(dh, axis=0, keepdims=True)
                    dr = _ln_backward(dh, xhat, rstd_ref[rows, :], gain_ref[...])
                    sum_ref[...] += jnp.sum(dr, axis=0, keepdims=True)
                    dr_ref[rows, :] = dr
                    drb_ref[rows, :] = dr.astype(BF16)

    tok = pl.BlockSpec((tm, d), lambda i, k: (i, 0))
    vec = pl.BlockSpec((1, d), lambda i, k: (0, 0))
    a_spec = pl.BlockSpec((None, tm, fs), lambda i, k: (k, i, 0))
    w_spec = pl.BlockSpec((None, fs, d), lambda i, k: (k, 0, 0))
    ins, in_specs = [dg3, wg3, du3, wu3, dres], [a_spec, w_spec, a_spec, w_spec, tok]
    if ln is None:
        out_shape, out_specs = jax.ShapeDtypeStruct((s, d), F32), tok
    else:
        ins += list(ln)
        in_specs += [tok, pl.BlockSpec((tm, 1), lambda i, k: (i, 0)), vec]
        out_shape = [jax.ShapeDtypeStruct((s, d), F32), jax.ShapeDtypeStruct((s, d), BF16)] + [jax.ShapeDtypeStruct((1, d), F32)] * 3
        out_specs = [tok, tok, vec, vec, vec]
    return _call(body, name, (s // tm, nk), in_specs, out_specs, out_shape, [pltpu.VMEM((tm, d), F32)], ins, rider)


def _proj_in_bwd_ln(dz, w, dres, ln, branch_scale, name):
    s, d = dres.shape
    n = w.shape[1]
    tm = min(s, 256)

    def body(dz_ref, w_hbm, dres_ref, xh_ref, rstd_ref, gain_ref, dr_ref, drb_ref, dg_ref, db_ref, w_ref, w_sem):
        @pl.when(pl.program_id(0) == 0)
        def _():
            whole = pltpu.make_async_copy(w_hbm, w_ref, w_sem.at[0])
            whole.start()
            whole.wait()
            dg_ref[...] = jnp.zeros_like(dg_ref)
            db_ref[...] = jnp.zeros_like(db_ref)

        acc = lax.dot_general(dz_ref[...], w_ref[...], NT_DIMS, preferred_element_type=F32)
        for rows in _row_blocks(tm):
            dh = ALPHA * dres_ref[rows, :] + acc[rows]
            xhat = xh_ref[rows, :]
            dg_ref[...] += jnp.sum(dh * xhat, axis=0, keepdims=True)
            db_ref[...] += jnp.sum(dh, axis=0, keepdims=True)
            dr = _ln_backward(dh, xhat, rstd_ref[rows, :], gain_ref[...])
            dr_ref[rows, :] = dr
            drb_ref[rows, :] = (branch_scale * dr).astype(BF16)

    tok = pl.BlockSpec((tm, d), lambda i: (i, 0))
    vec = pl.BlockSpec((1, d), lambda i: (0, 0))
    in_specs = [pl.BlockSpec((tm, n), lambda i: (i, 0)), ANY, tok, tok, pl.BlockSpec((tm, 1), lambda i: (i, 0)), vec]
    out_shape = [jax.ShapeDtypeStruct((s, d), F32), jax.ShapeDtypeStruct((s, d), BF16)] + [jax.ShapeDtypeStruct((1, d), F32)] * 2
    scratch = [pltpu.VMEM(w.shape, BF16), pltpu.SemaphoreType.DMA((1,))]
    return _call(body, name, (s // tm,), in_specs, [tok, tok, vec, vec], out_shape, scratch, [dz, w, dres] + list(ln))[0]


def _shift_rows_down(v, halo, k, row):
    out = pltpu.roll(v, k, 0)
    hr = halo.shape[0]
    for r in range(k):
        out = jnp.where(row == r, halo[hr - k + r:hr - k + r + 1, :], out)
    return out


def _shift_rows_up(v, halo, k, row):
    t = v.shape[0]
    out = pltpu.roll(v, t - k, 0)
    for r in range(k):
        out = jnp.where(row == t - k + r, halo[r:r + 1, :], out)
    return out


def _sgu_head_forward(z_ref, h, da, gv_ref, bv_ref):
    zu = z_ref[:, h * HEAD:(h + 1) * HEAD].astype(F32)
    zv = z_ref[:, da + h * HEAD:da + (h + 1) * HEAD].astype(F32)
    tu, tv = _gelu_tanh(zu), _gelu_tanh(zv)
    u = _gelu(zu, tu)
    v = _gelu(zv, tv)
    mu = jnp.mean(v, axis=-1, keepdims=True)
    c = v - mu
    rstd = lax.rsqrt(jnp.mean(c * c, axis=-1, keepdims=True) + LN_EPS)
    vhat = c * rstd
    vln = (vhat * gv_ref[h:h + 1, :] + bv_ref[h:h + 1, :]).astype(BF16)
    return (zu, tu), (zv, tv), u, vhat, rstd, vln


def _mixer_fwd(z, ws_masked, bs_wide, gv, bv, cw, name):
    s, zc = z.shape
    da = zc // 5
    nh = da // HEAD
    tm = min(s, MIXER_TOKENS)
    hb = tm // BF16_ROWS

    def body(z_ref, pc_ref, px_ref, ws_ref, bs_ref, gv_ref, bv_ref, cw_ref, y_ref):
        i = pl.program_id(0)
        for h in range(nh):
            _, _, u, _, _, vln = _sgu_head_forward(z_ref, h, da, gv_ref, bv_ref)
            for n in range(tm // CHUNK):
                rows = slice(n * CHUNK, (n + 1) * CHUNK)
                mixed = jnp.dot(ws_ref[h], vln[rows], preferred_element_type=F32) + bs_ref[h]
                y_ref[0, rows, h * HEAD:(h + 1) * HEAD] = (u[rows] * mixed).astype(BF16)
        gate_b = z_ref[:, 2 * da:3 * da].astype(F32)
        hc = z_ref[:, 3 * da:4 * da].astype(F32) * z_ref[:, 4 * da:5 * da].astype(F32)
        halo = jnp.where(i > 0, pc_ref[...].astype(F32) * px_ref[...].astype(F32), 0.0)
        row = lax.broadcasted_iota(jnp.int32, (tm, da), 0)
        y = cw_ref[0:1, :] * _shift_rows_down(hc, halo, 2, row) + cw_ref[1:2, :] * _shift_rows_down(hc, halo, 1, row) + cw_ref[2:3, :] * hc
        y_ref[1] = (gate_b * y).astype(BF16)

    prev = lambda col: pl.BlockSpec((BF16_ROWS, da), lambda i: (jnp.maximum(i * hb - 1, 0), col))
    in_specs = [pl.BlockSpec((tm, zc), lambda i: (i, 0)), prev(3), prev(4), _row((nh, CHUNK, CHUNK)), _row((nh, CHUNK, HEAD)),
                _row((nh, HEAD)), _row((nh, HEAD)), _row((CONV_TAPS, da))]
    return _call(body, name, (s // tm,), in_specs, pl.BlockSpec((2, tm, da), lambda i: (0, i, 0)),
                 jax.ShapeDtypeStruct((2, s, da), BF16), [], [z, z, z, ws_masked, bs_wide, gv, bv, cw])[0]


def _mixer_bwd(z, dy, ws_masked, bs_wide, gv, bv, cw, name, rider=None):
    s, zc = z.shape
    da = zc // 5
    nh = da // HEAD
    tm = min(s, MIXER_TOKENS)
    hb = tm // BF16_ROWS
    nblk = s // tm

    def body(z_ref, pc_ref, px_ref, nb_ref, dy_ref, ndy_ref, ws_ref, bs_ref, gv_ref, bv_ref, cw_ref,
             dz_ref, dws_ref, dbs_ref, dgv_ref, dbv_ref, dcw_ref, dbin_ref):
        i = pl.program_id(0)

        @pl.when(i == 0)
        def _():
            for ref in (dws_ref, dbs_ref, dgv_ref, dbv_ref, dcw_ref, dbin_ref):
                ref[...] = jnp.zeros_like(ref)

        causal = lax.broadcasted_iota(jnp.int32, (CHUNK, CHUNK), 0) >= lax.broadcasted_iota(jnp.int32, (CHUNK, CHUNK), 1)
        for h in range(nh):
            zu, zv, u, vhat, rstd, vln = _sgu_head_forward(z_ref, h, da, gv_ref, bv_ref)
            dya = dy_ref[0, :, h * HEAD:(h + 1) * HEAD].astype(F32)
            w = ws_ref[h]
            du_parts, dvln_parts = [], []
            for n in range(tm // CHUNK):
                rows = slice(n * CHUNK, (n + 1) * CHUNK)
                mixed = jnp.dot(w, vln[rows], preferred_element_type=F32) + bs_ref[h]
                du_parts.append(dya[rows] * mixed)
                dmix = dya[rows] * u[rows]
                dmix_b = dmix.astype(BF16)
                dws_ref[h] += jnp.where(causal, lax.dot_general(dmix_b, vln[rows], NT_DIMS, preferred_element_type=F32), 0.0)
                dbs_ref[h] += dmix
                dvln_parts.append(lax.dot_general(w, dmix_b, TN_DIMS, preferred_element_type=F32))
            du = jnp.concatenate(du_parts, axis=0)
            dvln = jnp.concatenate(dvln_parts, axis=0)
            dgv_ref[h:h + 1, :] += jnp.sum(dvln * vhat, axis=0, keepdims=True)
            dbv_ref[h:h + 1, :] += jnp.sum(dvln, axis=0, keepdims=True)
            dv = _ln_backward(dvln, vhat, rstd, gv_ref[h:h + 1, :])
            dzu = du * _gelu_grad(*zu)
            dzv = dv * _gelu_grad(*zv)
            ucols = slice(h * HEAD, (h + 1) * HEAD)
            vcols = slice(da + h * HEAD, da + (h + 1) * HEAD)
            dz_ref[:, ucols] = dzu.astype(BF16)
            dz_ref[:, vcols] = dzv.astype(BF16)
            dbin_ref[:, ucols] += jnp.sum(dzu, axis=0, keepdims=True)
            dbin_ref[:, vcols] += jnp.sum(dzv, axis=0, keepdims=True)

        gate_b = z_ref[:, 2 * da:3 * da].astype(F32)
        gate_c = z_ref[:, 3 * da:4 * da].astype(F32)
        xt = z_ref[:, 4 * da:5 * da].astype(F32)
        hc = gate_c * xt
        halo = jnp.where(i > 0, pc_ref[...].astype(F32) * px_ref[...].astype(F32), 0.0)
        row = lax.broadcasted_iota(jnp.int32, (tm, da), 0)
        sh1 = _shift_rows_down(hc, halo, 1, row)
        sh2 = _shift_rows_down(hc, halo, 2, row)
        y = cw_ref[0:1, :] * sh2 + cw_ref[1:2, :] * sh1 + cw_ref[2:3, :] * hc
        dyb = dy_ref[1].astype(F32)
        dconv = dyb * gate_b
        nhalo = jnp.where(i < nblk - 1, ndy_ref[...].astype(F32) * nb_ref[...].astype(F32), 0.0)
        dhc = cw_ref[2:3, :] * dconv + cw_ref[1:2, :] * _shift_rows_up(dconv, nhalo, 1, row) + cw_ref[0:1, :] * _shift_rows_up(dconv, nhalo, 2, row)
        dcw_ref[0:1, :] += jnp.sum(dconv * sh2, axis=0, keepdims=True)
        dcw_ref[1:2, :] += jnp.sum(dconv * sh1, axis=0, keepdims=True)
        dcw_ref[2:3, :] += jnp.sum(dconv * hc, axis=0, keepdims=True)
        for col, val in ((2, dyb * y), (3, dhc * xt), (4, dhc * gate_c)):
            cols = slice(col * da, (col + 1) * da)
            dz_ref[:, cols] = val.astype(BF16)
            dbin_ref[:, cols] += jnp.sum(val, axis=0, keepdims=True)

        @pl.when(i == nblk - 1)
        def _():
            for h in range(nh):
                dbs_ref[h] = jnp.broadcast_to(jnp.sum(dbs_ref[h], axis=1, keepdims=True), (CHUNK, HEAD))

    prev = lambda col: pl.BlockSpec((BF16_ROWS, da), lambda i: (jnp.maximum(i * hb - 1, 0), col))
    nxt = lambda i: jnp.minimum((i + 1) * hb, s // BF16_ROWS - 1)
    in_specs = [pl.BlockSpec((tm, zc), lambda i: (i, 0)), prev(3), prev(4), pl.BlockSpec((BF16_ROWS, da), lambda i: (nxt(i), 2)),
                pl.BlockSpec((2, tm, da), lambda i: (0, i, 0)), pl.BlockSpec((None, BF16_ROWS, da), lambda i: (1, nxt(i), 0)),
                _row((nh, CHUNK, CHUNK)), _row((nh, CHUNK, HEAD)), _row((nh, HEAD)), _row((nh, HEAD)), _row((CONV_TAPS, da))]
    out_specs = [pl.BlockSpec((tm, zc), lambda i: (i, 0)), _row((nh, CHUNK, CHUNK)), _row((nh, CHUNK, HEAD)), _row((nh, HEAD)),
                 _row((nh, HEAD)), _row((8, da)), _row((1, zc))]
    out_shape = [jax.ShapeDtypeStruct((s, zc), BF16), jax.ShapeDtypeStruct((nh, CHUNK, CHUNK), F32),
                 jax.ShapeDtypeStruct((nh, CHUNK, HEAD), F32), jax.ShapeDtypeStruct((nh, HEAD), F32),
                 jax.ShapeDtypeStruct((nh, HEAD), F32), jax.ShapeDtypeStruct((8, da), F32), jax.ShapeDtypeStruct((1, zc), F32)]
    return _call(body, name, (nblk,), in_specs, out_specs, out_shape, [], [z, z, z, z, dy, dy, ws_masked, bs_wide, gv, bv, cw], rider)


def _adam_update(g, w, m, v):
    m_new = ADAM_B1 * m + (1.0 - ADAM_B1) * g
    v_new = ADAM_B2 * v + (1.0 - ADAM_B2) * (g * g)
    m_hat = m_new / (1.0 - ADAM_B1 ** ADAM_STEP)
    v_hat = v_new / (1.0 - ADAM_B2 ** ADAM_STEP)
    return -ADAM_LR * (m_hat / (jnp.sqrt(v_hat) + ADAM_EPS) + ADAM_WD * w), m_new, v_new


def _adamw(gparts, w, m, v, name):
    n, r, c = gparts.shape
    tr = r // 4 if (r // 4) % BF16_ROWS == 0 else r

    def body(g_ref, w_ref, m_ref, v_ref, go_ref, d_ref, mo_ref, vo_ref):
        g = g_ref[0].astype(F32)
        for q in range(1, n):
            g = g + g_ref[q].astype(F32)
        go_ref[...] = g
        d_ref[...], mo_ref[...], vo_ref[...] = _adam_update(g, w_ref[...], m_ref[...], v_ref[...])

    blk = pl.BlockSpec((tr, c), lambda i: (i, 0))
    shp = jax.ShapeDtypeStruct((r, c), F32)
    return _call(body, name, (r // tr,), [pl.BlockSpec((n, tr, c), lambda i: (0, i, 0)), blk, blk, blk], [blk] * 4, [shp] * 4, [],
                 [gparts, w, m, v])[0]


def _adamw_small(packs, rows, w, m, v, conv, loss_scale, name):
    n_par, n_dev = len(rows), packs.shape[0]
    taps = conv[0].shape[0]

    def body(*refs):
        refs = list(refs)
        cut = lambda n: [refs.pop(0) for _ in range(n)]
        p_ref, w_refs, m_refs, v_refs, (cw_ref, cm_ref, cv_ref) = refs.pop(0), cut(n_par), cut(n_par), cut(n_par), cut(3)
        outs = [cut(4) for _ in range(n_par + 1)]
        loss_ref = refs.pop(0)
        at = 0
        for k in range(n_par):
            g = p_ref[0, at:at + rows[k], :]
            for dev in range(1, n_dev):
                g = g + p_ref[dev, at:at + rows[k], :]
            go_ref, d_ref, mo_ref, vo_ref = outs[k]
            go_ref[...] = g
            d_ref[...], mo_ref[...], vo_ref[...] = _adam_update(g, w_refs[k][...], m_refs[k][...], v_refs[k][...])
            at += rows[k]
        me = 4 * lax.axis_index("x") + 2 * lax.axis_index("y") + lax.axis_index("c")
        go_ref, d_ref, mo_ref, vo_ref = outs[n_par]
        for tap in range(taps):
            row = pl.ds(at + tap * n_dev + me, 1)
            g = p_ref[0, row, :]
            for dev in range(1, n_dev):
                g = g + p_ref[dev, row, :]
            one = slice(tap, tap + 1)
            go_ref[one, :] = g
            d_ref[one, :], mo_ref[one, :], vo_ref[one, :] = _adam_update(g, cw_ref[one, :], cm_ref[one, :], cv_ref[one, :])
        at += taps * n_dev
        sq = p_ref[0, at:, :]
        for dev in range(1, n_dev):
            sq = sq + p_ref[dev, at:, :]
        total = jnp.sum(jnp.sum(sq, axis=0, keepdims=True), axis=1, keepdims=True)
        loss_ref[...] = jnp.broadcast_to(loss_scale * total, loss_ref.shape)

    vmem = pl.BlockSpec(memory_space=pltpu.VMEM)
    ins = [packs] + list(w) + list(m) + list(v) + list(conv)
    out_shape = [jax.ShapeDtypeStruct(a.shape, F32) for a in list(w) + [conv[0]] for _ in range(4)] + [jax.ShapeDtypeStruct((8, LANES), F32)]
    outs = pl.pallas_call(body, name=name, in_specs=[vmem] * len(ins), out_specs=[vmem] * len(out_shape), out_shape=out_shape,
                          compiler_params=pltpu.CompilerParams(vmem_limit_bytes=V7X_VMEM_LIMIT))(*ins)
    return [outs[4 * k:4 * k + 4] for k in range(n_par + 1)], outs[-1][0, 0]


def _rows128(a):
    return a.reshape(-1, LANES)


def kernel(x, ffa_gate, ffa_up, ffa_down, ln_a_g, ln_a_b, w_in, b_in, w_s, b_s, ln_v_g, ln_v_b, conv_w, w_out, b_out, ln_m_g, ln_m_b, ffc_gate, ffc_up, ffc_down, ln_c_g, ln_c_b, loss_target, m_ffa_gate, m_ffa_up, m_ffa_down, m_ln_a_g, m_ln_a_b, m_w_in, m_b_in, m_w_s, m_b_s, m_ln_v_g, m_ln_v_b, m_conv_w, m_w_out, m_b_out, m_ln_m_g, m_ln_m_b, m_ffc_gate, m_ffc_up, m_ffc_down, m_ln_c_g, m_ln_c_b, v_ffa_gate, v_ffa_up, v_ffa_down, v_ln_a_g, v_ln_a_b, v_w_in, v_b_in, v_w_s, v_b_s, v_ln_v_g, v_ln_v_b, v_conv_w, v_w_out, v_b_out, v_ln_m_g, v_ln_m_b, v_ffc_gate, v_ffc_up, v_ffc_down, v_ln_c_g, v_ln_c_b):
    x2, target = x[0], loss_target[0]
    s, d = x2.shape
    da = d // 2
    nh = da // HEAD

    big = dict(ffa_gate=ffa_gate, ffa_up=ffa_up, ffa_down=ffa_down, w_in=w_in, w_out=w_out, ffc_gate=ffc_gate, ffc_up=ffc_up, ffc_down=ffc_down)
    big_m = dict(ffa_gate=m_ffa_gate, ffa_up=m_ffa_up, ffa_down=m_ffa_down, w_in=m_w_in, w_out=m_w_out, ffc_gate=m_ffc_gate, ffc_up=m_ffc_up, ffc_down=m_ffc_down)
    big_v = dict(ffa_gate=v_ffa_gate, ffa_up=v_ffa_up, ffa_down=v_ffa_down, w_in=v_w_in, w_out=v_w_out, ffc_gate=v_ffc_gate, ffc_up=v_ffc_up, ffc_down=v_ffc_down)
    local = lambda k, a: jnp.transpose(a[0]) if k in TRANSPOSED else a[0]
    shard = {k: local(k, w).astype(BF16) for k, w in big.items()}
    conv_rows = jnp.pad(conv_w[0], ((0, 8 - CONV_TAPS), (0, 0)))

    tril = jnp.tril(jnp.ones((CHUNK, CHUNK), dtype=bool))
    ws_masked = jnp.where(tril[None], w_s[0], 0.0).astype(BF16)
    bs_wide = jnp.broadcast_to(b_s[0][:, :, None], (nh, CHUNK, HEAD))
    gv, bv = ln_v_g.reshape(nh, HEAD), ln_v_b.reshape(nh, HEAD)

    full = {}
    xb = x2.astype(BF16)
    (silu_a, udsilu_a, hid_a), (full["ffa_gate"], full["ffa_up"], full["ffa_down"]) = _ffn_gateup_gathering(
        xb, shard["ffa_gate"], shard["ffa_up"], shard["ffa_down"], "ffa_gateup")
    (xhat1, h1b, rstd1), (full["w_in"], full["w_out"], conv_full) = _down_ln(
        hid_a, full["ffa_down"], None, x2, None, ln_a_g, ln_a_b, 0.5, "ffa_down_ln", rider=_Gather([shard["w_in"], shard["w_out"], conv_rows], by_columns=(0,)))
    cw = jnp.transpose(conv_full[:, :CONV_TAPS, :], (1, 0, 2)).reshape(CONV_TAPS, da)
    w_out2 = full["w_out"].reshape(2, da, d)
    z, (full["ffc_gate"],) = _proj_in(h1b, full["w_in"], b_in, "proj_in", _Gather([shard["ffc_gate"]]))
    ycat = _mixer_fwd(z, ws_masked, bs_wide, gv, bv, cw, "mixer_fwd")
    (xhat2, h2b, rstd2), _ = _down_ln(ycat, w_out2, b_out, xhat1, (ln_a_g, ln_a_b), ln_m_g, ln_m_b, 1.0, "proj_out_ln")
    (silu_c, udsilu_c, hid_c), (full["ffc_up"], full["ffc_down"]) = _ffn_gateup_gathering(
        h2b, full["ffc_gate"], shard["ffc_up"], shard["ffc_down"], "ffc_gateup")
    (dr3, dr3b, sq_err, d_ln_c_g, d_ln_c_b), _ = _down_ln(hid_c, full["ffc_down"], None, xhat2, (ln_m_g, ln_m_b), ln_c_g, ln_c_b, 0.5,
                                                          "ffc_down_ln_loss", target=target)

    landed = {}
    (dg_c, du_c), _ = _ffn_bwd_hidden(dr3b, full["ffc_down"], silu_c, udsilu_c, "ffc_bwd_hidden")
    part, _ = _dw_hidden_rows_paired(hid_c, dr3b, "ffc_dw_down")
    part, (landed["ffc_down"],) = _dw_hidden_rows_paired(dg_c, h2b, "ffc_dw_gate", _ChipScatter(part))
    part, (landed["ffc_gate"],) = _dw_hidden_rows_paired(du_c, h2b, "ffc_dw_up", _ChipScatter(part))
    (dr2, dr2b, d_ln_m_g, d_ln_m_b, d_b_out), (landed["ffc_up"],) = _ffn_bwd_input(
        dg_c, full["ffc_gate"], du_c, full["ffc_up"], dr3, (xhat2, rstd2, ln_m_g), "ffc_bwd_input_ln", _ChipScatter(part))
    dycat = _nt_hidden(dr2b, w_out2, "proj_out_bwd")
    part, _ = _dw_hidden_rows(ycat, dr2b, "proj_out_dw")
    (dz, d_w_s, d_b_s_wide, d_gv, d_bv, d_cw, d_b_in), (landed["w_out"],) = _mixer_bwd(
        z, dycat, ws_masked, bs_wide, gv, bv, cw, "mixer_bwd", _Scatter(part.reshape(N_DEV, d // N_DEV, d)))
    dr1, dr1b, d_ln_a_g, d_ln_a_b = _proj_in_bwd_ln(dz, full["w_in"], dr2, (xhat1, rstd1, ln_a_g), 0.5, "proj_in_bwd_ln")
    small_g = dict(ln_a_g=d_ln_a_g, ln_a_b=d_ln_a_b, b_in=d_b_in, w_s=d_w_s, b_s=d_b_s_wide[:, :, 0], ln_v_g=d_gv, ln_v_b=d_bv, b_out=d_b_out,
                   ln_m_g=d_ln_m_g, ln_m_b=d_ln_m_b, ln_c_g=d_ln_c_g, ln_c_b=d_ln_c_b)
    pack = jnp.concatenate([_rows128(g) for g in small_g.values()] + [_rows128(d_cw[:CONV_TAPS]), _rows128(sq_err)], axis=0)
    part, (packs,) = _dw_cols(h1b, dz, "proj_in_dw", _Scatter(pack, whole=True))
    (dg_a, du_a), (landed["w_in"],) = _ffn_bwd_hidden(dr1b, full["ffa_down"], silu_a, udsilu_a, "ffa_bwd_hidden", _Scatter(part, by_columns=True))
    part, _ = _dw_hidden_rows_paired(hid_a, dr1b, "ffa_dw_down")
    part, (landed["ffa_down"],) = _dw_hidden_rows_paired(dg_a, xb, "ffa_dw_gate", _ChipScatter(part))
    part, (landed["ffa_gate"],) = _dw_hidden_rows_paired(du_a, xb, "ffa_dw_up", _ChipScatter(part))
    grad_x, (landed["ffa_up"],) = _ffn_bwd_input(dg_a, full["ffa_gate"], du_a, full["ffa_up"], dr1, None, "ffa_bwd_input", _ChipScatter(part))

    grads, deltas, new_m, new_v = {}, {}, {}, {}
    for k in big:
        out = _adamw(landed[k], local(k, big[k]), local(k, big_m[k]), local(k, big_v[k]), "adamw_" + k)
        grads[k], deltas[k], new_m[k], new_v[k] = ((jnp.transpose(o) if k in TRANSPOSED else o).reshape(big[k].shape) for o in out)

    small = dict(ln_a_g=ln_a_g, ln_a_b=ln_a_b, b_in=b_in, w_s=w_s, b_s=b_s, ln_v_g=ln_v_g, ln_v_b=ln_v_b, b_out=b_out,
                 ln_m_g=ln_m_g, ln_m_b=ln_m_b, ln_c_g=ln_c_g, ln_c_b=ln_c_b)
    small_m = dict(ln_a_g=m_ln_a_g, ln_a_b=m_ln_a_b, b_in=m_b_in, w_s=m_w_s, b_s=m_b_s, ln_v_g=m_ln_v_g, ln_v_b=m_ln_v_b, b_out=m_b_out,
                   ln_m_g=m_ln_m_g, ln_m_b=m_ln_m_b, ln_c_g=m_ln_c_g, ln_c_b=m_ln_c_b)
    small_v = dict(ln_a_g=v_ln_a_g, ln_a_b=v_ln_a_b, b_in=v_b_in, w_s=v_w_s, b_s=v_b_s, ln_v_g=v_ln_v_g, ln_v_b=v_ln_v_b, b_out=v_b_out,
                   ln_m_g=v_ln_m_g, ln_m_b=v_ln_m_b, ln_c_g=v_ln_c_g, ln_c_b=v_ln_c_b)
    snames = list(small)
    assert snames == list(small_g) and conv_w.shape[2] == LANES and da == N_DEV * LANES
    views = lambda tree: [_rows128(tree[k]) for k in snames]
    out, loss = _adamw_small(packs, [a.shape[0] for a in views(small)], views(small), views(small_m), views(small_v),
                             (conv_w[0], m_conv_w[0], v_conv_w[0]), 0.5 / d, "adamw_small")
    for k, per_param in zip(snames + ["conv_w"], out):
        shape = conv_w.shape if k == "conv_w" else small[k].shape
        grads[k], deltas[k], new_m[k], new_v[k] = (o.reshape(shape) for o in per_param)

    order = ["ffa_gate", "ffa_up", "ffa_down", "ln_a_g", "ln_a_b", "w_in", "b_in", "w_s", "b_s", "ln_v_g", "ln_v_b", "conv_w", "w_out", "b_out",
             "ln_m_g", "ln_m_b", "ffc_gate", "ffc_up", "ffc_down", "ln_c_g", "ln_c_b"]
    return (loss, grad_x[None], *[grads[k] for k in order], *[deltas[k] for k in order], *[new_m[k] for k in order], *[new_v[k] for k in order])
```

```python
import math

import jax
import jax.numpy as jnp
from jax import lax
from jax.experimental import pallas as pl
from jax.experimental.pallas import tpu as pltpu

BF16 = jnp.bfloat16
F32 = jnp.float32
MESH = pl.DeviceIdType.MESH

N_DEV = 8
HEAD = 128
CHUNK = 128
CONV_TAPS = 3
LN_EPS = 1e-5
ALPHA = float(2 ** 0.25)
GELU_C = 0.7978845608028654
GELU_A = 0.044715
ADAM_LR, ADAM_B1, ADAM_B2, ADAM_EPS, ADAM_WD, ADAM_STEP = 0.001, 0.9, 0.999, 1e-08, 0.01, 10
V7X_VMEM_LIMIT = 56 * 1024 * 1024
LANES = 128
BF16_ROWS = 16
MXU_COLS = 256
TRANSPOSED = ("ffa_gate", "ffa_up", "ffc_gate", "ffc_up")
PROJ_IN_COLS = 1280
MIXER_TOKENS = 512
ADAM_SIDE_ROWS = 64
DW_TOKENS = 2048

NT_DIMS = (((1,), (1,)), ((), ()))
TN_DIMS = (((0,), (0,)), ((), ()))
ANY = pl.BlockSpec(memory_space=pl.ANY)


def _gelu_tanh(x):
    return jnp.tanh(GELU_C * (x + GELU_A * x * x * x))


def _gelu(x, t):
    return 0.5 * x * (1.0 + t)


def _gelu_grad(x, t):
    return 0.5 * (1.0 + t) + 0.5 * x * (1.0 - t * t) * GELU_C * (1.0 + 3.0 * GELU_A * x * x)


def _sigmoid(x):
    return 0.5 * jnp.tanh(0.5 * x) + 0.5


def _row(shape):
    return pl.BlockSpec(shape, lambda *_: (0,) * len(shape))


def _row_blocks(tm, rows=128):
    rows = min(rows, tm)
    return [slice(r, r + rows) for r in range(0, tm, rows)]


def _ln_backward(dh, xhat, rstd, gain):
    dxh = dh * gain
    m1 = jnp.mean(dxh, axis=-1, keepdims=True)
    m2 = jnp.mean(dxh * xhat, axis=-1, keepdims=True)
    return rstd * (dxh - m1 - xhat * m2)


def _place():
    x, y, c = lax.axis_index("x"), lax.axis_index("y"), lax.axis_index("c")
    return x, y, c, [(1 - x, y), (x, 1 - y), (1 - x, 1 - y)]


def _other_devices(x, y, c):
    flips = [(bx, by, bc) for bx in (0, 1) for by in (0, 1) for bc in (0, 1)][1:]
    return [(1 - x if bx else x, 1 - y if by else y, 1 - c if bc else c) for bx, by, bc in flips]


class _Gather:
    def __init__(self, shards, forward_at=0.75, by_columns=()):
        n = len(shards)
        self.n, self.forward_at, self.by_columns = n, forward_at, tuple(by_columns)
        self.inputs = list(shards)
        self.out_shapes = [jax.ShapeDtypeStruct((a.shape[0], N_DEV * a.shape[1]) if i in self.by_columns else (N_DEV,) + a.shape, a.dtype)
                           for i, a in enumerate(shards)]
        self.scratch = [pltpu.SemaphoreType.DMA((n, 7)), pltpu.SemaphoreType.DMA((n, 7)), pltpu.SemaphoreType.DMA((n,))]

    def _block(self, outs, a, dev):
        if a in self.by_columns:
            cols = outs[a].shape[1] // N_DEV
            return outs[a].at[:, pl.ds(dev * cols, cols)]
        return outs[a].at[dev]

    def _copy(self, outs, sems, a, k, block, to, src=None):
        dst = self._block(outs, a, block)
        return pltpu.make_async_remote_copy(src_ref=dst if src is None else src, dst_ref=dst, send_sem=sems[0].at[a, k],
                                            recv_sem=sems[1].at[a, k], device_id=to, device_id_type=MESH)

    def start(self, ins, outs, sems, urgent=None):
        x, y, c, chips = _place()
        me = 4 * x + 2 * y + c
        for a in range(self.n):
            pltpu.make_async_copy(ins[a], self._block(outs, a, me), sems[2].at[a]).start()
        urgent = list(range(self.n)) if urgent is None else list(urgent)
        for group in (urgent, [a for a in range(self.n) if a not in urgent]):
            for a in group:
                self._copy(outs, sems, a, 0, me, (x, y, 1 - c), src=ins[a]).start()
                for j in (0, 1):
                    self._copy(outs, sems, a, 1 + j, me, (*chips[j], c), src=ins[a]).start()
            for a in group:
                self._copy(outs, sems, a, 3, me, (*chips[2], c), src=ins[a]).start()

    def wait_sibling(self, outs, sems, a):
        x, y, c, _ = _place()
        self._copy(outs, sems, a, 0, 4 * x + 2 * y + 1 - c, (x, y, 1 - c)).wait_recv()

    def pass_on(self, outs, sems, a, j):
        x, y, c, chips = _place()
        block = 4 * chips[j][0] + 2 * chips[j][1] + c
        self._copy(outs, sems, a, 1 + j, block, (x, y, 1 - c)).wait_recv()
        self._copy(outs, sems, a, 4 + j, block, (x, y, 1 - c)).start()

    def wait_passed(self, outs, sems, a, j):
        x, y, c, chips = _place()
        self._copy(outs, sems, a, 4 + j, 4 * chips[j][0] + 2 * chips[j][1] + 1 - c, (x, y, 1 - c)).wait_recv()

    def wait_sent(self, ins, outs, sems, a):
        x, y, c, _ = _place()
        me = 4 * x + 2 * y + c
        for k in range(7):
            self._copy(outs, sems, a, k, me, (x, y, 1 - c), src=ins[a]).wait_send()
        pltpu.make_async_copy(ins[a], self._block(outs, a, me), sems[2].at[a]).wait()

    def forward(self, ins, outs, sems):
        for a in range(self.n):
            for j in range(3):
                self.pass_on(outs, sems, a, j)

    def finish(self, ins, outs, sems):
        for a in range(self.n):
            self.wait_sibling(outs, sems, a)
            for j in range(3):
                self.wait_passed(outs, sems, a, j)
        for a in range(self.n):
            self.wait_sent(ins, outs, sems, a)

    def before(self, step, n_steps, ins, outs, sems):
        pl.when(step == 0)(lambda: self.start(ins, outs, sems))
        pl.when(step == int(self.forward_at * (n_steps - 1)))(lambda: self.forward(ins, outs, sems))

    def after(self, step, n_steps, ins, outs, sems):
        pl.when(step == n_steps - 1)(lambda: self.finish(ins, outs, sems))


class _Scatter:
    def __init__(self, partial, whole=False, by_columns=False):
        self.whole, self.by_columns = whole, by_columns
        self.inputs = [partial]
        if whole:
            shape = (N_DEV,) + partial.shape
        elif by_columns:
            shape = (N_DEV, partial.shape[0], partial.shape[1] // N_DEV)
        else:
            shape = partial.shape
        self.out_shapes = [jax.ShapeDtypeStruct(shape, partial.dtype)]
        self.scratch = [pltpu.SemaphoreType.DMA((7,)), pltpu.SemaphoreType.DMA((7,)), pltpu.SemaphoreType.DMA((1,))]

    def _copies(self, ins, outs, sems):
        x, y, c, _ = _place()
        me = 4 * x + 2 * y + c
        if self.whole:
            block = lambda dev: ins[0]
        elif self.by_columns:
            cols = ins[0].shape[1] // N_DEV
            block = lambda dev: ins[0].at[:, pl.ds(dev * cols, cols)]
        else:
            block = lambda dev: ins[0].at[dev]
        mine = pltpu.make_async_copy(block(me), outs[0].at[me], sems[2].at[0])
        remote = [pltpu.make_async_remote_copy(src_ref=block(4 * px + 2 * py + pc), dst_ref=outs[0].at[me], send_sem=sems[0].at[k],
                                               recv_sem=sems[1].at[k], device_id=(px, py, pc), device_id_type=MESH)
                  for k, (px, py, pc) in enumerate(_other_devices(x, y, c))]
        return mine, remote

    def start(self, ins, outs, sems):
        mine, remote = self._copies(ins, outs, sems)
        mine.start()
        for cp in remote:
            cp.start()

    def finish(self, ins, outs, sems):
        mine, remote = self._copies(ins, outs, sems)
        for cp in remote:
            cp.wait()
        mine.wait()

    def before(self, step, n_steps, ins, outs, sems):
        pl.when(step == 0)(lambda: self.start(ins, outs, sems))

    def after(self, step, n_steps, ins, outs, sems):
        pl.when(step == n_steps - 1)(lambda: self.finish(ins, outs, sems))


class _ChipScatter(_Scatter):
    def __init__(self, sums):
        super().__init__(sums)
        self.scratch = [pltpu.SemaphoreType.DMA((3,)), pltpu.SemaphoreType.DMA((3,)), pltpu.SemaphoreType.DMA((1,))]

    def _copies(self, ins, outs, sems):
        x, y, c, chips = _place()
        my_chip = 2 * x + y
        mine = pltpu.make_async_copy(ins[0].at[my_chip], outs[0].at[my_chip], sems[2].at[0])
        remote = [pltpu.make_async_remote_copy(src_ref=ins[0].at[2 * px + py], dst_ref=outs[0].at[my_chip], send_sem=sems[0].at[k],
                                               recv_sem=sems[1].at[k], device_id=(px, py, c), device_id_type=MESH)
                  for k, (px, py) in enumerate(chips)]
        return mine, remote


class _AdamSide:
    def __init__(self, gparts, w, m, v):
        self.n, self.r, self.c = gparts.shape
        self.inputs = [gparts, w, m, v]
        self.out_shapes = [jax.ShapeDtypeStruct((self.r, self.c), F32)] * 4
        self.scratch = []

    def _plan(self, n_steps):
        rows = ADAM_SIDE_ROWS
        while self.r // rows > n_steps:
            rows *= 2
        assert self.r % rows == 0
        return rows, self.r // rows

    def _rows(self, grid):
        rows, n_blocks = self._plan(math.prod(grid))

        def block(*ids):
            step = 0
            for size, pid in zip(grid, ids):
                step = step * size + pid
            return jnp.minimum(step, n_blocks - 1)
        return pl.BlockSpec((rows, self.c), lambda *ids: (block(*ids), 0)), block, rows

    def in_specs(self, grid):
        spec, block, rows = self._rows(grid)
        return [pl.BlockSpec((self.n, rows, self.c), lambda *ids: (0, block(*ids), 0)), spec, spec, spec]

    def out_specs(self, grid):
        return [self._rows(grid)[0]] * 4

    def before(self, step, n_steps, ins, outs, sems):
        @pl.when(step < self._plan(n_steps)[1])
        def _():
            g_ref, w_ref, m_ref, v_ref = ins
            g = g_ref[0].astype(F32)
            for q in range(1, self.n):
                g = g + g_ref[q].astype(F32)
            outs[0][...] = g
            outs[1][...], outs[2][...], outs[3][...] = _adam_update(g, w_ref[...], m_ref[...], v_ref[...])

    def after(self, step, n_steps, ins, outs, sems):
        pass


def _call(body, name, grid, in_specs, out_specs, out_shape, scratch, ins, rider=None):
    single = not isinstance(out_shape, (list, tuple))
    out_shape = [out_shape] if single else list(out_shape)
    out_specs = [out_specs] if single else list(out_specs)
    params = pltpu.CompilerParams(dimension_semantics=("arbitrary",) * len(grid), vmem_limit_bytes=V7X_VMEM_LIMIT)
    if rider is None:
        outs = pl.pallas_call(body, name=name, grid=grid, in_specs=in_specs, out_specs=out_specs, out_shape=out_shape,
                              scratch_shapes=scratch, compiler_params=params)(*ins)
        return (outs[0] if single else outs), None
    n_in, n_out, n_scr = len(ins), len(out_shape), len(scratch)
    r_in, r_out = len(rider.inputs), len(rider.out_shapes)
    n_steps = math.prod(grid)
    specs = lambda kind, count: getattr(rider, kind)(grid) if hasattr(rider, kind) else [ANY] * count

    def carried(*refs):
        refs = list(refs)
        cut = lambda n: [refs.pop(0) for _ in range(n)]
        b_in, c_in, b_out, c_out, b_scr = cut(n_in), cut(r_in), cut(n_out), cut(r_out), cut(n_scr)
        step = 0
        for axis, size in enumerate(grid):
            step = step * size + pl.program_id(axis)
        rider.before(step, n_steps, c_in, c_out, refs)
        body(*b_in, *b_out, *b_scr)
        rider.after(step, n_steps, c_in, c_out, refs)

    outs = pl.pallas_call(
        carried, name=name, grid=grid, in_specs=list(in_specs) + specs("in_specs", r_in), out_specs=out_specs + specs("out_specs", r_out),
        out_shape=out_shape + rider.out_shapes, scratch_shapes=list(scratch) + rider.scratch, compiler_params=params,
    )(*ins, *rider.inputs)
    base = outs[:n_out]
    return (base[0] if single else base), outs[n_out:]


def _arrival_block(j):
    x, y, c = lax.axis_index("x"), lax.axis_index("y"), lax.axis_index("c")
    chip, other_core = j // 2, j % 2
    px = jnp.where((chip == 1) | (chip == 3), 1 - x, x)
    py = jnp.where((chip == 2) | (chip == 3), 1 - y, y)
    pc = jnp.where(other_core == 1, 1 - c, c)
    return 4 * px + 2 * py + pc


def _ffn_gateup_gathering(xb, gate, up_shard, down_shard, name):
    s, d = xb.shape
    fs = up_shard.shape[0]
    tm = min(s, 1024)
    ni = s // tm
    ask_at = max(ni - 2, 0)
    gate_here = gate.ndim == 2
    gather = _Gather(([gate] if gate_here else []) + [up_shard, down_shard])
    n_g = gather.n
    used_here, down = tuple(range(n_g - 1)), n_g - 1

    def body(x_ref, *refs):
        refs = list(refs)
        gate_full = None if gate_here else refs.pop(0)
        shards = [refs.pop(0) for _ in range(n_g)]
        silu_ref, udsilu_ref, h_ref = refs.pop(0), refs.pop(0), refs.pop(0)
        fulls = [refs.pop(0) for _ in range(n_g)]
        w_ref, w_sems = refs.pop(0), refs.pop(0)
        sems = refs
        j, i = pl.program_id(0), pl.program_id(1)
        gate_src, up_src = (fulls[0], fulls[1]) if gate_here else (gate_full, fulls[0])

        def load(slot, srcs):
            return [pltpu.make_async_copy(src, w_ref.at[slot, a], w_sems.at[slot, a]) for a, src in enumerate(srcs)]

        @pl.when((j == 0) & (i == 0))
        def _():
            gather.start(shards, fulls, sems, urgent=used_here)
            mine = load(0, (shards[0] if gate_here else gate_full.at[_arrival_block(0)], shards[n_g - 2]))
            for cp in mine:
                cp.start()
            for cp in mine:
                cp.wait()

        for nxt in range(1, N_DEV):
            @pl.when((j == nxt) & (i == 0))
            def _(nxt=nxt):
                for cp in load(nxt % 2, (gate_src.at[0], up_src.at[0])):
                    cp.wait()

        for nxt in range(1, N_DEV):
            @pl.when((j == nxt - 1) & (i == ask_at))
            def _(nxt=nxt):
                for a in used_here:
                    if nxt == 1:
                        gather.wait_sibling(fulls, sems, a)
                    elif nxt % 2 == 0:
                        gather.pass_on(fulls, sems, a, nxt // 2 - 1)
                    else:
                        gather.wait_passed(fulls, sems, a, nxt // 2 - 1)
                block = _arrival_block(nxt)
                for cp in load(nxt % 2, (gate_src.at[block], up_src.at[block])):
                    cp.start()

        @pl.when((j == N_DEV - 1) & (i == ask_at))
        def _():
            for other_chip in range(3):
                gather.pass_on(fulls, sems, down, other_chip)

        x = x_ref[...]
        g = lax.dot_general(x, w_ref[j % 2, 0], NT_DIMS, preferred_element_type=F32)
        u = lax.dot_general(x, w_ref[j % 2, 1], NT_DIMS, preferred_element_type=F32)
        sg = _sigmoid(g)
        silu = g * sg
        silu_ref[...] = silu.astype(BF16)
        udsilu_ref[...] = (u * (sg + silu * (1.0 - sg))).astype(BF16)
        h_ref[...] = (silu * u).astype(BF16)

        @pl.when((j == N_DEV - 1) & (i == ni - 1))
        def _():
            gather.wait_sibling(fulls, sems, down)
            for other_chip in range(3):
                gather.wait_passed(fulls, sems, down, other_chip)
            for a in range(n_g):
                gather.wait_sent(shards, fulls, sems, a)

    shp = jax.ShapeDtypeStruct((N_DEV, s, fs), BF16)
    o_spec = pl.BlockSpec((None, tm, fs), lambda j, i: (_arrival_block(j), i, 0))
    ins = ([] if gate_here else [gate]) + gather.inputs
    outs = pl.pallas_call(
        body, name=name, grid=(N_DEV, ni), in_specs=[pl.BlockSpec((tm, d), lambda j, i: (i, 0))] + [ANY] * len(ins),
        out_specs=[o_spec, o_spec, o_spec] + [ANY] * n_g, out_shape=[shp, shp, shp] + gather.out_shapes,
        scratch_shapes=[pltpu.VMEM((2, 2, fs, d), BF16), pltpu.SemaphoreType.DMA((2, 2))] + gather.scratch,
        compiler_params=pltpu.CompilerParams(dimension_semantics=("arbitrary", "arbitrary"), vmem_limit_bytes=V7X_VMEM_LIMIT),
    )(xb, *ins)
    return outs[:3], outs[3:]


def _down_ln(a3, w3, bias, res, res_affine, ln_g, ln_b, scale, name, target=None, rider=None):
    nk, s, tk = a3.shape
    d = w3.shape[2]
    tm = min(s, 256)
    final = target is not None

    def body(*refs):
        refs = list(refs)
        a_ref, w_hbm = refs[:2]
        del refs[:2]
        bias_ref = refs.pop(0) if bias is not None else None
        res_ref = refs.pop(0)
        rg_ref, rb_ref = (refs.pop(0), refs.pop(0)) if res_affine is not None else (None, None)
        g_ref, b_ref = refs.pop(0), refs.pop(0)
        t_ref = refs.pop(0) if final else None
        w_sem = refs.pop()
        w_ref = refs.pop()
        i = pl.program_id(0)
        if final:
            dr_ref, drb_ref, sq_ref, dg_ref, db_ref = refs
        else:
            xh_ref, hb_ref, rstd_ref = refs

        @pl.when(i == 0)
        def _():
            whole = pltpu.make_async_copy(w_hbm, w_ref, w_sem.at[0])
            whole.start()
            whole.wait()
            if final:
                sq_ref[...] = jnp.zeros_like(sq_ref)
                dg_ref[...] = jnp.zeros_like(dg_ref)
                db_ref[...] = jnp.zeros_like(db_ref)

        y = jnp.dot(a_ref[0], w_ref[0], preferred_element_type=F32)
        for k in range(1, nk):
            y = y + jnp.dot(a_ref[k], w_ref[k], preferred_element_type=F32)
        if bias_ref is not None:
            y = y + bias_ref[...]
        for rows in _row_blocks(tm):
            r = res_ref[rows, :]
            if rg_ref is not None:
                r = r * rg_ref[...] + rb_ref[...]
            r = ALPHA * r + scale * y[rows]
            mu = jnp.mean(r, axis=-1, keepdims=True)
            c = r - mu
            var = jnp.mean(c * c, axis=-1, keepdims=True)
            rstd = lax.rsqrt(var + LN_EPS)
            xhat = c * rstd
            h = xhat * g_ref[...] + b_ref[...]
            if not final:
                xh_ref[rows, :] = xhat
                hb_ref[rows, :] = h.astype(BF16)
                rstd_ref[rows, :] = rstd
            else:
                err = h - t_ref[rows, :]
                sq_ref[...] += jnp.sum(err * err, axis=0, keepdims=True)
                dh = err * (1.0 / d)
                dg_ref[...] += jnp.sum(dh * xhat, axis=0, keepdims=True)
                db_ref[...] += jnp.sum(dh, axis=0, keepdims=True)
                dr = _ln_backward(dh, xhat, rstd, g_ref[...])
                dr_ref[rows, :] = dr
                drb_ref[rows, :] = (scale * dr).astype(BF16)

    tok = pl.BlockSpec((tm, d), lambda i: (i, 0))
    vec = pl.BlockSpec((1, d), lambda i: (0, 0))
    ins = [a3, w3]
    in_specs = [pl.BlockSpec((nk, tm, tk), lambda i: (0, i, 0)), ANY]
    if bias is not None:
        ins.append(bias)
        in_specs.append(vec)
    ins.append(res)
    in_specs.append(tok)
    if res_affine is not None:
        ins += list(res_affine)
        in_specs += [vec, vec]
    ins += [ln_g, ln_b]
    in_specs += [vec, vec]
    if final:
        ins.append(target)
        in_specs.append(tok)
        out_shape = [jax.ShapeDtypeStruct((s, d), F32), jax.ShapeDtypeStruct((s, d), BF16)] + [jax.ShapeDtypeStruct((1, d), F32)] * 3
        out_specs = [tok, tok, vec, vec, vec]
    else:
        out_shape = [jax.ShapeDtypeStruct((s, d), F32), jax.ShapeDtypeStruct((s, d), BF16), jax.ShapeDtypeStruct((s, 1), F32)]
        out_specs = [tok, tok, pl.BlockSpec((tm, 1), lambda i: (i, 0))]
    scratch = [pltpu.VMEM((nk, tk, d), BF16), pltpu.SemaphoreType.DMA((1,))]
    return _call(body, name, (s // tm,), in_specs, out_specs, out_shape, scratch, ins, rider)


def _proj_in(hb, w, bias, name, rider=None):
    s, d = hb.shape
    n = w.shape[1]
    tm = min(s, 1024)
    tn = PROJ_IN_COLS if n % PROJ_IN_COLS == 0 else n

    def body(h_ref, w_ref, b_ref, z_ref):
        z_ref[...] = (jnp.dot(h_ref[...], w_ref[...], preferred_element_type=F32) + b_ref[...]).astype(BF16)

    in_specs = [pl.BlockSpec((tm, d), lambda i, j: (i, 0)), pl.BlockSpec((d, tn), lambda i, j: (0, j)),
                pl.BlockSpec((1, tn), lambda i, j: (0, j))]
    return _call(body, name, (s // tm, n // tn), in_specs, pl.BlockSpec((tm, tn), lambda i, j: (i, j)),
                 jax.ShapeDtypeStruct((s, n), BF16), [], [hb, w, bias], rider)


def _nt_hidden(ab, w3, name, rider=None):
    s, kdim = ab.shape
    nj, tn, _ = w3.shape
    tm = min(s, 1024)

    def body(a_ref, w_ref, o_ref):
        o_ref[...] = lax.dot_general(a_ref[...], w_ref[...], NT_DIMS, preferred_element_type=F32).astype(BF16)

    in_specs = [pl.BlockSpec((tm, kdim), lambda i, j: (i, 0)), pl.BlockSpec((None, tn, kdim), lambda i, j: (j, 0, 0))]
    return _call(body, name, (s // tm, nj), in_specs, pl.BlockSpec((None, tm, tn), lambda i, j: (j, i, 0)),
                 jax.ShapeDtypeStruct((nj, s, tn), BF16), [], [ab, w3], rider)


def _ffn_bwd_hidden(ab, w3, silu3, udsilu3, name, rider=None):
    s, kdim = ab.shape
    nj, tn, _ = w3.shape
    tm = min(s, 1024)

    def body(a_ref, w_ref, silu_ref, udsilu_ref, dg_ref, du_ref):
        a = a_ref[...]
        for c0 in range(0, tn, MXU_COLS):
            cols = slice(c0, min(c0 + MXU_COLS, tn))
            t = lax.dot_general(a, w_ref[cols, :], NT_DIMS, preferred_element_type=F32)
            du_ref[:, cols] = (t * silu_ref[:, cols].astype(F32)).astype(BF16)
            dg_ref[:, cols] = (t * udsilu_ref[:, cols].astype(F32)).astype(BF16)

    hid = pl.BlockSpec((None, tm, tn), lambda i, j: (j, i, 0))
    shp = jax.ShapeDtypeStruct((nj, s, tn), BF16)
    in_specs = [pl.BlockSpec((tm, kdim), lambda i, j: (i, 0)), pl.BlockSpec((None, tn, kdim), lambda i, j: (j, 0, 0)), hid, hid]
    return _call(body, name, (s // tm, nj), in_specs, [hid, hid], [shp, shp], [], [ab, w3, silu3, udsilu3], rider)


def _tn_dw(a, a_spec, b, b_spec, nj, m, n, s, tk, name, rider):
    def body(a_ref, b_ref, o_ref, acc_ref):
        k = pl.program_id(1)

        @pl.when(k == 0)
        def _():
            acc_ref[...] = jnp.zeros_like(acc_ref)

        acc_ref[...] += lax.dot_general(a_ref[...], b_ref[...], TN_DIMS, preferred_element_type=F32)

        @pl.when(k == s // tk - 1)
        def _():
            o_ref[...] = acc_ref[...].astype(BF16)

    return _call(body, name, (nj, s // tk), [a_spec, b_spec], pl.BlockSpec((None, m, n), lambda j, k: (j, 0, 0)),
                 jax.ShapeDtypeStruct((nj, m, n), BF16), [pltpu.VMEM((m, n), F32)], [a, b], rider)


def _dw_hidden_rows(hid3, db, name, rider=None):
    nj, s, fs = hid3.shape
    d = db.shape[1]
    tk = min(s, DW_TOKENS)
    return _tn_dw(hid3, pl.BlockSpec((None, tk, fs), lambda j, k: (j, k, 0)), db, pl.BlockSpec((tk, d), lambda j, k: (k, 0)),
                  nj, fs, d, s, tk, name, rider)


def _dw_hidden_rows_paired(hid3, db, name, rider=None):
    nj, s, fs = hid3.shape
    d = db.shape[1]
    tk = min(s, DW_TOKENS)
    nk = s // tk
    half = nj // 2

    def device_of(j):
        c = lax.axis_index("c")
        return 2 * (j % half) + jnp.where(j < half, 1 - c, c)

    def body(a_ref, b_ref, o_ref, theirs_ref, acc_ref, stage_ref, got_ref, send_sems, recv_sems, load_sem):
        j, k = pl.program_id(0), pl.program_id(1)
        x, y, c = lax.axis_index("x"), lax.axis_index("y"), lax.axis_index("c")

        def to_sibling(q):
            return pltpu.make_async_remote_copy(src_ref=stage_ref, dst_ref=theirs_ref.at[q], send_sem=send_sems.at[q],
                                                recv_sem=recv_sems.at[q], device_id=(x, y, 1 - c), device_id_type=MESH)

        def fetch(q):
            return pltpu.make_async_copy(theirs_ref.at[q], got_ref, load_sem.at[0])

        @pl.when(k == 0)
        def _():
            acc_ref[...] = jnp.zeros_like(acc_ref)

        acc_ref[...] += lax.dot_general(a_ref[...], b_ref[...], TN_DIMS, preferred_element_type=F32)

        for q in range(half):
            @pl.when((j == q) & (k == nk - 1))
            def _(q=q):
                if q > 0:
                    to_sibling(q - 1).wait_send()
                stage_ref[...] = acc_ref[...].astype(BF16)
                to_sibling(q).start()

            @pl.when((j == half + q) & (k == 0))
            def _(q=q):
                to_sibling(q).wait_recv()
                fetch(q).start()

            @pl.when((j == half + q) & (k == nk - 1))
            def _(q=q):
                if q == 0:
                    to_sibling(half - 1).wait_send()
                fetch(q).wait()
                o_ref[...] = (acc_ref[...] + got_ref[...].astype(F32)).astype(BF16)

    in_specs = [pl.BlockSpec((None, tk, fs), lambda j, k: (device_of(j), k, 0)), pl.BlockSpec((tk, d), lambda j, k: (k, 0))]
    out_specs = [pl.BlockSpec((None, fs, d), lambda j, k: (jnp.maximum(j - half, 0), 0, 0)), ANY]
    shp = jax.ShapeDtypeStruct((half, fs, d), BF16)
    scratch = [pltpu.VMEM((fs, d), F32), pltpu.VMEM((fs, d), BF16), pltpu.VMEM((fs, d), BF16),
               pltpu.SemaphoreType.DMA((half,)), pltpu.SemaphoreType.DMA((half,)), pltpu.SemaphoreType.DMA((1,))]
    (sums, _), riders_out = _call(body, name, (nj, nk), in_specs, out_specs, [shp, shp], scratch, [hid3, db], rider)
    return sums, riders_out


def _dw_cols(ab, dz, name, rider=None):
    s, d = ab.shape
    n = dz.shape[1]
    tn = PROJ_IN_COLS if n % PROJ_IN_COLS == 0 else n
    tk = min(s, DW_TOKENS)
    tr = d // 2

    def body(a_ref, b_ref, o_ref, acc_ref):
        k = pl.program_id(2)

        @pl.when(k == 0)
        def _():
            acc_ref[...] = jnp.zeros_like(acc_ref)

        acc_ref[...] += lax.dot_general(a_ref[...], b_ref[...], TN_DIMS, preferred_element_type=F32)

        @pl.when(k == s // tk - 1)
        def _():
            o_ref[...] = acc_ref[...].astype(BF16)

    in_specs = [pl.BlockSpec((tk, tr), lambda j, r, k: (k, r)), pl.BlockSpec((tk, tn), lambda j, r, k: (k, j))]
    return _call(body, name, (n // tn, d // tr, s // tk), in_specs, pl.BlockSpec((tr, tn), lambda j, r, k: (r, j)),
                 jax.ShapeDtypeStruct((d, n), BF16), [pltpu.VMEM((tr, tn), F32)], [ab, dz], rider)


def _ffn_bwd_input(dg3, wg3, du3, wu3, dres, ln, name, rider=None):
    s, d = dres.shape
    nk, _, fs = dg3.shape
    tm = min(s, 512)

    def body(*refs):
        refs = list(refs)
        dg_in, wg_ref, du_in, wu_ref, dres_ref = refs[:5]
        del refs[:5]
        if ln is not None:
            xh_ref, rstd_ref, gain_ref = refs.pop(0), refs.pop(0), refs.pop(0)
        acc_ref = refs.pop()
        i, k = pl.program_id(0), pl.program_id(1)

        @pl.when(k == 0)
        def _():
            acc_ref[...] = jnp.zeros_like(acc_ref)

        acc_ref[...] += (jnp.dot(dg_in[...], wg_ref[...], preferred_element_type=F32)
                         + jnp.dot(du_in[...], wu_ref[...], preferred_element_type=F32))

        @pl.when(k == nk - 1)
        def _():
            if ln is not None:
                dr_ref, drb_ref, dg_ref, db_ref, sum_ref = refs

                @pl.when(i == 0)
                def _():
                    dg_ref[...] = jnp.zeros_like(dg_ref)
                    db_ref[...] = jnp.zeros_like(db_ref)
                    sum_ref[...] = jnp.zeros_like(sum_ref)

            for rows in _row_blocks(tm):
                dh = ALPHA * dres_ref[rows, :] + acc_ref[rows, :]
                if ln is None:
                    refs[0][rows, :] = dh
                else:
                    xhat = xh_ref[rows, :]
                    dg_ref[...] += jnp.sum(dh * xhat, axis=0, keepdims=True)
                    db_ref[...] += jnp.sum(dh, axis=0, keepdims=True)
                    dr = _ln_backward(dh, xhat, rstd_ref[rows, :], gain_ref[...])
                    sum_ref[...] += jnp.sum(dr, axis=0, keepdims=True)
                    dr_ref[rows, :] = dr
                    drb_ref[rows, :] = dr.astype(BF16)

    tok = pl.BlockSpec((tm, d), lambda i, k: (i, 0))
    vec = pl.BlockSpec((1, d), lambda i, k: (0, 0))
    a_spec = pl.BlockSpec((None, tm, fs), lambda i, k: (k, i, 0))
    w_spec = pl.BlockSpec((None, fs, d), lambda i, k: (k, 0, 0))
    ins, in_specs = [dg3, wg3, du3, wu3, dres], [a_spec, w_spec, a_spec, w_spec, tok]
    if ln is None:
        out_shape, out_specs = jax.ShapeDtypeStruct((s, d), F32), tok
    else:
        ins += list(ln)
        in_specs += [tok, pl.BlockSpec((tm, 1), lambda i, k: (i, 0)), vec]
        out_shape = [jax.ShapeDtypeStruct((s, d), F32), jax.ShapeDtypeStruct((s, d), BF16)] + [jax.ShapeDtypeStruct((1, d), F32)] * 3
        out_specs = [tok, tok, vec, vec, vec]
    return _call(body, name, (s // tm, nk), in_specs, out_specs, out_shape, [pltpu.VMEM((tm, d), F32)], ins, rider)


def _proj_in_bwd_ln(dz, w, dres, ln, branch_scale, name, rider=None):
    s, d = dres.shape
    n = w.shape[1]
    tm = min(s, 256)

    def body(dz_ref, w_hbm, dres_ref, xh_ref, rstd_ref, gain_ref, dr_ref, drb_ref, dg_ref, db_ref, w_ref, w_sem):
        @pl.when(pl.program_id(0) == 0)
        def _():
            whole = pltpu.make_async_copy(w_hbm, w_ref, w_sem.at[0])
            whole.start()
            whole.wait()
            dg_ref[...] = jnp.zeros_like(dg_ref)
            db_ref[...] = jnp.zeros_like(db_ref)

        acc = lax.dot_general(dz_ref[...], w_ref[...], NT_DIMS, preferred_element_type=F32)
        for rows in _row_blocks(tm):
            dh = ALPHA * dres_ref[rows, :] + acc[rows]
            xhat = xh_ref[rows, :]
            dg_ref[...] += jnp.sum(dh * xhat, axis=0, keepdims=True)
            db_ref[...] += jnp.sum(dh, axis=0, keepdims=True)
            dr = _ln_backward(dh, xhat, rstd_ref[rows, :], gain_ref[...])
            dr_ref[rows, :] = dr
            drb_ref[rows, :] = (branch_scale * dr).astype(BF16)

    tok = pl.BlockSpec((tm, d), lambda i: (i, 0))
    vec = pl.BlockSpec((1, d), lambda i: (0, 0))
    in_specs = [pl.BlockSpec((tm, n), lambda i: (i, 0)), ANY, tok, tok, pl.BlockSpec((tm, 1), lambda i: (i, 0)), vec]
    out_shape = [jax.ShapeDtypeStruct((s, d), F32), jax.ShapeDtypeStruct((s, d), BF16)] + [jax.ShapeDtypeStruct((1, d), F32)] * 2
    scratch = [pltpu.VMEM(w.shape, BF16), pltpu.SemaphoreType.DMA((1,))]
    return _call(body, name, (s // tm,), in_specs, [tok, tok, vec, vec], out_shape, scratch, [dz, w, dres] + list(ln), rider)


def _shift_rows_down(v, halo, k, row):
    out = pltpu.roll(v, k, 0)
    hr = halo.shape[0]
    for r in range(k):
        out = jnp.where(row == r, halo[hr - k + r:hr - k + r + 1, :], out)
    return out


def _shift_rows_up(v, halo, k, row):
    t = v.shape[0]
    out = pltpu.roll(v, t - k, 0)
    for r in range(k):
        out = jnp.where(row == t - k + r, halo[r:r + 1, :], out)
    return out


def _sgu_head_forward(z_ref, h, da, gv_ref, bv_ref):
    zu = z_ref[:, h * HEAD:(h + 1) * HEAD].astype(F32)
    zv = z_ref[:, da + h * HEAD:da + (h + 1) * HEAD].astype(F32)
    tu, tv = _gelu_tanh(zu), _gelu_tanh(zv)
    u = _gelu(zu, tu)
    v = _gelu(zv, tv)
    mu = jnp.mean(v, axis=-1, keepdims=True)
    c = v - mu
    rstd = lax.rsqrt(jnp.mean(c * c, axis=-1, keepdims=True) + LN_EPS)
    vhat = c * rstd
    vln = (vhat * gv_ref[h:h + 1, :] + bv_ref[h:h + 1, :]).astype(BF16)
    return (zu, tu), (zv, tv), u, vhat, rstd, vln


def _mixer_fwd(z, ws_masked, bs_wide, gv, bv, cw, name):
    s, zc = z.shape
    da = zc // 5
    nh = da // HEAD
    tm = min(s, MIXER_TOKENS)
    hb = tm // BF16_ROWS

    def body(z_ref, pc_ref, px_ref, ws_ref, bs_ref, gv_ref, bv_ref, cw_ref, y_ref):
        i = pl.program_id(0)
        for h in range(nh):
            _, _, u, _, _, vln = _sgu_head_forward(z_ref, h, da, gv_ref, bv_ref)
            for n in range(tm // CHUNK):
                rows = slice(n * CHUNK, (n + 1) * CHUNK)
                mixed = jnp.dot(ws_ref[h], vln[rows], preferred_element_type=F32) + bs_ref[h]
                y_ref[0, rows, h * HEAD:(h + 1) * HEAD] = (u[rows] * mixed).astype(BF16)
        gate_b = z_ref[:, 2 * da:3 * da].astype(F32)
        hc = z_ref[:, 3 * da:4 * da].astype(F32) * z_ref[:, 4 * da:5 * da].astype(F32)
        halo = jnp.where(i > 0, pc_ref[...].astype(F32) * px_ref[...].astype(F32), 0.0)
        row = lax.broadcasted_iota(jnp.int32, (tm, da), 0)
        y = cw_ref[0:1, :] * _shift_rows_down(hc, halo, 2, row) + cw_ref[1:2, :] * _shift_rows_down(hc, halo, 1, row) + cw_ref[2:3, :] * hc
        y_ref[1] = (gate_b * y).astype(BF16)

    prev = lambda col: pl.BlockSpec((BF16_ROWS, da), lambda i: (jnp.maximum(i * hb - 1, 0), col))
    in_specs = [pl.BlockSpec((tm, zc), lambda i: (i, 0)), prev(3), prev(4), _row((nh, CHUNK, CHUNK)), _row((nh, CHUNK, HEAD)),
                _row((nh, HEAD)), _row((nh, HEAD)), _row((CONV_TAPS, da))]
    return _call(body, name, (s // tm,), in_specs, pl.BlockSpec((2, tm, da), lambda i: (0, i, 0)),
                 jax.ShapeDtypeStruct((2, s, da), BF16), [], [z, z, z, ws_masked, bs_wide, gv, bv, cw])[0]


def _mixer_bwd(z, dy, ws_masked, bs_wide, gv, bv, cw, name, rider=None):
    s, zc = z.shape
    da = zc // 5
    nh = da // HEAD
    tm = min(s, MIXER_TOKENS)
    hb = tm // BF16_ROWS
    nblk = s // tm

    def body(z_ref, pc_ref, px_ref, nb_ref, dy_ref, ndy_ref, ws_ref, bs_ref, gv_ref, bv_ref, cw_ref,
             dz_ref, dws_ref, dbs_ref, dgv_ref, dbv_ref, dcw_ref, dbin_ref):
        i = pl.program_id(0)

        @pl.when(i == 0)
        def _():
            for ref in (dws_ref, dbs_ref, dgv_ref, dbv_ref, dcw_ref, dbin_ref):
                ref[...] = jnp.zeros_like(ref)

        causal = lax.broadcasted_iota(jnp.int32, (CHUNK, CHUNK), 0) >= lax.broadcasted_iota(jnp.int32, (CHUNK, CHUNK), 1)
        for h in range(nh):
            zu, zv, u, vhat, rstd, vln = _sgu_head_forward(z_ref, h, da, gv_ref, bv_ref)
            dya = dy_ref[0, :, h * HEAD:(h + 1) * HEAD].astype(F32)
            w = ws_ref[h]
            du_parts, dvln_parts = [], []
            for n in range(tm // CHUNK):
                rows = slice(n * CHUNK, (n + 1) * CHUNK)
                mixed = jnp.dot(w, vln[rows], preferred_element_type=F32) + bs_ref[h]
                du_parts.append(dya[rows] * mixed)
                dmix = dya[rows] * u[rows]
                dmix_b = dmix.astype(BF16)
                dws_ref[h] += jnp.where(causal, lax.dot_general(dmix_b, vln[rows], NT_DIMS, preferred_element_type=F32), 0.0)
                dbs_ref[h] += dmix
                dvln_parts.append(lax.dot_general(w, dmix_b, TN_DIMS, preferred_element_type=F32))
            du = jnp.concatenate(du_parts, axis=0)
            dvln = jnp.concatenate(dvln_parts, axis=0)
            dgv_ref[h:h + 1, :] += jnp.sum(dvln * vhat, axis=0, keepdims=True)
            dbv_ref[h:h + 1, :] += jnp.sum(dvln, axis=0, keepdims=True)
            dv = _ln_backward(dvln, vhat, rstd, gv_ref[h:h + 1, :])
            dzu = du * _gelu_grad(*zu)
            dzv = dv * _gelu_grad(*zv)
            ucols = slice(h * HEAD, (h + 1) * HEAD)
            vcols = slice(da + h * HEAD, da + (h + 1) * HEAD)
            dz_ref[:, ucols] = dzu.astype(BF16)
            dz_ref[:, vcols] = dzv.astype(BF16)
            dbin_ref[:, ucols] += jnp.sum(dzu, axis=0, keepdims=True)
            dbin_ref[:, vcols] += jnp.sum(dzv, axis=0, keepdims=True)

        gate_b = z_ref[:, 2 * da:3 * da].astype(F32)
        gate_c = z_ref[:, 3 * da:4 * da].astype(F32)
        xt = z_ref[:, 4 * da:5 * da].astype(F32)
        hc = gate_c * xt
        halo = jnp.where(i > 0, pc_ref[...].astype(F32) * px_ref[...].astype(F32), 0.0)
        row = lax.broadcasted_iota(jnp.int32, (tm, da), 0)
        sh1 = _shift_rows_down(hc, halo, 1, row)
        sh2 = _shift_rows_down(hc, halo, 2, row)
        y = cw_ref[0:1, :] * sh2 + cw_ref[1:2, :] * sh1 + cw_ref[2:3, :] * hc
        dyb = dy_ref[1].astype(F32)
        dconv = dyb * gate_b
        nhalo = jnp.where(i < nblk - 1, ndy_ref[...].astype(F32) * nb_ref[...].astype(F32), 0.0)
        dhc = cw_ref[2:3, :] * dconv + cw_ref[1:2, :] * _shift_rows_up(dconv, nhalo, 1, row) + cw_ref[0:1, :] * _shift_rows_up(dconv, nhalo, 2, row)
        dcw_ref[0:1, :] += jnp.sum(dconv * sh2, axis=0, keepdims=True)
        dcw_ref[1:2, :] += jnp.sum(dconv * sh1, axis=0, keepdims=True)
        dcw_ref[2:3, :] += jnp.sum(dconv * hc, axis=0, keepdims=True)
        for col, val in ((2, dyb * y), (3, dhc * xt), (4, dhc * gate_c)):
            cols = slice(col * da, (col + 1) * da)
            dz_ref[:, cols] = val.astype(BF16)
            dbin_ref[:, cols] += jnp.sum(val, axis=0, keepdims=True)

        @pl.when(i == nblk - 1)
        def _():
            for h in range(nh):
                dbs_ref[h] = jnp.broadcast_to(jnp.sum(dbs_ref[h], axis=1, keepdims=True), (CHUNK, HEAD))

    prev = lambda col: pl.BlockSpec((BF16_ROWS, da), lambda i: (jnp.maximum(i * hb - 1, 0), col))
    nxt = lambda i: jnp.minimum((i + 1) * hb, s // BF16_ROWS - 1)
    in_specs = [pl.BlockSpec((tm, zc), lambda i: (i, 0)), prev(3), prev(4), pl.BlockSpec((BF16_ROWS, da), lambda i: (nxt(i), 2)),
                pl.BlockSpec((2, tm, da), lambda i: (0, i, 0)), pl.BlockSpec((None, BF16_ROWS, da), lambda i: (1, nxt(i), 0)),
                _row((nh, CHUNK, CHUNK)), _row((nh, CHUNK, HEAD)), _row((nh, HEAD)), _row((nh, HEAD)), _row((CONV_TAPS, da))]
    out_specs = [pl.BlockSpec((tm, zc), lambda i: (i, 0)), _row((nh, CHUNK, CHUNK)), _row((nh, CHUNK, HEAD)), _row((nh, HEAD)),
                 _row((nh, HEAD)), _row((8, da)), _row((1, zc))]
    out_shape = [jax.ShapeDtypeStruct((s, zc), BF16), jax.ShapeDtypeStruct((nh, CHUNK, CHUNK), F32),
                 jax.ShapeDtypeStruct((nh, CHUNK, HEAD), F32), jax.ShapeDtypeStruct((nh, HEAD), F32),
                 jax.ShapeDtypeStruct((nh, HEAD), F32), jax.ShapeDtypeStruct((8, da), F32), jax.ShapeDtypeStruct((1, zc), F32)]
    return _call(body, name, (nblk,), in_specs, out_specs, out_shape, [], [z, z, z, z, dy, dy, ws_masked, bs_wide, gv, bv, cw], rider)


def _adam_update(g, w, m, v):
    m_new = ADAM_B1 * m + (1.0 - ADAM_B1) * g
    v_new = ADAM_B2 * v + (1.0 - ADAM_B2) * (g * g)
    m_hat = m_new / (1.0 - ADAM_B1 ** ADAM_STEP)
    v_hat = v_new / (1.0 - ADAM_B2 ** ADAM_STEP)
    return -ADAM_LR * (m_hat / (jnp.sqrt(v_hat) + ADAM_EPS) + ADAM_WD * w), m_new, v_new


def _adamw(gparts, w, m, v, name):
    n, r, c = gparts.shape
    tr = r // 4 if (r // 4) % BF16_ROWS == 0 else r

    def body(g_ref, w_ref, m_ref, v_ref, go_ref, d_ref, mo_ref, vo_ref):
        g = g_ref[0].astype(F32)
        for q in range(1, n):
            g = g + g_ref[q].astype(F32)
        go_ref[...] = g
        d_ref[...], mo_ref[...], vo_ref[...] = _adam_update(g, w_ref[...], m_ref[...], v_ref[...])

    blk = pl.BlockSpec((tr, c), lambda i: (i, 0))
    shp = jax.ShapeDtypeStruct((r, c), F32)
    return _call(body, name, (r // tr,), [pl.BlockSpec((n, tr, c), lambda i: (0, i, 0)), blk, blk, blk], [blk] * 4, [shp] * 4, [],
                 [gparts, w, m, v])[0]


def _adamw_small(packs, rows, w, m, v, conv, loss_scale, name):
    n_par, n_dev = len(rows), packs.shape[0]
    taps = conv[0].shape[0]

    def body(*refs):
        refs = list(refs)
        cut = lambda n: [refs.pop(0) for _ in range(n)]
        p_ref, w_refs, m_refs, v_refs, (cw_ref, cm_ref, cv_ref) = refs.pop(0), cut(n_par), cut(n_par), cut(n_par), cut(3)
        outs = [cut(4) for _ in range(n_par + 1)]
        loss_ref = refs.pop(0)
        at = 0
        for k in range(n_par):
            g = p_ref[0, at:at + rows[k], :]
            for dev in range(1, n_dev):
                g = g + p_ref[dev, at:at + rows[k], :]
            go_ref, d_ref, mo_ref, vo_ref = outs[k]
            go_ref[...] = g
            d_ref[...], mo_ref[...], vo_ref[...] = _adam_update(g, w_refs[k][...], m_refs[k][...], v_refs[k][...])
            at += rows[k]
        me = 4 * lax.axis_index("x") + 2 * lax.axis_index("y") + lax.axis_index("c")
        go_ref, d_ref, mo_ref, vo_ref = outs[n_par]
        for tap in range(taps):
            row = pl.ds(at + tap * n_dev + me, 1)
            g = p_ref[0, row, :]
            for dev in range(1, n_dev):
                g = g + p_ref[dev, row, :]
            one = slice(tap, tap + 1)
            go_ref[one, :] = g
            d_ref[one, :], mo_ref[one, :], vo_ref[one, :] = _adam_update(g, cw_ref[one, :], cm_ref[one, :], cv_ref[one, :])
        at += taps * n_dev
        sq = p_ref[0, at:, :]
        for dev in range(1, n_dev):
            sq = sq + p_ref[dev, at:, :]
        total = jnp.sum(jnp.sum(sq, axis=0, keepdims=True), axis=1, keepdims=True)
        loss_ref[...] = jnp.broadcast_to(loss_scale * total, loss_ref.shape)

    vmem = pl.BlockSpec(memory_space=pltpu.VMEM)
    ins = [packs] + list(w) + list(m) + list(v) + list(conv)
    out_shape = [jax.ShapeDtypeStruct(a.shape, F32) for a in list(w) + [conv[0]] for _ in range(4)] + [jax.ShapeDtypeStruct((8, LANES), F32)]
    outs = pl.pallas_call(body, name=name, in_specs=[vmem] * len(ins), out_specs=[vmem] * len(out_shape), out_shape=out_shape,
                          compiler_params=pltpu.CompilerParams(vmem_limit_bytes=V7X_VMEM_LIMIT))(*ins)
    return [outs[4 * k:4 * k + 4] for k in range(n_par + 1)], outs[-1][0, 0]


def _rows128(a):
    return a.reshape(-1, LANES)


def kernel(x, ffa_gate, ffa_up, ffa_down, ln_a_g, ln_a_b, w_in, b_in, w_s, b_s, ln_v_g, ln_v_b, conv_w, w_out, b_out, ln_m_g, ln_m_b, ffc_gate, ffc_up, ffc_down, ln_c_g, ln_c_b, loss_target, m_ffa_gate, m_ffa_up, m_ffa_down, m_ln_a_g, m_ln_a_b, m_w_in, m_b_in, m_w_s, m_b_s, m_ln_v_g, m_ln_v_b, m_conv_w, m_w_out, m_b_out, m_ln_m_g, m_ln_m_b, m_ffc_gate, m_ffc_up, m_ffc_down, m_ln_c_g, m_ln_c_b, v_ffa_gate, v_ffa_up, v_ffa_down, v_ln_a_g, v_ln_a_b, v_w_in, v_b_in, v_w_s, v_b_s, v_ln_v_g, v_ln_v_b, v_conv_w, v_w_out, v_b_out, v_ln_m_g, v_ln_m_b, v_ffc_gate, v_ffc_up, v_ffc_down, v_ln_c_g, v_ln_c_b):
    x2, target = x[0], loss_target[0]
    s, d = x2.shape
    da = d // 2
    nh = da // HEAD

    big = dict(ffa_gate=ffa_gate, ffa_up=ffa_up, ffa_down=ffa_down, w_in=w_in, w_out=w_out, ffc_gate=ffc_gate, ffc_up=ffc_up, ffc_down=ffc_down)
    big_m = dict(ffa_gate=m_ffa_gate, ffa_up=m_ffa_up, ffa_down=m_ffa_down, w_in=m_w_in, w_out=m_w_out, ffc_gate=m_ffc_gate, ffc_up=m_ffc_up, ffc_down=m_ffc_down)
    big_v = dict(ffa_gate=v_ffa_gate, ffa_up=v_ffa_up, ffa_down=v_ffa_down, w_in=v_w_in, w_out=v_w_out, ffc_gate=v_ffc_gate, ffc_up=v_ffc_up, ffc_down=v_ffc_down)
    local = lambda k, a: jnp.transpose(a[0]) if k in TRANSPOSED else a[0]
    shard = {k: local(k, w).astype(BF16) for k, w in big.items()}
    conv_rows = jnp.pad(conv_w[0], ((0, 8 - CONV_TAPS), (0, 0)))

    tril = jnp.tril(jnp.ones((CHUNK, CHUNK), dtype=bool))
    ws_masked = jnp.where(tril[None], w_s[0], 0.0).astype(BF16)
    bs_wide = jnp.broadcast_to(b_s[0][:, :, None], (nh, CHUNK, HEAD))
    gv, bv = ln_v_g.reshape(nh, HEAD), ln_v_b.reshape(nh, HEAD)

    full = {}
    xb = x2.astype(BF16)
    (silu_a, udsilu_a, hid_a), (full["ffa_gate"], full["ffa_up"], full["ffa_down"]) = _ffn_gateup_gathering(
        xb, shard["ffa_gate"], shard["ffa_up"], shard["ffa_down"], "ffa_gateup")
    (xhat1, h1b, rstd1), (full["w_in"], full["w_out"], conv_full) = _down_ln(
        hid_a, full["ffa_down"], None, x2, None, ln_a_g, ln_a_b, 0.5, "ffa_down_ln", rider=_Gather([shard["w_in"], shard["w_out"], conv_rows], by_columns=(0,)))
    cw = jnp.transpose(conv_full[:, :CONV_TAPS, :], (1, 0, 2)).reshape(CONV_TAPS, da)
    w_out2 = full["w_out"].reshape(2, da, d)
    z, (full["ffc_gate"],) = _proj_in(h1b, full["w_in"], b_in, "proj_in", _Gather([shard["ffc_gate"]]))
    ycat = _mixer_fwd(z, ws_masked, bs_wide, gv, bv, cw, "mixer_fwd")
    (xhat2, h2b, rstd2), _ = _down_ln(ycat, w_out2, b_out, xhat1, (ln_a_g, ln_a_b), ln_m_g, ln_m_b, 1.0, "proj_out_ln")
    (silu_c, udsilu_c, hid_c), (full["ffc_up"], full["ffc_down"]) = _ffn_gateup_gathering(
        h2b, full["ffc_gate"], shard["ffc_up"], shard["ffc_down"], "ffc_gateup")
    (dr3, dr3b, sq_err, d_ln_c_g, d_ln_c_b), _ = _down_ln(hid_c, full["ffc_down"], None, xhat2, (ln_m_g, ln_m_b), ln_c_g, ln_c_b, 0.5,
                                                          "ffc_down_ln_loss", target=target)

    landed = {}
    (dg_c, du_c), _ = _ffn_bwd_hidden(dr3b, full["ffc_down"], silu_c, udsilu_c, "ffc_bwd_hidden")
    part, _ = _dw_hidden_rows_paired(hid_c, dr3b, "ffc_dw_down")
    part, (landed["ffc_down"],) = _dw_hidden_rows_paired(dg_c, h2b, "ffc_dw_gate", _ChipScatter(part))
    part, (landed["ffc_gate"],) = _dw_hidden_rows_paired(du_c, h2b, "ffc_dw_up", _ChipScatter(part))
    (dr2, dr2b, d_ln_m_g, d_ln_m_b, d_b_out), (landed["ffc_up"],) = _ffn_bwd_input(
        dg_c, full["ffc_gate"], du_c, full["ffc_up"], dr3, (xhat2, rstd2, ln_m_g), "ffc_bwd_input_ln", _ChipScatter(part))
    updated = {}
    adam_side = lambda k: _AdamSide(landed[k], local(k, big[k]), local(k, big_m[k]), local(k, big_v[k]))
    dycat, updated["ffc_down"] = _nt_hidden(dr2b, w_out2, "proj_out_bwd", adam_side("ffc_down"))
    part, _ = _dw_hidden_rows(ycat, dr2b, "proj_out_dw")
    (dz, d_w_s, d_b_s_wide, d_gv, d_bv, d_cw, d_b_in), (landed["w_out"],) = _mixer_bwd(
        z, dycat, ws_masked, bs_wide, gv, bv, cw, "mixer_bwd", _Scatter(part.reshape(N_DEV, d // N_DEV, d)))
    (dr1, dr1b, d_ln_a_g, d_ln_a_b), updated["ffc_gate"] = _proj_in_bwd_ln(
        dz, full["w_in"], dr2, (xhat1, rstd1, ln_a_g), 0.5, "proj_in_bwd_ln", adam_side("ffc_gate"))
    small_g = dict(ln_a_g=d_ln_a_g, ln_a_b=d_ln_a_b, b_in=d_b_in, w_s=d_w_s, b_s=d_b_s_wide[:, :, 0], ln_v_g=d_gv, ln_v_b=d_bv, b_out=d_b_out,
                   ln_m_g=d_ln_m_g, ln_m_b=d_ln_m_b, ln_c_g=d_ln_c_g, ln_c_b=d_ln_c_b)
    pack = jnp.concatenate([_rows128(g) for g in small_g.values()] + [_rows128(d_cw[:CONV_TAPS]), _rows128(sq_err)], axis=0)
    part, (packs,) = _dw_cols(h1b, dz, "proj_in_dw", _Scatter(pack, whole=True))
    (dg_a, du_a), (landed["w_in"],) = _ffn_bwd_hidden(dr1b, full["ffa_down"], silu_a, udsilu_a, "ffa_bwd_hidden", _Scatter(part, by_columns=True))
    part, _ = _dw_hidden_rows_paired(hid_a, dr1b, "ffa_dw_down")
    part, (landed["ffa_down"],) = _dw_hidden_rows_paired(dg_a, xb, "ffa_dw_gate", _ChipScatter(part))
    part, (landed["ffa_gate"],) = _dw_hidden_rows_paired(du_a, xb, "ffa_dw_up", _ChipScatter(part))
    grad_x, (landed["ffa_up"],) = _ffn_bwd_input(dg_a, full["ffa_gate"], du_a, full["ffa_up"], dr1, None, "ffa_bwd_input", _ChipScatter(part))

    grads, deltas, new_m, new_v = {}, {}, {}, {}
    for k in big:
        out = updated[k] if k in updated else _adamw(landed[k], local(k, big[k]), local(k, big_m[k]), local(k, big_v[k]), "adamw_" + k)
        grads[k], deltas[k], new_m[k], new_v[k] = ((jnp.transpose(o) if k in TRANSPOSED else o).reshape(big[k].shape) for o in out)

    small = dict(ln_a_g=ln_a_g, ln_a_b=ln_a_b, b_in=b_in, w_s=w_s, b_s=b_s, ln_v_g=ln_v_g, ln_v_b=ln_v_b, b_out=b_out,
                 ln_m_g=ln_m_g, ln_m_b=ln_m_b, ln_c_g=ln_c_g, ln_c_b=ln_c_b)
    small_m = dict(ln_a_g=m_ln_a_g, ln_a_b=m_ln_a_b, b_in=m_b_in, w_s=m_w_s, b_s=m_b_s, ln_v_g=m_ln_v_g, ln_v_b=m_ln_v_b, b_out=m_b_out,
                   ln_m_g=m_ln_m_g, ln_m_b=m_ln_m_b, ln_c_g=m_ln_c_g, ln_c_b=m_ln_c_b)
    small_v = dict(ln_a_g=v_ln_a_g, ln_a_b=v_ln_a_b, b_in=v_b_in, w_s=v_w_s, b_s=v_b_s, ln_v_g=v_ln_v_g, ln_v_b=v_ln_v_b, b_out=v_b_out,
                   ln_m_g=v_ln_m_g, ln_m_b=v_ln_m_b, ln_c_g=v_ln_c_g, ln_c_b=v_ln_c_b)
    snames = list(small)
    assert snames == list(small_g) and conv_w.shape[2] == LANES and da == N_DEV * LANES
    views = lambda tree: [_rows128(tree[k]) for k in snames]
    out, loss = _adamw_small(packs, [a.shape[0] for a in views(small)], views(small), views(small_m), views(small_v),
                             (conv_w[0], m_conv_w[0], v_conv_w[0]), 0.5 / d, "adamw_small")
    for k, per_param in zip(snames + ["conv_w"], out):
        shape = conv_w.shape if k == "conv_w" else small[k].shape
        grads[k], deltas[k], new_m[k], new_v[k] = (o.reshape(shape) for o in per_param)

    order = ["ffa_gate", "ffa_up", "ffa_down", "ln_a_g", "ln_a_b", "w_in", "b_in", "w_s", "b_s", "ln_v_g", "ln_v_b", "conv_w", "w_out", "b_out",
             "ln_m_g", "ln_m_b", "ffc_gate", "ffc_up", "ffc_down", "ln_c_g", "ln_c_b"]
    return (loss, grad_x[None], *[grads[k] for k in order], *[deltas[k] for k in order], *[new_m[k] for k in order], *[new_v[k] for k in order])
```

```python
import math

import jax
import jax.numpy as jnp
from jax import lax
from jax.experimental import pallas as pl
from jax.experimental.pallas import tpu as pltpu

BF16 = jnp.bfloat16
F32 = jnp.float32
MESH = pl.DeviceIdType.MESH

N_DEV = 8
HEAD = 128
CHUNK = 128
CONV_TAPS = 3
LN_EPS = 1e-5
ALPHA = float(2 ** 0.25)
GELU_C = 0.7978845608028654
GELU_A = 0.044715
ADAM_LR, ADAM_B1, ADAM_B2, ADAM_EPS, ADAM_WD, ADAM_STEP = 0.001, 0.9, 0.999, 1e-08, 0.01, 10
V7X_VMEM_LIMIT = 56 * 1024 * 1024
LANES = 128
BF16_ROWS = 16
MXU_COLS = 256
TRANSPOSED = ("ffa_gate", "ffa_up", "ffc_gate", "ffc_up")
PROJ_IN_COLS = 1280
MIXER_TOKENS = 512
ADAM_SIDE_ROWS = 64
DW_TOKENS = 2048

NT_DIMS = (((1,), (1,)), ((), ()))
TN_DIMS = (((0,), (0,)), ((), ()))
ANY = pl.BlockSpec(memory_space=pl.ANY)


def _gelu_tanh(x):
    return jnp.tanh(GELU_C * (x + GELU_A * x * x * x))


def _gelu(x, t):
    return 0.5 * x * (1.0 + t)


def _gelu_grad(x, t):
    return 0.5 * (1.0 + t) + 0.5 * x * (1.0 - t * t) * GELU_C * (1.0 + 3.0 * GELU_A * x * x)


def _sigmoid(x):
    return 0.5 * jnp.tanh(0.5 * x) + 0.5


def _row(shape):
    return pl.BlockSpec(shape, lambda *_: (0,) * len(shape))


def _row_blocks(tm, rows=128):
    rows = min(rows, tm)
    return [slice(r, r + rows) for r in range(0, tm, rows)]


def _ln_backward(dh, xhat, rstd, gain):
    dxh = dh * gain
    m1 = jnp.mean(dxh, axis=-1, keepdims=True)
    m2 = jnp.mean(dxh * xhat, axis=-1, keepdims=True)
    return rstd * (dxh - m1 - xhat * m2)


def _place():
    x, y, c = lax.axis_index("x"), lax.axis_index("y"), lax.axis_index("c")
    return x, y, c, [(1 - x, y), (x, 1 - y), (1 - x, 1 - y)]


def _other_devices(x, y, c):
    flips = [(bx, by, bc) for bx in (0, 1) for by in (0, 1) for bc in (0, 1)][1:]
    return [(1 - x if bx else x, 1 - y if by else y, 1 - c if bc else c) for bx, by, bc in flips]


class _Gather:
    def __init__(self, shards, forward_at=0.75, by_columns=()):
        n = len(shards)
        self.n, self.forward_at, self.by_columns = n, forward_at, tuple(by_columns)
        self.inputs = list(shards)
        self.out_shapes = [jax.ShapeDtypeStruct((a.shape[0], N_DEV * a.shape[1]) if i in self.by_columns else (N_DEV,) + a.shape, a.dtype)
                           for i, a in enumerate(shards)]
        self.scratch = [pltpu.SemaphoreType.DMA((n, 7)), pltpu.SemaphoreType.DMA((n, 7)), pltpu.SemaphoreType.DMA((n,))]

    def _block(self, outs, a, dev):
        if a in self.by_columns:
            cols = outs[a].shape[1] // N_DEV
            return outs[a].at[:, pl.ds(dev * cols, cols)]
        return outs[a].at[dev]

    def _copy(self, outs, sems, a, k, block, to, src=None):
        dst = self._block(outs, a, block)
        return pltpu.make_async_remote_copy(src_ref=dst if src is None else src, dst_ref=dst, send_sem=sems[0].at[a, k],
                                            recv_sem=sems[1].at[a, k], device_id=to, device_id_type=MESH)

    def start(self, ins, outs, sems, urgent=None):
        x, y, c, chips = _place()
        me = 4 * x + 2 * y + c
        for a in range(self.n):
            pltpu.make_async_copy(ins[a], self._block(outs, a, me), sems[2].at[a]).start()
        urgent = list(range(self.n)) if urgent is None else list(urgent)
        for group in (urgent, [a for a in range(self.n) if a not in urgent]):
            for a in group:
                self._copy(outs, sems, a, 0, me, (x, y, 1 - c), src=ins[a]).start()
                for j in (0, 1):
                    self._copy(outs, sems, a, 1 + j, me, (*chips[j], c), src=ins[a]).start()
            for a in group:
                self._copy(outs, sems, a, 3, me, (*chips[2], c), src=ins[a]).start()

    def wait_sibling(self, outs, sems, a):
        x, y, c, _ = _place()
        self._copy(outs, sems, a, 0, 4 * x + 2 * y + 1 - c, (x, y, 1 - c)).wait_recv()

    def pass_on(self, outs, sems, a, j):
        x, y, c, chips = _place()
        block = 4 * chips[j][0] + 2 * chips[j][1] + c
        self._copy(outs, sems, a, 1 + j, block, (x, y, 1 - c)).wait_recv()
        self._copy(outs, sems, a, 4 + j, block, (x, y, 1 - c)).start()

    def wait_passed(self, outs, sems, a, j):
        x, y, c, chips = _place()
        self._copy(outs, sems, a, 4 + j, 4 * chips[j][0] + 2 * chips[j][1] + 1 - c, (x, y, 1 - c)).wait_recv()

    def wait_sent(self, ins, outs, sems, a):
        x, y, c, _ = _place()
        me = 4 * x + 2 * y + c
        for k in range(7):
            self._copy(outs, sems, a, k, me, (x, y, 1 - c), src=ins[a]).wait_send()
        pltpu.make_async_copy(ins[a], self._block(outs, a, me), sems[2].at[a]).wait()

    def forward(self, ins, outs, sems):
        for a in range(self.n):
            for j in range(3):
                self.pass_on(outs, sems, a, j)

    def finish(self, ins, outs, sems):
        for a in range(self.n):
            self.wait_sibling(outs, sems, a)
            for j in range(3):
                self.wait_passed(outs, sems, a, j)
        for a in range(self.n):
            self.wait_sent(ins, outs, sems, a)

    def before(self, step, n_steps, ins, outs, sems):
        pl.when(step == 0)(lambda: self.start(ins, outs, sems))
        pl.when(step == int(self.forward_at * (n_steps - 1)))(lambda: self.forward(ins, outs, sems))

    def after(self, step, n_steps, ins, outs, sems):
        pl.when(step == n_steps - 1)(lambda: self.finish(ins, outs, sems))


class _Scatter:
    def __init__(self, partial, whole=False, by_columns=False):
        self.whole, self.by_columns = whole, by_columns
        self.inputs = [partial]
        if whole:
            shape = (N_DEV,) + partial.shape
        elif by_columns:
            shape = (N_DEV, partial.shape[0], partial.shape[1] // N_DEV)
        else:
            shape = partial.shape
        self.out_shapes = [jax.ShapeDtypeStruct(shape, partial.dtype)]
        self.scratch = [pltpu.SemaphoreType.DMA((7,)), pltpu.SemaphoreType.DMA((7,)), pltpu.SemaphoreType.DMA((1,))]

    def _copies(self, ins, outs, sems):
        x, y, c, _ = _place()
        me = 4 * x + 2 * y + c
        if self.whole:
            block = lambda dev: ins[0]
        elif self.by_columns:
            cols = ins[0].shape[1] // N_DEV
            block = lambda dev: ins[0].at[:, pl.ds(dev * cols, cols)]
        else:
            block = lambda dev: ins[0].at[dev]
        mine = pltpu.make_async_copy(block(me), outs[0].at[me], sems[2].at[0])
        remote = [pltpu.make_async_remote_copy(src_ref=block(4 * px + 2 * py + pc), dst_ref=outs[0].at[me], send_sem=sems[0].at[k],
                                               recv_sem=sems[1].at[k], device_id=(px, py, pc), device_id_type=MESH)
                  for k, (px, py, pc) in enumerate(_other_devices(x, y, c))]
        return mine, remote

    def start(self, ins, outs, sems):
        mine, remote = self._copies(ins, outs, sems)
        mine.start()
        for cp in remote:
            cp.start()

    def finish(self, ins, outs, sems):
        mine, remote = self._copies(ins, outs, sems)
        for cp in remote:
            cp.wait()
        mine.wait()

    def before(self, step, n_steps, ins, outs, sems):
        pl.when(step == 0)(lambda: self.start(ins, outs, sems))

    def after(self, step, n_steps, ins, outs, sems):
        pl.when(step == n_steps - 1)(lambda: self.finish(ins, outs, sems))


class _ChipScatter(_Scatter):
    def __init__(self, sums):
        super().__init__(sums)
        self.scratch = [pltpu.SemaphoreType.DMA((3,)), pltpu.SemaphoreType.DMA((3,)), pltpu.SemaphoreType.DMA((1,))]

    def _copies(self, ins, outs, sems):
        x, y, c, chips = _place()
        my_chip = 2 * x + y
        mine = pltpu.make_async_copy(ins[0].at[my_chip], outs[0].at[my_chip], sems[2].at[0])
        remote = [pltpu.make_async_remote_copy(src_ref=ins[0].at[2 * px + py], dst_ref=outs[0].at[my_chip], send_sem=sems[0].at[k],
                                               recv_sem=sems[1].at[k], device_id=(px, py, c), device_id_type=MESH)
                  for k, (px, py) in enumerate(chips)]
        return mine, remote


class _AdamSide:
    def __init__(self, gparts, w, m, v):
        self.n, self.r, self.c = gparts.shape
        self.inputs = [gparts, w, m, v]
        self.out_shapes = [jax.ShapeDtypeStruct((self.r, self.c), F32)] * 4
        self.scratch = []

    def _plan(self, n_steps):
        rows = ADAM_SIDE_ROWS
        while self.r // rows > n_steps:
            rows *= 2
        assert self.r % rows == 0
        return rows, self.r // rows

    def _rows(self, grid):
        rows, n_blocks = self._plan(math.prod(grid))

        def block(*ids):
            step = 0
            for size, pid in zip(grid, ids):
                step = step * size + pid
            return jnp.minimum(step, n_blocks - 1)
        return pl.BlockSpec((rows, self.c), lambda *ids: (block(*ids), 0)), block, rows

    def in_specs(self, grid):
        spec, block, rows = self._rows(grid)
        return [pl.BlockSpec((self.n, rows, self.c), lambda *ids: (0, block(*ids), 0)), spec, spec, spec]

    def out_specs(self, grid):
        return [self._rows(grid)[0]] * 4

    def before(self, step, n_steps, ins, outs, sems):
        @pl.when(step < self._plan(n_steps)[1])
        def _():
            g_ref, w_ref, m_ref, v_ref = ins
            g = g_ref[0].astype(F32)
            for q in range(1, self.n):
                g = g + g_ref[q].astype(F32)
            outs[0][...] = g
            outs[1][...], outs[2][...], outs[3][...] = _adam_update(g, w_ref[...], m_ref[...], v_ref[...])

    def after(self, step, n_steps, ins, outs, sems):
        pass


def _call(body, name, grid, in_specs, out_specs, out_shape, scratch, ins, rider=None):
    single = not isinstance(out_shape, (list, tuple))
    out_shape = [out_shape] if single else list(out_shape)
    out_specs = [out_specs] if single else list(out_specs)
    params = pltpu.CompilerParams(dimension_semantics=("arbitrary",) * len(grid), vmem_limit_bytes=V7X_VMEM_LIMIT)
    if rider is None:
        outs = pl.pallas_call(body, name=name, grid=grid, in_specs=in_specs, out_specs=out_specs, out_shape=out_shape,
                              scratch_shapes=scratch, compiler_params=params)(*ins)
        return (outs[0] if single else outs), None
    riders = list(rider) if isinstance(rider, (list, tuple)) else [rider]
    n_in, n_out, n_scr = len(ins), len(out_shape), len(scratch)
    n_steps = math.prod(grid)
    specs = lambda r, kind, count: getattr(r, kind)(grid) if hasattr(r, kind) else [ANY] * count

    def carried(*refs):
        refs = list(refs)
        cut = lambda n: [refs.pop(0) for _ in range(n)]
        b_in, r_in = cut(n_in), [cut(len(r.inputs)) for r in riders]
        b_out, r_out = cut(n_out), [cut(len(r.out_shapes)) for r in riders]
        b_scr, r_scr = cut(n_scr), [cut(len(r.scratch)) for r in riders]
        step = 0
        for axis, size in enumerate(grid):
            step = step * size + pl.program_id(axis)
        for r, r_ins, r_outs, r_sems in zip(riders, r_in, r_out, r_scr):
            r.before(step, n_steps, r_ins, r_outs, r_sems)
        body(*b_in, *b_out, *b_scr)
        for r, r_ins, r_outs, r_sems in zip(riders, r_in, r_out, r_scr):
            r.after(step, n_steps, r_ins, r_outs, r_sems)

    outs = pl.pallas_call(
        carried, name=name, grid=grid,
        in_specs=list(in_specs) + [sp for r in riders for sp in specs(r, "in_specs", len(r.inputs))],
        out_specs=out_specs + [sp for r in riders for sp in specs(r, "out_specs", len(r.out_shapes))],
        out_shape=out_shape + [sh for r in riders for sh in r.out_shapes],
        scratch_shapes=list(scratch) + [sc for r in riders for sc in r.scratch], compiler_params=params,
    )(*ins, *[a for r in riders for a in r.inputs])
    base, rest = outs[:n_out], list(outs[n_out:])
    per_rider = [[rest.pop(0) for _ in r.out_shapes] for r in riders]
    return (base[0] if single else base), (per_rider if isinstance(rider, (list, tuple)) else per_rider[0])


def _arrival_block(j):
    x, y, c = lax.axis_index("x"), lax.axis_index("y"), lax.axis_index("c")
    chip, other_core = j // 2, j % 2
    px = jnp.where((chip == 1) | (chip == 3), 1 - x, x)
    py = jnp.where((chip == 2) | (chip == 3), 1 - y, y)
    pc = jnp.where(other_core == 1, 1 - c, c)
    return 4 * px + 2 * py + pc


def _ffn_gateup_gathering(xb, gate, up_shard, down_shard, name):
    s, d = xb.shape
    fs = up_shard.shape[0]
    tm = min(s, 1024)
    ni = s // tm
    ask_at = max(ni - 2, 0)
    gate_here = gate.ndim == 2
    gather = _Gather(([gate] if gate_here else []) + [up_shard, down_shard])
    n_g = gather.n
    used_here, down = tuple(range(n_g - 1)), n_g - 1

    def body(x_ref, *refs):
        refs = list(refs)
        gate_full = None if gate_here else refs.pop(0)
        shards = [refs.pop(0) for _ in range(n_g)]
        silu_ref, udsilu_ref, h_ref = refs.pop(0), refs.pop(0), refs.pop(0)
        fulls = [refs.pop(0) for _ in range(n_g)]
        w_ref, w_sems = refs.pop(0), refs.pop(0)
        sems = refs
        j, i = pl.program_id(0), pl.program_id(1)
        gate_src, up_src = (fulls[0], fulls[1]) if gate_here else (gate_full, fulls[0])

        def load(slot, srcs):
            return [pltpu.make_async_copy(src, w_ref.at[slot, a], w_sems.at[slot, a]) for a, src in enumerate(srcs)]

        @pl.when((j == 0) & (i == 0))
        def _():
            gather.start(shards, fulls, sems, urgent=used_here)
            mine = load(0, (shards[0] if gate_here else gate_full.at[_arrival_block(0)], shards[n_g - 2]))
            for cp in mine:
                cp.start()
            for cp in mine:
                cp.wait()

        for nxt in range(1, N_DEV):
            @pl.when((j == nxt) & (i == 0))
            def _(nxt=nxt):
                for cp in load(nxt % 2, (gate_src.at[0], up_src.at[0])):
                    cp.wait()

        for nxt in range(1, N_DEV):
            @pl.when((j == nxt - 1) & (i == ask_at))
            def _(nxt=nxt):
                for a in used_here:
                    if nxt == 1:
                        gather.wait_sibling(fulls, sems, a)
                    elif nxt % 2 == 0:
                        gather.pass_on(fulls, sems, a, nxt // 2 - 1)
                    else:
                        gather.wait_passed(fulls, sems, a, nxt // 2 - 1)
                block = _arrival_block(nxt)
                for cp in load(nxt % 2, (gate_src.at[block], up_src.at[block])):
                    cp.start()

        @pl.when((j == N_DEV - 1) & (i == ask_at))
        def _():
            for other_chip in range(3):
                gather.pass_on(fulls, sems, down, other_chip)

        x = x_ref[...]
        g = lax.dot_general(x, w_ref[j % 2, 0], NT_DIMS, preferred_element_type=F32)
        u = lax.dot_general(x, w_ref[j % 2, 1], NT_DIMS, preferred_element_type=F32)
        sg = _sigmoid(g)
        silu = g * sg
        silu_ref[...] = silu.astype(BF16)
        udsilu_ref[...] = (u * (sg + silu * (1.0 - sg))).astype(BF16)
        h_ref[...] = (silu * u).astype(BF16)

        @pl.when((j == N_DEV - 1) & (i == ni - 1))
        def _():
            gather.wait_sibling(fulls, sems, down)
            for other_chip in range(3):
                gather.wait_passed(fulls, sems, down, other_chip)
            for a in range(n_g):
                gather.wait_sent(shards, fulls, sems, a)

    shp = jax.ShapeDtypeStruct((N_DEV, s, fs), BF16)
    o_spec = pl.BlockSpec((None, tm, fs), lambda j, i: (_arrival_block(j), i, 0))
    ins = ([] if gate_here else [gate]) + gather.inputs
    outs = pl.pallas_call(
        body, name=name, grid=(N_DEV, ni), in_specs=[pl.BlockSpec((tm, d), lambda j, i: (i, 0))] + [ANY] * len(ins),
        out_specs=[o_spec, o_spec, o_spec] + [ANY] * n_g, out_shape=[shp, shp, shp] + gather.out_shapes,
        scratch_shapes=[pltpu.VMEM((2, 2, fs, d), BF16), pltpu.SemaphoreType.DMA((2, 2))] + gather.scratch,
        compiler_params=pltpu.CompilerParams(dimension_semantics=("arbitrary", "arbitrary"), vmem_limit_bytes=V7X_VMEM_LIMIT),
    )(xb, *ins)
    return outs[:3], outs[3:]


def _down_ln(a3, w3, bias, res, res_affine, ln_g, ln_b, scale, name, target=None, rider=None):
    nk, s, tk = a3.shape
    d = w3.shape[2]
    tm = min(s, 256)
    final = target is not None

    def body(*refs):
        refs = list(refs)
        a_ref, w_hbm = refs[:2]
        del refs[:2]
        bias_ref = refs.pop(0) if bias is not None else None
        res_ref = refs.pop(0)
        rg_ref, rb_ref = (refs.pop(0), refs.pop(0)) if res_affine is not None else (None, None)
        g_ref, b_ref = refs.pop(0), refs.pop(0)
        t_ref = refs.pop(0) if final else None
        w_sem = refs.pop()
        w_ref = refs.pop()
        i = pl.program_id(0)
        if final:
            dr_ref, drb_ref, sq_ref, dg_ref, db_ref = refs
        else:
            xh_ref, hb_ref, rstd_ref = refs

        @pl.when(i == 0)
        def _():
            whole = pltpu.make_async_copy(w_hbm, w_ref, w_sem.at[0])
            whole.start()
            whole.wait()
            if final:
                sq_ref[...] = jnp.zeros_like(sq_ref)
                dg_ref[...] = jnp.zeros_like(dg_ref)
                db_ref[...] = jnp.zeros_like(db_ref)

        y = jnp.dot(a_ref[0], w_ref[0], preferred_element_type=F32)
        for k in range(1, nk):
            y = y + jnp.dot(a_ref[k], w_ref[k], preferred_element_type=F32)
        if bias_ref is not None:
            y = y + bias_ref[...]
        for rows in _row_blocks(tm):
            r = res_ref[rows, :]
            if rg_ref is not None:
                r = r * rg_ref[...] + rb_ref[...]
            r = ALPHA * r + scale * y[rows]
            mu = jnp.mean(r, axis=-1, keepdims=True)
            c = r - mu
            var = jnp.mean(c * c, axis=-1, keepdims=True)
            rstd = lax.rsqrt(var + LN_EPS)
            xhat = c * rstd
            h = xhat * g_ref[...] + b_ref[...]
            if not final:
                xh_ref[rows, :] = xhat
                hb_ref[rows, :] = h.astype(BF16)
                rstd_ref[rows, :] = rstd
            else:
                err = h - t_ref[rows, :]
                sq_ref[...] += jnp.sum(err * err, axis=0, keepdims=True)
                dh = err * (1.0 / d)
                dg_ref[...] += jnp.sum(dh * xhat, axis=0, keepdims=True)
                db_ref[...] += jnp.sum(dh, axis=0, keepdims=True)
                dr = _ln_backward(dh, xhat, rstd, g_ref[...])
                dr_ref[rows, :] = dr
                drb_ref[rows, :] = (scale * dr).astype(BF16)

    tok = pl.BlockSpec((tm, d), lambda i: (i, 0))
    vec = pl.BlockSpec((1, d), lambda i: (0, 0))
    ins = [a3, w3]
    in_specs = [pl.BlockSpec((nk, tm, tk), lambda i: (0, i, 0)), ANY]
    if bias is not None:
        ins.append(bias)
        in_specs.append(vec)
    ins.append(res)
    in_specs.append(tok)
    if res_affine is not None:
        ins += list(res_affine)
        in_specs += [vec, vec]
    ins += [ln_g, ln_b]
    in_specs += [vec, vec]
    if final:
        ins.append(target)
        in_specs.append(tok)
        out_shape = [jax.ShapeDtypeStruct((s, d), F32), jax.ShapeDtypeStruct((s, d), BF16)] + [jax.ShapeDtypeStruct((1, d), F32)] * 3
        out_specs = [tok, tok, vec, vec, vec]
    else:
        out_shape = [jax.ShapeDtypeStruct((s, d), F32), jax.ShapeDtypeStruct((s, d), BF16), jax.ShapeDtypeStruct((s, 1), F32)]
        out_specs = [tok, tok, pl.BlockSpec((tm, 1), lambda i: (i, 0))]
    scratch = [pltpu.VMEM((nk, tk, d), BF16), pltpu.SemaphoreType.DMA((1,))]
    return _call(body, name, (s // tm,), in_specs, out_specs, out_shape, scratch, ins, rider)


def _proj_in(hb, w, bias, name, rider=None):
    s, d = hb.shape
    n = w.shape[1]
    tm = min(s, 1024)
    tn = PROJ_IN_COLS if n % PROJ_IN_COLS == 0 else n

    def body(h_ref, w_ref, b_ref, z_ref):
        z_ref[...] = (jnp.dot(h_ref[...], w_ref[...], preferred_element_type=F32) + b_ref[...]).astype(BF16)

    in_specs = [pl.BlockSpec((tm, d), lambda i, j: (i, 0)), pl.BlockSpec((d, tn), lambda i, j: (0, j)),
                pl.BlockSpec((1, tn), lambda i, j: (0, j))]
    return _call(body, name, (s // tm, n // tn), in_specs, pl.BlockSpec((tm, tn), lambda i, j: (i, j)),
                 jax.ShapeDtypeStruct((s, n), BF16), [], [hb, w, bias], rider)


def _nt_hidden(ab, w3, name, rider=None):
    s, kdim = ab.shape
    nj, tn, _ = w3.shape
    tm = min(s, 1024)

    def body(a_ref, w_ref, o_ref):
        o_ref[...] = lax.dot_general(a_ref[...], w_ref[...], NT_DIMS, preferred_element_type=F32).astype(BF16)

    in_specs = [pl.BlockSpec((tm, kdim), lambda i, j: (i, 0)), pl.BlockSpec((None, tn, kdim), lambda i, j: (j, 0, 0))]
    return _call(body, name, (s // tm, nj), in_specs, pl.BlockSpec((None, tm, tn), lambda i, j: (j, i, 0)),
                 jax.ShapeDtypeStruct((nj, s, tn), BF16), [], [ab, w3], rider)


def _ffn_bwd_hidden(ab, w3, silu3, udsilu3, name, rider=None):
    s, kdim = ab.shape
    nj, tn, _ = w3.shape
    tm = min(s, 1024)

    def body(a_ref, w_ref, silu_ref, udsilu_ref, dg_ref, du_ref):
        a = a_ref[...]
        for c0 in range(0, tn, MXU_COLS):
            cols = slice(c0, min(c0 + MXU_COLS, tn))
            t = lax.dot_general(a, w_ref[cols, :], NT_DIMS, preferred_element_type=F32)
            du_ref[:, cols] = (t * silu_ref[:, cols].astype(F32)).astype(BF16)
            dg_ref[:, cols] = (t * udsilu_ref[:, cols].astype(F32)).astype(BF16)

    hid = pl.BlockSpec((None, tm, tn), lambda i, j: (j, i, 0))
    shp = jax.ShapeDtypeStruct((nj, s, tn), BF16)
    in_specs = [pl.BlockSpec((tm, kdim), lambda i, j: (i, 0)), pl.BlockSpec((None, tn, kdim), lambda i, j: (j, 0, 0)), hid, hid]
    return _call(body, name, (s // tm, nj), in_specs, [hid, hid], [shp, shp], [], [ab, w3, silu3, udsilu3], rider)


def _tn_dw(a, a_spec, b, b_spec, nj, m, n, s, tk, name, rider):
    def body(a_ref, b_ref, o_ref, acc_ref):
        k = pl.program_id(1)

        @pl.when(k == 0)
        def _():
            acc_ref[...] = jnp.zeros_like(acc_ref)

        acc_ref[...] += lax.dot_general(a_ref[...], b_ref[...], TN_DIMS, preferred_element_type=F32)

        @pl.when(k == s // tk - 1)
        def _():
            o_ref[...] = acc_ref[...].astype(BF16)

    return _call(body, name, (nj, s // tk), [a_spec, b_spec], pl.BlockSpec((None, m, n), lambda j, k: (j, 0, 0)),
                 jax.ShapeDtypeStruct((nj, m, n), BF16), [pltpu.VMEM((m, n), F32)], [a, b], rider)


def _dw_hidden_rows(hid3, db, name, rider=None):
    nj, s, fs = hid3.shape
    d = db.shape[1]
    tk = min(s, DW_TOKENS)
    return _tn_dw(hid3, pl.BlockSpec((None, tk, fs), lambda j, k: (j, k, 0)), db, pl.BlockSpec((tk, d), lambda j, k: (k, 0)),
                  nj, fs, d, s, tk, name, rider)


def _dw_hidden_rows_paired(hid3, db, name, rider=None):
    nj, s, fs = hid3.shape
    d = db.shape[1]
    tk = min(s, DW_TOKENS)
    nk = s // tk
    half = nj // 2

    def device_of(j):
        c = lax.axis_index("c")
        return 2 * (j % half) + jnp.where(j < half, 1 - c, c)

    def body(a_ref, b_ref, o_ref, theirs_ref, acc_ref, stage_ref, got_ref, send_sems, recv_sems, load_sem):
        j, k = pl.program_id(0), pl.program_id(1)
        x, y, c = lax.axis_index("x"), lax.axis_index("y"), lax.axis_index("c")

        def to_sibling(q):
            return pltpu.make_async_remote_copy(src_ref=stage_ref, dst_ref=theirs_ref.at[q], send_sem=send_sems.at[q],
                                                recv_sem=recv_sems.at[q], device_id=(x, y, 1 - c), device_id_type=MESH)

        def fetch(q):
            return pltpu.make_async_copy(theirs_ref.at[q], got_ref, load_sem.at[0])

        @pl.when(k == 0)
        def _():
            acc_ref[...] = jnp.zeros_like(acc_ref)

        acc_ref[...] += lax.dot_general(a_ref[...], b_ref[...], TN_DIMS, preferred_element_type=F32)

        for q in range(half):
            @pl.when((j == q) & (k == nk - 1))
            def _(q=q):
                if q > 0:
                    to_sibling(q - 1).wait_send()
                stage_ref[...] = acc_ref[...].astype(BF16)
                to_sibling(q).start()

            @pl.when((j == half + q) & (k == 0))
            def _(q=q):
                to_sibling(q).wait_recv()
                fetch(q).start()

            @pl.when((j == half + q) & (k == nk - 1))
            def _(q=q):
                if q == 0:
                    to_sibling(half - 1).wait_send()
                fetch(q).wait()
                o_ref[...] = (acc_ref[...] + got_ref[...].astype(F32)).astype(BF16)

    in_specs = [pl.BlockSpec((None, tk, fs), lambda j, k: (device_of(j), k, 0)), pl.BlockSpec((tk, d), lambda j, k: (k, 0))]
    out_specs = [pl.BlockSpec((None, fs, d), lambda j, k: (jnp.maximum(j - half, 0), 0, 0)), ANY]
    shp = jax.ShapeDtypeStruct((half, fs, d), BF16)
    scratch = [pltpu.VMEM((fs, d), F32), pltpu.VMEM((fs, d), BF16), pltpu.VMEM((fs, d), BF16),
               pltpu.SemaphoreType.DMA((half,)), pltpu.SemaphoreType.DMA((half,)), pltpu.SemaphoreType.DMA((1,))]
    (sums, _), riders_out = _call(body, name, (nj, nk), in_specs, out_specs, [shp, shp], scratch, [hid3, db], rider)
    return sums, riders_out


def _dw_cols(ab, dz, name, rider=None):
    s, d = ab.shape
    n = dz.shape[1]
    tn = PROJ_IN_COLS if n % PROJ_IN_COLS == 0 else n
    tk = min(s, DW_TOKENS)
    tr = d // 2

    def body(a_ref, b_ref, o_ref, acc_ref):
        k = pl.program_id(2)

        @pl.when(k == 0)
        def _():
            acc_ref[...] = jnp.zeros_like(acc_ref)

        acc_ref[...] += lax.dot_general(a_ref[...], b_ref[...], TN_DIMS, preferred_element_type=F32)

        @pl.when(k == s // tk - 1)
        def _():
            o_ref[...] = acc_ref[...].astype(BF16)

    in_specs = [pl.BlockSpec((tk, tr), lambda j, r, k: (k, r)), pl.BlockSpec((tk, tn), lambda j, r, k: (k, j))]
    return _call(body, name, (n // tn, d // tr, s // tk), in_specs, pl.BlockSpec((tr, tn), lambda j, r, k: (r, j)),
                 jax.ShapeDtypeStruct((d, n), BF16), [pltpu.VMEM((tr, tn), F32)], [ab, dz], rider)


def _ffn_bwd_input(dg3, wg3, du3, wu3, dres, ln, name, rider=None):
    s, d = dres.shape
    nk, _, fs = dg3.shape
    tm = min(s, 512)

    def body(*refs):
        refs = list(refs)
        dg_in, wg_ref, du_in, wu_ref, dres_ref = refs[:5]
        del refs[:5]
        if ln is not None:
            xh_ref, rstd_ref, gain_ref = refs.pop(0), refs.pop(0), refs.pop(0)
        acc_ref = refs.pop()
        i, k = pl.program_id(0), pl.program_id(1)

        @pl.when(k == 0)
        def _():
            acc_ref[...] = jnp.zeros_like(acc_ref)

        acc_ref[...] += (jnp.dot(dg_in[...], wg_ref[...], preferred_element_type=F32)
                         + jnp.dot(du_in[...], wu_ref[...], preferred_element_type=F32))

        @pl.when(k == nk - 1)
        def _():
            if ln is not None:
                dr_ref, drb_ref, dg_ref, db_ref, sum_ref = refs

                @pl.when(i == 0)
                def _():
                    dg_ref[...] = jnp.zeros_like(dg_ref)
                    db_ref[...] = jnp.zeros_like(db_ref)
                    sum_ref[...] = jnp.zeros_like(sum_ref)

            for rows in _row_blocks(tm):
                dh = ALPHA * dres_ref[rows, :] + acc_ref[rows, :]
                if ln is None:
                    refs[0][rows, :] = dh
                else:
                    xhat = xh_ref[rows, :]
                    dg_ref[...] += jnp.sum(dh * xhat, axis=0, keepdims=True)
                    db_ref[...] += jnp.sum(dh, axis=0, keepdims=True)
                    dr = _ln_backward(dh, xhat, rstd_ref[rows, :], gain_ref[...])
                    sum_ref[...] += jnp.sum(dr, axis=0, keepdims=True)
                    dr_ref[rows, :] = dr
                    drb_ref[rows, :] = dr.astype(BF16)

    tok = pl.BlockSpec((tm, d), lambda i, k: (i, 0))
    vec = pl.BlockSpec((1, d), lambda i, k: (0, 0))
    a_spec = pl.BlockSpec((None, tm, fs), lambda i, k: (k, i, 0))
    w_spec = pl.BlockSpec((None, fs, d), lambda i, k: (k, 0, 0))
    ins, in_specs = [dg3, wg3, du3, wu3, dres], [a_spec, w_spec, a_spec, w_spec, tok]
    if ln is None:
        out_shape, out_specs = jax.ShapeDtypeStruct((s, d), F32), tok
    else:
        ins += list(ln)
        in_specs += [tok, pl.BlockSpec((tm, 1), lambda i, k: (i, 0)), vec]
        out_shape = [jax.ShapeDtypeStruct((s, d), F32), jax.ShapeDtypeStruct((s, d), BF16)] + [jax.ShapeDtypeStruct((1, d), F32)] * 3
        out_specs = [tok, tok, vec, vec, vec]
    return _call(body, name, (s // tm, nk), in_specs, out_specs, out_shape, [pltpu.VMEM((tm, d), F32)], ins, rider)


def _proj_in_bwd_ln(dz, w, dres, ln, branch_scale, name, rider=None):
    s, d = dres.shape
    n = w.shape[1]
    tm = min(s, 256)

    def body(dz_ref, w_hbm, dres_ref, xh_ref, rstd_ref, gain_ref, dr_ref, drb_ref, dg_ref, db_ref, w_ref, w_sem):
        @pl.when(pl.program_id(0) == 0)
        def _():
            whole = pltpu.make_async_copy(w_hbm, w_ref, w_sem.at[0])
            whole.start()
            whole.wait()
            dg_ref[...] = jnp.zeros_like(dg_ref)
            db_ref[...] = jnp.zeros_like(db_ref)

        acc = lax.dot_general(dz_ref[...], w_ref[...], NT_DIMS, preferred_element_type=F32)
        for rows in _row_blocks(tm):
            dh = ALPHA * dres_ref[rows, :] + acc[rows]
            xhat = xh_ref[rows, :]
            dg_ref[...] += jnp.sum(dh * xhat, axis=0, keepdims=True)
            db_ref[...] += jnp.sum(dh, axis=0, keepdims=True)
            dr = _ln_backward(dh, xhat, rstd_ref[rows, :], gain_ref[...])
            dr_ref[rows, :] = dr
            drb_ref[rows, :] = (branch_scale * dr).astype(BF16)

    tok = pl.BlockSpec((tm, d), lambda i: (i, 0))
    vec = pl.BlockSpec((1, d), lambda i: (0, 0))
    in_specs = [pl.BlockSpec((tm, n), lambda i: (i, 0)), ANY, tok, tok, pl.BlockSpec((tm, 1), lambda i: (i, 0)), vec]
    out_shape = [jax.ShapeDtypeStruct((s, d), F32), jax.ShapeDtypeStruct((s, d), BF16)] + [jax.ShapeDtypeStruct((1, d), F32)] * 2
    scratch = [pltpu.VMEM(w.shape, BF16), pltpu.SemaphoreType.DMA((1,))]
    return _call(body, name, (s // tm,), in_specs, [tok, tok, vec, vec], out_shape, scratch, [dz, w, dres] + list(ln), rider)


def _shift_rows_down(v, halo, k, row):
    out = pltpu.roll(v, k, 0)
    hr = halo.shape[0]
    for r in range(k):
        out = jnp.where(row == r, halo[hr - k + r:hr - k + r + 1, :], out)
    return out


def _shift_rows_up(v, halo, k, row):
    t = v.shape[0]
    out = pltpu.roll(v, t - k, 0)
    for r in range(k):
        out = jnp.where(row == t - k + r, halo[r:r + 1, :], out)
    return out


def _sgu_head_forward(z_ref, h, da, gv_ref, bv_ref):
    zu = z_ref[:, h * HEAD:(h + 1) * HEAD].astype(F32)
    zv = z_ref[:, da + h * HEAD:da + (h + 1) * HEAD].astype(F32)
    tu, tv = _gelu_tanh(zu), _gelu_tanh(zv)
    u = _gelu(zu, tu)
    v = _gelu(zv, tv)
    mu = jnp.mean(v, axis=-1, keepdims=True)
    c = v - mu
    rstd = lax.rsqrt(jnp.mean(c * c, axis=-1, keepdims=True) + LN_EPS)
    vhat = c * rstd
    vln = (vhat * gv_ref[h:h + 1, :] + bv_ref[h:h + 1, :]).astype(BF16)
    return (zu, tu), (zv, tv), u, vhat, rstd, vln


def _mixer_fwd(z, ws_masked, bs_wide, gv, bv, cw, name):
    s, zc = z.shape
    da = zc // 5
    nh = da // HEAD
    tm = min(s, MIXER_TOKENS)
    hb = tm // BF16_ROWS

    def body(z_ref, pc_ref, px_ref, ws_ref, bs_ref, gv_ref, bv_ref, cw_ref, y_ref):
        i = pl.program_id(0)
        for h in range(nh):
            _, _, u, _, _, vln = _sgu_head_forward(z_ref, h, da, gv_ref, bv_ref)
            for n in range(tm // CHUNK):
                rows = slice(n * CHUNK, (n + 1) * CHUNK)
                mixed = jnp.dot(ws_ref[h], vln[rows], preferred_element_type=F32) + bs_ref[h]
                y_ref[0, rows, h * HEAD:(h + 1) * HEAD] = (u[rows] * mixed).astype(BF16)
        gate_b = z_ref[:, 2 * da:3 * da].astype(F32)
        hc = z_ref[:, 3 * da:4 * da].astype(F32) * z_ref[:, 4 * da:5 * da].astype(F32)
        halo = jnp.where(i > 0, pc_ref[...].astype(F32) * px_ref[...].astype(F32), 0.0)
        row = lax.broadcasted_iota(jnp.int32, (tm, da), 0)
        y = cw_ref[0:1, :] * _shift_rows_down(hc, halo, 2, row) + cw_ref[1:2, :] * _shift_rows_down(hc, halo, 1, row) + cw_ref[2:3, :] * hc
        y_ref[1] = (gate_b * y).astype(BF16)

    prev = lambda col: pl.BlockSpec((BF16_ROWS, da), lambda i: (jnp.maximum(i * hb - 1, 0), col))
    in_specs = [pl.BlockSpec((tm, zc), lambda i: (i, 0)), prev(3), prev(4), _row((nh, CHUNK, CHUNK)), _row((nh, CHUNK, HEAD)),
                _row((nh, HEAD)), _row((nh, HEAD)), _row((CONV_TAPS, da))]
    return _call(body, name, (s // tm,), in_specs, pl.BlockSpec((2, tm, da), lambda i: (0, i, 0)),
                 jax.ShapeDtypeStruct((2, s, da), BF16), [], [z, z, z, ws_masked, bs_wide, gv, bv, cw])[0]


def _mixer_bwd(z, dy, ws_masked, bs_wide, gv, bv, cw, name, rider=None):
    s, zc = z.shape
    da = zc // 5
    nh = da // HEAD
    tm = min(s, MIXER_TOKENS)
    hb = tm // BF16_ROWS
    nblk = s // tm

    def body(z_ref, pc_ref, px_ref, nb_ref, dy_ref, ndy_ref, ws_ref, bs_ref, gv_ref, bv_ref, cw_ref,
             dz_ref, dws_ref, dbs_ref, dgv_ref, dbv_ref, dcw_ref, dbin_ref):
        i = pl.program_id(0)

        @pl.when(i == 0)
        def _():
            for ref in (dws_ref, dbs_ref, dgv_ref, dbv_ref, dcw_ref, dbin_ref):
                ref[...] = jnp.zeros_like(ref)

        causal = lax.broadcasted_iota(jnp.int32, (CHUNK, CHUNK), 0) >= lax.broadcasted_iota(jnp.int32, (CHUNK, CHUNK), 1)
        for h in range(nh):
            zu, zv, u, vhat, rstd, vln = _sgu_head_forward(z_ref, h, da, gv_ref, bv_ref)
            dya = dy_ref[0, :, h * HEAD:(h + 1) * HEAD].astype(F32)
            w = ws_ref[h]
            du_parts, dvln_parts = [], []
            for n in range(tm // CHUNK):
                rows = slice(n * CHUNK, (n + 1) * CHUNK)
                mixed = jnp.dot(w, vln[rows], preferred_element_type=F32) + bs_ref[h]
                du_parts.append(dya[rows] * mixed)
                dmix = dya[rows] * u[rows]
                dmix_b = dmix.astype(BF16)
                dws_ref[h] += jnp.where(causal, lax.dot_general(dmix_b, vln[rows], NT_DIMS, preferred_element_type=F32), 0.0)
                dbs_ref[h] += dmix
                dvln_parts.append(lax.dot_general(w, dmix_b, TN_DIMS, preferred_element_type=F32))
            du = jnp.concatenate(du_parts, axis=0)
            dvln = jnp.concatenate(dvln_parts, axis=0)
            dgv_ref[h:h + 1, :] += jnp.sum(dvln * vhat, axis=0, keepdims=True)
            dbv_ref[h:h + 1, :] += jnp.sum(dvln, axis=0, keepdims=True)
            dv = _ln_backward(dvln, vhat, rstd, gv_ref[h:h + 1, :])
            dzu = du * _gelu_grad(*zu)
            dzv = dv * _gelu_grad(*zv)
            ucols = slice(h * HEAD, (h + 1) * HEAD)
            vcols = slice(da + h * HEAD, da + (h + 1) * HEAD)
            dz_ref[:, ucols] = dzu.astype(BF16)
            dz_ref[:, vcols] = dzv.astype(BF16)
            dbin_ref[:, ucols] += jnp.sum(dzu, axis=0, keepdims=True)
            dbin_ref[:, vcols] += jnp.sum(dzv, axis=0, keepdims=True)

        gate_b = z_ref[:, 2 * da:3 * da].astype(F32)
        gate_c = z_ref[:, 3 * da:4 * da].astype(F32)
        xt = z_ref[:, 4 * da:5 * da].astype(F32)
        hc = gate_c * xt
        halo = jnp.where(i > 0, pc_ref[...].astype(F32) * px_ref[...].astype(F32), 0.0)
        row = lax.broadcasted_iota(jnp.int32, (tm, da), 0)
        sh1 = _shift_rows_down(hc, halo, 1, row)
        sh2 = _shift_rows_down(hc, halo, 2, row)
        y = cw_ref[0:1, :] * sh2 + cw_ref[1:2, :] * sh1 + cw_ref[2:3, :] * hc
        dyb = dy_ref[1].astype(F32)
        dconv = dyb * gate_b
        nhalo = jnp.where(i < nblk - 1, ndy_ref[...].astype(F32) * nb_ref[...].astype(F32), 0.0)
        dhc = cw_ref[2:3, :] * dconv + cw_ref[1:2, :] * _shift_rows_up(dconv, nhalo, 1, row) + cw_ref[0:1, :] * _shift_rows_up(dconv, nhalo, 2, row)
        dcw_ref[0:1, :] += jnp.sum(dconv * sh2, axis=0, keepdims=True)
        dcw_ref[1:2, :] += jnp.sum(dconv * sh1, axis=0, keepdims=True)
        dcw_ref[2:3, :] += jnp.sum(dconv * hc, axis=0, keepdims=True)
        for col, val in ((2, dyb * y), (3, dhc * xt), (4, dhc * gate_c)):
            cols = slice(col * da, (col + 1) * da)
            dz_ref[:, cols] = val.astype(BF16)
            dbin_ref[:, cols] += jnp.sum(val, axis=0, keepdims=True)

        @pl.when(i == nblk - 1)
        def _():
            for h in range(nh):
                dbs_ref[h] = jnp.broadcast_to(jnp.sum(dbs_ref[h], axis=1, keepdims=True), (CHUNK, HEAD))

    prev = lambda col: pl.BlockSpec((BF16_ROWS, da), lambda i: (jnp.maximum(i * hb - 1, 0), col))
    nxt = lambda i: jnp.minimum((i + 1) * hb, s // BF16_ROWS - 1)
    in_specs = [pl.BlockSpec((tm, zc), lambda i: (i, 0)), prev(3), prev(4), pl.BlockSpec((BF16_ROWS, da), lambda i: (nxt(i), 2)),
                pl.BlockSpec((2, tm, da), lambda i: (0, i, 0)), pl.BlockSpec((None, BF16_ROWS, da), lambda i: (1, nxt(i), 0)),
                _row((nh, CHUNK, CHUNK)), _row((nh, CHUNK, HEAD)), _row((nh, HEAD)), _row((nh, HEAD)), _row((CONV_TAPS, da))]
    out_specs = [pl.BlockSpec((tm, zc), lambda i: (i, 0)), _row((nh, CHUNK, CHUNK)), _row((nh, CHUNK, HEAD)), _row((nh, HEAD)),
                 _row((nh, HEAD)), _row((8, da)), _row((1, zc))]
    out_shape = [jax.ShapeDtypeStruct((s, zc), BF16), jax.ShapeDtypeStruct((nh, CHUNK, CHUNK), F32),
                 jax.ShapeDtypeStruct((nh, CHUNK, HEAD), F32), jax.ShapeDtypeStruct((nh, HEAD), F32),
                 jax.ShapeDtypeStruct((nh, HEAD), F32), jax.ShapeDtypeStruct((8, da), F32), jax.ShapeDtypeStruct((1, zc), F32)]
    return _call(body, name, (nblk,), in_specs, out_specs, out_shape, [], [z, z, z, z, dy, dy, ws_masked, bs_wide, gv, bv, cw], rider)


def _adam_update(g, w, m, v):
    m_new = ADAM_B1 * m + (1.0 - ADAM_B1) * g
    v_new = ADAM_B2 * v + (1.0 - ADAM_B2) * (g * g)
    m_hat = m_new / (1.0 - ADAM_B1 ** ADAM_STEP)
    v_hat = v_new / (1.0 - ADAM_B2 ** ADAM_STEP)
    return -ADAM_LR * (m_hat / (jnp.sqrt(v_hat) + ADAM_EPS) + ADAM_WD * w), m_new, v_new


def _adamw(gparts, w, m, v, name):
    n, r, c = gparts.shape
    tr = r // 4 if (r // 4) % BF16_ROWS == 0 else r

    def body(g_ref, w_ref, m_ref, v_ref, go_ref, d_ref, mo_ref, vo_ref):
        g = g_ref[0].astype(F32)
        for q in range(1, n):
            g = g + g_ref[q].astype(F32)
        go_ref[...] = g
        d_ref[...], mo_ref[...], vo_ref[...] = _adam_update(g, w_ref[...], m_ref[...], v_ref[...])

    blk = pl.BlockSpec((tr, c), lambda i: (i, 0))
    shp = jax.ShapeDtypeStruct((r, c), F32)
    return _call(body, name, (r // tr,), [pl.BlockSpec((n, tr, c), lambda i: (0, i, 0)), blk, blk, blk], [blk] * 4, [shp] * 4, [],
                 [gparts, w, m, v])[0]


def _adamw_small(packs, rows, w, m, v, conv, loss_scale, name):
    n_par, n_dev = len(rows), packs.shape[0]
    taps = conv[0].shape[0]

    def body(*refs):
        refs = list(refs)
        cut = lambda n: [refs.pop(0) for _ in range(n)]
        p_ref, w_refs, m_refs, v_refs, (cw_ref, cm_ref, cv_ref) = refs.pop(0), cut(n_par), cut(n_par), cut(n_par), cut(3)
        outs = [cut(4) for _ in range(n_par + 1)]
        loss_ref = refs.pop(0)
        at = 0
        for k in range(n_par):
            g = p_ref[0, at:at + rows[k], :]
            for dev in range(1, n_dev):
                g = g + p_ref[dev, at:at + rows[k], :]
            go_ref, d_ref, mo_ref, vo_ref = outs[k]
            go_ref[...] = g
            d_ref[...], mo_ref[...], vo_ref[...] = _adam_update(g, w_refs[k][...], m_refs[k][...], v_refs[k][...])
            at += rows[k]
        me = 4 * lax.axis_index("x") + 2 * lax.axis_index("y") + lax.axis_index("c")
        go_ref, d_ref, mo_ref, vo_ref = outs[n_par]
        for tap in range(taps):
            row = pl.ds(at + tap * n_dev + me, 1)
            g = p_ref[0, row, :]
            for dev in range(1, n_dev):
                g = g + p_ref[dev, row, :]
            one = slice(tap, tap + 1)
            go_ref[one, :] = g
            d_ref[one, :], mo_ref[one, :], vo_ref[one, :] = _adam_update(g, cw_ref[one, :], cm_ref[one, :], cv_ref[one, :])
        at += taps * n_dev
        sq = p_ref[0, at:, :]
        for dev in range(1, n_dev):
            sq = sq + p_ref[dev, at:, :]
        total = jnp.sum(jnp.sum(sq, axis=0, keepdims=True), axis=1, keepdims=True)
        loss_ref[...] = jnp.broadcast_to(loss_scale * total, loss_ref.shape)

    vmem = pl.BlockSpec(memory_space=pltpu.VMEM)
    ins = [packs] + list(w) + list(m) + list(v) + list(conv)
    out_shape = [jax.ShapeDtypeStruct(a.shape, F32) for a in list(w) + [conv[0]] for _ in range(4)] + [jax.ShapeDtypeStruct((8, LANES), F32)]
    outs = pl.pallas_call(body, name=name, in_specs=[vmem] * len(ins), out_specs=[vmem] * len(out_shape), out_shape=out_shape,
                          compiler_params=pltpu.CompilerParams(vmem_limit_bytes=V7X_VMEM_LIMIT))(*ins)
    return [outs[4 * k:4 * k + 4] for k in range(n_par + 1)], outs[-1][0, 0]


def _rows128(a):
    return a.reshape(-1, LANES)


def kernel(x, ffa_gate, ffa_up, ffa_down, ln_a_g, ln_a_b, w_in, b_in, w_s, b_s, ln_v_g, ln_v_b, conv_w, w_out, b_out, ln_m_g, ln_m_b, ffc_gate, ffc_up, ffc_down, ln_c_g, ln_c_b, loss_target, m_ffa_gate, m_ffa_up, m_ffa_down, m_ln_a_g, m_ln_a_b, m_w_in, m_b_in, m_w_s, m_b_s, m_ln_v_g, m_ln_v_b, m_conv_w, m_w_out, m_b_out, m_ln_m_g, m_ln_m_b, m_ffc_gate, m_ffc_up, m_ffc_down, m_ln_c_g, m_ln_c_b, v_ffa_gate, v_ffa_up, v_ffa_down, v_ln_a_g, v_ln_a_b, v_w_in, v_b_in, v_w_s, v_b_s, v_ln_v_g, v_ln_v_b, v_conv_w, v_w_out, v_b_out, v_ln_m_g, v_ln_m_b, v_ffc_gate, v_ffc_up, v_ffc_down, v_ln_c_g, v_ln_c_b):
    x2, target = x[0], loss_target[0]
    s, d = x2.shape
    da = d // 2
    nh = da // HEAD

    big = dict(ffa_gate=ffa_gate, ffa_up=ffa_up, ffa_down=ffa_down, w_in=w_in, w_out=w_out, ffc_gate=ffc_gate, ffc_up=ffc_up, ffc_down=ffc_down)
    big_m = dict(ffa_gate=m_ffa_gate, ffa_up=m_ffa_up, ffa_down=m_ffa_down, w_in=m_w_in, w_out=m_w_out, ffc_gate=m_ffc_gate, ffc_up=m_ffc_up, ffc_down=m_ffc_down)
    big_v = dict(ffa_gate=v_ffa_gate, ffa_up=v_ffa_up, ffa_down=v_ffa_down, w_in=v_w_in, w_out=v_w_out, ffc_gate=v_ffc_gate, ffc_up=v_ffc_up, ffc_down=v_ffc_down)
    local = lambda k, a: jnp.transpose(a[0]) if k in TRANSPOSED else a[0]
    shard = {k: local(k, w).astype(BF16) for k, w in big.items()}
    conv_rows = jnp.pad(conv_w[0], ((0, 8 - CONV_TAPS), (0, 0)))

    tril = jnp.tril(jnp.ones((CHUNK, CHUNK), dtype=bool))
    ws_masked = jnp.where(tril[None], w_s[0], 0.0).astype(BF16)
    bs_wide = jnp.broadcast_to(b_s[0][:, :, None], (nh, CHUNK, HEAD))
    gv, bv = ln_v_g.reshape(nh, HEAD), ln_v_b.reshape(nh, HEAD)

    full = {}
    xb = x2.astype(BF16)
    (silu_a, udsilu_a, hid_a), (full["ffa_gate"], full["ffa_up"], full["ffa_down"]) = _ffn_gateup_gathering(
        xb, shard["ffa_gate"], shard["ffa_up"], shard["ffa_down"], "ffa_gateup")
    (xhat1, h1b, rstd1), (full["w_in"], full["w_out"], conv_full) = _down_ln(
        hid_a, full["ffa_down"], None, x2, None, ln_a_g, ln_a_b, 0.5, "ffa_down_ln", rider=_Gather([shard["w_in"], shard["w_out"], conv_rows], by_columns=(0,)))
    cw = jnp.transpose(conv_full[:, :CONV_TAPS, :], (1, 0, 2)).reshape(CONV_TAPS, da)
    w_out2 = full["w_out"].reshape(2, da, d)
    z, (full["ffc_gate"],) = _proj_in(h1b, full["w_in"], b_in, "proj_in", _Gather([shard["ffc_gate"]]))
    ycat = _mixer_fwd(z, ws_masked, bs_wide, gv, bv, cw, "mixer_fwd")
    (xhat2, h2b, rstd2), _ = _down_ln(ycat, w_out2, b_out, xhat1, (ln_a_g, ln_a_b), ln_m_g, ln_m_b, 1.0, "proj_out_ln")
    (silu_c, udsilu_c, hid_c), (full["ffc_up"], full["ffc_down"]) = _ffn_gateup_gathering(
        h2b, full["ffc_gate"], shard["ffc_up"], shard["ffc_down"], "ffc_gateup")
    (dr3, dr3b, sq_err, d_ln_c_g, d_ln_c_b), _ = _down_ln(hid_c, full["ffc_down"], None, xhat2, (ln_m_g, ln_m_b), ln_c_g, ln_c_b, 0.5,
                                                          "ffc_down_ln_loss", target=target)

    landed = {}
    (dg_c, du_c), _ = _ffn_bwd_hidden(dr3b, full["ffc_down"], silu_c, udsilu_c, "ffc_bwd_hidden")
    part, _ = _dw_hidden_rows_paired(hid_c, dr3b, "ffc_dw_down")
    part, (landed["ffc_down"],) = _dw_hidden_rows_paired(dg_c, h2b, "ffc_dw_gate", _ChipScatter(part))
    part, (landed["ffc_gate"],) = _dw_hidden_rows_paired(du_c, h2b, "ffc_dw_up", _ChipScatter(part))
    (dr2, dr2b, d_ln_m_g, d_ln_m_b, d_b_out), (landed["ffc_up"],) = _ffn_bwd_input(
        dg_c, full["ffc_gate"], du_c, full["ffc_up"], dr3, (xhat2, rstd2, ln_m_g), "ffc_bwd_input_ln", _ChipScatter(part))
    updated = {}
    adam_side = lambda k: _AdamSide(landed[k], local(k, big[k]), local(k, big_m[k]), local(k, big_v[k]))
    dycat, (updated["ffc_down"], updated["ffc_up"]) = _nt_hidden(dr2b, w_out2, "proj_out_bwd", [adam_side("ffc_down"), adam_side("ffc_up")])
    part, _ = _dw_hidden_rows(ycat, dr2b, "proj_out_dw")
    (dz, d_w_s, d_b_s_wide, d_gv, d_bv, d_cw, d_b_in), (landed["w_out"],) = _mixer_bwd(
        z, dycat, ws_masked, bs_wide, gv, bv, cw, "mixer_bwd", _Scatter(part.reshape(N_DEV, d // N_DEV, d)))
    (dr1, dr1b, d_ln_a_g, d_ln_a_b), updated["ffc_gate"] = _proj_in_bwd_ln(
        dz, full["w_in"], dr2, (xhat1, rstd1, ln_a_g), 0.5, "proj_in_bwd_ln", adam_side("ffc_gate"))
    small_g = dict(ln_a_g=d_ln_a_g, ln_a_b=d_ln_a_b, b_in=d_b_in, w_s=d_w_s, b_s=d_b_s_wide[:, :, 0], ln_v_g=d_gv, ln_v_b=d_bv, b_out=d_b_out,
                   ln_m_g=d_ln_m_g, ln_m_b=d_ln_m_b, ln_c_g=d_ln_c_g, ln_c_b=d_ln_c_b)
    pack = jnp.concatenate([_rows128(g) for g in small_g.values()] + [_rows128(d_cw[:CONV_TAPS]), _rows128(sq_err)], axis=0)
    part, (packs,) = _dw_cols(h1b, dz, "proj_in_dw", _Scatter(pack, whole=True))
    (dg_a, du_a), (landed["w_in"],) = _ffn_bwd_hidden(dr1b, full["ffa_down"], silu_a, udsilu_a, "ffa_bwd_hidden", _Scatter(part, by_columns=True))
    part, _ = _dw_hidden_rows_paired(hid_a, dr1b, "ffa_dw_down")
    part, (landed["ffa_down"],) = _dw_hidden_rows_paired(dg_a, xb, "ffa_dw_gate", _ChipScatter(part))
    part, (landed["ffa_gate"],) = _dw_hidden_rows_paired(du_a, xb, "ffa_dw_up", _ChipScatter(part))
    grad_x, (landed["ffa_up"],) = _ffn_bwd_input(dg_a, full["ffa_gate"], du_a, full["ffa_up"], dr1, None, "ffa_bwd_input", _ChipScatter(part))

    grads, deltas, new_m, new_v = {}, {}, {}, {}
    for k in big:
        out = updated[k] if k in updated else _adamw(landed[k], local(k, big[k]), local(k, big_m[k]), local(k, big_v[k]), "adamw_" + k)
        grads[k], deltas[k], new_m[k], new_v[k] = ((jnp.transpose(o) if k in TRANSPOSED else o).reshape(big[k].shape) for o in out)

    small = dict(ln_a_g=ln_a_g, ln_a_b=ln_a_b, b_in=b_in, w_s=w_s, b_s=b_s, ln_v_g=ln_v_g, ln_v_b=ln_v_b, b_out=b_out,
                 ln_m_g=ln_m_g, ln_m_b=ln_m_b, ln_c_g=ln_c_g, ln_c_b=ln_c_b)
    small_m = dict(ln_a_g=m_ln_a_g, ln_a_b=m_ln_a_b, b_in=m_b_in, w_s=m_w_s, b_s=m_b_s, ln_v_g=m_ln_v_g, ln_v_b=m_ln_v_b, b_out=m_b_out,
                   ln_m_g=m_ln_m_g, ln_m_b=m_ln_m_b, ln_c_g=m_ln_c_g, ln_c_b=m_ln_c_b)
    small_v = dict(ln_a_g=v_ln_a_g, ln_a_b=v_ln_a_b, b_in=v_b_in, w_s=v_w_s, b_s=v_b_s, ln_v_g=v_ln_v_g, ln_v_b=v_ln_v_b, b_out=v_b_out,
                   ln_m_g=v_ln_m_g, ln_m_b=v_ln_m_b, ln_c_g=v_ln_c_g, ln_c_b=v_ln_c_b)
    snames = list(small)
    assert snames == list(small_g) and conv_w.shape[2] == LANES and da == N_DEV * LANES
    views = lambda tree: [_rows128(tree[k]) for k in snames]
    out, loss = _adamw_small(packs, [a.shape[0] for a in views(small)], views(small), views(small_m), views(small_v),
                             (conv_w[0], m_conv_w[0], v_conv_w[0]), 0.5 / d, "adamw_small")
    for k, per_param in zip(snames + ["conv_w"], out):
        shape = conv_w.shape if k == "conv_w" else small[k].shape
        grads[k], deltas[k], new_m[k], new_v[k] = (o.reshape(shape) for o in per_param)

    order = ["ffa_gate", "ffa_up", "ffa_down", "ln_a_g", "ln_a_b", "w_in", "b_in", "w_s", "b_s", "ln_v_g", "ln_v_b", "conv_w", "w_out", "b_out",
             "ln_m_g", "ln_m_b", "ffc_gate", "ffc_up", "ffc_down", "ln_c_g", "ln_c_b"]
    return (loss, grad_x[None], *[grads[k] for k in order], *[deltas[k] for k in order], *[new_m[k] for k in order], *[new_v[k] for k in order])
```

```python
import math

import jax
import jax.numpy as jnp
from jax import lax
from jax.experimental import pallas as pl
from jax.experimental.pallas import tpu as pltpu

BF16 = jnp.bfloat16
F32 = jnp.float32
MESH = pl.DeviceIdType.MESH

N_DEV = 8
HEAD = 128
CHUNK = 128
CONV_TAPS = 3
LN_EPS = 1e-5
ALPHA = float(2 ** 0.25)
GELU_C = 0.7978845608028654
GELU_A = 0.044715
ADAM_LR, ADAM_B1, ADAM_B2, ADAM_EPS, ADAM_WD, ADAM_STEP = 0.001, 0.9, 0.999, 1e-08, 0.01, 10
V7X_VMEM_LIMIT = 56 * 1024 * 1024
LANES = 128
BF16_ROWS = 16
MXU_COLS = 256
TRANSPOSED = ("ffa_gate", "ffa_up", "ffc_gate", "ffc_up")
PROJ_IN_COLS = 1280
MIXER_TOKENS = 512
ADAM_SIDE_ROWS = 64
DW_TOKENS = 2048

NT_DIMS = (((1,), (1,)), ((), ()))
TN_DIMS = (((0,), (0,)), ((), ()))
ANY = pl.BlockSpec(memory_space=pl.ANY)


def _gelu_tanh(x):
    return jnp.tanh(GELU_C * (x + GELU_A * x * x * x))


def _gelu(x, t):
    return 0.5 * x * (1.0 + t)


def _gelu_grad(x, t):
    return 0.5 * (1.0 + t) + 0.5 * x * (1.0 - t * t) * GELU_C * (1.0 + 3.0 * GELU_A * x * x)


def _sigmoid(x):
    return 0.5 * jnp.tanh(0.5 * x) + 0.5


def _row(shape):
    return pl.BlockSpec(shape, lambda *_: (0,) * len(shape))


def _row_blocks(tm, rows=128):
    rows = min(rows, tm)
    return [slice(r, r + rows) for r in range(0, tm, rows)]


def _ln_backward(dh, xhat, rstd, gain):
    dxh = dh * gain
    m1 = jnp.mean(dxh, axis=-1, keepdims=True)
    m2 = jnp.mean(dxh * xhat, axis=-1, keepdims=True)
    return rstd * (dxh - m1 - xhat * m2)


def _place():
    x, y, c = lax.axis_index("x"), lax.axis_index("y"), lax.axis_index("c")
    return x, y, c, [(1 - x, y), (x, 1 - y), (1 - x, 1 - y)]


def _other_devices(x, y, c):
    flips = [(bx, by, bc) for bx in (0, 1) for by in (0, 1) for bc in (0, 1)][1:]
    return [(1 - x if bx else x, 1 - y if by else y, 1 - c if bc else c) for bx, by, bc in flips]


class _Gather:
    def __init__(self, shards, forward_at=0.75, by_columns=()):
        n = len(shards)
        self.n, self.forward_at, self.by_columns = n, forward_at, tuple(by_columns)
        self.inputs = list(shards)
        self.out_shapes = [jax.ShapeDtypeStruct((a.shape[0], N_DEV * a.shape[1]) if i in self.by_columns else (N_DEV,) + a.shape, a.dtype)
                           for i, a in enumerate(shards)]
        self.scratch = [pltpu.SemaphoreType.DMA((n, 7)), pltpu.SemaphoreType.DMA((n, 7)), pltpu.SemaphoreType.DMA((n,))]

    def _block(self, outs, a, dev):
        if a in self.by_columns:
            cols = outs[a].shape[1] // N_DEV
            return outs[a].at[:, pl.ds(dev * cols, cols)]
        return outs[a].at[dev]

    def _copy(self, outs, sems, a, k, block, to, src=None):
        dst = self._block(outs, a, block)
        return pltpu.make_async_remote_copy(src_ref=dst if src is None else src, dst_ref=dst, send_sem=sems[0].at[a, k],
                                            recv_sem=sems[1].at[a, k], device_id=to, device_id_type=MESH)

    def start(self, ins, outs, sems, urgent=None):
        x, y, c, chips = _place()
        me = 4 * x + 2 * y + c
        for a in range(self.n):
            pltpu.make_async_copy(ins[a], self._block(outs, a, me), sems[2].at[a]).start()
        urgent = list(range(self.n)) if urgent is None else list(urgent)
        for group in (urgent, [a for a in range(self.n) if a not in urgent]):
            for a in group:
                self._copy(outs, sems, a, 0, me, (x, y, 1 - c), src=ins[a]).start()
                for j in (0, 1):
                    self._copy(outs, sems, a, 1 + j, me, (*chips[j], c), src=ins[a]).start()
            for a in group:
                self._copy(outs, sems, a, 3, me, (*chips[2], c), src=ins[a]).start()

    def wait_sibling(self, outs, sems, a):
        x, y, c, _ = _place()
        self._copy(outs, sems, a, 0, 4 * x + 2 * y + 1 - c, (x, y, 1 - c)).wait_recv()

    def pass_on(self, outs, sems, a, j):
        x, y, c, chips = _place()
        block = 4 * chips[j][0] + 2 * chips[j][1] + c
        self._copy(outs, sems, a, 1 + j, block, (x, y, 1 - c)).wait_recv()
        self._copy(outs, sems, a, 4 + j, block, (x, y, 1 - c)).start()

    def wait_passed(self, outs, sems, a, j):
        x, y, c, chips = _place()
        self._copy(outs, sems, a, 4 + j, 4 * chips[j][0] + 2 * chips[j][1] + 1 - c, (x, y, 1 - c)).wait_recv()

    def wait_sent(self, ins, outs, sems, a):
        x, y, c, _ = _place()
        me = 4 * x + 2 * y + c
        for k in range(7):
            self._copy(outs, sems, a, k, me, (x, y, 1 - c), src=ins[a]).wait_send()
        pltpu.make_async_copy(ins[a], self._block(outs, a, me), sems[2].at[a]).wait()

    def forward(self, ins, outs, sems):
        for a in range(self.n):
            for j in range(3):
                self.pass_on(outs, sems, a, j)

    def finish(self, ins, outs, sems):
        for a in range(self.n):
            self.wait_sibling(outs, sems, a)
            for j in range(3):
                self.wait_passed(outs, sems, a, j)
        for a in range(self.n):
            self.wait_sent(ins, outs, sems, a)

    def before(self, step, n_steps, ins, outs, sems):
        pl.when(step == 0)(lambda: self.start(ins, outs, sems))
        pl.when(step == int(self.forward_at * (n_steps - 1)))(lambda: self.forward(ins, outs, sems))

    def after(self, step, n_steps, ins, outs, sems):
        pl.when(step == n_steps - 1)(lambda: self.finish(ins, outs, sems))


class _Scatter:
    def __init__(self, partial, whole=False, by_columns=False):
        self.whole, self.by_columns = whole, by_columns
        self.inputs = [partial]
        if whole:
            shape = (N_DEV,) + partial.shape
        elif by_columns:
            shape = (N_DEV, partial.shape[0], partial.shape[1] // N_DEV)
        else:
            shape = partial.shape
        self.out_shapes = [jax.ShapeDtypeStruct(shape, partial.dtype)]
        self.scratch = [pltpu.SemaphoreType.DMA((7,)), pltpu.SemaphoreType.DMA((7,)), pltpu.SemaphoreType.DMA((1,))]

    def _copies(self, ins, outs, sems):
        x, y, c, _ = _place()
        me = 4 * x + 2 * y + c
        if self.whole:
            block = lambda dev: ins[0]
        elif self.by_columns:
            cols = ins[0].shape[1] // N_DEV
            block = lambda dev: ins[0].at[:, pl.ds(dev * cols, cols)]
        else:
            block = lambda dev: ins[0].at[dev]
        mine = pltpu.make_async_copy(block(me), outs[0].at[me], sems[2].at[0])
        remote = [pltpu.make_async_remote_copy(src_ref=block(4 * px + 2 * py + pc), dst_ref=outs[0].at[me], send_sem=sems[0].at[k],
                                               recv_sem=sems[1].at[k], device_id=(px, py, pc), device_id_type=MESH)
                  for k, (px, py, pc) in enumerate(_other_devices(x, y, c))]
        return mine, remote

    def start(self, ins, outs, sems):
        mine, remote = self._copies(ins, outs, sems)
        mine.start()
        for cp in remote:
            cp.start()

    def finish(self, ins, outs, sems):
        mine, remote = self._copies(ins, outs, sems)
        for cp in remote:
            cp.wait()
        mine.wait()

    def before(self, step, n_steps, ins, outs, sems):
        pl.when(step == 0)(lambda: self.start(ins, outs, sems))

    def after(self, step, n_steps, ins, outs, sems):
        pl.when(step == n_steps - 1)(lambda: self.finish(ins, outs, sems))


class _ChipScatter(_Scatter):
    def __init__(self, sums):
        super().__init__(sums)
        self.scratch = [pltpu.SemaphoreType.DMA((3,)), pltpu.SemaphoreType.DMA((3,)), pltpu.SemaphoreType.DMA((1,))]

    def _copies(self, ins, outs, sems):
        x, y, c, chips = _place()
        my_chip = 2 * x + y
        mine = pltpu.make_async_copy(ins[0].at[my_chip], outs[0].at[my_chip], sems[2].at[0])
        remote = [pltpu.make_async_remote_copy(src_ref=ins[0].at[2 * px + py], dst_ref=outs[0].at[my_chip], send_sem=sems[0].at[k],
                                               recv_sem=sems[1].at[k], device_id=(px, py, c), device_id_type=MESH)
                  for k, (px, py) in enumerate(chips)]
        return mine, remote


class _AdamSide:
    def __init__(self, gparts, w, m, v):
        self.n, self.r, self.c = gparts.shape
        self.inputs = [gparts, w, m, v]
        self.out_shapes = [jax.ShapeDtypeStruct((self.r, self.c), F32)] * 4
        self.scratch = []

    def _plan(self, n_steps):
        rows = ADAM_SIDE_ROWS
        while self.r // rows > n_steps:
            rows *= 2
        assert self.r % rows == 0
        return rows, self.r // rows

    def _rows(self, grid):
        rows, n_blocks = self._plan(math.prod(grid))

        def block(*ids):
            step = 0
            for size, pid in zip(grid, ids):
                step = step * size + pid
            return jnp.minimum(step, n_blocks - 1)
        return pl.BlockSpec((rows, self.c), lambda *ids: (block(*ids), 0)), block, rows

    def in_specs(self, grid):
        spec, block, rows = self._rows(grid)
        return [pl.BlockSpec((self.n, rows, self.c), lambda *ids: (0, block(*ids), 0)), spec, spec, spec]

    def out_specs(self, grid):
        return [self._rows(grid)[0]] * 4

    def before(self, step, n_steps, ins, outs, sems):
        @pl.when(step < self._plan(n_steps)[1])
        def _():
            g_ref, w_ref, m_ref, v_ref = ins
            g = g_ref[0].astype(F32)
            for q in range(1, self.n):
                g = g + g_ref[q].astype(F32)
            outs[0][...] = g
            outs[1][...], outs[2][...], outs[3][...] = _adam_update(g, w_ref[...], m_ref[...], v_ref[...])

    def after(self, step, n_steps, ins, outs, sems):
        pass


def _call(body, name, grid, in_specs, out_specs, out_shape, scratch, ins, rider=None):
    single = not isinstance(out_shape, (list, tuple))
    out_shape = [out_shape] if single else list(out_shape)
    out_specs = [out_specs] if single else list(out_specs)
    params = pltpu.CompilerParams(dimension_semantics=("arbitrary",) * len(grid), vmem_limit_bytes=V7X_VMEM_LIMIT)
    if rider is None:
        outs = pl.pallas_call(body, name=name, grid=grid, in_specs=in_specs, out_specs=out_specs, out_shape=out_shape,
                              scratch_shapes=scratch, compiler_params=params)(*ins)
        return (outs[0] if single else outs), None
    riders = list(rider) if isinstance(rider, (list, tuple)) else [rider]
    n_in, n_out, n_scr = len(ins), len(out_shape), len(scratch)
    n_steps = math.prod(grid)
    specs = lambda r, kind, count: getattr(r, kind)(grid) if hasattr(r, kind) else [ANY] * count

    def carried(*refs):
        refs = list(refs)
        cut = lambda n: [refs.pop(0) for _ in range(n)]
        b_in, r_in = cut(n_in), [cut(len(r.inputs)) for r in riders]
        b_out, r_out = cut(n_out), [cut(len(r.out_shapes)) for r in riders]
        b_scr, r_scr = cut(n_scr), [cut(len(r.scratch)) for r in riders]
        step = 0
        for axis, size in enumerate(grid):
            step = step * size + pl.program_id(axis)
        for r, r_ins, r_outs, r_sems in zip(riders, r_in, r_out, r_scr):
            r.before(step, n_steps, r_ins, r_outs, r_sems)
        body(*b_in, *b_out, *b_scr)
        for r, r_ins, r_outs, r_sems in zip(riders, r_in, r_out, r_scr):
            r.after(step, n_steps, r_ins, r_outs, r_sems)

    outs = pl.pallas_call(
        carried, name=name, grid=grid,
        in_specs=list(in_specs) + [sp for r in riders for sp in specs(r, "in_specs", len(r.inputs))],
        out_specs=out_specs + [sp for r in riders for sp in specs(r, "out_specs", len(r.out_shapes))],
        out_shape=out_shape + [sh for r in riders for sh in r.out_shapes],
        scratch_shapes=list(scratch) + [sc for r in riders for sc in r.scratch], compiler_params=params,
    )(*ins, *[a for r in riders for a in r.inputs])
    base, rest = outs[:n_out], list(outs[n_out:])
    per_rider = [[rest.pop(0) for _ in r.out_shapes] for r in riders]
    return (base[0] if single else base), (per_rider if isinstance(rider, (list, tuple)) else per_rider[0])


def _arrival_block(j):
    x, y, c = lax.axis_index("x"), lax.axis_index("y"), lax.axis_index("c")
    chip, other_core = j // 2, j % 2
    px = jnp.where((chip == 1) | (chip == 3), 1 - x, x)
    py = jnp.where((chip == 2) | (chip == 3), 1 - y, y)
    pc = jnp.where(other_core == 1, 1 - c, c)
    return 4 * px + 2 * py + pc


def _ffn_gateup_gathering(xb, gate, up_shard, down_shard, name):
    s, d = xb.shape
    fs = up_shard.shape[0]
    tm = min(s, 1024)
    ni = s // tm
    ask_at = max(ni - 2, 0)
    gate_here = gate.ndim == 2
    gather = _Gather(([gate] if gate_here else []) + [up_shard, down_shard])
    n_g = gather.n
    used_here, down = tuple(range(n_g - 1)), n_g - 1

    def body(x_ref, *refs):
        refs = list(refs)
        gate_full = None if gate_here else refs.pop(0)
        shards = [refs.pop(0) for _ in range(n_g)]
        silu_ref, udsilu_ref, h_ref = refs.pop(0), refs.pop(0), refs.pop(0)
        fulls = [refs.pop(0) for _ in range(n_g)]
        w_ref, w_sems = refs.pop(0), refs.pop(0)
        sems = refs
        j, i = pl.program_id(0), pl.program_id(1)
        gate_src, up_src = (fulls[0], fulls[1]) if gate_here else (gate_full, fulls[0])

        def load(slot, srcs):
            return [pltpu.make_async_copy(src, w_ref.at[slot, a], w_sems.at[slot, a]) for a, src in enumerate(srcs)]

        @pl.when((j == 0) & (i == 0))
        def _():
            gather.start(shards, fulls, sems, urgent=used_here)
            mine = load(0, (shards[0] if gate_here else gate_full.at[_arrival_block(0)], shards[n_g - 2]))
            for cp in mine:
                cp.start()
            for cp in mine:
                cp.wait()

        for nxt in range(1, N_DEV):
            @pl.when((j == nxt) & (i == 0))
            def _(nxt=nxt):
                for cp in load(nxt % 2, (gate_src.at[0], up_src.at[0])):
                    cp.wait()

        for nxt in range(1, N_DEV):
            @pl.when((j == nxt - 1) & (i == ask_at))
            def _(nxt=nxt):
                for a in used_here:
                    if nxt == 1:
                        gather.wait_sibling(fulls, sems, a)
                    elif nxt % 2 == 0:
                        gather.pass_on(fulls, sems, a, nxt // 2 - 1)
                    else:
                        gather.wait_passed(fulls, sems, a, nxt // 2 - 1)
                block = _arrival_block(nxt)
                for cp in load(nxt % 2, (gate_src.at[block], up_src.at[block])):
                    cp.start()

        @pl.when((j == N_DEV - 1) & (i == ask_at))
        def _():
            for other_chip in range(3):
                gather.pass_on(fulls, sems, down, other_chip)

        def shard_step(slot):
            x = x_ref[...]
            g = lax.dot_general(x, w_ref[slot, 0], NT_DIMS, preferred_element_type=F32)
            u = lax.dot_general(x, w_ref[slot, 1], NT_DIMS, preferred_element_type=F32)
            sg = _sigmoid(g)
            silu = g * sg
            silu_ref[...] = silu.astype(BF16)
            udsilu_ref[...] = (u * (sg + silu * (1.0 - sg))).astype(BF16)
            h_ref[...] = (silu * u).astype(BF16)

        pl.when(j % 2 == 0)(lambda: shard_step(0))
        pl.when(j % 2 == 1)(lambda: shard_step(1))

        @pl.when((j == N_DEV - 1) & (i == ni - 1))
        def _():
            gather.wait_sibling(fulls, sems, down)
            for other_chip in range(3):
                gather.wait_passed(fulls, sems, down, other_chip)
            for a in range(n_g):
                gather.wait_sent(shards, fulls, sems, a)

    shp = jax.ShapeDtypeStruct((N_DEV, s, fs), BF16)
    o_spec = pl.BlockSpec((None, tm, fs), lambda j, i: (_arrival_block(j), i, 0))
    ins = ([] if gate_here else [gate]) + gather.inputs
    outs = pl.pallas_call(
        body, name=name, grid=(N_DEV, ni), in_specs=[pl.BlockSpec((tm, d), lambda j, i: (i, 0))] + [ANY] * len(ins),
        out_specs=[o_spec, o_spec, o_spec] + [ANY] * n_g, out_shape=[shp, shp, shp] + gather.out_shapes,
        scratch_shapes=[pltpu.VMEM((2, 2, fs, d), BF16), pltpu.SemaphoreType.DMA((2, 2))] + gather.scratch,
        compiler_params=pltpu.CompilerParams(dimension_semantics=("arbitrary", "arbitrary"), vmem_limit_bytes=V7X_VMEM_LIMIT),
    )(xb, *ins)
    return outs[:3], outs[3:]


def _down_ln(a3, w3, bias, res, res_affine, ln_g, ln_b, scale, name, target=None, rider=None):
    nk, s, tk = a3.shape
    d = w3.shape[2]
    tm = min(s, 256)
    final = target is not None

    def body(*refs):
        refs = list(refs)
        a_ref, w_hbm = refs[:2]
        del refs[:2]
        bias_ref = refs.pop(0) if bias is not None else None
        res_ref = refs.pop(0)
        rg_ref, rb_ref = (refs.pop(0), refs.pop(0)) if res_affine is not None else (None, None)
        g_ref, b_ref = refs.pop(0), refs.pop(0)
        t_ref = refs.pop(0) if final else None
        w_sem = refs.pop()
        w_ref = refs.pop()
        i = pl.program_id(0)
        if final:
            dr_ref, drb_ref, sq_ref, dg_ref, db_ref = refs
        else:
            xh_ref, hb_ref, rstd_ref = refs

        @pl.when(i == 0)
        def _():
            whole = pltpu.make_async_copy(w_hbm, w_ref, w_sem.at[0])
            whole.start()
            whole.wait()
            if final:
                sq_ref[...] = jnp.zeros_like(sq_ref)
                dg_ref[...] = jnp.zeros_like(dg_ref)
                db_ref[...] = jnp.zeros_like(db_ref)

        y = jnp.dot(a_ref[0], w_ref[0], preferred_element_type=F32)
        for k in range(1, nk):
            y = y + jnp.dot(a_ref[k], w_ref[k], preferred_element_type=F32)
        if bias_ref is not None:
            y = y + bias_ref[...]
        for rows in _row_blocks(tm):
            r = res_ref[rows, :]
            if rg_ref is not None:
                r = r * rg_ref[...] + rb_ref[...]
            r = ALPHA * r + scale * y[rows]
            mu = jnp.mean(r, axis=-1, keepdims=True)
            c = r - mu
            var = jnp.mean(c * c, axis=-1, keepdims=True)
            rstd = lax.rsqrt(var + LN_EPS)
            xhat = c * rstd
            h = xhat * g_ref[...] + b_ref[...]
            if not final:
                xh_ref[rows, :] = xhat
                hb_ref[rows, :] = h.astype(BF16)
                rstd_ref[rows, :] = rstd
            else:
                err = h - t_ref[rows, :]
                sq_ref[...] += jnp.sum(err * err, axis=0, keepdims=True)
                dh = err * (1.0 / d)
                dg_ref[...] += jnp.sum(dh * xhat, axis=0, keepdims=True)
                db_ref[...] += jnp.sum(dh, axis=0, keepdims=True)
                dr = _ln_backward(dh, xhat, rstd, g_ref[...])
                dr_ref[rows, :] = dr
                drb_ref[rows, :] = (scale * dr).astype(BF16)

    tok = pl.BlockSpec((tm, d), lambda i: (i, 0))
    vec = pl.BlockSpec((1, d), lambda i: (0, 0))
    ins = [a3, w3]
    in_specs = [pl.BlockSpec((nk, tm, tk), lambda i: (0, i, 0)), ANY]
    if bias is not None:
        ins.append(bias)
        in_specs.append(vec)
    ins.append(res)
    in_specs.append(tok)
    if res_affine is not None:
        ins += list(res_affine)
        in_specs += [vec, vec]
    ins += [ln_g, ln_b]
    in_specs += [vec, vec]
    if final:
        ins.append(target)
        in_specs.append(tok)
        out_shape = [jax.ShapeDtypeStruct((s, d), F32), jax.ShapeDtypeStruct((s, d), BF16)] + [jax.ShapeDtypeStruct((1, d), F32)] * 3
        out_specs = [tok, tok, vec, vec, vec]
    else:
        out_shape = [jax.ShapeDtypeStruct((s, d), F32), jax.ShapeDtypeStruct((s, d), BF16), jax.ShapeDtypeStruct((s, 1), F32)]
        out_specs = [tok, tok, pl.BlockSpec((tm, 1), lambda i: (i, 0))]
    scratch = [pltpu.VMEM((nk, tk, d), BF16), pltpu.SemaphoreType.DMA((1,))]
    return _call(body, name, (s // tm,), in_specs, out_specs, out_shape, scratch, ins, rider)


def _proj_in(hb, w, bias, name, rider=None):
    s, d = hb.shape
    n = w.shape[1]
    tm = min(s, 1024)
    tn = PROJ_IN_COLS if n % PROJ_IN_COLS == 0 else n

    def body(h_ref, w_ref, b_ref, z_ref):
        z_ref[...] = (jnp.dot(h_ref[...], w_ref[...], preferred_element_type=F32) + b_ref[...]).astype(BF16)

    in_specs = [pl.BlockSpec((tm, d), lambda i, j: (i, 0)), pl.BlockSpec((d, tn), lambda i, j: (0, j)),
                pl.BlockSpec((1, tn), lambda i, j: (0, j))]
    return _call(body, name, (s // tm, n // tn), in_specs, pl.BlockSpec((tm, tn), lambda i, j: (i, j)),
                 jax.ShapeDtypeStruct((s, n), BF16), [], [hb, w, bias], rider)


def _nt_hidden(ab, w3, name, rider=None):
    s, kdim = ab.shape
    nj, tn, _ = w3.shape
    tm = min(s, 1024)

    def body(a_ref, w_ref, o_ref):
        o_ref[...] = lax.dot_general(a_ref[...], w_ref[...], NT_DIMS, preferred_element_type=F32).astype(BF16)

    in_specs = [pl.BlockSpec((tm, kdim), lambda i, j: (i, 0)), pl.BlockSpec((None, tn, kdim), lambda i, j: (j, 0, 0))]
    return _call(body, name, (s // tm, nj), in_specs, pl.BlockSpec((None, tm, tn), lambda i, j: (j, i, 0)),
                 jax.ShapeDtypeStruct((nj, s, tn), BF16), [], [ab, w3], rider)


def _ffn_bwd_hidden(ab, w3, silu3, udsilu3, name, rider=None):
    s, kdim = ab.shape
    nj, tn, _ = w3.shape
    tm = min(s, 1024)

    def body(a_ref, w_ref, silu_ref, udsilu_ref, dg_ref, du_ref):
        a = a_ref[...]
        for c0 in range(0, tn, MXU_COLS):
            cols = slice(c0, min(c0 + MXU_COLS, tn))
            t = lax.dot_general(a, w_ref[cols, :], NT_DIMS, preferred_element_type=F32)
            du_ref[:, cols] = (t * silu_ref[:, cols].astype(F32)).astype(BF16)
            dg_ref[:, cols] = (t * udsilu_ref[:, cols].astype(F32)).astype(BF16)

    hid = pl.BlockSpec((None, tm, tn), lambda i, j: (j, i, 0))
    shp = jax.ShapeDtypeStruct((nj, s, tn), BF16)
    in_specs = [pl.BlockSpec((tm, kdim), lambda i, j: (i, 0)), pl.BlockSpec((None, tn, kdim), lambda i, j: (j, 0, 0)), hid, hid]
    return _call(body, name, (s // tm, nj), in_specs, [hid, hid], [shp, shp], [], [ab, w3, silu3, udsilu3], rider)


def _tn_dw(a, a_spec, b, b_spec, nj, m, n, s, tk, name, rider):
    def body(a_ref, b_ref, o_ref, acc_ref):
        k = pl.program_id(1)

        @pl.when(k == 0)
        def _():
            acc_ref[...] = jnp.zeros_like(acc_ref)

        acc_ref[...] += lax.dot_general(a_ref[...], b_ref[...], TN_DIMS, preferred_element_type=F32)

        @pl.when(k == s // tk - 1)
        def _():
            o_ref[...] = acc_ref[...].astype(BF16)

    return _call(body, name, (nj, s // tk), [a_spec, b_spec], pl.BlockSpec((None, m, n), lambda j, k: (j, 0, 0)),
                 jax.ShapeDtypeStruct((nj, m, n), BF16), [pltpu.VMEM((m, n), F32)], [a, b], rider)


def _dw_hidden_rows(hid3, db, name, rider=None):
    nj, s, fs = hid3.shape
    d = db.shape[1]
    tk = min(s, DW_TOKENS)
    return _tn_dw(hid3, pl.BlockSpec((None, tk, fs), lambda j, k: (j, k, 0)), db, pl.BlockSpec((tk, d), lambda j, k: (k, 0)),
                  nj, fs, d, s, tk, name, rider)


def _dw_hidden_rows_paired(hid3, db, name, rider=None):
    nj, s, fs = hid3.shape
    d = db.shape[1]
    tk = min(s, DW_TOKENS)
    nk = s // tk
    half = nj // 2

    def device_of(j):
        c = lax.axis_index("c")
        return 2 * (j % half) + jnp.where(j < half, 1 - c, c)

    def body(a_ref, b_ref, o_ref, theirs_ref, acc_ref, stage_ref, got_ref, send_sems, recv_sems, load_sem):
        j, k = pl.program_id(0), pl.program_id(1)
        x, y, c = lax.axis_index("x"), lax.axis_index("y"), lax.axis_index("c")

        def to_sibling(q):
            return pltpu.make_async_remote_copy(src_ref=stage_ref, dst_ref=theirs_ref.at[q], send_sem=send_sems.at[q],
                                                recv_sem=recv_sems.at[q], device_id=(x, y, 1 - c), device_id_type=MESH)

        def fetch(q):
            return pltpu.make_async_copy(theirs_ref.at[q], got_ref, load_sem.at[0])

        @pl.when(k == 0)
        def _():
            acc_ref[...] = jnp.zeros_like(acc_ref)

        acc_ref[...] += lax.dot_general(a_ref[...], b_ref[...], TN_DIMS, preferred_element_type=F32)

        for q in range(half):
            @pl.when((j == q) & (k == nk - 1))
            def _(q=q):
                if q > 0:
                    to_sibling(q - 1).wait_send()
                stage_ref[...] = acc_ref[...].astype(BF16)
                to_sibling(q).start()

            @pl.when((j == half + q) & (k == 0))
            def _(q=q):
                to_sibling(q).wait_recv()
                fetch(q).start()

            @pl.when((j == half + q) & (k == nk - 1))
            def _(q=q):
                if q == 0:
                    to_sibling(half - 1).wait_send()
                fetch(q).wait()
                o_ref[...] = (acc_ref[...] + got_ref[...].astype(F32)).astype(BF16)

    in_specs = [pl.BlockSpec((None, tk, fs), lambda j, k: (device_of(j), k, 0)), pl.BlockSpec((tk, d), lambda j, k: (k, 0))]
    out_specs = [pl.BlockSpec((None, fs, d), lambda j, k: (jnp.maximum(j - half, 0), 0, 0)), ANY]
    shp = jax.ShapeDtypeStruct((half, fs, d), BF16)
    scratch = [pltpu.VMEM((fs, d), F32), pltpu.VMEM((fs, d), BF16), pltpu.VMEM((fs, d), BF16),
               pltpu.SemaphoreType.DMA((half,)), pltpu.SemaphoreType.DMA((half,)), pltpu.SemaphoreType.DMA((1,))]
    (sums, _), riders_out = _call(body, name, (nj, nk), in_specs, out_specs, [shp, shp], scratch, [hid3, db], rider)
    return sums, riders_out


def _dw_cols(ab, dz, name, rider=None):
    s, d = ab.shape
    n = dz.shape[1]
    tn = PROJ_IN_COLS if n % PROJ_IN_COLS == 0 else n
    tk = min(s, DW_TOKENS)
    tr = d // 2

    def body(a_ref, b_ref, o_ref, acc_ref):
        k = pl.program_id(2)

        @pl.when(k == 0)
        def _():
            acc_ref[...] = jnp.zeros_like(acc_ref)

        acc_ref[...] += lax.dot_general(a_ref[...], b_ref[...], TN_DIMS, preferred_element_type=F32)

        @pl.when(k == s // tk - 1)
        def _():
            o_ref[...] = acc_ref[...].astype(BF16)

    in_specs = [pl.BlockSpec((tk, tr), lambda j, r, k: (k, r)), pl.BlockSpec((tk, tn), lambda j, r, k: (k, j))]
    return _call(body, name, (n // tn, d // tr, s // tk), in_specs, pl.BlockSpec((tr, tn), lambda j, r, k: (r, j)),
                 jax.ShapeDtypeStruct((d, n), BF16), [pltpu.VMEM((tr, tn), F32)], [ab, dz], rider)


def _ffn_bwd_input(dg3, wg3, du3, wu3, dres, ln, name, rider=None):
    s, d = dres.shape
    nk, _, fs = dg3.shape
    tm = min(s, 512)

    def body(*refs):
        refs = list(refs)
        dg_in, wg_ref, du_in, wu_ref, dres_ref = refs[:5]
        del refs[:5]
        if ln is not None:
            xh_ref, rstd_ref, gain_ref = refs.pop(0), refs.pop(0), refs.pop(0)
        acc_ref = refs.pop()
        i, k = pl.program_id(0), pl.program_id(1)

        @pl.when(k == 0)
        def _():
            acc_ref[...] = jnp.zeros_like(acc_ref)

        acc_ref[...] += (jnp.dot(dg_in[...], wg_ref[...], preferred_element_type=F32)
                         + jnp.dot(du_in[...], wu_ref[...], preferred_element_type=F32))

        @pl.when(k == nk - 1)
        def _():
            if ln is not None:
                dr_ref, drb_ref, dg_ref, db_ref, sum_ref = refs

                @pl.when(i == 0)
                def _():
                    dg_ref[...] = jnp.zeros_like(dg_ref)
                    db_ref[...] = jnp.zeros_like(db_ref)
                    sum_ref[...] = jnp.zeros_like(sum_ref)

            for rows in _row_blocks(tm):
                dh = ALPHA * dres_ref[rows, :] + acc_ref[rows, :]
                if ln is None:
                    refs[0][rows, :] = dh
                else:
                    xhat = xh_ref[rows, :]
                    dg_ref[...] += jnp.sum(dh * xhat, axis=0, keepdims=True)
                    db_ref[...] += jnp.sum(dh, axis=0, keepdims=True)
                    dr = _ln_backward(dh, xhat, rstd_ref[rows, :], gain_ref[...])
                    sum_ref[...] += jnp.sum(dr, axis=0, keepdims=True)
                    dr_ref[rows, :] = dr
                    drb_ref[rows, :] = dr.astype(BF16)

    tok = pl.BlockSpec((tm, d), lambda i, k: (i, 0))
    vec = pl.BlockSpec((1, d), lambda i, k: (0, 0))
    a_spec = pl.BlockSpec((None, tm, fs), lambda i, k: (k, i, 0))
    w_spec = pl.BlockSpec((None, fs, d), lambda i, k: (k, 0, 0))
    ins, in_specs = [dg3, wg3, du3, wu3, dres], [a_spec, w_spec, a_spec, w_spec, tok]
    if ln is None:
        out_shape, out_specs = jax.ShapeDtypeStruct((s, d), F32), tok
    else:
        ins += list(ln)
        in_specs += [tok, pl.BlockSpec((tm, 1), lambda i, k: (i, 0)), vec]
        out_shape = [jax.ShapeDtypeStruct((s, d), F32), jax.ShapeDtypeStruct((s, d), BF16)] + [jax.ShapeDtypeStruct((1, d), F32)] * 3
        out_specs = [tok, tok, vec, vec, vec]
    return _call(body, name, (s // tm, nk), in_specs, out_specs, out_shape, [pltpu.VMEM((tm, d), F32)], ins, rider)


def _proj_in_bwd_ln(dz, w, dres, ln, branch_scale, name, rider=None):
    s, d = dres.shape
    n = w.shape[1]
    tm = min(s, 256)

    def body(dz_ref, w_hbm, dres_ref, xh_ref, rstd_ref, gain_ref, dr_ref, drb_ref, dg_ref, db_ref, w_ref, w_sem):
        @pl.when(pl.program_id(0) == 0)
        def _():
            whole = pltpu.make_async_copy(w_hbm, w_ref, w_sem.at[0])
            whole.start()
            whole.wait()
            dg_ref[...] = jnp.zeros_like(dg_ref)
            db_ref[...] = jnp.zeros_like(db_ref)

        acc = lax.dot_general(dz_ref[...], w_ref[...], NT_DIMS, preferred_element_type=F32)
        for rows in _row_blocks(tm):
            dh = ALPHA * dres_ref[rows, :] + acc[rows]
            xhat = xh_ref[rows, :]
            dg_ref[...] += jnp.sum(dh * xhat, axis=0, keepdims=True)
            db_ref[...] += jnp.sum(dh, axis=0, keepdims=True)
            dr = _ln_backward(dh, xhat, rstd_ref[rows, :], gain_ref[...])
            dr_ref[rows, :] = dr
            drb_ref[rows, :] = (branch_scale * dr).astype(BF16)

    tok = pl.BlockSpec((tm, d), lambda i: (i, 0))
    vec = pl.BlockSpec((1, d), lambda i: (0, 0))
    in_specs = [pl.BlockSpec((tm, n), lambda i: (i, 0)), ANY, tok, tok, pl.BlockSpec((tm, 1), lambda i: (i, 0)), vec]
    out_shape = [jax.ShapeDtypeStruct((s, d), F32), jax.ShapeDtypeStruct((s, d), BF16)] + [jax.ShapeDtypeStruct((1, d), F32)] * 2
    scratch = [pltpu.VMEM(w.shape, BF16), pltpu.SemaphoreType.DMA((1,))]
    return _call(body, name, (s // tm,), in_specs, [tok, tok, vec, vec], out_shape, scratch, [dz, w, dres] + list(ln), rider)


def _shift_rows_down(v, halo, k, row):
    out = pltpu.roll(v, k, 0)
    hr = halo.shape[0]
    for r in range(k):
        out = jnp.where(row == r, halo[hr - k + r:hr - k + r + 1, :], out)
    return out


def _shift_rows_up(v, halo, k, row):
    t = v.shape[0]
    out = pltpu.roll(v, t - k, 0)
    for r in range(k):
        out = jnp.where(row == t - k + r, halo[r:r + 1, :], out)
    return out


def _sgu_head_forward(z_ref, h, da, gv_ref, bv_ref):
    zu = z_ref[:, h * HEAD:(h + 1) * HEAD].astype(F32)
    zv = z_ref[:, da + h * HEAD:da + (h + 1) * HEAD].astype(F32)
    tu, tv = _gelu_tanh(zu), _gelu_tanh(zv)
    u = _gelu(zu, tu)
    v = _gelu(zv, tv)
    mu = jnp.mean(v, axis=-1, keepdims=True)
    c = v - mu
    rstd = lax.rsqrt(jnp.mean(c * c, axis=-1, keepdims=True) + LN_EPS)
    vhat = c * rstd
    vln = (vhat * gv_ref[h:h + 1, :] + bv_ref[h:h + 1, :]).astype(BF16)
    return (zu, tu), (zv, tv), u, vhat, rstd, vln


def _mixer_fwd(z, ws_masked, bs_wide, gv, bv, cw, name):
    s, zc = z.shape
    da = zc // 5
    nh = da // HEAD
    tm = min(s, MIXER_TOKENS)
    hb = tm // BF16_ROWS

    def body(z_ref, pc_ref, px_ref, ws_ref, bs_ref, gv_ref, bv_ref, cw_ref, y_ref):
        i = pl.program_id(0)
        for h in range(nh):
            _, _, u, _, _, vln = _sgu_head_forward(z_ref, h, da, gv_ref, bv_ref)
            for n in range(tm // CHUNK):
                rows = slice(n * CHUNK, (n + 1) * CHUNK)
                mixed = jnp.dot(ws_ref[h], vln[rows], preferred_element_type=F32) + bs_ref[h]
                y_ref[0, rows, h * HEAD:(h + 1) * HEAD] = (u[rows] * mixed).astype(BF16)
        gate_b = z_ref[:, 2 * da:3 * da].astype(F32)
        hc = z_ref[:, 3 * da:4 * da].astype(F32) * z_ref[:, 4 * da:5 * da].astype(F32)
        halo = jnp.where(i > 0, pc_ref[...].astype(F32) * px_ref[...].astype(F32), 0.0)
        row = lax.broadcasted_iota(jnp.int32, (tm, da), 0)
        y = cw_ref[0:1, :] * _shift_rows_down(hc, halo, 2, row) + cw_ref[1:2, :] * _shift_rows_down(hc, halo, 1, row) + cw_ref[2:3, :] * hc
        y_ref[1] = (gate_b * y).astype(BF16)

    prev = lambda col: pl.BlockSpec((BF16_ROWS, da), lambda i: (jnp.maximum(i * hb - 1, 0), col))
    in_specs = [pl.BlockSpec((tm, zc), lambda i: (i, 0)), prev(3), prev(4), _row((nh, CHUNK, CHUNK)), _row((nh, CHUNK, HEAD)),
                _row((nh, HEAD)), _row((nh, HEAD)), _row((CONV_TAPS, da))]
    return _call(body, name, (s // tm,), in_specs, pl.BlockSpec((2, tm, da), lambda i: (0, i, 0)),
                 jax.ShapeDtypeStruct((2, s, da), BF16), [], [z, z, z, ws_masked, bs_wide, gv, bv, cw])[0]


def _mixer_bwd(z, dy, ws_masked, bs_wide, gv, bv, cw, name, rider=None):
    s, zc = z.shape
    da = zc // 5
    nh = da // HEAD
    tm = min(s, MIXER_TOKENS)
    hb = tm // BF16_ROWS
    nblk = s // tm

    def body(z_ref, pc_ref, px_ref, nb_ref, dy_ref, ndy_ref, ws_ref, bs_ref, gv_ref, bv_ref, cw_ref,
             dz_ref, dws_ref, dbs_ref, dgv_ref, dbv_ref, dcw_ref, dbin_ref):
        i = pl.program_id(0)

        @pl.when(i == 0)
        def _():
            for ref in (dws_ref, dbs_ref, dgv_ref, dbv_ref, dcw_ref, dbin_ref):
                ref[...] = jnp.zeros_like(ref)

        causal = lax.broadcasted_iota(jnp.int32, (CHUNK, CHUNK), 0) >= lax.broadcasted_iota(jnp.int32, (CHUNK, CHUNK), 1)
        for h in range(nh):
            zu, zv, u, vhat, rstd, vln = _sgu_head_forward(z_ref, h, da, gv_ref, bv_ref)
            dya = dy_ref[0, :, h * HEAD:(h + 1) * HEAD].astype(F32)
            w = ws_ref[h]
            du_parts, dvln_parts = [], []
            for n in range(tm // CHUNK):
                rows = slice(n * CHUNK, (n + 1) * CHUNK)
                mixed = jnp.dot(w, vln[rows], preferred_element_type=F32) + bs_ref[h]
                du_parts.append(dya[rows] * mixed)
                dmix = dya[rows] * u[rows]
                dmix_b = dmix.astype(BF16)
                dws_ref[h] += jnp.where(causal, lax.dot_general(dmix_b, vln[rows], NT_DIMS, preferred_element_type=F32), 0.0)
                dbs_ref[h] += dmix
                dvln_parts.append(lax.dot_general(w, dmix_b, TN_DIMS, preferred_element_type=F32))
            du = jnp.concatenate(du_parts, axis=0)
            dvln = jnp.concatenate(dvln_parts, axis=0)
            dgv_ref[h:h + 1, :] += jnp.sum(dvln * vhat, axis=0, keepdims=True)
            dbv_ref[h:h + 1, :] += jnp.sum(dvln, axis=0, keepdims=True)
            dv = _ln_backward(dvln, vhat, rstd, gv_ref[h:h + 1, :])
            dzu = du * _gelu_grad(*zu)
            dzv = dv * _gelu_grad(*zv)
            ucols = slice(h * HEAD, (h + 1) * HEAD)
            vcols = slice(da + h * HEAD, da + (h + 1) * HEAD)
            dz_ref[:, ucols] = dzu.astype(BF16)
            dz_ref[:, vcols] = dzv.astype(BF16)
            dbin_ref[:, ucols] += jnp.sum(dzu, axis=0, keepdims=True)
            dbin_ref[:, vcols] += jnp.sum(dzv, axis=0, keepdims=True)

        gate_b = z_ref[:, 2 * da:3 * da].astype(F32)
        gate_c = z_ref[:, 3 * da:4 * da].astype(F32)
        xt = z_ref[:, 4 * da:5 * da].astype(F32)
        hc = gate_c * xt
        halo = jnp.where(i > 0, pc_ref[...].astype(F32) * px_ref[...].astype(F32), 0.0)
        row = lax.broadcasted_iota(jnp.int32, (tm, da), 0)
        sh1 = _shift_rows_down(hc, halo, 1, row)
        sh2 = _shift_rows_down(hc, halo, 2, row)
        y = cw_ref[0:1, :] * sh2 + cw_ref[1:2, :] * sh1 + cw_ref[2:3, :] * hc
        dyb = dy_ref[1].astype(F32)
        dconv = dyb * gate_b
        nhalo = jnp.where(i < nblk - 1, ndy_ref[...].astype(F32) * nb_ref[...].astype(F32), 0.0)
        dhc = cw_ref[2:3, :] * dconv + cw_ref[1:2, :] * _shift_rows_up(dconv, nhalo, 1, row) + cw_ref[0:1, :] * _shift_rows_up(dconv, nhalo, 2, row)
        dcw_ref[0:1, :] += jnp.sum(dconv * sh2, axis=0, keepdims=True)
        dcw_ref[1:2, :] += jnp.sum(dconv * sh1, axis=0, keepdims=True)
        dcw_ref[2:3, :] += jnp.sum(dconv * hc, axis=0, keepdims=True)
        for col, val in ((2, dyb * y), (3, dhc * xt), (4, dhc * gate_c)):
            cols = slice(col * da, (col + 1) * da)
            dz_ref[:, cols] = val.astype(BF16)
            dbin_ref[:, cols] += jnp.sum(val, axis=0, keepdims=True)

        @pl.when(i == nblk - 1)
        def _():
            for h in range(nh):
                dbs_ref[h] = jnp.broadcast_to(jnp.sum(dbs_ref[h], axis=1, keepdims=True), (CHUNK, HEAD))

    prev = lambda col: pl.BlockSpec((BF16_ROWS, da), lambda i: (jnp.maximum(i * hb - 1, 0), col))
    nxt = lambda i: jnp.minimum((i + 1) * hb, s // BF16_ROWS - 1)
    in_specs = [pl.BlockSpec((tm, zc), lambda i: (i, 0)), prev(3), prev(4), pl.BlockSpec((BF16_ROWS, da), lambda i: (nxt(i), 2)),
                pl.BlockSpec((2, tm, da), lambda i: (0, i, 0)), pl.BlockSpec((None, BF16_ROWS, da), lambda i: (1, nxt(i), 0)),
                _row((nh, CHUNK, CHUNK)), _row((nh, CHUNK, HEAD)), _row((nh, HEAD)), _row((nh, HEAD)), _row((CONV_TAPS, da))]
    out_specs = [pl.BlockSpec((tm, zc), lambda i: (i, 0)), _row((nh, CHUNK, CHUNK)), _row((nh, CHUNK, HEAD)), _row((nh, HEAD)),
                 _row((nh, HEAD)), _row((8, da)), _row((1, zc))]
    out_shape = [jax.ShapeDtypeStruct((s, zc), BF16), jax.ShapeDtypeStruct((nh, CHUNK, CHUNK), F32),
                 jax.ShapeDtypeStruct((nh, CHUNK, HEAD), F32), jax.ShapeDtypeStruct((nh, HEAD), F32),
                 jax.ShapeDtypeStruct((nh, HEAD), F32), jax.ShapeDtypeStruct((8, da), F32), jax.ShapeDtypeStruct((1, zc), F32)]
    return _call(body, name, (nblk,), in_specs, out_specs, out_shape, [], [z, z, z, z, dy, dy, ws_masked, bs_wide, gv, bv, cw], rider)


def _adam_update(g, w, m, v):
    m_new = ADAM_B1 * m + (1.0 - ADAM_B1) * g
    v_new = ADAM_B2 * v + (1.0 - ADAM_B2) * (g * g)
    m_hat = m_new / (1.0 - ADAM_B1 ** ADAM_STEP)
    v_hat = v_new / (1.0 - ADAM_B2 ** ADAM_STEP)
    return -ADAM_LR * (m_hat / (jnp.sqrt(v_hat) + ADAM_EPS) + ADAM_WD * w), m_new, v_new


def _adamw(gparts, w, m, v, name):
    n, r, c = gparts.shape
    tr = r // 4 if (r // 4) % BF16_ROWS == 0 else r

    def body(g_ref, w_ref, m_ref, v_ref, go_ref, d_ref, mo_ref, vo_ref):
        g = g_ref[0].astype(F32)
        for q in range(1, n):
            g = g + g_ref[q].astype(F32)
        go_ref[...] = g
        d_ref[...], mo_ref[...], vo_ref[...] = _adam_update(g, w_ref[...], m_ref[...], v_ref[...])

    blk = pl.BlockSpec((tr, c), lambda i: (i, 0))
    shp = jax.ShapeDtypeStruct((r, c), F32)
    return _call(body, name, (r // tr,), [pl.BlockSpec((n, tr, c), lambda i: (0, i, 0)), blk, blk, blk], [blk] * 4, [shp] * 4, [],
                 [gparts, w, m, v])[0]


def _adamw_small(packs, rows, w, m, v, conv, loss_scale, name):
    n_par, n_dev = len(rows), packs.shape[0]
    taps = conv[0].shape[0]

    def body(*refs):
        refs = list(refs)
        cut = lambda n: [refs.pop(0) for _ in range(n)]
        p_ref, w_refs, m_refs, v_refs, (cw_ref, cm_ref, cv_ref) = refs.pop(0), cut(n_par), cut(n_par), cut(n_par), cut(3)
        outs = [cut(4) for _ in range(n_par + 1)]
        loss_ref = refs.pop(0)
        at = 0
        for k in range(n_par):
            g = p_ref[0, at:at + rows[k], :]
            for dev in range(1, n_dev):
                g = g + p_ref[dev, at:at + rows[k], :]
            go_ref, d_ref, mo_ref, vo_ref = outs[k]
            go_ref[...] = g
            d_ref[...], mo_ref[...], vo_ref[...] = _adam_update(g, w_refs[k][...], m_refs[k][...], v_refs[k][...])
            at += rows[k]
        me = 4 * lax.axis_index("x") + 2 * lax.axis_index("y") + lax.axis_index("c")
        go_ref, d_ref, mo_ref, vo_ref = outs[n_par]
        for tap in range(taps):
            row = pl.ds(at + tap * n_dev + me, 1)
            g = p_ref[0, row, :]
            for dev in range(1, n_dev):
                g = g + p_ref[dev, row, :]
            one = slice(tap, tap + 1)
            go_ref[one, :] = g
            d_ref[one, :], mo_ref[one, :], vo_ref[one, :] = _adam_update(g, cw_ref[one, :], cm_ref[one, :], cv_ref[one, :])
        at += taps * n_dev
        sq = p_ref[0, at:, :]
        for dev in range(1, n_dev):
            sq = sq + p_ref[dev, at:, :]
        total = jnp.sum(jnp.sum(sq, axis=0, keepdims=True), axis=1, keepdims=True)
        loss_ref[...] = jnp.broadcast_to(loss_scale * total, loss_ref.shape)

    vmem = pl.BlockSpec(memory_space=pltpu.VMEM)
    ins = [packs] + list(w) + list(m) + list(v) + list(conv)
    out_shape = [jax.ShapeDtypeStruct(a.shape, F32) for a in list(w) + [conv[0]] for _ in range(4)] + [jax.ShapeDtypeStruct((8, LANES), F32)]
    outs = pl.pallas_call(body, name=name, in_specs=[vmem] * len(ins), out_specs=[vmem] * len(out_shape), out_shape=out_shape,
                          compiler_params=pltpu.CompilerParams(vmem_limit_bytes=V7X_VMEM_LIMIT))(*ins)
    return [outs[4 * k:4 * k + 4] for k in range(n_par + 1)], outs[-1][0, 0]


def _rows128(a):
    return a.reshape(-1, LANES)


def kernel(x, ffa_gate, ffa_up, ffa_down, ln_a_g, ln_a_b, w_in, b_in, w_s, b_s, ln_v_g, ln_v_b, conv_w, w_out, b_out, ln_m_g, ln_m_b, ffc_gate, ffc_up, ffc_down, ln_c_g, ln_c_b, loss_target, m_ffa_gate, m_ffa_up, m_ffa_down, m_ln_a_g, m_ln_a_b, m_w_in, m_b_in, m_w_s, m_b_s, m_ln_v_g, m_ln_v_b, m_conv_w, m_w_out, m_b_out, m_ln_m_g, m_ln_m_b, m_ffc_gate, m_ffc_up, m_ffc_down, m_ln_c_g, m_ln_c_b, v_ffa_gate, v_ffa_up, v_ffa_down, v_ln_a_g, v_ln_a_b, v_w_in, v_b_in, v_w_s, v_b_s, v_ln_v_g, v_ln_v_b, v_conv_w, v_w_out, v_b_out, v_ln_m_g, v_ln_m_b, v_ffc_gate, v_ffc_up, v_ffc_down, v_ln_c_g, v_ln_c_b):
    x2, target = x[0], loss_target[0]
    s, d = x2.shape
    da = d // 2
    nh = da // HEAD

    big = dict(ffa_gate=ffa_gate, ffa_up=ffa_up, ffa_down=ffa_down, w_in=w_in, w_out=w_out, ffc_gate=ffc_gate, ffc_up=ffc_up, ffc_down=ffc_down)
    big_m = dict(ffa_gate=m_ffa_gate, ffa_up=m_ffa_up, ffa_down=m_ffa_down, w_in=m_w_in, w_out=m_w_out, ffc_gate=m_ffc_gate, ffc_up=m_ffc_up, ffc_down=m_ffc_down)
    big_v = dict(ffa_gate=v_ffa_gate, ffa_up=v_ffa_up, ffa_down=v_ffa_down, w_in=v_w_in, w_out=v_w_out, ffc_gate=v_ffc_gate, ffc_up=v_ffc_up, ffc_down=v_ffc_down)
    local = lambda k, a: jnp.transpose(a[0]) if k in TRANSPOSED else a[0]
    shard = {k: local(k, w).astype(BF16) for k, w in big.items()}
    conv_rows = jnp.pad(conv_w[0], ((0, 8 - CONV_TAPS), (0, 0)))

    tril = jnp.tril(jnp.ones((CHUNK, CHUNK), dtype=bool))
    ws_masked = jnp.where(tril[None], w_s[0], 0.0).astype(BF16)
    bs_wide = jnp.broadcast_to(b_s[0][:, :, None], (nh, CHUNK, HEAD))
    gv, bv = ln_v_g.reshape(nh, HEAD), ln_v_b.reshape(nh, HEAD)

    full = {}
    xb = x2.astype(BF16)
    (silu_a, udsilu_a, hid_a), (full["ffa_gate"], full["ffa_up"], full["ffa_down"]) = _ffn_gateup_gathering(
        xb, shard["ffa_gate"], shard["ffa_up"], shard["ffa_down"], "ffa_gateup")
    (xhat1, h1b, rstd1), (full["w_in"], full["w_out"], conv_full) = _down_ln(
        hid_a, full["ffa_down"], None, x2, None, ln_a_g, ln_a_b, 0.5, "ffa_down_ln", rider=_Gather([shard["w_in"], shard["w_out"], conv_rows], by_columns=(0,)))
    cw = jnp.transpose(conv_full[:, :CONV_TAPS, :], (1, 0, 2)).reshape(CONV_TAPS, da)
    w_out2 = full["w_out"].reshape(2, da, d)
    z, (full["ffc_gate"],) = _proj_in(h1b, full["w_in"], b_in, "proj_in", _Gather([shard["ffc_gate"]]))
    ycat = _mixer_fwd(z, ws_masked, bs_wide, gv, bv, cw, "mixer_fwd")
    (xhat2, h2b, rstd2), _ = _down_ln(ycat, w_out2, b_out, xhat1, (ln_a_g, ln_a_b), ln_m_g, ln_m_b, 1.0, "proj_out_ln")
    (silu_c, udsilu_c, hid_c), (full["ffc_up"], full["ffc_down"]) = _ffn_gateup_gathering(
        h2b, full["ffc_gate"], shard["ffc_up"], shard["ffc_down"], "ffc_gateup")
    (dr3, dr3b, sq_err, d_ln_c_g, d_ln_c_b), _ = _down_ln(hid_c, full["ffc_down"], None, xhat2, (ln_m_g, ln_m_b), ln_c_g, ln_c_b, 0.5,
                                                          "ffc_down_ln_loss", target=target)

    landed = {}
    (dg_c, du_c), _ = _ffn_bwd_hidden(dr3b, full["ffc_down"], silu_c, udsilu_c, "ffc_bwd_hidden")
    part, _ = _dw_hidden_rows_paired(hid_c, dr3b, "ffc_dw_down")
    part, (landed["ffc_down"],) = _dw_hidden_rows_paired(dg_c, h2b, "ffc_dw_gate", _ChipScatter(part))
    part, (landed["ffc_gate"],) = _dw_hidden_rows_paired(du_c, h2b, "ffc_dw_up", _ChipScatter(part))
    (dr2, dr2b, d_ln_m_g, d_ln_m_b, d_b_out), (landed["ffc_up"],) = _ffn_bwd_input(
        dg_c, full["ffc_gate"], du_c, full["ffc_up"], dr3, (xhat2, rstd2, ln_m_g), "ffc_bwd_input_ln", _ChipScatter(part))
    updated = {}
    adam_side = lambda k: _AdamSide(landed[k], local(k, big[k]), local(k, big_m[k]), local(k, big_v[k]))
    dycat, (updated["ffc_down"], updated["ffc_up"]) = _nt_hidden(dr2b, w_out2, "proj_out_bwd", [adam_side("ffc_down"), adam_side("ffc_up")])
    part, _ = _dw_hidden_rows(ycat, dr2b, "proj_out_dw")
    (dz, d_w_s, d_b_s_wide, d_gv, d_bv, d_cw, d_b_in), (landed["w_out"],) = _mixer_bwd(
        z, dycat, ws_masked, bs_wide, gv, bv, cw, "mixer_bwd", _Scatter(part.reshape(N_DEV, d // N_DEV, d)))
    (dr1, dr1b, d_ln_a_g, d_ln_a_b), updated["ffc_gate"] = _proj_in_bwd_ln(
        dz, full["w_in"], dr2, (xhat1, rstd1, ln_a_g), 0.5, "proj_in_bwd_ln", adam_side("ffc_gate"))
    small_g = dict(ln_a_g=d_ln_a_g, ln_a_b=d_ln_a_b, b_in=d_b_in, w_s=d_w_s, b_s=d_b_s_wide[:, :, 0], ln_v_g=d_gv, ln_v_b=d_bv, b_out=d_b_out,
                   ln_m_g=d_ln_m_g, ln_m_b=d_ln_m_b, ln_c_g=d_ln_c_g, ln_c_b=d_ln_c_b)
    pack = jnp.concatenate([_rows128(g) for g in small_g.values()] + [_rows128(d_cw[:CONV_TAPS]), _rows128(sq_err)], axis=0)
    part, (packs,) = _dw_cols(h1b, dz, "proj_in_dw", _Scatter(pack, whole=True))
    (dg_a, du_a), (landed["w_in"],) = _ffn_bwd_hidden(dr1b, full["ffa_down"], silu_a, udsilu_a, "ffa_bwd_hidden", _Scatter(part, by_columns=True))
    part, _ = _dw_hidden_rows_paired(hid_a, dr1b, "ffa_dw_down")
    part, (landed["ffa_down"],) = _dw_hidden_rows_paired(dg_a, xb, "ffa_dw_gate", _ChipScatter(part))
    part, (landed["ffa_gate"],) = _dw_hidden_rows_paired(du_a, xb, "ffa_dw_up", _ChipScatter(part))
    grad_x, (landed["ffa_up"],) = _ffn_bwd_input(dg_a, full["ffa_gate"], du_a, full["ffa_up"], dr1, None, "ffa_bwd_input", _ChipScatter(part))

    grads, deltas, new_m, new_v = {}, {}, {}, {}
    for k in big:
        out = updated[k] if k in updated else _adamw(landed[k], local(k, big[k]), local(k, big_m[k]), local(k, big_v[k]), "adamw_" + k)
        grads[k], deltas[k], new_m[k], new_v[k] = ((jnp.transpose(o) if k in TRANSPOSED else o).reshape(big[k].shape) for o in out)

    small = dict(ln_a_g=ln_a_g, ln_a_b=ln_a_b, b_in=b_in, w_s=w_s, b_s=b_s, ln_v_g=ln_v_g, ln_v_b=ln_v_b, b_out=b_out,
                 ln_m_g=ln_m_g, ln_m_b=ln_m_b, ln_c_g=ln_c_g, ln_c_b=ln_c_b)
    small_m = dict(ln_a_g=m_ln_a_g, ln_a_b=m_ln_a_b, b_in=m_b_in, w_s=m_w_s, b_s=m_b_s, ln_v_g=m_ln_v_g, ln_v_b=m_ln_v_b, b_out=m_b_out,
                   ln_m_g=m_ln_m_g, ln_m_b=m_ln_m_b, ln_c_g=m_ln_c_g, ln_c_b=m_ln_c_b)
    small_v = dict(ln_a_g=v_ln_a_g, ln_a_b=v_ln_a_b, b_in=v_b_in, w_s=v_w_s, b_s=v_b_s, ln_v_g=v_ln_v_g, ln_v_b=v_ln_v_b, b_out=v_b_out,
                   ln_m_g=v_ln_m_g, ln_m_b=v_ln_m_b, ln_c_g=v_ln_c_g, ln_c_b=v_ln_c_b)
    snames = list(small)
    assert snames == list(small_g) and conv_w.shape[2] == LANES and da == N_DEV * LANES
    views = lambda tree: [_rows128(tree[k]) for k in snames]
    out, loss = _adamw_small(packs, [a.shape[0] for a in views(small)], views(small), views(small_m), views(small_v),
                             (conv_w[0], m_conv_w[0], v_conv_w[0]), 0.5 / d, "adamw_small")
    for k, per_param in zip(snames + ["conv_w"], out):
        shape = conv_w.shape if k == "conv_w" else small[k].shape
        grads[k], deltas[k], new_m[k], new_v[k] = (o.reshape(shape) for o in per_param)

    order = ["ffa_gate", "ffa_up", "ffa_down", "ln_a_g", "ln_a_b", "w_in", "b_in", "w_s", "b_s", "ln_v_g", "ln_v_b", "conv_w", "w_out", "b_out",
             "ln_m_g", "ln_m_b", "ffc_gate", "ffc_up", "ffc_down", "ln_c_g", "ln_c_b"]
    return (loss, grad_x[None], *[grads[k] for k in order], *[deltas[k] for k in order], *[new_m[k] for k in order], *[new_v[k] for k in order])
```

```python
import math

import jax
import jax.numpy as jnp
from jax import lax
from jax.experimental import pallas as pl
from jax.experimental.pallas import tpu as pltpu

BF16 = jnp.bfloat16
F32 = jnp.float32
MESH = pl.DeviceIdType.MESH

N_DEV = 8
HEAD = 128
CHUNK = 128
CONV_TAPS = 3
LN_EPS = 1e-5
ALPHA = float(2 ** 0.25)
GELU_C = 0.7978845608028654
GELU_A = 0.044715
ADAM_LR, ADAM_B1, ADAM_B2, ADAM_EPS, ADAM_WD, ADAM_STEP = 0.001, 0.9, 0.999, 1e-08, 0.01, 10
V7X_VMEM_LIMIT = 56 * 1024 * 1024
LANES = 128
BF16_ROWS = 16
MXU_COLS = 256
TRANSPOSED = ("ffa_gate", "ffa_up", "ffc_gate", "ffc_up")
PROJ_IN_COLS = 1280
MIXER_TOKENS = 512
ADAM_SIDE_ROWS = 64
DW_TOKENS = 2048

NT_DIMS = (((1,), (1,)), ((), ()))
TN_DIMS = (((0,), (0,)), ((), ()))
ANY = pl.BlockSpec(memory_space=pl.ANY)


def _gelu_tanh(x):
    return jnp.tanh(GELU_C * (x + GELU_A * x * x * x))


def _gelu(x, t):
    return 0.5 * x * (1.0 + t)


def _gelu_grad(x, t):
    return 0.5 * (1.0 + t) + 0.5 * x * (1.0 - t * t) * GELU_C * (1.0 + 3.0 * GELU_A * x * x)


def _sigmoid(x):
    return 0.5 * jnp.tanh(0.5 * x) + 0.5


def _row(shape):
    return pl.BlockSpec(shape, lambda *_: (0,) * len(shape))


def _row_blocks(tm, rows=128):
    rows = min(rows, tm)
    return [slice(r, r + rows) for r in range(0, tm, rows)]


def _ln_backward(dh, xhat, rstd, gain):
    dxh = dh * gain
    m1 = jnp.mean(dxh, axis=-1, keepdims=True)
    m2 = jnp.mean(dxh * xhat, axis=-1, keepdims=True)
    return rstd * (dxh - m1 - xhat * m2)


def _place():
    x, y, c = lax.axis_index("x"), lax.axis_index("y"), lax.axis_index("c")
    return x, y, c, [(1 - x, y), (x, 1 - y), (1 - x, 1 - y)]


def _other_devices(x, y, c):
    flips = [(bx, by, bc) for bx in (0, 1) for by in (0, 1) for bc in (0, 1)][1:]
    return [(1 - x if bx else x, 1 - y if by else y, 1 - c if bc else c) for bx, by, bc in flips]


class _Gather:
    def __init__(self, shards, forward_at=0.75, by_columns=()):
        n = len(shards)
        self.n, self.forward_at, self.by_columns = n, forward_at, tuple(by_columns)
        self.inputs = list(shards)
        self.out_shapes = [jax.ShapeDtypeStruct((a.shape[0], N_DEV * a.shape[1]) if i in self.by_columns else (N_DEV,) + a.shape, a.dtype)
                           for i, a in enumerate(shards)]
        self.scratch = [pltpu.SemaphoreType.DMA((n, 7)), pltpu.SemaphoreType.DMA((n, 7)), pltpu.SemaphoreType.DMA((n,))]

    def _block(self, outs, a, dev):
        if a in self.by_columns:
            cols = outs[a].shape[1] // N_DEV
            return outs[a].at[:, pl.ds(dev * cols, cols)]
        return outs[a].at[dev]

    def _copy(self, outs, sems, a, k, block, to, src=None):
        dst = self._block(outs, a, block)
        return pltpu.make_async_remote_copy(src_ref=dst if src is None else src, dst_ref=dst, send_sem=sems[0].at[a, k],
                                            recv_sem=sems[1].at[a, k], device_id=to, device_id_type=MESH)

    def start(self, ins, outs, sems, urgent=None):
        x, y, c, chips = _place()
        me = 4 * x + 2 * y + c
        for a in range(self.n):
            pltpu.make_async_copy(ins[a], self._block(outs, a, me), sems[2].at[a]).start()
        urgent = list(range(self.n)) if urgent is None else list(urgent)
        for group in (urgent, [a for a in range(self.n) if a not in urgent]):
            for a in group:
                self._copy(outs, sems, a, 0, me, (x, y, 1 - c), src=ins[a]).start()
                for j in (0, 1):
                    self._copy(outs, sems, a, 1 + j, me, (*chips[j], c), src=ins[a]).start()
            for a in group:
                self._copy(outs, sems, a, 3, me, (*chips[2], c), src=ins[a]).start()

    def wait_sibling(self, outs, sems, a):
        x, y, c, _ = _place()
        self._copy(outs, sems, a, 0, 4 * x + 2 * y + 1 - c, (x, y, 1 - c)).wait_recv()

    def pass_on(self, outs, sems, a, j):
        x, y, c, chips = _place()
        block = 4 * chips[j][0] + 2 * chips[j][1] + c
        self._copy(outs, sems, a, 1 + j, block, (x, y, 1 - c)).wait_recv()
        self._copy(outs, sems, a, 4 + j, block, (x, y, 1 - c)).start()

    def wait_passed(self, outs, sems, a, j):
        x, y, c, chips = _place()
        self._copy(outs, sems, a, 4 + j, 4 * chips[j][0] + 2 * chips[j][1] + 1 - c, (x, y, 1 - c)).wait_recv()

    def wait_sent(self, ins, outs, sems, a):
        x, y, c, _ = _place()
        me = 4 * x + 2 * y + c
        for k in range(7):
            self._copy(outs, sems, a, k, me, (x, y, 1 - c), src=ins[a]).wait_send()
        pltpu.make_async_copy(ins[a], self._block(outs, a, me), sems[2].at[a]).wait()

    def forward(self, ins, outs, sems):
        for a in range(self.n):
            for j in range(3):
                self.pass_on(outs, sems, a, j)

    def finish(self, ins, outs, sems):
        for a in range(self.n):
            self.wait_sibling(outs, sems, a)
            for j in range(3):
                self.wait_passed(outs, sems, a, j)
        for a in range(self.n):
            self.wait_sent(ins, outs, sems, a)

    def before(self, step, n_steps, ins, outs, sems):
        pl.when(step == 0)(lambda: self.start(ins, outs, sems))
        pl.when(step == int(self.forward_at * (n_steps - 1)))(lambda: self.forward(ins, outs, sems))

    def after(self, step, n_steps, ins, outs, sems):
        pl.when(step == n_steps - 1)(lambda: self.finish(ins, outs, sems))


class _Scatter:
    def __init__(self, partial, whole=False, by_columns=False):
        self.whole, self.by_columns = whole, by_columns
        self.inputs = [partial]
        if whole:
            shape = (N_DEV,) + partial.shape
        elif by_columns:
            shape = (N_DEV, partial.shape[0], partial.shape[1] // N_DEV)
        else:
            shape = partial.shape
        self.out_shapes = [jax.ShapeDtypeStruct(shape, partial.dtype)]
        self.scratch = [pltpu.SemaphoreType.DMA((7,)), pltpu.SemaphoreType.DMA((7,)), pltpu.SemaphoreType.DMA((1,))]

    def _copies(self, ins, outs, sems):
        x, y, c, _ = _place()
        me = 4 * x + 2 * y + c
        if self.whole:
            block = lambda dev: ins[0]
        elif self.by_columns:
            cols = ins[0].shape[1] // N_DEV
            block = lambda dev: ins[0].at[:, pl.ds(dev * cols, cols)]
        else:
            block = lambda dev: ins[0].at[dev]
        mine = pltpu.make_async_copy(block(me), outs[0].at[me], sems[2].at[0])
        remote = [pltpu.make_async_remote_copy(src_ref=block(4 * px + 2 * py + pc), dst_ref=outs[0].at[me], send_sem=sems[0].at[k],
                                               recv_sem=sems[1].at[k], device_id=(px, py, pc), device_id_type=MESH)
                  for k, (px, py, pc) in enumerate(_other_devices(x, y, c))]
        return mine, remote

    def start(self, ins, outs, sems):
        mine, remote = self._copies(ins, outs, sems)
        mine.start()
        for cp in remote:
            cp.start()

    def finish(self, ins, outs, sems):
        mine, remote = self._copies(ins, outs, sems)
        for cp in remote:
            cp.wait()
        mine.wait()

    def before(self, step, n_steps, ins, outs, sems):
        pl.when(step == 0)(lambda: self.start(ins, outs, sems))

    def after(self, step, n_steps, ins, outs, sems):
        pl.when(step == n_steps - 1)(lambda: self.finish(ins, outs, sems))


class _ChipScatter(_Scatter):
    def __init__(self, sums):
        super().__init__(sums)
        self.scratch = [pltpu.SemaphoreType.DMA((3,)), pltpu.SemaphoreType.DMA((3,)), pltpu.SemaphoreType.DMA((1,))]

    def _copies(self, ins, outs, sems):
        x, y, c, chips = _place()
        my_chip = 2 * x + y
        mine = pltpu.make_async_copy(ins[0].at[my_chip], outs[0].at[my_chip], sems[2].at[0])
        remote = [pltpu.make_async_remote_copy(src_ref=ins[0].at[2 * px + py], dst_ref=outs[0].at[my_chip], send_sem=sems[0].at[k],
                                               recv_sem=sems[1].at[k], device_id=(px, py, c), device_id_type=MESH)
                  for k, (px, py) in enumerate(chips)]
        return mine, remote


class _AdamSide:
    def __init__(self, gparts, w, m, v):
        self.n, self.r, self.c = gparts.shape
        self.inputs = [gparts, w, m, v]
        self.out_shapes = [jax.ShapeDtypeStruct((self.r, self.c), F32)] * 4
        self.scratch = []

    def _plan(self, n_steps):
        rows = ADAM_SIDE_ROWS
        while self.r // rows > n_steps:
            rows *= 2
        assert self.r % rows == 0
        return rows, self.r // rows

    def _rows(self, grid):
        rows, n_blocks = self._plan(math.prod(grid))

        def block(*ids):
            step = 0
            for size, pid in zip(grid, ids):
                step = step * size + pid
            return jnp.minimum(step, n_blocks - 1)
        return pl.BlockSpec((rows, self.c), lambda *ids: (block(*ids), 0)), block, rows

    def in_specs(self, grid):
        spec, block, rows = self._rows(grid)
        return [pl.BlockSpec((self.n, rows, self.c), lambda *ids: (0, block(*ids), 0)), spec, spec, spec]

    def out_specs(self, grid):
        return [self._rows(grid)[0]] * 4

    def before(self, step, n_steps, ins, outs, sems):
        @pl.when(step < self._plan(n_steps)[1])
        def _():
            g_ref, w_ref, m_ref, v_ref = ins
            g = g_ref[0].astype(F32)
            for q in range(1, self.n):
                g = g + g_ref[q].astype(F32)
            outs[0][...] = g
            outs[1][...], outs[2][...], outs[3][...] = _adam_update(g, w_ref[...], m_ref[...], v_ref[...])

    def after(self, step, n_steps, ins, outs, sems):
        pass


def _call(body, name, grid, in_specs, out_specs, out_shape, scratch, ins, rider=None):
    single = not isinstance(out_shape, (list, tuple))
    out_shape = [out_shape] if single else list(out_shape)
    out_specs = [out_specs] if single else list(out_specs)
    params = pltpu.CompilerParams(dimension_semantics=("arbitrary",) * len(grid), vmem_limit_bytes=V7X_VMEM_LIMIT)
    if rider is None:
        outs = pl.pallas_call(body, name=name, grid=grid, in_specs=in_specs, out_specs=out_specs, out_shape=out_shape,
                              scratch_shapes=scratch, compiler_params=params)(*ins)
        return (outs[0] if single else outs), None
    riders = list(rider) if isinstance(rider, (list, tuple)) else [rider]
    n_in, n_out, n_scr = len(ins), len(out_shape), len(scratch)
    n_steps = math.prod(grid)
    specs = lambda r, kind, count: getattr(r, kind)(grid) if hasattr(r, kind) else [ANY] * count

    def carried(*refs):
        refs = list(refs)
        cut = lambda n: [refs.pop(0) for _ in range(n)]
        b_in, r_in = cut(n_in), [cut(len(r.inputs)) for r in riders]
        b_out, r_out = cut(n_out), [cut(len(r.out_shapes)) for r in riders]
        b_scr, r_scr = cut(n_scr), [cut(len(r.scratch)) for r in riders]
        step = 0
        for axis, size in enumerate(grid):
            step = step * size + pl.program_id(axis)
        for r, r_ins, r_outs, r_sems in zip(riders, r_in, r_out, r_scr):
            r.before(step, n_steps, r_ins, r_outs, r_sems)
        body(*b_in, *b_out, *b_scr)
        for r, r_ins, r_outs, r_sems in zip(riders, r_in, r_out, r_scr):
            r.after(step, n_steps, r_ins, r_outs, r_sems)

    outs = pl.pallas_call(
        carried, name=name, grid=grid,
        in_specs=list(in_specs) + [sp for r in riders for sp in specs(r, "in_specs", len(r.inputs))],
        out_specs=out_specs + [sp for r in riders for sp in specs(r, "out_specs", len(r.out_shapes))],
        out_shape=out_shape + [sh for r in riders for sh in r.out_shapes],
        scratch_shapes=list(scratch) + [sc for r in riders for sc in r.scratch], compiler_params=params,
    )(*ins, *[a for r in riders for a in r.inputs])
    base, rest = outs[:n_out], list(outs[n_out:])
    per_rider = [[rest.pop(0) for _ in r.out_shapes] for r in riders]
    return (base[0] if single else base), (per_rider if isinstance(rider, (list, tuple)) else per_rider[0])


def _arrival_block(j):
    x, y, c = lax.axis_index("x"), lax.axis_index("y"), lax.axis_index("c")
    chip, other_core = j // 2, j % 2
    px = jnp.where((chip == 1) | (chip == 3), 1 - x, x)
    py = jnp.where((chip == 2) | (chip == 3), 1 - y, y)
    pc = jnp.where(other_core == 1, 1 - c, c)
    return 4 * px + 2 * py + pc


def _ffn_gateup_gathering(xb, gate, up_shard, down_shard, name):
    s, d = xb.shape
    fs = up_shard.shape[0]
    tm = min(s, 1024)
    ni = s // tm
    ask_at = max(ni - 2, 0)
    gate_here = gate.ndim == 2
    gather = _Gather(([gate] if gate_here else []) + [up_shard, down_shard])
    n_g = gather.n
    used_here, down = tuple(range(n_g - 1)), n_g - 1

    def body(x_ref, *refs):
        refs = list(refs)
        gate_full = None if gate_here else refs.pop(0)
        shards = [refs.pop(0) for _ in range(n_g)]
        silu_ref, udsilu_ref, h_ref = refs.pop(0), refs.pop(0), refs.pop(0)
        fulls = [refs.pop(0) for _ in range(n_g)]
        w_ref, w_sems = refs.pop(0), refs.pop(0)
        sems = refs
        j, i = pl.program_id(0), pl.program_id(1)
        gate_src, up_src = (fulls[0], fulls[1]) if gate_here else (gate_full, fulls[0])

        def load(slot, srcs):
            return [pltpu.make_async_copy(src, w_ref.at[slot, a], w_sems.at[slot, a]) for a, src in enumerate(srcs)]

        @pl.when((j == 0) & (i == 0))
        def _():
            gather.start(shards, fulls, sems, urgent=used_here)
            mine = load(0, (shards[0] if gate_here else gate_full.at[_arrival_block(0)], shards[n_g - 2]))
            for cp in mine:
                cp.start()
            for cp in mine:
                cp.wait()

        for nxt in range(1, N_DEV):
            @pl.when((j == nxt) & (i == 0))
            def _(nxt=nxt):
                for cp in load(nxt % 2, (gate_src.at[0], up_src.at[0])):
                    cp.wait()

        for nxt in range(1, N_DEV):
            @pl.when((j == nxt - 1) & (i == ask_at))
            def _(nxt=nxt):
                for a in used_here:
                    if nxt == 1:
                        gather.wait_sibling(fulls, sems, a)
                    elif nxt % 2 == 0:
                        gather.pass_on(fulls, sems, a, nxt // 2 - 1)
                    else:
                        gather.wait_passed(fulls, sems, a, nxt // 2 - 1)
                block = _arrival_block(nxt)
                for cp in load(nxt % 2, (gate_src.at[block], up_src.at[block])):
                    cp.start()

        @pl.when((j == N_DEV - 1) & (i == ask_at))
        def _():
            for other_chip in range(3):
                gather.pass_on(fulls, sems, down, other_chip)

        x = x_ref[...]
        g = lax.dot_general(x, w_ref[j % 2, 0], NT_DIMS, preferred_element_type=F32)
        u = lax.dot_general(x, w_ref[j % 2, 1], NT_DIMS, preferred_element_type=F32)
        sg = _sigmoid(g)
        silu = g * sg
        silu_ref[...] = silu.astype(BF16)
        udsilu_ref[...] = (u * (sg + silu * (1.0 - sg))).astype(BF16)
        h_ref[...] = (silu * u).astype(BF16)

        @pl.when((j == N_DEV - 1) & (i == ni - 1))
        def _():
            gather.wait_sibling(fulls, sems, down)
            for other_chip in range(3):
                gather.wait_passed(fulls, sems, down, other_chip)
            for a in range(n_g):
                gather.wait_sent(shards, fulls, sems, a)

    shp = jax.ShapeDtypeStruct((N_DEV, s, fs), BF16)
    o_spec = pl.BlockSpec((None, tm, fs), lambda j, i: (_arrival_block(j), i, 0))
    ins = ([] if gate_here else [gate]) + gather.inputs
    outs = pl.pallas_call(
        body, name=name, grid=(N_DEV, ni), in_specs=[pl.BlockSpec((tm, d), lambda j, i: (i, 0))] + [ANY] * len(ins),
        out_specs=[o_spec, o_spec, o_spec] + [ANY] * n_g, out_shape=[shp, shp, shp] + gather.out_shapes,
        scratch_shapes=[pltpu.VMEM((2, 2, fs, d), BF16), pltpu.SemaphoreType.DMA((2, 2))] + gather.scratch,
        compiler_params=pltpu.CompilerParams(dimension_semantics=("arbitrary", "arbitrary"), vmem_limit_bytes=V7X_VMEM_LIMIT),
    )(xb, *ins)
    return outs[:3], outs[3:]


def _down_ln(a3, w3, bias, res, res_affine, ln_g, ln_b, scale, name, target=None, rider=None):
    nk, s, tk = a3.shape
    d = w3.shape[2]
    tm = min(s, 256)
    final = target is not None

    def body(*refs):
        refs = list(refs)
        a_ref, w_hbm = refs[:2]
        del refs[:2]
        bias_ref = refs.pop(0) if bias is not None else None
        res_ref = refs.pop(0)
        rg_ref, rb_ref = (refs.pop(0), refs.pop(0)) if res_affine is not None else (None, None)
        g_ref, b_ref = refs.pop(0), refs.pop(0)
        t_ref = refs.pop(0) if final else None
        w_sem = refs.pop()
        w_ref = refs.pop()
        i = pl.program_id(0)
        if final:
            dr_ref, drb_ref, sq_ref, dg_ref, db_ref = refs
        else:
            xh_ref, hb_ref, rstd_ref = refs

        @pl.when(i == 0)
        def _():
            whole = pltpu.make_async_copy(w_hbm, w_ref, w_sem.at[0])
            whole.start()
            whole.wait()
            if final:
                sq_ref[...] = jnp.zeros_like(sq_ref)
                dg_ref[...] = jnp.zeros_like(dg_ref)
                db_ref[...] = jnp.zeros_like(db_ref)

        y = jnp.dot(a_ref[0], w_ref[0], preferred_element_type=F32)
        for k in range(1, nk):
            y = y + jnp.dot(a_ref[k], w_ref[k], preferred_element_type=F32)
        if bias_ref is not None:
            y = y + bias_ref[...]
        for rows in _row_blocks(tm):
            r = res_ref[rows, :]
            if rg_ref is not None:
                r = r * rg_ref[...] + rb_ref[...]
            r = ALPHA * r + scale * y[rows]
            mu = jnp.mean(r, axis=-1, keepdims=True)
            c = r - mu
            var = jnp.mean(c * c, axis=-1, keepdims=True)
            rstd = lax.rsqrt(var + LN_EPS)
            xhat = c * rstd
            h = xhat * g_ref[...] + b_ref[...]
            if not final:
                xh_ref[rows, :] = xhat
                hb_ref[rows, :] = h.astype(BF16)
                rstd_ref[rows, :] = rstd
            else:
                err = h - t_ref[rows, :]
                sq_ref[...] += jnp.sum(err * err, axis=0, keepdims=True)
                dh = err * (1.0 / d)
                dg_ref[...] += jnp.sum(dh * xhat, axis=0, keepdims=True)
                db_ref[...] += jnp.sum(dh, axis=0, keepdims=True)
                dr = _ln_backward(dh, xhat, rstd, g_ref[...])
                dr_ref[rows, :] = dr
                drb_ref[rows, :] = (scale * dr).astype(BF16)

    tok = pl.BlockSpec((tm, d), lambda i: (i, 0))
    vec = pl.BlockSpec((1, d), lambda i: (0, 0))
    ins = [a3, w3]
    in_specs = [pl.BlockSpec((nk, tm, tk), lambda i: (0, i, 0)), ANY]
    if bias is not None:
        ins.append(bias)
        in_specs.append(vec)
    ins.append(res)
    in_specs.append(tok)
    if res_affine is not None:
        ins += list(res_affine)
        in_specs += [vec, vec]
    ins += [ln_g, ln_b]
    in_specs += [vec, vec]
    if final:
        ins.append(target)
        in_specs.append(tok)
        out_shape = [jax.ShapeDtypeStruct((s, d), F32), jax.ShapeDtypeStruct((s, d), BF16)] + [jax.ShapeDtypeStruct((1, d), F32)] * 3
        out_specs = [tok, tok, vec, vec, vec]
    else:
        out_shape = [jax.ShapeDtypeStruct((s, d), F32), jax.ShapeDtypeStruct((s, d), BF16), jax.ShapeDtypeStruct((s, 1), F32)]
        out_specs = [tok, tok, pl.BlockSpec((tm, 1), lambda i: (i, 0))]
    scratch = [pltpu.VMEM((nk, tk, d), BF16), pltpu.SemaphoreType.DMA((1,))]
    return _call(body, name, (s // tm,), in_specs, out_specs, out_shape, scratch, ins, rider)


def _proj_in(hb, w, bias, name, rider=None):
    s, d = hb.shape
    n = w.shape[1]
    tm = min(s, 1024)
    tn = PROJ_IN_COLS if n % PROJ_IN_COLS == 0 else n

    def body(h_ref, w_ref, b_ref, z_ref):
        z_ref[...] = (jnp.dot(h_ref[...], w_ref[...], preferred_element_type=F32) + b_ref[...]).astype(BF16)

    in_specs = [pl.BlockSpec((tm, d), lambda i, j: (i, 0)), pl.BlockSpec((d, tn), lambda i, j: (0, j)),
                pl.BlockSpec((1, tn), lambda i, j: (0, j))]
    return _call(body, name, (s // tm, n // tn), in_specs, pl.BlockSpec((tm, tn), lambda i, j: (i, j)),
                 jax.ShapeDtypeStruct((s, n), BF16), [], [hb, w, bias], rider)


def _nt_hidden(ab, w3, name, rider=None):
    s, kdim = ab.shape
    nj, tn, _ = w3.shape
    tm = min(s, 1024)

    def body(a_ref, w_ref, o_ref):
        o_ref[...] = lax.dot_general(a_ref[...], w_ref[...], NT_DIMS, preferred_element_type=F32).astype(BF16)

    in_specs = [pl.BlockSpec((tm, kdim), lambda i, j: (i, 0)), pl.BlockSpec((None, tn, kdim), lambda i, j: (j, 0, 0))]
    return _call(body, name, (s // tm, nj), in_specs, pl.BlockSpec((None, tm, tn), lambda i, j: (j, i, 0)),
                 jax.ShapeDtypeStruct((nj, s, tn), BF16), [], [ab, w3], rider)


def _ffn_bwd_hidden(ab, w3, silu3, udsilu3, name, rider=None):
    s, kdim = ab.shape
    nj, tn, _ = w3.shape
    tm = min(s, 1024)

    def body(a_ref, w_ref, silu_ref, udsilu_ref, dg_ref, du_ref):
        a = a_ref[...]
        for c0 in range(0, tn, MXU_COLS):
            cols = slice(c0, min(c0 + MXU_COLS, tn))
            t = lax.dot_general(a, w_ref[cols, :], NT_DIMS, preferred_element_type=F32)
            du_ref[:, cols] = (t * silu_ref[:, cols].astype(F32)).astype(BF16)
            dg_ref[:, cols] = (t * udsilu_ref[:, cols].astype(F32)).astype(BF16)

    hid = pl.BlockSpec((None, tm, tn), lambda i, j: (j, i, 0))
    shp = jax.ShapeDtypeStruct((nj, s, tn), BF16)
    in_specs = [pl.BlockSpec((tm, kdim), lambda i, j: (i, 0)), pl.BlockSpec((None, tn, kdim), lambda i, j: (j, 0, 0)), hid, hid]
    return _call(body, name, (s // tm, nj), in_specs, [hid, hid], [shp, shp], [], [ab, w3, silu3, udsilu3], rider)


def _tn_dw(a, a_spec, b, b_spec, nj, m, n, s, tk, name, rider):
    def body(a_ref, b_ref, o_ref, acc_ref):
        k = pl.program_id(1)

        @pl.when(k == 0)
        def _():
            acc_ref[...] = jnp.zeros_like(acc_ref)

        acc_ref[...] += lax.dot_general(a_ref[...], b_ref[...], TN_DIMS, preferred_element_type=F32)

        @pl.when(k == s // tk - 1)
        def _():
            o_ref[...] = acc_ref[...].astype(BF16)

    return _call(body, name, (nj, s // tk), [a_spec, b_spec], pl.BlockSpec((None, m, n), lambda j, k: (j, 0, 0)),
                 jax.ShapeDtypeStruct((nj, m, n), BF16), [pltpu.VMEM((m, n), F32)], [a, b], rider)


def _dw_hidden_rows(hid3, db, name, rider=None):
    nj, s, fs = hid3.shape
    d = db.shape[1]
    tk = min(s, DW_TOKENS)
    return _tn_dw(hid3, pl.BlockSpec((None, tk, fs), lambda j, k: (j, k, 0)), db, pl.BlockSpec((tk, d), lambda j, k: (k, 0)),
                  nj, fs, d, s, tk, name, rider)


def _dw_hidden_rows_paired(hid3, db, name, rider=None):
    nj, s, fs = hid3.shape
    d = db.shape[1]
    tk = min(s, DW_TOKENS)
    nk = s // tk
    half = nj // 2

    def device_of(j):
        c = lax.axis_index("c")
        return 2 * (j % half) + jnp.where(j < half, 1 - c, c)

    def body(a_ref, b_ref, o_ref, theirs_ref, acc_ref, stage_ref, got_ref, send_sems, recv_sems, load_sem):
        j, k = pl.program_id(0), pl.program_id(1)
        x, y, c = lax.axis_index("x"), lax.axis_index("y"), lax.axis_index("c")

        def to_sibling(q):
            return pltpu.make_async_remote_copy(src_ref=stage_ref, dst_ref=theirs_ref.at[q], send_sem=send_sems.at[q],
                                                recv_sem=recv_sems.at[q], device_id=(x, y, 1 - c), device_id_type=MESH)

        def fetch(q):
            return pltpu.make_async_copy(theirs_ref.at[q], got_ref, load_sem.at[0])

        @pl.when(k == 0)
        def _():
            acc_ref[...] = jnp.zeros_like(acc_ref)

        acc_ref[...] += lax.dot_general(a_ref[...], b_ref[...], TN_DIMS, preferred_element_type=F32)

        for q in range(half):
            @pl.when((j == q) & (k == nk - 1))
            def _(q=q):
                if q > 0:
                    to_sibling(q - 1).wait_send()
                stage_ref[...] = acc_ref[...].astype(BF16)
                to_sibling(q).start()

            @pl.when((j == half + q) & (k == 0))
            def _(q=q):
                to_sibling(q).wait_recv()
                fetch(q).start()

            @pl.when((j == half + q) & (k == nk - 1))
            def _(q=q):
                if q == 0:
                    to_sibling(half - 1).wait_send()
                fetch(q).wait()
                o_ref[...] = (acc_ref[...] + got_ref[...].astype(F32)).astype(BF16)

    in_specs = [pl.BlockSpec((None, tk, fs), lambda j, k: (device_of(j), k, 0)), pl.BlockSpec((tk, d), lambda j, k: (k, 0))]
    out_specs = [pl.BlockSpec((None, fs, d), lambda j, k: (jnp.maximum(j - half, 0), 0, 0)), ANY]
    shp = jax.ShapeDtypeStruct((half, fs, d), BF16)
    scratch = [pltpu.VMEM((fs, d), F32), pltpu.VMEM((fs, d), BF16), pltpu.VMEM((fs, d), BF16),
               pltpu.SemaphoreType.DMA((half,)), pltpu.SemaphoreType.DMA((half,)), pltpu.SemaphoreType.DMA((1,))]
    (sums, _), riders_out = _call(body, name, (nj, nk), in_specs, out_specs, [shp, shp], scratch, [hid3, db], rider)
    return sums, riders_out


def _dw_cols(ab, dz, name, rider=None):
    s, d = ab.shape
    n = dz.shape[1]
    tn = PROJ_IN_COLS if n % PROJ_IN_COLS == 0 else n
    tk = min(s, DW_TOKENS)
    tr = d // 2

    def body(a_ref, b_ref, o_ref, acc_ref):
        k = pl.program_id(2)

        @pl.when(k == 0)
        def _():
            acc_ref[...] = jnp.zeros_like(acc_ref)

        acc_ref[...] += lax.dot_general(a_ref[...], b_ref[...], TN_DIMS, preferred_element_type=F32)

        @pl.when(k == s // tk - 1)
        def _():
            o_ref[...] = acc_ref[...].astype(BF16)

    in_specs = [pl.BlockSpec((tk, tr), lambda j, r, k: (k, r)), pl.BlockSpec((tk, tn), lambda j, r, k: (k, j))]
    return _call(body, name, (n // tn, d // tr, s // tk), in_specs, pl.BlockSpec((tr, tn), lambda j, r, k: (r, j)),
                 jax.ShapeDtypeStruct((d, n), BF16), [pltpu.VMEM((tr, tn), F32)], [ab, dz], rider)


def _ffn_bwd_input(dg3, wg3, du3, wu3, dres, ln, name, rider=None):
    s, d = dres.shape
    nk, _, fs = dg3.shape
    tm = min(s, 512)

    def body(*refs):
        refs = list(refs)
        dg_in, wg_ref, du_in, wu_ref, dres_ref = refs[:5]
        del refs[:5]
        if ln is not None:
            xh_ref, rstd_ref, gain_ref = refs.pop(0), refs.pop(0), refs.pop(0)
        acc_ref = refs.pop()
        i, k = pl.program_id(0), pl.program_id(1)

        @pl.when(k == 0)
        def _():
            acc_ref[...] = jnp.zeros_like(acc_ref)

        acc_ref[...] += (jnp.dot(dg_in[...], wg_ref[...], preferred_element_type=F32)
                         + jnp.dot(du_in[...], wu_ref[...], preferred_element_type=F32))

        @pl.when(k == nk - 1)
        def _():
            if ln is not None:
                dr_ref, drb_ref, dg_ref, db_ref, sum_ref = refs

                @pl.when(i == 0)
                def _():
                    dg_ref[...] = jnp.zeros_like(dg_ref)
                    db_ref[...] = jnp.zeros_like(db_ref)
                    sum_ref[...] = jnp.zeros_like(sum_ref)

            for rows in _row_blocks(tm):
                dh = ALPHA * dres_ref[rows, :] + acc_ref[rows, :]
                if ln is None:
                    refs[0][rows, :] = dh
                else:
                    xhat = xh_ref[rows, :]
                    dg_ref[...] += jnp.sum(dh * xhat, axis=0, keepdims=True)
                    db_ref[...] += jnp.sum(dh, axis=0, keepdims=True)
                    dr = _ln_backward(dh, xhat, rstd_ref[rows, :], gain_ref[...])
                    sum_ref[...] += jnp.sum(dr, axis=0, keepdims=True)
                    dr_ref[rows, :] = dr
                    drb_ref[rows, :] = dr.astype(BF16)

    tok = pl.BlockSpec((tm, d), lambda i, k: (i, 0))
    vec = pl.BlockSpec((1, d), lambda i, k: (0, 0))
    a_spec = pl.BlockSpec((None, tm, fs), lambda i, k: (k, i, 0))
    w_spec = pl.BlockSpec((None, fs, d), lambda i, k: (k, 0, 0))
    ins, in_specs = [dg3, wg3, du3, wu3, dres], [a_spec, w_spec, a_spec, w_spec, tok]
    if ln is None:
        out_shape, out_specs = jax.ShapeDtypeStruct((s, d), F32), tok
    else:
        ins += list(ln)
        in_specs += [tok, pl.BlockSpec((tm, 1), lambda i, k: (i, 0)), vec]
        out_shape = [jax.ShapeDtypeStruct((s, d), F32), jax.ShapeDtypeStruct((s, d), BF16)] + [jax.ShapeDtypeStruct((1, d), F32)] * 3
        out_specs = [tok, tok, vec, vec, vec]
    return _call(body, name, (s // tm, nk), in_specs, out_specs, out_shape, [pltpu.VMEM((tm, d), F32)], ins, rider)


def _proj_in_bwd_ln(dz, w, dres, ln, branch_scale, name, rider=None):
    s, d = dres.shape
    n = w.shape[1]
    tm = min(s, 256)

    def body(dz_ref, w_hbm, dres_ref, xh_ref, rstd_ref, gain_ref, dr_ref, drb_ref, dg_ref, db_ref, w_ref, w_sem):
        @pl.when(pl.program_id(0) == 0)
        def _():
            whole = pltpu.make_async_copy(w_hbm, w_ref, w_sem.at[0])
            whole.start()
            whole.wait()
            dg_ref[...] = jnp.zeros_like(dg_ref)
            db_ref[...] = jnp.zeros_like(db_ref)

        acc = lax.dot_general(dz_ref[...], w_ref[...], NT_DIMS, preferred_element_type=F32)
        for rows in _row_blocks(tm):
            dh = ALPHA * dres_ref[rows, :] + acc[rows]
            xhat = xh_ref[rows, :]
            dg_ref[...] += jnp.sum(dh * xhat, axis=0, keepdims=True)
            db_ref[...] += jnp.sum(dh, axis=0, keepdims=True)
            dr = _ln_backward(dh, xhat, rstd_ref[rows, :], gain_ref[...])
            dr_ref[rows, :] = dr
            drb_ref[rows, :] = (branch_scale * dr).astype(BF16)

    tok = pl.BlockSpec((tm, d), lambda i: (i, 0))
    vec = pl.BlockSpec((1, d), lambda i: (0, 0))
    in_specs = [pl.BlockSpec((tm, n), lambda i: (i, 0)), ANY, tok, tok, pl.BlockSpec((tm, 1), lambda i: (i, 0)), vec]
    out_shape = [jax.ShapeDtypeStruct((s, d), F32), jax.ShapeDtypeStruct((s, d), BF16)] + [jax.ShapeDtypeStruct((1, d), F32)] * 2
    scratch = [pltpu.VMEM(w.shape, BF16), pltpu.SemaphoreType.DMA((1,))]
    return _call(body, name, (s // tm,), in_specs, [tok, tok, vec, vec], out_shape, scratch, [dz, w, dres] + list(ln), rider)


def _shift_rows_down(v, halo, k, row):
    out = pltpu.roll(v, k, 0)
    hr = halo.shape[0]
    for r in range(k):
        out = jnp.where(row == r, halo[hr - k + r:hr - k + r + 1, :], out)
    return out


def _shift_rows_up(v, halo, k, row):
    t = v.shape[0]
    out = pltpu.roll(v, t - k, 0)
    for r in range(k):
        out = jnp.where(row == t - k + r, halo[r:r + 1, :], out)
    return out


def _sgu_head_forward(z_ref, h, da, gv_ref, bv_ref):
    zu = z_ref[:, h * HEAD:(h + 1) * HEAD].astype(F32)
    zv = z_ref[:, da + h * HEAD:da + (h + 1) * HEAD].astype(F32)
    tu, tv = _gelu_tanh(zu), _gelu_tanh(zv)
    u = _gelu(zu, tu)
    v = _gelu(zv, tv)
    mu = jnp.mean(v, axis=-1, keepdims=True)
    c = v - mu
    rstd = lax.rsqrt(jnp.mean(c * c, axis=-1, keepdims=True) + LN_EPS)
    vhat = c * rstd
    vln = (vhat * gv_ref[h:h + 1, :] + bv_ref[h:h + 1, :]).astype(BF16)
    return (zu, tu), (zv, tv), u, vhat, rstd, vln


def _mixer_fwd(z, ws_masked, bs_wide, gv, bv, cw, name):
    s, zc = z.shape
    da = zc // 5
    nh = da // HEAD
    tm = min(s, MIXER_TOKENS)
    hb = tm // BF16_ROWS

    def body(z_ref, pc_ref, px_ref, ws_ref, bs_ref, gv_ref, bv_ref, cw_ref, y_ref):
        i = pl.program_id(0)
        for h in range(nh):
            _, _, u, _, _, vln = _sgu_head_forward(z_ref, h, da, gv_ref, bv_ref)
            for n in range(tm // CHUNK):
                rows = slice(n * CHUNK, (n + 1) * CHUNK)
                mixed = jnp.dot(ws_ref[h], vln[rows], preferred_element_type=F32) + bs_ref[h]
                y_ref[0, rows, h * HEAD:(h + 1) * HEAD] = (u[rows] * mixed).astype(BF16)
        gate_b = z_ref[:, 2 * da:3 * da].astype(F32)
        hc = z_ref[:, 3 * da:4 * da].astype(F32) * z_ref[:, 4 * da:5 * da].astype(F32)
        halo = jnp.where(i > 0, pc_ref[...].astype(F32) * px_ref[...].astype(F32), 0.0)
        row = lax.broadcasted_iota(jnp.int32, (tm, da), 0)
        y = cw_ref[0:1, :] * _shift_rows_down(hc, halo, 2, row) + cw_ref[1:2, :] * _shift_rows_down(hc, halo, 1, row) + cw_ref[2:3, :] * hc
        y_ref[1] = (gate_b * y).astype(BF16)

    prev = lambda col: pl.BlockSpec((BF16_ROWS, da), lambda i: (jnp.maximum(i * hb - 1, 0), col))
    in_specs = [pl.BlockSpec((tm, zc), lambda i: (i, 0)), prev(3), prev(4), _row((nh, CHUNK, CHUNK)), _row((nh, CHUNK, HEAD)),
                _row((nh, HEAD)), _row((nh, HEAD)), _row((CONV_TAPS, da))]
    return _call(body, name, (s // tm,), in_specs, pl.BlockSpec((2, tm, da), lambda i: (0, i, 0)),
                 jax.ShapeDtypeStruct((2, s, da), BF16), [], [z, z, z, ws_masked, bs_wide, gv, bv, cw])[0]


def _mixer_bwd(z, dy, ws_masked, bs_wide, gv, bv, cw, name, rider=None):
    s, zc = z.shape
    da = zc // 5
    nh = da // HEAD
    tm = min(s, MIXER_TOKENS)
    hb = tm // BF16_ROWS
    nblk = s // tm

    def body(z_ref, pc_ref, px_ref, nb_ref, dy_ref, ndy_ref, ws_ref, bs_ref, gv_ref, bv_ref, cw_ref,
             dz_ref, dws_ref, dbs_ref, dgv_ref, dbv_ref, dcw_ref, dbin_ref):
        i = pl.program_id(0)

        @pl.when(i == 0)
        def _():
            for ref in (dws_ref, dbs_ref, dgv_ref, dbv_ref, dcw_ref, dbin_ref):
                ref[...] = jnp.zeros_like(ref)

        causal = lax.broadcasted_iota(jnp.int32, (CHUNK, CHUNK), 0) >= lax.broadcasted_iota(jnp.int32, (CHUNK, CHUNK), 1)
        for h in range(nh):
            zu, zv, u, vhat, rstd, vln = _sgu_head_forward(z_ref, h, da, gv_ref, bv_ref)
            dya = dy_ref[0, :, h * HEAD:(h + 1) * HEAD].astype(F32)
            w = ws_ref[h]
            du_parts, dvln_parts = [], []
            for n in range(tm // CHUNK):
                rows = slice(n * CHUNK, (n + 1) * CHUNK)
                mixed = jnp.dot(w, vln[rows], preferred_element_type=F32) + bs_ref[h]
                du_parts.append(dya[rows] * mixed)
                dmix = dya[rows] * u[rows]
                dmix_b = dmix.astype(BF16)
                dws_ref[h] += jnp.where(causal, lax.dot_general(dmix_b, vln[rows], NT_DIMS, preferred_element_type=F32), 0.0)
                dbs_ref[h] += dmix
                dvln_parts.append(lax.dot_general(w, dmix_b, TN_DIMS, preferred_element_type=F32))
            du = jnp.concatenate(du_parts, axis=0)
            dvln = jnp.concatenate(dvln_parts, axis=0)
            dgv_ref[h:h + 1, :] += jnp.sum(dvln * vhat, axis=0, keepdims=True)
            dbv_ref[h:h + 1, :] += jnp.sum(dvln, axis=0, keepdims=True)
            dv = _ln_backward(dvln, vhat, rstd, gv_ref[h:h + 1, :])
            dzu = du * _gelu_grad(*zu)
            dzv = dv * _gelu_grad(*zv)
            ucols = slice(h * HEAD, (h + 1) * HEAD)
            vcols = slice(da + h * HEAD, da + (h + 1) * HEAD)
            dz_ref[:, ucols] = dzu.astype(BF16)
            dz_ref[:, vcols] = dzv.astype(BF16)
            dbin_ref[:, ucols] += jnp.sum(dzu, axis=0, keepdims=True)
            dbin_ref[:, vcols] += jnp.sum(dzv, axis=0, keepdims=True)

        gate_b = z_ref[:, 2 * da:3 * da].astype(F32)
        gate_c = z_ref[:, 3 * da:4 * da].astype(F32)
        xt = z_ref[:, 4 * da:5 * da].astype(F32)
        hc = gate_c * xt
        halo = jnp.where(i > 0, pc_ref[...].astype(F32) * px_ref[...].astype(F32), 0.0)
        row = lax.broadcasted_iota(jnp.int32, (tm, da), 0)
        sh1 = _shift_rows_down(hc, halo, 1, row)
        sh2 = _shift_rows_down(hc, halo, 2, row)
        y = cw_ref[0:1, :] * sh2 + cw_ref[1:2, :] * sh1 + cw_ref[2:3, :] * hc
        dyb = dy_ref[1].astype(F32)
        dconv = dyb * gate_b
        nhalo = jnp.where(i < nblk - 1, ndy_ref[...].astype(F32) * nb_ref[...].astype(F32), 0.0)
        dhc = cw_ref[2:3, :] * dconv + cw_ref[1:2, :] * _shift_rows_up(dconv, nhalo, 1, row) + cw_ref[0:1, :] * _shift_rows_up(dconv, nhalo, 2, row)
        dcw_ref[0:1, :] += jnp.sum(dconv * sh2, axis=0, keepdims=True)
        dcw_ref[1:2, :] += jnp.sum(dconv * sh1, axis=0, keepdims=True)
        dcw_ref[2:3, :] += jnp.sum(dconv * hc, axis=0, keepdims=True)
        for col, val in ((2, dyb * y), (3, dhc * xt), (4, dhc * gate_c)):
            cols = slice(col * da, (col + 1) * da)
            dz_ref[:, cols] = val.astype(BF16)
            dbin_ref[:, cols] += jnp.sum(val, axis=0, keepdims=True)

        @pl.when(i == nblk - 1)
        def _():
            for h in range(nh):
                dbs_ref[h] = jnp.broadcast_to(jnp.sum(dbs_ref[h], axis=1, keepdims=True), (CHUNK, HEAD))

    prev = lambda col: pl.BlockSpec((BF16_ROWS, da), lambda i: (jnp.maximum(i * hb - 1, 0), col))
    nxt = lambda i: jnp.minimum((i + 1) * hb, s // BF16_ROWS - 1)
    in_specs = [pl.BlockSpec((tm, zc), lambda i: (i, 0)), prev(3), prev(4), pl.BlockSpec((BF16_ROWS, da), lambda i: (nxt(i), 2)),
                pl.BlockSpec((2, tm, da), lambda i: (0, i, 0)), pl.BlockSpec((None, BF16_ROWS, da), lambda i: (1, nxt(i), 0)),
                _row((nh, CHUNK, CHUNK)), _row((nh, CHUNK, HEAD)), _row((nh, HEAD)), _row((nh, HEAD)), _row((CONV_TAPS, da))]
    out_specs = [pl.BlockSpec((tm, zc), lambda i: (i, 0)), _row((nh, CHUNK, CHUNK)), _row((nh, CHUNK, HEAD)), _row((nh, HEAD)),
                 _row((nh, HEAD)), _row((8, da)), _row((1, zc))]
    out_shape = [jax.ShapeDtypeStruct((s, zc), BF16), jax.ShapeDtypeStruct((nh, CHUNK, CHUNK), F32),
                 jax.ShapeDtypeStruct((nh, CHUNK, HEAD), F32), jax.ShapeDtypeStruct((nh, HEAD), F32),
                 jax.ShapeDtypeStruct((nh, HEAD), F32), jax.ShapeDtypeStruct((8, da), F32), jax.ShapeDtypeStruct((1, zc), F32)]
    return _call(body, name, (nblk,), in_specs, out_specs, out_shape, [], [z, z, z, z, dy, dy, ws_masked, bs_wide, gv, bv, cw], rider)


def _adam_update(g, w, m, v):
    m_new = ADAM_B1 * m + (1.0 - ADAM_B1) * g
    v_new = ADAM_B2 * v + (1.0 - ADAM_B2) * (g * g)
    m_hat = m_new / (1.0 - ADAM_B1 ** ADAM_STEP)
    v_hat = v_new / (1.0 - ADAM_B2 ** ADAM_STEP)
    return -ADAM_LR * (m_hat / (jnp.sqrt(v_hat) + ADAM_EPS) + ADAM_WD * w), m_new, v_new


def _adamw(gparts, w, m, v, name):
    n, r, c = gparts.shape
    tr = ADAM_SIDE_ROWS if r % ADAM_SIDE_ROWS == 0 else r

    def body(g_ref, w_ref, m_ref, v_ref, go_ref, d_ref, mo_ref, vo_ref):
        g = g_ref[0].astype(F32)
        for q in range(1, n):
            g = g + g_ref[q].astype(F32)
        go_ref[...] = g
        d_ref[...], mo_ref[...], vo_ref[...] = _adam_update(g, w_ref[...], m_ref[...], v_ref[...])

    blk = pl.BlockSpec((tr, c), lambda i: (i, 0))
    shp = jax.ShapeDtypeStruct((r, c), F32)
    return _call(body, name, (r // tr,), [pl.BlockSpec((n, tr, c), lambda i: (0, i, 0)), blk, blk, blk], [blk] * 4, [shp] * 4, [],
                 [gparts, w, m, v])[0]


def _adamw_small(packs, rows, w, m, v, conv, loss_scale, name):
    n_par, n_dev = len(rows), packs.shape[0]
    taps = conv[0].shape[0]

    def body(*refs):
        refs = list(refs)
        cut = lambda n: [refs.pop(0) for _ in range(n)]
        p_ref, w_refs, m_refs, v_refs, (cw_ref, cm_ref, cv_ref) = refs.pop(0), cut(n_par), cut(n_par), cut(n_par), cut(3)
        outs = [cut(4) for _ in range(n_par + 1)]
        loss_ref = refs.pop(0)
        at = 0
        for k in range(n_par):
            g = p_ref[0, at:at + rows[k], :]
            for dev in range(1, n_dev):
                g = g + p_ref[dev, at:at + rows[k], :]
            go_ref, d_ref, mo_ref, vo_ref = outs[k]
            go_ref[...] = g
            d_ref[...], mo_ref[...], vo_ref[...] = _adam_update(g, w_refs[k][...], m_refs[k][...], v_refs[k][...])
            at += rows[k]
        me = 4 * lax.axis_index("x") + 2 * lax.axis_index("y") + lax.axis_index("c")
        go_ref, d_ref, mo_ref, vo_ref = outs[n_par]
        for tap in range(taps):
            row = pl.ds(at + tap * n_dev + me, 1)
            g = p_ref[0, row, :]
            for dev in range(1, n_dev):
                g = g + p_ref[dev, row, :]
            one = slice(tap, tap + 1)
            go_ref[one, :] = g
            d_ref[one, :], mo_ref[one, :], vo_ref[one, :] = _adam_update(g, cw_ref[one, :], cm_ref[one, :], cv_ref[one, :])
        at += taps * n_dev
        sq = p_ref[0, at:, :]
        for dev in range(1, n_dev):
            sq = sq + p_ref[dev, at:, :]
        total = jnp.sum(jnp.sum(sq, axis=0, keepdims=True), axis=1, keepdims=True)
        loss_ref[...] = jnp.broadcast_to(loss_scale * total, loss_ref.shape)

    vmem = pl.BlockSpec(memory_space=pltpu.VMEM)
    ins = [packs] + list(w) + list(m) + list(v) + list(conv)
    out_shape = [jax.ShapeDtypeStruct(a.shape, F32) for a in list(w) + [conv[0]] for _ in range(4)] + [jax.ShapeDtypeStruct((8, LANES), F32)]
    outs = pl.pallas_call(body, name=name, in_specs=[vmem] * len(ins), out_specs=[vmem] * len(out_shape), out_shape=out_shape,
                          compiler_params=pltpu.CompilerParams(vmem_limit_bytes=V7X_VMEM_LIMIT))(*ins)
    return [outs[4 * k:4 * k + 4] for k in range(n_par + 1)], outs[-1][0, 0]


def _rows128(a):
    return a.reshape(-1, LANES)


def kernel(x, ffa_gate, ffa_up, ffa_down, ln_a_g, ln_a_b, w_in, b_in, w_s, b_s, ln_v_g, ln_v_b, conv_w, w_out, b_out, ln_m_g, ln_m_b, ffc_gate, ffc_up, ffc_down, ln_c_g, ln_c_b, loss_target, m_ffa_gate, m_ffa_up, m_ffa_down, m_ln_a_g, m_ln_a_b, m_w_in, m_b_in, m_w_s, m_b_s, m_ln_v_g, m_ln_v_b, m_conv_w, m_w_out, m_b_out, m_ln_m_g, m_ln_m_b, m_ffc_gate, m_ffc_up, m_ffc_down, m_ln_c_g, m_ln_c_b, v_ffa_gate, v_ffa_up, v_ffa_down, v_ln_a_g, v_ln_a_b, v_w_in, v_b_in, v_w_s, v_b_s, v_ln_v_g, v_ln_v_b, v_conv_w, v_w_out, v_b_out, v_ln_m_g, v_ln_m_b, v_ffc_gate, v_ffc_up, v_ffc_down, v_ln_c_g, v_ln_c_b):
    x2, target = x[0], loss_target[0]
    s, d = x2.shape
    da = d // 2
    nh = da // HEAD

    big = dict(ffa_gate=ffa_gate, ffa_up=ffa_up, ffa_down=ffa_down, w_in=w_in, w_out=w_out, ffc_gate=ffc_gate, ffc_up=ffc_up, ffc_down=ffc_down)
    big_m = dict(ffa_gate=m_ffa_gate, ffa_up=m_ffa_up, ffa_down=m_ffa_down, w_in=m_w_in, w_out=m_w_out, ffc_gate=m_ffc_gate, ffc_up=m_ffc_up, ffc_down=m_ffc_down)
    big_v = dict(ffa_gate=v_ffa_gate, ffa_up=v_ffa_up, ffa_down=v_ffa_down, w_in=v_w_in, w_out=v_w_out, ffc_gate=v_ffc_gate, ffc_up=v_ffc_up, ffc_down=v_ffc_down)
    local = lambda k, a: jnp.transpose(a[0]) if k in TRANSPOSED else a[0]
    shard = {k: local(k, w).astype(BF16) for k, w in big.items()}
    conv_rows = jnp.pad(conv_w[0], ((0, 8 - CONV_TAPS), (0, 0)))

    tril = jnp.tril(jnp.ones((CHUNK, CHUNK), dtype=bool))
    ws_masked = jnp.where(tril[None], w_s[0], 0.0).astype(BF16)
    bs_wide = jnp.broadcast_to(b_s[0][:, :, None], (nh, CHUNK, HEAD))
    gv, bv = ln_v_g.reshape(nh, HEAD), ln_v_b.reshape(nh, HEAD)

    full = {}
    xb = x2.astype(BF16)
    (silu_a, udsilu_a, hid_a), (full["ffa_gate"], full["ffa_up"], full["ffa_down"]) = _ffn_gateup_gathering(
        xb, shard["ffa_gate"], shard["ffa_up"], shard["ffa_down"], "ffa_gateup")
    (xhat1, h1b, rstd1), (full["w_in"], full["w_out"], conv_full) = _down_ln(
        hid_a, full["ffa_down"], None, x2, None, ln_a_g, ln_a_b, 0.5, "ffa_down_ln", rider=_Gather([shard["w_in"], shard["w_out"], conv_rows], by_columns=(0,)))
    cw = jnp.transpose(conv_full[:, :CONV_TAPS, :], (1, 0, 2)).reshape(CONV_TAPS, da)
    w_out2 = full["w_out"].reshape(2, da, d)
    z, (full["ffc_gate"],) = _proj_in(h1b, full["w_in"], b_in, "proj_in", _Gather([shard["ffc_gate"]]))
    ycat = _mixer_fwd(z, ws_masked, bs_wide, gv, bv, cw, "mixer_fwd")
    (xhat2, h2b, rstd2), _ = _down_ln(ycat, w_out2, b_out, xhat1, (ln_a_g, ln_a_b), ln_m_g, ln_m_b, 1.0, "proj_out_ln")
    (silu_c, udsilu_c, hid_c), (full["ffc_up"], full["ffc_down"]) = _ffn_gateup_gathering(
        h2b, full["ffc_gate"], shard["ffc_up"], shard["ffc_down"], "ffc_gateup")
    (dr3, dr3b, sq_err, d_ln_c_g, d_ln_c_b), _ = _down_ln(hid_c, full["ffc_down"], None, xhat2, (ln_m_g, ln_m_b), ln_c_g, ln_c_b, 0.5,
                                                          "ffc_down_ln_loss", target=target)

    landed = {}
    (dg_c, du_c), _ = _ffn_bwd_hidden(dr3b, full["ffc_down"], silu_c, udsilu_c, "ffc_bwd_hidden")
    part, _ = _dw_hidden_rows_paired(hid_c, dr3b, "ffc_dw_down")
    part, (landed["ffc_down"],) = _dw_hidden_rows_paired(dg_c, h2b, "ffc_dw_gate", _ChipScatter(part))
    part, (landed["ffc_gate"],) = _dw_hidden_rows_paired(du_c, h2b, "ffc_dw_up", _ChipScatter(part))
    (dr2, dr2b, d_ln_m_g, d_ln_m_b, d_b_out), (landed["ffc_up"],) = _ffn_bwd_input(
        dg_c, full["ffc_gate"], du_c, full["ffc_up"], dr3, (xhat2, rstd2, ln_m_g), "ffc_bwd_input_ln", _ChipScatter(part))
    updated = {}
    adam_side = lambda k: _AdamSide(landed[k], local(k, big[k]), local(k, big_m[k]), local(k, big_v[k]))
    dycat, (updated["ffc_down"], updated["ffc_up"]) = _nt_hidden(dr2b, w_out2, "proj_out_bwd", [adam_side("ffc_down"), adam_side("ffc_up")])
    part, _ = _dw_hidden_rows(ycat, dr2b, "proj_out_dw")
    (dz, d_w_s, d_b_s_wide, d_gv, d_bv, d_cw, d_b_in), (landed["w_out"],) = _mixer_bwd(
        z, dycat, ws_masked, bs_wide, gv, bv, cw, "mixer_bwd", _Scatter(part.reshape(N_DEV, d // N_DEV, d)))
    (dr1, dr1b, d_ln_a_g, d_ln_a_b), updated["ffc_gate"] = _proj_in_bwd_ln(
        dz, full["w_in"], dr2, (xhat1, rstd1, ln_a_g), 0.5, "proj_in_bwd_ln", adam_side("ffc_gate"))
    small_g = dict(ln_a_g=d_ln_a_g, ln_a_b=d_ln_a_b, b_in=d_b_in, w_s=d_w_s, b_s=d_b_s_wide[:, :, 0], ln_v_g=d_gv, ln_v_b=d_bv, b_out=d_b_out,
                   ln_m_g=d_ln_m_g, ln_m_b=d_ln_m_b, ln_c_g=d_ln_c_g, ln_c_b=d_ln_c_b)
    pack = jnp.concatenate([_rows128(g) for g in small_g.values()] + [_rows128(d_cw[:CONV_TAPS]), _rows128(sq_err)], axis=0)
    part, (packs,) = _dw_cols(h1b, dz, "proj_in_dw", _Scatter(pack, whole=True))
    (dg_a, du_a), (landed["w_in"],) = _ffn_bwd_hidden(dr1b, full["ffa_down"], silu_a, udsilu_a, "ffa_bwd_hidden", _Scatter(part, by_columns=True))
    part, _ = _dw_hidden_rows_paired(hid_a, dr1b, "ffa_dw_down")
    part, (landed["ffa_down"],) = _dw_hidden_rows_paired(dg_a, xb, "ffa_dw_gate", _ChipScatter(part))
    part, (landed["ffa_gate"],) = _dw_hidden_rows_paired(du_a, xb, "ffa_dw_up", _ChipScatter(part))
    grad_x, (landed["ffa_up"],) = _ffn_bwd_input(dg_a, full["ffa_gate"], du_a, full["ffa_up"], dr1, None, "ffa_bwd_input", _ChipScatter(part))

    grads, deltas, new_m, new_v = {}, {}, {}, {}
    for k in big:
        out = updated[k] if k in updated else _adamw(landed[k], local(k, big[k]), local(k, big_m[k]), local(k, big_v[k]), "adamw_" + k)
        grads[k], deltas[k], new_m[k], new_v[k] = ((jnp.transpose(o) if k in TRANSPOSED else o).reshape(big[k].shape) for o in out)

    small = dict(ln_a_g=ln_a_g, ln_a_b=ln_a_b, b_in=b_in, w_s=w_s, b_s=b_s, ln_v_g=ln_v_g, ln_v_b=ln_v_b, b_out=b_out,
                 ln_m_g=ln_m_g, ln_m_b=ln_m_b, ln_c_g=ln_c_g, ln_c_b=ln_c_b)
    small_m = dict(ln_a_g=m_ln_a_g, ln_a_b=m_ln_a_b, b_in=m_b_in, w_s=m_w_s, b_s=m_b_s, ln_v_g=m_ln_v_g, ln_v_b=m_ln_v_b, b_out=m_b_out,
                   ln_m_g=m_ln_m_g, ln_m_b=m_ln_m_b, ln_c_g=m_ln_c_g, ln_c_b=m_ln_c_b)
    small_v = dict(ln_a_g=v_ln_a_g, ln_a_b=v_ln_a_b, b_in=v_b_in, w_s=v_w_s, b_s=v_b_s, ln_v_g=v_ln_v_g, ln_v_b=v_ln_v_b, b_out=v_b_out,
                   ln_m_g=v_ln_m_g, ln_m_b=v_ln_m_b, ln_c_g=v_ln_c_g, ln_c_b=v_ln_c_b)
    snames = list(small)
    assert snames == list(small_g) and conv_w.shape[2] == LANES and da == N_DEV * LANES
    views = lambda tree: [_rows128(tree[k]) for k in snames]
    out, loss = _adamw_small(packs, [a.shape[0] for a in views(small)], views(small), views(small_m), views(small_v),
                             (conv_w[0], m_conv_w[0], v_conv_w[0]), 0.5 / d, "adamw_small")
    for k, per_param in zip(snames + ["conv_w"], out):
        shape = conv_w.shape if k == "conv_w" else small[k].shape
        grads[k], deltas[k], new_m[k], new_v[k] = (o.reshape(shape) for o in per_param)

    order = ["ffa_gate", "ffa_up", "ffa_down", "ln_a_g", "ln_a_b", "w_in", "b_in", "w_s", "b_s", "ln_v_g", "ln_v_b", "conv_w", "w_out", "b_out",
             "ln_m_g", "ln_m_b", "ffc_gate", "ffc_up", "ffc_down", "ln_c_g", "ln_c_b"]
    return (loss, grad_x[None], *[grads[k] for k in order], *[deltas[k] for k in order], *[new_m[k] for k in order], *[new_v[k] for k in order])
```
